```python
import jax, jax.numpy as jnp
from jax import lax
import numpy as np

D_MODEL = 1024
BATCH = 8
SEQ = 8192
DEPTH = 1

CHUNK = 64
EPS = 1e-5
N_BRANCH = 2
GMLP_BLOCK = 128
GMLP_WIDTH = 1024
GMLP_GROUPS = 8
GMLP_GDIM = GMLP_WIDTH // GMLP_GROUPS
SSM_INNER = 2 * D_MODEL
SSM_HEAD_DIM = 64
SSM_HEADS = SSM_INNER // SSM_HEAD_DIM
SSM_GROUPS = 4
SSM_HPG = SSM_HEADS // SSM_GROUPS
SSM_STATE = 128
SSM_CONV = 4
SSM_CHUNK = CHUNK
SSM_XBC = SSM_INNER + 2 * SSM_GROUPS * SSM_STATE
D_FF = 2816
FFN_CONV = 3
IN_COLS = N_BRANCH * D_MODEL + 2 * GMLP_WIDTH + SSM_INNER + SSM_XBC + SSM_HEADS

kernel_name = "hybrid_gmlp_ssd_gated_merge_block"


def rmsnorm(x, w):
    xf = x.astype(jnp.float32)
    y = xf * lax.rsqrt(jnp.mean(xf * xf, axis=-1, keepdims=True) + EPS)
    return (y * w.astype(jnp.float32)).astype(x.dtype)


def causal_dwconv(x, w, b):
    K, C = w.shape
    y = lax.conv_general_dilated(
        x, w[:, None, :].astype(x.dtype), window_strides=(1,), padding=[(K - 1, 0)],
        dimension_numbers=('NWC', 'WIO', 'NWC'), feature_group_count=C)
    return y + b.astype(x.dtype)


def gmlp_mixer(za, ln_w, ln_b, w_s, b_s):
    Bsz, S, _ = za.shape
    z = jax.nn.gelu(za)
    u, v = jnp.split(z, 2, axis=-1)
    nb = S // GMLP_BLOCK
    v = v.reshape(Bsz, nb, GMLP_BLOCK, GMLP_GROUPS, GMLP_GDIM)
    vf = v.astype(jnp.float32)
    mu = jnp.mean(vf, axis=-1, keepdims=True)
    var = jnp.mean(jnp.square(vf - mu), axis=-1, keepdims=True)
    v = ((vf - mu) * lax.rsqrt(var + EPS) * ln_w.astype(jnp.float32)
         + ln_b.astype(jnp.float32)).astype(z.dtype)
    chunk_id = jnp.arange(GMLP_BLOCK) // CHUNK
    mask = chunk_id[None, :] <= chunk_id[:, None]
    ws = jnp.where(mask[None], w_s, 0).astype(v.dtype)
    sv = jnp.einsum('gts,bnsgd->bntgd', ws, v) + b_s.T.astype(v.dtype)[None, None, :, :, None]
    return u * sv.reshape(Bsz, S, GMLP_WIDTH)


def ssd_scan(x, dt, A, Bm, Cm):
    Bsz, S = x.shape[:2]
    L = SSM_CHUNK
    nc = S // L
    f32 = jnp.float32
    x = x.astype(f32).reshape(Bsz, nc, L, SSM_GROUPS, SSM_HPG, SSM_HEAD_DIM)
    dt = dt.reshape(Bsz, nc, L, SSM_GROUPS, SSM_HPG)
    Bm = Bm.astype(f32).reshape(Bsz, nc, L, SSM_GROUPS, SSM_STATE)
    Cm = Cm.astype(f32).reshape(Bsz, nc, L, SSM_GROUPS, SSM_STATE)
    a_cum = jnp.cumsum(dt * A.reshape(SSM_GROUPS, SSM_HPG), axis=2)
    xdt = x * dt[..., None]
    seg = a_cum[:, :, :, None] - a_cum[:, :, None, :]
    causal = jnp.tril(jnp.ones((L, L), dtype=bool))
    decay = jnp.exp(jnp.where(causal[:, :, None, None], seg, -jnp.inf))
    cb = jnp.einsum('bclgn,bcsgn->bclsg', Cm, Bm)
    y_diag = jnp.einsum('bclsg,bclsgh,bcsghp->bclghp', cb, decay, xdt)
    decay_end = jnp.exp(a_cum[:, :, -1:] - a_cum)
    states = jnp.einsum('bcsgn,bcsgh,bcsghp->bcghpn', Bm, decay_end, xdt)
    chunk_decay = jnp.exp(a_cum[:, :, -1])

    def step(h, inp):
        st, dec = inp
        return h * dec[..., None, None] + st, h

    h0 = jnp.zeros((Bsz, SSM_GROUPS, SSM_HPG, SSM_HEAD_DIM, SSM_STATE), f32)
    _, prev = lax.scan(step, h0, (jnp.moveaxis(states, 1, 0), jnp.moveaxis(chunk_decay, 1, 0)))
    prev = jnp.moveaxis(prev, 0, 1)
    y_off = jnp.einsum('bclgn,bcghpn,bclgh->bclghp', Cm, prev, jnp.exp(a_cum))
    return (y_diag + y_off).reshape(Bsz, S, SSM_HEADS, SSM_HEAD_DIM)


def mamba2_mixer(z, xbc, dt_raw, conv_w, conv_b, dt_bias, a_log, d_skip, norm_w):
    Bsz, S, _ = z.shape
    xbc = jax.nn.silu(causal_dwconv(xbc, conv_w, conv_b))
    xs, Bm, Cm = jnp.split(xbc, [SSM_INNER, SSM_INNER + SSM_GROUPS * SSM_STATE], axis=-1)
    xs = xs.reshape(Bsz, S, SSM_HEADS, SSM_HEAD_DIM)
    Bm = Bm.reshape(Bsz, S, SSM_GROUPS, SSM_STATE)
    Cm = Cm.reshape(Bsz, S, SSM_GROUPS, SSM_STATE)
    dt = jax.nn.softplus(dt_raw.astype(jnp.float32) + dt_bias.astype(jnp.float32))
    A = -jnp.exp(a_log.astype(jnp.float32))
    y = ssd_scan(xs, dt, A, Bm, Cm) + d_skip.astype(jnp.float32)[:, None] * xs.astype(jnp.float32)
    y = y.reshape(Bsz, S, SSM_INNER) * jax.nn.silu(z.astype(jnp.float32))
    yg = y.reshape(Bsz, S, SSM_GROUPS, SSM_INNER // SSM_GROUPS)
    yg = yg * lax.rsqrt(jnp.mean(yg * yg, axis=-1, keepdims=True) + EPS)
    y = yg.reshape(Bsz, S, SSM_INNER) * norm_w.astype(jnp.float32)
    return y.astype(z.dtype)


def _fwd_setup_inputs(seed: int = 0) -> dict:
    key = jax.random.key(seed)
    ks = jax.random.split(key, 24)

    def nrm(k, shape, scale):
        return jax.random.normal(k, shape, jnp.float32) * scale

    def gain(k, shape):
        return 1.0 + 0.02 * jax.random.normal(k, shape, jnp.float32)

    dt0 = jnp.exp(jax.random.uniform(ks[12], (DEPTH, SSM_HEADS), jnp.float32,
                                     np.log(1e-3), np.log(1e-1)))
    dt_bias = dt0 + jnp.log(-jnp.expm1(-dt0))
    a_log = jnp.log(jax.random.uniform(ks[13], (DEPTH, SSM_HEADS), jnp.float32, 1.0, 16.0))
    return {
        "x": nrm(ks[0], (BATCH, SEQ, D_MODEL), 1.0),
        "mix_norm_w": gain(ks[1], (DEPTH, D_MODEL)),
        "w_in": nrm(ks[2], (DEPTH, D_MODEL, IN_COLS), D_MODEL ** -0.5),
        "gate_bias": nrm(ks[3], (DEPTH, N_BRANCH, D_MODEL), 0.01),
        "gmlp_ln_w": gain(ks[4], (DEPTH, GMLP_GROUPS, GMLP_GDIM)),
        "gmlp_ln_b": nrm(ks[5], (DEPTH, GMLP_GROUPS, GMLP_GDIM), 0.01),
        "gmlp_ws": nrm(ks[6], (DEPTH, GMLP_GROUPS, GMLP_BLOCK, GMLP_BLOCK), 0.5 * GMLP_BLOCK ** -0.5),
        "gmlp_bs": gain(ks[7], (DEPTH, GMLP_GROUPS, GMLP_BLOCK)),
        "ssm_conv_w": nrm(ks[8], (DEPTH, SSM_CONV, SSM_XBC), SSM_CONV ** -0.5),
        "ssm_conv_b": nrm(ks[9], (DEPTH, SSM_XBC), 0.01),
        "ssm_dt_bias": dt_bias,
        "ssm_a_log": a_log,
        "ssm_d": gain(ks[10], (DEPTH, SSM_HEADS)),
        "ssm_norm_w": gain(ks[11], (DEPTH, SSM_INNER)),
        "w_proj_a": nrm(ks[14], (DEPTH, GMLP_WIDTH, D_MODEL), GMLP_WIDTH ** -0.5),
        "w_proj_b": nrm(ks[15], (DEPTH, SSM_INNER, D_MODEL), SSM_INNER ** -0.5),
        "w_out": nrm(ks[16], (DEPTH, D_MODEL, D_MODEL), D_MODEL ** -0.5),
        "ffn_norm_w": gain(ks[17], (DEPTH, D_MODEL)),
        "ffn_w_up": nrm(ks[18], (DEPTH, D_MODEL, 2 * D_FF), D_MODEL ** -0.5),
        "ffn_conv_w": nrm(ks[19], (DEPTH, FFN_CONV, 2 * D_FF), FFN_CONV ** -0.5),
        "ffn_conv_b": nrm(ks[20], (DEPTH, 2 * D_FF), 0.01),
        "ffn_w_down": nrm(ks[21], (DEPTH, D_FF, D_MODEL), D_FF ** -0.5),
        "final_norm_w": gain(ks[22], (D_MODEL,)),
    }


def _fwd_reference(x, mix_norm_w, w_in, gate_bias, gmlp_ln_w, gmlp_ln_b, gmlp_ws, gmlp_bs,
              ssm_conv_w, ssm_conv_b, ssm_dt_bias, ssm_a_log, ssm_d, ssm_norm_w,
              w_proj_a, w_proj_b, w_out, ffn_norm_w, ffn_w_up, ffn_conv_w, ffn_conv_b,
              ffn_w_down, final_norm_w):
    splits = [D_MODEL, 2 * D_MODEL, 2 * D_MODEL + 2 * GMLP_WIDTH,
              2 * D_MODEL + 2 * GMLP_WIDTH + SSM_INNER,
              2 * D_MODEL + 2 * GMLP_WIDTH + SSM_INNER + SSM_XBC]
    h = x
    for l in range(DEPTH):
        xn = rmsnorm(h, mix_norm_w[l])
        proj = xn @ w_in[l]
        g_a, g_b, za, z, xbc, dt_raw = jnp.split(proj, splits, axis=-1)
        y_a = gmlp_mixer(za, gmlp_ln_w[l], gmlp_ln_b[l], gmlp_ws[l], gmlp_bs[l]) @ w_proj_a[l]
        y_b = mamba2_mixer(z, xbc, dt_raw, ssm_conv_w[l], ssm_conv_b[l], ssm_dt_bias[l],
                           ssm_a_log[l], ssm_d[l], ssm_norm_w[l]) @ w_proj_b[l]
        merged = (jax.nn.sigmoid(g_a + gate_bias[l, 0]) * y_a
                  + jax.nn.sigmoid(g_b + gate_bias[l, 1]) * y_b)
        h = h + merged @ w_out[l]
        hn = rmsnorm(h, ffn_norm_w[l])
        up = causal_dwconv(hn @ ffn_w_up[l], ffn_conv_w[l], ffn_conv_b[l])
        gate, val = jnp.split(up, 2, axis=-1)
        h = h + (jax.nn.silu(gate) * val) @ ffn_w_down[l]
    return rmsnorm(h, final_norm_w)


import jax as _jax
import jax.numpy as _jnp

TWIN_FORMAT = 'train_step'
FWD_PARAMS = ['x', 'mix_norm_w', 'w_in', 'gate_bias', 'gmlp_ln_w', 'gmlp_ln_b', 'gmlp_ws', 'gmlp_bs', 'ssm_conv_w', 'ssm_conv_b', 'ssm_dt_bias', 'ssm_a_log', 'ssm_d', 'ssm_norm_w', 'w_proj_a', 'w_proj_b', 'w_out', 'ffn_norm_w', 'ffn_w_up', 'ffn_conv_w', 'ffn_conv_b', 'ffn_w_down', 'final_norm_w']
TWIN_WEIGHTS = ['mix_norm_w', 'w_in', 'gate_bias', 'gmlp_ln_w', 'gmlp_ln_b', 'gmlp_ws', 'gmlp_bs', 'ssm_conv_w', 'ssm_conv_b', 'ssm_dt_bias', 'ssm_a_log', 'ssm_d', 'ssm_norm_w', 'w_proj_a', 'w_proj_b', 'w_out', 'ffn_norm_w', 'ffn_w_up', 'ffn_conv_w', 'ffn_conv_b', 'ffn_w_down', 'final_norm_w']
TWIN_DIFF_INPUT = 'x'
TWIN_INPUTS = ['x', 'mix_norm_w', 'w_in', 'gate_bias', 'gmlp_ln_w', 'gmlp_ln_b', 'gmlp_ws', 'gmlp_bs', 'ssm_conv_w', 'ssm_conv_b', 'ssm_dt_bias', 'ssm_a_log', 'ssm_d', 'ssm_norm_w', 'w_proj_a', 'w_proj_b', 'w_out', 'ffn_norm_w', 'ffn_w_up', 'ffn_conv_w', 'ffn_conv_b', 'ffn_w_down', 'final_norm_w', 'loss_target', 'm_mix_norm_w', 'm_w_in', 'm_gate_bias', 'm_gmlp_ln_w', 'm_gmlp_ln_b', 'm_gmlp_ws', 'm_gmlp_bs', 'm_ssm_conv_w', 'm_ssm_conv_b', 'm_ssm_dt_bias', 'm_ssm_a_log', 'm_ssm_d', 'm_ssm_norm_w', 'm_w_proj_a', 'm_w_proj_b', 'm_w_out', 'm_ffn_norm_w', 'm_ffn_w_up', 'm_ffn_conv_w', 'm_ffn_conv_b', 'm_ffn_w_down', 'm_final_norm_w', 'v_mix_norm_w', 'v_w_in', 'v_gate_bias', 'v_gmlp_ln_w', 'v_gmlp_ln_b', 'v_gmlp_ws', 'v_gmlp_bs', 'v_ssm_conv_w', 'v_ssm_conv_b', 'v_ssm_dt_bias', 'v_ssm_a_log', 'v_ssm_d', 'v_ssm_norm_w', 'v_w_proj_a', 'v_w_proj_b', 'v_w_out', 'v_ffn_norm_w', 'v_ffn_w_up', 'v_ffn_conv_w', 'v_ffn_conv_b', 'v_ffn_w_down', 'v_final_norm_w']
TWIN_OUTPUTS = ['loss', 'grad_x', 'grad_mix_norm_w', 'grad_w_in', 'grad_gate_bias', 'grad_gmlp_ln_w', 'grad_gmlp_ln_b', 'grad_gmlp_ws', 'grad_gmlp_bs', 'grad_ssm_conv_w', 'grad_ssm_conv_b', 'grad_ssm_dt_bias', 'grad_ssm_a_log', 'grad_ssm_d', 'grad_ssm_norm_w', 'grad_w_proj_a', 'grad_w_proj_b', 'grad_w_out', 'grad_ffn_norm_w', 'grad_ffn_w_up', 'grad_ffn_conv_w', 'grad_ffn_conv_b', 'grad_ffn_w_down', 'grad_final_norm_w', 'delta_mix_norm_w', 'delta_w_in', 'delta_gate_bias', 'delta_gmlp_ln_w', 'delta_gmlp_ln_b', 'delta_gmlp_ws', 'delta_gmlp_bs', 'delta_ssm_conv_w', 'delta_ssm_conv_b', 'delta_ssm_dt_bias', 'delta_ssm_a_log', 'delta_ssm_d', 'delta_ssm_norm_w', 'delta_w_proj_a', 'delta_w_proj_b', 'delta_w_out', 'delta_ffn_norm_w', 'delta_ffn_w_up', 'delta_ffn_conv_w', 'delta_ffn_conv_b', 'delta_ffn_w_down', 'delta_final_norm_w', 'new_m_mix_norm_w', 'new_m_w_in', 'new_m_gate_bias', 'new_m_gmlp_ln_w', 'new_m_gmlp_ln_b', 'new_m_gmlp_ws', 'new_m_gmlp_bs', 'new_m_ssm_conv_w', 'new_m_ssm_conv_b', 'new_m_ssm_dt_bias', 'new_m_ssm_a_log', 'new_m_ssm_d', 'new_m_ssm_norm_w', 'new_m_w_proj_a', 'new_m_w_proj_b', 'new_m_w_out', 'new_m_ffn_norm_w', 'new_m_ffn_w_up', 'new_m_ffn_conv_w', 'new_m_ffn_conv_b', 'new_m_ffn_w_down', 'new_m_final_norm_w', 'new_v_mix_norm_w', 'new_v_w_in', 'new_v_gate_bias', 'new_v_gmlp_ln_w', 'new_v_gmlp_ln_b', 'new_v_gmlp_ws', 'new_v_gmlp_bs', 'new_v_ssm_conv_w', 'new_v_ssm_conv_b', 'new_v_ssm_dt_bias', 'new_v_ssm_a_log', 'new_v_ssm_d', 'new_v_ssm_norm_w', 'new_v_w_proj_a', 'new_v_w_proj_b', 'new_v_w_out', 'new_v_ffn_norm_w', 'new_v_ffn_w_up', 'new_v_ffn_conv_w', 'new_v_ffn_conv_b', 'new_v_ffn_w_down', 'new_v_final_norm_w']
TWIN_LEAF_KINDS = {'loss': 'loss', 'grad_x': 'grad_x', 'grad_mix_norm_w': 'grad_w', 'grad_w_in': 'grad_w', 'grad_gate_bias': 'grad_w', 'grad_gmlp_ln_w': 'grad_w', 'grad_gmlp_ln_b': 'grad_w', 'grad_gmlp_ws': 'grad_w', 'grad_gmlp_bs': 'grad_w', 'grad_ssm_conv_w': 'grad_w', 'grad_ssm_conv_b': 'grad_w', 'grad_ssm_dt_bias': 'grad_w', 'grad_ssm_a_log': 'grad_w', 'grad_ssm_d': 'grad_w', 'grad_ssm_norm_w': 'grad_w', 'grad_w_proj_a': 'grad_w', 'grad_w_proj_b': 'grad_w', 'grad_w_out': 'grad_w', 'grad_ffn_norm_w': 'grad_w', 'grad_ffn_w_up': 'grad_w', 'grad_ffn_conv_w': 'grad_w', 'grad_ffn_conv_b': 'grad_w', 'grad_ffn_w_down': 'grad_w', 'grad_final_norm_w': 'grad_w', 'delta_mix_norm_w': 'delta_w', 'delta_w_in': 'delta_w', 'delta_gate_bias': 'delta_w', 'delta_gmlp_ln_w': 'delta_w', 'delta_gmlp_ln_b': 'delta_w', 'delta_gmlp_ws': 'delta_w', 'delta_gmlp_bs': 'delta_w', 'delta_ssm_conv_w': 'delta_w', 'delta_ssm_conv_b': 'delta_w', 'delta_ssm_dt_bias': 'delta_w', 'delta_ssm_a_log': 'delta_w', 'delta_ssm_d': 'delta_w', 'delta_ssm_norm_w': 'delta_w', 'delta_w_proj_a': 'delta_w', 'delta_w_proj_b': 'delta_w', 'delta_w_out': 'delta_w', 'delta_ffn_norm_w': 'delta_w', 'delta_ffn_w_up': 'delta_w', 'delta_ffn_conv_w': 'delta_w', 'delta_ffn_conv_b': 'delta_w', 'delta_ffn_w_down': 'delta_w', 'delta_final_norm_w': 'delta_w', 'new_m_mix_norm_w': 'new_m', 'new_m_w_in': 'new_m', 'new_m_gate_bias': 'new_m', 'new_m_gmlp_ln_w': 'new_m', 'new_m_gmlp_ln_b': 'new_m', 'new_m_gmlp_ws': 'new_m', 'new_m_gmlp_bs': 'new_m', 'new_m_ssm_conv_w': 'new_m', 'new_m_ssm_conv_b': 'new_m', 'new_m_ssm_dt_bias': 'new_m', 'new_m_ssm_a_log': 'new_m', 'new_m_ssm_d': 'new_m', 'new_m_ssm_norm_w': 'new_m', 'new_m_w_proj_a': 'new_m', 'new_m_w_proj_b': 'new_m', 'new_m_w_out': 'new_m', 'new_m_ffn_norm_w': 'new_m', 'new_m_ffn_w_up': 'new_m', 'new_m_ffn_conv_w': 'new_m', 'new_m_ffn_conv_b': 'new_m', 'new_m_ffn_w_down': 'new_m', 'new_m_final_norm_w': 'new_m', 'new_v_mix_norm_w': 'new_v', 'new_v_w_in': 'new_v', 'new_v_gate_bias': 'new_v', 'new_v_gmlp_ln_w': 'new_v', 'new_v_gmlp_ln_b': 'new_v', 'new_v_gmlp_ws': 'new_v', 'new_v_gmlp_bs': 'new_v', 'new_v_ssm_conv_w': 'new_v', 'new_v_ssm_conv_b': 'new_v', 'new_v_ssm_dt_bias': 'new_v', 'new_v_ssm_a_log': 'new_v', 'new_v_ssm_d': 'new_v', 'new_v_ssm_norm_w': 'new_v', 'new_v_w_proj_a': 'new_v', 'new_v_w_proj_b': 'new_v', 'new_v_w_out': 'new_v', 'new_v_ffn_norm_w': 'new_v', 'new_v_ffn_w_up': 'new_v', 'new_v_ffn_conv_w': 'new_v', 'new_v_ffn_conv_b': 'new_v', 'new_v_ffn_w_down': 'new_v', 'new_v_final_norm_w': 'new_v'}


def _forward(args):
    return _fwd_reference(*[args[k] for k in FWD_PARAMS])


def _output_shape():
    out = _jax.eval_shape(lambda: _forward(_fwd_setup_inputs(0)))
    return out.shape, out.dtype

N_MICROBATCH = 1
ADAM_LR = 0.001
ADAM_B1 = 0.9
ADAM_B2 = 0.999
ADAM_EPS = 1e-08
ADAM_WD = 0.01
ADAM_STEP = 10
PER_EXAMPLE_BATCH_AXIS = {'x': 0, 'loss_target': 0}
SHARED_INPUTS = []
_WEIGHT_DTYPES = {'mix_norm_w': _jnp.float32, 'w_in': _jnp.float32, 'gate_bias': _jnp.float32, 'gmlp_ln_w': _jnp.float32, 'gmlp_ln_b': _jnp.float32, 'gmlp_ws': _jnp.float32, 'gmlp_bs': _jnp.float32, 'ssm_conv_w': _jnp.float32, 'ssm_conv_b': _jnp.float32, 'ssm_dt_bias': _jnp.float32, 'ssm_a_log': _jnp.float32, 'ssm_d': _jnp.float32, 'ssm_norm_w': _jnp.float32, 'w_proj_a': _jnp.float32, 'w_proj_b': _jnp.float32, 'w_out': _jnp.float32, 'ffn_norm_w': _jnp.float32, 'ffn_w_up': _jnp.float32, 'ffn_conv_w': _jnp.float32, 'ffn_conv_b': _jnp.float32, 'ffn_w_down': _jnp.float32, 'final_norm_w': _jnp.float32}
MOMENT_SCALE = {'mix_norm_w': 2.274626e-01, 'w_in': 7.342061e-02, 'gate_bias': 4.235342e-02, 'gmlp_ln_w': 3.581734e-02, 'gmlp_ln_b': 3.538968e-02, 'gmlp_ws': 7.051924e-02, 'gmlp_bs': 8.013338e-02, 'ssm_conv_w': 7.745309e-02, 'ssm_conv_b': 1.044095e-01, 'ssm_dt_bias': 1.538820e-01, 'ssm_a_log': 2.940643e-01, 'ssm_d': 5.671416e-01, 'ssm_norm_w': 8.775555e-02, 'w_proj_a': 8.907392e-02, 'w_proj_b': 1.249835e-01, 'w_out': 1.534548e-01, 'ffn_norm_w': 1.656179e-01, 'ffn_w_up': 6.987629e-02, 'ffn_conv_w': 6.913958e-02, 'ffn_conv_b': 6.853275e-02, 'ffn_w_down': 1.148004e-01, 'final_norm_w': 6.395382e+01}


def _to_microbatches(a, axis):
    t = _jnp.moveaxis(a, axis, 0)
    t = t.reshape((N_MICROBATCH, t.shape[0] // N_MICROBATCH) + t.shape[1:])
    return _jnp.moveaxis(t, 1, axis + 1)


def setup_inputs(seed: int = 0) -> dict:
    inp = _fwd_setup_inputs(seed)
    key = _jax.random.fold_in(_jax.random.key(seed), 7919)
    shape, _ = _output_shape()
    out = dict(inp)
    out["loss_target"] = _jax.random.normal(_jax.random.fold_in(key, 0), shape, _jnp.float32)
    for i, name in enumerate(TWIN_WEIGHTS):
        w = inp[name].astype(_jnp.float32)
        if MOMENT_SCALE is None:
            s = _jnp.sqrt(_jnp.mean(_jnp.square(w)) + 1e-30)
        else:
            s = MOMENT_SCALE[name]
        km, kv = _jax.random.split(_jax.random.fold_in(key, i + 1))
        out[name] = w
        out["m_" + name] = s * _jax.random.normal(km, w.shape, _jnp.float32)
        out["v_" + name] = (s * s) * _jax.random.uniform(kv, w.shape, _jnp.float32, 0.5, 1.5)
    if N_MICROBATCH > 1:
        for name, axis in PER_EXAMPLE_BATCH_AXIS.items():
            out[name] = _to_microbatches(out[name], axis)
    return {'x': out['x'], 'mix_norm_w': out['mix_norm_w'], 'w_in': out['w_in'], 'gate_bias': out['gate_bias'], 'gmlp_ln_w': out['gmlp_ln_w'], 'gmlp_ln_b': out['gmlp_ln_b'], 'gmlp_ws': out['gmlp_ws'], 'gmlp_bs': out['gmlp_bs'], 'ssm_conv_w': out['ssm_conv_w'], 'ssm_conv_b': out['ssm_conv_b'], 'ssm_dt_bias': out['ssm_dt_bias'], 'ssm_a_log': out['ssm_a_log'], 'ssm_d': out['ssm_d'], 'ssm_norm_w': out['ssm_norm_w'], 'w_proj_a': out['w_proj_a'], 'w_proj_b': out['w_proj_b'], 'w_out': out['w_out'], 'ffn_norm_w': out['ffn_norm_w'], 'ffn_w_up': out['ffn_w_up'], 'ffn_conv_w': out['ffn_conv_w'], 'ffn_conv_b': out['ffn_conv_b'], 'ffn_w_down': out['ffn_w_down'], 'final_norm_w': out['final_norm_w'], 'loss_target': out['loss_target'], 'm_mix_norm_w': out['m_mix_norm_w'], 'm_w_in': out['m_w_in'], 'm_gate_bias': out['m_gate_bias'], 'm_gmlp_ln_w': out['m_gmlp_ln_w'], 'm_gmlp_ln_b': out['m_gmlp_ln_b'], 'm_gmlp_ws': out['m_gmlp_ws'], 'm_gmlp_bs': out['m_gmlp_bs'], 'm_ssm_conv_w': out['m_ssm_conv_w'], 'm_ssm_conv_b': out['m_ssm_conv_b'], 'm_ssm_dt_bias': out['m_ssm_dt_bias'], 'm_ssm_a_log': out['m_ssm_a_log'], 'm_ssm_d': out['m_ssm_d'], 'm_ssm_norm_w': out['m_ssm_norm_w'], 'm_w_proj_a': out['m_w_proj_a'], 'm_w_proj_b': out['m_w_proj_b'], 'm_w_out': out['m_w_out'], 'm_ffn_norm_w': out['m_ffn_norm_w'], 'm_ffn_w_up': out['m_ffn_w_up'], 'm_ffn_conv_w': out['m_ffn_conv_w'], 'm_ffn_conv_b': out['m_ffn_conv_b'], 'm_ffn_w_down': out['m_ffn_w_down'], 'm_final_norm_w': out['m_final_norm_w'], 'v_mix_norm_w': out['v_mix_norm_w'], 'v_w_in': out['v_w_in'], 'v_gate_bias': out['v_gate_bias'], 'v_gmlp_ln_w': out['v_gmlp_ln_w'], 'v_gmlp_ln_b': out['v_gmlp_ln_b'], 'v_gmlp_ws': out['v_gmlp_ws'], 'v_gmlp_bs': out['v_gmlp_bs'], 'v_ssm_conv_w': out['v_ssm_conv_w'], 'v_ssm_conv_b': out['v_ssm_conv_b'], 'v_ssm_dt_bias': out['v_ssm_dt_bias'], 'v_ssm_a_log': out['v_ssm_a_log'], 'v_ssm_d': out['v_ssm_d'], 'v_ssm_norm_w': out['v_ssm_norm_w'], 'v_w_proj_a': out['v_w_proj_a'], 'v_w_proj_b': out['v_w_proj_b'], 'v_w_out': out['v_w_out'], 'v_ffn_norm_w': out['v_ffn_norm_w'], 'v_ffn_w_up': out['v_ffn_w_up'], 'v_ffn_conv_w': out['v_ffn_conv_w'], 'v_ffn_conv_b': out['v_ffn_conv_b'], 'v_ffn_w_down': out['v_ffn_w_down'], 'v_final_norm_w': out['v_final_norm_w']}


def _loss(weights, diff, rest, loss_target):
    with _jax.named_scope("forward"):
        args = {**rest, TWIN_DIFF_INPUT: diff, **{k: w.astype(_WEIGHT_DTYPES[k]) for k, w in weights.items()}}
        y = _forward(args)
    with _jax.named_scope("loss_head"):
        err = _jnp.square(y.astype(_jnp.float32) - loss_target)
        return 0.5 * _jnp.sum(_jnp.mean(err, axis=-1)) if err.ndim else 0.5 * err


def _adamw(w, g, m, v):
    m = ADAM_B1 * m + (1.0 - ADAM_B1) * g
    v = ADAM_B2 * v + (1.0 - ADAM_B2) * _jnp.square(g)
    m_hat = m / (1.0 - ADAM_B1 ** ADAM_STEP)
    v_hat = v / (1.0 - ADAM_B2 ** ADAM_STEP)
    delta = -ADAM_LR * (m_hat / (_jnp.sqrt(v_hat) + ADAM_EPS) + ADAM_WD * w)
    return delta, m, v


def reference(x, mix_norm_w, w_in, gate_bias, gmlp_ln_w, gmlp_ln_b, gmlp_ws, gmlp_bs, ssm_conv_w, ssm_conv_b, ssm_dt_bias, ssm_a_log, ssm_d, ssm_norm_w, w_proj_a, w_proj_b, w_out, ffn_norm_w, ffn_w_up, ffn_conv_w, ffn_conv_b, ffn_w_down, final_norm_w, loss_target, m_mix_norm_w, m_w_in, m_gate_bias, m_gmlp_ln_w, m_gmlp_ln_b, m_gmlp_ws, m_gmlp_bs, m_ssm_conv_w, m_ssm_conv_b, m_ssm_dt_bias, m_ssm_a_log, m_ssm_d, m_ssm_norm_w, m_w_proj_a, m_w_proj_b, m_w_out, m_ffn_norm_w, m_ffn_w_up, m_ffn_conv_w, m_ffn_conv_b, m_ffn_w_down, m_final_norm_w, v_mix_norm_w, v_w_in, v_gate_bias, v_gmlp_ln_w, v_gmlp_ln_b, v_gmlp_ws, v_gmlp_bs, v_ssm_conv_w, v_ssm_conv_b, v_ssm_dt_bias, v_ssm_a_log, v_ssm_d, v_ssm_norm_w, v_w_proj_a, v_w_proj_b, v_w_out, v_ffn_norm_w, v_ffn_w_up, v_ffn_conv_w, v_ffn_conv_b, v_ffn_w_down, v_final_norm_w):
    given = dict(x=x, mix_norm_w=mix_norm_w, w_in=w_in, gate_bias=gate_bias, gmlp_ln_w=gmlp_ln_w, gmlp_ln_b=gmlp_ln_b, gmlp_ws=gmlp_ws, gmlp_bs=gmlp_bs, ssm_conv_w=ssm_conv_w, ssm_conv_b=ssm_conv_b, ssm_dt_bias=ssm_dt_bias, ssm_a_log=ssm_a_log, ssm_d=ssm_d, ssm_norm_w=ssm_norm_w, w_proj_a=w_proj_a, w_proj_b=w_proj_b, w_out=w_out, ffn_norm_w=ffn_norm_w, ffn_w_up=ffn_w_up, ffn_conv_w=ffn_conv_w, ffn_conv_b=ffn_conv_b, ffn_w_down=ffn_w_down, final_norm_w=final_norm_w, loss_target=loss_target, m_mix_norm_w=m_mix_norm_w, m_w_in=m_w_in, m_gate_bias=m_gate_bias, m_gmlp_ln_w=m_gmlp_ln_w, m_gmlp_ln_b=m_gmlp_ln_b, m_gmlp_ws=m_gmlp_ws, m_gmlp_bs=m_gmlp_bs, m_ssm_conv_w=m_ssm_conv_w, m_ssm_conv_b=m_ssm_conv_b, m_ssm_dt_bias=m_ssm_dt_bias, m_ssm_a_log=m_ssm_a_log, m_ssm_d=m_ssm_d, m_ssm_norm_w=m_ssm_norm_w, m_w_proj_a=m_w_proj_a, m_w_proj_b=m_w_proj_b, m_w_out=m_w_out, m_ffn_norm_w=m_ffn_norm_w, m_ffn_w_up=m_ffn_w_up, m_ffn_conv_w=m_ffn_conv_w, m_ffn_conv_b=m_ffn_conv_b, m_ffn_w_down=m_ffn_w_down, m_final_norm_w=m_final_norm_w, v_mix_norm_w=v_mix_norm_w, v_w_in=v_w_in, v_gate_bias=v_gate_bias, v_gmlp_ln_w=v_gmlp_ln_w, v_gmlp_ln_b=v_gmlp_ln_b, v_gmlp_ws=v_gmlp_ws, v_gmlp_bs=v_gmlp_bs, v_ssm_conv_w=v_ssm_conv_w, v_ssm_conv_b=v_ssm_conv_b, v_ssm_dt_bias=v_ssm_dt_bias, v_ssm_a_log=v_ssm_a_log, v_ssm_d=v_ssm_d, v_ssm_norm_w=v_ssm_norm_w, v_w_proj_a=v_w_proj_a, v_w_proj_b=v_w_proj_b, v_w_out=v_w_out, v_ffn_norm_w=v_ffn_norm_w, v_ffn_w_up=v_ffn_w_up, v_ffn_conv_w=v_ffn_conv_w, v_ffn_conv_b=v_ffn_conv_b, v_ffn_w_down=v_ffn_w_down, v_final_norm_w=v_final_norm_w)
    weights = {n: given[n] for n in TWIN_WEIGHTS}
    shared = {n: given[n] for n in SHARED_INPUTS}
    per_example = {n: given[n] for n in ['x']}
    grad_fn = _jax.value_and_grad(_loss, argnums=(0, 1))

    def one_microbatch(ex, loss_target):
        ex = dict(ex)
        diff = ex.pop(TWIN_DIFF_INPUT)
        return grad_fn(weights, diff, {**shared, **ex}, loss_target)

    if N_MICROBATCH == 1:
        loss, (grad_w, grad_x) = one_microbatch(per_example, given["loss_target"])
    else:
        def body(carry, xs):
            loss_sum, grad_sum = carry
            l_k, (gw_k, gx_k) = one_microbatch(xs[0], xs[1])
            with _jax.named_scope("update"):
                return (loss_sum + l_k, _jax.tree.map(_jnp.add, grad_sum, gw_k)), gx_k

        init = (_jnp.zeros((), _jnp.float32), _jax.tree.map(_jnp.zeros_like, weights))
        (loss, grad_w), grad_x = _jax.lax.scan(body, init, (per_example, given["loss_target"]))
    with _jax.named_scope("update"):
        delta_w, new_m, new_v = {}, {}, {}
        for n in TWIN_WEIGHTS:
            delta_w[n], new_m[n], new_v[n] = _adamw(weights[n], grad_w[n], given["m_" + n], given["v_" + n])
    return (loss, grad_x, *[grad_w[n] for n in TWIN_WEIGHTS], *[delta_w[n] for n in TWIN_WEIGHTS],
            *[new_m[n] for n in TWIN_WEIGHTS], *[new_v[n] for n in TWIN_WEIGHTS])
```

```python
import functools

import jax
import jax.numpy as jnp
from jax import lax
from jax.experimental import pallas as pl
from jax.experimental.pallas import tpu as pltpu

F32 = jnp.float32
BF16 = jnp.bfloat16

D = 1024
EPS = 1e-5
GW = 1024
GB = 128
GG = 8
GD = 128
GCH = 64
SI = 2048
SH = 32
SP = 64
SG = 4
SN = 128
SGW = SI // SG
SK = 4
SXBC = SI + 2 * SG * SN
DFF = 2816
FK = 3
PMAIN = 2 * D + 2 * GW + SI + SXBC
IN_COLS = PMAIN + SH
NDEV = 8
ADAM_LR, ADAM_B1, ADAM_B2, ADAM_EPS, ADAM_WD, ADAM_STEP = 0.001, 0.9, 0.999, 1e-08, 0.01, 10

LANE = 128
SUBLANE = 8
VMEM_MB_V7X = 64
VMEM_CAP_MB = 56

LS = 128
FT = DFF // 2

NN = (((1,), (0,)), ((), ()))
NT = (((1,), (1,)), ((), ()))
TN = (((0,), (0,)), ((), ()))


def _params(sem, vmem_mb):
    return pltpu.CompilerParams(dimension_semantics=sem,
                                vmem_limit_bytes=min(int(vmem_mb), VMEM_CAP_MB) * 1024 * 1024)


def _dot(a, b, dims=NN):
    return lax.dot_general(a, b, dims, preferred_element_type=F32)


def _sigmoid(x):
    return 1.0 / (1.0 + jnp.exp(-x))


def _split3(v):
    hi = v.astype(BF16)
    r = v - hi.astype(F32)
    mid = r.astype(BF16)
    lo = (r - mid.astype(F32)).astype(BF16)
    return hi, mid, lo


def _dot3(a_f32, b_bf16, dims):
    hi, mid, lo = _split3(a_f32)
    return _dot(hi, b_bf16, dims) + _dot(mid, b_bf16, dims) + _dot(lo, b_bf16, dims)


def _dot3_rhs(a_bf16, b_f32, dims):
    hi, mid, lo = _split3(b_f32)
    return _dot(a_bf16, hi, dims) + _dot(a_bf16, mid, dims) + _dot(a_bf16, lo, dims)


def _matmul(a, b, *, name, out_dtype, ta=False, tb=False, tm=1024, tn=1024, tk=1024, add=None,
            j_outer=False):
    if ta:
        K, M = a.shape
    else:
        M, K = a.shape
    if tb:
        N, K2 = b.shape
    else:
        K2, N = b.shape
    assert K == K2, (a.shape, b.shape, ta, tb)
    tm, tn, tk = min(tm, M), min(tn, N), min(tk, K)
    assert M % tm == 0 and N % tn == 0 and K % tk == 0, (M, N, K, tm, tn, tk)
    nk = K // tk
    dims = (((0 if ta else 1,), (1 if tb else 0,)), ((), ()))
    has_add = add is not None

    def body(*refs):
        if has_add:
            a_ref, b_ref, add_ref, o_ref = refs[:4]
        else:
            a_ref, b_ref, o_ref = refs[:3]
            add_ref = None
        p = lax.dot_general(a_ref[...].astype(BF16), b_ref[...].astype(BF16), dims,
                            preferred_element_type=F32)

        def finish(acc):
            if has_add:
                acc = acc + add_ref[...].astype(F32)
            o_ref[...] = acc.astype(o_ref.dtype)

        if nk == 1:
            finish(p)
        else:
            acc_ref = refs[-1]
            k = pl.program_id(2)

            @pl.when(k == 0)
            def _():
                acc_ref[...] = p

            @pl.when(jnp.logical_and(k > 0, k < nk - 1))
            def _():
                acc_ref[...] += p

            @pl.when(k == nk - 1)
            def _():
                finish(acc_ref[...] + p)

    if j_outer:
        ij = lambda g0, g1: (g1, g0)
        grid = (N // tn, M // tm, nk)
    else:
        ij = lambda g0, g1: (g0, g1)
        grid = (M // tm, N // tn, nk)

    def a_map(g0, g1, k):
        i, _ = ij(g0, g1)
        return (k, i) if ta else (i, k)

    def b_map(g0, g1, k):
        _, j = ij(g0, g1)
        return (j, k) if tb else (k, j)

    def o_map(g0, g1, k):
        return ij(g0, g1)

    in_specs = [pl.BlockSpec((tk, tm) if ta else (tm, tk), a_map),
                pl.BlockSpec((tn, tk) if tb else (tk, tn), b_map)]
    args = [a, b]
    if has_add:
        in_specs.append(pl.BlockSpec((tm, tn), o_map))
        args.append(add)
    scratch = [pltpu.VMEM((tm, tn), F32)] if nk > 1 else []
    osz = jnp.dtype(out_dtype).itemsize
    est = (2 * (tm * tk * a.dtype.itemsize + tk * tn * b.dtype.itemsize) + 2 * tm * tn * osz
           + (2 * tm * tn * add.dtype.itemsize if has_add else 0)
           + 3 * tm * tn * 4 + 2 * (tm * tk + tk * tn)) / 2 ** 20 + 4
    return pl.pallas_call(
        body, grid=grid, in_specs=in_specs, out_specs=pl.BlockSpec((tm, tn), o_map),
        out_shape=jax.ShapeDtypeStruct((M, N), out_dtype), scratch_shapes=scratch, name=name,
        compiler_params=_params(("arbitrary", "arbitrary", "arbitrary"), est))(*args)


def _rms_fwd(x, w, *, name):
    T = x.shape[0]
    tm = min(512, T)

    def body(x_ref, w_ref, o_ref):
        xv = x_ref[...]
        r = lax.rsqrt(jnp.mean(xv * xv, axis=-1, keepdims=True) + EPS)
        o_ref[...] = (xv * r * w_ref[...]).astype(BF16)

    return pl.pallas_call(
        body, grid=(T // tm,),
        in_specs=[pl.BlockSpec((tm, D), lambda i: (i, 0)), pl.BlockSpec((1, D), lambda i: (0, 0))],
        out_specs=pl.BlockSpec((tm, D), lambda i: (i, 0)),
        out_shape=jax.ShapeDtypeStruct((T, D), BF16), name=name,
        compiler_params=_params(("arbitrary",), 24))(x, w)


def _rms_bwd(x, w, dy, dres, *, name):
    T = x.shape[0]
    tm = min(512, T)

    def body(x_ref, w_ref, dy_ref, dres_ref, dx_ref, dw_ref):
        xv = x_ref[...]
        r = lax.rsqrt(jnp.mean(xv * xv, axis=-1, keepdims=True) + EPS)
        xhat = xv * r
        dyv = dy_ref[...].astype(F32)
        g = dyv * w_ref[...]
        dx_ref[...] = dres_ref[...] + r * (g - xhat * jnp.mean(g * xhat, axis=-1, keepdims=True))
        part = jnp.sum(dyv * xhat, axis=0, keepdims=True)

        @pl.when(pl.program_id(0) == 0)
        def _():
            dw_ref[...] = part

        @pl.when(pl.program_id(0) > 0)
        def _():
            dw_ref[...] += part

    row = pl.BlockSpec((tm, D), lambda i: (i, 0))
    vec = pl.BlockSpec((1, D), lambda i: (0, 0))
    return pl.pallas_call(
        body, grid=(T // tm,), in_specs=[row, vec, row, row], out_specs=[row, vec],
        out_shape=[jax.ShapeDtypeStruct((T, D), F32), jax.ShapeDtypeStruct((1, D), F32)], name=name,
        compiler_params=_params(("arbitrary",), 32))(x, w, dy, dres)


def _loss_head(h, tgt, w):
    T = h.shape[0]
    tm = min(512, T)

    def body(h_ref, t_ref, w_ref, loss_ref, dh_ref, dw_ref):
        hv = h_ref[...]
        r = lax.rsqrt(jnp.mean(hv * hv, axis=-1, keepdims=True) + EPS)
        xhat = hv * r
        wv = w_ref[...]
        err = xhat * wv - t_ref[...]
        lpart = 0.5 * jnp.sum(jnp.mean(err * err, axis=-1, keepdims=True), axis=0, keepdims=True)
        dy = err * (1.0 / D)
        g = dy * wv
        dh_ref[...] = r * (g - xhat * jnp.mean(g * xhat, axis=-1, keepdims=True))
        wpart = jnp.sum(dy * xhat, axis=0, keepdims=True)
        lrow = jnp.broadcast_to(lpart, (1, LANE))

        @pl.when(pl.program_id(0) == 0)
        def _():
            dw_ref[...] = wpart
            loss_ref[...] = lrow

        @pl.when(pl.program_id(0) > 0)
        def _():
            dw_ref[...] += wpart
            loss_ref[...] += lrow

    row = pl.BlockSpec((tm, D), lambda i: (i, 0))
    vec = pl.BlockSpec((1, D), lambda i: (0, 0))
    return pl.pallas_call(
        body, grid=(T // tm,), in_specs=[row, row, vec],
        out_specs=[pl.BlockSpec((1, LANE), lambda i: (0, 0)), row, vec],
        out_shape=[jax.ShapeDtypeStruct((1, LANE), F32), jax.ShapeDtypeStruct((T, D), F32),
                   jax.ShapeDtypeStruct((1, D), F32)], name="loss_head",
        compiler_params=_params(("arbitrary",), 32))(h, tgt, w)


_GELU_C = 0.7978845608028654
_GELU_A = 0.044715


def _gelu(x):
    t = jnp.tanh(_GELU_C * (x + _GELU_A * x * x * x))
    return 0.5 * x * (1.0 + t), t


def _gelu_grad(x, t):
    return 0.5 * (1.0 + t) + 0.5 * x * (1.0 - t * t) * _GELU_C * (1.0 + 3.0 * _GELU_A * x * x)


def _gmlp_mask():
    r = lax.broadcasted_iota(jnp.int32, (GB, GB), 0) // GCH
    c = lax.broadcasted_iota(jnp.int32, (GB, GB), 1) // GCH
    return c <= r


def _gmlp_fwd(proj, lnw, lnb, ws, bst):
    T = proj.shape[0]
    tm = min(512, T)
    nblk = tm // GB

    def body(u_ref, v_ref, lnw_ref, lnb_ref, ws_ref, bst_ref, o_ref):
        mask = _gmlp_mask()
        u, _ = _gelu(u_ref[...].astype(F32))
        v, _ = _gelu(v_ref[...].astype(F32))
        for g in range(GG):
            cs = slice(g * GD, (g + 1) * GD)
            vg = v[:, cs]
            mu = jnp.mean(vg, axis=-1, keepdims=True)
            vc = vg - mu
            var = jnp.mean(vc * vc, axis=-1, keepdims=True)
            vn = (vc * lax.rsqrt(var + EPS) * lnw_ref[g:g + 1, :] + lnb_ref[g:g + 1, :]).astype(BF16)
            wsg = jnp.where(mask, ws_ref[g], 0.0).astype(BF16)
            bcol = bst_ref[:, g:g + 1]
            for blk in range(nblk):
                rs = slice(blk * GB, (blk + 1) * GB)
                sv = _dot(wsg, vn[rs, :]) + bcol
                o_ref[rs, cs] = (u[rs, cs] * sv).astype(BF16)

    full = lambda shape: pl.BlockSpec(shape, lambda i: tuple(0 for _ in shape))
    return pl.pallas_call(
        body, grid=(T // tm,),
        in_specs=[pl.BlockSpec((tm, GW), lambda i: (i, 2)), pl.BlockSpec((tm, GW), lambda i: (i, 3)),
                  full((GG, GD)), full((GG, GD)), full((GG, GB, GB)), full((GB, GG))],
        out_specs=pl.BlockSpec((tm, GW), lambda i: (i, 0)),
        out_shape=jax.ShapeDtypeStruct((T, GW), BF16), name="gmlp_fwd",
        compiler_params=_params(("arbitrary",), 40))(proj, proj, lnw, lnb, ws, bst)


def _gmlp_bwd(proj, dya, dproj, lnw, lnb, ws, bst):
    T = proj.shape[0]
    tm = min(512, T)
    nblk = tm // GB

    def body(u_ref, v_ref, dya_ref, dproj_in, lnw_ref, lnb_ref, ws_ref, bst_ref,
             dz_ref, dlnw_ref, dlnb_ref, dws_ref, dbst_ref):
        del dproj_in
        first = pl.program_id(0) == 0

        @pl.when(first)
        def _():
            dlnw_ref[...] = jnp.zeros_like(dlnw_ref)
            dlnb_ref[...] = jnp.zeros_like(dlnb_ref)
            dws_ref[...] = jnp.zeros_like(dws_ref)
            dbst_ref[...] = jnp.zeros_like(dbst_ref)

        mask = _gmlp_mask()
        lane = lax.broadcasted_iota(jnp.int32, (GB, LANE), 1)
        ur = u_ref[...].astype(F32)
        vr = v_ref[...].astype(F32)
        u, tu = _gelu(ur)
        v, tv = _gelu(vr)
        gu = _gelu_grad(ur, tu)
        gv = _gelu_grad(vr, tv)
        dy = dya_ref[...].astype(F32)
        dbst = jnp.zeros((GB, LANE), F32)
        dlnw_rows, dlnb_rows = [], []
        for g in range(GG):
            cs = slice(g * GD, (g + 1) * GD)
            vg = v[:, cs]
            mu = jnp.mean(vg, axis=-1, keepdims=True)
            vc = vg - mu
            var = jnp.mean(vc * vc, axis=-1, keepdims=True)
            rstd = lax.rsqrt(var + EPS)
            xhat = vc * rstd
            lw = lnw_ref[g:g + 1, :]
            vn = (xhat * lw + lnb_ref[g:g + 1, :]).astype(BF16)
            wsg = jnp.where(mask, ws_ref[g], 0.0).astype(BF16)
            bcol = bst_ref[:, g:g + 1]
            dyg = dy[:, cs]
            ug = u[:, cs]
            dsv = dyg * ug
            dsv_b = dsv.astype(BF16)
            dws_g = jnp.zeros((GB, GB), F32)
            bsum = jnp.zeros((GB, 1), F32)
            dvn_parts = []
            for blk in range(nblk):
                rs = slice(blk * GB, (blk + 1) * GB)
                sv = _dot(wsg, vn[rs, :]) + bcol
                dz_ref[rs, cs] = (dyg[rs, :] * sv * gu[rs, cs]).astype(BF16)
                dws_g = dws_g + _dot(dsv_b[rs, :], vn[rs, :], NT)
                bsum = bsum + jnp.sum(dsv[rs, :], axis=-1, keepdims=True)
                dvn_parts.append(_dot(wsg, dsv_b[rs, :], TN))
            dvn = jnp.concatenate(dvn_parts, axis=0)
            dws_ref[g] += jnp.where(mask, dws_g, 0.0)
            dbst = dbst + jnp.where(lane == g, bsum, 0.0)
            dlnw_rows.append(jnp.sum(dvn * xhat, axis=0, keepdims=True))
            dlnb_rows.append(jnp.sum(dvn, axis=0, keepdims=True))
            dxh = dvn * lw
            dvg = rstd * (dxh - jnp.mean(dxh, axis=-1, keepdims=True)
                          - xhat * jnp.mean(dxh * xhat, axis=-1, keepdims=True))
            dz_ref[:, GW + g * GD:GW + (g + 1) * GD] = (dvg * gv[:, cs]).astype(BF16)
        dlnw_ref[...] += jnp.concatenate(dlnw_rows, axis=0)
        dlnb_ref[...] += jnp.concatenate(dlnb_rows, axis=0)
        dbst_ref[...] += dbst

    full = lambda shape: pl.BlockSpec(shape, lambda i: tuple(0 for _ in shape))
    outs = pl.pallas_call(
        body, grid=(T // tm,),
        in_specs=[pl.BlockSpec((tm, GW), lambda i: (i, 2)), pl.BlockSpec((tm, GW), lambda i: (i, 3)),
                  pl.BlockSpec((tm, GW), lambda i: (i, 0)), pl.BlockSpec(memory_space=pl.ANY),
                  full((GG, GD)), full((GG, GD)), full((GG, GB, GB)), full((GB, GG))],
        out_specs=[pl.BlockSpec((tm, 2 * GW), lambda i: (i, 1)), full((GG, GD)), full((GG, GD)),
                   full((GG, GB, GB)), full((GB, LANE))],
        out_shape=[jax.ShapeDtypeStruct(dproj.shape, dproj.dtype), jax.ShapeDtypeStruct((GG, GD), F32),
                   jax.ShapeDtypeStruct((GG, GD), F32), jax.ShapeDtypeStruct((GG, GB, GB), F32),
                   jax.ShapeDtypeStruct((GB, LANE), F32)],
        input_output_aliases={3: 0}, name="gmlp_bwd",
        compiler_params=_params(("arbitrary",), 48))(proj, proj, dya, dproj, lnw, lnb, ws, bst)
    return outs


def _merge_fwd(ya_pre, yb_pre, proj, bias, wpa, wpb):
    T = proj.shape[0]
    tm = min(512, T)

    def body(ya_ref, yb_ref, g_ref, b_ref, wpa_ref, wpb_ref, m_ref, oa_ref, ob_ref):
        ya = _dot(ya_ref[...], wpa_ref[...])
        yb = _dot(yb_ref[...], wpb_ref[...])
        g = g_ref[...].astype(F32)
        sa = _sigmoid(g[:, :D] + b_ref[0:1, :])
        sb = _sigmoid(g[:, D:] + b_ref[1:2, :])
        m_ref[...] = (sa * ya + sb * yb).astype(BF16)
        oa_ref[...] = ya.astype(BF16)
        ob_ref[...] = yb.astype(BF16)

    row = lambda w: pl.BlockSpec((tm, w), lambda i: (i, 0))
    full = lambda shape: pl.BlockSpec(shape, lambda i: tuple(0 for _ in shape))
    o = jax.ShapeDtypeStruct((T, D), BF16)
    return pl.pallas_call(
        body, grid=(T // tm,),
        in_specs=[row(GW), row(SI), row(2 * D), full((2, D)), full((GW, D)), full((SI, D))],
        out_specs=[row(D), row(D), row(D)], out_shape=[o, o, o], name="merge_fwd",
        compiler_params=_params(("arbitrary",), 40))(ya_pre, yb_pre, proj, bias, wpa, wpb)


def _merge_bwd(dm, proj, bias, ya, yb, wpa, wpb):
    T = proj.shape[0]
    tm = min(512, T)

    def body(dm_ref, g_ref, b_ref, ya_ref, yb_ref, wpa_ref, wpb_ref,
             dg_ref, dya_ref, dyb_ref, dpa_ref, dpb_ref, db_ref):
        dmv = dm_ref[...].astype(F32)
        g = g_ref[...].astype(F32)
        sa = _sigmoid(g[:, :D] + b_ref[0:1, :])
        sb = _sigmoid(g[:, D:] + b_ref[1:2, :])
        dya = (dmv * sa).astype(BF16)
        dyb = (dmv * sb).astype(BF16)
        dga = dmv * ya_ref[...].astype(F32) * sa * (1.0 - sa)
        dgb = dmv * yb_ref[...].astype(F32) * sb * (1.0 - sb)
        dg_ref[:, :D] = dga.astype(BF16)
        dg_ref[:, D:] = dgb.astype(BF16)
        dya_ref[...] = dya
        dyb_ref[...] = dyb
        dpa_ref[...] = _dot(dya, wpa_ref[...], NT).astype(BF16)
        dpb_ref[...] = _dot(dyb, wpb_ref[...], NT).astype(BF16)
        part = jnp.concatenate([jnp.sum(dga, axis=0, keepdims=True), jnp.sum(dgb, axis=0, keepdims=True)], axis=0)

        @pl.when(pl.program_id(0) == 0)
        def _():
            db_ref[...] = part

        @pl.when(pl.program_id(0) > 0)
        def _():
            db_ref[...] += part

    row = lambda w: pl.BlockSpec((tm, w), lambda i: (i, 0))
    full = lambda shape: pl.BlockSpec(shape, lambda i: tuple(0 for _ in shape))
    o = lambda w: jax.ShapeDtypeStruct((T, w), BF16)
    return pl.pallas_call(
        body, grid=(T // tm,),
        in_specs=[row(D), row(2 * D), full((2, D)), row(D), row(D), full((GW, D)), full((SI, D))],
        out_specs=[row(2 * D), row(D), row(D), row(GW), row(SI), full((2, D))],
        out_shape=[o(PMAIN), o(D), o(D), o(GW), o(SI), jax.ShapeDtypeStruct((2, D), F32)], name="merge_bwd",
        compiler_params=_params(("arbitrary",), 48))(dm, proj, bias, ya, yb, wpa, wpb)


def _ffn_act_fwd(up, cw, cb):
    T = up.shape[0]
    tm = min(512, T)
    H = SUBLANE

    def body(up_ref, cw_ref, cb_ref, o_ref, ext):
        @pl.when(pl.program_id(1) == 0)
        def _():
            ext[0:H, :] = jnp.zeros((H, 2 * FT), F32)

        ext[H:H + tm, :] = up_ref[...].astype(F32)
        xc = cb_ref[...] + cw_ref[0:1, :] * ext[H - 2:H - 2 + tm, :]
        xc = xc + cw_ref[1:2, :] * ext[H - 1:H - 1 + tm, :]
        xc = xc + cw_ref[2:3, :] * ext[H:H + tm, :]
        gate = xc[:, :FT]
        o_ref[...] = (gate * _sigmoid(gate) * xc[:, FT:]).astype(BF16)
        ext[0:H, :] = ext[tm:tm + H, :]

    return pl.pallas_call(
        body, grid=(2, T // tm),
        in_specs=[pl.BlockSpec((tm, 2 * FT), lambda j, i: (i, j)), pl.BlockSpec((FK, 2 * FT), lambda j, i: (0, j)),
                  pl.BlockSpec((1, 2 * FT), lambda j, i: (0, j))],
        out_specs=pl.BlockSpec((tm, FT), lambda j, i: (i, j)),
        out_shape=jax.ShapeDtypeStruct((T, DFF), BF16),
        scratch_shapes=[pltpu.VMEM((tm + H, 2 * FT), F32)], name="ffn_act_fwd",
        compiler_params=_params(("arbitrary", "arbitrary"), 48))(up, cw, cb)


def _ffn_act_bwd(up, dact, cw, cb):
    T = up.shape[0]
    tm = min(512, T)
    nt = T // tm
    H = SUBLANE
    hb = tm // H

    def body(up_ref, halo_ref, da_ref, cw_ref, cb_ref, dup_ref, dcw_ref, dcb_ref, ext, extd):
        i = pl.program_id(1)
        ri = nt - 1 - i

        @pl.when(i == 0)
        def _():
            extd[tm:tm + H, :] = jnp.zeros((H, 2 * FT), F32)
            dcw_ref[...] = jnp.zeros_like(dcw_ref)
            dcb_ref[...] = jnp.zeros_like(dcb_ref)

        ext[0:H, :] = jnp.where(ri > 0, halo_ref[...].astype(F32), 0.0)
        ext[H:H + tm, :] = up_ref[...].astype(F32)
        taps = [ext[H - 2 + k:H - 2 + k + tm, :] for k in range(FK)]
        xc = cb_ref[...] + cw_ref[0:1, :] * taps[0] + cw_ref[1:2, :] * taps[1] + cw_ref[2:3, :] * taps[2]
        gate, val = xc[:, :FT], xc[:, FT:]
        sg = _sigmoid(gate)
        dav = da_ref[...].astype(F32)
        dgate = dav * val * sg * (1.0 + gate * (1.0 - sg))
        dval = dav * gate * sg
        dxc = jnp.concatenate([dgate, dval], axis=1)
        dcb_ref[...] += jnp.sum(dxc, axis=0, keepdims=True)
        dcw_ref[...] += jnp.concatenate([jnp.sum(dxc * t, axis=0, keepdims=True) for t in taps], axis=0)
        extd[0:tm, :] = dxc
        dup = cw_ref[2:3, :] * dxc + cw_ref[1:2, :] * extd[1:1 + tm, :] + cw_ref[0:1, :] * extd[2:2 + tm, :]
        dup_ref[...] = dup.astype(BF16)
        extd[tm:tm + H, :] = dxc[0:H, :]

    return pl.pallas_call(
        body, grid=(2, nt),
        in_specs=[pl.BlockSpec((tm, 2 * FT), lambda j, i: (nt - 1 - i, j)),
                  pl.BlockSpec((H, 2 * FT), lambda j, i: (jnp.maximum((nt - 1 - i) * hb - 1, 0), j)),
                  pl.BlockSpec((tm, FT), lambda j, i: (nt - 1 - i, j)),
                  pl.BlockSpec((FK, 2 * FT), lambda j, i: (0, j)), pl.BlockSpec((1, 2 * FT), lambda j, i: (0, j))],
        out_specs=[pl.BlockSpec((tm, 2 * FT), lambda j, i: (nt - 1 - i, j)),
                   pl.BlockSpec((FK, 2 * FT), lambda j, i: (0, j)), pl.BlockSpec((1, 2 * FT), lambda j, i: (0, j))],
        out_shape=[jax.ShapeDtypeStruct((T, 2 * DFF), BF16), jax.ShapeDtypeStruct((FK, 2 * DFF), F32),
                   jax.ShapeDtypeStruct((1, 2 * DFF), F32)],
        scratch_shapes=[pltpu.VMEM((tm + H, 2 * FT), F32), pltpu.VMEM((tm + H, 2 * FT), F32)], name="ffn_act_bwd",
        compiler_params=_params(("arbitrary", "arbitrary"), 56))(up, up, dact, cw, cb)


def _softplus(x):
    e = jnp.exp(-jnp.abs(x))
    return jnp.maximum(x, 0.0) + jnp.where(e < 1e-4, e * (1.0 - 0.5 * e), jnp.log(1.0 + e))


def _ssd_consts():
    li = lax.broadcasted_iota(jnp.int32, (LS, LS), 0)
    si = lax.broadcasted_iota(jnp.int32, (LS, LS), 1)
    tril = si <= li
    hh = lax.broadcasted_iota(jnp.int32, (LANE, SI), 0)
    cc = lax.broadcasted_iota(jnp.int32, (LANE, SI), 1) // SP
    expand = jnp.where(hh == cc, 1.0, 0.0).astype(BF16)
    return tril, expand


def _ssd_pre(ext, cw_ref, cb_ref, dt_ref, dtb_ref, alog_ref, tril, expand):
    H = SUBLANE
    taps = [ext[H - (SK - 1) + k:H - (SK - 1) + k + LS, :] for k in range(SK)]
    xc = cb_ref[...]
    for k in range(SK):
        xc = xc + cw_ref[k:k + 1, :] * taps[k]
    sx = _sigmoid(xc)
    xbc = xc * sx
    xs, bm, cm = xbc[:, :SI], xbc[:, SI:SI + SG * SN], xbc[:, SI + SG * SN:]
    dtin = dt_ref[...] + dtb_ref[...]
    dt = _softplus(dtin)
    a_neg = -jnp.exp(alog_ref[...])
    dta = dt * a_neg
    trilb = jnp.where(tril, 1.0, 0.0).astype(BF16)
    a = _dot3_rhs(trilb, dta, NN)
    a_exp = _dot3(a, expand, NN)
    dt_exp = _dot3(dt, expand, NN)
    xdt = xs * dt_exp
    a_last = a_exp[LS - 1:LS, :]
    return dict(taps=taps, xc=xc, sx=sx, xs=xs, bm=bm, cm=cm, dtin=dt_ref[...] + dtb_ref[...], dt=dt, a_neg=a_neg,
                a=a, a_t=a.T, a_exp=a_exp, dt_exp=dt_exp, xdt=xdt, ea=jnp.exp(a_exp),
                w=jnp.exp(a_last - a_exp), eal=jnp.exp(a_last))


def _head_decay(pre, tril, h):
    seg = pre["a"][:, h:h + 1] - pre["a_t"][h:h + 1, :]
    return jnp.exp(jnp.where(tril, seg, -1e30))


def _ssd_fwd(proj, dtraw, cw, cb, dtb, alog, dexp, nw):
    T = proj.shape[0]
    nc = T // LS
    H = SUBLANE

    def body(z_ref, x_ref, dt_ref, cw_ref, cb_ref, dtb_ref, alog_ref, dexp_ref, nw_ref,
             yb_ref, y_ref, sp_ref, ext, st):
        @pl.when(pl.program_id(0) == 0)
        def _():
            ext[0:H, :] = jnp.zeros((H, SXBC), F32)
            st[...] = jnp.zeros_like(st)

        ext[H:H + LS, :] = x_ref[...].astype(F32)
        tril, expand = _ssd_consts()
        pre = _ssd_pre(ext, cw_ref, cb_ref, dt_ref, dtb_ref, alog_ref, tril, expand)
        lane = lax.broadcasted_iota(jnp.int32, (LS, LANE), 1)
        lo = lane < SP
        zf = z_ref[...].astype(F32)
        siluz = zf * _sigmoid(zf)
        for g in range(SG):
            gs = slice(g * SGW, (g + 1) * SGW)
            bg = pre["bm"][:, g * SN:(g + 1) * SN].astype(BF16)
            cg = pre["cm"][:, g * SN:(g + 1) * SN].astype(BF16)
            gmat = _dot(cg, bg, NT)
            sg = st[g]
            sp_ref[0, g] = sg
            yoff = _dot(cg, sg.astype(BF16))
            parts = []
            for j in range(SGW // LANE):
                h0 = g * (SGW // SP) + 2 * j
                m0 = gmat * _head_decay(pre, tril, h0)
                m1 = gmat * _head_decay(pre, tril, h0 + 1)
                xp = pre["xdt"][:, g * SGW + j * LANE:g * SGW + (j + 1) * LANE]
                rhs = jnp.concatenate([jnp.where(lo, xp, 0.0), jnp.where(lo, 0.0, xp)], axis=0).astype(BF16)
                parts.append(_dot(jnp.concatenate([m0, m1], axis=1).astype(BF16), rhs))
            y = (jnp.concatenate(parts, axis=1) + pre["ea"][:, gs] * yoff + dexp_ref[:, gs] * pre["xs"][:, gs])
            st[g] = pre["eal"][:, gs] * sg + _dot(bg, (pre["w"][:, gs] * pre["xdt"][:, gs]).astype(BF16), TN)
            y_ref[:, gs] = y
            yg = y * siluz[:, gs]
            r = lax.rsqrt(jnp.mean(yg * yg, axis=-1, keepdims=True) + EPS)
            yb_ref[:, gs] = (yg * r * nw_ref[:, gs]).astype(BF16)
        ext[0:H, :] = ext[LS:LS + H, :]

    vec = lambda w: pl.BlockSpec((1, w), lambda c: (0, 0))
    return pl.pallas_call(
        body, grid=(nc,),
        in_specs=[pl.BlockSpec((LS, SI), lambda c: (c, 2)), pl.BlockSpec((LS, SXBC), lambda c: (c, 2)),
                  pl.BlockSpec((LS, LANE), lambda c: (c, 0)),
                  pl.BlockSpec((SK, SXBC), lambda c: (0, 0)), vec(SXBC), vec(LANE), vec(LANE), vec(SI), vec(SI)],
        out_specs=[pl.BlockSpec((LS, SI), lambda c: (c, 0)), pl.BlockSpec((LS, SI), lambda c: (c, 0)),
                   pl.BlockSpec((1, SG, SN, SGW), lambda c: (c, 0, 0, 0))],
        out_shape=[jax.ShapeDtypeStruct((T, SI), BF16), jax.ShapeDtypeStruct((T, SI), F32),
                   jax.ShapeDtypeStruct((nc, SG, SN, SGW), F32)],
        scratch_shapes=[pltpu.VMEM((LS + H, SXBC), F32), pltpu.VMEM((SG, SN, SGW), F32)], name="ssd_fwd",
        compiler_params=_params(("arbitrary",), VMEM_CAP_MB))(proj, proj, dtraw, cw, cb, dtb, alog, dexp, nw)


def _ssd_bwd(proj, dtraw, y, sprev, dyb, dproj, cw, cb, dtb, alog, dexp, nw):
    T = proj.shape[0]
    nc = T // LS
    H = SUBLANE
    hb = LS // H
    NJ = (SI + SXBC) // D
    J0 = (2 * D + 2 * GW) // D

    def body(z_ref, x_ref, halo_ref, dt_ref, y_ref, sp_ref, dyb_ref, dproj_in,
             cw_ref, cb_ref, dtb_ref, alog_ref, dexp_ref, nw_ref,
             dp_ref, ddt_ref, dcw_ref, dcb_ref, ddtb_ref, da_ref, dd_ref, dnw_ref,
             ext, extd, ds, stage):
        del dproj_in
        i = pl.program_id(0)
        j = pl.program_id(1)
        c = nc - 1 - i

        @pl.when(jnp.logical_and(i == 0, j == 0))
        def _():
            extd[LS:LS + H, :] = jnp.zeros((H, SXBC), F32)
            ds[...] = jnp.zeros_like(ds)
            for r in (dcw_ref, dcb_ref, ddtb_ref, da_ref, dd_ref, dnw_ref):
                r[...] = jnp.zeros_like(r)

        @pl.when(j == 0)
        def _():
            ext[0:H, :] = jnp.where(c > 0, halo_ref[...].astype(F32), 0.0)
            ext[H:H + LS, :] = x_ref[...].astype(F32)
            tril, expand = _ssd_consts()
            pre = _ssd_pre(ext, cw_ref, cb_ref, dt_ref, dtb_ref, alog_ref, tril, expand)
            lane = lax.broadcasted_iota(jnp.int32, (LS, LANE), 1)
            sub = lax.broadcasted_iota(jnp.int32, (LANE, LS), 0)
            rowi = lax.broadcasted_iota(jnp.int32, (LS, 1), 0)
            lo = lane < SP
            xs, xdt, ea, w, eal = pre["xs"], pre["xdt"], pre["ea"], pre["w"], pre["eal"]

            zf = z_ref[...].astype(F32)
            sz = _sigmoid(zf)
            siluz = zf * sz
            yv = y_ref[...]
            yg = yv * siluz
            dout = dyb_ref[...].astype(F32)
            dyg_parts, dnw_parts = [], []
            for g in range(SG):
                gs = slice(g * SGW, (g + 1) * SGW)
                ygg = yg[:, gs]
                r = lax.rsqrt(jnp.mean(ygg * ygg, axis=-1, keepdims=True) + EPS)
                yhat = ygg * r
                dn = dout[:, gs] * nw_ref[:, gs]
                dnw_parts.append(jnp.sum(dout[:, gs] * yhat, axis=0, keepdims=True))
                dyg_parts.append(r * (dn - yhat * jnp.mean(dn * yhat, axis=-1, keepdims=True)))
            dyg = jnp.concatenate(dyg_parts, axis=1)
            dnw_ref[...] += jnp.concatenate(dnw_parts, axis=1)
            dy = dyg * siluz
            stage[:, 0:SI] = (dyg * yv * sz * (1.0 + zf * (1.0 - sz))).astype(BF16)
            dd_ref[...] += jnp.sum(dy * xs, axis=0, keepdims=True)
            tt = ea * dy

            da_rows = jnp.zeros((LS, LANE), F32)
            da_cols = jnp.zeros((LANE, LS), F32)
            dxdt_parts, db_parts, dc_parts, daexp_parts = [], [], [], []
            for g in range(SG):
                gs = slice(g * SGW, (g + 1) * SGW)
                bg = pre["bm"][:, g * SN:(g + 1) * SN].astype(BF16)
                cg = pre["cm"][:, g * SN:(g + 1) * SN].astype(BF16)
                sg = sp_ref[0, g]
                sgb = sg.astype(BF16)
                dsg = ds[g]
                dsgb = dsg.astype(BF16)
                ttg = tt[:, gs].astype(BF16)
                yoff = _dot(cg, sgb)
                dc = _dot(ttg, sgb, NT)
                gmat = _dot(cg, bg, NT)
                dgm = jnp.zeros((LS, LS), F32)
                dxdt_pairs = []
                for jj in range(SGW // LANE):
                    h0 = g * (SGW // SP) + 2 * jj
                    ps = slice(g * SGW + jj * LANE, g * SGW + (jj + 1) * LANE)
                    l0 = _head_decay(pre, tril, h0)
                    l1 = _head_decay(pre, tril, h0 + 1)
                    m0 = gmat * l0
                    m1 = gmat * l1
                    dyp = dy[:, ps]
                    dy_lo = jnp.where(lo, dyp, 0.0).astype(BF16)
                    dy_hi = jnp.where(lo, 0.0, dyp).astype(BF16)
                    xpb = xdt[:, ps].astype(BF16)
                    dm0 = _dot(dy_lo, xpb, NT)
                    dm1 = _dot(dy_hi, xpb, NT)
                    q0 = dm0 * m0
                    q1 = dm1 * m1
                    da_rows = da_rows + jnp.where(lane == h0, jnp.sum(q0, axis=1, keepdims=True), 0.0)
                    da_rows = da_rows + jnp.where(lane == h0 + 1, jnp.sum(q1, axis=1, keepdims=True), 0.0)
                    da_cols = da_cols + jnp.where(sub == h0, jnp.sum(q0, axis=0, keepdims=True), 0.0)
                    da_cols = da_cols + jnp.where(sub == h0 + 1, jnp.sum(q1, axis=0, keepdims=True), 0.0)
                    dgm = dgm + dm0 * l0 + dm1 * l1
                    mcat = jnp.concatenate([m0, m1], axis=0).astype(BF16)
                    dycat = jnp.concatenate([dy_lo, dy_hi], axis=0)
                    dxdt_pairs.append(_dot(mcat, dycat, TN))
                dgb = dgm.astype(BF16)
                dc = dc + _dot(dgb, bg)
                db = _dot(dgb, cg, TN)
                zg = _dot(bg, dsgb)
                wg, xdtg = w[:, gs], xdt[:, gs]
                dxdt_g = jnp.concatenate(dxdt_pairs, axis=1) + wg * zg
                qg = zg * xdtg * wg
                last = (jnp.sum(qg, axis=0, keepdims=True)
                        + jnp.sum(dsg * sg, axis=0, keepdims=True) * eal[:, gs])
                daexp_parts.append(dy[:, gs] * ea[:, gs] * yoff - qg + jnp.where(rowi == LS - 1, last, 0.0))
                db = db + _dot((wg * xdtg).astype(BF16), dsgb, NT)
                ds[g] = eal[:, gs] * dsg + _dot(cg, ttg, TN)
                dxdt_parts.append(dxdt_g)
                db_parts.append(db)
                dc_parts.append(dc)
            dxdt = jnp.concatenate(dxdt_parts, axis=1)
            da_exp = jnp.concatenate(daexp_parts, axis=1)
            da = _dot3(da_exp, expand, NT) + da_rows - da_cols.T
            triub = jnp.where(tril, 1.0, 0.0).astype(BF16)
            ddta = _dot3_rhs(triub, da, TN)
            ddt = ddta * pre["a_neg"] + _dot3(dxdt * xs, expand, NT)
            da_ref[...] += jnp.sum(ddta * pre["dt"], axis=0, keepdims=True)
            ddt_raw = ddt * _sigmoid(pre["dtin"])
            ddt_ref[...] = ddt_raw
            ddtb_ref[...] += jnp.sum(ddt_raw, axis=0, keepdims=True)
            dxs = dexp_ref[...] * dy + dxdt * pre["dt_exp"]
            dxbc = jnp.concatenate([dxs] + db_parts + dc_parts, axis=1)
            sx, xc = pre["sx"], pre["xc"]
            dxc = dxbc * sx * (1.0 + xc * (1.0 - sx))
            dcb_ref[...] += jnp.sum(dxc, axis=0, keepdims=True)
            dcw_ref[...] += jnp.concatenate([jnp.sum(dxc * t, axis=0, keepdims=True) for t in pre["taps"]], axis=0)
            extd[0:LS, :] = dxc
            dxr = cw_ref[SK - 1:SK, :] * dxc
            for k in range(SK - 1):
                dxr = dxr + cw_ref[k:k + 1, :] * extd[SK - 1 - k:SK - 1 - k + LS, :]
            stage[:, SI:] = dxr.astype(BF16)
            extd[LS:LS + H, :] = dxc[0:H, :]

        dp_ref[...] = stage[:, pl.ds(pl.multiple_of(j * D, D), D)]

    vec = lambda w: pl.BlockSpec((1, w), lambda i, j: (0, 0))
    rev = lambda w, cb_: pl.BlockSpec((LS, w), lambda i, j: (nc - 1 - i, cb_))
    outs = pl.pallas_call(
        body, grid=(nc, NJ),
        in_specs=[rev(SI, 2), rev(SXBC, 2),
                  pl.BlockSpec((H, SXBC), lambda i, j: (jnp.maximum((nc - 1 - i) * hb - 1, 0), 2)),
                  rev(LANE, 0), rev(SI, 0),
                  pl.BlockSpec((1, SG, SN, SGW), lambda i, j: (nc - 1 - i, 0, 0, 0)),
                  rev(SI, 0), pl.BlockSpec(memory_space=pl.ANY),
                  pl.BlockSpec((SK, SXBC), lambda i, j: (0, 0)), vec(SXBC), vec(LANE), vec(LANE), vec(SI), vec(SI)],
        out_specs=[pl.BlockSpec((LS, D), lambda i, j: (nc - 1 - i, J0 + j)), rev(LANE, 0),
                   pl.BlockSpec((SK, SXBC), lambda i, j: (0, 0)), vec(SXBC), vec(LANE), vec(LANE), vec(SI), vec(SI)],
        out_shape=[jax.ShapeDtypeStruct(dproj.shape, dproj.dtype), jax.ShapeDtypeStruct((T, LANE), F32),
                   jax.ShapeDtypeStruct((SK, SXBC), F32), jax.ShapeDtypeStruct((1, SXBC), F32),
                   jax.ShapeDtypeStruct((1, LANE), F32), jax.ShapeDtypeStruct((1, LANE), F32),
                   jax.ShapeDtypeStruct((1, SI), F32), jax.ShapeDtypeStruct((1, SI), F32)],
        scratch_shapes=[pltpu.VMEM((LS + H, SXBC), F32), pltpu.VMEM((LS + H, SXBC), F32),
                        pltpu.VMEM((SG, SN, SGW), F32), pltpu.VMEM((LS, SI + SXBC), BF16)],
        input_output_aliases={7: 0}, name="ssd_bwd",
        compiler_params=_params(("arbitrary", "arbitrary"), VMEM_CAP_MB))(
            proj, proj, proj, dtraw, y, sprev, dyb, dproj, cw, cb, dtb, alog, dexp, nw)
    return outs


def _perm_ffn_cols(a):
    lead = a.shape[:-1]
    return a.reshape(lead + (2, 2, FT)).swapaxes(-3, -2).reshape(lead + (2 * DFF,))


def _perm_ffn_rows(a):
    return a.reshape((2, 2, FT) + a.shape[1:]).swapaxes(0, 1).reshape(a.shape)


def _pad_lanes(v, n=LANE):
    return jnp.pad(v, ((0, 0), (0, n - v.shape[-1])))


def _local_step(x, tgt, w):
    T = x.shape[0]
    win_main = w["w_in_t"][:PMAIN]
    win_dt = jnp.pad(w["w_in_t"][PMAIN:], ((0, LANE - SH), (0, 0)))
    wup = _perm_ffn_rows(w["ffn_w_up_t"])
    fcw = _perm_ffn_cols(w["ffn_conv_w"])
    fcb = _perm_ffn_cols(w["ffn_conv_b"][None, :])
    mixw = w["mix_norm_w"][None, :]
    ffnw = w["ffn_norm_w"][None, :]
    finw = w["final_norm_w"][None, :]
    bst = w["gmlp_bs"].T
    scb = w["ssm_conv_b"][None, :]
    dtb = _pad_lanes(w["ssm_dt_bias"][None, :])
    alog = _pad_lanes(w["ssm_a_log"][None, :])
    dexp = jnp.repeat(w["ssm_d"], SP)[None, :]
    snw = w["ssm_norm_w"][None, :]

    xn = _rms_fwd(x, mixw, name="mix_norm")
    proj = _matmul(xn, win_main, name="in_proj", out_dtype=BF16, tb=True, tn=1536, j_outer=True)
    dtraw = _matmul(xn, win_dt, name="in_proj_dt", out_dtype=F32, tb=True)
    ya_pre = _gmlp_fwd(proj, w["gmlp_ln_w"], w["gmlp_ln_b"], w["gmlp_ws"], bst)
    yb_pre, y_ssd, sprev = _ssd_fwd(proj, dtraw, w["ssm_conv_w"], scb, dtb, alog, dexp, snw)
    merged, ya, yb = _merge_fwd(ya_pre, yb_pre, proj, w["gate_bias"], w["w_proj_a"], w["w_proj_b"])
    h1 = _matmul(merged, w["w_out"], name="out_proj", out_dtype=F32, add=x)
    hn = _rms_fwd(h1, ffnw, name="ffn_norm")
    up = _matmul(hn, wup, name="ffn_up", out_dtype=BF16, tb=True, tn=FT, j_outer=True)
    act = _ffn_act_fwd(up, fcw, fcb)
    h2 = _matmul(act, w["ffn_w_down"], name="ffn_down", out_dtype=F32, tk=FT, add=h1)

    loss_row, dh2, d_finw = _loss_head(h2, tgt, finw)
    dact = _matmul(dh2, w["ffn_w_down"], name="ffn_down_dx", out_dtype=BF16, tb=True, tn=FT)
    d_wdown = _matmul(act, dh2, name="ffn_down_dw", out_dtype=F32, ta=True, tm=FT)
    dup, d_fcw, d_fcb = _ffn_act_bwd(up, dact, fcw, fcb)
    dhn = _matmul(dup, wup, name="ffn_up_dx", out_dtype=F32, tk=FT)
    d_wup = _matmul(dup, hn, name="ffn_up_dw", out_dtype=F32, ta=True, tm=FT)
    dh1, d_ffnw = _rms_bwd(h1, ffnw, dhn, dh2, name="ffn_norm_bwd")
    dmerged = _matmul(dh1, w["w_out"], name="out_proj_dx", out_dtype=BF16, tb=True)
    d_wout = _matmul(merged, dh1, name="out_proj_dw", out_dtype=F32, ta=True)
    dproj, dya, dyb, dya_pre, dyb_pre, d_gbias = _merge_bwd(dmerged, proj, w["gate_bias"], ya, yb,
                                                           w["w_proj_a"], w["w_proj_b"])
    d_wpa = _matmul(ya_pre, dya, name="proj_a_dw", out_dtype=F32, ta=True)
    d_wpb = _matmul(yb_pre, dyb, name="proj_b_dw", out_dtype=F32, ta=True)
    dproj, d_lnw, d_lnb, d_ws, d_bst = _gmlp_bwd(proj, dya_pre, dproj, w["gmlp_ln_w"], w["gmlp_ln_b"],
                                                 w["gmlp_ws"], bst)
    dproj, ddt, d_scw, d_scb, d_dtb, d_a, d_dch, d_snw = _ssd_bwd(
        proj, dtraw, y_ssd, sprev, dyb_pre, dproj, w["ssm_conv_w"], scb, dtb, alog, dexp, snw)
    dxn = _matmul(ddt, win_dt, name="in_proj_dt_dx", out_dtype=F32)
    dxn = _matmul(dproj, win_main, name="in_proj_dx", out_dtype=F32, tk=1536, add=dxn)
    d_win_main = _matmul(dproj, xn, name="in_proj_dw", out_dtype=F32, ta=True, tm=1536)
    d_win_dt = _matmul(ddt, xn, name="in_proj_dt_dw", out_dtype=F32, ta=True)
    grad_x, d_mixw = _rms_bwd(x, mixw, dxn, dh1, name="mix_norm_bwd")

    a_neg = -jnp.exp(w["ssm_a_log"])
    grads = {
        "mix_norm_w": d_mixw[0],
        "w_in_t": jnp.concatenate([d_win_main, d_win_dt[:SH]], axis=0),
        "gate_bias": d_gbias,
        "gmlp_ln_w": d_lnw, "gmlp_ln_b": d_lnb, "gmlp_ws": d_ws, "gmlp_bs": d_bst[:, :GG].T,
        "ssm_conv_w": d_scw, "ssm_conv_b": d_scb[0],
        "ssm_dt_bias": d_dtb[0, :SH], "ssm_a_log": d_a[0, :SH] * a_neg,
        "ssm_d": d_dch.reshape(SH, SP).sum(axis=-1), "ssm_norm_w": d_snw[0],
        "w_proj_a": d_wpa, "w_proj_b": d_wpb, "w_out": d_wout,
        "ffn_norm_w": d_ffnw[0],
        "ffn_w_up_t": _perm_ffn_rows(d_wup), "ffn_conv_w": _perm_ffn_cols(d_fcw), "ffn_conv_b": _perm_ffn_cols(d_fcb)[0],
        "ffn_w_down": d_wdown, "final_norm_w": d_finw[0],
    }
    return loss_row[0, 0], grad_x, grads


MESH = pl.DeviceIdType.MESH
HBM_SPEC = pl.BlockSpec(memory_space=pltpu.HBM)


def _axes():
    return lax.axis_index("x"), lax.axis_index("y"), lax.axis_index("c")


def _all_gather(shards, *, name):
    na = len(shards)

    def body(*refs):
        x_refs, out_refs = refs[:na], refs[na:2 * na]
        send_sems, recv_sems, local_sems = refs[2 * na:]
        x, y, c = _axes()
        me, sibling = (x, y, c), (x, y, 1 - c)
        chips = [(1 - x, y), (x, 1 - y), (1 - x, 1 - y)]

        def slot(a, px, py, pc):
            return out_refs[a].at[4 * px + 2 * py + pc]

        def copy(a, k, block, to, src=None):
            return pltpu.make_async_remote_copy(
                src_ref=slot(a, *block) if src is None else src, dst_ref=slot(a, *block),
                send_sem=send_sems.at[7 * a + k], recv_sem=recv_sems.at[7 * a + k], device_id=to, device_id_type=MESH)

        mine = [pltpu.make_async_copy(x_refs[a], slot(a, *me), local_sems.at[a]) for a in range(na)]
        for cp in mine:
            cp.start()
        first = []
        for a in range(na):
            first.append(copy(a, 0, me, sibling, src=x_refs[a]))
            first += [copy(a, 1 + j, me, (*chip, c), src=x_refs[a]) for j, chip in enumerate(chips)]
        for cp in first:
            cp.start()
        passed = []
        for j, chip in enumerate(chips):
            for a in range(na):
                copy(a, 1 + j, (*chip, c), me).wait_recv()
                cp = copy(a, 4 + j, (*chip, c), sibling)
                cp.start()
                passed.append(cp)
        for a in range(na):
            copy(a, 0, sibling, me).wait_recv()
        for j, chip in enumerate(chips):
            for a in range(na):
                copy(a, 4 + j, (*chip, 1 - c), me).wait_recv()
        for cp in first + passed:
            cp.wait_send()
        for cp in mine:
            cp.wait()

    return pl.pallas_call(
        body, out_shape=[jax.ShapeDtypeStruct((NDEV,) + s.shape, s.dtype) for s in shards],
        in_specs=[HBM_SPEC] * na, out_specs=[HBM_SPEC] * na,
        scratch_shapes=[pltpu.SemaphoreType.DMA((7 * na,)), pltpu.SemaphoreType.DMA((7 * na,)),
                        pltpu.SemaphoreType.DMA((na,))],
        name=name)(*shards)


def _exchange(srcs, plan, *, name):
    na = len(srcs)
    n = len(plan(0, 0, 0))

    def body(*refs):
        src_refs, out_refs = refs[:na], refs[na:2 * na]
        send_sems, recv_sems = refs[2 * na:]
        x, y, c = _axes()
        copies = []
        for k, (slab, peer) in enumerate(plan(x, y, c)):
            for a in range(na):
                cp = pltpu.make_async_remote_copy(
                    src_ref=src_refs[a].at[slab], dst_ref=out_refs[a].at[k], send_sem=send_sems.at[n * a + k],
                    recv_sem=recv_sems.at[n * a + k], device_id=peer, device_id_type=MESH)
                cp.start()
                copies.append(cp)
        for cp in copies:
            cp.wait()

    return pl.pallas_call(
        body, out_shape=[jax.ShapeDtypeStruct((n,) + s.shape[1:], s.dtype) for s in srcs],
        in_specs=[HBM_SPEC] * na, out_specs=[HBM_SPEC] * na,
        scratch_shapes=[pltpu.SemaphoreType.DMA((n * na,)), pltpu.SemaphoreType.DMA((n * na,))], name=name)(*srcs)


def _to_sibling_plan(x, y, c):
    return [(2 * q + (1 - c), (x, y, 1 - c)) for q in range(4)]


def _to_chips_plan(x, y, c):
    q = 2 * x + y
    return [(q ^ 2, (1 - x, y, c)), (q ^ 1, (x, 1 - y, c)), (q ^ 3, (1 - x, 1 - y, c))]


def _row_tile(rows, row_bytes, budget=2 * 2 ** 20, align=2 * SUBLANE):
    best = None
    for d in range(align, rows + 1, align):
        if rows % d == 0 and d * row_bytes <= budget:
            best = d
    return best or rows


def _pair_add(g, ra, c_idx, *, name):
    _, _, R, C = g.shape
    tr = _row_tile(R, C * 4)

    def body(c_ref, g_ref, ra_ref, o_ref):
        del c_ref
        o_ref[...] = (g_ref[0].astype(F32) + ra_ref[...].astype(F32)).astype(o_ref.dtype)

    return pl.pallas_call(
        body,
        grid_spec=pltpu.PrefetchScalarGridSpec(
            num_scalar_prefetch=1, grid=(4, R // tr),
            in_specs=[pl.BlockSpec((1, 1, tr, C), lambda q, r, cr: (q, cr[0], r, 0)),
                      pl.BlockSpec((1, tr, C), lambda q, r, cr: (q, r, 0))],
            out_specs=pl.BlockSpec((1, tr, C), lambda q, r, cr: (q, r, 0))),
        out_shape=jax.ShapeDtypeStruct((4, R, C), g.dtype), name=name,
        compiler_params=_params(("arbitrary", "arbitrary"), 24))(c_idx, g, ra)


def _grad_sum(p, rb, q_idx, *, name):
    _, R, C = p.shape
    tr = _row_tile(R, C * 4)

    def body(q_ref, p_ref, rb_ref, o_ref):
        del q_ref
        g = p_ref[0].astype(F32)
        for k in range(3):
            g = g + rb_ref[k].astype(F32)
        o_ref[...] = g

    return pl.pallas_call(
        body,
        grid_spec=pltpu.PrefetchScalarGridSpec(
            num_scalar_prefetch=1, grid=(R // tr,),
            in_specs=[pl.BlockSpec((1, tr, C), lambda r, qr: (qr[0], r, 0)),
                      pl.BlockSpec((3, tr, C), lambda r, qr: (0, r, 0))],
            out_specs=pl.BlockSpec((tr, C), lambda r, qr: (r, 0))),
        out_shape=jax.ShapeDtypeStruct((R, C), F32), name=name,
        compiler_params=_params(("arbitrary",), 32))(q_idx, p, rb)


def _adamw(g, w, m, v):
    m = ADAM_B1 * m + (1.0 - ADAM_B1) * g
    v = ADAM_B2 * v + (1.0 - ADAM_B2) * (g * g)
    m_hat = m / (1.0 - ADAM_B1 ** ADAM_STEP)
    v_hat = v / (1.0 - ADAM_B2 ** ADAM_STEP)
    delta = -ADAM_LR * (m_hat / (jnp.sqrt(v_hat) + ADAM_EPS) + ADAM_WD * w)
    return delta, m, v


def _adam(g, w, m, v, *, name):
    R, C = w.shape
    tr = _row_tile(R, C * 4, budget=2 ** 20, align=SUBLANE)

    def body(g_ref, w_ref, m_ref, v_ref, d_out, m_out, v_out):
        delta, mn, vn = _adamw(g_ref[...], w_ref[...], m_ref[...], v_ref[...])
        d_out[...] = delta
        m_out[...] = mn
        v_out[...] = vn

    row = pl.BlockSpec((tr, C), lambda r: (r, 0))
    o = jax.ShapeDtypeStruct((R, C), F32)
    return pl.pallas_call(
        body, grid=(R // tr,), in_specs=[row, row, row, row], out_specs=[row, row, row], out_shape=[o, o, o],
        name=name, compiler_params=_params(("arbitrary",), 32))(g, w, m, v)


WEIGHTS = ["mix_norm_w", "w_in", "gate_bias", "gmlp_ln_w", "gmlp_ln_b", "gmlp_ws", "gmlp_bs", "ssm_conv_w",
           "ssm_conv_b", "ssm_dt_bias", "ssm_a_log", "ssm_d", "ssm_norm_w", "w_proj_a", "w_proj_b", "w_out",
           "ffn_norm_w", "ffn_w_up", "ffn_conv_w", "ffn_conv_b", "ffn_w_down", "final_norm_w"]
SHARDED = {"w_in": ((D, IN_COLS), 1), "gate_bias": ((2, D), 1), "ssm_conv_w": ((SK, SXBC), 1),
           "w_proj_a": ((GW, D), 0), "w_proj_b": ((SI, D), 0), "w_out": ((D, D), 0),
           "ffn_w_up": ((D, 2 * DFF), 1), "ffn_conv_w": ((FK, 2 * DFF), 1), "ffn_w_down": ((DFF, D), 0)}
REPLICATED = {"mix_norm_w": (D,), "gmlp_ln_w": (GG, GD), "gmlp_ln_b": (GG, GD), "gmlp_ws": (GG, GB, GB),
              "gmlp_bs": (GG, GB), "ssm_conv_b": (SXBC,), "ssm_dt_bias": (SH,), "ssm_a_log": (SH,), "ssm_d": (SH,),
              "ssm_norm_w": (SI,), "ffn_norm_w": (D,), "ffn_conv_b": (2 * DFF,), "final_norm_w": (D,)}
REPL_ORDER = [n for n in WEIGHTS if n in REPLICATED]
BIG = [("w_proj_a", GW // NDEV, False), ("w_proj_b", SI // NDEV, False), ("w_out", D // NDEV, False),
       ("ffn_w_up", 2 * DFF // NDEV, True), ("ffn_w_down", DFF // NDEV, False), ("w_in", IN_COLS // NDEV, True)]
VECTORS = ["gate_bias", "ssm_conv_w", "ffn_conv_w"]


def _size(shape):
    n = 1
    for s in shape:
        n *= s
    return n


def _round_up(n, k):
    return (n + k - 1) // k * k


BIG_OFF = {}
_off = 0
for _n, _r, _t in BIG:
    BIG_OFF[_n] = _off
    _off += _r
BIG_USED = _off
BIG_ROWS = _round_up(BIG_USED, 2 * SUBLANE)
assert all(BIG_OFF[n] % (2 * SUBLANE) == 0 for n, _, _ in BIG)
VEC_SHAPE = {n: (SHARDED[n][0][0], SHARDED[n][0][1] // NDEV) for n in VECTORS}
VEC_ELEMS = sum(_size(VEC_SHAPE[n]) for n in VECTORS)
VEC_ROWS = _round_up(VEC_ELEMS, SUBLANE * LANE) // LANE
REPL_ELEMS = sum(_size(REPLICATED[n]) for n in REPL_ORDER)
REPL_ROWS = _round_up(REPL_ELEMS, NDEV * SUBLANE * LANE) // (NDEV * LANE)
SMALL_ROWS = VEC_ROWS + REPL_ROWS


def _pack_big(arrs, dtype):
    parts = [(arrs[n].T if t else arrs[n]).astype(dtype) for n, _, t in BIG]
    parts.append(jnp.zeros((BIG_ROWS - BIG_USED, D), dtype))
    return jnp.concatenate(parts, axis=0)


def _pack_vectors(arrs):
    flat = jnp.concatenate([arrs[n].reshape(-1) for n in VECTORS])
    return jnp.pad(flat, (0, VEC_ROWS * LANE - VEC_ELEMS)).reshape(VEC_ROWS, LANE)


def _unpack_vectors(flat):
    out, off = {}, 0
    v = flat.reshape(-1)
    for n in VECTORS:
        k = _size(VEC_SHAPE[n])
        out[n] = v[off:off + k].reshape((1,) + VEC_SHAPE[n])
        off += k
    return out


def _join_vectors(allv):
    out, off = {}, 0
    v = allv.reshape(NDEV, VEC_ROWS * LANE)
    for n in VECTORS:
        r, c = VEC_SHAPE[n]
        out[n] = v[:, off:off + r * c].reshape(NDEV, r, c).transpose(1, 0, 2).reshape(r, c * NDEV)
        off += r * c
    return out


def _split_vectors(grads):
    parts = []
    for n in VECTORS:
        r, c = VEC_SHAPE[n]
        parts.append(grads[n].reshape(r, NDEV, c).transpose(1, 0, 2).reshape(NDEV, r * c))
    v = jnp.concatenate(parts, axis=1)
    return jnp.pad(v, ((0, 0), (0, VEC_ROWS * LANE - VEC_ELEMS))).reshape(NDEV, VEC_ROWS, LANE)


def _pack_repl(arrs):
    flat = jnp.concatenate([arrs[n].reshape(-1) for n in REPL_ORDER])
    return jnp.pad(flat, (0, NDEV * REPL_ROWS * LANE - REPL_ELEMS)).reshape(NDEV * REPL_ROWS, LANE)


def _unpack_repl(flat, shapes):
    out, off = {}, 0
    v = flat.reshape(-1)
    for n in REPL_ORDER:
        k = _size(REPLICATED[n])
        out[n] = v[off:off + k].reshape(shapes[n])
        off += k
    return out


def kernel(x, mix_norm_w, w_in, gate_bias, gmlp_ln_w, gmlp_ln_b, gmlp_ws, gmlp_bs, ssm_conv_w, ssm_conv_b, ssm_dt_bias, ssm_a_log, ssm_d, ssm_norm_w, w_proj_a, w_proj_b, w_out, ffn_norm_w, ffn_w_up, ffn_conv_w, ffn_conv_b, ffn_w_down, final_norm_w, loss_target, m_mix_norm_w, m_w_in, m_gate_bias, m_gmlp_ln_w, m_gmlp_ln_b, m_gmlp_ws, m_gmlp_bs, m_ssm_conv_w, m_ssm_conv_b, m_ssm_dt_bias, m_ssm_a_log, m_ssm_d, m_ssm_norm_w, m_w_proj_a, m_w_proj_b, m_w_out, m_ffn_norm_w, m_ffn_w_up, m_ffn_conv_w, m_ffn_conv_b, m_ffn_w_down, m_final_norm_w, v_mix_norm_w, v_w_in, v_gate_bias, v_gmlp_ln_w, v_gmlp_ln_b, v_gmlp_ws, v_gmlp_bs, v_ssm_conv_w, v_ssm_conv_b, v_ssm_dt_bias, v_ssm_a_log, v_ssm_d, v_ssm_norm_w, v_w_proj_a, v_w_proj_b, v_w_out, v_ffn_norm_w, v_ffn_w_up, v_ffn_conv_w, v_ffn_conv_b, v_ffn_w_down, v_final_norm_w):
    given = dict(locals())
    wts = {n: given[n] for n in WEIGHTS}
    mom = {n: given["m_" + n] for n in WEIGHTS}
    var = {n: given["v_" + n] for n in WEIGHTS}
    xi, yi, ci = _axes()
    c_idx = jnp.reshape(ci, (1,)).astype(jnp.int32)
    q_idx = jnp.reshape(2 * xi + yi, (1,)).astype(jnp.int32)
    big_names = [n for n, _, _ in BIG]
    drop = lambda d, names: {n: d[n][0] for n in names}

    all_big, all_vec = _all_gather([_pack_big(drop(wts, big_names), BF16), _pack_vectors(wts)],
                                   name="weights_all_gather")
    full = {}
    for n, r, t in BIG:
        full[n + "_t" if t else n] = all_big[:, BIG_OFF[n]:BIG_OFF[n] + r].reshape(NDEV * r, D)
    full.update(_join_vectors(all_vec))
    for n in REPL_ORDER:
        full[n] = wts[n].reshape(REPLICATED[n])

    loss_local, grad_x, grads = _local_step(x[0], loss_target[0], full)

    send_big = jnp.concatenate(
        [grads[n + "_t" if t else n].reshape(NDEV, r, D) for n, r, t in BIG]
        + [jnp.zeros((NDEV, BIG_ROWS - BIG_USED, D), F32)], axis=1).astype(BF16)
    send_small = jnp.concatenate([_split_vectors(grads), _pack_repl(grads).reshape(NDEV, REPL_ROWS, LANE)], axis=1)
    sib_big, sib_small = _exchange([send_big, send_small], _to_sibling_plan, name="grads_to_sibling")
    sum_big = _pair_add(send_big.reshape(4, 2, BIG_ROWS, D), sib_big, c_idx, name="grad_pair_add")
    sum_small = _pair_add(send_small.reshape(4, 2, SMALL_ROWS, LANE), sib_small, c_idx, name="grad_pair_add_small")
    chips_big, chips_small = _exchange([sum_big, sum_small], _to_chips_plan, name="grads_to_chips")
    g_big = _grad_sum(sum_big, chips_big, q_idx, name="grad_sum")
    g_small = _grad_sum(sum_small, chips_small, q_idx, name="grad_sum_small")

    outs = {}
    for n, r, t in BIG:
        g = g_big[BIG_OFF[n]:BIG_OFF[n] + r]
        g = g.T if t else g
        outs[n] = tuple(a[None] for a in (g,) + tuple(_adam(g, wts[n][0], mom[n][0], var[n][0], name="adam_" + n)))
    g_vec = g_small[:VEC_ROWS]
    vec_out = (g_vec,) + tuple(_adam(g_vec, _pack_vectors(wts), _pack_vectors(mom), _pack_vectors(var),
                                     name="adam_vectors"))
    vec_out = [_unpack_vectors(a) for a in vec_out]
    for n in VECTORS:
        outs[n] = tuple(a[n] for a in vec_out)

    g_repl = _all_gather([g_small[VEC_ROWS:]], name="replicated_grads_all_gather")[0].reshape(NDEV * REPL_ROWS, LANE)
    repl_out = (g_repl,) + tuple(_adam(g_repl, _pack_repl(wts), _pack_repl(mom), _pack_repl(var),
                                       name="adam_replicated"))
    shapes = {n: wts[n].shape for n in WEIGHTS}
    repl_out = [_unpack_repl(a, shapes) for a in repl_out]
    for n in REPL_ORDER:
        outs[n] = tuple(a[n] for a in repl_out)

    loss = lax.psum(loss_local, ("x", "y", "c"))
    return (loss, grad_x[None]) + tuple(outs[n][k] for k in range(4) for n in WEIGHTS)
```

```python
import functools

import jax
import jax.numpy as jnp
from jax import lax
from jax.experimental import pallas as pl
from jax.experimental.pallas import tpu as pltpu

F32 = jnp.float32
BF16 = jnp.bfloat16

D = 1024
EPS = 1e-5
GW = 1024
GB = 128
GG = 8
GD = 128
GCH = 64
SI = 2048
SH = 32
SP = 64
SG = 4
SN = 128
SGW = SI // SG
SK = 4
SXBC = SI + 2 * SG * SN
DFF = 2816
FK = 3
PMAIN = 2 * D + 2 * GW + SI + SXBC
IN_COLS = PMAIN + SH
NDEV = 8
ADAM_LR, ADAM_B1, ADAM_B2, ADAM_EPS, ADAM_WD, ADAM_STEP = 0.001, 0.9, 0.999, 1e-08, 0.01, 10

LANE = 128
SUBLANE = 8
VMEM_MB_V7X = 64
VMEM_CAP_MB = 56

LS = 128
FT = DFF // 2

NN = (((1,), (0,)), ((), ()))
NT = (((1,), (1,)), ((), ()))
TN = (((0,), (0,)), ((), ()))


def _params(sem, vmem_mb):
    return pltpu.CompilerParams(dimension_semantics=sem,
                                vmem_limit_bytes=min(int(vmem_mb), VMEM_CAP_MB) * 1024 * 1024)


def _dot(a, b, dims=NN):
    return lax.dot_general(a, b, dims, preferred_element_type=F32)


def _sigmoid(x):
    return 1.0 / (1.0 + jnp.exp(-x))


def _split3(v):
    hi = v.astype(BF16)
    r = v - hi.astype(F32)
    mid = r.astype(BF16)
    lo = (r - mid.astype(F32)).astype(BF16)
    return hi, mid, lo


def _dot3(a_f32, b_bf16, dims):
    hi, mid, lo = _split3(a_f32)
    return _dot(hi, b_bf16, dims) + _dot(mid, b_bf16, dims) + _dot(lo, b_bf16, dims)


def _dot3_rhs(a_bf16, b_f32, dims):
    hi, mid, lo = _split3(b_f32)
    return _dot(a_bf16, hi, dims) + _dot(a_bf16, mid, dims) + _dot(a_bf16, lo, dims)


def _matmul(a, b, *, name, out_dtype, ta=False, tb=False, tm=1024, tn=1024, tk=1024, add=None,
            j_outer=False, b_rows=None):
    if ta:
        K, M = a.shape
    else:
        M, K = a.shape
    if tb:
        N, K2 = b.shape
        N = b_rows or N
    else:
        K2, N = b.shape
        K2 = b_rows or K2
    assert K == K2, (a.shape, b.shape, ta, tb)
    tm, tn, tk = min(tm, M), min(tn, N), min(tk, K)
    assert M % tm == 0 and N % tn == 0 and K % tk == 0, (M, N, K, tm, tn, tk)
    nk = K // tk
    dims = (((0 if ta else 1,), (1 if tb else 0,)), ((), ()))
    has_add = add is not None

    def body(*refs):
        if has_add:
            a_ref, b_ref, add_ref, o_ref = refs[:4]
        else:
            a_ref, b_ref, o_ref = refs[:3]
            add_ref = None
        p = lax.dot_general(a_ref[...].astype(BF16), b_ref[...].astype(BF16), dims,
                            preferred_element_type=F32)

        def finish(acc):
            if has_add:
                acc = acc + add_ref[...].astype(F32)
            o_ref[...] = acc.astype(o_ref.dtype)

        if nk == 1:
            finish(p)
        else:
            acc_ref = refs[-1]
            k = pl.program_id(2)

            @pl.when(k == 0)
            def _():
                acc_ref[...] = p

            @pl.when(jnp.logical_and(k > 0, k < nk - 1))
            def _():
                acc_ref[...] += p

            @pl.when(k == nk - 1)
            def _():
                finish(acc_ref[...] + p)

    if j_outer:
        ij = lambda g0, g1: (g1, g0)
        grid = (N // tn, M // tm, nk)
    else:
        ij = lambda g0, g1: (g0, g1)
        grid = (M // tm, N // tn, nk)

    def a_map(g0, g1, k):
        i, _ = ij(g0, g1)
        return (k, i) if ta else (i, k)

    def b_map(g0, g1, k):
        _, j = ij(g0, g1)
        return (j, k) if tb else (k, j)

    def o_map(g0, g1, k):
        return ij(g0, g1)

    in_specs = [pl.BlockSpec((tk, tm) if ta else (tm, tk), a_map),
                pl.BlockSpec((tn, tk) if tb else (tk, tn), b_map)]
    args = [a, b]
    if has_add:
        in_specs.append(pl.BlockSpec((tm, tn), o_map))
        args.append(add)
    scratch = [pltpu.VMEM((tm, tn), F32)] if nk > 1 else []
    osz = jnp.dtype(out_dtype).itemsize
    est = (2 * (tm * tk * a.dtype.itemsize + tk * tn * b.dtype.itemsize) + 2 * tm * tn * osz
           + (2 * tm * tn * add.dtype.itemsize if has_add else 0)
           + 3 * tm * tn * 4 + 2 * (tm * tk + tk * tn)) / 2 ** 20 + 4
    return pl.pallas_call(
        body, grid=grid, in_specs=in_specs, out_specs=pl.BlockSpec((tm, tn), o_map),
        out_shape=jax.ShapeDtypeStruct((M, N), out_dtype), scratch_shapes=scratch, name=name,
        compiler_params=_params(("arbitrary", "arbitrary", "arbitrary"), est))(*args)


def _rms_fwd(x, w, *, name):
    T = x.shape[0]
    tm = min(512, T)

    def body(x_ref, w_ref, o_ref):
        xv = x_ref[...]
        r = lax.rsqrt(jnp.mean(xv * xv, axis=-1, keepdims=True) + EPS)
        o_ref[...] = (xv * r * w_ref[...]).astype(BF16)

    return pl.pallas_call(
        body, grid=(T // tm,),
        in_specs=[pl.BlockSpec((tm, D), lambda i: (i, 0)), pl.BlockSpec((1, D), lambda i: (0, 0))],
        out_specs=pl.BlockSpec((tm, D), lambda i: (i, 0)),
        out_shape=jax.ShapeDtypeStruct((T, D), BF16), name=name,
        compiler_params=_params(("arbitrary",), 24))(x, w)


def _rms_bwd(x, w, dy, dres, *, name):
    T = x.shape[0]
    tm = min(512, T)

    def body(x_ref, w_ref, dy_ref, dres_ref, dx_ref, dw_ref):
        xv = x_ref[...]
        r = lax.rsqrt(jnp.mean(xv * xv, axis=-1, keepdims=True) + EPS)
        xhat = xv * r
        dyv = dy_ref[...].astype(F32)
        g = dyv * w_ref[...]
        dx_ref[...] = dres_ref[...] + r * (g - xhat * jnp.mean(g * xhat, axis=-1, keepdims=True))
        part = jnp.sum(dyv * xhat, axis=0, keepdims=True)

        @pl.when(pl.program_id(0) == 0)
        def _():
            dw_ref[...] = part

        @pl.when(pl.program_id(0) > 0)
        def _():
            dw_ref[...] += part

    row = pl.BlockSpec((tm, D), lambda i: (i, 0))
    vec = pl.BlockSpec((1, D), lambda i: (0, 0))
    return pl.pallas_call(
        body, grid=(T // tm,), in_specs=[row, vec, row, row], out_specs=[row, vec],
        out_shape=[jax.ShapeDtypeStruct((T, D), F32), jax.ShapeDtypeStruct((1, D), F32)], name=name,
        compiler_params=_params(("arbitrary",), 32))(x, w, dy, dres)


def _loss_head(h, tgt, w):
    T = h.shape[0]
    tm = min(512, T)

    def body(h_ref, t_ref, w_ref, loss_ref, dh_ref, dw_ref):
        hv = h_ref[...]
        r = lax.rsqrt(jnp.mean(hv * hv, axis=-1, keepdims=True) + EPS)
        xhat = hv * r
        wv = w_ref[...]
        err = xhat * wv - t_ref[...]
        lpart = 0.5 * jnp.sum(jnp.mean(err * err, axis=-1, keepdims=True), axis=0, keepdims=True)
        dy = err * (1.0 / D)
        g = dy * wv
        dh_ref[...] = r * (g - xhat * jnp.mean(g * xhat, axis=-1, keepdims=True))
        wpart = jnp.sum(dy * xhat, axis=0, keepdims=True)
        lrow = jnp.broadcast_to(lpart, (1, LANE))

        @pl.when(pl.program_id(0) == 0)
        def _():
            dw_ref[...] = wpart
            loss_ref[...] = lrow

        @pl.when(pl.program_id(0) > 0)
        def _():
            dw_ref[...] += wpart
            loss_ref[...] += lrow

    row = pl.BlockSpec((tm, D), lambda i: (i, 0))
    vec = pl.BlockSpec((1, D), lambda i: (0, 0))
    return pl.pallas_call(
        body, grid=(T // tm,), in_specs=[row, row, vec],
        out_specs=[pl.BlockSpec((1, LANE), lambda i: (0, 0)), row, vec],
        out_shape=[jax.ShapeDtypeStruct((1, LANE), F32), jax.ShapeDtypeStruct((T, D), F32),
                   jax.ShapeDtypeStruct((1, D), F32)], name="loss_head",
        compiler_params=_params(("arbitrary",), 32))(h, tgt, w)


_GELU_C = 0.7978845608028654
_GELU_A = 0.044715


def _gelu(x):
    t = jnp.tanh(_GELU_C * (x + _GELU_A * x * x * x))
    return 0.5 * x * (1.0 + t), t


def _gelu_grad(x, t):
    return 0.5 * (1.0 + t) + 0.5 * x * (1.0 - t * t) * _GELU_C * (1.0 + 3.0 * _GELU_A * x * x)


def _gmlp_mask():
    r = lax.broadcasted_iota(jnp.int32, (GB, GB), 0) // GCH
    c = lax.broadcasted_iota(jnp.int32, (GB, GB), 1) // GCH
    return c <= r


def _gmlp_fwd(proj, lnw, lnb, ws, bst):
    T = proj.shape[0]
    tm = min(512, T)
    nblk = tm // GB

    def body(u_ref, v_ref, lnw_ref, lnb_ref, ws_ref, bst_ref, o_ref):
        mask = _gmlp_mask()
        u, _ = _gelu(u_ref[...].astype(F32))
        v, _ = _gelu(v_ref[...].astype(F32))
        for g in range(GG):
            cs = slice(g * GD, (g + 1) * GD)
            vg = v[:, cs]
            mu = jnp.mean(vg, axis=-1, keepdims=True)
            vc = vg - mu
            var = jnp.mean(vc * vc, axis=-1, keepdims=True)
            vn = (vc * lax.rsqrt(var + EPS) * lnw_ref[g:g + 1, :] + lnb_ref[g:g + 1, :]).astype(BF16)
            wsg = jnp.where(mask, ws_ref[g], 0.0).astype(BF16)
            bcol = bst_ref[:, g:g + 1]
            for blk in range(nblk):
                rs = slice(blk * GB, (blk + 1) * GB)
                sv = _dot(wsg, vn[rs, :]) + bcol
                o_ref[rs, cs] = (u[rs, cs] * sv).astype(BF16)

    full = lambda shape: pl.BlockSpec(shape, lambda i: tuple(0 for _ in shape))
    return pl.pallas_call(
        body, grid=(T // tm,),
        in_specs=[pl.BlockSpec((tm, GW), lambda i: (i, 2)), pl.BlockSpec((tm, GW), lambda i: (i, 3)),
                  full((GG, GD)), full((GG, GD)), full((GG, GB, GB)), full((GB, GG))],
        out_specs=pl.BlockSpec((tm, GW), lambda i: (i, 0)),
        out_shape=jax.ShapeDtypeStruct((T, GW), BF16), name="gmlp_fwd",
        compiler_params=_params(("arbitrary",), 40))(proj, proj, lnw, lnb, ws, bst)


def _gmlp_bwd(proj, dya, dproj, lnw, lnb, ws, bst):
    T = proj.shape[0]
    tm = min(512, T)
    nblk = tm // GB

    def body(u_ref, v_ref, dya_ref, dproj_in, lnw_ref, lnb_ref, ws_ref, bst_ref,
             dz_ref, dlnw_ref, dlnb_ref, dws_ref, dbst_ref):
        del dproj_in
        first = pl.program_id(0) == 0

        @pl.when(first)
        def _():
            dlnw_ref[...] = jnp.zeros_like(dlnw_ref)
            dlnb_ref[...] = jnp.zeros_like(dlnb_ref)
            dws_ref[...] = jnp.zeros_like(dws_ref)
            dbst_ref[...] = jnp.zeros_like(dbst_ref)

        mask = _gmlp_mask()
        lane = lax.broadcasted_iota(jnp.int32, (GB, LANE), 1)
        ur = u_ref[...].astype(F32)
        vr = v_ref[...].astype(F32)
        u, tu = _gelu(ur)
        v, tv = _gelu(vr)
        gu = _gelu_grad(ur, tu)
        gv = _gelu_grad(vr, tv)
        dy = dya_ref[...].astype(F32)
        dbst = jnp.zeros((GB, LANE), F32)
        dlnw_rows, dlnb_rows = [], []
        for g in range(GG):
            cs = slice(g * GD, (g + 1) * GD)
            vg = v[:, cs]
            mu = jnp.mean(vg, axis=-1, keepdims=True)
            vc = vg - mu
            var = jnp.mean(vc * vc, axis=-1, keepdims=True)
            rstd = lax.rsqrt(var + EPS)
            xhat = vc * rstd
            lw = lnw_ref[g:g + 1, :]
            vn = (xhat * lw + lnb_ref[g:g + 1, :]).astype(BF16)
            wsg = jnp.where(mask, ws_ref[g], 0.0).astype(BF16)
            bcol = bst_ref[:, g:g + 1]
            dyg = dy[:, cs]
            ug = u[:, cs]
            dsv = dyg * ug
            dsv_b = dsv.astype(BF16)
            dws_g = jnp.zeros((GB, GB), F32)
            bsum = jnp.zeros((GB, 1), F32)
            dvn_parts = []
            for blk in range(nblk):
                rs = slice(blk * GB, (blk + 1) * GB)
                sv = _dot(wsg, vn[rs, :]) + bcol
                dz_ref[rs, cs] = (dyg[rs, :] * sv * gu[rs, cs]).astype(BF16)
                dws_g = dws_g + _dot(dsv_b[rs, :], vn[rs, :], NT)
                bsum = bsum + jnp.sum(dsv[rs, :], axis=-1, keepdims=True)
                dvn_parts.append(_dot(wsg, dsv_b[rs, :], TN))
            dvn = jnp.concatenate(dvn_parts, axis=0)
            dws_ref[g] += jnp.where(mask, dws_g, 0.0)
            dbst = dbst + jnp.where(lane == g, bsum, 0.0)
            dlnw_rows.append(jnp.sum(dvn * xhat, axis=0, keepdims=True))
            dlnb_rows.append(jnp.sum(dvn, axis=0, keepdims=True))
            dxh = dvn * lw
            dvg = rstd * (dxh - jnp.mean(dxh, axis=-1, keepdims=True)
                          - xhat * jnp.mean(dxh * xhat, axis=-1, keepdims=True))
            dz_ref[:, GW + g * GD:GW + (g + 1) * GD] = (dvg * gv[:, cs]).astype(BF16)
        dlnw_ref[...] += jnp.concatenate(dlnw_rows, axis=0)
        dlnb_ref[...] += jnp.concatenate(dlnb_rows, axis=0)
        dbst_ref[...] += dbst

    full = lambda shape: pl.BlockSpec(shape, lambda i: tuple(0 for _ in shape))
    outs = pl.pallas_call(
        body, grid=(T // tm,),
        in_specs=[pl.BlockSpec((tm, GW), lambda i: (i, 2)), pl.BlockSpec((tm, GW), lambda i: (i, 3)),
                  pl.BlockSpec((tm, GW), lambda i: (i, 0)), pl.BlockSpec(memory_space=pl.ANY),
                  full((GG, GD)), full((GG, GD)), full((GG, GB, GB)), full((GB, GG))],
        out_specs=[pl.BlockSpec((tm, 2 * GW), lambda i: (i, 1)), full((GG, GD)), full((GG, GD)),
                   full((GG, GB, GB)), full((GB, LANE))],
        out_shape=[jax.ShapeDtypeStruct(dproj.shape, dproj.dtype), jax.ShapeDtypeStruct((GG, GD), F32),
                   jax.ShapeDtypeStruct((GG, GD), F32), jax.ShapeDtypeStruct((GG, GB, GB), F32),
                   jax.ShapeDtypeStruct((GB, LANE), F32)],
        input_output_aliases={3: 0}, name="gmlp_bwd",
        compiler_params=_params(("arbitrary",), 48))(proj, proj, dya, dproj, lnw, lnb, ws, bst)
    return outs


def _merge_fwd(ya_pre, yb_pre, proj, bias, wpa, wpb):
    T = proj.shape[0]
    tm = min(512, T)

    def body(ya_ref, yb_ref, g_ref, b_ref, wpa_ref, wpb_ref, m_ref, oa_ref, ob_ref):
        ya = _dot(ya_ref[...], wpa_ref[...])
        yb = _dot(yb_ref[...], wpb_ref[...])
        g = g_ref[...].astype(F32)
        sa = _sigmoid(g[:, :D] + b_ref[0:1, :])
        sb = _sigmoid(g[:, D:] + b_ref[1:2, :])
        m_ref[...] = (sa * ya + sb * yb).astype(BF16)
        oa_ref[...] = ya.astype(BF16)
        ob_ref[...] = yb.astype(BF16)

    row = lambda w: pl.BlockSpec((tm, w), lambda i: (i, 0))
    full = lambda shape: pl.BlockSpec(shape, lambda i: tuple(0 for _ in shape))
    o = jax.ShapeDtypeStruct((T, D), BF16)
    return pl.pallas_call(
        body, grid=(T // tm,),
        in_specs=[row(GW), row(SI), row(2 * D), full((2, D)), full((GW, D)), full((SI, D))],
        out_specs=[row(D), row(D), row(D)], out_shape=[o, o, o], name="merge_fwd",
        compiler_params=_params(("arbitrary",), 40))(ya_pre, yb_pre, proj, bias, wpa, wpb)


def _merge_bwd(dm, proj, bias, ya, yb, wpa, wpb):
    T = proj.shape[0]
    tm = min(512, T)

    def body(dm_ref, g_ref, b_ref, ya_ref, yb_ref, wpa_ref, wpb_ref,
             dg_ref, dya_ref, dyb_ref, dpa_ref, dpb_ref, db_ref):
        dmv = dm_ref[...].astype(F32)
        g = g_ref[...].astype(F32)
        sa = _sigmoid(g[:, :D] + b_ref[0:1, :])
        sb = _sigmoid(g[:, D:] + b_ref[1:2, :])
        dya = (dmv * sa).astype(BF16)
        dyb = (dmv * sb).astype(BF16)
        dga = dmv * ya_ref[...].astype(F32) * sa * (1.0 - sa)
        dgb = dmv * yb_ref[...].astype(F32) * sb * (1.0 - sb)
        dg_ref[:, :D] = dga.astype(BF16)
        dg_ref[:, D:] = dgb.astype(BF16)
        dya_ref[...] = dya
        dyb_ref[...] = dyb
        dpa_ref[...] = _dot(dya, wpa_ref[...], NT).astype(BF16)
        dpb_ref[...] = _dot(dyb, wpb_ref[...], NT).astype(BF16)
        part = jnp.concatenate([jnp.sum(dga, axis=0, keepdims=True), jnp.sum(dgb, axis=0, keepdims=True)], axis=0)

        @pl.when(pl.program_id(0) == 0)
        def _():
            db_ref[...] = part

        @pl.when(pl.program_id(0) > 0)
        def _():
            db_ref[...] += part

    row = lambda w: pl.BlockSpec((tm, w), lambda i: (i, 0))
    full = lambda shape: pl.BlockSpec(shape, lambda i: tuple(0 for _ in shape))
    o = lambda w: jax.ShapeDtypeStruct((T, w), BF16)
    return pl.pallas_call(
        body, grid=(T // tm,),
        in_specs=[row(D), row(2 * D), full((2, D)), row(D), row(D), full((GW, D)), full((SI, D))],
        out_specs=[row(2 * D), row(D), row(D), row(GW), row(SI), full((2, D))],
        out_shape=[o(PMAIN), o(D), o(D), o(GW), o(SI), jax.ShapeDtypeStruct((2, D), F32)], name="merge_bwd",
        compiler_params=_params(("arbitrary",), 48))(dm, proj, bias, ya, yb, wpa, wpb)


STRIP = 2 * SUBLANE


def _shift_down(cur, prev, k):
    ext = jnp.concatenate([prev, cur], axis=0)
    return ext[SUBLANE - k:SUBLANE - k + cur.shape[0]]


def _shift_up(cur, nxt, k):
    ext = jnp.concatenate([cur, nxt], axis=0)
    return ext[k:k + cur.shape[0]]


def _fold(a):
    return a[:SUBLANE] + a[SUBLANE:]


def _ffn_conv(x, prev, w, b):
    return b + w[0] * _shift_down(x, prev, 2) + w[1] * _shift_down(x, prev, 1) + w[2] * x


def _ffn_act_fwd(up, cw, cb):
    T = up.shape[0]
    tm = min(512, T)
    ns = tm // STRIP

    def body(up_ref, cw_ref, cb_ref, o_ref, halo):
        @pl.when(pl.program_id(1) == 0)
        def _():
            halo[...] = jnp.zeros_like(halo)

        bro = lambda r: jnp.broadcast_to(r, (STRIP, LANE))
        for cc in range(FT // LANE):
            gs = slice(cc * LANE, (cc + 1) * LANE)
            vs = slice(FT + cc * LANE, FT + (cc + 1) * LANE)
            wg = [bro(cw_ref[k:k + 1, gs]) for k in range(FK)]
            wv = [bro(cw_ref[k:k + 1, vs]) for k in range(FK)]
            bg, bv = bro(cb_ref[:, gs]), bro(cb_ref[:, vs])

            def strip(s, carry, gs=gs, vs=vs, wg=wg, wv=wv, bg=bg, bv=bv):
                pg, pv = carry
                rows = pl.ds(pl.multiple_of(s * STRIP, STRIP), STRIP)
                xg = up_ref[rows, gs].astype(F32)
                xv = up_ref[rows, vs].astype(F32)
                gate = _ffn_conv(xg, pg, wg, bg)
                val = _ffn_conv(xv, pv, wv, bv)
                o_ref[rows, gs] = (gate * _sigmoid(gate) * val).astype(BF16)
                return xg[STRIP - SUBLANE:], xv[STRIP - SUBLANE:]

            pg, pv = lax.fori_loop(0, ns, strip, (halo[:, gs], halo[:, vs]), unroll=4)
            halo[:, gs] = pg
            halo[:, vs] = pv

    return pl.pallas_call(
        body, grid=(2, T // tm),
        in_specs=[pl.BlockSpec((tm, 2 * FT), lambda j, i: (i, j)), pl.BlockSpec((FK, 2 * FT), lambda j, i: (0, j)),
                  pl.BlockSpec((1, 2 * FT), lambda j, i: (0, j))],
        out_specs=pl.BlockSpec((tm, FT), lambda j, i: (i, j)),
        out_shape=jax.ShapeDtypeStruct((T, DFF), BF16),
        scratch_shapes=[pltpu.VMEM((SUBLANE, 2 * FT), F32)], name="ffn_act_fwd",
        compiler_params=_params(("arbitrary", "arbitrary"), 32))(up, cw, cb)


def _ffn_act_bwd(up, dact, cw, cb):
    T = up.shape[0]
    tm = min(512, T)
    nt = T // tm
    ns = tm // STRIP
    hb = tm // SUBLANE

    def body(up_ref, halo_ref, da_ref, cw_ref, cb_ref, dup_ref, dcw_ref, dcb_ref, nxt):
        i = pl.program_id(1)
        ri = nt - 1 - i

        @pl.when(i == 0)
        def _():
            nxt[...] = jnp.zeros_like(nxt)
            dcw_ref[...] = jnp.zeros_like(dcw_ref)
            dcb_ref[...] = jnp.zeros_like(dcb_ref)

        bro = lambda r: jnp.broadcast_to(r, (STRIP, LANE))
        zero = jnp.zeros((SUBLANE, LANE), F32)
        for cc in range(FT // LANE):
            gs = slice(cc * LANE, (cc + 1) * LANE)
            vs = slice(FT + cc * LANE, FT + (cc + 1) * LANE)
            wg = [bro(cw_ref[k:k + 1, gs]) for k in range(FK)]
            wv = [bro(cw_ref[k:k + 1, vs]) for k in range(FK)]
            bg, bv = bro(cb_ref[:, gs]), bro(cb_ref[:, vs])
            halo_g = jnp.where(ri > 0, halo_ref[:, gs].astype(F32), 0.0)
            halo_v = jnp.where(ri > 0, halo_ref[:, vs].astype(F32), 0.0)

            def strip(t, carry, gs=gs, vs=vs, wg=wg, wv=wv, bg=bg, bv=bv, halo_g=halo_g, halo_v=halo_v):
                ng, nv, abg, abv, awg, awv = carry
                s = ns - 1 - t
                rows = pl.ds(pl.multiple_of(s * STRIP, STRIP), STRIP)
                before = pl.ds(pl.multiple_of(jnp.maximum(s - 1, 0) * STRIP, STRIP), STRIP)
                xg = up_ref[rows, gs].astype(F32)
                xv = up_ref[rows, vs].astype(F32)
                pg = jnp.where(s > 0, up_ref[before, gs].astype(F32)[STRIP - SUBLANE:], halo_g)
                pv = jnp.where(s > 0, up_ref[before, vs].astype(F32)[STRIP - SUBLANE:], halo_v)
                tg = [_shift_down(xg, pg, 2), _shift_down(xg, pg, 1), xg]
                tv = [_shift_down(xv, pv, 2), _shift_down(xv, pv, 1), xv]
                gate = bg + wg[0] * tg[0] + wg[1] * tg[1] + wg[2] * tg[2]
                val = bv + wv[0] * tv[0] + wv[1] * tv[1] + wv[2] * tv[2]
                sg = _sigmoid(gate)
                dav = da_ref[rows, gs].astype(F32)
                dgate = dav * val * sg * (1.0 + gate * (1.0 - sg))
                dval = dav * gate * sg
                abg = abg + _fold(dgate)
                abv = abv + _fold(dval)
                awg = [awg[k] + _fold(dgate * tg[k]) for k in range(FK)]
                awv = [awv[k] + _fold(dval * tv[k]) for k in range(FK)]
                dup_ref[rows, gs] = (wg[2] * dgate + wg[1] * _shift_up(dgate, ng, 1)
                                     + wg[0] * _shift_up(dgate, ng, 2)).astype(BF16)
                dup_ref[rows, vs] = (wv[2] * dval + wv[1] * _shift_up(dval, nv, 1)
                                     + wv[0] * _shift_up(dval, nv, 2)).astype(BF16)
                return dgate[:SUBLANE], dval[:SUBLANE], abg, abv, awg, awv

            init = (nxt[:, gs], nxt[:, vs], zero, zero, [zero] * FK, [zero] * FK)
            ng, nv, abg, abv, awg, awv = lax.fori_loop(0, ns, strip, init)
            nxt[:, gs] = ng
            nxt[:, vs] = nv
            dcb_ref[:, gs] += jnp.sum(abg, axis=0, keepdims=True)
            dcb_ref[:, vs] += jnp.sum(abv, axis=0, keepdims=True)
            for k in range(FK):
                dcw_ref[k:k + 1, gs] += jnp.sum(awg[k], axis=0, keepdims=True)
                dcw_ref[k:k + 1, vs] += jnp.sum(awv[k], axis=0, keepdims=True)

    return pl.pallas_call(
        body, grid=(2, nt),
        in_specs=[pl.BlockSpec((tm, 2 * FT), lambda j, i: (nt - 1 - i, j)),
                  pl.BlockSpec((SUBLANE, 2 * FT), lambda j, i: (jnp.maximum((nt - 1 - i) * hb - 1, 0), j)),
                  pl.BlockSpec((tm, FT), lambda j, i: (nt - 1 - i, j)),
                  pl.BlockSpec((FK, 2 * FT), lambda j, i: (0, j)), pl.BlockSpec((1, 2 * FT), lambda j, i: (0, j))],
        out_specs=[pl.BlockSpec((tm, 2 * FT), lambda j, i: (nt - 1 - i, j)),
                   pl.BlockSpec((FK, 2 * FT), lambda j, i: (0, j)), pl.BlockSpec((1, 2 * FT), lambda j, i: (0, j))],
        out_shape=[jax.ShapeDtypeStruct((T, 2 * DFF), BF16), jax.ShapeDtypeStruct((FK, 2 * DFF), F32),
                   jax.ShapeDtypeStruct((1, 2 * DFF), F32)],
        scratch_shapes=[pltpu.VMEM((SUBLANE, 2 * FT), F32)], name="ffn_act_bwd",
        compiler_params=_params(("arbitrary", "arbitrary"), 32))(up, up, dact, cw, cb)


def _softplus(x):
    e = jnp.exp(-jnp.abs(x))
    return jnp.maximum(x, 0.0) + jnp.where(e < 1e-4, e * (1.0 - 0.5 * e), jnp.log(1.0 + e))


def _ssd_consts():
    li = lax.broadcasted_iota(jnp.int32, (LS, LS), 0)
    si = lax.broadcasted_iota(jnp.int32, (LS, LS), 1)
    tril = si <= li
    hh = lax.broadcasted_iota(jnp.int32, (LANE, SI), 0)
    cc = lax.broadcasted_iota(jnp.int32, (LANE, SI), 1) // SP
    expand = jnp.where(hh == cc, 1.0, 0.0).astype(BF16)
    return tril, expand


def _ssd_pre(ext, cw_ref, cb_ref, dt_ref, dtb_ref, alog_ref, tril, expand):
    H = SUBLANE
    taps = [ext[H - (SK - 1) + k:H - (SK - 1) + k + LS, :] for k in range(SK)]
    xc = cb_ref[...]
    for k in range(SK):
        xc = xc + cw_ref[k:k + 1, :] * taps[k]
    sx = _sigmoid(xc)
    xbc = xc * sx
    xs, bm, cm = xbc[:, :SI], xbc[:, SI:SI + SG * SN], xbc[:, SI + SG * SN:]
    dtin = dt_ref[...] + dtb_ref[...]
    dt = _softplus(dtin)
    a_neg = -jnp.exp(alog_ref[...])
    dta = dt * a_neg
    trilb = jnp.where(tril, 1.0, 0.0).astype(BF16)
    a = _dot3_rhs(trilb, dta, NN)
    a_exp = _dot3(a, expand, NN)
    dt_exp = _dot3(dt, expand, NN)
    xdt = xs * dt_exp
    a_last = a_exp[LS - 1:LS, :]
    return dict(taps=taps, xc=xc, sx=sx, xs=xs, bm=bm, cm=cm, dtin=dt_ref[...] + dtb_ref[...], dt=dt, a_neg=a_neg,
                a=a, a_t=a.T, a_exp=a_exp, dt_exp=dt_exp, xdt=xdt, ea=jnp.exp(a_exp),
                w=jnp.exp(a_last - a_exp), eal=jnp.exp(a_last))


def _head_decay(pre, tril, h):
    seg = pre["a"][:, h:h + 1] - pre["a_t"][h:h + 1, :]
    return jnp.exp(jnp.where(tril, seg, -1e30))


def _ssd_fwd(proj, dtraw, cw, cb, dtb, alog, dexp, nw):
    T = proj.shape[0]
    nc = T // LS
    H = SUBLANE

    def body(z_ref, x_ref, dt_ref, cw_ref, cb_ref, dtb_ref, alog_ref, dexp_ref, nw_ref,
             yb_ref, y_ref, sp_ref, ext, st):
        @pl.when(pl.program_id(0) == 0)
        def _():
            ext[0:H, :] = jnp.zeros((H, SXBC), F32)
            st[...] = jnp.zeros_like(st)

        ext[H:H + LS, :] = x_ref[...].astype(F32)
        tril, expand = _ssd_consts()
        pre = _ssd_pre(ext, cw_ref, cb_ref, dt_ref, dtb_ref, alog_ref, tril, expand)
        lane = lax.broadcasted_iota(jnp.int32, (LS, LANE), 1)
        lo = lane < SP
        zf = z_ref[...].astype(F32)
        siluz = zf * _sigmoid(zf)
        for g in range(SG):
            gs = slice(g * SGW, (g + 1) * SGW)
            bg = pre["bm"][:, g * SN:(g + 1) * SN].astype(BF16)
            cg = pre["cm"][:, g * SN:(g + 1) * SN].astype(BF16)
            gmat = _dot(cg, bg, NT)
            sg = st[g]
            sp_ref[0, g] = sg
            yoff = _dot(cg, sg.astype(BF16))
            parts = []
            for j in range(SGW // LANE):
                h0 = g * (SGW // SP) + 2 * j
                m0 = gmat * _head_decay(pre, tril, h0)
                m1 = gmat * _head_decay(pre, tril, h0 + 1)
                xp = pre["xdt"][:, g * SGW + j * LANE:g * SGW + (j + 1) * LANE]
                rhs = jnp.concatenate([jnp.where(lo, xp, 0.0), jnp.where(lo, 0.0, xp)], axis=0).astype(BF16)
                parts.append(_dot(jnp.concatenate([m0, m1], axis=1).astype(BF16), rhs))
            y = (jnp.concatenate(parts, axis=1) + pre["ea"][:, gs] * yoff + dexp_ref[:, gs] * pre["xs"][:, gs])
            st[g] = pre["eal"][:, gs] * sg + _dot(bg, (pre["w"][:, gs] * pre["xdt"][:, gs]).astype(BF16), TN)
            y_ref[:, gs] = y
            yg = y * siluz[:, gs]
            r = lax.rsqrt(jnp.mean(yg * yg, axis=-1, keepdims=True) + EPS)
            yb_ref[:, gs] = (yg * r * nw_ref[:, gs]).astype(BF16)
        ext[0:H, :] = ext[LS:LS + H, :]

    vec = lambda w: pl.BlockSpec((1, w), lambda c: (0, 0))
    return pl.pallas_call(
        body, grid=(nc,),
        in_specs=[pl.BlockSpec((LS, SI), lambda c: (c, 2)), pl.BlockSpec((LS, SXBC), lambda c: (c, 2)),
                  pl.BlockSpec((LS, LANE), lambda c: (c, 0)),
                  pl.BlockSpec((SK, SXBC), lambda c: (0, 0)), vec(SXBC), vec(LANE), vec(LANE), vec(SI), vec(SI)],
        out_specs=[pl.BlockSpec((LS, SI), lambda c: (c, 0)), pl.BlockSpec((LS, SI), lambda c: (c, 0)),
                   pl.BlockSpec((1, SG, SN, SGW), lambda c: (c, 0, 0, 0))],
        out_shape=[jax.ShapeDtypeStruct((T, SI), BF16), jax.ShapeDtypeStruct((T, SI), F32),
                   jax.ShapeDtypeStruct((nc, SG, SN, SGW), F32)],
        scratch_shapes=[pltpu.VMEM((LS + H, SXBC), F32), pltpu.VMEM((SG, SN, SGW), F32)], name="ssd_fwd",
        compiler_params=_params(("arbitrary",), VMEM_CAP_MB))(proj, proj, dtraw, cw, cb, dtb, alog, dexp, nw)


def _ssd_bwd(proj, dtraw, y, sprev, dyb, dproj, cw, cb, dtb, alog, dexp, nw):
    T = proj.shape[0]
    nc = T // LS
    H = SUBLANE
    hb = LS // H
    NJ = (SI + SXBC) // D
    J0 = (2 * D + 2 * GW) // D

    def body(z_ref, x_ref, halo_ref, dt_ref, y_ref, sp_ref, dyb_ref, dproj_in,
             cw_ref, cb_ref, dtb_ref, alog_ref, dexp_ref, nw_ref,
             dp_ref, ddt_ref, dcw_ref, dcb_ref, ddtb_ref, da_ref, dd_ref, dnw_ref,
             ext, extd, ds, stage):
        del dproj_in
        i = pl.program_id(0)
        j = pl.program_id(1)
        c = nc - 1 - i

        @pl.when(jnp.logical_and(i == 0, j == 0))
        def _():
            extd[LS:LS + H, :] = jnp.zeros((H, SXBC), F32)
            ds[...] = jnp.zeros_like(ds)
            for r in (dcw_ref, dcb_ref, ddtb_ref, da_ref, dd_ref, dnw_ref):
                r[...] = jnp.zeros_like(r)

        @pl.when(j == 0)
        def _():
            ext[0:H, :] = jnp.where(c > 0, halo_ref[...].astype(F32), 0.0)
            ext[H:H + LS, :] = x_ref[...].astype(F32)
            tril, expand = _ssd_consts()
            pre = _ssd_pre(ext, cw_ref, cb_ref, dt_ref, dtb_ref, alog_ref, tril, expand)
            lane = lax.broadcasted_iota(jnp.int32, (LS, LANE), 1)
            sub = lax.broadcasted_iota(jnp.int32, (LANE, LS), 0)
            rowi = lax.broadcasted_iota(jnp.int32, (LS, 1), 0)
            lo = lane < SP
            xs, xdt, ea, w, eal = pre["xs"], pre["xdt"], pre["ea"], pre["w"], pre["eal"]

            zf = z_ref[...].astype(F32)
            sz = _sigmoid(zf)
            siluz = zf * sz
            yv = y_ref[...]
            yg = yv * siluz
            dout = dyb_ref[...].astype(F32)
            dyg_parts, dnw_parts = [], []
            for g in range(SG):
                gs = slice(g * SGW, (g + 1) * SGW)
                ygg = yg[:, gs]
                r = lax.rsqrt(jnp.mean(ygg * ygg, axis=-1, keepdims=True) + EPS)
                yhat = ygg * r
                dn = dout[:, gs] * nw_ref[:, gs]
                dnw_parts.append(jnp.sum(dout[:, gs] * yhat, axis=0, keepdims=True))
                dyg_parts.append(r * (dn - yhat * jnp.mean(dn * yhat, axis=-1, keepdims=True)))
            dyg = jnp.concatenate(dyg_parts, axis=1)
            dnw_ref[...] += jnp.concatenate(dnw_parts, axis=1)
            dy = dyg * siluz
            stage[:, 0:SI] = (dyg * yv * sz * (1.0 + zf * (1.0 - sz))).astype(BF16)
            dd_ref[...] += jnp.sum(dy * xs, axis=0, keepdims=True)
            tt = ea * dy

            da_rows = jnp.zeros((LS, LANE), F32)
            da_cols = jnp.zeros((LANE, LS), F32)
            dxdt_parts, db_parts, dc_parts, daexp_parts = [], [], [], []
            for g in range(SG):
                gs = slice(g * SGW, (g + 1) * SGW)
                bg = pre["bm"][:, g * SN:(g + 1) * SN].astype(BF16)
                cg = pre["cm"][:, g * SN:(g + 1) * SN].astype(BF16)
                sg = sp_ref[0, g]
                sgb = sg.astype(BF16)
                dsg = ds[g]
                dsgb = dsg.astype(BF16)
                ttg = tt[:, gs].astype(BF16)
                yoff = _dot(cg, sgb)
                dc = _dot(ttg, sgb, NT)
                gmat = _dot(cg, bg, NT)
                dgm = jnp.zeros((LS, LS), F32)
                dxdt_pairs = []
                for jj in range(SGW // LANE):
                    h0 = g * (SGW // SP) + 2 * jj
                    ps = slice(g * SGW + jj * LANE, g * SGW + (jj + 1) * LANE)
                    l0 = _head_decay(pre, tril, h0)
                    l1 = _head_decay(pre, tril, h0 + 1)
                    m0 = gmat * l0
                    m1 = gmat * l1
                    dyp = dy[:, ps]
                    dy_lo = jnp.where(lo, dyp, 0.0).astype(BF16)
                    dy_hi = jnp.where(lo, 0.0, dyp).astype(BF16)
                    xpb = xdt[:, ps].astype(BF16)
                    dm0 = _dot(dy_lo, xpb, NT)
                    dm1 = _dot(dy_hi, xpb, NT)
                    q0 = dm0 * m0
                    q1 = dm1 * m1
                    da_rows = da_rows + jnp.where(lane == h0, jnp.sum(q0, axis=1, keepdims=True), 0.0)
                    da_rows = da_rows + jnp.where(lane == h0 + 1, jnp.sum(q1, axis=1, keepdims=True), 0.0)
                    da_cols = da_cols + jnp.where(sub == h0, jnp.sum(q0, axis=0, keepdims=True), 0.0)
                    da_cols = da_cols + jnp.where(sub == h0 + 1, jnp.sum(q1, axis=0, keepdims=True), 0.0)
                    dgm = dgm + dm0 * l0 + dm1 * l1
                    mcat = jnp.concatenate([m0, m1], axis=0).astype(BF16)
                    dycat = jnp.concatenate([dy_lo, dy_hi], axis=0)
                    dxdt_pairs.append(_dot(mcat, dycat, TN))
                dgb = dgm.astype(BF16)
                dc = dc + _dot(dgb, bg)
                db = _dot(dgb, cg, TN)
                zg = _dot(bg, dsgb)
                wg, xdtg = w[:, gs], xdt[:, gs]
                dxdt_g = jnp.concatenate(dxdt_pairs, axis=1) + wg * zg
                qg = zg * xdtg * wg
                last = (jnp.sum(qg, axis=0, keepdims=True)
                        + jnp.sum(dsg * sg, axis=0, keepdims=True) * eal[:, gs])
                daexp_parts.append(dy[:, gs] * ea[:, gs] * yoff - qg + jnp.where(rowi == LS - 1, last, 0.0))
                db = db + _dot((wg * xdtg).astype(BF16), dsgb, NT)
                ds[g] = eal[:, gs] * dsg + _dot(cg, ttg, TN)
                dxdt_parts.append(dxdt_g)
                db_parts.append(db)
                dc_parts.append(dc)
            dxdt = jnp.concatenate(dxdt_parts, axis=1)
            da_exp = jnp.concatenate(daexp_parts, axis=1)
            da = _dot3(da_exp, expand, NT) + da_rows - da_cols.T
            triub = jnp.where(tril, 1.0, 0.0).astype(BF16)
            ddta = _dot3_rhs(triub, da, TN)
            ddt = ddta * pre["a_neg"] + _dot3(dxdt * xs, expand, NT)
            da_ref[...] += jnp.sum(ddta * pre["dt"], axis=0, keepdims=True)
            ddt_raw = ddt * _sigmoid(pre["dtin"])
            ddt_ref[...] = ddt_raw
            ddtb_ref[...] += jnp.sum(ddt_raw, axis=0, keepdims=True)
            dxs = dexp_ref[...] * dy + dxdt * pre["dt_exp"]
            dxbc = jnp.concatenate([dxs] + db_parts + dc_parts, axis=1)
            sx, xc = pre["sx"], pre["xc"]
            dxc = dxbc * sx * (1.0 + xc * (1.0 - sx))
            dcb_ref[...] += jnp.sum(dxc, axis=0, keepdims=True)
            dcw_ref[...] += jnp.concatenate([jnp.sum(dxc * t, axis=0, keepdims=True) for t in pre["taps"]], axis=0)
            extd[0:LS, :] = dxc
            dxr = cw_ref[SK - 1:SK, :] * dxc
            for k in range(SK - 1):
                dxr = dxr + cw_ref[k:k + 1, :] * extd[SK - 1 - k:SK - 1 - k + LS, :]
            stage[:, SI:] = dxr.astype(BF16)
            extd[LS:LS + H, :] = dxc[0:H, :]

        dp_ref[...] = stage[:, pl.ds(pl.multiple_of(j * D, D), D)]

    vec = lambda w: pl.BlockSpec((1, w), lambda i, j: (0, 0))
    rev = lambda w, cb_: pl.BlockSpec((LS, w), lambda i, j: (nc - 1 - i, cb_))
    outs = pl.pallas_call(
        body, grid=(nc, NJ),
        in_specs=[rev(SI, 2), rev(SXBC, 2),
                  pl.BlockSpec((H, SXBC), lambda i, j: (jnp.maximum((nc - 1 - i) * hb - 1, 0), 2)),
                  rev(LANE, 0), rev(SI, 0),
                  pl.BlockSpec((1, SG, SN, SGW), lambda i, j: (nc - 1 - i, 0, 0, 0)),
                  rev(SI, 0), pl.BlockSpec(memory_space=pl.ANY),
                  pl.BlockSpec((SK, SXBC), lambda i, j: (0, 0)), vec(SXBC), vec(LANE), vec(LANE), vec(SI), vec(SI)],
        out_specs=[pl.BlockSpec((LS, D), lambda i, j: (nc - 1 - i, J0 + j)), rev(LANE, 0),
                   pl.BlockSpec((SK, SXBC), lambda i, j: (0, 0)), vec(SXBC), vec(LANE), vec(LANE), vec(SI), vec(SI)],
        out_shape=[jax.ShapeDtypeStruct(dproj.shape, dproj.dtype), jax.ShapeDtypeStruct((T, LANE), F32),
                   jax.ShapeDtypeStruct((SK, SXBC), F32), jax.ShapeDtypeStruct((1, SXBC), F32),
                   jax.ShapeDtypeStruct((1, LANE), F32), jax.ShapeDtypeStruct((1, LANE), F32),
                   jax.ShapeDtypeStruct((1, SI), F32), jax.ShapeDtypeStruct((1, SI), F32)],
        scratch_shapes=[pltpu.VMEM((LS + H, SXBC), F32), pltpu.VMEM((LS + H, SXBC), F32),
                        pltpu.VMEM((SG, SN, SGW), F32), pltpu.VMEM((LS, SI + SXBC), BF16)],
        input_output_aliases={7: 0}, name="ssd_bwd",
        compiler_params=_params(("arbitrary", "arbitrary"), VMEM_CAP_MB))(
            proj, proj, proj, dtraw, y, sprev, dyb, dproj, cw, cb, dtb, alog, dexp, nw)
    return outs


def _perm_ffn_cols(a):
    lead = a.shape[:-1]
    return a.reshape(lead + (2, 2, FT)).swapaxes(-3, -2).reshape(lead + (2 * DFF,))


def _perm_ffn_rows(a):
    return a.reshape((2, 2, FT) + a.shape[1:]).swapaxes(0, 1).reshape(a.shape)


def _pad_lanes(v, n=LANE):
    return jnp.pad(v, ((0, 0), (0, n - v.shape[-1])))


def _local_step(x, tgt, w):
    T = x.shape[0]
    win_t = w["w_in_t"]
    win_dt = jnp.pad(w["w_in_t"][PMAIN:], ((0, LANE - SH), (0, 0)))
    wup = _perm_ffn_rows(w["ffn_w_up_t"])
    fcw = _perm_ffn_cols(w["ffn_conv_w"])
    fcb = _perm_ffn_cols(w["ffn_conv_b"][None, :])
    mixw = w["mix_norm_w"][None, :]
    ffnw = w["ffn_norm_w"][None, :]
    finw = w["final_norm_w"][None, :]
    bst = w["gmlp_bs"].T
    scb = w["ssm_conv_b"][None, :]
    dtb = _pad_lanes(w["ssm_dt_bias"][None, :])
    alog = _pad_lanes(w["ssm_a_log"][None, :])
    dexp = jnp.repeat(w["ssm_d"], SP)[None, :]
    snw = w["ssm_norm_w"][None, :]

    xn = _rms_fwd(x, mixw, name="mix_norm")
    proj = _matmul(xn, win_t, name="in_proj", out_dtype=BF16, tb=True, tn=1536, j_outer=True, b_rows=PMAIN)
    dtraw = _matmul(xn, win_dt, name="in_proj_dt", out_dtype=F32, tb=True)
    ya_pre = _gmlp_fwd(proj, w["gmlp_ln_w"], w["gmlp_ln_b"], w["gmlp_ws"], bst)
    yb_pre, y_ssd, sprev = _ssd_fwd(proj, dtraw, w["ssm_conv_w"], scb, dtb, alog, dexp, snw)
    merged, ya, yb = _merge_fwd(ya_pre, yb_pre, proj, w["gate_bias"], w["w_proj_a"], w["w_proj_b"])
    h1 = _matmul(merged, w["w_out"], name="out_proj", out_dtype=F32, add=x)
    hn = _rms_fwd(h1, ffnw, name="ffn_norm")
    up = _matmul(hn, wup, name="ffn_up", out_dtype=BF16, tb=True, tn=FT, j_outer=True)
    act = _ffn_act_fwd(up, fcw, fcb)
    h2 = _matmul(act, w["ffn_w_down"], name="ffn_down", out_dtype=F32, tk=FT, add=h1)

    loss_row, dh2, d_finw = _loss_head(h2, tgt, finw)
    dact = _matmul(dh2, w["ffn_w_down"], name="ffn_down_dx", out_dtype=BF16, tb=True, tn=FT)
    d_wdown = _matmul(act, dh2, name="ffn_down_dw", out_dtype=F32, ta=True, tm=FT)
    dup, d_fcw, d_fcb = _ffn_act_bwd(up, dact, fcw, fcb)
    dhn = _matmul(dup, wup, name="ffn_up_dx", out_dtype=F32, tk=FT)
    d_wup = _matmul(dup, hn, name="ffn_up_dw", out_dtype=F32, ta=True, tm=FT)
    dh1, d_ffnw = _rms_bwd(h1, ffnw, dhn, dh2, name="ffn_norm_bwd")
    dmerged = _matmul(dh1, w["w_out"], name="out_proj_dx", out_dtype=BF16, tb=True)
    d_wout = _matmul(merged, dh1, name="out_proj_dw", out_dtype=F32, ta=True)
    dproj, dya, dyb, dya_pre, dyb_pre, d_gbias = _merge_bwd(dmerged, proj, w["gate_bias"], ya, yb,
                                                           w["w_proj_a"], w["w_proj_b"])
    d_wpa = _matmul(ya_pre, dya, name="proj_a_dw", out_dtype=F32, ta=True)
    d_wpb = _matmul(yb_pre, dyb, name="proj_b_dw", out_dtype=F32, ta=True)
    dproj, d_lnw, d_lnb, d_ws, d_bst = _gmlp_bwd(proj, dya_pre, dproj, w["gmlp_ln_w"], w["gmlp_ln_b"],
                                                 w["gmlp_ws"], bst)
    dproj, ddt, d_scw, d_scb, d_dtb, d_a, d_dch, d_snw = _ssd_bwd(
        proj, dtraw, y_ssd, sprev, dyb_pre, dproj, w["ssm_conv_w"], scb, dtb, alog, dexp, snw)
    dxn = _matmul(ddt, win_dt, name="in_proj_dt_dx", out_dtype=F32)
    dxn = _matmul(dproj, win_t, name="in_proj_dx", out_dtype=F32, tk=1536, add=dxn, b_rows=PMAIN)
    d_win_main = _matmul(dproj, xn, name="in_proj_dw", out_dtype=F32, ta=True, tm=1536)
    d_win_dt = _matmul(ddt, xn, name="in_proj_dt_dw", out_dtype=F32, ta=True)
    grad_x, d_mixw = _rms_bwd(x, mixw, dxn, dh1, name="mix_norm_bwd")

    a_neg = -jnp.exp(w["ssm_a_log"])
    grads = {
        "mix_norm_w": d_mixw[0],
        "w_in_t": jnp.concatenate([d_win_main, d_win_dt[:SH]], axis=0),
        "gate_bias": d_gbias,
        "gmlp_ln_w": d_lnw, "gmlp_ln_b": d_lnb, "gmlp_ws": d_ws, "gmlp_bs": d_bst[:, :GG].T,
        "ssm_conv_w": d_scw, "ssm_conv_b": d_scb[0],
        "ssm_dt_bias": d_dtb[0, :SH], "ssm_a_log": d_a[0, :SH] * a_neg,
        "ssm_d": d_dch.reshape(SH, SP).sum(axis=-1), "ssm_norm_w": d_snw[0],
        "w_proj_a": d_wpa, "w_proj_b": d_wpb, "w_out": d_wout,
        "ffn_norm_w": d_ffnw[0],
        "ffn_w_up_t": _perm_ffn_rows(d_wup), "ffn_conv_w": _perm_ffn_cols(d_fcw), "ffn_conv_b": _perm_ffn_cols(d_fcb)[0],
        "ffn_w_down": d_wdown, "final_norm_w": d_finw[0],
    }
    return loss_row[0, 0], grad_x, grads


MESH = pl.DeviceIdType.MESH
HBM_SPEC = pl.BlockSpec(memory_space=pltpu.HBM)


def _axes():
    return lax.axis_index("x"), lax.axis_index("y"), lax.axis_index("c")


def _all_gather(shards, *, name):
    na = len(shards)

    def body(*refs):
        x_refs, out_refs = refs[:na], refs[na:2 * na]
        send_sems, recv_sems, local_sems = refs[2 * na:]
        x, y, c = _axes()
        me, sibling = (x, y, c), (x, y, 1 - c)
        chips = [(1 - x, y), (x, 1 - y), (1 - x, 1 - y)]

        def slot(a, px, py, pc):
            return out_refs[a].at[4 * px + 2 * py + pc]

        def copy(a, k, block, to, src=None):
            return pltpu.make_async_remote_copy(
                src_ref=slot(a, *block) if src is None else src, dst_ref=slot(a, *block),
                send_sem=send_sems.at[7 * a + k], recv_sem=recv_sems.at[7 * a + k], device_id=to, device_id_type=MESH)

        mine = [pltpu.make_async_copy(x_refs[a], slot(a, *me), local_sems.at[a]) for a in range(na)]
        for cp in mine:
            cp.start()
        first = []
        for a in range(na):
            first.append(copy(a, 0, me, sibling, src=x_refs[a]))
            first += [copy(a, 1 + j, me, (*chip, c), src=x_refs[a]) for j, chip in enumerate(chips)]
        for cp in first:
            cp.start()
        passed = []
        for j, chip in enumerate(chips):
            for a in range(na):
                copy(a, 1 + j, (*chip, c), me).wait_recv()
                cp = copy(a, 4 + j, (*chip, c), sibling)
                cp.start()
                passed.append(cp)
        for a in range(na):
            copy(a, 0, sibling, me).wait_recv()
        for j, chip in enumerate(chips):
            for a in range(na):
                copy(a, 4 + j, (*chip, 1 - c), me).wait_recv()
        for cp in first + passed:
            cp.wait_send()
        for cp in mine:
            cp.wait()

    return pl.pallas_call(
        body, out_shape=[jax.ShapeDtypeStruct((NDEV,) + s.shape, s.dtype) for s in shards],
        in_specs=[HBM_SPEC] * na, out_specs=[HBM_SPEC] * na,
        scratch_shapes=[pltpu.SemaphoreType.DMA((7 * na,)), pltpu.SemaphoreType.DMA((7 * na,)),
                        pltpu.SemaphoreType.DMA((na,))],
        name=name)(*shards)


def _exchange(srcs, plan, *, name):
    na = len(srcs)
    n = len(plan(0, 0, 0))

    def body(*refs):
        src_refs, out_refs = refs[:na], refs[na:2 * na]
        send_sems, recv_sems = refs[2 * na:]
        x, y, c = _axes()
        copies = []
        for k, (slab, peer) in enumerate(plan(x, y, c)):
            for a in range(na):
                cp = pltpu.make_async_remote_copy(
                    src_ref=src_refs[a].at[slab], dst_ref=out_refs[a].at[k], send_sem=send_sems.at[n * a + k],
                    recv_sem=recv_sems.at[n * a + k], device_id=peer, device_id_type=MESH)
                cp.start()
                copies.append(cp)
        for cp in copies:
            cp.wait()

    return pl.pallas_call(
        body, out_shape=[jax.ShapeDtypeStruct((n,) + s.shape[1:], s.dtype) for s in srcs],
        in_specs=[HBM_SPEC] * na, out_specs=[HBM_SPEC] * na,
        scratch_shapes=[pltpu.SemaphoreType.DMA((n * na,)), pltpu.SemaphoreType.DMA((n * na,))], name=name)(*srcs)


def _to_sibling_plan(x, y, c):
    return [(2 * q + (1 - c), (x, y, 1 - c)) for q in range(4)]


def _to_chips_plan(x, y, c):
    q = 2 * x + y
    return [(q ^ 2, (1 - x, y, c)), (q ^ 1, (x, 1 - y, c)), (q ^ 3, (1 - x, 1 - y, c))]


def _row_tile(rows, row_bytes, budget=2 * 2 ** 20, align=2 * SUBLANE):
    best = None
    for d in range(align, rows + 1, align):
        if rows % d == 0 and d * row_bytes <= budget:
            best = d
    return best or rows


def _pair_add(g, ra, c_idx, *, name):
    _, _, R, C = g.shape
    tr = _row_tile(R, C * 4)

    def body(c_ref, g_ref, ra_ref, o_ref):
        del c_ref
        o_ref[...] = (g_ref[0].astype(F32) + ra_ref[...].astype(F32)).astype(o_ref.dtype)

    return pl.pallas_call(
        body,
        grid_spec=pltpu.PrefetchScalarGridSpec(
            num_scalar_prefetch=1, grid=(4, R // tr),
            in_specs=[pl.BlockSpec((1, 1, tr, C), lambda q, r, cr: (q, cr[0], r, 0)),
                      pl.BlockSpec((1, tr, C), lambda q, r, cr: (q, r, 0))],
            out_specs=pl.BlockSpec((1, tr, C), lambda q, r, cr: (q, r, 0))),
        out_shape=jax.ShapeDtypeStruct((4, R, C), g.dtype), name=name,
        compiler_params=_params(("arbitrary", "arbitrary"), 24))(c_idx, g, ra)


def _grad_sum(p, rb, q_idx, *, name):
    _, R, C = p.shape
    tr = _row_tile(R, C * 4)

    def body(q_ref, p_ref, rb_ref, o_ref):
        del q_ref
        g = p_ref[0].astype(F32)
        for k in range(3):
            g = g + rb_ref[k].astype(F32)
        o_ref[...] = g

    return pl.pallas_call(
        body,
        grid_spec=pltpu.PrefetchScalarGridSpec(
            num_scalar_prefetch=1, grid=(R // tr,),
            in_specs=[pl.BlockSpec((1, tr, C), lambda r, qr: (qr[0], r, 0)),
                      pl.BlockSpec((3, tr, C), lambda r, qr: (0, r, 0))],
            out_specs=pl.BlockSpec((tr, C), lambda r, qr: (r, 0))),
        out_shape=jax.ShapeDtypeStruct((R, C), F32), name=name,
        compiler_params=_params(("arbitrary",), 32))(q_idx, p, rb)


def _adamw(g, w, m, v):
    m = ADAM_B1 * m + (1.0 - ADAM_B1) * g
    v = ADAM_B2 * v + (1.0 - ADAM_B2) * (g * g)
    m_hat = m / (1.0 - ADAM_B1 ** ADAM_STEP)
    v_hat = v / (1.0 - ADAM_B2 ** ADAM_STEP)
    delta = -ADAM_LR * (m_hat / (jnp.sqrt(v_hat) + ADAM_EPS) + ADAM_WD * w)
    return delta, m, v


def _adam(g, w, m, v, *, name):
    R, C = w.shape
    tr = _row_tile(R, C * 4, budget=2 ** 20, align=SUBLANE)

    def body(g_ref, w_ref, m_ref, v_ref, d_out, m_out, v_out):
        delta, mn, vn = _adamw(g_ref[...], w_ref[...], m_ref[...], v_ref[...])
        d_out[...] = delta
        m_out[...] = mn
        v_out[...] = vn

    row = pl.BlockSpec((tr, C), lambda r: (r, 0))
    o = jax.ShapeDtypeStruct((R, C), F32)
    return pl.pallas_call(
        body, grid=(R // tr,), in_specs=[row, row, row, row], out_specs=[row, row, row], out_shape=[o, o, o],
        name=name, compiler_params=_params(("arbitrary",), 32))(g, w, m, v)


WEIGHTS = ["mix_norm_w", "w_in", "gate_bias", "gmlp_ln_w", "gmlp_ln_b", "gmlp_ws", "gmlp_bs", "ssm_conv_w",
           "ssm_conv_b", "ssm_dt_bias", "ssm_a_log", "ssm_d", "ssm_norm_w", "w_proj_a", "w_proj_b", "w_out",
           "ffn_norm_w", "ffn_w_up", "ffn_conv_w", "ffn_conv_b", "ffn_w_down", "final_norm_w"]
SHARDED = {"w_in": ((D, IN_COLS), 1), "gate_bias": ((2, D), 1), "ssm_conv_w": ((SK, SXBC), 1),
           "w_proj_a": ((GW, D), 0), "w_proj_b": ((SI, D), 0), "w_out": ((D, D), 0),
           "ffn_w_up": ((D, 2 * DFF), 1), "ffn_conv_w": ((FK, 2 * DFF), 1), "ffn_w_down": ((DFF, D), 0)}
REPLICATED = {"mix_norm_w": (D,), "gmlp_ln_w": (GG, GD), "gmlp_ln_b": (GG, GD), "gmlp_ws": (GG, GB, GB),
              "gmlp_bs": (GG, GB), "ssm_conv_b": (SXBC,), "ssm_dt_bias": (SH,), "ssm_a_log": (SH,), "ssm_d": (SH,),
              "ssm_norm_w": (SI,), "ffn_norm_w": (D,), "ffn_conv_b": (2 * DFF,), "final_norm_w": (D,)}
REPL_ORDER = [n for n in WEIGHTS if n in REPLICATED]
BTILE = 2 * SUBLANE
WIN_R = IN_COLS // NDEV
WIN_P = WIN_R + BTILE - WIN_R % BTILE
WIN_A = [WIN_R * d // BTILE * BTILE for d in range(NDEV)]
assert all(WIN_A[d] + WIN_P >= WIN_R * (d + 1) for d in range(NDEV)) and WIN_A[-1] + WIN_P == IN_COLS
BIG = [("w_proj_a", GW // NDEV, False), ("w_proj_b", SI // NDEV, False), ("w_out", D // NDEV, False),
       ("ffn_w_up", 2 * DFF // NDEV, True), ("ffn_w_down", DFF // NDEV, False), ("w_in", WIN_P, True)]
VECTORS = ["gate_bias", "ssm_conv_w", "ffn_conv_w"]


def _size(shape):
    n = 1
    for s in shape:
        n *= s
    return n


def _round_up(n, k):
    return (n + k - 1) // k * k


BIG_OFF = {}
_off = 0
for _n, _r, _t in BIG:
    BIG_OFF[_n] = _off
    _off += _r
BIG_USED = _off
BIG_ROWS = _round_up(BIG_USED, 2 * SUBLANE)
assert all(BIG_OFF[n] % (2 * SUBLANE) == 0 for n, _, _ in BIG)
VEC_SHAPE = {n: (SHARDED[n][0][0], SHARDED[n][0][1] // NDEV) for n in VECTORS}
VEC_ELEMS = sum(_size(VEC_SHAPE[n]) for n in VECTORS)
VEC_ROWS = _round_up(VEC_ELEMS, SUBLANE * LANE) // LANE
REPL_ELEMS = sum(_size(REPLICATED[n]) for n in REPL_ORDER)
REPL_ROWS = _round_up(REPL_ELEMS, NDEV * SUBLANE * LANE) // (NDEV * LANE)
SMALL_ROWS = VEC_ROWS + REPL_ROWS


def _win_offset(dev):
    return WIN_R * dev - WIN_R * dev // BTILE * BTILE


def _pack_big(arrs, dtype, dev):
    parts = []
    for n, r, t in BIG:
        a = (arrs[n].T if t else arrs[n]).astype(dtype)
        if n == "w_in":
            a = lax.dynamic_update_slice(jnp.zeros((WIN_P, D), dtype), a, (_win_offset(dev), 0))
        parts.append(a)
    parts.append(jnp.zeros((BIG_ROWS - BIG_USED, D), dtype))
    return jnp.concatenate(parts, axis=0)


def _join_windows(win):
    parts = []
    for d in range(NDEV):
        lo = BTILE if WIN_A[d] % WIN_R else 0
        if lo:
            parts.append(win[d - 1, WIN_P - BTILE:] + win[d, :BTILE])
        hi = WIN_P - BTILE if d + 1 < NDEV and WIN_A[d + 1] < WIN_A[d] + WIN_P else WIN_P
        parts.append(win[d, lo:hi])
    return jnp.concatenate(parts, axis=0)


def _split_windows(g):
    return jnp.stack([g[a:a + WIN_P] for a in WIN_A])


def _pack_vectors(arrs):
    flat = jnp.concatenate([arrs[n].reshape(-1) for n in VECTORS])
    return jnp.pad(flat, (0, VEC_ROWS * LANE - VEC_ELEMS)).reshape(VEC_ROWS, LANE)


def _unpack_vectors(flat):
    out, off = {}, 0
    v = flat.reshape(-1)
    for n in VECTORS:
        k = _size(VEC_SHAPE[n])
        out[n] = v[off:off + k].reshape((1,) + VEC_SHAPE[n])
        off += k
    return out


def _join_vectors(allv):
    out, off = {}, 0
    v = allv.reshape(NDEV, VEC_ROWS * LANE)
    for n in VECTORS:
        r, c = VEC_SHAPE[n]
        out[n] = v[:, off:off + r * c].reshape(NDEV, r, c).transpose(1, 0, 2).reshape(r, c * NDEV)
        off += r * c
    return out


def _split_vectors(grads):
    parts = []
    for n in VECTORS:
        r, c = VEC_SHAPE[n]
        parts.append(grads[n].reshape(r, NDEV, c).transpose(1, 0, 2).reshape(NDEV, r * c))
    v = jnp.concatenate(parts, axis=1)
    return jnp.pad(v, ((0, 0), (0, VEC_ROWS * LANE - VEC_ELEMS))).reshape(NDEV, VEC_ROWS, LANE)


def _pack_repl(arrs):
    flat = jnp.concatenate([arrs[n].reshape(-1) for n in REPL_ORDER])
    return jnp.pad(flat, (0, NDEV * REPL_ROWS * LANE - REPL_ELEMS)).reshape(NDEV * REPL_ROWS, LANE)


def _unpack_repl(flat, shapes):
    out, off = {}, 0
    v = flat.reshape(-1)
    for n in REPL_ORDER:
        k = _size(REPLICATED[n])
        out[n] = v[off:off + k].reshape(shapes[n])
        off += k
    return out


def kernel(x, mix_norm_w, w_in, gate_bias, gmlp_ln_w, gmlp_ln_b, gmlp_ws, gmlp_bs, ssm_conv_w, ssm_conv_b, ssm_dt_bias, ssm_a_log, ssm_d, ssm_norm_w, w_proj_a, w_proj_b, w_out, ffn_norm_w, ffn_w_up, ffn_conv_w, ffn_conv_b, ffn_w_down, final_norm_w, loss_target, m_mix_norm_w, m_w_in, m_gate_bias, m_gmlp_ln_w, m_gmlp_ln_b, m_gmlp_ws, m_gmlp_bs, m_ssm_conv_w, m_ssm_conv_b, m_ssm_dt_bias, m_ssm_a_log, m_ssm_d, m_ssm_norm_w, m_w_proj_a, m_w_proj_b, m_w_out, m_ffn_norm_w, m_ffn_w_up, m_ffn_conv_w, m_ffn_conv_b, m_ffn_w_down, m_final_norm_w, v_mix_norm_w, v_w_in, v_gate_bias, v_gmlp_ln_w, v_gmlp_ln_b, v_gmlp_ws, v_gmlp_bs, v_ssm_conv_w, v_ssm_conv_b, v_ssm_dt_bias, v_ssm_a_log, v_ssm_d, v_ssm_norm_w, v_w_proj_a, v_w_proj_b, v_w_out, v_ffn_norm_w, v_ffn_w_up, v_ffn_conv_w, v_ffn_conv_b, v_ffn_w_down, v_final_norm_w):
    given = dict(locals())
    wts = {n: given[n] for n in WEIGHTS}
    mom = {n: given["m_" + n] for n in WEIGHTS}
    var = {n: given["v_" + n] for n in WEIGHTS}
    xi, yi, ci = _axes()
    c_idx = jnp.reshape(ci, (1,)).astype(jnp.int32)
    q_idx = jnp.reshape(2 * xi + yi, (1,)).astype(jnp.int32)
    big_names = [n for n, _, _ in BIG]
    drop = lambda d, names: {n: d[n][0] for n in names}

    dev = 4 * xi + 2 * yi + ci
    all_big, all_vec = _all_gather([_pack_big(drop(wts, big_names), BF16, dev), _pack_vectors(wts)],
                                   name="weights_all_gather")
    full = {}
    for n, r, t in BIG:
        part = all_big[:, BIG_OFF[n]:BIG_OFF[n] + r]
        full[n + "_t" if t else n] = _join_windows(part) if n == "w_in" else part.reshape(NDEV * r, D)
    full.update(_join_vectors(all_vec))
    for n in REPL_ORDER:
        full[n] = wts[n].reshape(REPLICATED[n])

    loss_local, grad_x, grads = _local_step(x[0], loss_target[0], full)

    send_big = jnp.concatenate(
        [_split_windows(grads["w_in_t"]) if n == "w_in" else grads[n + "_t" if t else n].reshape(NDEV, r, D)
         for n, r, t in BIG]
        + [jnp.zeros((NDEV, BIG_ROWS - BIG_USED, D), F32)], axis=1).astype(BF16)
    send_small = jnp.concatenate([_split_vectors(grads), _pack_repl(grads).reshape(NDEV, REPL_ROWS, LANE)], axis=1)
    sib_big, sib_small = _exchange([send_big, send_small], _to_sibling_plan, name="grads_to_sibling")
    sum_big = _pair_add(send_big.reshape(4, 2, BIG_ROWS, D), sib_big, c_idx, name="grad_pair_add")
    sum_small = _pair_add(send_small.reshape(4, 2, SMALL_ROWS, LANE), sib_small, c_idx, name="grad_pair_add_small")
    chips_big, chips_small = _exchange([sum_big, sum_small], _to_chips_plan, name="grads_to_chips")
    g_big = _grad_sum(sum_big, chips_big, q_idx, name="grad_sum")
    g_small = _grad_sum(sum_small, chips_small, q_idx, name="grad_sum_small")

    outs = {}
    for n, r, t in BIG:
        g = g_big[BIG_OFF[n]:BIG_OFF[n] + r]
        if n == "w_in":
            g = lax.dynamic_slice(g, (_win_offset(dev), 0), (WIN_R, D))
        g = g.T if t else g
        outs[n] = tuple(a[None] for a in (g,) + tuple(_adam(g, wts[n][0], mom[n][0], var[n][0], name="adam_" + n)))
    g_vec = g_small[:VEC_ROWS]
    vec_out = (g_vec,) + tuple(_adam(g_vec, _pack_vectors(wts), _pack_vectors(mom), _pack_vectors(var),
                                     name="adam_vectors"))
    vec_out = [_unpack_vectors(a) for a in vec_out]
    for n in VECTORS:
        outs[n] = tuple(a[n] for a in vec_out)

    g_repl = _all_gather([g_small[VEC_ROWS:]], name="replicated_grads_all_gather")[0].reshape(NDEV * REPL_ROWS, LANE)
    repl_out = (g_repl,) + tuple(_adam(g_repl, _pack_repl(wts), _pack_repl(mom), _pack_repl(var),
                                       name="adam_replicated"))
    shapes = {n: wts[n].shape for n in WEIGHTS}
    repl_out = [_unpack_repl(a, shapes) for a in repl_out]
    for n in REPL_ORDER:
        outs[n] = tuple(a[n] for a in repl_out)

    loss = lax.psum(loss_local, ("x", "y", "c"))
    return (loss, grad_x[None]) + tuple(outs[n][k] for k in range(4) for n in WEIGHTS)
```

```python
import functools

import jax
import jax.numpy as jnp
from jax import lax
from jax.experimental import pallas as pl
from jax.experimental.pallas import tpu as pltpu

F32 = jnp.float32
BF16 = jnp.bfloat16

D = 1024
EPS = 1e-5
GW = 1024
GB = 128
GG = 8
GD = 128
GCH = 64
SI = 2048
SH = 32
SP = 64
SG = 4
SN = 128
SGW = SI // SG
SK = 4
SXBC = SI + 2 * SG * SN
DFF = 2816
FK = 3
PMAIN = 2 * D + 2 * GW + SI + SXBC
IN_COLS = PMAIN + SH
NDEV = 8
ADAM_LR, ADAM_B1, ADAM_B2, ADAM_EPS, ADAM_WD, ADAM_STEP = 0.001, 0.9, 0.999, 1e-08, 0.01, 10

LANE = 128
SUBLANE = 8
VMEM_MB_V7X = 64
VMEM_CAP_MB = 56

LS = 128
FT = DFF // 2

NN = (((1,), (0,)), ((), ()))
NT = (((1,), (1,)), ((), ()))
TN = (((0,), (0,)), ((), ()))


def _params(sem, vmem_mb):
    return pltpu.CompilerParams(dimension_semantics=sem,
                                vmem_limit_bytes=min(int(vmem_mb), VMEM_CAP_MB) * 1024 * 1024)


def _dot(a, b, dims=NN):
    return lax.dot_general(a, b, dims, preferred_element_type=F32)


def _sigmoid(x):
    return 1.0 / (1.0 + jnp.exp(-x))


def _split3(v):
    hi = v.astype(BF16)
    r = v - hi.astype(F32)
    mid = r.astype(BF16)
    lo = (r - mid.astype(F32)).astype(BF16)
    return hi, mid, lo


def _dot3(a_f32, b_bf16, dims):
    hi, mid, lo = _split3(a_f32)
    return _dot(hi, b_bf16, dims) + _dot(mid, b_bf16, dims) + _dot(lo, b_bf16, dims)


def _dot3_rhs(a_bf16, b_f32, dims):
    hi, mid, lo = _split3(b_f32)
    return _dot(a_bf16, hi, dims) + _dot(a_bf16, mid, dims) + _dot(a_bf16, lo, dims)


def _matmul(a, b, *, name, out_dtype, ta=False, tb=False, tm=1024, tn=1024, tk=1024, add=None,
            j_outer=False, b_rows=None, side=None):
    if ta:
        K, M = a.shape
    else:
        M, K = a.shape
    if tb:
        N, K2 = b.shape
        N = b_rows or N
    else:
        K2, N = b.shape
        K2 = b_rows or K2
    assert K == K2, (a.shape, b.shape, ta, tb)
    tm, tn, tk = min(tm, M), min(tn, N), min(tk, K)
    assert M % tm == 0 and N % tn == 0 and K % tk == 0, (M, N, K, tm, tn, tk)
    nk = K // tk
    dims = (((0 if ta else 1,), (1 if tb else 0,)), ((), ()))
    has_add = add is not None
    n_in = 3 if has_add else 2
    s_in = len(side.inputs) if side else 0
    s_out = len(side.out_shapes) if side else 0
    grid = (N // tn, M // tm, nk) if j_outer else (M // tm, N // tn, nk)

    def body(*refs):
        a_ref, b_ref = refs[:2]
        add_ref = refs[2] if has_add else None
        o_ref = refs[n_in + s_in]
        if side:
            side_refs = (refs[n_in:n_in + s_in], refs[n_in + s_in + 1:n_in + s_in + 1 + s_out]) + tuple(refs[-2:])
            ids = [pl.program_id(d) for d in range(3)]
            first = functools.reduce(jnp.logical_and, [i == 0 for i in ids])
            last = functools.reduce(jnp.logical_and, [i == g - 1 for i, g in zip(ids, grid)])

            @pl.when(first)
            def _():
                for cp in side.make(*side_refs):
                    cp.start()

            @pl.when(last)
            def _():
                for cp in side.make(*side_refs):
                    cp.wait()

        p = lax.dot_general(a_ref[...].astype(BF16), b_ref[...].astype(BF16), dims,
                            preferred_element_type=F32)

        def finish(acc):
            if has_add:
                acc = acc + add_ref[...].astype(F32)
            o_ref[...] = acc.astype(o_ref.dtype)

        if nk == 1:
            finish(p)
        else:
            acc_ref = refs[n_in + s_in + 1 + s_out]
            k = pl.program_id(2)

            @pl.when(k == 0)
            def _():
                acc_ref[...] = p

            @pl.when(jnp.logical_and(k > 0, k < nk - 1))
            def _():
                acc_ref[...] += p

            @pl.when(k == nk - 1)
            def _():
                finish(acc_ref[...] + p)

    if j_outer:
        ij = lambda g0, g1: (g1, g0)
    else:
        ij = lambda g0, g1: (g0, g1)

    def a_map(g0, g1, k):
        i, _ = ij(g0, g1)
        return (k, i) if ta else (i, k)

    def b_map(g0, g1, k):
        _, j = ij(g0, g1)
        return (j, k) if tb else (k, j)

    def o_map(g0, g1, k):
        return ij(g0, g1)

    in_specs = [pl.BlockSpec((tk, tm) if ta else (tm, tk), a_map),
                pl.BlockSpec((tn, tk) if tb else (tk, tn), b_map)]
    args = [a, b]
    if has_add:
        in_specs.append(pl.BlockSpec((tm, tn), o_map))
        args.append(add)
    scratch = [pltpu.VMEM((tm, tn), F32)] if nk > 1 else []
    osz = jnp.dtype(out_dtype).itemsize
    est = (2 * (tm * tk * a.dtype.itemsize + tk * tn * b.dtype.itemsize) + 2 * tm * tn * osz
           + (2 * tm * tn * add.dtype.itemsize if has_add else 0)
           + 3 * tm * tn * 4 + 2 * (tm * tk + tk * tn)) / 2 ** 20 + 4
    out_specs = [pl.BlockSpec((tm, tn), o_map)]
    out_shape = [jax.ShapeDtypeStruct((M, N), out_dtype)]
    aliases = {}
    if side:
        hbm = pl.BlockSpec(memory_space=pltpu.HBM)
        in_specs += [hbm] * s_in
        args += list(side.inputs)
        out_specs += [hbm] * s_out
        out_shape += list(side.out_shapes)
        scratch += [pltpu.SemaphoreType.DMA((side.nsem,)), pltpu.SemaphoreType.DMA((side.nsem,))]
        aliases = {n_in + i: 1 + j for i, j in side.aliases}
    outs = pl.pallas_call(
        body, grid=grid, in_specs=in_specs, out_specs=out_specs, out_shape=out_shape, scratch_shapes=scratch,
        input_output_aliases=aliases, name=name,
        compiler_params=_params(("arbitrary", "arbitrary", "arbitrary"), est))(*args)
    return (outs[0], list(outs[1:])) if side else outs[0]


class _Side:
    def __init__(self, inputs, out_shapes, nsem, make, aliases=()):
        self.inputs, self.out_shapes, self.nsem, self.make, self.aliases = inputs, out_shapes, nsem, make, aliases


def _rms_fwd(x, w, *, name):
    T = x.shape[0]
    tm = min(512, T)

    def body(x_ref, w_ref, o_ref):
        xv = x_ref[...]
        r = lax.rsqrt(jnp.mean(xv * xv, axis=-1, keepdims=True) + EPS)
        o_ref[...] = (xv * r * w_ref[...]).astype(BF16)

    return pl.pallas_call(
        body, grid=(T // tm,),
        in_specs=[pl.BlockSpec((tm, D), lambda i: (i, 0)), pl.BlockSpec((1, D), lambda i: (0, 0))],
        out_specs=pl.BlockSpec((tm, D), lambda i: (i, 0)),
        out_shape=jax.ShapeDtypeStruct((T, D), BF16), name=name,
        compiler_params=_params(("arbitrary",), 24))(x, w)


def _rms_bwd(x, w, dy, dres, *, name):
    T = x.shape[0]
    tm = min(512, T)

    def body(x_ref, w_ref, dy_ref, dres_ref, dx_ref, dw_ref):
        xv = x_ref[...]
        r = lax.rsqrt(jnp.mean(xv * xv, axis=-1, keepdims=True) + EPS)
        xhat = xv * r
        dyv = dy_ref[...].astype(F32)
        g = dyv * w_ref[...]
        dx_ref[...] = dres_ref[...] + r * (g - xhat * jnp.mean(g * xhat, axis=-1, keepdims=True))
        part = jnp.sum(dyv * xhat, axis=0, keepdims=True)

        @pl.when(pl.program_id(0) == 0)
        def _():
            dw_ref[...] = part

        @pl.when(pl.program_id(0) > 0)
        def _():
            dw_ref[...] += part

    row = pl.BlockSpec((tm, D), lambda i: (i, 0))
    vec = pl.BlockSpec((1, D), lambda i: (0, 0))
    return pl.pallas_call(
        body, grid=(T // tm,), in_specs=[row, vec, row, row], out_specs=[row, vec],
        out_shape=[jax.ShapeDtypeStruct((T, D), F32), jax.ShapeDtypeStruct((1, D), F32)], name=name,
        compiler_params=_params(("arbitrary",), 32))(x, w, dy, dres)


def _loss_head(h, tgt, w):
    T = h.shape[0]
    tm = min(512, T)

    def body(h_ref, t_ref, w_ref, loss_ref, dh_ref, dw_ref):
        hv = h_ref[...]
        r = lax.rsqrt(jnp.mean(hv * hv, axis=-1, keepdims=True) + EPS)
        xhat = hv * r
        wv = w_ref[...]
        err = xhat * wv - t_ref[...]
        lpart = 0.5 * jnp.sum(jnp.mean(err * err, axis=-1, keepdims=True), axis=0, keepdims=True)
        dy = err * (1.0 / D)
        g = dy * wv
        dh_ref[...] = r * (g - xhat * jnp.mean(g * xhat, axis=-1, keepdims=True))
        wpart = jnp.sum(dy * xhat, axis=0, keepdims=True)
        lrow = jnp.broadcast_to(lpart, (1, LANE))

        @pl.when(pl.program_id(0) == 0)
        def _():
            dw_ref[...] = wpart
            loss_ref[...] = lrow

        @pl.when(pl.program_id(0) > 0)
        def _():
            dw_ref[...] += wpart
            loss_ref[...] += lrow

    row = pl.BlockSpec((tm, D), lambda i: (i, 0))
    vec = pl.BlockSpec((1, D), lambda i: (0, 0))
    return pl.pallas_call(
        body, grid=(T // tm,), in_specs=[row, row, vec],
        out_specs=[pl.BlockSpec((1, LANE), lambda i: (0, 0)), row, vec],
        out_shape=[jax.ShapeDtypeStruct((1, LANE), F32), jax.ShapeDtypeStruct((T, D), F32),
                   jax.ShapeDtypeStruct((1, D), F32)], name="loss_head",
        compiler_params=_params(("arbitrary",), 32))(h, tgt, w)


_GELU_C = 0.7978845608028654
_GELU_A = 0.044715


def _gelu(x):
    t = jnp.tanh(_GELU_C * (x + _GELU_A * x * x * x))
    return 0.5 * x * (1.0 + t), t


def _gelu_grad(x, t):
    return 0.5 * (1.0 + t) + 0.5 * x * (1.0 - t * t) * _GELU_C * (1.0 + 3.0 * _GELU_A * x * x)


def _gmlp_mask():
    r = lax.broadcasted_iota(jnp.int32, (GB, GB), 0) // GCH
    c = lax.broadcasted_iota(jnp.int32, (GB, GB), 1) // GCH
    return c <= r


def _gmlp_fwd(proj, lnw, lnb, ws, bst):
    T = proj.shape[0]
    tm = min(512, T)
    nblk = tm // GB

    def body(u_ref, v_ref, lnw_ref, lnb_ref, ws_ref, bst_ref, o_ref):
        mask = _gmlp_mask()
        u, _ = _gelu(u_ref[...].astype(F32))
        v, _ = _gelu(v_ref[...].astype(F32))
        for g in range(GG):
            cs = slice(g * GD, (g + 1) * GD)
            vg = v[:, cs]
            mu = jnp.mean(vg, axis=-1, keepdims=True)
            vc = vg - mu
            var = jnp.mean(vc * vc, axis=-1, keepdims=True)
            vn = (vc * lax.rsqrt(var + EPS) * lnw_ref[g:g + 1, :] + lnb_ref[g:g + 1, :]).astype(BF16)
            wsg = jnp.where(mask, ws_ref[g], 0.0).astype(BF16)
            bcol = bst_ref[:, g:g + 1]
            for blk in range(nblk):
                rs = slice(blk * GB, (blk + 1) * GB)
                sv = _dot(wsg, vn[rs, :]) + bcol
                o_ref[rs, cs] = (u[rs, cs] * sv).astype(BF16)

    full = lambda shape: pl.BlockSpec(shape, lambda i: tuple(0 for _ in shape))
    return pl.pallas_call(
        body, grid=(T // tm,),
        in_specs=[pl.BlockSpec((tm, GW), lambda i: (i, 2)), pl.BlockSpec((tm, GW), lambda i: (i, 3)),
                  full((GG, GD)), full((GG, GD)), full((GG, GB, GB)), full((GB, GG))],
        out_specs=pl.BlockSpec((tm, GW), lambda i: (i, 0)),
        out_shape=jax.ShapeDtypeStruct((T, GW), BF16), name="gmlp_fwd",
        compiler_params=_params(("arbitrary",), 40))(proj, proj, lnw, lnb, ws, bst)


def _gmlp_bwd(proj, dya, dproj, lnw, lnb, ws, bst):
    T = proj.shape[0]
    tm = min(512, T)
    nblk = tm // GB

    def body(u_ref, v_ref, dya_ref, dproj_in, lnw_ref, lnb_ref, ws_ref, bst_ref,
             dz_ref, dlnw_ref, dlnb_ref, dws_ref, dbst_ref):
        del dproj_in
        first = pl.program_id(0) == 0

        @pl.when(first)
        def _():
            dlnw_ref[...] = jnp.zeros_like(dlnw_ref)
            dlnb_ref[...] = jnp.zeros_like(dlnb_ref)
            dws_ref[...] = jnp.zeros_like(dws_ref)
            dbst_ref[...] = jnp.zeros_like(dbst_ref)

        mask = _gmlp_mask()
        lane = lax.broadcasted_iota(jnp.int32, (GB, LANE), 1)
        ur = u_ref[...].astype(F32)
        vr = v_ref[...].astype(F32)
        u, tu = _gelu(ur)
        v, tv = _gelu(vr)
        gu = _gelu_grad(ur, tu)
        gv = _gelu_grad(vr, tv)
        dy = dya_ref[...].astype(F32)
        dbst = jnp.zeros((GB, LANE), F32)
        dlnw_rows, dlnb_rows = [], []
        for g in range(GG):
            cs = slice(g * GD, (g + 1) * GD)
            vg = v[:, cs]
            mu = jnp.mean(vg, axis=-1, keepdims=True)
            vc = vg - mu
            var = jnp.mean(vc * vc, axis=-1, keepdims=True)
            rstd = lax.rsqrt(var + EPS)
            xhat = vc * rstd
            lw = lnw_ref[g:g + 1, :]
            vn = (xhat * lw + lnb_ref[g:g + 1, :]).astype(BF16)
            wsg = jnp.where(mask, ws_ref[g], 0.0).astype(BF16)
            bcol = bst_ref[:, g:g + 1]
            dyg = dy[:, cs]
            ug = u[:, cs]
            dsv = dyg * ug
            dsv_b = dsv.astype(BF16)
            dws_g = jnp.zeros((GB, GB), F32)
            bsum = jnp.zeros((GB, 1), F32)
            dvn_parts = []
            for blk in range(nblk):
                rs = slice(blk * GB, (blk + 1) * GB)
                sv = _dot(wsg, vn[rs, :]) + bcol
                dz_ref[rs, cs] = (dyg[rs, :] * sv * gu[rs, cs]).astype(BF16)
                dws_g = dws_g + _dot(dsv_b[rs, :], vn[rs, :], NT)
                bsum = bsum + jnp.sum(dsv[rs, :], axis=-1, keepdims=True)
                dvn_parts.append(_dot(wsg, dsv_b[rs, :], TN))
            dvn = jnp.concatenate(dvn_parts, axis=0)
            dws_ref[g] += jnp.where(mask, dws_g, 0.0)
            dbst = dbst + jnp.where(lane == g, bsum, 0.0)
            dlnw_rows.append(jnp.sum(dvn * xhat, axis=0, keepdims=True))
            dlnb_rows.append(jnp.sum(dvn, axis=0, keepdims=True))
            dxh = dvn * lw
            dvg = rstd * (dxh - jnp.mean(dxh, axis=-1, keepdims=True)
                          - xhat * jnp.mean(dxh * xhat, axis=-1, keepdims=True))
            dz_ref[:, GW + g * GD:GW + (g + 1) * GD] = (dvg * gv[:, cs]).astype(BF16)
        dlnw_ref[...] += jnp.concatenate(dlnw_rows, axis=0)
        dlnb_ref[...] += jnp.concatenate(dlnb_rows, axis=0)
        dbst_ref[...] += dbst

    full = lambda shape: pl.BlockSpec(shape, lambda i: tuple(0 for _ in shape))
    outs = pl.pallas_call(
        body, grid=(T // tm,),
        in_specs=[pl.BlockSpec((tm, GW), lambda i: (i, 2)), pl.BlockSpec((tm, GW), lambda i: (i, 3)),
                  pl.BlockSpec((tm, GW), lambda i: (i, 0)), pl.BlockSpec(memory_space=pl.ANY),
                  full((GG, GD)), full((GG, GD)), full((GG, GB, GB)), full((GB, GG))],
        out_specs=[pl.BlockSpec((tm, 2 * GW), lambda i: (i, 1)), full((GG, GD)), full((GG, GD)),
                   full((GG, GB, GB)), full((GB, LANE))],
        out_shape=[jax.ShapeDtypeStruct(dproj.shape, dproj.dtype), jax.ShapeDtypeStruct((GG, GD), F32),
                   jax.ShapeDtypeStruct((GG, GD), F32), jax.ShapeDtypeStruct((GG, GB, GB), F32),
                   jax.ShapeDtypeStruct((GB, LANE), F32)],
        input_output_aliases={3: 0}, name="gmlp_bwd",
        compiler_params=_params(("arbitrary",), 48))(proj, proj, dya, dproj, lnw, lnb, ws, bst)
    return outs


def _merge_fwd(ya_pre, yb_pre, proj, bias, wpa, wpb):
    T = proj.shape[0]
    tm = min(512, T)

    def body(ya_ref, yb_ref, g_ref, b_ref, wpa_ref, wpb_ref, m_ref, oa_ref, ob_ref):
        ya = _dot(ya_ref[...], wpa_ref[...])
        yb = _dot(yb_ref[...], wpb_ref[...])
        g = g_ref[...].astype(F32)
        sa = _sigmoid(g[:, :D] + b_ref[0:1, :])
        sb = _sigmoid(g[:, D:] + b_ref[1:2, :])
        m_ref[...] = (sa * ya + sb * yb).astype(BF16)
        oa_ref[...] = ya.astype(BF16)
        ob_ref[...] = yb.astype(BF16)

    row = lambda w: pl.BlockSpec((tm, w), lambda i: (i, 0))
    full = lambda shape: pl.BlockSpec(shape, lambda i: tuple(0 for _ in shape))
    o = jax.ShapeDtypeStruct((T, D), BF16)
    return pl.pallas_call(
        body, grid=(T // tm,),
        in_specs=[row(GW), row(SI), row(2 * D), full((2, D)), full((GW, D)), full((SI, D))],
        out_specs=[row(D), row(D), row(D)], out_shape=[o, o, o], name="merge_fwd",
        compiler_params=_params(("arbitrary",), 40))(ya_pre, yb_pre, proj, bias, wpa, wpb)


def _merge_bwd(dm, proj, bias, ya, yb, wpa, wpb):
    T = proj.shape[0]
    tm = min(512, T)

    def body(dm_ref, g_ref, b_ref, ya_ref, yb_ref, wpa_ref, wpb_ref,
             dg_ref, dya_ref, dyb_ref, dpa_ref, dpb_ref, db_ref):
        dmv = dm_ref[...].astype(F32)
        g = g_ref[...].astype(F32)
        sa = _sigmoid(g[:, :D] + b_ref[0:1, :])
        sb = _sigmoid(g[:, D:] + b_ref[1:2, :])
        dya = (dmv * sa).astype(BF16)
        dyb = (dmv * sb).astype(BF16)
        dga = dmv * ya_ref[...].astype(F32) * sa * (1.0 - sa)
        dgb = dmv * yb_ref[...].astype(F32) * sb * (1.0 - sb)
        dg_ref[:, :D] = dga.astype(BF16)
        dg_ref[:, D:] = dgb.astype(BF16)
        dya_ref[...] = dya
        dyb_ref[...] = dyb
        dpa_ref[...] = _dot(dya, wpa_ref[...], NT).astype(BF16)
        dpb_ref[...] = _dot(dyb, wpb_ref[...], NT).astype(BF16)
        part = jnp.concatenate([jnp.sum(dga, axis=0, keepdims=True), jnp.sum(dgb, axis=0, keepdims=True)], axis=0)

        @pl.when(pl.program_id(0) == 0)
        def _():
            db_ref[...] = part

        @pl.when(pl.program_id(0) > 0)
        def _():
            db_ref[...] += part

    row = lambda w: pl.BlockSpec((tm, w), lambda i: (i, 0))
    full = lambda shape: pl.BlockSpec(shape, lambda i: tuple(0 for _ in shape))
    o = lambda w: jax.ShapeDtypeStruct((T, w), BF16)
    return pl.pallas_call(
        body, grid=(T // tm,),
        in_specs=[row(D), row(2 * D), full((2, D)), row(D), row(D), full((GW, D)), full((SI, D))],
        out_specs=[row(2 * D), row(D), row(D), row(GW), row(SI), full((2, D))],
        out_shape=[o(PMAIN), o(D), o(D), o(GW), o(SI), jax.ShapeDtypeStruct((2, D), F32)], name="merge_bwd",
        compiler_params=_params(("arbitrary",), 48))(dm, proj, bias, ya, yb, wpa, wpb)


def _ffn_act_fwd(up, cw, cb):
    T = up.shape[0]
    tm = min(512, T)
    H = SUBLANE

    def body(up_ref, cw_ref, cb_ref, o_ref, ext):
        @pl.when(pl.program_id(1) == 0)
        def _():
            ext[0:H, :] = jnp.zeros((H, 2 * FT), F32)

        ext[H:H + tm, :] = up_ref[...].astype(F32)
        xc = cb_ref[...] + cw_ref[0:1, :] * ext[H - 2:H - 2 + tm, :]
        xc = xc + cw_ref[1:2, :] * ext[H - 1:H - 1 + tm, :]
        xc = xc + cw_ref[2:3, :] * ext[H:H + tm, :]
        gate = xc[:, :FT]
        o_ref[...] = (gate * _sigmoid(gate) * xc[:, FT:]).astype(BF16)
        ext[0:H, :] = ext[tm:tm + H, :]

    return pl.pallas_call(
        body, grid=(2, T // tm),
        in_specs=[pl.BlockSpec((tm, 2 * FT), lambda j, i: (i, j)), pl.BlockSpec((FK, 2 * FT), lambda j, i: (0, j)),
                  pl.BlockSpec((1, 2 * FT), lambda j, i: (0, j))],
        out_specs=pl.BlockSpec((tm, FT), lambda j, i: (i, j)),
        out_shape=jax.ShapeDtypeStruct((T, DFF), BF16),
        scratch_shapes=[pltpu.VMEM((tm + H, 2 * FT), F32)], name="ffn_act_fwd",
        compiler_params=_params(("arbitrary", "arbitrary"), 48))(up, cw, cb)


def _ffn_act_bwd(up, dact, cw, cb):
    T = up.shape[0]
    tm = min(512, T)
    nt = T // tm
    H = SUBLANE
    hb = tm // H

    def body(up_ref, halo_ref, da_ref, cw_ref, cb_ref, dup_ref, dcw_ref, dcb_ref, ext, extd):
        i = pl.program_id(1)
        ri = nt - 1 - i

        @pl.when(i == 0)
        def _():
            extd[tm:tm + H, :] = jnp.zeros((H, 2 * FT), F32)
            dcw_ref[...] = jnp.zeros_like(dcw_ref)
            dcb_ref[...] = jnp.zeros_like(dcb_ref)

        ext[0:H, :] = jnp.where(ri > 0, halo_ref[...].astype(F32), 0.0)
        ext[H:H + tm, :] = up_ref[...].astype(F32)
        taps = [ext[H - 2 + k:H - 2 + k + tm, :] for k in range(FK)]
        xc = cb_ref[...] + cw_ref[0:1, :] * taps[0] + cw_ref[1:2, :] * taps[1] + cw_ref[2:3, :] * taps[2]
        gate, val = xc[:, :FT], xc[:, FT:]
        sg = _sigmoid(gate)
        dav = da_ref[...].astype(F32)
        dgate = dav * val * sg * (1.0 + gate * (1.0 - sg))
        dval = dav * gate * sg
        dxc = jnp.concatenate([dgate, dval], axis=1)
        dcb_ref[...] += jnp.sum(dxc, axis=0, keepdims=True)
        dcw_ref[...] += jnp.concatenate([jnp.sum(dxc * t, axis=0, keepdims=True) for t in taps], axis=0)
        extd[0:tm, :] = dxc
        dup = cw_ref[2:3, :] * dxc + cw_ref[1:2, :] * extd[1:1 + tm, :] + cw_ref[0:1, :] * extd[2:2 + tm, :]
        dup_ref[...] = dup.astype(BF16)
        extd[tm:tm + H, :] = dxc[0:H, :]

    return pl.pallas_call(
        body, grid=(2, nt),
        in_specs=[pl.BlockSpec((tm, 2 * FT), lambda j, i: (nt - 1 - i, j)),
                  pl.BlockSpec((H, 2 * FT), lambda j, i: (jnp.maximum((nt - 1 - i) * hb - 1, 0), j)),
                  pl.BlockSpec((tm, FT), lambda j, i: (nt - 1 - i, j)),
                  pl.BlockSpec((FK, 2 * FT), lambda j, i: (0, j)), pl.BlockSpec((1, 2 * FT), lambda j, i: (0, j))],
        out_specs=[pl.BlockSpec((tm, 2 * FT), lambda j, i: (nt - 1 - i, j)),
                   pl.BlockSpec((FK, 2 * FT), lambda j, i: (0, j)), pl.BlockSpec((1, 2 * FT), lambda j, i: (0, j))],
        out_shape=[jax.ShapeDtypeStruct((T, 2 * DFF), BF16), jax.ShapeDtypeStruct((FK, 2 * DFF), F32),
                   jax.ShapeDtypeStruct((1, 2 * DFF), F32)],
        scratch_shapes=[pltpu.VMEM((tm + H, 2 * FT), F32), pltpu.VMEM((tm + H, 2 * FT), F32)], name="ffn_act_bwd",
        compiler_params=_params(("arbitrary", "arbitrary"), 56))(up, up, dact, cw, cb)


def _softplus(x):
    e = jnp.exp(-jnp.abs(x))
    return jnp.maximum(x, 0.0) + jnp.where(e < 1e-4, e * (1.0 - 0.5 * e), jnp.log(1.0 + e))


def _ssd_consts():
    li = lax.broadcasted_iota(jnp.int32, (LS, LS), 0)
    si = lax.broadcasted_iota(jnp.int32, (LS, LS), 1)
    tril = si <= li
    hh = lax.broadcasted_iota(jnp.int32, (LANE, SI), 0)
    cc = lax.broadcasted_iota(jnp.int32, (LANE, SI), 1) // SP
    expand = jnp.where(hh == cc, 1.0, 0.0).astype(BF16)
    return tril, expand


def _ssd_pre(ext, cw_ref, cb_ref, dt_ref, dtb_ref, alog_ref, tril, expand):
    H = SUBLANE
    taps = [ext[H - (SK - 1) + k:H - (SK - 1) + k + LS, :] for k in range(SK)]
    xc = cb_ref[...]
    for k in range(SK):
        xc = xc + cw_ref[k:k + 1, :] * taps[k]
    sx = _sigmoid(xc)
    xbc = xc * sx
    xs, bm, cm = xbc[:, :SI], xbc[:, SI:SI + SG * SN], xbc[:, SI + SG * SN:]
    dtin = dt_ref[...] + dtb_ref[...]
    dt = _softplus(dtin)
    a_neg = -jnp.exp(alog_ref[...])
    dta = dt * a_neg
    trilb = jnp.where(tril, 1.0, 0.0).astype(BF16)
    a = _dot3_rhs(trilb, dta, NN)
    a_exp = _dot3(a, expand, NN)
    dt_exp = _dot3(dt, expand, NN)
    xdt = xs * dt_exp
    a_last = a_exp[LS - 1:LS, :]
    return dict(taps=taps, xc=xc, sx=sx, xs=xs, bm=bm, cm=cm, dtin=dt_ref[...] + dtb_ref[...], dt=dt, a_neg=a_neg,
                a=a, a_t=a.T, a_exp=a_exp, dt_exp=dt_exp, xdt=xdt, ea=jnp.exp(a_exp),
                w=jnp.exp(a_last - a_exp), eal=jnp.exp(a_last))


def _head_decay(pre, tril, h):
    seg = pre["a"][:, h:h + 1] - pre["a_t"][h:h + 1, :]
    return jnp.exp(jnp.where(tril, seg, -1e30))


def _ssd_fwd(proj, dtraw, cw, cb, dtb, alog, dexp, nw):
    T = proj.shape[0]
    nc = T // LS
    H = SUBLANE

    def body(z_ref, x_ref, dt_ref, cw_ref, cb_ref, dtb_ref, alog_ref, dexp_ref, nw_ref,
             yb_ref, y_ref, sp_ref, ext, st):
        @pl.when(pl.program_id(0) == 0)
        def _():
            ext[0:H, :] = jnp.zeros((H, SXBC), F32)
            st[...] = jnp.zeros_like(st)

        ext[H:H + LS, :] = x_ref[...].astype(F32)
        tril, expand = _ssd_consts()
        pre = _ssd_pre(ext, cw_ref, cb_ref, dt_ref, dtb_ref, alog_ref, tril, expand)
        lane = lax.broadcasted_iota(jnp.int32, (LS, LANE), 1)
        lo = lane < SP
        zf = z_ref[...].astype(F32)
        siluz = zf * _sigmoid(zf)
        for g in range(SG):
            gs = slice(g * SGW, (g + 1) * SGW)
            bg = pre["bm"][:, g * SN:(g + 1) * SN].astype(BF16)
            cg = pre["cm"][:, g * SN:(g + 1) * SN].astype(BF16)
            gmat = _dot(cg, bg, NT)
            sg = st[g]
            sp_ref[0, g] = sg
            yoff = _dot(cg, sg.astype(BF16))
            parts = []
            for j in range(SGW // LANE):
                h0 = g * (SGW // SP) + 2 * j
                m0 = gmat * _head_decay(pre, tril, h0)
                m1 = gmat * _head_decay(pre, tril, h0 + 1)
                xp = pre["xdt"][:, g * SGW + j * LANE:g * SGW + (j + 1) * LANE]
                rhs = jnp.concatenate([jnp.where(lo, xp, 0.0), jnp.where(lo, 0.0, xp)], axis=0).astype(BF16)
                parts.append(_dot(jnp.concatenate([m0, m1], axis=1).astype(BF16), rhs))
            y = (jnp.concatenate(parts, axis=1) + pre["ea"][:, gs] * yoff + dexp_ref[:, gs] * pre["xs"][:, gs])
            st[g] = pre["eal"][:, gs] * sg + _dot(bg, (pre["w"][:, gs] * pre["xdt"][:, gs]).astype(BF16), TN)
            y_ref[:, gs] = y
            yg = y * siluz[:, gs]
            r = lax.rsqrt(jnp.mean(yg * yg, axis=-1, keepdims=True) + EPS)
            yb_ref[:, gs] = (yg * r * nw_ref[:, gs]).astype(BF16)
        ext[0:H, :] = ext[LS:LS + H, :]

    vec = lambda w: pl.BlockSpec((1, w), lambda c: (0, 0))
    return pl.pallas_call(
        body, grid=(nc,),
        in_specs=[pl.BlockSpec((LS, SI), lambda c: (c, 2)), pl.BlockSpec((LS, SXBC), lambda c: (c, 2)),
                  pl.BlockSpec((LS, LANE), lambda c: (c, 0)),
                  pl.BlockSpec((SK, SXBC), lambda c: (0, 0)), vec(SXBC), vec(LANE), vec(LANE), vec(SI), vec(SI)],
        out_specs=[pl.BlockSpec((LS, SI), lambda c: (c, 0)), pl.BlockSpec((LS, SI), lambda c: (c, 0)),
                   pl.BlockSpec((1, SG, SN, SGW), lambda c: (c, 0, 0, 0))],
        out_shape=[jax.ShapeDtypeStruct((T, SI), BF16), jax.ShapeDtypeStruct((T, SI), F32),
                   jax.ShapeDtypeStruct((nc, SG, SN, SGW), F32)],
        scratch_shapes=[pltpu.VMEM((LS + H, SXBC), F32), pltpu.VMEM((SG, SN, SGW), F32)], name="ssd_fwd",
        compiler_params=_params(("arbitrary",), VMEM_CAP_MB))(proj, proj, dtraw, cw, cb, dtb, alog, dexp, nw)


def _ssd_bwd(proj, dtraw, y, sprev, dyb, dproj, cw, cb, dtb, alog, dexp, nw):
    T = proj.shape[0]
    nc = T // LS
    H = SUBLANE
    hb = LS // H
    NJ = (SI + SXBC) // D
    J0 = (2 * D + 2 * GW) // D

    def body(z_ref, x_ref, halo_ref, dt_ref, y_ref, sp_ref, dyb_ref, dproj_in,
             cw_ref, cb_ref, dtb_ref, alog_ref, dexp_ref, nw_ref,
             dp_ref, ddt_ref, dcw_ref, dcb_ref, ddtb_ref, da_ref, dd_ref, dnw_ref,
             ext, extd, ds, stage):
        del dproj_in
        i = pl.program_id(0)
        j = pl.program_id(1)
        c = nc - 1 - i

        @pl.when(jnp.logical_and(i == 0, j == 0))
        def _():
            extd[LS:LS + H, :] = jnp.zeros((H, SXBC), F32)
            ds[...] = jnp.zeros_like(ds)
            for r in (dcw_ref, dcb_ref, ddtb_ref, da_ref, dd_ref, dnw_ref):
                r[...] = jnp.zeros_like(r)

        @pl.when(j == 0)
        def _():
            ext[0:H, :] = jnp.where(c > 0, halo_ref[...].astype(F32), 0.0)
            ext[H:H + LS, :] = x_ref[...].astype(F32)
            tril, expand = _ssd_consts()
            pre = _ssd_pre(ext, cw_ref, cb_ref, dt_ref, dtb_ref, alog_ref, tril, expand)
            lane = lax.broadcasted_iota(jnp.int32, (LS, LANE), 1)
            sub = lax.broadcasted_iota(jnp.int32, (LANE, LS), 0)
            rowi = lax.broadcasted_iota(jnp.int32, (LS, 1), 0)
            lo = lane < SP
            xs, xdt, ea, w, eal = pre["xs"], pre["xdt"], pre["ea"], pre["w"], pre["eal"]

            zf = z_ref[...].astype(F32)
            sz = _sigmoid(zf)
            siluz = zf * sz
            yv = y_ref[...]
            yg = yv * siluz
            dout = dyb_ref[...].astype(F32)
            dyg_parts, dnw_parts = [], []
            for g in range(SG):
                gs = slice(g * SGW, (g + 1) * SGW)
                ygg = yg[:, gs]
                r = lax.rsqrt(jnp.mean(ygg * ygg, axis=-1, keepdims=True) + EPS)
                yhat = ygg * r
                dn = dout[:, gs] * nw_ref[:, gs]
                dnw_parts.append(jnp.sum(dout[:, gs] * yhat, axis=0, keepdims=True))
                dyg_parts.append(r * (dn - yhat * jnp.mean(dn * yhat, axis=-1, keepdims=True)))
            dyg = jnp.concatenate(dyg_parts, axis=1)
            dnw_ref[...] += jnp.concatenate(dnw_parts, axis=1)
            dy = dyg * siluz
            stage[:, 0:SI] = (dyg * yv * sz * (1.0 + zf * (1.0 - sz))).astype(BF16)
            dd_ref[...] += jnp.sum(dy * xs, axis=0, keepdims=True)
            tt = ea * dy

            da_rows = jnp.zeros((LS, LANE), F32)
            da_cols = jnp.zeros((LANE, LS), F32)
            dxdt_parts, db_parts, dc_parts, daexp_parts = [], [], [], []
            for g in range(SG):
                gs = slice(g * SGW, (g + 1) * SGW)
                bg = pre["bm"][:, g * SN:(g + 1) * SN].astype(BF16)
                cg = pre["cm"][:, g * SN:(g + 1) * SN].astype(BF16)
                sg = sp_ref[0, g]
                sgb = sg.astype(BF16)
                dsg = ds[g]
                dsgb = dsg.astype(BF16)
                ttg = tt[:, gs].astype(BF16)
                yoff = _dot(cg, sgb)
                dc = _dot(ttg, sgb, NT)
                gmat = _dot(cg, bg, NT)
                dgm = jnp.zeros((LS, LS), F32)
                dxdt_pairs = []
                for jj in range(SGW // LANE):
                    h0 = g * (SGW // SP) + 2 * jj
                    ps = slice(g * SGW + jj * LANE, g * SGW + (jj + 1) * LANE)
                    l0 = _head_decay(pre, tril, h0)
                    l1 = _head_decay(pre, tril, h0 + 1)
                    m0 = gmat * l0
                    m1 = gmat * l1
                    dyp = dy[:, ps]
                    dy_lo = jnp.where(lo, dyp, 0.0).astype(BF16)
                    dy_hi = jnp.where(lo, 0.0, dyp).astype(BF16)
                    xpb = xdt[:, ps].astype(BF16)
                    dm0 = _dot(dy_lo, xpb, NT)
                    dm1 = _dot(dy_hi, xpb, NT)
                    q0 = dm0 * m0
                    q1 = dm1 * m1
                    da_rows = da_rows + jnp.where(lane == h0, jnp.sum(q0, axis=1, keepdims=True), 0.0)
                    da_rows = da_rows + jnp.where(lane == h0 + 1, jnp.sum(q1, axis=1, keepdims=True), 0.0)
                    da_cols = da_cols + jnp.where(sub == h0, jnp.sum(q0, axis=0, keepdims=True), 0.0)
                    da_cols = da_cols + jnp.where(sub == h0 + 1, jnp.sum(q1, axis=0, keepdims=True), 0.0)
                    dgm = dgm + dm0 * l0 + dm1 * l1
                    mcat = jnp.concatenate([m0, m1], axis=0).astype(BF16)
                    dycat = jnp.concatenate([dy_lo, dy_hi], axis=0)
                    dxdt_pairs.append(_dot(mcat, dycat, TN))
                dgb = dgm.astype(BF16)
                dc = dc + _dot(dgb, bg)
                db = _dot(dgb, cg, TN)
                zg = _dot(bg, dsgb)
                wg, xdtg = w[:, gs], xdt[:, gs]
                dxdt_g = jnp.concatenate(dxdt_pairs, axis=1) + wg * zg
                qg = zg * xdtg * wg
                last = (jnp.sum(qg, axis=0, keepdims=True)
                        + jnp.sum(dsg * sg, axis=0, keepdims=True) * eal[:, gs])
                daexp_parts.append(dy[:, gs] * ea[:, gs] * yoff - qg + jnp.where(rowi == LS - 1, last, 0.0))
                db = db + _dot((wg * xdtg).astype(BF16), dsgb, NT)
                ds[g] = eal[:, gs] * dsg + _dot(cg, ttg, TN)
                dxdt_parts.append(dxdt_g)
                db_parts.append(db)
                dc_parts.append(dc)
            dxdt = jnp.concatenate(dxdt_parts, axis=1)
            da_exp = jnp.concatenate(daexp_parts, axis=1)
            da = _dot3(da_exp, expand, NT) + da_rows - da_cols.T
            triub = jnp.where(tril, 1.0, 0.0).astype(BF16)
            ddta = _dot3_rhs(triub, da, TN)
            ddt = ddta * pre["a_neg"] + _dot3(dxdt * xs, expand, NT)
            da_ref[...] += jnp.sum(ddta * pre["dt"], axis=0, keepdims=True)
            ddt_raw = ddt * _sigmoid(pre["dtin"])
            ddt_ref[...] = ddt_raw
            ddtb_ref[...] += jnp.sum(ddt_raw, axis=0, keepdims=True)
            dxs = dexp_ref[...] * dy + dxdt * pre["dt_exp"]
            dxbc = jnp.concatenate([dxs] + db_parts + dc_parts, axis=1)
            sx, xc = pre["sx"], pre["xc"]
            dxc = dxbc * sx * (1.0 + xc * (1.0 - sx))
            dcb_ref[...] += jnp.sum(dxc, axis=0, keepdims=True)
            dcw_ref[...] += jnp.concatenate([jnp.sum(dxc * t, axis=0, keepdims=True) for t in pre["taps"]], axis=0)
            extd[0:LS, :] = dxc
            dxr = cw_ref[SK - 1:SK, :] * dxc
            for k in range(SK - 1):
                dxr = dxr + cw_ref[k:k + 1, :] * extd[SK - 1 - k:SK - 1 - k + LS, :]
            stage[:, SI:] = dxr.astype(BF16)
            extd[LS:LS + H, :] = dxc[0:H, :]

        dp_ref[...] = stage[:, pl.ds(pl.multiple_of(j * D, D), D)]

    vec = lambda w: pl.BlockSpec((1, w), lambda i, j: (0, 0))
    rev = lambda w, cb_: pl.BlockSpec((LS, w), lambda i, j: (nc - 1 - i, cb_))
    outs = pl.pallas_call(
        body, grid=(nc, NJ),
        in_specs=[rev(SI, 2), rev(SXBC, 2),
                  pl.BlockSpec((H, SXBC), lambda i, j: (jnp.maximum((nc - 1 - i) * hb - 1, 0), 2)),
                  rev(LANE, 0), rev(SI, 0),
                  pl.BlockSpec((1, SG, SN, SGW), lambda i, j: (nc - 1 - i, 0, 0, 0)),
                  rev(SI, 0), pl.BlockSpec(memory_space=pl.ANY),
                  pl.BlockSpec((SK, SXBC), lambda i, j: (0, 0)), vec(SXBC), vec(LANE), vec(LANE), vec(SI), vec(SI)],
        out_specs=[pl.BlockSpec((LS, D), lambda i, j: (nc - 1 - i, J0 + j)), rev(LANE, 0),
                   pl.BlockSpec((SK, SXBC), lambda i, j: (0, 0)), vec(SXBC), vec(LANE), vec(LANE), vec(SI), vec(SI)],
        out_shape=[jax.ShapeDtypeStruct(dproj.shape, dproj.dtype), jax.ShapeDtypeStruct((T, LANE), F32),
                   jax.ShapeDtypeStruct((SK, SXBC), F32), jax.ShapeDtypeStruct((1, SXBC), F32),
                   jax.ShapeDtypeStruct((1, LANE), F32), jax.ShapeDtypeStruct((1, LANE), F32),
                   jax.ShapeDtypeStruct((1, SI), F32), jax.ShapeDtypeStruct((1, SI), F32)],
        scratch_shapes=[pltpu.VMEM((LS + H, SXBC), F32), pltpu.VMEM((LS + H, SXBC), F32),
                        pltpu.VMEM((SG, SN, SGW), F32), pltpu.VMEM((LS, SI + SXBC), BF16)],
        input_output_aliases={7: 0}, name="ssd_bwd",
        compiler_params=_params(("arbitrary", "arbitrary"), VMEM_CAP_MB))(
            proj, proj, proj, dtraw, y, sprev, dyb, dproj, cw, cb, dtb, alog, dexp, nw)
    return outs


def _perm_ffn_cols(a):
    lead = a.shape[:-1]
    return a.reshape(lead + (2, 2, FT)).swapaxes(-3, -2).reshape(lead + (2 * DFF,))


def _perm_ffn_rows(a):
    return a.reshape((2, 2, FT) + a.shape[1:]).swapaxes(0, 1).reshape(a.shape)


def _pad_lanes(v, n=LANE):
    return jnp.pad(v, ((0, 0), (0, n - v.shape[-1])))


LATE = ["w_proj_a", "w_proj_b", "w_out", "ffn_w_up_t", "ffn_w_down"]


class _NoExchange:
    def gather_start(self):
        return None

    def gather_pass_on(self, outs):
        return None

    def late_weights(self, w, outs):
        return w

    def reduce_late(self, grads):
        return None

    def reduce_w_in(self, grad):
        return None

    def reduced(self, late_outs, w_in_outs):
        pass


def _local_step(x, tgt, w, hooks=None):
    hooks = hooks or _NoExchange()

    def mm(*args, side=None, **kw):
        out = _matmul(*args, side=side, **kw)
        return out if side is not None else (out, [])

    win_t = w["w_in_t"]
    win_dt = jnp.pad(w["w_in_t"][PMAIN:], ((0, LANE - SH), (0, 0)))
    fcw = _perm_ffn_cols(w["ffn_conv_w"])
    fcb = _perm_ffn_cols(w["ffn_conv_b"][None, :])
    mixw = w["mix_norm_w"][None, :]
    ffnw = w["ffn_norm_w"][None, :]
    finw = w["final_norm_w"][None, :]
    bst = w["gmlp_bs"].T
    scb = w["ssm_conv_b"][None, :]
    dtb = _pad_lanes(w["ssm_dt_bias"][None, :])
    alog = _pad_lanes(w["ssm_a_log"][None, :])
    dexp = jnp.repeat(w["ssm_d"], SP)[None, :]
    snw = w["ssm_norm_w"][None, :]

    xn = _rms_fwd(x, mixw, name="mix_norm")
    proj, got = mm(xn, win_t, name="in_proj", out_dtype=BF16, tb=True, tn=1536, j_outer=True, b_rows=PMAIN,
                   side=hooks.gather_start())
    dtraw, got = mm(xn, win_dt, name="in_proj_dt", out_dtype=F32, tb=True, side=hooks.gather_pass_on(got))
    w = hooks.late_weights(w, got)
    wup = _perm_ffn_rows(w["ffn_w_up_t"])
    ya_pre = _gmlp_fwd(proj, w["gmlp_ln_w"], w["gmlp_ln_b"], w["gmlp_ws"], bst)
    yb_pre, y_ssd, sprev = _ssd_fwd(proj, dtraw, w["ssm_conv_w"], scb, dtb, alog, dexp, snw)
    merged, ya, yb = _merge_fwd(ya_pre, yb_pre, proj, w["gate_bias"], w["w_proj_a"], w["w_proj_b"])
    h1 = _matmul(merged, w["w_out"], name="out_proj", out_dtype=F32, add=x)
    hn = _rms_fwd(h1, ffnw, name="ffn_norm")
    up = _matmul(hn, wup, name="ffn_up", out_dtype=BF16, tb=True, tn=FT, j_outer=True)
    act = _ffn_act_fwd(up, fcw, fcb)
    h2 = _matmul(act, w["ffn_w_down"], name="ffn_down", out_dtype=F32, tk=FT, add=h1)

    loss_row, dh2, d_finw = _loss_head(h2, tgt, finw)
    dact = _matmul(dh2, w["ffn_w_down"], name="ffn_down_dx", out_dtype=BF16, tb=True, tn=FT)
    d_wdown = _matmul(act, dh2, name="ffn_down_dw", out_dtype=F32, ta=True, tm=FT)
    dup, d_fcw, d_fcb = _ffn_act_bwd(up, dact, fcw, fcb)
    dhn = _matmul(dup, wup, name="ffn_up_dx", out_dtype=F32, tk=FT)
    d_wup = _matmul(dup, hn, name="ffn_up_dw", out_dtype=F32, ta=True, tm=FT)
    dh1, d_ffnw = _rms_bwd(h1, ffnw, dhn, dh2, name="ffn_norm_bwd")
    dmerged = _matmul(dh1, w["w_out"], name="out_proj_dx", out_dtype=BF16, tb=True)
    d_wout = _matmul(merged, dh1, name="out_proj_dw", out_dtype=F32, ta=True)
    dproj, dya, dyb, dya_pre, dyb_pre, d_gbias = _merge_bwd(dmerged, proj, w["gate_bias"], ya, yb,
                                                           w["w_proj_a"], w["w_proj_b"])
    d_wpa = _matmul(ya_pre, dya, name="proj_a_dw", out_dtype=F32, ta=True)
    d_wpb = _matmul(yb_pre, dyb, name="proj_b_dw", out_dtype=F32, ta=True)
    dproj, d_lnw, d_lnb, d_ws, d_bst = _gmlp_bwd(proj, dya_pre, dproj, w["gmlp_ln_w"], w["gmlp_ln_b"],
                                                 w["gmlp_ws"], bst)
    dproj, ddt, d_scw, d_scb, d_dtb, d_a, d_dch, d_snw = _ssd_bwd(
        proj, dtraw, y_ssd, sprev, dyb_pre, dproj, w["ssm_conv_w"], scb, dtb, alog, dexp, snw)
    late = {"w_proj_a": d_wpa, "w_proj_b": d_wpb, "w_out": d_wout, "ffn_w_up_t": _perm_ffn_rows(d_wup),
            "ffn_w_down": d_wdown}
    d_win_main, late_outs = mm(dproj, xn, name="in_proj_dw", out_dtype=F32, ta=True, tm=1536,
                               side=hooks.reduce_late(late))
    d_win_dt = _matmul(ddt, xn, name="in_proj_dt_dw", out_dtype=F32, ta=True)
    d_win_t = jnp.concatenate([d_win_main, d_win_dt[:SH]], axis=0)
    dxn = _matmul(ddt, win_dt, name="in_proj_dt_dx", out_dtype=F32)
    dxn, w_in_outs = mm(dproj, win_t, name="in_proj_dx", out_dtype=F32, tk=1536, add=dxn, b_rows=PMAIN,
                        side=hooks.reduce_w_in(d_win_t))
    hooks.reduced(late_outs, w_in_outs)
    grad_x, d_mixw = _rms_bwd(x, mixw, dxn, dh1, name="mix_norm_bwd")

    a_neg = -jnp.exp(w["ssm_a_log"])
    grads = {
        "mix_norm_w": d_mixw[0],
        "w_in_t": d_win_t,
        "gate_bias": d_gbias,
        "gmlp_ln_w": d_lnw, "gmlp_ln_b": d_lnb, "gmlp_ws": d_ws, "gmlp_bs": d_bst[:, :GG].T,
        "ssm_conv_w": d_scw, "ssm_conv_b": d_scb[0],
        "ssm_dt_bias": d_dtb[0, :SH], "ssm_a_log": d_a[0, :SH] * a_neg,
        "ssm_d": d_dch.reshape(SH, SP).sum(axis=-1), "ssm_norm_w": d_snw[0],
        **late,
        "ffn_norm_w": d_ffnw[0],
        "ffn_conv_w": _perm_ffn_cols(d_fcw), "ffn_conv_b": _perm_ffn_cols(d_fcb)[0],
        "ffn_w_down": d_wdown, "final_norm_w": d_finw[0],
    }
    return loss_row[0, 0], grad_x, grads


MESH = pl.DeviceIdType.MESH
HBM_SPEC = pl.BlockSpec(memory_space=pltpu.HBM)


def _axes():
    return lax.axis_index("x"), lax.axis_index("y"), lax.axis_index("c")


def _all_gather(shards, *, name):
    na = len(shards)

    def body(*refs):
        x_refs, out_refs = refs[:na], refs[na:2 * na]
        send_sems, recv_sems, local_sems = refs[2 * na:]
        x, y, c = _axes()
        me, sibling = (x, y, c), (x, y, 1 - c)
        chips = [(1 - x, y), (x, 1 - y), (1 - x, 1 - y)]

        def slot(a, px, py, pc):
            return out_refs[a].at[4 * px + 2 * py + pc]

        def copy(a, k, block, to, src=None):
            return pltpu.make_async_remote_copy(
                src_ref=slot(a, *block) if src is None else src, dst_ref=slot(a, *block),
                send_sem=send_sems.at[7 * a + k], recv_sem=recv_sems.at[7 * a + k], device_id=to, device_id_type=MESH)

        mine = [pltpu.make_async_copy(x_refs[a], slot(a, *me), local_sems.at[a]) for a in range(na)]
        for cp in mine:
            cp.start()
        first = []
        for a in range(na):
            first.append(copy(a, 0, me, sibling, src=x_refs[a]))
            first += [copy(a, 1 + j, me, (*chip, c), src=x_refs[a]) for j, chip in enumerate(chips)]
        for cp in first:
            cp.start()
        passed = []
        for j, chip in enumerate(chips):
            for a in range(na):
                copy(a, 1 + j, (*chip, c), me).wait_recv()
                cp = copy(a, 4 + j, (*chip, c), sibling)
                cp.start()
                passed.append(cp)
        for a in range(na):
            copy(a, 0, sibling, me).wait_recv()
        for j, chip in enumerate(chips):
            for a in range(na):
                copy(a, 4 + j, (*chip, 1 - c), me).wait_recv()
        for cp in first + passed:
            cp.wait_send()
        for cp in mine:
            cp.wait()

    return pl.pallas_call(
        body, out_shape=[jax.ShapeDtypeStruct((NDEV,) + s.shape, s.dtype) for s in shards],
        in_specs=[HBM_SPEC] * na, out_specs=[HBM_SPEC] * na,
        scratch_shapes=[pltpu.SemaphoreType.DMA((7 * na,)), pltpu.SemaphoreType.DMA((7 * na,)),
                        pltpu.SemaphoreType.DMA((na,))],
        name=name)(*shards)


def _exchange(srcs, plan, *, name):
    na = len(srcs)
    n = len(plan(0, 0, 0))

    def body(*refs):
        src_refs, out_refs = refs[:na], refs[na:2 * na]
        send_sems, recv_sems = refs[2 * na:]
        x, y, c = _axes()
        copies = []
        for k, (slab, peer) in enumerate(plan(x, y, c)):
            for a in range(na):
                cp = pltpu.make_async_remote_copy(
                    src_ref=src_refs[a].at[slab], dst_ref=out_refs[a].at[k], send_sem=send_sems.at[n * a + k],
                    recv_sem=recv_sems.at[n * a + k], device_id=peer, device_id_type=MESH)
                cp.start()
                copies.append(cp)
        for cp in copies:
            cp.wait()

    return pl.pallas_call(
        body, out_shape=[jax.ShapeDtypeStruct((n,) + s.shape[1:], s.dtype) for s in srcs],
        in_specs=[HBM_SPEC] * na, out_specs=[HBM_SPEC] * na,
        scratch_shapes=[pltpu.SemaphoreType.DMA((n * na,)), pltpu.SemaphoreType.DMA((n * na,))], name=name)(*srcs)


def _to_sibling_plan(x, y, c):
    return [(2 * q + (1 - c), (x, y, 1 - c)) for q in range(4)]


def _to_chips_plan(x, y, c):
    q = 2 * x + y
    return [(q ^ 2, (1 - x, y, c)), (q ^ 1, (x, 1 - y, c)), (q ^ 3, (1 - x, 1 - y, c))]


def _row_tile(rows, row_bytes, budget=2 * 2 ** 20, align=2 * SUBLANE):
    best = None
    for d in range(align, rows + 1, align):
        if rows % d == 0 and d * row_bytes <= budget:
            best = d
    return best or rows


def _pair_add(g, ra, c_idx, *, name):
    _, _, R, C = g.shape
    tr = _row_tile(R, C * 4)

    def body(c_ref, g_ref, ra_ref, o_ref):
        del c_ref
        o_ref[...] = (g_ref[0].astype(F32) + ra_ref[...].astype(F32)).astype(o_ref.dtype)

    return pl.pallas_call(
        body,
        grid_spec=pltpu.PrefetchScalarGridSpec(
            num_scalar_prefetch=1, grid=(4, R // tr),
            in_specs=[pl.BlockSpec((1, 1, tr, C), lambda q, r, cr: (q, cr[0], r, 0)),
                      pl.BlockSpec((1, tr, C), lambda q, r, cr: (q, r, 0))],
            out_specs=pl.BlockSpec((1, tr, C), lambda q, r, cr: (q, r, 0))),
        out_shape=jax.ShapeDtypeStruct((4, R, C), g.dtype), name=name,
        compiler_params=_params(("arbitrary", "arbitrary"), 24))(c_idx, g, ra)


def _grad_sum(p, rb, q_idx, *, name):
    _, R, C = p.shape
    tr = _row_tile(R, C * 4)

    def body(q_ref, p_ref, rb_ref, o_ref):
        del q_ref
        g = p_ref[0].astype(F32)
        for k in range(3):
            g = g + rb_ref[k].astype(F32)
        o_ref[...] = g

    return pl.pallas_call(
        body,
        grid_spec=pltpu.PrefetchScalarGridSpec(
            num_scalar_prefetch=1, grid=(R // tr,),
            in_specs=[pl.BlockSpec((1, tr, C), lambda r, qr: (qr[0], r, 0)),
                      pl.BlockSpec((3, tr, C), lambda r, qr: (0, r, 0))],
            out_specs=pl.BlockSpec((tr, C), lambda r, qr: (r, 0))),
        out_shape=jax.ShapeDtypeStruct((R, C), F32), name=name,
        compiler_params=_params(("arbitrary",), 32))(q_idx, p, rb)


def _adamw(g, w, m, v):
    m = ADAM_B1 * m + (1.0 - ADAM_B1) * g
    v = ADAM_B2 * v + (1.0 - ADAM_B2) * (g * g)
    m_hat = m / (1.0 - ADAM_B1 ** ADAM_STEP)
    v_hat = v / (1.0 - ADAM_B2 ** ADAM_STEP)
    delta = -ADAM_LR * (m_hat / (jnp.sqrt(v_hat) + ADAM_EPS) + ADAM_WD * w)
    return delta, m, v


def _adam(g, w, m, v, *, name):
    R, C = w.shape
    tr = _row_tile(R, C * 4, budget=2 ** 20, align=SUBLANE)

    def body(g_ref, w_ref, m_ref, v_ref, d_out, m_out, v_out):
        delta, mn, vn = _adamw(g_ref[...], w_ref[...], m_ref[...], v_ref[...])
        d_out[...] = delta
        m_out[...] = mn
        v_out[...] = vn

    row = pl.BlockSpec((tr, C), lambda r: (r, 0))
    o = jax.ShapeDtypeStruct((R, C), F32)
    return pl.pallas_call(
        body, grid=(R // tr,), in_specs=[row, row, row, row], out_specs=[row, row, row], out_shape=[o, o, o],
        name=name, compiler_params=_params(("arbitrary",), 32))(g, w, m, v)


WEIGHTS = ["mix_norm_w", "w_in", "gate_bias", "gmlp_ln_w", "gmlp_ln_b", "gmlp_ws", "gmlp_bs", "ssm_conv_w",
           "ssm_conv_b", "ssm_dt_bias", "ssm_a_log", "ssm_d", "ssm_norm_w", "w_proj_a", "w_proj_b", "w_out",
           "ffn_norm_w", "ffn_w_up", "ffn_conv_w", "ffn_conv_b", "ffn_w_down", "final_norm_w"]
SHARDED = {"w_in": ((D, IN_COLS), 1), "gate_bias": ((2, D), 1), "ssm_conv_w": ((SK, SXBC), 1),
           "w_proj_a": ((GW, D), 0), "w_proj_b": ((SI, D), 0), "w_out": ((D, D), 0),
           "ffn_w_up": ((D, 2 * DFF), 1), "ffn_conv_w": ((FK, 2 * DFF), 1), "ffn_w_down": ((DFF, D), 0)}
REPLICATED = {"mix_norm_w": (D,), "gmlp_ln_w": (GG, GD), "gmlp_ln_b": (GG, GD), "gmlp_ws": (GG, GB, GB),
              "gmlp_bs": (GG, GB), "ssm_conv_b": (SXBC,), "ssm_dt_bias": (SH,), "ssm_a_log": (SH,), "ssm_d": (SH,),
              "ssm_norm_w": (SI,), "ffn_norm_w": (D,), "ffn_conv_b": (2 * DFF,), "final_norm_w": (D,)}
REPL_ORDER = [n for n in WEIGHTS if n in REPLICATED]
BTILE = 2 * SUBLANE
WIN_R = IN_COLS // NDEV
WIN_P = WIN_R + BTILE - WIN_R % BTILE
WIN_A = [WIN_R * d // BTILE * BTILE for d in range(NDEV)]
assert all(WIN_A[d] + WIN_P >= WIN_R * (d + 1) for d in range(NDEV)) and WIN_A[-1] + WIN_P == IN_COLS
BIG = [("w_proj_a", GW // NDEV, False), ("w_proj_b", SI // NDEV, False), ("w_out", D // NDEV, False),
       ("ffn_w_up", 2 * DFF // NDEV, True), ("ffn_w_down", DFF // NDEV, False), ("w_in", WIN_P, True)]
VECTORS = ["gate_bias", "ssm_conv_w", "ffn_conv_w"]


def _size(shape):
    n = 1
    for s in shape:
        n *= s
    return n


def _round_up(n, k):
    return (n + k - 1) // k * k


BIG_OFF = {}
_off = 0
for _n, _r, _t in BIG:
    BIG_OFF[_n] = _off
    _off += _r
BIG_USED = _off
BIG_ROWS = _round_up(BIG_USED, 2 * SUBLANE)
assert all(BIG_OFF[n] % (2 * SUBLANE) == 0 for n, _, _ in BIG)
VEC_SHAPE = {n: (SHARDED[n][0][0], SHARDED[n][0][1] // NDEV) for n in VECTORS}
VEC_ELEMS = sum(_size(VEC_SHAPE[n]) for n in VECTORS)
VEC_ROWS = _round_up(VEC_ELEMS, SUBLANE * LANE) // LANE
REPL_ELEMS = sum(_size(REPLICATED[n]) for n in REPL_ORDER)
REPL_ROWS = _round_up(REPL_ELEMS, NDEV * SUBLANE * LANE) // (NDEV * LANE)
SMALL_ROWS = VEC_ROWS + REPL_ROWS


def _win_offset(dev):
    return WIN_R * dev - WIN_R * dev // BTILE * BTILE


def _pack_big(arrs, dtype, dev):
    parts = []
    for n, r, t in BIG:
        a = (arrs[n].T if t else arrs[n]).astype(dtype)
        if n == "w_in":
            a = lax.dynamic_update_slice(jnp.zeros((WIN_P, D), dtype), a, (_win_offset(dev), 0))
        parts.append(a)
    parts.append(jnp.zeros((BIG_ROWS - BIG_USED, D), dtype))
    return jnp.concatenate(parts, axis=0)


def _join_windows(win):
    parts = []
    for d in range(NDEV):
        lo = BTILE if WIN_A[d] % WIN_R else 0
        if lo:
            parts.append(win[d - 1, WIN_P - BTILE:] + win[d, :BTILE])
        hi = WIN_P - BTILE if d + 1 < NDEV and WIN_A[d + 1] < WIN_A[d] + WIN_P else WIN_P
        parts.append(win[d, lo:hi])
    return jnp.concatenate(parts, axis=0)


def _split_windows(g):
    return jnp.stack([g[a:a + WIN_P] for a in WIN_A])


def _pack_vectors(arrs):
    flat = jnp.concatenate([arrs[n].reshape(-1) for n in VECTORS])
    return jnp.pad(flat, (0, VEC_ROWS * LANE - VEC_ELEMS)).reshape(VEC_ROWS, LANE)


def _unpack_vectors(flat):
    out, off = {}, 0
    v = flat.reshape(-1)
    for n in VECTORS:
        k = _size(VEC_SHAPE[n])
        out[n] = v[off:off + k].reshape((1,) + VEC_SHAPE[n])
        off += k
    return out


def _join_vectors(allv):
    out, off = {}, 0
    v = allv.reshape(NDEV, VEC_ROWS * LANE)
    for n in VECTORS:
        r, c = VEC_SHAPE[n]
        out[n] = v[:, off:off + r * c].reshape(NDEV, r, c).transpose(1, 0, 2).reshape(r, c * NDEV)
        off += r * c
    return out


def _split_vectors(grads):
    parts = []
    for n in VECTORS:
        r, c = VEC_SHAPE[n]
        parts.append(grads[n].reshape(r, NDEV, c).transpose(1, 0, 2).reshape(NDEV, r * c))
    v = jnp.concatenate(parts, axis=1)
    return jnp.pad(v, ((0, 0), (0, VEC_ROWS * LANE - VEC_ELEMS))).reshape(NDEV, VEC_ROWS, LANE)


LATE_ROWS = BIG_OFF["w_in"]
assert LATE_ROWS + WIN_P == BIG_ROWS and BIG[-1][0] == "w_in"


def _remote(src, dst, send_sems, recv_sems, k, to):
    return pltpu.make_async_remote_copy(src_ref=src, dst_ref=dst, send_sem=send_sems.at[k], recv_sem=recv_sems.at[k],
                                        device_id=to, device_id_type=MESH)


class _Exchange:
    def __init__(self, late_shard, c_idx):
        self.late_shard, self.c_idx = late_shard, c_idx

    def gather_start(self):
        shard = self.late_shard

        def make(ins, outs, send_sems, recv_sems):
            (x_ref,), (out,) = ins, outs
            x, y, c = _axes()
            mine = out.at[4 * x + 2 * y + c]
            peers = [(x, y, 1 - c), (1 - x, y, c), (x, 1 - y, c), (1 - x, 1 - y, c)]
            copies = [_remote(x_ref, mine, send_sems, recv_sems, k, p) for k, p in enumerate(peers)]
            return copies + [pltpu.make_async_copy(x_ref, mine, send_sems.at[len(peers)])]

        return _Side([shard], [jax.ShapeDtypeStruct((NDEV,) + shard.shape, shard.dtype)], 5, make)

    def gather_pass_on(self, outs):
        (buf,) = outs

        def make(ins, outs, send_sems, recv_sems):
            (src,), (dst,) = ins, outs
            x, y, c = _axes()
            slots = [4 * px + 2 * py + c for px, py in [(1 - x, y), (x, 1 - y), (1 - x, 1 - y)]]
            return [_remote(src.at[s], dst.at[s], send_sems, recv_sems, k, (x, y, 1 - c)) for k, s in enumerate(slots)]

        return _Side([buf], [jax.ShapeDtypeStruct(buf.shape, buf.dtype)], 3, make, aliases=[(0, 0)])

    def late_weights(self, w, outs):
        (buf,) = outs
        w = dict(w)
        for n, r, t in BIG[:-1]:
            w[n + "_t" if t else n] = buf[:, BIG_OFF[n]:BIG_OFF[n] + r].reshape(NDEV * r, D)
        return w

    def _reduce(self, send, tag):
        (sib,) = _exchange([send], _to_sibling_plan, name=tag + "_grads_to_sibling")
        sums = _pair_add(send.reshape((4, 2) + send.shape[1:]), sib, self.c_idx, name=tag + "_grad_pair_add")

        def make(ins, outs, send_sems, recv_sems):
            (src,), (dst,) = ins, outs
            return [_remote(src.at[slab], dst.at[k], send_sems, recv_sems, k, peer)
                    for k, (slab, peer) in enumerate(_to_chips_plan(*_axes()))]

        return sums, _Side([sums], [jax.ShapeDtypeStruct((3,) + sums.shape[1:], sums.dtype)], 3, make)

    def reduce_late(self, grads):
        send = jnp.concatenate([grads[n + "_t" if t else n].reshape(NDEV, r, D) for n, r, t in BIG[:-1]], axis=1)
        self.late_sum, side = self._reduce(send.astype(BF16), "late")
        return side

    def reduce_w_in(self, grad):
        self.w_in_sum, side = self._reduce(_split_windows(grad).astype(BF16), "w_in")
        return side

    def reduced(self, late_outs, w_in_outs):
        (self.late_from_chips,), (self.w_in_from_chips,) = late_outs, w_in_outs


def _pack_repl(arrs):
    flat = jnp.concatenate([arrs[n].reshape(-1) for n in REPL_ORDER])
    return jnp.pad(flat, (0, NDEV * REPL_ROWS * LANE - REPL_ELEMS)).reshape(NDEV * REPL_ROWS, LANE)


def _unpack_repl(flat, shapes):
    out, off = {}, 0
    v = flat.reshape(-1)
    for n in REPL_ORDER:
        k = _size(REPLICATED[n])
        out[n] = v[off:off + k].reshape(shapes[n])
        off += k
    return out


def kernel(x, mix_norm_w, w_in, gate_bias, gmlp_ln_w, gmlp_ln_b, gmlp_ws, gmlp_bs, ssm_conv_w, ssm_conv_b, ssm_dt_bias, ssm_a_log, ssm_d, ssm_norm_w, w_proj_a, w_proj_b, w_out, ffn_norm_w, ffn_w_up, ffn_conv_w, ffn_conv_b, ffn_w_down, final_norm_w, loss_target, m_mix_norm_w, m_w_in, m_gate_bias, m_gmlp_ln_w, m_gmlp_ln_b, m_gmlp_ws, m_gmlp_bs, m_ssm_conv_w, m_ssm_conv_b, m_ssm_dt_bias, m_ssm_a_log, m_ssm_d, m_ssm_norm_w, m_w_proj_a, m_w_proj_b, m_w_out, m_ffn_norm_w, m_ffn_w_up, m_ffn_conv_w, m_ffn_conv_b, m_ffn_w_down, m_final_norm_w, v_mix_norm_w, v_w_in, v_gate_bias, v_gmlp_ln_w, v_gmlp_ln_b, v_gmlp_ws, v_gmlp_bs, v_ssm_conv_w, v_ssm_conv_b, v_ssm_dt_bias, v_ssm_a_log, v_ssm_d, v_ssm_norm_w, v_w_proj_a, v_w_proj_b, v_w_out, v_ffn_norm_w, v_ffn_w_up, v_ffn_conv_w, v_ffn_conv_b, v_ffn_w_down, v_final_norm_w):
    given = dict(locals())
    wts = {n: given[n] for n in WEIGHTS}
    mom = {n: given["m_" + n] for n in WEIGHTS}
    var = {n: given["v_" + n] for n in WEIGHTS}
    xi, yi, ci = _axes()
    c_idx = jnp.reshape(ci, (1,)).astype(jnp.int32)
    q_idx = jnp.reshape(2 * xi + yi, (1,)).astype(jnp.int32)
    big_names = [n for n, _, _ in BIG]
    drop = lambda d, names: {n: d[n][0] for n in names}

    dev = 4 * xi + 2 * yi + ci
    packed = _pack_big(drop(wts, big_names), BF16, dev)
    all_win, all_vec = _all_gather([packed[LATE_ROWS:], _pack_vectors(wts)], name="w_in_all_gather")
    full = {"w_in_t": _join_windows(all_win)}
    full.update(_join_vectors(all_vec))
    for n in REPL_ORDER:
        full[n] = wts[n].reshape(REPLICATED[n])

    hooks = _Exchange(packed[:LATE_ROWS], c_idx)
    loss_local, grad_x, grads = _local_step(x[0], loss_target[0], full, hooks)
    g_late = _grad_sum(hooks.late_sum, hooks.late_from_chips, q_idx, name="late_grad_sum")
    g_win = _grad_sum(hooks.w_in_sum, hooks.w_in_from_chips, q_idx, name="w_in_grad_sum")

    send_small = jnp.concatenate([_split_vectors(grads), _pack_repl(grads).reshape(NDEV, REPL_ROWS, LANE)], axis=1)
    (sib_small,) = _exchange([send_small], _to_sibling_plan, name="small_grads_to_sibling")
    sum_small = _pair_add(send_small.reshape(4, 2, SMALL_ROWS, LANE), sib_small, c_idx, name="small_grad_pair_add")
    (chips_small,) = _exchange([sum_small], _to_chips_plan, name="small_grads_to_chips")
    g_small = _grad_sum(sum_small, chips_small, q_idx, name="small_grad_sum")

    outs = {}
    for n, r, t in BIG:
        if n == "w_in":
            g = lax.dynamic_slice(g_win, (_win_offset(dev), 0), (WIN_R, D))
        else:
            g = g_late[BIG_OFF[n]:BIG_OFF[n] + r]
        g = g.T if t else g
        outs[n] = tuple(a[None] for a in (g,) + tuple(_adam(g, wts[n][0], mom[n][0], var[n][0], name="adam_" + n)))
    g_vec = g_small[:VEC_ROWS]
    vec_out = (g_vec,) + tuple(_adam(g_vec, _pack_vectors(wts), _pack_vectors(mom), _pack_vectors(var),
                                     name="adam_vectors"))
    vec_out = [_unpack_vectors(a) for a in vec_out]
    for n in VECTORS:
        outs[n] = tuple(a[n] for a in vec_out)

    g_repl = _all_gather([g_small[VEC_ROWS:]], name="replicated_grads_all_gather")[0].reshape(NDEV * REPL_ROWS, LANE)
    repl_out = (g_repl,) + tuple(_adam(g_repl, _pack_repl(wts), _pack_repl(mom), _pack_repl(var),
                                       name="adam_replicated"))
    shapes = {n: wts[n].shape for n in WEIGHTS}
    repl_out = [_unpack_repl(a, shapes) for a in repl_out]
    for n in REPL_ORDER:
        outs[n] = tuple(a[n] for a in repl_out)

    loss = lax.psum(loss_local, ("x", "y", "c"))
    return (loss, grad_x[None]) + tuple(outs[n][k] for k in range(4) for n in WEIGHTS)
```

```python
import functools

import jax
import jax.numpy as jnp
from jax import lax
from jax.experimental import pallas as pl
from jax.experimental.pallas import tpu as pltpu

F32 = jnp.float32
BF16 = jnp.bfloat16

D = 1024
EPS = 1e-5
GW = 1024
GB = 128
GG = 8
GD = 128
GCH = 64
SI = 2048
SH = 32
SP = 64
SG = 4
SN = 128
SGW = SI // SG
SK = 4
SXBC = SI + 2 * SG * SN
DFF = 2816
FK = 3
PMAIN = 2 * D + 2 * GW + SI + SXBC
IN_COLS = PMAIN + SH
NDEV = 8
ADAM_LR, ADAM_B1, ADAM_B2, ADAM_EPS, ADAM_WD, ADAM_STEP = 0.001, 0.9, 0.999, 1e-08, 0.01, 10

LANE = 128
SUBLANE = 8
VMEM_MB_V7X = 64
VMEM_CAP_MB = 56

LS = 128
FT = DFF // 2

NN = (((1,), (0,)), ((), ()))
NT = (((1,), (1,)), ((), ()))
TN = (((0,), (0,)), ((), ()))


def _params(sem, vmem_mb):
    return pltpu.CompilerParams(dimension_semantics=sem,
                                vmem_limit_bytes=min(int(vmem_mb), VMEM_CAP_MB) * 1024 * 1024)


def _dot(a, b, dims=NN):
    return lax.dot_general(a, b, dims, preferred_element_type=F32)


def _sigmoid(x):
    return 1.0 / (1.0 + jnp.exp(-x))


def _split3(v):
    hi = v.astype(BF16)
    r = v - hi.astype(F32)
    mid = r.astype(BF16)
    lo = (r - mid.astype(F32)).astype(BF16)
    return hi, mid, lo


def _dot3(a_f32, b_bf16, dims):
    hi, mid, lo = _split3(a_f32)
    return _dot(hi, b_bf16, dims) + _dot(mid, b_bf16, dims) + _dot(lo, b_bf16, dims)


def _dot3_rhs(a_bf16, b_f32, dims):
    hi, mid, lo = _split3(b_f32)
    return _dot(a_bf16, hi, dims) + _dot(a_bf16, mid, dims) + _dot(a_bf16, lo, dims)


def _matmul(a, b, *, name, out_dtype, ta=False, tb=False, tm=1024, tn=1024, tk=1024, add=None,
            j_outer=False, b_rows=None, side=None):
    if ta:
        K, M = a.shape
    else:
        M, K = a.shape
    if tb:
        N, K2 = b.shape
        N = b_rows or N
    else:
        K2, N = b.shape
        K2 = b_rows or K2
    assert K == K2, (a.shape, b.shape, ta, tb)
    tm, tn, tk = min(tm, M), min(tn, N), min(tk, K)
    assert M % tm == 0 and N % tn == 0 and K % tk == 0, (M, N, K, tm, tn, tk)
    nk = K // tk
    dims = (((0 if ta else 1,), (1 if tb else 0,)), ((), ()))
    has_add = add is not None
    n_in = 3 if has_add else 2
    s_in = len(side.inputs) if side else 0
    s_out = len(side.out_shapes) if side else 0
    grid = (N // tn, M // tm, nk) if j_outer else (M // tm, N // tn, nk)

    def body(*refs):
        a_ref, b_ref = refs[:2]
        add_ref = refs[2] if has_add else None
        o_ref = refs[n_in + s_in]
        if side:
            side_refs = (refs[n_in:n_in + s_in], refs[n_in + s_in + 1:n_in + s_in + 1 + s_out]) + tuple(refs[-2:])
            ids = [pl.program_id(d) for d in range(3)]
            first = functools.reduce(jnp.logical_and, [i == 0 for i in ids])
            last = functools.reduce(jnp.logical_and, [i == g - 1 for i, g in zip(ids, grid)])

            @pl.when(first)
            def _():
                for cp in side.make(*side_refs):
                    cp.start()

            @pl.when(last)
            def _():
                for cp in side.make(*side_refs):
                    cp.wait()

        p = lax.dot_general(a_ref[...].astype(BF16), b_ref[...].astype(BF16), dims,
                            preferred_element_type=F32)

        def finish(acc):
            if has_add:
                acc = acc + add_ref[...].astype(F32)
            o_ref[...] = acc.astype(o_ref.dtype)

        if nk == 1:
            finish(p)
        else:
            acc_ref = refs[n_in + s_in + 1 + s_out]
            k = pl.program_id(2)

            @pl.when(k == 0)
            def _():
                acc_ref[...] = p

            @pl.when(jnp.logical_and(k > 0, k < nk - 1))
            def _():
                acc_ref[...] += p

            @pl.when(k == nk - 1)
            def _():
                finish(acc_ref[...] + p)

    if j_outer:
        ij = lambda g0, g1: (g1, g0)
    else:
        ij = lambda g0, g1: (g0, g1)

    def a_map(g0, g1, k):
        i, _ = ij(g0, g1)
        return (k, i) if ta else (i, k)

    def b_map(g0, g1, k):
        _, j = ij(g0, g1)
        return (j, k) if tb else (k, j)

    def o_map(g0, g1, k):
        return ij(g0, g1)

    in_specs = [pl.BlockSpec((tk, tm) if ta else (tm, tk), a_map),
                pl.BlockSpec((tn, tk) if tb else (tk, tn), b_map)]
    args = [a, b]
    if has_add:
        in_specs.append(pl.BlockSpec((tm, tn), o_map))
        args.append(add)
    scratch = [pltpu.VMEM((tm, tn), F32)] if nk > 1 else []
    osz = jnp.dtype(out_dtype).itemsize
    est = (2 * (tm * tk * a.dtype.itemsize + tk * tn * b.dtype.itemsize) + 2 * tm * tn * osz
           + (2 * tm * tn * add.dtype.itemsize if has_add else 0)
           + 3 * tm * tn * 4 + 2 * (tm * tk + tk * tn)) / 2 ** 20 + 4
    out_specs = [pl.BlockSpec((tm, tn), o_map)]
    out_shape = [jax.ShapeDtypeStruct((M, N), out_dtype)]
    aliases = {}
    if side:
        hbm = pl.BlockSpec(memory_space=pltpu.HBM)
        in_specs += [hbm] * s_in
        args += list(side.inputs)
        out_specs += [hbm] * s_out
        out_shape += list(side.out_shapes)
        scratch += [pltpu.SemaphoreType.DMA((side.nsem,)), pltpu.SemaphoreType.DMA((side.nsem,))]
        aliases = {n_in + i: 1 + j for i, j in side.aliases}
    outs = pl.pallas_call(
        body, grid=grid, in_specs=in_specs, out_specs=out_specs, out_shape=out_shape, scratch_shapes=scratch,
        input_output_aliases=aliases, name=name,
        compiler_params=_params(("arbitrary", "arbitrary", "arbitrary"), est))(*args)
    return (outs[0], list(outs[1:])) if side else outs[0]


class _Side:
    def __init__(self, inputs, out_shapes, nsem, make, aliases=()):
        self.inputs, self.out_shapes, self.nsem, self.make, self.aliases = inputs, out_shapes, nsem, make, aliases


def _rms_fwd(x, w, *, name):
    T = x.shape[0]
    tm = min(512, T)

    def body(x_ref, w_ref, o_ref):
        xv = x_ref[...]
        r = lax.rsqrt(jnp.mean(xv * xv, axis=-1, keepdims=True) + EPS)
        o_ref[...] = (xv * r * w_ref[...]).astype(BF16)

    return pl.pallas_call(
        body, grid=(T // tm,),
        in_specs=[pl.BlockSpec((tm, D), lambda i: (i, 0)), pl.BlockSpec((1, D), lambda i: (0, 0))],
        out_specs=pl.BlockSpec((tm, D), lambda i: (i, 0)),
        out_shape=jax.ShapeDtypeStruct((T, D), BF16), name=name,
        compiler_params=_params(("arbitrary",), 24))(x, w)


def _rms_bwd(x, w, dy, dres, *, name):
    T = x.shape[0]
    tm = min(512, T)

    def body(x_ref, w_ref, dy_ref, dres_ref, dx_ref, dw_ref):
        xv = x_ref[...]
        r = lax.rsqrt(jnp.mean(xv * xv, axis=-1, keepdims=True) + EPS)
        xhat = xv * r
        dyv = dy_ref[...].astype(F32)
        g = dyv * w_ref[...]
        dx_ref[...] = dres_ref[...] + r * (g - xhat * jnp.mean(g * xhat, axis=-1, keepdims=True))
        part = jnp.sum(dyv * xhat, axis=0, keepdims=True)

        @pl.when(pl.program_id(0) == 0)
        def _():
            dw_ref[...] = part

        @pl.when(pl.program_id(0) > 0)
        def _():
            dw_ref[...] += part

    row = pl.BlockSpec((tm, D), lambda i: (i, 0))
    vec = pl.BlockSpec((1, D), lambda i: (0, 0))
    return pl.pallas_call(
        body, grid=(T // tm,), in_specs=[row, vec, row, row], out_specs=[row, vec],
        out_shape=[jax.ShapeDtypeStruct((T, D), F32), jax.ShapeDtypeStruct((1, D), F32)], name=name,
        compiler_params=_params(("arbitrary",), 32))(x, w, dy, dres)


def _loss_head(h, tgt, w):
    T = h.shape[0]
    tm = min(512, T)

    def body(h_ref, t_ref, w_ref, loss_ref, dh_ref, dw_ref):
        hv = h_ref[...]
        r = lax.rsqrt(jnp.mean(hv * hv, axis=-1, keepdims=True) + EPS)
        xhat = hv * r
        wv = w_ref[...]
        err = xhat * wv - t_ref[...]
        lpart = 0.5 * jnp.sum(jnp.mean(err * err, axis=-1, keepdims=True), axis=0, keepdims=True)
        dy = err * (1.0 / D)
        g = dy * wv
        dh_ref[...] = r * (g - xhat * jnp.mean(g * xhat, axis=-1, keepdims=True))
        wpart = jnp.sum(dy * xhat, axis=0, keepdims=True)
        lrow = jnp.broadcast_to(lpart, (1, LANE))

        @pl.when(pl.program_id(0) == 0)
        def _():
            dw_ref[...] = wpart
            loss_ref[...] = lrow

        @pl.when(pl.program_id(0) > 0)
        def _():
            dw_ref[...] += wpart
            loss_ref[...] += lrow

    row = pl.BlockSpec((tm, D), lambda i: (i, 0))
    vec = pl.BlockSpec((1, D), lambda i: (0, 0))
    return pl.pallas_call(
        body, grid=(T // tm,), in_specs=[row, row, vec],
        out_specs=[pl.BlockSpec((1, LANE), lambda i: (0, 0)), row, vec],
        out_shape=[jax.ShapeDtypeStruct((1, LANE), F32), jax.ShapeDtypeStruct((T, D), F32),
                   jax.ShapeDtypeStruct((1, D), F32)], name="loss_head",
        compiler_params=_params(("arbitrary",), 32))(h, tgt, w)


_GELU_C = 0.7978845608028654
_GELU_A = 0.044715


def _gelu(x):
    t = jnp.tanh(_GELU_C * (x + _GELU_A * x * x * x))
    return 0.5 * x * (1.0 + t), t


def _gelu_grad(x, t):
    return 0.5 * (1.0 + t) + 0.5 * x * (1.0 - t * t) * _GELU_C * (1.0 + 3.0 * _GELU_A * x * x)


def _gmlp_mask():
    r = lax.broadcasted_iota(jnp.int32, (GB, GB), 0) // GCH
    c = lax.broadcasted_iota(jnp.int32, (GB, GB), 1) // GCH
    return c <= r


def _gmlp_fwd(proj, lnw, lnb, ws, bst):
    T = proj.shape[0]
    tm = min(512, T)
    nblk = tm // GB

    def body(u_ref, v_ref, lnw_ref, lnb_ref, ws_ref, bst_ref, o_ref):
        mask = _gmlp_mask()
        u, _ = _gelu(u_ref[...].astype(F32))
        v, _ = _gelu(v_ref[...].astype(F32))
        for g in range(GG):
            cs = slice(g * GD, (g + 1) * GD)
            vg = v[:, cs]
            mu = jnp.mean(vg, axis=-1, keepdims=True)
            vc = vg - mu
            var = jnp.mean(vc * vc, axis=-1, keepdims=True)
            vn = (vc * lax.rsqrt(var + EPS) * lnw_ref[g:g + 1, :] + lnb_ref[g:g + 1, :]).astype(BF16)
            wsg = jnp.where(mask, ws_ref[g], 0.0).astype(BF16)
            bcol = bst_ref[:, g:g + 1]
            for blk in range(nblk):
                rs = slice(blk * GB, (blk + 1) * GB)
                sv = _dot(wsg, vn[rs, :]) + bcol
                o_ref[rs, cs] = (u[rs, cs] * sv).astype(BF16)

    full = lambda shape: pl.BlockSpec(shape, lambda i: tuple(0 for _ in shape))
    return pl.pallas_call(
        body, grid=(T // tm,),
        in_specs=[pl.BlockSpec((tm, GW), lambda i: (i, 2)), pl.BlockSpec((tm, GW), lambda i: (i, 3)),
                  full((GG, GD)), full((GG, GD)), full((GG, GB, GB)), full((GB, GG))],
        out_specs=pl.BlockSpec((tm, GW), lambda i: (i, 0)),
        out_shape=jax.ShapeDtypeStruct((T, GW), BF16), name="gmlp_fwd",
        compiler_params=_params(("arbitrary",), 40))(proj, proj, lnw, lnb, ws, bst)


def _gmlp_bwd(proj, dya, dproj, lnw, lnb, ws, bst):
    T = proj.shape[0]
    tm = min(512, T)
    nblk = tm // GB

    def body(u_ref, v_ref, dya_ref, dproj_in, lnw_ref, lnb_ref, ws_ref, bst_ref,
             dz_ref, dlnw_ref, dlnb_ref, dws_ref, dbst_ref):
        del dproj_in
        first = pl.program_id(0) == 0

        @pl.when(first)
        def _():
            dlnw_ref[...] = jnp.zeros_like(dlnw_ref)
            dlnb_ref[...] = jnp.zeros_like(dlnb_ref)
            dws_ref[...] = jnp.zeros_like(dws_ref)
            dbst_ref[...] = jnp.zeros_like(dbst_ref)

        mask = _gmlp_mask()
        lane = lax.broadcasted_iota(jnp.int32, (GB, LANE), 1)
        ur = u_ref[...].astype(F32)
        vr = v_ref[...].astype(F32)
        u, tu = _gelu(ur)
        v, tv = _gelu(vr)
        gu = _gelu_grad(ur, tu)
        gv = _gelu_grad(vr, tv)
        dy = dya_ref[...].astype(F32)
        dbst = jnp.zeros((GB, LANE), F32)
        dlnw_rows, dlnb_rows = [], []
        for g in range(GG):
            cs = slice(g * GD, (g + 1) * GD)
            vg = v[:, cs]
            mu = jnp.mean(vg, axis=-1, keepdims=True)
            vc = vg - mu
            var = jnp.mean(vc * vc, axis=-1, keepdims=True)
            rstd = lax.rsqrt(var + EPS)
            xhat = vc * rstd
            lw = lnw_ref[g:g + 1, :]
            vn = (xhat * lw + lnb_ref[g:g + 1, :]).astype(BF16)
            wsg = jnp.where(mask, ws_ref[g], 0.0).astype(BF16)
            bcol = bst_ref[:, g:g + 1]
            dyg = dy[:, cs]
            ug = u[:, cs]
            dsv = dyg * ug
            dsv_b = dsv.astype(BF16)
            dws_g = jnp.zeros((GB, GB), F32)
            bsum = jnp.zeros((GB, 1), F32)
            dvn_parts = []
            for blk in range(nblk):
                rs = slice(blk * GB, (blk + 1) * GB)
                sv = _dot(wsg, vn[rs, :]) + bcol
                dz_ref[rs, cs] = (dyg[rs, :] * sv * gu[rs, cs]).astype(BF16)
                dws_g = dws_g + _dot(dsv_b[rs, :], vn[rs, :], NT)
                bsum = bsum + jnp.sum(dsv[rs, :], axis=-1, keepdims=True)
                dvn_parts.append(_dot(wsg, dsv_b[rs, :], TN))
            dvn = jnp.concatenate(dvn_parts, axis=0)
            dws_ref[g] += jnp.where(mask, dws_g, 0.0)
            dbst = dbst + jnp.where(lane == g, bsum, 0.0)
            dlnw_rows.append(jnp.sum(dvn * xhat, axis=0, keepdims=True))
            dlnb_rows.append(jnp.sum(dvn, axis=0, keepdims=True))
            dxh = dvn * lw
            dvg = rstd * (dxh - jnp.mean(dxh, axis=-1, keepdims=True)
                          - xhat * jnp.mean(dxh * xhat, axis=-1, keepdims=True))
            dz_ref[:, GW + g * GD:GW + (g + 1) * GD] = (dvg * gv[:, cs]).astype(BF16)
        dlnw_ref[...] += jnp.concatenate(dlnw_rows, axis=0)
        dlnb_ref[...] += jnp.concatenate(dlnb_rows, axis=0)
        dbst_ref[...] += dbst

    full = lambda shape: pl.BlockSpec(shape, lambda i: tuple(0 for _ in shape))
    outs = pl.pallas_call(
        body, grid=(T // tm,),
        in_specs=[pl.BlockSpec((tm, GW), lambda i: (i, 2)), pl.BlockSpec((tm, GW), lambda i: (i, 3)),
                  pl.BlockSpec((tm, GW), lambda i: (i, 0)), pl.BlockSpec(memory_space=pl.ANY),
                  full((GG, GD)), full((GG, GD)), full((GG, GB, GB)), full((GB, GG))],
        out_specs=[pl.BlockSpec((tm, 2 * GW), lambda i: (i, 1)), full((GG, GD)), full((GG, GD)),
                   full((GG, GB, GB)), full((GB, LANE))],
        out_shape=[jax.ShapeDtypeStruct(dproj.shape, dproj.dtype), jax.ShapeDtypeStruct((GG, GD), F32),
                   jax.ShapeDtypeStruct((GG, GD), F32), jax.ShapeDtypeStruct((GG, GB, GB), F32),
                   jax.ShapeDtypeStruct((GB, LANE), F32)],
        input_output_aliases={3: 0}, name="gmlp_bwd",
        compiler_params=_params(("arbitrary",), 48))(proj, proj, dya, dproj, lnw, lnb, ws, bst)
    return outs


def _merge_fwd(ya_pre, yb_pre, proj, bias, wpa, wpb):
    T = proj.shape[0]
    tm = min(512, T)

    def body(ya_ref, yb_ref, g_ref, b_ref, wpa_ref, wpb_ref, m_ref, oa_ref, ob_ref):
        ya = _dot(ya_ref[...], wpa_ref[...])
        yb = _dot(yb_ref[...], wpb_ref[...])
        g = g_ref[...].astype(F32)
        sa = _sigmoid(g[:, :D] + b_ref[0:1, :])
        sb = _sigmoid(g[:, D:] + b_ref[1:2, :])
        m_ref[...] = (sa * ya + sb * yb).astype(BF16)
        oa_ref[...] = ya.astype(BF16)
        ob_ref[...] = yb.astype(BF16)

    row = lambda w: pl.BlockSpec((tm, w), lambda i: (i, 0))
    full = lambda shape: pl.BlockSpec(shape, lambda i: tuple(0 for _ in shape))
    o = jax.ShapeDtypeStruct((T, D), BF16)
    return pl.pallas_call(
        body, grid=(T // tm,),
        in_specs=[row(GW), row(SI), row(2 * D), full((2, D)), full((GW, D)), full((SI, D))],
        out_specs=[row(D), row(D), row(D)], out_shape=[o, o, o], name="merge_fwd",
        compiler_params=_params(("arbitrary",), 40))(ya_pre, yb_pre, proj, bias, wpa, wpb)


def _merge_bwd(dm, proj, bias, ya, yb, wpa, wpb):
    T = proj.shape[0]
    tm = min(512, T)

    def body(dm_ref, g_ref, b_ref, ya_ref, yb_ref, wpa_ref, wpb_ref,
             dg_ref, dya_ref, dyb_ref, dpa_ref, dpb_ref, db_ref):
        dmv = dm_ref[...].astype(F32)
        g = g_ref[...].astype(F32)
        sa = _sigmoid(g[:, :D] + b_ref[0:1, :])
        sb = _sigmoid(g[:, D:] + b_ref[1:2, :])
        dya = (dmv * sa).astype(BF16)
        dyb = (dmv * sb).astype(BF16)
        dga = dmv * ya_ref[...].astype(F32) * sa * (1.0 - sa)
        dgb = dmv * yb_ref[...].astype(F32) * sb * (1.0 - sb)
        dg_ref[:, :D] = dga.astype(BF16)
        dg_ref[:, D:] = dgb.astype(BF16)
        dya_ref[...] = dya
        dyb_ref[...] = dyb
        dpa_ref[...] = _dot(dya, wpa_ref[...], NT).astype(BF16)
        dpb_ref[...] = _dot(dyb, wpb_ref[...], NT).astype(BF16)
        part = jnp.concatenate([jnp.sum(dga, axis=0, keepdims=True), jnp.sum(dgb, axis=0, keepdims=True)], axis=0)

        @pl.when(pl.program_id(0) == 0)
        def _():
            db_ref[...] = part

        @pl.when(pl.program_id(0) > 0)
        def _():
            db_ref[...] += part

    row = lambda w: pl.BlockSpec((tm, w), lambda i: (i, 0))
    full = lambda shape: pl.BlockSpec(shape, lambda i: tuple(0 for _ in shape))
    o = lambda w: jax.ShapeDtypeStruct((T, w), BF16)
    return pl.pallas_call(
        body, grid=(T // tm,),
        in_specs=[row(D), row(2 * D), full((2, D)), row(D), row(D), full((GW, D)), full((SI, D))],
        out_specs=[row(2 * D), row(D), row(D), row(GW), row(SI), full((2, D))],
        out_shape=[o(PMAIN), o(D), o(D), o(GW), o(SI), jax.ShapeDtypeStruct((2, D), F32)], name="merge_bwd",
        compiler_params=_params(("arbitrary",), 48))(dm, proj, bias, ya, yb, wpa, wpb)


RB = 128


def _shift_matrix(j):
    r = lax.broadcasted_iota(jnp.int32, (RB, RB), 0)
    c = lax.broadcasted_iota(jnp.int32, (RB, RB), 1)
    return jnp.where(c == r - j, 1.0, 0.0).astype(BF16)


def _rows_down(xb, before, shifts):
    H = SUBLANE
    mats = [_shift_matrix(j) for j in shifts]
    outs = [[] for _ in shifts]
    for b in range(xb.shape[0] // RB):
        blk = xb[b * RB:(b + 1) * RB]
        edge = jnp.concatenate([before, blk[:2 * H].astype(F32)[:H]], axis=0)
        for i, j in enumerate(shifts):
            outs[i] += [edge[H - j:2 * H - j], _dot(mats[i], blk)[H:]]
        before = blk[RB - 2 * H:].astype(F32)[H:]
    return [jnp.concatenate(o, axis=0) for o in outs]


def _rows_up(xb, after, shifts):
    H = SUBLANE
    nb = xb.shape[0] // RB
    mats = [_shift_matrix(-j) for j in shifts]
    outs = [[] for _ in shifts]
    for b in range(nb):
        blk = xb[b * RB:(b + 1) * RB]
        nxt = xb[(b + 1) * RB:(b + 1) * RB + 2 * H].astype(F32)[:H] if b + 1 < nb else after
        edge = jnp.concatenate([blk[RB - 2 * H:].astype(F32)[H:], nxt], axis=0)
        for i, j in enumerate(shifts):
            outs[i] += [_dot(mats[i], blk)[:RB - H], edge[j:H + j]]
    return [jnp.concatenate(o, axis=0) for o in outs]


def _ffn_act_fwd(up, cw, cb):
    T = up.shape[0]
    tm = min(512, T)
    H = SUBLANE

    def body(up_ref, cw_ref, cb_ref, o_ref, xc_ref, halo):
        @pl.when(pl.program_id(1) == 0)
        def _():
            halo[...] = jnp.zeros_like(halo)

        xb = up_ref[...]
        x2, x1 = _rows_down(xb, halo[...], (2, 1))
        xc = cb_ref[...] + cw_ref[0:1, :] * x2 + cw_ref[1:2, :] * x1 + cw_ref[2:3, :] * xb.astype(F32)
        xc_ref[...] = xc.astype(BF16)
        gate = xc[:, :FT]
        o_ref[...] = (gate * _sigmoid(gate) * xc[:, FT:]).astype(BF16)
        halo[...] = xb[tm - 2 * H:].astype(F32)[H:]

    tile = pl.BlockSpec((tm, 2 * FT), lambda j, i: (i, j))
    return pl.pallas_call(
        body, grid=(2, T // tm),
        in_specs=[tile, pl.BlockSpec((FK, 2 * FT), lambda j, i: (0, j)), pl.BlockSpec((1, 2 * FT), lambda j, i: (0, j))],
        out_specs=[pl.BlockSpec((tm, FT), lambda j, i: (i, j)), tile],
        out_shape=[jax.ShapeDtypeStruct((T, DFF), BF16), jax.ShapeDtypeStruct((T, 2 * DFF), BF16)],
        scratch_shapes=[pltpu.VMEM((H, 2 * FT), F32)], name="ffn_act_fwd",
        compiler_params=_params(("arbitrary", "arbitrary"), 48))(up, cw, cb)


def _ffn_act_bwd(up, xc, dact, cw):
    T = up.shape[0]
    tm = min(512, T)
    nt = T // tm
    H = SUBLANE

    def body(up_ref, xc_ref, da_ref, cw_ref, dup_ref, dcw_ref, dcb_ref, ahead):
        @pl.when(pl.program_id(1) == 0)
        def _():
            ahead[...] = jnp.zeros_like(ahead)
            dcw_ref[...] = jnp.zeros_like(dcw_ref)
            dcb_ref[...] = jnp.zeros_like(dcb_ref)

        xcv = xc_ref[...].astype(F32)
        gate, val = xcv[:, :FT], xcv[:, FT:]
        sg = _sigmoid(gate)
        dav = da_ref[...].astype(F32)
        dgate = dav * val * sg * (1.0 + gate * (1.0 - sg))
        dval = dav * gate * sg
        dxc = jnp.concatenate([dgate, dval], axis=1)
        d1, d2 = _rows_up(dxc.astype(BF16), ahead[...], (1, 2))
        x = up_ref[...].astype(F32)
        dcb_ref[...] += jnp.sum(dxc, axis=0, keepdims=True)
        dcw_ref[...] += jnp.concatenate([jnp.sum(d * x, axis=0, keepdims=True) for d in (d2, d1, dxc)], axis=0)
        dup_ref[...] = (cw_ref[2:3, :] * dxc + cw_ref[1:2, :] * d1 + cw_ref[0:1, :] * d2).astype(BF16)
        ahead[...] = dxc[0:H, :]

    tile = pl.BlockSpec((tm, 2 * FT), lambda j, i: (nt - 1 - i, j))
    return pl.pallas_call(
        body, grid=(2, nt),
        in_specs=[tile, tile, pl.BlockSpec((tm, FT), lambda j, i: (nt - 1 - i, j)),
                  pl.BlockSpec((FK, 2 * FT), lambda j, i: (0, j))],
        out_specs=[tile, pl.BlockSpec((FK, 2 * FT), lambda j, i: (0, j)), pl.BlockSpec((1, 2 * FT), lambda j, i: (0, j))],
        out_shape=[jax.ShapeDtypeStruct((T, 2 * DFF), BF16), jax.ShapeDtypeStruct((FK, 2 * DFF), F32),
                   jax.ShapeDtypeStruct((1, 2 * DFF), F32)],
        scratch_shapes=[pltpu.VMEM((H, 2 * FT), F32)], name="ffn_act_bwd",
        compiler_params=_params(("arbitrary", "arbitrary"), 56))(up, xc, dact, cw)


def _softplus(x):
    e = jnp.exp(-jnp.abs(x))
    return jnp.maximum(x, 0.0) + jnp.where(e < 1e-4, e * (1.0 - 0.5 * e), jnp.log(1.0 + e))


def _ssd_consts():
    li = lax.broadcasted_iota(jnp.int32, (LS, LS), 0)
    si = lax.broadcasted_iota(jnp.int32, (LS, LS), 1)
    tril = si <= li
    hh = lax.broadcasted_iota(jnp.int32, (LANE, SI), 0)
    cc = lax.broadcasted_iota(jnp.int32, (LANE, SI), 1) // SP
    expand = jnp.where(hh == cc, 1.0, 0.0).astype(BF16)
    return tril, expand


def _ssd_pre(xc, dt_ref, dtb_ref, alog_ref, tril, expand):
    sx = _sigmoid(xc)
    xbc = xc * sx
    xs, bm, cm = xbc[:, :SI], xbc[:, SI:SI + SG * SN], xbc[:, SI + SG * SN:]
    dtin = dt_ref[...] + dtb_ref[...]
    dt = _softplus(dtin)
    a_neg = -jnp.exp(alog_ref[...])
    dta = dt * a_neg
    trilb = jnp.where(tril, 1.0, 0.0).astype(BF16)
    a = _dot3_rhs(trilb, dta, NN)
    a_exp = _dot3(a, expand, NN)
    dt_exp = _dot3(dt, expand, NN)
    xdt = xs * dt_exp
    a_last = a_exp[LS - 1:LS, :]
    return dict(xc=xc, sx=sx, xs=xs, bm=bm, cm=cm, dtin=dt_ref[...] + dtb_ref[...], dt=dt, a_neg=a_neg,
                a=a, a_t=a.T, a_exp=a_exp, dt_exp=dt_exp, xdt=xdt, ea=jnp.exp(a_exp),
                w=jnp.exp(a_last - a_exp), eal=jnp.exp(a_last))


def _head_decay(pre, tril, h):
    seg = pre["a"][:, h:h + 1] - pre["a_t"][h:h + 1, :]
    return jnp.exp(jnp.where(tril, seg, -1e30))


def _ssd_fwd(proj, dtraw, cw, cb, dtb, alog, dexp, nw):
    T = proj.shape[0]
    nc = T // LS
    H = SUBLANE

    def body(z_ref, x_ref, dt_ref, cw_ref, cb_ref, dtb_ref, alog_ref, dexp_ref, nw_ref,
             yb_ref, y_ref, sp_ref, xc_ref, halo, st):
        @pl.when(pl.program_id(0) == 0)
        def _():
            halo[...] = jnp.zeros_like(halo)
            st[...] = jnp.zeros_like(st)

        xb = x_ref[...]
        taps = _rows_down(xb, halo[...], (3, 2, 1)) + [xb.astype(F32)]
        xc = cb_ref[...]
        for k in range(SK):
            xc = xc + cw_ref[k:k + 1, :] * taps[k]
        xc_ref[...] = xc.astype(BF16)
        tril, expand = _ssd_consts()
        pre = _ssd_pre(xc, dt_ref, dtb_ref, alog_ref, tril, expand)
        lane = lax.broadcasted_iota(jnp.int32, (LS, LANE), 1)
        lo = lane < SP
        zf = z_ref[...].astype(F32)
        siluz = zf * _sigmoid(zf)
        for g in range(SG):
            gs = slice(g * SGW, (g + 1) * SGW)
            bg = pre["bm"][:, g * SN:(g + 1) * SN].astype(BF16)
            cg = pre["cm"][:, g * SN:(g + 1) * SN].astype(BF16)
            gmat = _dot(cg, bg, NT)
            sg = st[g]
            sp_ref[0, g] = sg
            yoff = _dot(cg, sg.astype(BF16))
            parts = []
            for j in range(SGW // LANE):
                h0 = g * (SGW // SP) + 2 * j
                m0 = gmat * _head_decay(pre, tril, h0)
                m1 = gmat * _head_decay(pre, tril, h0 + 1)
                xp = pre["xdt"][:, g * SGW + j * LANE:g * SGW + (j + 1) * LANE]
                rhs = jnp.concatenate([jnp.where(lo, xp, 0.0), jnp.where(lo, 0.0, xp)], axis=0).astype(BF16)
                parts.append(_dot(jnp.concatenate([m0, m1], axis=1).astype(BF16), rhs))
            y = (jnp.concatenate(parts, axis=1) + pre["ea"][:, gs] * yoff + dexp_ref[:, gs] * pre["xs"][:, gs])
            st[g] = pre["eal"][:, gs] * sg + _dot(bg, (pre["w"][:, gs] * pre["xdt"][:, gs]).astype(BF16), TN)
            y_ref[:, gs] = y
            yg = y * siluz[:, gs]
            r = lax.rsqrt(jnp.mean(yg * yg, axis=-1, keepdims=True) + EPS)
            yb_ref[:, gs] = (yg * r * nw_ref[:, gs]).astype(BF16)
        halo[...] = xb[LS - 2 * H:].astype(F32)[H:]

    vec = lambda w: pl.BlockSpec((1, w), lambda c: (0, 0))
    return pl.pallas_call(
        body, grid=(nc,),
        in_specs=[pl.BlockSpec((LS, SI), lambda c: (c, 2)), pl.BlockSpec((LS, SXBC), lambda c: (c, 2)),
                  pl.BlockSpec((LS, LANE), lambda c: (c, 0)),
                  pl.BlockSpec((SK, SXBC), lambda c: (0, 0)), vec(SXBC), vec(LANE), vec(LANE), vec(SI), vec(SI)],
        out_specs=[pl.BlockSpec((LS, SI), lambda c: (c, 0)), pl.BlockSpec((LS, SI), lambda c: (c, 0)),
                   pl.BlockSpec((1, SG, SN, SGW), lambda c: (c, 0, 0, 0)), pl.BlockSpec((LS, SXBC), lambda c: (c, 0))],
        out_shape=[jax.ShapeDtypeStruct((T, SI), BF16), jax.ShapeDtypeStruct((T, SI), F32),
                   jax.ShapeDtypeStruct((nc, SG, SN, SGW), F32), jax.ShapeDtypeStruct((T, SXBC), BF16)],
        scratch_shapes=[pltpu.VMEM((H, SXBC), F32), pltpu.VMEM((SG, SN, SGW), F32)], name="ssd_fwd",
        compiler_params=_params(("arbitrary",), VMEM_CAP_MB))(proj, proj, dtraw, cw, cb, dtb, alog, dexp, nw)


def _ssd_bwd(proj, xcs, dtraw, y, sprev, dyb, dproj, cw, dtb, alog, dexp, nw):
    T = proj.shape[0]
    nc = T // LS
    H = SUBLANE
    NJ = (SI + SXBC) // D
    J0 = (2 * D + 2 * GW) // D

    def body(z_ref, x_ref, xc_ref, dt_ref, y_ref, sp_ref, dyb_ref, dproj_in,
             cw_ref, dtb_ref, alog_ref, dexp_ref, nw_ref,
             dp_ref, ddt_ref, dcw_ref, dcb_ref, ddtb_ref, da_ref, dd_ref, dnw_ref,
             ahead, ds, stage):
        del dproj_in
        i = pl.program_id(0)
        j = pl.program_id(1)

        @pl.when(jnp.logical_and(i == 0, j == 0))
        def _():
            ahead[...] = jnp.zeros_like(ahead)
            ds[...] = jnp.zeros_like(ds)
            for r in (dcw_ref, dcb_ref, ddtb_ref, da_ref, dd_ref, dnw_ref):
                r[...] = jnp.zeros_like(r)

        @pl.when(j == 0)
        def _():
            tril, expand = _ssd_consts()
            pre = _ssd_pre(xc_ref[...].astype(F32), dt_ref, dtb_ref, alog_ref, tril, expand)
            lane = lax.broadcasted_iota(jnp.int32, (LS, LANE), 1)
            sub = lax.broadcasted_iota(jnp.int32, (LANE, LS), 0)
            rowi = lax.broadcasted_iota(jnp.int32, (LS, 1), 0)
            lo = lane < SP
            xs, xdt, ea, w, eal = pre["xs"], pre["xdt"], pre["ea"], pre["w"], pre["eal"]

            zf = z_ref[...].astype(F32)
            sz = _sigmoid(zf)
            siluz = zf * sz
            yv = y_ref[...]
            yg = yv * siluz
            dout = dyb_ref[...].astype(F32)
            dyg_parts, dnw_parts = [], []
            for g in range(SG):
                gs = slice(g * SGW, (g + 1) * SGW)
                ygg = yg[:, gs]
                r = lax.rsqrt(jnp.mean(ygg * ygg, axis=-1, keepdims=True) + EPS)
                yhat = ygg * r
                dn = dout[:, gs] * nw_ref[:, gs]
                dnw_parts.append(jnp.sum(dout[:, gs] * yhat, axis=0, keepdims=True))
                dyg_parts.append(r * (dn - yhat * jnp.mean(dn * yhat, axis=-1, keepdims=True)))
            dyg = jnp.concatenate(dyg_parts, axis=1)
            dnw_ref[...] += jnp.concatenate(dnw_parts, axis=1)
            dy = dyg * siluz
            stage[:, 0:SI] = (dyg * yv * sz * (1.0 + zf * (1.0 - sz))).astype(BF16)
            dd_ref[...] += jnp.sum(dy * xs, axis=0, keepdims=True)
            tt = ea * dy

            da_rows = jnp.zeros((LS, LANE), F32)
            da_cols = jnp.zeros((LANE, LS), F32)
            dxdt_parts, db_parts, dc_parts, daexp_parts = [], [], [], []
            for g in range(SG):
                gs = slice(g * SGW, (g + 1) * SGW)
                bg = pre["bm"][:, g * SN:(g + 1) * SN].astype(BF16)
                cg = pre["cm"][:, g * SN:(g + 1) * SN].astype(BF16)
                sg = sp_ref[0, g]
                sgb = sg.astype(BF16)
                dsg = ds[g]
                dsgb = dsg.astype(BF16)
                ttg = tt[:, gs].astype(BF16)
                yoff = _dot(cg, sgb)
                dc = _dot(ttg, sgb, NT)
                gmat = _dot(cg, bg, NT)
                dgm = jnp.zeros((LS, LS), F32)
                dxdt_pairs = []
                for jj in range(SGW // LANE):
                    h0 = g * (SGW // SP) + 2 * jj
                    ps = slice(g * SGW + jj * LANE, g * SGW + (jj + 1) * LANE)
                    l0 = _head_decay(pre, tril, h0)
                    l1 = _head_decay(pre, tril, h0 + 1)
                    m0 = gmat * l0
                    m1 = gmat * l1
                    dyp = dy[:, ps]
                    dy_lo = jnp.where(lo, dyp, 0.0).astype(BF16)
                    dy_hi = jnp.where(lo, 0.0, dyp).astype(BF16)
                    xpb = xdt[:, ps].astype(BF16)
                    dm0 = _dot(dy_lo, xpb, NT)
                    dm1 = _dot(dy_hi, xpb, NT)
                    q0 = dm0 * m0
                    q1 = dm1 * m1
                    da_rows = da_rows + jnp.where(lane == h0, jnp.sum(q0, axis=1, keepdims=True), 0.0)
                    da_rows = da_rows + jnp.where(lane == h0 + 1, jnp.sum(q1, axis=1, keepdims=True), 0.0)
                    da_cols = da_cols + jnp.where(sub == h0, jnp.sum(q0, axis=0, keepdims=True), 0.0)
                    da_cols = da_cols + jnp.where(sub == h0 + 1, jnp.sum(q1, axis=0, keepdims=True), 0.0)
                    dgm = dgm + dm0 * l0 + dm1 * l1
                    mcat = jnp.concatenate([m0, m1], axis=0).astype(BF16)
                    dycat = jnp.concatenate([dy_lo, dy_hi], axis=0)
                    dxdt_pairs.append(_dot(mcat, dycat, TN))
                dgb = dgm.astype(BF16)
                dc = dc + _dot(dgb, bg)
                db = _dot(dgb, cg, TN)
                zg = _dot(bg, dsgb)
                wg, xdtg = w[:, gs], xdt[:, gs]
                dxdt_g = jnp.concatenate(dxdt_pairs, axis=1) + wg * zg
                qg = zg * xdtg * wg
                last = (jnp.sum(qg, axis=0, keepdims=True)
                        + jnp.sum(dsg * sg, axis=0, keepdims=True) * eal[:, gs])
                daexp_parts.append(dy[:, gs] * ea[:, gs] * yoff - qg + jnp.where(rowi == LS - 1, last, 0.0))
                db = db + _dot((wg * xdtg).astype(BF16), dsgb, NT)
                ds[g] = eal[:, gs] * dsg + _dot(cg, ttg, TN)
                dxdt_parts.append(dxdt_g)
                db_parts.append(db)
                dc_parts.append(dc)
            dxdt = jnp.concatenate(dxdt_parts, axis=1)
            da_exp = jnp.concatenate(daexp_parts, axis=1)
            da = _dot3(da_exp, expand, NT) + da_rows - da_cols.T
            triub = jnp.where(tril, 1.0, 0.0).astype(BF16)
            ddta = _dot3_rhs(triub, da, TN)
            ddt = ddta * pre["a_neg"] + _dot3(dxdt * xs, expand, NT)
            da_ref[...] += jnp.sum(ddta * pre["dt"], axis=0, keepdims=True)
            ddt_raw = ddt * _sigmoid(pre["dtin"])
            ddt_ref[...] = ddt_raw
            ddtb_ref[...] += jnp.sum(ddt_raw, axis=0, keepdims=True)
            dxs = dexp_ref[...] * dy + dxdt * pre["dt_exp"]
            dxbc = jnp.concatenate([dxs] + db_parts + dc_parts, axis=1)
            sx, xc = pre["sx"], pre["xc"]
            dxc = dxbc * sx * (1.0 + xc * (1.0 - sx))
            taps = _rows_up(dxc.astype(BF16), ahead[...], (3, 2, 1)) + [dxc]
            xr = x_ref[...].astype(F32)
            dcb_ref[...] += jnp.sum(dxc, axis=0, keepdims=True)
            dcw_ref[...] += jnp.concatenate([jnp.sum(t * xr, axis=0, keepdims=True) for t in taps], axis=0)
            dxr = cw_ref[0:1, :] * taps[0]
            for k in range(1, SK):
                dxr = dxr + cw_ref[k:k + 1, :] * taps[k]
            stage[:, SI:] = dxr.astype(BF16)
            ahead[...] = dxc[0:H, :]

        dp_ref[...] = stage[:, pl.ds(pl.multiple_of(j * D, D), D)]

    vec = lambda w: pl.BlockSpec((1, w), lambda i, j: (0, 0))
    rev = lambda w, cb_: pl.BlockSpec((LS, w), lambda i, j: (nc - 1 - i, cb_))
    outs = pl.pallas_call(
        body, grid=(nc, NJ),
        in_specs=[rev(SI, 2), rev(SXBC, 2), rev(SXBC, 0),
                  rev(LANE, 0), rev(SI, 0),
                  pl.BlockSpec((1, SG, SN, SGW), lambda i, j: (nc - 1 - i, 0, 0, 0)),
                  rev(SI, 0), pl.BlockSpec(memory_space=pl.ANY),
                  pl.BlockSpec((SK, SXBC), lambda i, j: (0, 0)), vec(LANE), vec(LANE), vec(SI), vec(SI)],
        out_specs=[pl.BlockSpec((LS, D), lambda i, j: (nc - 1 - i, J0 + j)), rev(LANE, 0),
                   pl.BlockSpec((SK, SXBC), lambda i, j: (0, 0)), vec(SXBC), vec(LANE), vec(LANE), vec(SI), vec(SI)],
        out_shape=[jax.ShapeDtypeStruct(dproj.shape, dproj.dtype), jax.ShapeDtypeStruct((T, LANE), F32),
                   jax.ShapeDtypeStruct((SK, SXBC), F32), jax.ShapeDtypeStruct((1, SXBC), F32),
                   jax.ShapeDtypeStruct((1, LANE), F32), jax.ShapeDtypeStruct((1, LANE), F32),
                   jax.ShapeDtypeStruct((1, SI), F32), jax.ShapeDtypeStruct((1, SI), F32)],
        scratch_shapes=[pltpu.VMEM((H, SXBC), F32),
                        pltpu.VMEM((SG, SN, SGW), F32), pltpu.VMEM((LS, SI + SXBC), BF16)],
        input_output_aliases={7: 0}, name="ssd_bwd",
        compiler_params=_params(("arbitrary", "arbitrary"), VMEM_CAP_MB))(
            proj, proj, xcs, dtraw, y, sprev, dyb, dproj, cw, dtb, alog, dexp, nw)
    return outs


def _perm_ffn_cols(a):
    lead = a.shape[:-1]
    return a.reshape(lead + (2, 2, FT)).swapaxes(-3, -2).reshape(lead + (2 * DFF,))


def _perm_ffn_rows(a):
    return a.reshape((2, 2, FT) + a.shape[1:]).swapaxes(0, 1).reshape(a.shape)


def _pad_lanes(v, n=LANE):
    return jnp.pad(v, ((0, 0), (0, n - v.shape[-1])))


LATE = ["w_proj_a", "w_proj_b", "w_out", "ffn_w_up_t", "ffn_w_down"]


class _NoExchange:
    def gather_start(self):
        return None

    def gather_pass_on(self, outs):
        return None

    def late_weights(self, w, outs):
        return w

    def reduce_late(self, grads):
        return None

    def reduce_w_in(self, grad):
        return None

    def reduced(self, late_outs, w_in_outs):
        pass


def _local_step(x, tgt, w, hooks=None):
    hooks = hooks or _NoExchange()

    def mm(*args, side=None, **kw):
        out = _matmul(*args, side=side, **kw)
        return out if side is not None else (out, [])

    win_t = w["w_in_t"]
    win_dt = jnp.pad(w["w_in_t"][PMAIN:], ((0, LANE - SH), (0, 0)))
    fcw = _perm_ffn_cols(w["ffn_conv_w"])
    fcb = _perm_ffn_cols(w["ffn_conv_b"][None, :])
    mixw = w["mix_norm_w"][None, :]
    ffnw = w["ffn_norm_w"][None, :]
    finw = w["final_norm_w"][None, :]
    bst = w["gmlp_bs"].T
    scb = w["ssm_conv_b"][None, :]
    dtb = _pad_lanes(w["ssm_dt_bias"][None, :])
    alog = _pad_lanes(w["ssm_a_log"][None, :])
    dexp = jnp.repeat(w["ssm_d"], SP)[None, :]
    snw = w["ssm_norm_w"][None, :]

    xn = _rms_fwd(x, mixw, name="mix_norm")
    proj, got = mm(xn, win_t, name="in_proj", out_dtype=BF16, tb=True, tn=1536, j_outer=True, b_rows=PMAIN,
                   side=hooks.gather_start())
    dtraw, got = mm(xn, win_dt, name="in_proj_dt", out_dtype=F32, tb=True, side=hooks.gather_pass_on(got))
    w = hooks.late_weights(w, got)
    wup = _perm_ffn_rows(w["ffn_w_up_t"])
    ya_pre = _gmlp_fwd(proj, w["gmlp_ln_w"], w["gmlp_ln_b"], w["gmlp_ws"], bst)
    yb_pre, y_ssd, sprev, ssm_xc = _ssd_fwd(proj, dtraw, w["ssm_conv_w"], scb, dtb, alog, dexp, snw)
    merged, ya, yb = _merge_fwd(ya_pre, yb_pre, proj, w["gate_bias"], w["w_proj_a"], w["w_proj_b"])
    h1 = _matmul(merged, w["w_out"], name="out_proj", out_dtype=F32, add=x)
    hn = _rms_fwd(h1, ffnw, name="ffn_norm")
    up = _matmul(hn, wup, name="ffn_up", out_dtype=BF16, tb=True, tn=FT, j_outer=True)
    act, ffn_xc = _ffn_act_fwd(up, fcw, fcb)
    h2 = _matmul(act, w["ffn_w_down"], name="ffn_down", out_dtype=F32, tk=FT, add=h1)

    loss_row, dh2, d_finw = _loss_head(h2, tgt, finw)
    dact = _matmul(dh2, w["ffn_w_down"], name="ffn_down_dx", out_dtype=BF16, tb=True, tn=FT)
    d_wdown = _matmul(act, dh2, name="ffn_down_dw", out_dtype=F32, ta=True, tm=FT)
    dup, d_fcw, d_fcb = _ffn_act_bwd(up, ffn_xc, dact, fcw)
    dhn = _matmul(dup, wup, name="ffn_up_dx", out_dtype=F32, tk=FT)
    d_wup = _matmul(dup, hn, name="ffn_up_dw", out_dtype=F32, ta=True, tm=FT)
    dh1, d_ffnw = _rms_bwd(h1, ffnw, dhn, dh2, name="ffn_norm_bwd")
    dmerged = _matmul(dh1, w["w_out"], name="out_proj_dx", out_dtype=BF16, tb=True)
    d_wout = _matmul(merged, dh1, name="out_proj_dw", out_dtype=F32, ta=True)
    dproj, dya, dyb, dya_pre, dyb_pre, d_gbias = _merge_bwd(dmerged, proj, w["gate_bias"], ya, yb,
                                                           w["w_proj_a"], w["w_proj_b"])
    d_wpa = _matmul(ya_pre, dya, name="proj_a_dw", out_dtype=F32, ta=True)
    d_wpb = _matmul(yb_pre, dyb, name="proj_b_dw", out_dtype=F32, ta=True)
    dproj, d_lnw, d_lnb, d_ws, d_bst = _gmlp_bwd(proj, dya_pre, dproj, w["gmlp_ln_w"], w["gmlp_ln_b"],
                                                 w["gmlp_ws"], bst)
    dproj, ddt, d_scw, d_scb, d_dtb, d_a, d_dch, d_snw = _ssd_bwd(
        proj, ssm_xc, dtraw, y_ssd, sprev, dyb_pre, dproj, w["ssm_conv_w"], dtb, alog, dexp, snw)
    late = {"w_proj_a": d_wpa, "w_proj_b": d_wpb, "w_out": d_wout, "ffn_w_up_t": _perm_ffn_rows(d_wup),
            "ffn_w_down": d_wdown}
    d_win_main, late_outs = mm(dproj, xn, name="in_proj_dw", out_dtype=F32, ta=True, tm=1536,
                               side=hooks.reduce_late(late))
    d_win_dt = _matmul(ddt, xn, name="in_proj_dt_dw", out_dtype=F32, ta=True)
    d_win_t = jnp.concatenate([d_win_main, d_win_dt[:SH]], axis=0)
    dxn = _matmul(ddt, win_dt, name="in_proj_dt_dx", out_dtype=F32)
    dxn, w_in_outs = mm(dproj, win_t, name="in_proj_dx", out_dtype=F32, tk=1536, add=dxn, b_rows=PMAIN,
                        side=hooks.reduce_w_in(d_win_t))
    hooks.reduced(late_outs, w_in_outs)
    grad_x, d_mixw = _rms_bwd(x, mixw, dxn, dh1, name="mix_norm_bwd")

    a_neg = -jnp.exp(w["ssm_a_log"])
    grads = {
        "mix_norm_w": d_mixw[0],
        "w_in_t": d_win_t,
        "gate_bias": d_gbias,
        "gmlp_ln_w": d_lnw, "gmlp_ln_b": d_lnb, "gmlp_ws": d_ws, "gmlp_bs": d_bst[:, :GG].T,
        "ssm_conv_w": d_scw, "ssm_conv_b": d_scb[0],
        "ssm_dt_bias": d_dtb[0, :SH], "ssm_a_log": d_a[0, :SH] * a_neg,
        "ssm_d": d_dch.reshape(SH, SP).sum(axis=-1), "ssm_norm_w": d_snw[0],
        **late,
        "ffn_norm_w": d_ffnw[0],
        "ffn_conv_w": _perm_ffn_cols(d_fcw), "ffn_conv_b": _perm_ffn_cols(d_fcb)[0],
        "ffn_w_down": d_wdown, "final_norm_w": d_finw[0],
    }
    return loss_row[0, 0], grad_x, grads


MESH = pl.DeviceIdType.MESH
HBM_SPEC = pl.BlockSpec(memory_space=pltpu.HBM)


def _axes():
    return lax.axis_index("x"), lax.axis_index("y"), lax.axis_index("c")


def _all_gather(shards, *, name):
    na = len(shards)

    def body(*refs):
        x_refs, out_refs = refs[:na], refs[na:2 * na]
        send_sems, recv_sems, local_sems = refs[2 * na:]
        x, y, c = _axes()
        me, sibling = (x, y, c), (x, y, 1 - c)
        chips = [(1 - x, y), (x, 1 - y), (1 - x, 1 - y)]

        def slot(a, px, py, pc):
            return out_refs[a].at[4 * px + 2 * py + pc]

        def copy(a, k, block, to, src=None):
            return pltpu.make_async_remote_copy(
                src_ref=slot(a, *block) if src is None else src, dst_ref=slot(a, *block),
                send_sem=send_sems.at[7 * a + k], recv_sem=recv_sems.at[7 * a + k], device_id=to, device_id_type=MESH)

        mine = [pltpu.make_async_copy(x_refs[a], slot(a, *me), local_sems.at[a]) for a in range(na)]
        for cp in mine:
            cp.start()
        first = []
        for a in range(na):
            first.append(copy(a, 0, me, sibling, src=x_refs[a]))
            first += [copy(a, 1 + j, me, (*chip, c), src=x_refs[a]) for j, chip in enumerate(chips)]
        for cp in first:
            cp.start()
        passed = []
        for j, chip in enumerate(chips):
            for a in range(na):
                copy(a, 1 + j, (*chip, c), me).wait_recv()
                cp = copy(a, 4 + j, (*chip, c), sibling)
                cp.start()
                passed.append(cp)
        for a in range(na):
            copy(a, 0, sibling, me).wait_recv()
        for j, chip in enumerate(chips):
            for a in range(na):
                copy(a, 4 + j, (*chip, 1 - c), me).wait_recv()
        for cp in first + passed:
            cp.wait_send()
        for cp in mine:
            cp.wait()

    return pl.pallas_call(
        body, out_shape=[jax.ShapeDtypeStruct((NDEV,) + s.shape, s.dtype) for s in shards],
        in_specs=[HBM_SPEC] * na, out_specs=[HBM_SPEC] * na,
        scratch_shapes=[pltpu.SemaphoreType.DMA((7 * na,)), pltpu.SemaphoreType.DMA((7 * na,)),
                        pltpu.SemaphoreType.DMA((na,))],
        name=name)(*shards)


def _exchange(srcs, plan, *, name):
    na = len(srcs)
    n = len(plan(0, 0, 0))

    def body(*refs):
        src_refs, out_refs = refs[:na], refs[na:2 * na]
        send_sems, recv_sems = refs[2 * na:]
        x, y, c = _axes()
        copies = []
        for k, (slab, peer) in enumerate(plan(x, y, c)):
            for a in range(na):
                cp = pltpu.make_async_remote_copy(
                    src_ref=src_refs[a].at[slab], dst_ref=out_refs[a].at[k], send_sem=send_sems.at[n * a + k],
                    recv_sem=recv_sems.at[n * a + k], device_id=peer, device_id_type=MESH)
                cp.start()
                copies.append(cp)
        for cp in copies:
            cp.wait()

    return pl.pallas_call(
        body, out_shape=[jax.ShapeDtypeStruct((n,) + s.shape[1:], s.dtype) for s in srcs],
        in_specs=[HBM_SPEC] * na, out_specs=[HBM_SPEC] * na,
        scratch_shapes=[pltpu.SemaphoreType.DMA((n * na,)), pltpu.SemaphoreType.DMA((n * na,))], name=name)(*srcs)


def _to_sibling_plan(x, y, c):
    return [(2 * q + (1 - c), (x, y, 1 - c)) for q in range(4)]


def _to_chips_plan(x, y, c):
    q = 2 * x + y
    return [(q ^ 2, (1 - x, y, c)), (q ^ 1, (x, 1 - y, c)), (q ^ 3, (1 - x, 1 - y, c))]


def _row_tile(rows, row_bytes, budget=2 * 2 ** 20, align=2 * SUBLANE):
    if rows * row_bytes <= 2 * budget:
        return rows
    best = None
    for d in range(align, rows + 1, align):
        if rows % d == 0 and d * row_bytes <= budget:
            best = d
    return best or rows


def _pair_add(g, ra, c_idx, *, name):
    _, _, R, C = g.shape
    tr = _row_tile(R, C * 4, budget=3 * 2 ** 20)

    def body(c_ref, g_ref, ra_ref, o_ref):
        del c_ref
        o_ref[...] = (g_ref[0].astype(F32) + ra_ref[...].astype(F32)).astype(o_ref.dtype)

    return pl.pallas_call(
        body,
        grid_spec=pltpu.PrefetchScalarGridSpec(
            num_scalar_prefetch=1, grid=(4, R // tr),
            in_specs=[pl.BlockSpec((1, 1, tr, C), lambda q, r, cr: (q, cr[0], r, 0)),
                      pl.BlockSpec((1, tr, C), lambda q, r, cr: (q, r, 0))],
            out_specs=pl.BlockSpec((1, tr, C), lambda q, r, cr: (q, r, 0))),
        out_shape=jax.ShapeDtypeStruct((4, R, C), g.dtype), name=name,
        compiler_params=_params(("arbitrary", "arbitrary"), 24))(c_idx, g, ra)


def _grad_sum(p, rb, q_idx, *, name):
    _, R, C = p.shape
    tr = _row_tile(R, C * 4, budget=3 * 2 ** 20)

    def body(q_ref, p_ref, rb_ref, o_ref):
        del q_ref
        g = p_ref[0].astype(F32)
        for k in range(3):
            g = g + rb_ref[k].astype(F32)
        o_ref[...] = g

    return pl.pallas_call(
        body,
        grid_spec=pltpu.PrefetchScalarGridSpec(
            num_scalar_prefetch=1, grid=(R // tr,),
            in_specs=[pl.BlockSpec((1, tr, C), lambda r, qr: (qr[0], r, 0)),
                      pl.BlockSpec((3, tr, C), lambda r, qr: (0, r, 0))],
            out_specs=pl.BlockSpec((tr, C), lambda r, qr: (r, 0))),
        out_shape=jax.ShapeDtypeStruct((R, C), F32), name=name,
        compiler_params=_params(("arbitrary",), 40))(q_idx, p, rb)


def _adamw(g, w, m, v):
    m = ADAM_B1 * m + (1.0 - ADAM_B1) * g
    v = ADAM_B2 * v + (1.0 - ADAM_B2) * (g * g)
    m_hat = m / (1.0 - ADAM_B1 ** ADAM_STEP)
    v_hat = v / (1.0 - ADAM_B2 ** ADAM_STEP)
    delta = -ADAM_LR * (m_hat / (jnp.sqrt(v_hat) + ADAM_EPS) + ADAM_WD * w)
    return delta, m, v


def _adam(g, w, m, v, *, name):
    R, C = w.shape
    tr = _row_tile(R, C * 4, budget=2 ** 20, align=SUBLANE)

    def body(g_ref, w_ref, m_ref, v_ref, d_out, m_out, v_out):
        delta, mn, vn = _adamw(g_ref[...], w_ref[...], m_ref[...], v_ref[...])
        d_out[...] = delta
        m_out[...] = mn
        v_out[...] = vn

    row = pl.BlockSpec((tr, C), lambda r: (r, 0))
    o = jax.ShapeDtypeStruct((R, C), F32)
    return pl.pallas_call(
        body, grid=(R // tr,), in_specs=[row, row, row, row], out_specs=[row, row, row], out_shape=[o, o, o],
        name=name, compiler_params=_params(("arbitrary",), 32))(g, w, m, v)


WEIGHTS = ["mix_norm_w", "w_in", "gate_bias", "gmlp_ln_w", "gmlp_ln_b", "gmlp_ws", "gmlp_bs", "ssm_conv_w",
           "ssm_conv_b", "ssm_dt_bias", "ssm_a_log", "ssm_d", "ssm_norm_w", "w_proj_a", "w_proj_b", "w_out",
           "ffn_norm_w", "ffn_w_up", "ffn_conv_w", "ffn_conv_b", "ffn_w_down", "final_norm_w"]
SHARDED = {"w_in": ((D, IN_COLS), 1), "gate_bias": ((2, D), 1), "ssm_conv_w": ((SK, SXBC), 1),
           "w_proj_a": ((GW, D), 0), "w_proj_b": ((SI, D), 0), "w_out": ((D, D), 0),
           "ffn_w_up": ((D, 2 * DFF), 1), "ffn_conv_w": ((FK, 2 * DFF), 1), "ffn_w_down": ((DFF, D), 0)}
REPLICATED = {"mix_norm_w": (D,), "gmlp_ln_w": (GG, GD), "gmlp_ln_b": (GG, GD), "gmlp_ws": (GG, GB, GB),
              "gmlp_bs": (GG, GB), "ssm_conv_b": (SXBC,), "ssm_dt_bias": (SH,), "ssm_a_log": (SH,), "ssm_d": (SH,),
              "ssm_norm_w": (SI,), "ffn_norm_w": (D,), "ffn_conv_b": (2 * DFF,), "final_norm_w": (D,)}
REPL_ORDER = [n for n in WEIGHTS if n in REPLICATED]
BTILE = 2 * SUBLANE
WIN_R = IN_COLS // NDEV
WIN_P = WIN_R + BTILE - WIN_R % BTILE
WIN_A = [WIN_R * d // BTILE * BTILE for d in range(NDEV)]
assert all(WIN_A[d] + WIN_P >= WIN_R * (d + 1) for d in range(NDEV)) and WIN_A[-1] + WIN_P == IN_COLS
BIG = [("w_proj_a", GW // NDEV, False), ("w_proj_b", SI // NDEV, False), ("w_out", D // NDEV, False),
       ("ffn_w_up", 2 * DFF // NDEV, True), ("ffn_w_down", DFF // NDEV, False), ("w_in", WIN_P, True)]
VECTORS = ["gate_bias", "ssm_conv_w", "ffn_conv_w"]


def _size(shape):
    n = 1
    for s in shape:
        n *= s
    return n


def _round_up(n, k):
    return (n + k - 1) // k * k


BIG_OFF = {}
_off = 0
for _n, _r, _t in BIG:
    BIG_OFF[_n] = _off
    _off += _r
BIG_USED = _off
BIG_ROWS = _round_up(BIG_USED, 2 * SUBLANE)
assert all(BIG_OFF[n] % (2 * SUBLANE) == 0 for n, _, _ in BIG)
VEC_SHAPE = {n: (SHARDED[n][0][0], SHARDED[n][0][1] // NDEV) for n in VECTORS}
VEC_ELEMS = sum(_size(VEC_SHAPE[n]) for n in VECTORS)
VEC_ROWS = _round_up(VEC_ELEMS, SUBLANE * LANE) // LANE
REPL_ELEMS = sum(_size(REPLICATED[n]) for n in REPL_ORDER)
REPL_ROWS = _round_up(REPL_ELEMS, NDEV * SUBLANE * LANE) // (NDEV * LANE)
SMALL_ROWS = VEC_ROWS + REPL_ROWS


def _win_offset(dev):
    return WIN_R * dev - WIN_R * dev // BTILE * BTILE


def _pack_big(arrs, dtype, dev):
    parts = []
    for n, r, t in BIG:
        a = (arrs[n].T if t else arrs[n]).astype(dtype)
        if n == "w_in":
            a = lax.dynamic_update_slice(jnp.zeros((WIN_P, D), dtype), a, (_win_offset(dev), 0))
        parts.append(a)
    parts.append(jnp.zeros((BIG_ROWS - BIG_USED, D), dtype))
    return jnp.concatenate(parts, axis=0)


def _join_windows(win):
    parts = []
    for d in range(NDEV):
        lo = BTILE if WIN_A[d] % WIN_R else 0
        if lo:
            parts.append(win[d - 1, WIN_P - BTILE:] + win[d, :BTILE])
        hi = WIN_P - BTILE if d + 1 < NDEV and WIN_A[d + 1] < WIN_A[d] + WIN_P else WIN_P
        parts.append(win[d, lo:hi])
    return jnp.concatenate(parts, axis=0)


def _split_windows(g):
    return jnp.stack([g[a:a + WIN_P] for a in WIN_A])


def _pack_vectors(arrs):
    flat = jnp.concatenate([arrs[n].reshape(-1) for n in VECTORS])
    return jnp.pad(flat, (0, VEC_ROWS * LANE - VEC_ELEMS)).reshape(VEC_ROWS, LANE)


def _unpack_vectors(flat):
    out, off = {}, 0
    v = flat.reshape(-1)
    for n in VECTORS:
        k = _size(VEC_SHAPE[n])
        out[n] = v[off:off + k].reshape((1,) + VEC_SHAPE[n])
        off += k
    return out


def _join_vectors(allv):
    out, off = {}, 0
    v = allv.reshape(NDEV, VEC_ROWS * LANE)
    for n in VECTORS:
        r, c = VEC_SHAPE[n]
        out[n] = v[:, off:off + r * c].reshape(NDEV, r, c).transpose(1, 0, 2).reshape(r, c * NDEV)
        off += r * c
    return out


def _split_vectors(grads):
    parts = []
    for n in VECTORS:
        r, c = VEC_SHAPE[n]
        parts.append(grads[n].reshape(r, NDEV, c).transpose(1, 0, 2).reshape(NDEV, r * c))
    v = jnp.concatenate(parts, axis=1)
    return jnp.pad(v, ((0, 0), (0, VEC_ROWS * LANE - VEC_ELEMS))).reshape(NDEV, VEC_ROWS, LANE)


LATE_ROWS = BIG_OFF["w_in"]
assert LATE_ROWS + WIN_P == BIG_ROWS and BIG[-1][0] == "w_in"


def _remote(src, dst, send_sems, recv_sems, k, to):
    return pltpu.make_async_remote_copy(src_ref=src, dst_ref=dst, send_sem=send_sems.at[k], recv_sem=recv_sems.at[k],
                                        device_id=to, device_id_type=MESH)


class _Exchange:
    def __init__(self, late_shard, c_idx):
        self.late_shard, self.c_idx = late_shard, c_idx

    def gather_start(self):
        shard = self.late_shard

        def make(ins, outs, send_sems, recv_sems):
            (x_ref,), (out,) = ins, outs
            x, y, c = _axes()
            mine = out.at[4 * x + 2 * y + c]
            peers = [(x, y, 1 - c), (1 - x, y, c), (x, 1 - y, c), (1 - x, 1 - y, c)]
            copies = [_remote(x_ref, mine, send_sems, recv_sems, k, p) for k, p in enumerate(peers)]
            return copies + [pltpu.make_async_copy(x_ref, mine, send_sems.at[len(peers)])]

        return _Side([shard], [jax.ShapeDtypeStruct((NDEV,) + shard.shape, shard.dtype)], 5, make)

    def gather_pass_on(self, outs):
        (buf,) = outs

        def make(ins, outs, send_sems, recv_sems):
            (src,), (dst,) = ins, outs
            x, y, c = _axes()
            slots = [4 * px + 2 * py + c for px, py in [(1 - x, y), (x, 1 - y), (1 - x, 1 - y)]]
            return [_remote(src.at[s], dst.at[s], send_sems, recv_sems, k, (x, y, 1 - c)) for k, s in enumerate(slots)]

        return _Side([buf], [jax.ShapeDtypeStruct(buf.shape, buf.dtype)], 3, make, aliases=[(0, 0)])

    def late_weights(self, w, outs):
        (buf,) = outs
        w = dict(w)
        for n, r, t in BIG[:-1]:
            w[n + "_t" if t else n] = buf[:, BIG_OFF[n]:BIG_OFF[n] + r].reshape(NDEV * r, D)
        return w

    def _reduce(self, send, tag):
        (sib,) = _exchange([send], _to_sibling_plan, name=tag + "_grads_to_sibling")
        sums = _pair_add(send.reshape((4, 2) + send.shape[1:]), sib, self.c_idx, name=tag + "_grad_pair_add")

        def make(ins, outs, send_sems, recv_sems):
            (src,), (dst,) = ins, outs
            return [_remote(src.at[slab], dst.at[k], send_sems, recv_sems, k, peer)
                    for k, (slab, peer) in enumerate(_to_chips_plan(*_axes()))]

        return sums, _Side([sums], [jax.ShapeDtypeStruct((3,) + sums.shape[1:], sums.dtype)], 3, make)

    def reduce_late(self, grads):
        send = jnp.concatenate([grads[n + "_t" if t else n].reshape(NDEV, r, D) for n, r, t in BIG[:-1]], axis=1)
        self.late_sum, side = self._reduce(send.astype(BF16), "late")
        return side

    def reduce_w_in(self, grad):
        self.w_in_sum, side = self._reduce(_split_windows(grad).astype(BF16), "w_in")
        return side

    def reduced(self, late_outs, w_in_outs):
        (self.late_from_chips,), (self.w_in_from_chips,) = late_outs, w_in_outs


def _pack_repl(arrs):
    flat = jnp.concatenate([arrs[n].reshape(-1) for n in REPL_ORDER])
    return jnp.pad(flat, (0, NDEV * REPL_ROWS * LANE - REPL_ELEMS)).reshape(NDEV * REPL_ROWS, LANE)


def _unpack_repl(flat, shapes):
    out, off = {}, 0
    v = flat.reshape(-1)
    for n in REPL_ORDER:
        k = _size(REPLICATED[n])
        out[n] = v[off:off + k].reshape(shapes[n])
        off += k
    return out


def kernel(x, mix_norm_w, w_in, gate_bias, gmlp_ln_w, gmlp_ln_b, gmlp_ws, gmlp_bs, ssm_conv_w, ssm_conv_b, ssm_dt_bias, ssm_a_log, ssm_d, ssm_norm_w, w_proj_a, w_proj_b, w_out, ffn_norm_w, ffn_w_up, ffn_conv_w, ffn_conv_b, ffn_w_down, final_norm_w, loss_target, m_mix_norm_w, m_w_in, m_gate_bias, m_gmlp_ln_w, m_gmlp_ln_b, m_gmlp_ws, m_gmlp_bs, m_ssm_conv_w, m_ssm_conv_b, m_ssm_dt_bias, m_ssm_a_log, m_ssm_d, m_ssm_norm_w, m_w_proj_a, m_w_proj_b, m_w_out, m_ffn_norm_w, m_ffn_w_up, m_ffn_conv_w, m_ffn_conv_b, m_ffn_w_down, m_final_norm_w, v_mix_norm_w, v_w_in, v_gate_bias, v_gmlp_ln_w, v_gmlp_ln_b, v_gmlp_ws, v_gmlp_bs, v_ssm_conv_w, v_ssm_conv_b, v_ssm_dt_bias, v_ssm_a_log, v_ssm_d, v_ssm_norm_w, v_w_proj_a, v_w_proj_b, v_w_out, v_ffn_norm_w, v_ffn_w_up, v_ffn_conv_w, v_ffn_conv_b, v_ffn_w_down, v_final_norm_w):
    given = dict(locals())
    wts = {n: given[n] for n in WEIGHTS}
    mom = {n: given["m_" + n] for n in WEIGHTS}
    var = {n: given["v_" + n] for n in WEIGHTS}
    xi, yi, ci = _axes()
    c_idx = jnp.reshape(ci, (1,)).astype(jnp.int32)
    q_idx = jnp.reshape(2 * xi + yi, (1,)).astype(jnp.int32)
    big_names = [n for n, _, _ in BIG]
    drop = lambda d, names: {n: d[n][0] for n in names}

    dev = 4 * xi + 2 * yi + ci
    packed = _pack_big(drop(wts, big_names), BF16, dev)
    all_win, all_vec = _all_gather([packed[LATE_ROWS:], _pack_vectors(wts)], name="w_in_all_gather")
    full = {"w_in_t": _join_windows(all_win)}
    full.update(_join_vectors(all_vec))
    for n in REPL_ORDER:
        full[n] = wts[n].reshape(REPLICATED[n])

    hooks = _Exchange(packed[:LATE_ROWS], c_idx)
    loss_local, grad_x, grads = _local_step(x[0], loss_target[0], full, hooks)
    g_late = _grad_sum(hooks.late_sum, hooks.late_from_chips, q_idx, name="late_grad_sum")
    g_win = _grad_sum(hooks.w_in_sum, hooks.w_in_from_chips, q_idx, name="w_in_grad_sum")

    send_small = jnp.concatenate([_split_vectors(grads), _pack_repl(grads).reshape(NDEV, REPL_ROWS, LANE)], axis=1)
    (sib_small,) = _exchange([send_small], _to_sibling_plan, name="small_grads_to_sibling")
    sum_small = _pair_add(send_small.reshape(4, 2, SMALL_ROWS, LANE), sib_small, c_idx, name="small_grad_pair_add")
    (chips_small,) = _exchange([sum_small], _to_chips_plan, name="small_grads_to_chips")
    g_small = _grad_sum(sum_small, chips_small, q_idx, name="small_grad_sum")

    outs = {}
    for n, r, t in BIG:
        if n == "w_in":
            g = lax.dynamic_slice(g_win, (_win_offset(dev), 0), (WIN_R, D))
        else:
            g = g_late[BIG_OFF[n]:BIG_OFF[n] + r]
        g = g.T if t else g
        outs[n] = tuple(a[None] for a in (g,) + tuple(_adam(g, wts[n][0], mom[n][0], var[n][0], name="adam_" + n)))
    g_vec = g_small[:VEC_ROWS]
    vec_out = (g_vec,) + tuple(_adam(g_vec, _pack_vectors(wts), _pack_vectors(mom), _pack_vectors(var),
                                     name="adam_vectors"))
    vec_out = [_unpack_vectors(a) for a in vec_out]
    for n in VECTORS:
        outs[n] = tuple(a[n] for a in vec_out)

    g_repl = _all_gather([g_small[VEC_ROWS:]], name="replicated_grads_all_gather")[0].reshape(NDEV * REPL_ROWS, LANE)
    repl_out = (g_repl,) + tuple(_adam(g_repl, _pack_repl(wts), _pack_repl(mom), _pack_repl(var),
                                       name="adam_replicated"))
    shapes = {n: wts[n].shape for n in WEIGHTS}
    repl_out = [_unpack_repl(a, shapes) for a in repl_out]
    for n in REPL_ORDER:
        outs[n] = tuple(a[n] for a in repl_out)

    loss = lax.psum(loss_local, ("x", "y", "c"))
    return (loss, grad_x[None]) + tuple(outs[n][k] for k in range(4) for n in WEIGHTS)
```

```python
import functools

import jax
import jax.numpy as jnp
from jax import lax
from jax.experimental import pallas as pl
from jax.experimental.pallas import tpu as pltpu

F32 = jnp.float32
BF16 = jnp.bfloat16

D = 1024
EPS = 1e-5
GW = 1024
GB = 128
GG = 8
GD = 128
GCH = 64
SI = 2048
SH = 32
SP = 64
SG = 4
SN = 128
SGW = SI // SG
SK = 4
SXBC = SI + 2 * SG * SN
DFF = 2816
FK = 3
PMAIN = 2 * D + 2 * GW + SI + SXBC
IN_COLS = PMAIN + SH
NDEV = 8
ADAM_LR, ADAM_B1, ADAM_B2, ADAM_EPS, ADAM_WD, ADAM_STEP = 0.001, 0.9, 0.999, 1e-08, 0.01, 10

LANE = 128
SUBLANE = 8
VMEM_MB_V7X = 64
VMEM_CAP_MB = VMEM_MB_V7X - 8

LS = 128
FT = DFF // 2

NN = (((1,), (0,)), ((), ()))
NT = (((1,), (1,)), ((), ()))
TN = (((0,), (0,)), ((), ()))


def _params(sem, vmem_mb):
    return pltpu.CompilerParams(dimension_semantics=sem,
                                vmem_limit_bytes=min(int(vmem_mb), VMEM_CAP_MB) * 1024 * 1024)


def _dot(a, b, dims=NN):
    return lax.dot_general(a, b, dims, preferred_element_type=F32)


def _sigmoid(x):
    return 1.0 / (1.0 + jnp.exp(-x))


def _split3(v):
    hi = v.astype(BF16)
    r = v - hi.astype(F32)
    mid = r.astype(BF16)
    lo = (r - mid.astype(F32)).astype(BF16)
    return hi, mid, lo


def _dot3(a_f32, b_bf16, dims):
    hi, mid, lo = _split3(a_f32)
    return _dot(hi, b_bf16, dims) + _dot(mid, b_bf16, dims) + _dot(lo, b_bf16, dims)


def _dot3_rhs(a_bf16, b_f32, dims):
    hi, mid, lo = _split3(b_f32)
    return _dot(a_bf16, hi, dims) + _dot(a_bf16, mid, dims) + _dot(a_bf16, lo, dims)


def _matmul(a, b, *, name, out_dtype, ta=False, tb=False, tm=1024, tn=1024, tk=1024, add=None,
            j_outer=False, b_rows=None, side=None):
    if ta:
        K, M = a.shape
    else:
        M, K = a.shape
    if tb:
        N, K2 = b.shape
        N = b_rows or N
    else:
        K2, N = b.shape
        K2 = b_rows or K2
    assert K == K2, (a.shape, b.shape, ta, tb)
    tm, tn, tk = min(tm, M), min(tn, N), min(tk, K)
    assert M % tm == 0 and N % tn == 0 and K % tk == 0, (M, N, K, tm, tn, tk)
    nk = K // tk
    dims = (((0 if ta else 1,), (1 if tb else 0,)), ((), ()))
    has_add = add is not None
    n_in = 3 if has_add else 2
    s_in = len(side.inputs) if side else 0
    s_out = len(side.out_shapes) if side else 0
    grid = (N // tn, M // tm, nk) if j_outer else (M // tm, N // tn, nk)

    def body(*refs):
        a_ref, b_ref = refs[:2]
        add_ref = refs[2] if has_add else None
        o_ref = refs[n_in + s_in]
        if side:
            side_refs = (refs[n_in:n_in + s_in], refs[n_in + s_in + 1:n_in + s_in + 1 + s_out]) + tuple(refs[-2:])
            ids = [pl.program_id(d) for d in range(3)]
            first = functools.reduce(jnp.logical_and, [i == 0 for i in ids])
            last = functools.reduce(jnp.logical_and, [i == g - 1 for i, g in zip(ids, grid)])

            @pl.when(first)
            def _():
                for cp in side.make(*side_refs):
                    cp.start()

            @pl.when(last)
            def _():
                for cp in side.make(*side_refs):
                    cp.wait()

        p = lax.dot_general(a_ref[...].astype(BF16), b_ref[...].astype(BF16), dims,
                            preferred_element_type=F32)

        def finish(acc):
            if has_add:
                acc = acc + add_ref[...].astype(F32)
            o_ref[...] = acc.astype(o_ref.dtype)

        if nk == 1:
            finish(p)
        else:
            acc_ref = refs[n_in + s_in + 1 + s_out]
            k = pl.program_id(2)

            @pl.when(k == 0)
            def _():
                acc_ref[...] = p

            @pl.when(jnp.logical_and(k > 0, k < nk - 1))
            def _():
                acc_ref[...] += p

            @pl.when(k == nk - 1)
            def _():
                finish(acc_ref[...] + p)

    if j_outer:
        ij = lambda g0, g1: (g1, g0)
    else:
        ij = lambda g0, g1: (g0, g1)

    def a_map(g0, g1, k):
        i, _ = ij(g0, g1)
        return (k, i) if ta else (i, k)

    def b_map(g0, g1, k):
        _, j = ij(g0, g1)
        return (j, k) if tb else (k, j)

    def o_map(g0, g1, k):
        return ij(g0, g1)

    in_specs = [pl.BlockSpec((tk, tm) if ta else (tm, tk), a_map),
                pl.BlockSpec((tn, tk) if tb else (tk, tn), b_map)]
    args = [a, b]
    if has_add:
        in_specs.append(pl.BlockSpec((tm, tn), o_map))
        args.append(add)
    scratch = [pltpu.VMEM((tm, tn), F32)] if nk > 1 else []
    osz = jnp.dtype(out_dtype).itemsize
    est = (2 * (tm * tk * a.dtype.itemsize + tk * tn * b.dtype.itemsize) + 2 * tm * tn * osz
           + (2 * tm * tn * add.dtype.itemsize if has_add else 0)
           + 3 * tm * tn * 4 + 2 * (tm * tk + tk * tn)) / 2 ** 20 + 4
    out_specs = [pl.BlockSpec((tm, tn), o_map)]
    out_shape = [jax.ShapeDtypeStruct((M, N), out_dtype)]
    aliases = {}
    if side:
        hbm = pl.BlockSpec(memory_space=pltpu.HBM)
        in_specs += [hbm] * s_in
        args += list(side.inputs)
        out_specs += [hbm] * s_out
        out_shape += list(side.out_shapes)
        scratch += [pltpu.SemaphoreType.DMA((side.nsem,)), pltpu.SemaphoreType.DMA((side.nsem,))]
        aliases = {n_in + i: 1 + j for i, j in side.aliases}
    outs = pl.pallas_call(
        body, grid=grid, in_specs=in_specs, out_specs=out_specs, out_shape=out_shape, scratch_shapes=scratch,
        input_output_aliases=aliases, name=name,
        compiler_params=_params(("arbitrary", "arbitrary", "arbitrary"), est))(*args)
    return (outs[0], list(outs[1:])) if side else outs[0]


class _Side:
    def __init__(self, inputs, out_shapes, nsem, make, aliases=()):
        self.inputs, self.out_shapes, self.nsem, self.make, self.aliases = inputs, out_shapes, nsem, make, aliases


def _rms_fwd(x, w, *, name):
    T = x.shape[0]
    tm = min(512, T)

    def body(x_ref, w_ref, o_ref):
        xv = x_ref[...]
        r = lax.rsqrt(jnp.mean(xv * xv, axis=-1, keepdims=True) + EPS)
        o_ref[...] = (xv * r * w_ref[...]).astype(BF16)

    return pl.pallas_call(
        body, grid=(T // tm,),
        in_specs=[pl.BlockSpec((tm, D), lambda i: (i, 0)), pl.BlockSpec((1, D), lambda i: (0, 0))],
        out_specs=pl.BlockSpec((tm, D), lambda i: (i, 0)),
        out_shape=jax.ShapeDtypeStruct((T, D), BF16), name=name,
        compiler_params=_params(("arbitrary",), 24))(x, w)


def _rms_bwd(x, w, dy, dres, *, name):
    T = x.shape[0]
    tm = min(512, T)

    def body(x_ref, w_ref, dy_ref, dres_ref, dx_ref, dw_ref):
        xv = x_ref[...]
        r = lax.rsqrt(jnp.mean(xv * xv, axis=-1, keepdims=True) + EPS)
        xhat = xv * r
        dyv = dy_ref[...].astype(F32)
        g = dyv * w_ref[...]
        dx_ref[...] = dres_ref[...] + r * (g - xhat * jnp.mean(g * xhat, axis=-1, keepdims=True))
        part = jnp.sum(dyv * xhat, axis=0, keepdims=True)

        @pl.when(pl.program_id(0) == 0)
        def _():
            dw_ref[...] = part

        @pl.when(pl.program_id(0) > 0)
        def _():
            dw_ref[...] += part

    row = pl.BlockSpec((tm, D), lambda i: (i, 0))
    vec = pl.BlockSpec((1, D), lambda i: (0, 0))
    return pl.pallas_call(
        body, grid=(T // tm,), in_specs=[row, vec, row, row], out_specs=[row, vec],
        out_shape=[jax.ShapeDtypeStruct((T, D), F32), jax.ShapeDtypeStruct((1, D), F32)], name=name,
        compiler_params=_params(("arbitrary",), 32))(x, w, dy, dres)


def _loss_head(h, tgt, w):
    T = h.shape[0]
    tm = min(512, T)

    def body(h_ref, t_ref, w_ref, loss_ref, dh_ref, dw_ref):
        hv = h_ref[...]
        r = lax.rsqrt(jnp.mean(hv * hv, axis=-1, keepdims=True) + EPS)
        xhat = hv * r
        wv = w_ref[...]
        err = xhat * wv - t_ref[...]
        lpart = 0.5 * jnp.sum(jnp.mean(err * err, axis=-1, keepdims=True), axis=0, keepdims=True)
        dy = err * (1.0 / D)
        g = dy * wv
        dh_ref[...] = r * (g - xhat * jnp.mean(g * xhat, axis=-1, keepdims=True))
        wpart = jnp.sum(dy * xhat, axis=0, keepdims=True)
        lrow = jnp.broadcast_to(lpart, (1, LANE))

        @pl.when(pl.program_id(0) == 0)
        def _():
            dw_ref[...] = wpart
            loss_ref[...] = lrow

        @pl.when(pl.program_id(0) > 0)
        def _():
            dw_ref[...] += wpart
            loss_ref[...] += lrow

    row = pl.BlockSpec((tm, D), lambda i: (i, 0))
    vec = pl.BlockSpec((1, D), lambda i: (0, 0))
    return pl.pallas_call(
        body, grid=(T // tm,), in_specs=[row, row, vec],
        out_specs=[pl.BlockSpec((1, LANE), lambda i: (0, 0)), row, vec],
        out_shape=[jax.ShapeDtypeStruct((1, LANE), F32), jax.ShapeDtypeStruct((T, D), F32),
                   jax.ShapeDtypeStruct((1, D), F32)], name="loss_head",
        compiler_params=_params(("arbitrary",), 32))(h, tgt, w)


_GELU_C = 0.7978845608028654
_GELU_A = 0.044715


def _gelu(x):
    t = jnp.tanh(_GELU_C * (x + _GELU_A * x * x * x))
    return 0.5 * x * (1.0 + t), t


def _gelu_grad(x, t):
    return 0.5 * (1.0 + t) + 0.5 * x * (1.0 - t * t) * _GELU_C * (1.0 + 3.0 * _GELU_A * x * x)


def _gmlp_mask():
    r = lax.broadcasted_iota(jnp.int32, (GB, GB), 0) // GCH
    c = lax.broadcasted_iota(jnp.int32, (GB, GB), 1) // GCH
    return c <= r


def _gmlp_fwd(proj, lnw, lnb, ws, bst):
    T = proj.shape[0]
    tm = min(512, T)
    nblk = tm // GB

    def body(u_ref, v_ref, lnw_ref, lnb_ref, ws_ref, bst_ref, o_ref):
        mask = _gmlp_mask()
        u, _ = _gelu(u_ref[...].astype(F32))
        v, _ = _gelu(v_ref[...].astype(F32))
        for g in range(GG):
            cs = slice(g * GD, (g + 1) * GD)
            vg = v[:, cs]
            mu = jnp.mean(vg, axis=-1, keepdims=True)
            vc = vg - mu
            var = jnp.mean(vc * vc, axis=-1, keepdims=True)
            vn = (vc * lax.rsqrt(var + EPS) * lnw_ref[g:g + 1, :] + lnb_ref[g:g + 1, :]).astype(BF16)
            wsg = jnp.where(mask, ws_ref[g], 0.0).astype(BF16)
            bcol = bst_ref[:, g:g + 1]
            for blk in range(nblk):
                rs = slice(blk * GB, (blk + 1) * GB)
                sv = _dot(wsg, vn[rs, :]) + bcol
                o_ref[rs, cs] = (u[rs, cs] * sv).astype(BF16)

    full = lambda shape: pl.BlockSpec(shape, lambda i: tuple(0 for _ in shape))
    return pl.pallas_call(
        body, grid=(T // tm,),
        in_specs=[pl.BlockSpec((tm, GW), lambda i: (i, 2)), pl.BlockSpec((tm, GW), lambda i: (i, 3)),
                  full((GG, GD)), full((GG, GD)), full((GG, GB, GB)), full((GB, GG))],
        out_specs=pl.BlockSpec((tm, GW), lambda i: (i, 0)),
        out_shape=jax.ShapeDtypeStruct((T, GW), BF16), name="gmlp_fwd",
        compiler_params=_params(("arbitrary",), 40))(proj, proj, lnw, lnb, ws, bst)


def _gmlp_bwd(proj, dya, dproj, lnw, lnb, ws, bst):
    T = proj.shape[0]
    tm = min(512, T)
    nblk = tm // GB

    def body(u_ref, v_ref, dya_ref, dproj_in, lnw_ref, lnb_ref, ws_ref, bst_ref,
             dz_ref, dlnw_ref, dlnb_ref, dws_ref, dbst_ref):
        del dproj_in
        first = pl.program_id(0) == 0

        @pl.when(first)
        def _():
            dlnw_ref[...] = jnp.zeros_like(dlnw_ref)
            dlnb_ref[...] = jnp.zeros_like(dlnb_ref)
            dws_ref[...] = jnp.zeros_like(dws_ref)
            dbst_ref[...] = jnp.zeros_like(dbst_ref)

        mask = _gmlp_mask()
        lane = lax.broadcasted_iota(jnp.int32, (GB, LANE), 1)
        ur = u_ref[...].astype(F32)
        vr = v_ref[...].astype(F32)
        u, tu = _gelu(ur)
        v, tv = _gelu(vr)
        gu = _gelu_grad(ur, tu)
        gv = _gelu_grad(vr, tv)
        dy = dya_ref[...].astype(F32)
        dbst = jnp.zeros((GB, LANE), F32)
        dlnw_rows, dlnb_rows = [], []
        for g in range(GG):
            cs = slice(g * GD, (g + 1) * GD)
            vg = v[:, cs]
            mu = jnp.mean(vg, axis=-1, keepdims=True)
            vc = vg - mu
            var = jnp.mean(vc * vc, axis=-1, keepdims=True)
            rstd = lax.rsqrt(var + EPS)
            xhat = vc * rstd
            lw = lnw_ref[g:g + 1, :]
            vn = (xhat * lw + lnb_ref[g:g + 1, :]).astype(BF16)
            wsg = jnp.where(mask, ws_ref[g], 0.0).astype(BF16)
            bcol = bst_ref[:, g:g + 1]
            dyg = dy[:, cs]
            ug = u[:, cs]
            dsv = dyg * ug
            dsv_b = dsv.astype(BF16)
            dws_g = jnp.zeros((GB, GB), F32)
            bsum = jnp.zeros((GB, 1), F32)
            dvn_parts = []
            for blk in range(nblk):
                rs = slice(blk * GB, (blk + 1) * GB)
                sv = _dot(wsg, vn[rs, :]) + bcol
                dz_ref[rs, cs] = (dyg[rs, :] * sv * gu[rs, cs]).astype(BF16)
                dws_g = dws_g + _dot(dsv_b[rs, :], vn[rs, :], NT)
                bsum = bsum + jnp.sum(dsv[rs, :], axis=-1, keepdims=True)
                dvn_parts.append(_dot(wsg, dsv_b[rs, :], TN))
            dvn = jnp.concatenate(dvn_parts, axis=0)
            dws_ref[g] += jnp.where(mask, dws_g, 0.0)
            dbst = dbst + jnp.where(lane == g, bsum, 0.0)
            dlnw_rows.append(jnp.sum(dvn * xhat, axis=0, keepdims=True))
            dlnb_rows.append(jnp.sum(dvn, axis=0, keepdims=True))
            dxh = dvn * lw
            dvg = rstd * (dxh - jnp.mean(dxh, axis=-1, keepdims=True)
                          - xhat * jnp.mean(dxh * xhat, axis=-1, keepdims=True))
            dz_ref[:, GW + g * GD:GW + (g + 1) * GD] = (dvg * gv[:, cs]).astype(BF16)
        dlnw_ref[...] += jnp.concatenate(dlnw_rows, axis=0)
        dlnb_ref[...] += jnp.concatenate(dlnb_rows, axis=0)
        dbst_ref[...] += dbst

    full = lambda shape: pl.BlockSpec(shape, lambda i: tuple(0 for _ in shape))
    outs = pl.pallas_call(
        body, grid=(T // tm,),
        in_specs=[pl.BlockSpec((tm, GW), lambda i: (i, 2)), pl.BlockSpec((tm, GW), lambda i: (i, 3)),
                  pl.BlockSpec((tm, GW), lambda i: (i, 0)), pl.BlockSpec(memory_space=pl.ANY),
                  full((GG, GD)), full((GG, GD)), full((GG, GB, GB)), full((GB, GG))],
        out_specs=[pl.BlockSpec((tm, 2 * GW), lambda i: (i, 1)), full((GG, GD)), full((GG, GD)),
                   full((GG, GB, GB)), full((GB, LANE))],
        out_shape=[jax.ShapeDtypeStruct(dproj.shape, dproj.dtype), jax.ShapeDtypeStruct((GG, GD), F32),
                   jax.ShapeDtypeStruct((GG, GD), F32), jax.ShapeDtypeStruct((GG, GB, GB), F32),
                   jax.ShapeDtypeStruct((GB, LANE), F32)],
        input_output_aliases={3: 0}, name="gmlp_bwd",
        compiler_params=_params(("arbitrary",), 48))(proj, proj, dya, dproj, lnw, lnb, ws, bst)
    return outs


def _merge_fwd(ya_pre, yb_pre, proj, bias, wpa, wpb):
    T = proj.shape[0]
    tm = min(512, T)

    def body(ya_ref, yb_ref, g_ref, b_ref, wpa_ref, wpb_ref, m_ref, oa_ref, ob_ref):
        ya = _dot(ya_ref[...], wpa_ref[...])
        yb = _dot(yb_ref[...], wpb_ref[...])
        g = g_ref[...].astype(F32)
        sa = _sigmoid(g[:, :D] + b_ref[0:1, :])
        sb = _sigmoid(g[:, D:] + b_ref[1:2, :])
        m_ref[...] = (sa * ya + sb * yb).astype(BF16)
        oa_ref[...] = ya.astype(BF16)
        ob_ref[...] = yb.astype(BF16)

    row = lambda w: pl.BlockSpec((tm, w), lambda i: (i, 0))
    full = lambda shape: pl.BlockSpec(shape, lambda i: tuple(0 for _ in shape))
    o = jax.ShapeDtypeStruct((T, D), BF16)
    return pl.pallas_call(
        body, grid=(T // tm,),
        in_specs=[row(GW), row(SI), row(2 * D), full((2, D)), full((GW, D)), full((SI, D))],
        out_specs=[row(D), row(D), row(D)], out_shape=[o, o, o], name="merge_fwd",
        compiler_params=_params(("arbitrary",), 40))(ya_pre, yb_pre, proj, bias, wpa, wpb)


def _merge_bwd(dm, proj, bias, ya, yb, wpa, wpb):
    T = proj.shape[0]
    tm = min(512, T)

    def body(dm_ref, g_ref, b_ref, ya_ref, yb_ref, wpa_ref, wpb_ref,
             dg_ref, dya_ref, dyb_ref, dpa_ref, dpb_ref, db_ref):
        dmv = dm_ref[...].astype(F32)
        g = g_ref[...].astype(F32)
        sa = _sigmoid(g[:, :D] + b_ref[0:1, :])
        sb = _sigmoid(g[:, D:] + b_ref[1:2, :])
        dya = (dmv * sa).astype(BF16)
        dyb = (dmv * sb).astype(BF16)
        dga = dmv * ya_ref[...].astype(F32) * sa * (1.0 - sa)
        dgb = dmv * yb_ref[...].astype(F32) * sb * (1.0 - sb)
        dg_ref[:, :D] = dga.astype(BF16)
        dg_ref[:, D:] = dgb.astype(BF16)
        dya_ref[...] = dya
        dyb_ref[...] = dyb
        dpa_ref[...] = _dot(dya, wpa_ref[...], NT).astype(BF16)
        dpb_ref[...] = _dot(dyb, wpb_ref[...], NT).astype(BF16)
        part = jnp.concatenate([jnp.sum(dga, axis=0, keepdims=True), jnp.sum(dgb, axis=0, keepdims=True)], axis=0)

        @pl.when(pl.program_id(0) == 0)
        def _():
            db_ref[...] = part

        @pl.when(pl.program_id(0) > 0)
        def _():
            db_ref[...] += part

    row = lambda w: pl.BlockSpec((tm, w), lambda i: (i, 0))
    full = lambda shape: pl.BlockSpec(shape, lambda i: tuple(0 for _ in shape))
    o = lambda w: jax.ShapeDtypeStruct((T, w), BF16)
    return pl.pallas_call(
        body, grid=(T // tm,),
        in_specs=[row(D), row(2 * D), full((2, D)), row(D), row(D), full((GW, D)), full((SI, D))],
        out_specs=[row(2 * D), row(D), row(D), row(GW), row(SI), full((2, D))],
        out_shape=[o(PMAIN), o(D), o(D), o(GW), o(SI), jax.ShapeDtypeStruct((2, D), F32)], name="merge_bwd",
        compiler_params=_params(("arbitrary",), 48))(dm, proj, bias, ya, yb, wpa, wpb)


RB = 128


def _shift_matrix(j):
    r = lax.broadcasted_iota(jnp.int32, (RB, RB), 0)
    c = lax.broadcasted_iota(jnp.int32, (RB, RB), 1)
    return jnp.where(c == r - j, 1.0, 0.0).astype(BF16)


def _rows_down(xb, before, shifts):
    H = SUBLANE
    mats = [_shift_matrix(j) for j in shifts]
    outs = [[] for _ in shifts]
    for b in range(xb.shape[0] // RB):
        blk = xb[b * RB:(b + 1) * RB]
        edge = jnp.concatenate([before, blk[:2 * H].astype(F32)[:H]], axis=0)
        for i, j in enumerate(shifts):
            outs[i] += [edge[H - j:2 * H - j], _dot(mats[i], blk)[H:]]
        before = blk[RB - 2 * H:].astype(F32)[H:]
    return [jnp.concatenate(o, axis=0) for o in outs]


def _rows_up(xb, after, shifts):
    H = SUBLANE
    nb = xb.shape[0] // RB
    mats = [_shift_matrix(-j) for j in shifts]
    outs = [[] for _ in shifts]
    for b in range(nb):
        blk = xb[b * RB:(b + 1) * RB]
        nxt = xb[(b + 1) * RB:(b + 1) * RB + 2 * H].astype(F32)[:H] if b + 1 < nb else after
        edge = jnp.concatenate([blk[RB - 2 * H:].astype(F32)[H:], nxt], axis=0)
        for i, j in enumerate(shifts):
            outs[i] += [_dot(mats[i], blk)[:RB - H], edge[j:H + j]]
    return [jnp.concatenate(o, axis=0) for o in outs]


def _ffn_act_fwd(up, cw, cb):
    T = up.shape[0]
    tm = min(512, T)
    H = SUBLANE

    def body(up_ref, cw_ref, cb_ref, o_ref, xc_ref, halo):
        @pl.when(pl.program_id(1) == 0)
        def _():
            halo[...] = jnp.zeros_like(halo)

        xb = up_ref[...]
        x2, x1 = _rows_down(xb, halo[...], (2, 1))
        xc = cb_ref[...] + cw_ref[0:1, :] * x2 + cw_ref[1:2, :] * x1 + cw_ref[2:3, :] * xb.astype(F32)
        xc_ref[...] = xc.astype(BF16)
        gate = xc[:, :FT]
        o_ref[...] = (gate * _sigmoid(gate) * xc[:, FT:]).astype(BF16)
        halo[...] = xb[tm - 2 * H:].astype(F32)[H:]

    tile = pl.BlockSpec((tm, 2 * FT), lambda j, i: (i, j))
    return pl.pallas_call(
        body, grid=(2, T // tm),
        in_specs=[tile, pl.BlockSpec((FK, 2 * FT), lambda j, i: (0, j)), pl.BlockSpec((1, 2 * FT), lambda j, i: (0, j))],
        out_specs=[pl.BlockSpec((tm, FT), lambda j, i: (i, j)), tile],
        out_shape=[jax.ShapeDtypeStruct((T, DFF), BF16), jax.ShapeDtypeStruct((T, 2 * DFF), BF16)],
        scratch_shapes=[pltpu.VMEM((H, 2 * FT), F32)], name="ffn_act_fwd",
        compiler_params=_params(("arbitrary", "arbitrary"), 48))(up, cw, cb)


def _ffn_act_bwd(up, xc, dact, cw):
    T = up.shape[0]
    tm = min(512, T)
    nt = T // tm
    H = SUBLANE

    def body(up_ref, xc_ref, da_ref, cw_ref, dup_ref, dcw_ref, dcb_ref, ahead):
        @pl.when(pl.program_id(1) == 0)
        def _():
            ahead[...] = jnp.zeros_like(ahead)
            dcw_ref[...] = jnp.zeros_like(dcw_ref)
            dcb_ref[...] = jnp.zeros_like(dcb_ref)

        xcv = xc_ref[...].astype(F32)
        gate, val = xcv[:, :FT], xcv[:, FT:]
        sg = _sigmoid(gate)
        dav = da_ref[...].astype(F32)
        dgate = dav * val * sg * (1.0 + gate * (1.0 - sg))
        dval = dav * gate * sg
        dxc = jnp.concatenate([dgate, dval], axis=1)
        d1, d2 = _rows_up(dxc.astype(BF16), ahead[...], (1, 2))
        x = up_ref[...].astype(F32)
        dcb_ref[...] += jnp.sum(dxc, axis=0, keepdims=True)
        dcw_ref[...] += jnp.concatenate([jnp.sum(d * x, axis=0, keepdims=True) for d in (d2, d1, dxc)], axis=0)
        dup_ref[...] = (cw_ref[2:3, :] * dxc + cw_ref[1:2, :] * d1 + cw_ref[0:1, :] * d2).astype(BF16)
        ahead[...] = dxc[0:H, :]

    tile = pl.BlockSpec((tm, 2 * FT), lambda j, i: (nt - 1 - i, j))
    return pl.pallas_call(
        body, grid=(2, nt),
        in_specs=[tile, tile, pl.BlockSpec((tm, FT), lambda j, i: (nt - 1 - i, j)),
                  pl.BlockSpec((FK, 2 * FT), lambda j, i: (0, j))],
        out_specs=[tile, pl.BlockSpec((FK, 2 * FT), lambda j, i: (0, j)), pl.BlockSpec((1, 2 * FT), lambda j, i: (0, j))],
        out_shape=[jax.ShapeDtypeStruct((T, 2 * DFF), BF16), jax.ShapeDtypeStruct((FK, 2 * DFF), F32),
                   jax.ShapeDtypeStruct((1, 2 * DFF), F32)],
        scratch_shapes=[pltpu.VMEM((H, 2 * FT), F32)], name="ffn_act_bwd",
        compiler_params=_params(("arbitrary", "arbitrary"), 56))(up, xc, dact, cw)


def _softplus(x):
    e = jnp.exp(-jnp.abs(x))
    return jnp.maximum(x, 0.0) + jnp.where(e < 1e-4, e * (1.0 - 0.5 * e), jnp.log(1.0 + e))


def _ssd_consts():
    li = lax.broadcasted_iota(jnp.int32, (LS, LS), 0)
    si = lax.broadcasted_iota(jnp.int32, (LS, LS), 1)
    tril = si <= li
    hh = lax.broadcasted_iota(jnp.int32, (LANE, SI), 0)
    cc = lax.broadcasted_iota(jnp.int32, (LANE, SI), 1) // SP
    expand = jnp.where(hh == cc, 1.0, 0.0).astype(BF16)
    return tril, expand


def _ssd_pre(xc, dt_ref, dtb_ref, alog_ref, tril, expand):
    sx = _sigmoid(xc)
    xbc = xc * sx
    xs, bm, cm = xbc[:, :SI], xbc[:, SI:SI + SG * SN], xbc[:, SI + SG * SN:]
    dtin = dt_ref[...] + dtb_ref[...]
    dt = _softplus(dtin)
    a_neg = -jnp.exp(alog_ref[...])
    dta = dt * a_neg
    trilb = jnp.where(tril, 1.0, 0.0).astype(BF16)
    a = _dot3_rhs(trilb, dta, NN)
    a_exp = _dot3(a, expand, NN)
    dt_exp = _dot3(dt, expand, NN)
    xdt = xs * dt_exp
    a_last = a_exp[LS - 1:LS, :]
    return dict(xc=xc, sx=sx, xs=xs, bm=bm, cm=cm, dtin=dt_ref[...] + dtb_ref[...], dt=dt, a_neg=a_neg,
                a=a, a_t=a.T, a_exp=a_exp, dt_exp=dt_exp, xdt=xdt, ea=jnp.exp(a_exp),
                w=jnp.exp(a_last - a_exp), eal=jnp.exp(a_last))


def _head_decay(pre, tril, h):
    seg = pre["a"][:, h:h + 1] - pre["a_t"][h:h + 1, :]
    return jnp.exp(jnp.where(tril, seg, -1e30))


def _ssd_fwd(proj, dtraw, cw, cb, dtb, alog, dexp, nw):
    T = proj.shape[0]
    nc = T // LS
    H = SUBLANE

    def body(z_ref, x_ref, dt_ref, cw_ref, cb_ref, dtb_ref, alog_ref, dexp_ref, nw_ref,
             yb_ref, y_ref, sp_ref, xc_ref, halo, st):
        @pl.when(pl.program_id(0) == 0)
        def _():
            halo[...] = jnp.zeros_like(halo)
            st[...] = jnp.zeros_like(st)

        xb = x_ref[...]
        taps = _rows_down(xb, halo[...], (3, 2, 1)) + [xb.astype(F32)]
        xc = cb_ref[...]
        for k in range(SK):
            xc = xc + cw_ref[k:k + 1, :] * taps[k]
        xc_ref[...] = xc.astype(BF16)
        tril, expand = _ssd_consts()
        pre = _ssd_pre(xc, dt_ref, dtb_ref, alog_ref, tril, expand)
        lane = lax.broadcasted_iota(jnp.int32, (LS, LANE), 1)
        lo = lane < SP
        zf = z_ref[...].astype(F32)
        siluz = zf * _sigmoid(zf)
        for g in range(SG):
            gs = slice(g * SGW, (g + 1) * SGW)
            bg = pre["bm"][:, g * SN:(g + 1) * SN].astype(BF16)
            cg = pre["cm"][:, g * SN:(g + 1) * SN].astype(BF16)
            gmat = _dot(cg, bg, NT)
            sg = st[g]
            sp_ref[0, g] = sg
            yoff = _dot(cg, sg.astype(BF16))
            parts = []
            for j in range(SGW // LANE):
                h0 = g * (SGW // SP) + 2 * j
                m0 = gmat * _head_decay(pre, tril, h0)
                m1 = gmat * _head_decay(pre, tril, h0 + 1)
                xp = pre["xdt"][:, g * SGW + j * LANE:g * SGW + (j + 1) * LANE]
                rhs = jnp.concatenate([jnp.where(lo, xp, 0.0), jnp.where(lo, 0.0, xp)], axis=0).astype(BF16)
                parts.append(_dot(jnp.concatenate([m0, m1], axis=1).astype(BF16), rhs))
            y = (jnp.concatenate(parts, axis=1) + pre["ea"][:, gs] * yoff + dexp_ref[:, gs] * pre["xs"][:, gs])
            st[g] = pre["eal"][:, gs] * sg + _dot(bg, (pre["w"][:, gs] * pre["xdt"][:, gs]).astype(BF16), TN)
            y_ref[:, gs] = y
            yg = y * siluz[:, gs]
            r = lax.rsqrt(jnp.mean(yg * yg, axis=-1, keepdims=True) + EPS)
            yb_ref[:, gs] = (yg * r * nw_ref[:, gs]).astype(BF16)
        halo[...] = xb[LS - 2 * H:].astype(F32)[H:]

    vec = lambda w: pl.BlockSpec((1, w), lambda c: (0, 0))
    return pl.pallas_call(
        body, grid=(nc,),
        in_specs=[pl.BlockSpec((LS, SI), lambda c: (c, 2)), pl.BlockSpec((LS, SXBC), lambda c: (c, 2)),
                  pl.BlockSpec((LS, LANE), lambda c: (c, 0)),
                  pl.BlockSpec((SK, SXBC), lambda c: (0, 0)), vec(SXBC), vec(LANE), vec(LANE), vec(SI), vec(SI)],
        out_specs=[pl.BlockSpec((LS, SI), lambda c: (c, 0)), pl.BlockSpec((LS, SI), lambda c: (c, 0)),
                   pl.BlockSpec((1, SG, SN, SGW), lambda c: (c, 0, 0, 0)), pl.BlockSpec((LS, SXBC), lambda c: (c, 0))],
        out_shape=[jax.ShapeDtypeStruct((T, SI), BF16), jax.ShapeDtypeStruct((T, SI), F32),
                   jax.ShapeDtypeStruct((nc, SG, SN, SGW), F32), jax.ShapeDtypeStruct((T, SXBC), BF16)],
        scratch_shapes=[pltpu.VMEM((H, SXBC), F32), pltpu.VMEM((SG, SN, SGW), F32)], name="ssd_fwd",
        compiler_params=_params(("arbitrary",), VMEM_CAP_MB))(proj, proj, dtraw, cw, cb, dtb, alog, dexp, nw)


def _ssd_bwd(proj, xcs, dtraw, y, sprev, dyb, dproj, cw, dtb, alog, dexp, nw):
    T = proj.shape[0]
    nc = T // LS
    H = SUBLANE
    NJ = (SI + SXBC) // D
    J0 = (2 * D + 2 * GW) // D

    def body(z_ref, x_ref, xc_ref, dt_ref, y_ref, sp_ref, dyb_ref, dproj_in,
             cw_ref, dtb_ref, alog_ref, dexp_ref, nw_ref,
             dp_ref, ddt_ref, dcw_ref, dcb_ref, ddtb_ref, da_ref, dd_ref, dnw_ref,
             ahead, ds, stage):
        del dproj_in
        i = pl.program_id(0)
        j = pl.program_id(1)

        @pl.when(jnp.logical_and(i == 0, j == 0))
        def _():
            ahead[...] = jnp.zeros_like(ahead)
            ds[...] = jnp.zeros_like(ds)
            for r in (dcw_ref, dcb_ref, ddtb_ref, da_ref, dd_ref, dnw_ref):
                r[...] = jnp.zeros_like(r)

        @pl.when(j == 0)
        def _():
            tril, expand = _ssd_consts()
            pre = _ssd_pre(xc_ref[...].astype(F32), dt_ref, dtb_ref, alog_ref, tril, expand)
            lane = lax.broadcasted_iota(jnp.int32, (LS, LANE), 1)
            sub = lax.broadcasted_iota(jnp.int32, (LANE, LS), 0)
            rowi = lax.broadcasted_iota(jnp.int32, (LS, 1), 0)
            lo = lane < SP
            xs, xdt, ea, w, eal = pre["xs"], pre["xdt"], pre["ea"], pre["w"], pre["eal"]

            zf = z_ref[...].astype(F32)
            sz = _sigmoid(zf)
            siluz = zf * sz
            yv = y_ref[...]
            yg = yv * siluz
            dout = dyb_ref[...].astype(F32)
            dyg_parts, dnw_parts = [], []
            for g in range(SG):
                gs = slice(g * SGW, (g + 1) * SGW)
                ygg = yg[:, gs]
                r = lax.rsqrt(jnp.mean(ygg * ygg, axis=-1, keepdims=True) + EPS)
                yhat = ygg * r
                dn = dout[:, gs] * nw_ref[:, gs]
                dnw_parts.append(jnp.sum(dout[:, gs] * yhat, axis=0, keepdims=True))
                dyg_parts.append(r * (dn - yhat * jnp.mean(dn * yhat, axis=-1, keepdims=True)))
            dyg = jnp.concatenate(dyg_parts, axis=1)
            dnw_ref[...] += jnp.concatenate(dnw_parts, axis=1)
            dy = dyg * siluz
            stage[:, 0:SI] = (dyg * yv * sz * (1.0 + zf * (1.0 - sz))).astype(BF16)
            dd_ref[...] += jnp.sum(dy * xs, axis=0, keepdims=True)
            tt = ea * dy

            da_rows = jnp.zeros((LS, LANE), F32)
            da_cols = jnp.zeros((LANE, LS), F32)
            dxdt_parts, db_parts, dc_parts, daexp_parts = [], [], [], []
            for g in range(SG):
                gs = slice(g * SGW, (g + 1) * SGW)
                bg = pre["bm"][:, g * SN:(g + 1) * SN].astype(BF16)
                cg = pre["cm"][:, g * SN:(g + 1) * SN].astype(BF16)
                sg = sp_ref[0, g]
                sgb = sg.astype(BF16)
                dsg = ds[g]
                dsgb = dsg.astype(BF16)
                ttg = tt[:, gs].astype(BF16)
                yoff = _dot(cg, sgb)
                dc = _dot(ttg, sgb, NT)
                gmat = _dot(cg, bg, NT)
                dgm = jnp.zeros((LS, LS), F32)
                dxdt_pairs = []
                for jj in range(SGW // LANE):
                    h0 = g * (SGW // SP) + 2 * jj
                    ps = slice(g * SGW + jj * LANE, g * SGW + (jj + 1) * LANE)
                    l0 = _head_decay(pre, tril, h0)
                    l1 = _head_decay(pre, tril, h0 + 1)
                    m0 = gmat * l0
                    m1 = gmat * l1
                    dyp = dy[:, ps]
                    dy_lo = jnp.where(lo, dyp, 0.0).astype(BF16)
                    dy_hi = jnp.where(lo, 0.0, dyp).astype(BF16)
                    xpb = xdt[:, ps].astype(BF16)
                    dm0 = _dot(dy_lo, xpb, NT)
                    dm1 = _dot(dy_hi, xpb, NT)
                    q0 = dm0 * m0
                    q1 = dm1 * m1
                    da_rows = da_rows + jnp.where(lane == h0, jnp.sum(q0, axis=1, keepdims=True), 0.0)
                    da_rows = da_rows + jnp.where(lane == h0 + 1, jnp.sum(q1, axis=1, keepdims=True), 0.0)
                    da_cols = da_cols + jnp.where(sub == h0, jnp.sum(q0, axis=0, keepdims=True), 0.0)
                    da_cols = da_cols + jnp.where(sub == h0 + 1, jnp.sum(q1, axis=0, keepdims=True), 0.0)
                    dgm = dgm + dm0 * l0 + dm1 * l1
                    mcat = jnp.concatenate([m0, m1], axis=0).astype(BF16)
                    dycat = jnp.concatenate([dy_lo, dy_hi], axis=0)
                    dxdt_pairs.append(_dot(mcat, dycat, TN))
                dgb = dgm.astype(BF16)
                dc = dc + _dot(dgb, bg)
                db = _dot(dgb, cg, TN)
                zg = _dot(bg, dsgb)
                wg, xdtg = w[:, gs], xdt[:, gs]
                dxdt_g = jnp.concatenate(dxdt_pairs, axis=1) + wg * zg
                qg = zg * xdtg * wg
                last = (jnp.sum(qg, axis=0, keepdims=True)
                        + jnp.sum(dsg * sg, axis=0, keepdims=True) * eal[:, gs])
                daexp_parts.append(dy[:, gs] * ea[:, gs] * yoff - qg + jnp.where(rowi == LS - 1, last, 0.0))
                db = db + _dot((wg * xdtg).astype(BF16), dsgb, NT)
                ds[g] = eal[:, gs] * dsg + _dot(cg, ttg, TN)
                dxdt_parts.append(dxdt_g)
                db_parts.append(db)
                dc_parts.append(dc)
            dxdt = jnp.concatenate(dxdt_parts, axis=1)
            da_exp = jnp.concatenate(daexp_parts, axis=1)
            da = _dot3(da_exp, expand, NT) + da_rows - da_cols.T
            triub = jnp.where(tril, 1.0, 0.0).astype(BF16)
            ddta = _dot3_rhs(triub, da, TN)
            ddt = ddta * pre["a_neg"] + _dot3(dxdt * xs, expand, NT)
            da_ref[...] += jnp.sum(ddta * pre["dt"], axis=0, keepdims=True)
            ddt_raw = ddt * _sigmoid(pre["dtin"])
            ddt_ref[...] = ddt_raw
            ddtb_ref[...] += jnp.sum(ddt_raw, axis=0, keepdims=True)
            dxs = dexp_ref[...] * dy + dxdt * pre["dt_exp"]
            dxbc = jnp.concatenate([dxs] + db_parts + dc_parts, axis=1)
            sx, xc = pre["sx"], pre["xc"]
            dxc = dxbc * sx * (1.0 + xc * (1.0 - sx))
            taps = _rows_up(dxc.astype(BF16), ahead[...], (3, 2, 1)) + [dxc]
            xr = x_ref[...].astype(F32)
            dcb_ref[...] += jnp.sum(dxc, axis=0, keepdims=True)
            dcw_ref[...] += jnp.concatenate([jnp.sum(t * xr, axis=0, keepdims=True) for t in taps], axis=0)
            dxr = cw_ref[0:1, :] * taps[0]
            for k in range(1, SK):
                dxr = dxr + cw_ref[k:k + 1, :] * taps[k]
            stage[:, SI:] = dxr.astype(BF16)
            ahead[...] = dxc[0:H, :]

        dp_ref[...] = stage[:, pl.ds(pl.multiple_of(j * D, D), D)]

    vec = lambda w: pl.BlockSpec((1, w), lambda i, j: (0, 0))
    rev = lambda w, cb_: pl.BlockSpec((LS, w), lambda i, j: (nc - 1 - i, cb_))
    outs = pl.pallas_call(
        body, grid=(nc, NJ),
        in_specs=[rev(SI, 2), rev(SXBC, 2), rev(SXBC, 0),
                  rev(LANE, 0), rev(SI, 0),
                  pl.BlockSpec((1, SG, SN, SGW), lambda i, j: (nc - 1 - i, 0, 0, 0)),
                  rev(SI, 0), pl.BlockSpec(memory_space=pl.ANY),
                  pl.BlockSpec((SK, SXBC), lambda i, j: (0, 0)), vec(LANE), vec(LANE), vec(SI), vec(SI)],
        out_specs=[pl.BlockSpec((LS, D), lambda i, j: (nc - 1 - i, J0 + j)), rev(LANE, 0),
                   pl.BlockSpec((SK, SXBC), lambda i, j: (0, 0)), vec(SXBC), vec(LANE), vec(LANE), vec(SI), vec(SI)],
        out_shape=[jax.ShapeDtypeStruct(dproj.shape, dproj.dtype), jax.ShapeDtypeStruct((T, LANE), F32),
                   jax.ShapeDtypeStruct((SK, SXBC), F32), jax.ShapeDtypeStruct((1, SXBC), F32),
                   jax.ShapeDtypeStruct((1, LANE), F32), jax.ShapeDtypeStruct((1, LANE), F32),
                   jax.ShapeDtypeStruct((1, SI), F32), jax.ShapeDtypeStruct((1, SI), F32)],
        scratch_shapes=[pltpu.VMEM((H, SXBC), F32),
                        pltpu.VMEM((SG, SN, SGW), F32), pltpu.VMEM((LS, SI + SXBC), BF16)],
        input_output_aliases={7: 0}, name="ssd_bwd",
        compiler_params=_params(("arbitrary", "arbitrary"), VMEM_CAP_MB))(
            proj, proj, xcs, dtraw, y, sprev, dyb, dproj, cw, dtb, alog, dexp, nw)
    return outs


def _perm_ffn_cols(a):
    lead = a.shape[:-1]
    return a.reshape(lead + (2, 2, FT)).swapaxes(-3, -2).reshape(lead + (2 * DFF,))


def _perm_ffn_rows(a):
    return a.reshape((2, 2, FT) + a.shape[1:]).swapaxes(0, 1).reshape(a.shape)


def _pad_lanes(v, n=LANE):
    return jnp.pad(v, ((0, 0), (0, n - v.shape[-1])))


LATE = ["w_proj_a", "w_proj_b", "w_out", "ffn_w_up_t", "ffn_w_down"]


class _NoExchange:
    def gather_start(self):
        return None

    def gather_pass_on(self, outs):
        return None

    def late_weights(self, w, outs):
        return w

    def reduce_late(self, grads):
        return None

    def reduce_w_in(self, grad):
        return None

    def reduced(self, late_outs, w_in_outs):
        pass


def _local_step(x, tgt, w, hooks=None):
    hooks = hooks or _NoExchange()

    def mm(*args, side=None, **kw):
        out = _matmul(*args, side=side, **kw)
        return out if side is not None else (out, [])

    win_t = w["w_in_t"]
    win_dt = jnp.pad(w["w_in_t"][PMAIN:], ((0, LANE - SH), (0, 0)))
    fcw = _perm_ffn_cols(w["ffn_conv_w"])
    fcb = _perm_ffn_cols(w["ffn_conv_b"][None, :])
    mixw = w["mix_norm_w"][None, :]
    ffnw = w["ffn_norm_w"][None, :]
    finw = w["final_norm_w"][None, :]
    bst = w["gmlp_bs"].T
    scb = w["ssm_conv_b"][None, :]
    dtb = _pad_lanes(w["ssm_dt_bias"][None, :])
    alog = _pad_lanes(w["ssm_a_log"][None, :])
    dexp = jnp.repeat(w["ssm_d"], SP)[None, :]
    snw = w["ssm_norm_w"][None, :]

    xn = _rms_fwd(x, mixw, name="mix_norm")
    proj, got = mm(xn, win_t, name="in_proj", out_dtype=BF16, tb=True, tn=1536, j_outer=True, b_rows=PMAIN,
                   side=hooks.gather_start())
    dtraw, got = mm(xn, win_dt, name="in_proj_dt", out_dtype=F32, tb=True, side=hooks.gather_pass_on(got))
    w = hooks.late_weights(w, got)
    wup = _perm_ffn_rows(w["ffn_w_up_t"])
    ya_pre = _gmlp_fwd(proj, w["gmlp_ln_w"], w["gmlp_ln_b"], w["gmlp_ws"], bst)
    yb_pre, y_ssd, sprev, ssm_xc = _ssd_fwd(proj, dtraw, w["ssm_conv_w"], scb, dtb, alog, dexp, snw)
    merged, ya, yb = _merge_fwd(ya_pre, yb_pre, proj, w["gate_bias"], w["w_proj_a"], w["w_proj_b"])
    h1 = _matmul(merged, w["w_out"], name="out_proj", out_dtype=F32, add=x)
    hn = _rms_fwd(h1, ffnw, name="ffn_norm")
    up = _matmul(hn, wup, name="ffn_up", out_dtype=BF16, tb=True, tn=FT, j_outer=True)
    act, ffn_xc = _ffn_act_fwd(up, fcw, fcb)
    h2 = _matmul(act, w["ffn_w_down"], name="ffn_down", out_dtype=F32, tk=FT, add=h1)

    loss_row, dh2, d_finw = _loss_head(h2, tgt, finw)
    dact = _matmul(dh2, w["ffn_w_down"], name="ffn_down_dx", out_dtype=BF16, tb=True, tn=FT)
    d_wdown = _matmul(act, dh2, name="ffn_down_dw", out_dtype=F32, ta=True, tm=FT)
    dup, d_fcw, d_fcb = _ffn_act_bwd(up, ffn_xc, dact, fcw)
    dhn = _matmul(dup, wup, name="ffn_up_dx", out_dtype=F32, tk=FT)
    d_wup = _matmul(dup, hn, name="ffn_up_dw", out_dtype=F32, ta=True, tm=FT)
    dh1, d_ffnw = _rms_bwd(h1, ffnw, dhn, dh2, name="ffn_norm_bwd")
    dmerged = _matmul(dh1, w["w_out"], name="out_proj_dx", out_dtype=BF16, tb=True)
    d_wout = _matmul(merged, dh1, name="out_proj_dw", out_dtype=F32, ta=True)
    dproj, dya, dyb, dya_pre, dyb_pre, d_gbias = _merge_bwd(dmerged, proj, w["gate_bias"], ya, yb,
                                                           w["w_proj_a"], w["w_proj_b"])
    d_wpa = _matmul(ya_pre, dya, name="proj_a_dw", out_dtype=F32, ta=True)
    d_wpb = _matmul(yb_pre, dyb, name="proj_b_dw", out_dtype=F32, ta=True)
    dproj, d_lnw, d_lnb, d_ws, d_bst = _gmlp_bwd(proj, dya_pre, dproj, w["gmlp_ln_w"], w["gmlp_ln_b"],
                                                 w["gmlp_ws"], bst)
    dproj, ddt, d_scw, d_scb, d_dtb, d_a, d_dch, d_snw = _ssd_bwd(
        proj, ssm_xc, dtraw, y_ssd, sprev, dyb_pre, dproj, w["ssm_conv_w"], dtb, alog, dexp, snw)
    late = {"w_proj_a": d_wpa, "w_proj_b": d_wpb, "w_out": d_wout, "ffn_w_up_t": _perm_ffn_rows(d_wup),
            "ffn_w_down": d_wdown}
    d_win_main, late_outs = mm(dproj, xn, name="in_proj_dw", out_dtype=F32, ta=True, tm=1536,
                               side=hooks.reduce_late(late))
    d_win_dt = _matmul(ddt, xn, name="in_proj_dt_dw", out_dtype=F32, ta=True)
    d_win_t = jnp.concatenate([d_win_main, d_win_dt[:SH]], axis=0)
    dxn = _matmul(ddt, win_dt, name="in_proj_dt_dx", out_dtype=F32)
    dxn, w_in_outs = mm(dproj, win_t, name="in_proj_dx", out_dtype=F32, tk=1536, add=dxn, b_rows=PMAIN,
                        side=hooks.reduce_w_in(d_win_t))
    hooks.reduced(late_outs, w_in_outs)
    grad_x, d_mixw = _rms_bwd(x, mixw, dxn, dh1, name="mix_norm_bwd")

    a_neg = -jnp.exp(w["ssm_a_log"])
    grads = {
        "mix_norm_w": d_mixw[0],
        "w_in_t": d_win_t,
        "gate_bias": d_gbias,
        "gmlp_ln_w": d_lnw, "gmlp_ln_b": d_lnb, "gmlp_ws": d_ws, "gmlp_bs": d_bst[:, :GG].T,
        "ssm_conv_w": d_scw, "ssm_conv_b": d_scb[0],
        "ssm_dt_bias": d_dtb[0, :SH], "ssm_a_log": d_a[0, :SH] * a_neg,
        "ssm_d": d_dch.reshape(SH, SP).sum(axis=-1), "ssm_norm_w": d_snw[0],
        **late,
        "ffn_norm_w": d_ffnw[0],
        "ffn_conv_w": _perm_ffn_cols(d_fcw), "ffn_conv_b": _perm_ffn_cols(d_fcb)[0],
        "ffn_w_down": d_wdown, "final_norm_w": d_finw[0],
    }
    return loss_row[0, 0], grad_x, grads


MESH = pl.DeviceIdType.MESH
HBM_SPEC = pl.BlockSpec(memory_space=pltpu.HBM)


def _axes():
    return lax.axis_index("x"), lax.axis_index("y"), lax.axis_index("c")


def _all_gather(shards, *, name):
    na = len(shards)

    def body(*refs):
        x_refs, out_refs = refs[:na], refs[na:2 * na]
        send_sems, recv_sems, local_sems = refs[2 * na:]
        x, y, c = _axes()
        me, sibling = (x, y, c), (x, y, 1 - c)
        chips = [(1 - x, y), (x, 1 - y), (1 - x, 1 - y)]

        def slot(a, px, py, pc):
            return out_refs[a].at[4 * px + 2 * py + pc]

        def copy(a, k, block, to, src=None):
            return pltpu.make_async_remote_copy(
                src_ref=slot(a, *block) if src is None else src, dst_ref=slot(a, *block),
                send_sem=send_sems.at[7 * a + k], recv_sem=recv_sems.at[7 * a + k], device_id=to, device_id_type=MESH)

        mine = [pltpu.make_async_copy(x_refs[a], slot(a, *me), local_sems.at[a]) for a in range(na)]
        for cp in mine:
            cp.start()
        first = []
        for a in range(na):
            first.append(copy(a, 0, me, sibling, src=x_refs[a]))
            first += [copy(a, 1 + j, me, (*chip, c), src=x_refs[a]) for j, chip in enumerate(chips)]
        for cp in first:
            cp.start()
        passed = []
        for j, chip in enumerate(chips):
            for a in range(na):
                copy(a, 1 + j, (*chip, c), me).wait_recv()
                cp = copy(a, 4 + j, (*chip, c), sibling)
                cp.start()
                passed.append(cp)
        for a in range(na):
            copy(a, 0, sibling, me).wait_recv()
        for j, chip in enumerate(chips):
            for a in range(na):
                copy(a, 4 + j, (*chip, 1 - c), me).wait_recv()
        for cp in first + passed:
            cp.wait_send()
        for cp in mine:
            cp.wait()

    return pl.pallas_call(
        body, out_shape=[jax.ShapeDtypeStruct((NDEV,) + s.shape, s.dtype) for s in shards],
        in_specs=[HBM_SPEC] * na, out_specs=[HBM_SPEC] * na,
        scratch_shapes=[pltpu.SemaphoreType.DMA((7 * na,)), pltpu.SemaphoreType.DMA((7 * na,)),
                        pltpu.SemaphoreType.DMA((na,))],
        name=name)(*shards)


def _exchange(srcs, plan, *, name):
    na = len(srcs)
    n = len(plan(0, 0, 0))

    def body(*refs):
        src_refs, out_refs = refs[:na], refs[na:2 * na]
        send_sems, recv_sems = refs[2 * na:]
        x, y, c = _axes()
        copies = []
        for k, (slab, peer) in enumerate(plan(x, y, c)):
            for a in range(na):
                cp = pltpu.make_async_remote_copy(
                    src_ref=src_refs[a].at[slab], dst_ref=out_refs[a].at[k], send_sem=send_sems.at[n * a + k],
                    recv_sem=recv_sems.at[n * a + k], device_id=peer, device_id_type=MESH)
                cp.start()
                copies.append(cp)
        for cp in copies:
            cp.wait()

    return pl.pallas_call(
        body, out_shape=[jax.ShapeDtypeStruct((n,) + s.shape[1:], s.dtype) for s in srcs],
        in_specs=[HBM_SPEC] * na, out_specs=[HBM_SPEC] * na,
        scratch_shapes=[pltpu.SemaphoreType.DMA((n * na,)), pltpu.SemaphoreType.DMA((n * na,))], name=name)(*srcs)


def _to_sibling_plan(x, y, c):
    return [(2 * q + (1 - c), (x, y, 1 - c)) for q in range(4)]


def _to_chips_plan(x, y, c):
    q = 2 * x + y
    return [(q ^ 2, (1 - x, y, c)), (q ^ 1, (x, 1 - y, c)), (q ^ 3, (1 - x, 1 - y, c))]


def _row_tile(rows, row_bytes, budget=2 * 2 ** 20, align=2 * SUBLANE):
    if rows * row_bytes <= 2 * budget:
        return rows
    best = None
    for d in range(align, rows + 1, align):
        if rows % d == 0 and d * row_bytes <= budget:
            best = d
    return best or rows


def _pair_add(g, ra, c_idx, *, name):
    _, _, R, C = g.shape
    tr = _row_tile(R, C * 4, budget=3 * 2 ** 20)

    def body(c_ref, g_ref, ra_ref, o_ref):
        del c_ref
        o_ref[...] = (g_ref[0].astype(F32) + ra_ref[...].astype(F32)).astype(o_ref.dtype)

    return pl.pallas_call(
        body,
        grid_spec=pltpu.PrefetchScalarGridSpec(
            num_scalar_prefetch=1, grid=(4, R // tr),
            in_specs=[pl.BlockSpec((1, 1, tr, C), lambda q, r, cr: (q, cr[0], r, 0)),
                      pl.BlockSpec((1, tr, C), lambda q, r, cr: (q, r, 0))],
            out_specs=pl.BlockSpec((1, tr, C), lambda q, r, cr: (q, r, 0))),
        out_shape=jax.ShapeDtypeStruct((4, R, C), g.dtype), name=name,
        compiler_params=_params(("arbitrary", "arbitrary"), 24))(c_idx, g, ra)


def _grad_sum(p, rb, q_idx, *, name):
    _, R, C = p.shape
    tr = _row_tile(R, C * 4, budget=3 * 2 ** 20)

    def body(q_ref, p_ref, rb_ref, o_ref):
        del q_ref
        g = p_ref[0].astype(F32)
        for k in range(3):
            g = g + rb_ref[k].astype(F32)
        o_ref[...] = g

    return pl.pallas_call(
        body,
        grid_spec=pltpu.PrefetchScalarGridSpec(
            num_scalar_prefetch=1, grid=(R // tr,),
            in_specs=[pl.BlockSpec((1, tr, C), lambda r, qr: (qr[0], r, 0)),
                      pl.BlockSpec((3, tr, C), lambda r, qr: (0, r, 0))],
            out_specs=pl.BlockSpec((tr, C), lambda r, qr: (r, 0))),
        out_shape=jax.ShapeDtypeStruct((R, C), F32), name=name,
        compiler_params=_params(("arbitrary",), 40))(q_idx, p, rb)


def _adamw(g, w, m, v):
    m = ADAM_B1 * m + (1.0 - ADAM_B1) * g
    v = ADAM_B2 * v + (1.0 - ADAM_B2) * (g * g)
    m_hat = m / (1.0 - ADAM_B1 ** ADAM_STEP)
    v_hat = v / (1.0 - ADAM_B2 ** ADAM_STEP)
    delta = -ADAM_LR * (m_hat / (jnp.sqrt(v_hat) + ADAM_EPS) + ADAM_WD * w)
    return delta, m, v


def _adam(g, w, m, v, *, name):
    _, R, C = w.shape
    tr = _row_tile(R, C * 4, budget=2 ** 20, align=SUBLANE)

    def body(g_ref, w_ref, m_ref, v_ref, d_out, m_out, v_out):
        delta, mn, vn = _adamw(g_ref[...], w_ref[...], m_ref[...], v_ref[...])
        d_out[...] = delta
        m_out[...] = mn
        v_out[...] = vn

    row = pl.BlockSpec((1, tr, C), lambda r: (0, r, 0))
    o = jax.ShapeDtypeStruct((1, R, C), F32)
    return pl.pallas_call(
        body, grid=(R // tr,), in_specs=[row, row, row, row], out_specs=[row, row, row], out_shape=[o, o, o],
        name=name, compiler_params=_params(("arbitrary",), 32))(g, w, m, v)


def _vmem_specs(n):
    return [pl.BlockSpec(memory_space=pltpu.VMEM)] * n


def _pair_sum_many(mine, theirs, *, name):
    n = len(mine)

    def body(*refs):
        for a in range(n):
            refs[2 * n + a][...] = refs[a][...] + refs[n + a][0]

    return pl.pallas_call(
        body, out_shape=[jax.ShapeDtypeStruct(m.shape, m.dtype) for m in mine], in_specs=_vmem_specs(2 * n),
        out_specs=_vmem_specs(n), name=name)(*mine, *theirs)


def _chip_sum_many(own, recv, q_idx, *, name):
    n = len(own)

    def body(q_ref, *refs):
        q = q_ref[0]
        for a in range(n):
            mine, r = refs[a][...], refs[n + a]
            total = None
            for chip in range(4):
                e = q ^ chip
                term = jnp.where(e == 0, mine, jnp.where(e == 2, r[0], jnp.where(e == 1, r[1], r[2])))
                total = term if total is None else total + term
            refs[2 * n + a][...] = total

    return pl.pallas_call(
        body, out_shape=[jax.ShapeDtypeStruct(m.shape, m.dtype) for m in own],
        in_specs=[pl.BlockSpec(memory_space=pltpu.SMEM)] + _vmem_specs(2 * n), out_specs=_vmem_specs(n),
        name=name)(q_idx, *own, *recv)


def _adam_many(gs, ws, ms, vs, *, name):
    n = len(gs)

    def body(*refs):
        for a in range(n):
            delta, mn, vn = _adamw(*(refs[k * n + a][...] for k in range(4)))
            refs[4 * n + a][...] = delta
            refs[5 * n + a][...] = mn
            refs[6 * n + a][...] = vn

    shapes = [jax.ShapeDtypeStruct(w.shape, w.dtype) for w in ws]
    out = pl.pallas_call(body, out_shape=shapes * 3, in_specs=_vmem_specs(4 * n), out_specs=_vmem_specs(3 * n),
                         name=name)(*gs, *ws, *ms, *vs)
    return out[:n], out[n:2 * n], out[2 * n:]


WEIGHTS = ["mix_norm_w", "w_in", "gate_bias", "gmlp_ln_w", "gmlp_ln_b", "gmlp_ws", "gmlp_bs", "ssm_conv_w",
           "ssm_conv_b", "ssm_dt_bias", "ssm_a_log", "ssm_d", "ssm_norm_w", "w_proj_a", "w_proj_b", "w_out",
           "ffn_norm_w", "ffn_w_up", "ffn_conv_w", "ffn_conv_b", "ffn_w_down", "final_norm_w"]
SHARDED = {"w_in": ((D, IN_COLS), 1), "gate_bias": ((2, D), 1), "ssm_conv_w": ((SK, SXBC), 1),
           "w_proj_a": ((GW, D), 0), "w_proj_b": ((SI, D), 0), "w_out": ((D, D), 0),
           "ffn_w_up": ((D, 2 * DFF), 1), "ffn_conv_w": ((FK, 2 * DFF), 1), "ffn_w_down": ((DFF, D), 0)}
REPLICATED = {"mix_norm_w": (D,), "gmlp_ln_w": (GG, GD), "gmlp_ln_b": (GG, GD), "gmlp_ws": (GG, GB, GB),
              "gmlp_bs": (GG, GB), "ssm_conv_b": (SXBC,), "ssm_dt_bias": (SH,), "ssm_a_log": (SH,), "ssm_d": (SH,),
              "ssm_norm_w": (SI,), "ffn_norm_w": (D,), "ffn_conv_b": (2 * DFF,), "final_norm_w": (D,)}
REPL_ORDER = [n for n in WEIGHTS if n in REPLICATED]
BTILE = 2 * SUBLANE
WIN_R = IN_COLS // NDEV
WIN_P = WIN_R + BTILE - WIN_R % BTILE
WIN_A = [WIN_R * d // BTILE * BTILE for d in range(NDEV)]
assert all(WIN_A[d] + WIN_P >= WIN_R * (d + 1) for d in range(NDEV)) and WIN_A[-1] + WIN_P == IN_COLS
BIG = [("w_proj_a", GW // NDEV, False), ("w_proj_b", SI // NDEV, False), ("w_out", D // NDEV, False),
       ("ffn_w_up", 2 * DFF // NDEV, True), ("ffn_w_down", DFF // NDEV, False), ("w_in", WIN_P, True)]
VECTORS = ["gate_bias", "ssm_conv_w", "ffn_conv_w"]


def _round_up(n, k):
    return (n + k - 1) // k * k


BIG_OFF = {}
_off = 0
for _n, _r, _t in BIG:
    BIG_OFF[_n] = _off
    _off += _r
BIG_USED = _off
BIG_ROWS = _round_up(BIG_USED, 2 * SUBLANE)
assert all(BIG_OFF[n] % (2 * SUBLANE) == 0 for n, _, _ in BIG)
VEC_SHAPE = {n: (SHARDED[n][0][0], SHARDED[n][0][1] // NDEV) for n in VECTORS}


def _win_offset(dev):
    return WIN_R * dev - WIN_R * dev // BTILE * BTILE


def _pack_big(arrs, dtype, dev):
    parts = []
    for n, r, t in BIG:
        a = (arrs[n].T if t else arrs[n]).astype(dtype)
        if n == "w_in":
            a = lax.dynamic_update_slice(jnp.zeros((WIN_P, D), dtype), a, (_win_offset(dev), 0))
        parts.append(a)
    parts.append(jnp.zeros((BIG_ROWS - BIG_USED, D), dtype))
    return jnp.concatenate(parts, axis=0)


def _join_windows(win):
    parts = []
    for d in range(NDEV):
        lo = BTILE if WIN_A[d] % WIN_R else 0
        if lo:
            parts.append(win[d - 1, WIN_P - BTILE:] + win[d, :BTILE])
        hi = WIN_P - BTILE if d + 1 < NDEV and WIN_A[d + 1] < WIN_A[d] + WIN_P else WIN_P
        parts.append(win[d, lo:hi])
    return jnp.concatenate(parts, axis=0)


def _split_windows(g):
    return jnp.stack([g[a:a + WIN_P] for a in WIN_A])


LATE_ROWS = BIG_OFF["w_in"]
assert LATE_ROWS + WIN_P == BIG_ROWS and BIG[-1][0] == "w_in"


def _remote(src, dst, send_sems, recv_sems, k, to):
    return pltpu.make_async_remote_copy(src_ref=src, dst_ref=dst, send_sem=send_sems.at[k], recv_sem=recv_sems.at[k],
                                        device_id=to, device_id_type=MESH)


class _Exchange:
    def __init__(self, late_shard, c_idx):
        self.late_shard, self.c_idx = late_shard, c_idx

    def gather_start(self):
        shard = self.late_shard

        def make(ins, outs, send_sems, recv_sems):
            (x_ref,), (out,) = ins, outs
            x, y, c = _axes()
            mine = out.at[4 * x + 2 * y + c]
            peers = [(x, y, 1 - c), (1 - x, y, c), (x, 1 - y, c), (1 - x, 1 - y, c)]
            copies = [_remote(x_ref, mine, send_sems, recv_sems, k, p) for k, p in enumerate(peers)]
            return copies + [pltpu.make_async_copy(x_ref, mine, send_sems.at[len(peers)])]

        return _Side([shard], [jax.ShapeDtypeStruct((NDEV,) + shard.shape, shard.dtype)], 5, make)

    def gather_pass_on(self, outs):
        (buf,) = outs

        def make(ins, outs, send_sems, recv_sems):
            (src,), (dst,) = ins, outs
            x, y, c = _axes()
            slots = [4 * px + 2 * py + c for px, py in [(1 - x, y), (x, 1 - y), (1 - x, 1 - y)]]
            return [_remote(src.at[s], dst.at[s], send_sems, recv_sems, k, (x, y, 1 - c)) for k, s in enumerate(slots)]

        return _Side([buf], [jax.ShapeDtypeStruct(buf.shape, buf.dtype)], 3, make, aliases=[(0, 0)])

    def late_weights(self, w, outs):
        (buf,) = outs
        w = dict(w)
        for n, r, t in BIG[:-1]:
            w[n + "_t" if t else n] = buf[:, BIG_OFF[n]:BIG_OFF[n] + r].reshape(NDEV * r, D)
        return w

    def _reduce(self, send, tag):
        (sib,) = _exchange([send], _to_sibling_plan, name=tag + "_grads_to_sibling")
        sums = _pair_add(send.reshape((4, 2) + send.shape[1:]), sib, self.c_idx, name=tag + "_grad_pair_add")

        def make(ins, outs, send_sems, recv_sems):
            (src,), (dst,) = ins, outs
            return [_remote(src.at[slab], dst.at[k], send_sems, recv_sems, k, peer)
                    for k, (slab, peer) in enumerate(_to_chips_plan(*_axes()))]

        return sums, _Side([sums], [jax.ShapeDtypeStruct((3,) + sums.shape[1:], sums.dtype)], 3, make)

    def reduce_late(self, grads):
        send = jnp.concatenate([grads[n + "_t" if t else n].reshape(NDEV, r, D) for n, r, t in BIG[:-1]], axis=1)
        self.late_sum, side = self._reduce(send.astype(BF16), "late")
        return side

    def reduce_w_in(self, grad):
        self.w_in_sum, side = self._reduce(_split_windows(grad).astype(BF16), "w_in")
        return side

    def reduced(self, late_outs, w_in_outs):
        (self.late_from_chips,), (self.w_in_from_chips,) = late_outs, w_in_outs


def kernel(x, mix_norm_w, w_in, gate_bias, gmlp_ln_w, gmlp_ln_b, gmlp_ws, gmlp_bs, ssm_conv_w, ssm_conv_b, ssm_dt_bias, ssm_a_log, ssm_d, ssm_norm_w, w_proj_a, w_proj_b, w_out, ffn_norm_w, ffn_w_up, ffn_conv_w, ffn_conv_b, ffn_w_down, final_norm_w, loss_target, m_mix_norm_w, m_w_in, m_gate_bias, m_gmlp_ln_w, m_gmlp_ln_b, m_gmlp_ws, m_gmlp_bs, m_ssm_conv_w, m_ssm_conv_b, m_ssm_dt_bias, m_ssm_a_log, m_ssm_d, m_ssm_norm_w, m_w_proj_a, m_w_proj_b, m_w_out, m_ffn_norm_w, m_ffn_w_up, m_ffn_conv_w, m_ffn_conv_b, m_ffn_w_down, m_final_norm_w, v_mix_norm_w, v_w_in, v_gate_bias, v_gmlp_ln_w, v_gmlp_ln_b, v_gmlp_ws, v_gmlp_bs, v_ssm_conv_w, v_ssm_conv_b, v_ssm_dt_bias, v_ssm_a_log, v_ssm_d, v_ssm_norm_w, v_w_proj_a, v_w_proj_b, v_w_out, v_ffn_norm_w, v_ffn_w_up, v_ffn_conv_w, v_ffn_conv_b, v_ffn_w_down, v_final_norm_w):
    given = dict(locals())
    wts = {n: given[n] for n in WEIGHTS}
    mom = {n: given["m_" + n] for n in WEIGHTS}
    var = {n: given["v_" + n] for n in WEIGHTS}
    xi, yi, ci = _axes()
    c_idx = jnp.reshape(ci, (1,)).astype(jnp.int32)
    q_idx = jnp.reshape(2 * xi + yi, (1,)).astype(jnp.int32)
    big_names = [n for n, _, _ in BIG]
    drop = lambda d, names: {n: d[n][0] for n in names}

    dev = 4 * xi + 2 * yi + ci
    packed = _pack_big(drop(wts, big_names), BF16, dev)
    gathered = _all_gather([packed[LATE_ROWS:]] + [wts[n] for n in VECTORS], name="w_in_all_gather")
    full = {"w_in_t": _join_windows(gathered[0])}
    for n, a in zip(VECTORS, gathered[1:]):
        r, c = VEC_SHAPE[n]
        full[n] = a[:, 0].transpose(1, 0, 2).reshape(r, NDEV * c)
    for n in REPL_ORDER:
        full[n] = wts[n].reshape(REPLICATED[n])

    hooks = _Exchange(packed[:LATE_ROWS], c_idx)
    loss_local, grad_x, grads = _local_step(x[0], loss_target[0], full, hooks)
    g_late = _grad_sum(hooks.late_sum, hooks.late_from_chips, q_idx, name="late_grad_sum")
    g_win = _grad_sum(hooks.w_in_sum, hooks.w_in_from_chips, q_idx, name="w_in_grad_sum")

    small = VECTORS + REPL_ORDER
    as_2d = lambda a: a if a.ndim >= 2 else a[None]
    part = [grads[n].reshape((1,) + SHARDED[n][0] if n in VECTORS else as_2d(wts[n]).shape) for n in small]
    from_sibling = _exchange([p[None] for p in part], lambda x, y, c: [(0, (x, y, 1 - c))],
                             name="small_grads_to_sibling")
    chip_sums = _pair_sum_many(part, from_sibling, name="small_grad_pair_sum")
    from_chips = _exchange([s[None] for s in chip_sums],
                           lambda x, y, c: [(0, (1 - x, y, c)), (0, (x, 1 - y, c)), (0, (1 - x, 1 - y, c))],
                           name="small_grads_to_chips")
    g_small = dict(zip(small, _chip_sum_many(chip_sums, from_chips, q_idx, name="small_grad_chip_sum")))
    for n in VECTORS:
        c = VEC_SHAPE[n][1]
        g_small[n] = lax.dynamic_slice_in_dim(g_small[n], dev * c, c, axis=2)

    outs = {}
    small_g = [g_small[n] for n in small]
    small_out = _adam_many(small_g, *[[as_2d(d[n]) for n in small] for d in (wts, mom, var)], name="adam_small")
    for i, n in enumerate(small):
        outs[n] = tuple(a[i].reshape(wts[n].shape) for a in (small_g,) + tuple(small_out))
    for n, r, t in BIG:
        if n == "w_in":
            g = lax.dynamic_slice(g_win, (_win_offset(dev), 0), (WIN_R, D))
        else:
            g = g_late[BIG_OFF[n]:BIG_OFF[n] + r]
        g = (g.T if t else g)[None]
        outs[n] = (g,) + tuple(_adam(g, wts[n], mom[n], var[n], name="adam_" + n))

    loss = lax.psum(loss_local, ("x", "y", "c"))
    return (loss, grad_x[None]) + tuple(outs[n][k] for k in range(4) for n in WEIGHTS)
```

```python
import functools

import jax
import jax.numpy as jnp
from jax import lax
from jax.experimental import pallas as pl
from jax.experimental.pallas import tpu as pltpu

F32 = jnp.float32
BF16 = jnp.bfloat16

D = 1024
EPS = 1e-5
GW = 1024
GB = 128
GG = 8
GD = 128
GCH = 64
SI = 2048
SH = 32
SP = 64
SG = 4
SN = 128
SGW = SI // SG
SK = 4
SXBC = SI + 2 * SG * SN
DFF = 2816
FK = 3
PMAIN = 2 * D + 2 * GW + SI + SXBC
IN_COLS = PMAIN + SH
DP_SSM = SI + SXBC
DP_GAP = (DP_SSM - (2 * D + 2 * GW) % DP_SSM) % DP_SSM
DP_COLS = 2 * D + 2 * GW + DP_GAP + DP_SSM
assert DP_GAP % D == 0 and (2 * D + 2 * GW) % D == 0
NDEV = 8
ADAM_LR, ADAM_B1, ADAM_B2, ADAM_EPS, ADAM_WD, ADAM_STEP = 0.001, 0.9, 0.999, 1e-08, 0.01, 10

LANE = 128
SUBLANE = 8
VMEM_MB_V7X = 64
VMEM_CAP_MB = VMEM_MB_V7X - 8

LS = 128
FT = DFF // 2

NN = (((1,), (0,)), ((), ()))
NT = (((1,), (1,)), ((), ()))
TN = (((0,), (0,)), ((), ()))


def _params(sem, vmem_mb):
    return pltpu.CompilerParams(dimension_semantics=sem,
                                vmem_limit_bytes=min(int(vmem_mb), VMEM_CAP_MB) * 1024 * 1024)


def _dot(a, b, dims=NN):
    return lax.dot_general(a, b, dims, preferred_element_type=F32)


def _sigmoid(x):
    return 1.0 / (1.0 + jnp.exp(-x))


def _split3(v):
    hi = v.astype(BF16)
    r = v - hi.astype(F32)
    mid = r.astype(BF16)
    lo = (r - mid.astype(F32)).astype(BF16)
    return hi, mid, lo


def _dot3(a_f32, b_bf16, dims):
    hi, mid, lo = _split3(a_f32)
    return _dot(hi, b_bf16, dims) + _dot(mid, b_bf16, dims) + _dot(lo, b_bf16, dims)


def _dot3_rhs(a_bf16, b_f32, dims):
    hi, mid, lo = _split3(b_f32)
    return _dot(a_bf16, hi, dims) + _dot(a_bf16, mid, dims) + _dot(a_bf16, lo, dims)


def _matmul(a, b, *, name, out_dtype, ta=False, tb=False, tm=1024, tn=1024, tk=1024, add=None,
            j_outer=False, b_rows=None, a_gap=None, side=None):
    gap0, gapw = a_gap or (0, 0)
    if ta:
        K, M = a.shape
        M -= gapw
    else:
        M, K = a.shape
        K -= gapw
    if tb:
        N, K2 = b.shape
        N = b_rows or N
    else:
        K2, N = b.shape
        K2 = b_rows or K2
    assert K == K2, (a.shape, b.shape, ta, tb)
    tm, tn, tk = min(tm, M), min(tn, N), min(tk, K)
    assert M % tm == 0 and N % tn == 0 and K % tk == 0, (M, N, K, tm, tn, tk)
    nk = K // tk
    dims = (((0 if ta else 1,), (1 if tb else 0,)), ((), ()))
    has_add = add is not None
    n_in = 3 if has_add else 2
    s_in = len(side.inputs) if side else 0
    s_out = len(side.out_shapes) if side else 0
    grid = (N // tn, M // tm, nk) if j_outer else (M // tm, N // tn, nk)

    def body(*refs):
        a_ref, b_ref = refs[:2]
        add_ref = refs[2] if has_add else None
        o_ref = refs[n_in + s_in]
        if side:
            side_refs = (refs[n_in:n_in + s_in], refs[n_in + s_in + 1:n_in + s_in + 1 + s_out]) + tuple(refs[-2:])
            ids = [pl.program_id(d) for d in range(3)]
            first = functools.reduce(jnp.logical_and, [i == 0 for i in ids])
            last = functools.reduce(jnp.logical_and, [i == g - 1 for i, g in zip(ids, grid)])

            @pl.when(first)
            def _():
                for cp in side.make(*side_refs):
                    cp.start()

            @pl.when(last)
            def _():
                for cp in side.make(*side_refs):
                    cp.wait()

        p = lax.dot_general(a_ref[...].astype(BF16), b_ref[...].astype(BF16), dims,
                            preferred_element_type=F32)

        def finish(acc):
            if has_add:
                acc = acc + add_ref[...].astype(F32)
            o_ref[...] = acc.astype(o_ref.dtype)

        if nk == 1:
            finish(p)
        else:
            acc_ref = refs[n_in + s_in + 1 + s_out]
            k = pl.program_id(2)

            @pl.when(k == 0)
            def _():
                acc_ref[...] = p

            @pl.when(jnp.logical_and(k > 0, k < nk - 1))
            def _():
                acc_ref[...] += p

            @pl.when(k == nk - 1)
            def _():
                finish(acc_ref[...] + p)

    if j_outer:
        ij = lambda g0, g1: (g1, g0)
    else:
        ij = lambda g0, g1: (g0, g1)

    ta_col = tm if ta else tk
    assert gap0 % ta_col == 0 and gapw % ta_col == 0, (a_gap, ta_col)

    def a_map(g0, g1, k):
        i, _ = ij(g0, g1)
        col = i if ta else k
        col = col + jnp.where(col >= gap0 // ta_col, gapw // ta_col, 0) if gapw else col
        return (k, col) if ta else (i, col)

    def b_map(g0, g1, k):
        _, j = ij(g0, g1)
        return (j, k) if tb else (k, j)

    def o_map(g0, g1, k):
        return ij(g0, g1)

    in_specs = [pl.BlockSpec((tk, tm) if ta else (tm, tk), a_map),
                pl.BlockSpec((tn, tk) if tb else (tk, tn), b_map)]
    args = [a, b]
    if has_add:
        in_specs.append(pl.BlockSpec((tm, tn), o_map))
        args.append(add)
    scratch = [pltpu.VMEM((tm, tn), F32)] if nk > 1 else []
    osz = jnp.dtype(out_dtype).itemsize
    est = (2 * (tm * tk * a.dtype.itemsize + tk * tn * b.dtype.itemsize) + 2 * tm * tn * osz
           + (2 * tm * tn * add.dtype.itemsize if has_add else 0)
           + 3 * tm * tn * 4 + 2 * (tm * tk + tk * tn)) / 2 ** 20 + 4
    out_specs = [pl.BlockSpec((tm, tn), o_map)]
    out_shape = [jax.ShapeDtypeStruct((M, N), out_dtype)]
    aliases = {}
    if side:
        hbm = pl.BlockSpec(memory_space=pltpu.HBM)
        in_specs += [hbm] * s_in
        args += list(side.inputs)
        out_specs += [hbm] * s_out
        out_shape += list(side.out_shapes)
        scratch += [pltpu.SemaphoreType.DMA((side.nsem,)), pltpu.SemaphoreType.DMA((side.nsem,))]
        aliases = {n_in + i: 1 + j for i, j in side.aliases}
    outs = pl.pallas_call(
        body, grid=grid, in_specs=in_specs, out_specs=out_specs, out_shape=out_shape, scratch_shapes=scratch,
        input_output_aliases=aliases, name=name,
        compiler_params=_params(("arbitrary", "arbitrary", "arbitrary"), est))(*args)
    return (outs[0], list(outs[1:])) if side else outs[0]


class _Side:
    def __init__(self, inputs, out_shapes, nsem, make, aliases=()):
        self.inputs, self.out_shapes, self.nsem, self.make, self.aliases = inputs, out_shapes, nsem, make, aliases


def _rms_fwd(x, w, *, name):
    T = x.shape[0]
    tm = min(512, T)

    def body(x_ref, w_ref, o_ref):
        xv = x_ref[...]
        r = lax.rsqrt(jnp.mean(xv * xv, axis=-1, keepdims=True) + EPS)
        o_ref[...] = (xv * r * w_ref[...]).astype(BF16)

    return pl.pallas_call(
        body, grid=(T // tm,),
        in_specs=[pl.BlockSpec((tm, D), lambda i: (i, 0)), pl.BlockSpec((1, D), lambda i: (0, 0))],
        out_specs=pl.BlockSpec((tm, D), lambda i: (i, 0)),
        out_shape=jax.ShapeDtypeStruct((T, D), BF16), name=name,
        compiler_params=_params(("arbitrary",), 24))(x, w)


def _rms_bwd(x, w, dy, dres, *, name):
    T = x.shape[0]
    tm = min(512, T)

    def body(x_ref, w_ref, dy_ref, dres_ref, dx_ref, dw_ref):
        xv = x_ref[...]
        r = lax.rsqrt(jnp.mean(xv * xv, axis=-1, keepdims=True) + EPS)
        xhat = xv * r
        dyv = dy_ref[...].astype(F32)
        g = dyv * w_ref[...]
        dx_ref[...] = dres_ref[...] + r * (g - xhat * jnp.mean(g * xhat, axis=-1, keepdims=True))
        part = jnp.sum(dyv * xhat, axis=0, keepdims=True)

        @pl.when(pl.program_id(0) == 0)
        def _():
            dw_ref[...] = part

        @pl.when(pl.program_id(0) > 0)
        def _():
            dw_ref[...] += part

    row = pl.BlockSpec((tm, D), lambda i: (i, 0))
    vec = pl.BlockSpec((1, D), lambda i: (0, 0))
    return pl.pallas_call(
        body, grid=(T // tm,), in_specs=[row, vec, row, row], out_specs=[row, vec],
        out_shape=[jax.ShapeDtypeStruct((T, D), F32), jax.ShapeDtypeStruct((1, D), F32)], name=name,
        compiler_params=_params(("arbitrary",), 32))(x, w, dy, dres)


def _loss_head(h, tgt, w):
    T = h.shape[0]
    tm = min(512, T)

    def body(h_ref, t_ref, w_ref, loss_ref, dh_ref, dw_ref):
        hv = h_ref[...]
        r = lax.rsqrt(jnp.mean(hv * hv, axis=-1, keepdims=True) + EPS)
        xhat = hv * r
        wv = w_ref[...]
        err = xhat * wv - t_ref[...]
        lpart = 0.5 * jnp.sum(jnp.mean(err * err, axis=-1, keepdims=True), axis=0, keepdims=True)
        dy = err * (1.0 / D)
        g = dy * wv
        dh_ref[...] = r * (g - xhat * jnp.mean(g * xhat, axis=-1, keepdims=True))
        wpart = jnp.sum(dy * xhat, axis=0, keepdims=True)
        lrow = jnp.broadcast_to(lpart, (1, LANE))

        @pl.when(pl.program_id(0) == 0)
        def _():
            dw_ref[...] = wpart
            loss_ref[...] = lrow

        @pl.when(pl.program_id(0) > 0)
        def _():
            dw_ref[...] += wpart
            loss_ref[...] += lrow

    row = pl.BlockSpec((tm, D), lambda i: (i, 0))
    vec = pl.BlockSpec((1, D), lambda i: (0, 0))
    return pl.pallas_call(
        body, grid=(T // tm,), in_specs=[row, row, vec],
        out_specs=[pl.BlockSpec((1, LANE), lambda i: (0, 0)), row, vec],
        out_shape=[jax.ShapeDtypeStruct((1, LANE), F32), jax.ShapeDtypeStruct((T, D), F32),
                   jax.ShapeDtypeStruct((1, D), F32)], name="loss_head",
        compiler_params=_params(("arbitrary",), 32))(h, tgt, w)


_GELU_C = 0.7978845608028654
_GELU_A = 0.044715


def _gelu(x):
    t = jnp.tanh(_GELU_C * (x + _GELU_A * x * x * x))
    return 0.5 * x * (1.0 + t), t


def _gelu_grad(x, t):
    return 0.5 * (1.0 + t) + 0.5 * x * (1.0 - t * t) * _GELU_C * (1.0 + 3.0 * _GELU_A * x * x)


def _gmlp_mask():
    r = lax.broadcasted_iota(jnp.int32, (GB, GB), 0) // GCH
    c = lax.broadcasted_iota(jnp.int32, (GB, GB), 1) // GCH
    return c <= r


def _gmlp_fwd(proj, lnw, lnb, ws, bst):
    T = proj.shape[0]
    tm = min(512, T)
    nblk = tm // GB

    def body(u_ref, v_ref, lnw_ref, lnb_ref, ws_ref, bst_ref, o_ref):
        mask = _gmlp_mask()
        u, _ = _gelu(u_ref[...].astype(F32))
        v, _ = _gelu(v_ref[...].astype(F32))
        for g in range(GG):
            cs = slice(g * GD, (g + 1) * GD)
            vg = v[:, cs]
            mu = jnp.mean(vg, axis=-1, keepdims=True)
            vc = vg - mu
            var = jnp.mean(vc * vc, axis=-1, keepdims=True)
            vn = (vc * lax.rsqrt(var + EPS) * lnw_ref[g:g + 1, :] + lnb_ref[g:g + 1, :]).astype(BF16)
            wsg = jnp.where(mask, ws_ref[g], 0.0).astype(BF16)
            bcol = bst_ref[:, g:g + 1]
            for blk in range(nblk):
                rs = slice(blk * GB, (blk + 1) * GB)
                sv = _dot(wsg, vn[rs, :]) + bcol
                o_ref[rs, cs] = (u[rs, cs] * sv).astype(BF16)

    full = lambda shape: pl.BlockSpec(shape, lambda i: tuple(0 for _ in shape))
    return pl.pallas_call(
        body, grid=(T // tm,),
        in_specs=[pl.BlockSpec((tm, GW), lambda i: (i, 2)), pl.BlockSpec((tm, GW), lambda i: (i, 3)),
                  full((GG, GD)), full((GG, GD)), full((GG, GB, GB)), full((GB, GG))],
        out_specs=pl.BlockSpec((tm, GW), lambda i: (i, 0)),
        out_shape=jax.ShapeDtypeStruct((T, GW), BF16), name="gmlp_fwd",
        compiler_params=_params(("arbitrary",), 40))(proj, proj, lnw, lnb, ws, bst)


def _gmlp_bwd(proj, dya, dproj, lnw, lnb, ws, bst):
    T = proj.shape[0]
    tm = min(512, T)
    nblk = tm // GB

    def body(u_ref, v_ref, dya_ref, dproj_in, lnw_ref, lnb_ref, ws_ref, bst_ref,
             dz_ref, dlnw_ref, dlnb_ref, dws_ref, dbst_ref):
        del dproj_in
        first = pl.program_id(0) == 0

        @pl.when(first)
        def _():
            dlnw_ref[...] = jnp.zeros_like(dlnw_ref)
            dlnb_ref[...] = jnp.zeros_like(dlnb_ref)
            dws_ref[...] = jnp.zeros_like(dws_ref)
            dbst_ref[...] = jnp.zeros_like(dbst_ref)

        mask = _gmlp_mask()
        lane = lax.broadcasted_iota(jnp.int32, (GB, LANE), 1)
        ur = u_ref[...].astype(F32)
        vr = v_ref[...].astype(F32)
        u, tu = _gelu(ur)
        v, tv = _gelu(vr)
        gu = _gelu_grad(ur, tu)
        gv = _gelu_grad(vr, tv)
        dy = dya_ref[...].astype(F32)
        dbst = jnp.zeros((GB, LANE), F32)
        dlnw_rows, dlnb_rows = [], []
        for g in range(GG):
            cs = slice(g * GD, (g + 1) * GD)
            vg = v[:, cs]
            mu = jnp.mean(vg, axis=-1, keepdims=True)
            vc = vg - mu
            var = jnp.mean(vc * vc, axis=-1, keepdims=True)
            rstd = lax.rsqrt(var + EPS)
            xhat = vc * rstd
            lw = lnw_ref[g:g + 1, :]
            vn = (xhat * lw + lnb_ref[g:g + 1, :]).astype(BF16)
            wsg = jnp.where(mask, ws_ref[g], 0.0).astype(BF16)
            bcol = bst_ref[:, g:g + 1]
            dyg = dy[:, cs]
            ug = u[:, cs]
            dsv = dyg * ug
            dsv_b = dsv.astype(BF16)
            dws_g = jnp.zeros((GB, GB), F32)
            bsum = jnp.zeros((GB, 1), F32)
            dvn_parts = []
            for blk in range(nblk):
                rs = slice(blk * GB, (blk + 1) * GB)
                sv = _dot(wsg, vn[rs, :]) + bcol
                dz_ref[rs, cs] = (dyg[rs, :] * sv * gu[rs, cs]).astype(BF16)
                dws_g = dws_g + _dot(dsv_b[rs, :], vn[rs, :], NT)
                bsum = bsum + jnp.sum(dsv[rs, :], axis=-1, keepdims=True)
                dvn_parts.append(_dot(wsg, dsv_b[rs, :], TN))
            dvn = jnp.concatenate(dvn_parts, axis=0)
            dws_ref[g] += jnp.where(mask, dws_g, 0.0)
            dbst = dbst + jnp.where(lane == g, bsum, 0.0)
            dlnw_rows.append(jnp.sum(dvn * xhat, axis=0, keepdims=True))
            dlnb_rows.append(jnp.sum(dvn, axis=0, keepdims=True))
            dxh = dvn * lw
            dvg = rstd * (dxh - jnp.mean(dxh, axis=-1, keepdims=True)
                          - xhat * jnp.mean(dxh * xhat, axis=-1, keepdims=True))
            dz_ref[:, GW + g * GD:GW + (g + 1) * GD] = (dvg * gv[:, cs]).astype(BF16)
        dlnw_ref[...] += jnp.concatenate(dlnw_rows, axis=0)
        dlnb_ref[...] += jnp.concatenate(dlnb_rows, axis=0)
        dbst_ref[...] += dbst

    full = lambda shape: pl.BlockSpec(shape, lambda i: tuple(0 for _ in shape))
    outs = pl.pallas_call(
        body, grid=(T // tm,),
        in_specs=[pl.BlockSpec((tm, GW), lambda i: (i, 2)), pl.BlockSpec((tm, GW), lambda i: (i, 3)),
                  pl.BlockSpec((tm, GW), lambda i: (i, 0)), pl.BlockSpec(memory_space=pl.ANY),
                  full((GG, GD)), full((GG, GD)), full((GG, GB, GB)), full((GB, GG))],
        out_specs=[pl.BlockSpec((tm, 2 * GW), lambda i: (i, 1)), full((GG, GD)), full((GG, GD)),
                   full((GG, GB, GB)), full((GB, LANE))],
        out_shape=[jax.ShapeDtypeStruct(dproj.shape, dproj.dtype), jax.ShapeDtypeStruct((GG, GD), F32),
                   jax.ShapeDtypeStruct((GG, GD), F32), jax.ShapeDtypeStruct((GG, GB, GB), F32),
                   jax.ShapeDtypeStruct((GB, LANE), F32)],
        input_output_aliases={3: 0}, name="gmlp_bwd",
        compiler_params=_params(("arbitrary",), 48))(proj, proj, dya, dproj, lnw, lnb, ws, bst)
    return outs


def _merge_fwd(ya_pre, yb_pre, proj, bias, wpa, wpb):
    T = proj.shape[0]
    tm = min(512, T)

    def body(ya_ref, yb_ref, g_ref, b_ref, wpa_ref, wpb_ref, m_ref, oa_ref, ob_ref):
        ya = _dot(ya_ref[...], wpa_ref[...])
        yb = _dot(yb_ref[...], wpb_ref[...])
        g = g_ref[...].astype(F32)
        sa = _sigmoid(g[:, :D] + b_ref[0:1, :])
        sb = _sigmoid(g[:, D:] + b_ref[1:2, :])
        m_ref[...] = (sa * ya + sb * yb).astype(BF16)
        oa_ref[...] = ya.astype(BF16)
        ob_ref[...] = yb.astype(BF16)

    row = lambda w: pl.BlockSpec((tm, w), lambda i: (i, 0))
    full = lambda shape: pl.BlockSpec(shape, lambda i: tuple(0 for _ in shape))
    o = jax.ShapeDtypeStruct((T, D), BF16)
    return pl.pallas_call(
        body, grid=(T // tm,),
        in_specs=[row(GW), row(SI), row(2 * D), full((2, D)), full((GW, D)), full((SI, D))],
        out_specs=[row(D), row(D), row(D)], out_shape=[o, o, o], name="merge_fwd",
        compiler_params=_params(("arbitrary",), 40))(ya_pre, yb_pre, proj, bias, wpa, wpb)


def _merge_bwd(dm, proj, bias, ya, yb, wpa, wpb):
    T = proj.shape[0]
    tm = min(512, T)

    def body(dm_ref, g_ref, b_ref, ya_ref, yb_ref, wpa_ref, wpb_ref,
             dg_ref, dya_ref, dyb_ref, dpa_ref, dpb_ref, db_ref):
        dmv = dm_ref[...].astype(F32)
        g = g_ref[...].astype(F32)
        sa = _sigmoid(g[:, :D] + b_ref[0:1, :])
        sb = _sigmoid(g[:, D:] + b_ref[1:2, :])
        dya = (dmv * sa).astype(BF16)
        dyb = (dmv * sb).astype(BF16)
        dga = dmv * ya_ref[...].astype(F32) * sa * (1.0 - sa)
        dgb = dmv * yb_ref[...].astype(F32) * sb * (1.0 - sb)
        dg_ref[:, :D] = dga.astype(BF16)
        dg_ref[:, D:] = dgb.astype(BF16)
        dya_ref[...] = dya
        dyb_ref[...] = dyb
        dpa_ref[...] = _dot(dya, wpa_ref[...], NT).astype(BF16)
        dpb_ref[...] = _dot(dyb, wpb_ref[...], NT).astype(BF16)
        part = jnp.concatenate([jnp.sum(dga, axis=0, keepdims=True), jnp.sum(dgb, axis=0, keepdims=True)], axis=0)

        @pl.when(pl.program_id(0) == 0)
        def _():
            db_ref[...] = part

        @pl.when(pl.program_id(0) > 0)
        def _():
            db_ref[...] += part

    row = lambda w: pl.BlockSpec((tm, w), lambda i: (i, 0))
    full = lambda shape: pl.BlockSpec(shape, lambda i: tuple(0 for _ in shape))
    o = lambda w: jax.ShapeDtypeStruct((T, w), BF16)
    return pl.pallas_call(
        body, grid=(T // tm,),
        in_specs=[row(D), row(2 * D), full((2, D)), row(D), row(D), full((GW, D)), full((SI, D))],
        out_specs=[row(2 * D), row(D), row(D), row(GW), row(SI), full((2, D))],
        out_shape=[o(DP_COLS), o(D), o(D), o(GW), o(SI), jax.ShapeDtypeStruct((2, D), F32)], name="merge_bwd",
        compiler_params=_params(("arbitrary",), 48))(dm, proj, bias, ya, yb, wpa, wpb)


RB = 128


def _shift_matrix(j):
    r = lax.broadcasted_iota(jnp.int32, (RB, RB), 0)
    c = lax.broadcasted_iota(jnp.int32, (RB, RB), 1)
    return jnp.where(c == r - j, 1.0, 0.0).astype(BF16)


def _rows_down(xb, before, shifts):
    H = SUBLANE
    mats = [_shift_matrix(j) for j in shifts]
    outs = [[] for _ in shifts]
    for b in range(xb.shape[0] // RB):
        blk = xb[b * RB:(b + 1) * RB]
        edge = jnp.concatenate([before, blk[:2 * H].astype(F32)[:H]], axis=0)
        for i, j in enumerate(shifts):
            outs[i] += [edge[H - j:2 * H - j], _dot(mats[i], blk)[H:]]
        before = blk[RB - 2 * H:].astype(F32)[H:]
    return [jnp.concatenate(o, axis=0) for o in outs]


def _rows_up(xb, after, shifts):
    H = SUBLANE
    nb = xb.shape[0] // RB
    mats = [_shift_matrix(-j) for j in shifts]
    outs = [[] for _ in shifts]
    for b in range(nb):
        blk = xb[b * RB:(b + 1) * RB]
        nxt = xb[(b + 1) * RB:(b + 1) * RB + 2 * H].astype(F32)[:H] if b + 1 < nb else after
        edge = jnp.concatenate([blk[RB - 2 * H:].astype(F32)[H:], nxt], axis=0)
        for i, j in enumerate(shifts):
            outs[i] += [_dot(mats[i], blk)[:RB - H], edge[j:H + j]]
    return [jnp.concatenate(o, axis=0) for o in outs]


def _ffn_act_fwd(up, cw, cb):
    T = up.shape[0]
    tm = min(512, T)
    H = SUBLANE

    def body(up_ref, cw_ref, cb_ref, o_ref, xc_ref, halo):
        @pl.when(pl.program_id(1) == 0)
        def _():
            halo[...] = jnp.zeros_like(halo)

        xb = up_ref[...]
        x2, x1 = _rows_down(xb, halo[...], (2, 1))
        xc = cb_ref[...] + cw_ref[0:1, :] * x2 + cw_ref[1:2, :] * x1 + cw_ref[2:3, :] * xb.astype(F32)
        xc_ref[...] = xc.astype(BF16)
        gate = xc[:, :FT]
        o_ref[...] = (gate * _sigmoid(gate) * xc[:, FT:]).astype(BF16)
        halo[...] = xb[tm - 2 * H:].astype(F32)[H:]

    tile = pl.BlockSpec((tm, 2 * FT), lambda j, i: (i, j))
    return pl.pallas_call(
        body, grid=(2, T // tm),
        in_specs=[tile, pl.BlockSpec((FK, 2 * FT), lambda j, i: (0, j)), pl.BlockSpec((1, 2 * FT), lambda j, i: (0, j))],
        out_specs=[pl.BlockSpec((tm, FT), lambda j, i: (i, j)), tile],
        out_shape=[jax.ShapeDtypeStruct((T, DFF), BF16), jax.ShapeDtypeStruct((T, 2 * DFF), BF16)],
        scratch_shapes=[pltpu.VMEM((H, 2 * FT), F32)], name="ffn_act_fwd",
        compiler_params=_params(("arbitrary", "arbitrary"), 48))(up, cw, cb)


def _ffn_act_bwd(up, xc, dact, cw):
    T = up.shape[0]
    tm = min(512, T)
    nt = T // tm
    H = SUBLANE

    def body(up_ref, xc_ref, da_ref, cw_ref, dup_ref, dcw_ref, dcb_ref, ahead):
        @pl.when(pl.program_id(1) == 0)
        def _():
            ahead[...] = jnp.zeros_like(ahead)
            dcw_ref[...] = jnp.zeros_like(dcw_ref)
            dcb_ref[...] = jnp.zeros_like(dcb_ref)

        xcv = xc_ref[...].astype(F32)
        gate, val = xcv[:, :FT], xcv[:, FT:]
        sg = _sigmoid(gate)
        dav = da_ref[...].astype(F32)
        dgate = dav * val * sg * (1.0 + gate * (1.0 - sg))
        dval = dav * gate * sg
        dxc = jnp.concatenate([dgate, dval], axis=1)
        d1, d2 = _rows_up(dxc.astype(BF16), ahead[...], (1, 2))
        x = up_ref[...].astype(F32)
        dcb_ref[...] += jnp.sum(dxc, axis=0, keepdims=True)
        dcw_ref[...] += jnp.concatenate([jnp.sum(d * x, axis=0, keepdims=True) for d in (d2, d1, dxc)], axis=0)
        dup_ref[...] = (cw_ref[2:3, :] * dxc + cw_ref[1:2, :] * d1 + cw_ref[0:1, :] * d2).astype(BF16)
        ahead[...] = dxc[0:H, :]

    tile = pl.BlockSpec((tm, 2 * FT), lambda j, i: (nt - 1 - i, j))
    return pl.pallas_call(
        body, grid=(2, nt),
        in_specs=[tile, tile, pl.BlockSpec((tm, FT), lambda j, i: (nt - 1 - i, j)),
                  pl.BlockSpec((FK, 2 * FT), lambda j, i: (0, j))],
        out_specs=[tile, pl.BlockSpec((FK, 2 * FT), lambda j, i: (0, j)), pl.BlockSpec((1, 2 * FT), lambda j, i: (0, j))],
        out_shape=[jax.ShapeDtypeStruct((T, 2 * DFF), BF16), jax.ShapeDtypeStruct((FK, 2 * DFF), F32),
                   jax.ShapeDtypeStruct((1, 2 * DFF), F32)],
        scratch_shapes=[pltpu.VMEM((H, 2 * FT), F32)], name="ffn_act_bwd",
        compiler_params=_params(("arbitrary", "arbitrary"), 56))(up, xc, dact, cw)


def _softplus(x):
    e = jnp.exp(-jnp.abs(x))
    return jnp.maximum(x, 0.0) + jnp.where(e < 1e-4, e * (1.0 - 0.5 * e), jnp.log(1.0 + e))


def _ssd_consts():
    li = lax.broadcasted_iota(jnp.int32, (LS, LS), 0)
    si = lax.broadcasted_iota(jnp.int32, (LS, LS), 1)
    tril = si <= li
    hh = lax.broadcasted_iota(jnp.int32, (LANE, SI), 0)
    cc = lax.broadcasted_iota(jnp.int32, (LANE, SI), 1) // SP
    expand = jnp.where(hh == cc, 1.0, 0.0).astype(BF16)
    return tril, expand


def _ssd_pre(xc, dt_ref, dtb_ref, alog_ref, tril, expand):
    sx = _sigmoid(xc)
    xbc = xc * sx
    xs, bm, cm = xbc[:, :SI], xbc[:, SI:SI + SG * SN], xbc[:, SI + SG * SN:]
    dtin = dt_ref[...] + dtb_ref[...]
    dt = _softplus(dtin)
    a_neg = -jnp.exp(alog_ref[...])
    dta = dt * a_neg
    trilb = jnp.where(tril, 1.0, 0.0).astype(BF16)
    a = _dot3_rhs(trilb, dta, NN)
    a_exp = _dot3(a, expand, NN)
    dt_exp = _dot3(dt, expand, NN)
    xdt = xs * dt_exp
    a_last = a_exp[LS - 1:LS, :]
    return dict(xc=xc, sx=sx, xs=xs, bm=bm, cm=cm, dtin=dt_ref[...] + dtb_ref[...], dt=dt, a_neg=a_neg,
                a=a, a_t=a.T, a_exp=a_exp, dt_exp=dt_exp, xdt=xdt, ea=jnp.exp(a_exp),
                w=jnp.exp(a_last - a_exp), eal=jnp.exp(a_last))


def _head_decay(pre, tril, h):
    seg = pre["a"][:, h:h + 1] - pre["a_t"][h:h + 1, :]
    return jnp.exp(jnp.where(tril, seg, -1e30))


def _ssd_fwd(proj, dtraw, cw, cb, dtb, alog, dexp, nw):
    T = proj.shape[0]
    nc = T // LS
    H = SUBLANE

    def body(z_ref, x_ref, dt_ref, cw_ref, cb_ref, dtb_ref, alog_ref, dexp_ref, nw_ref,
             yb_ref, y_ref, sp_ref, xc_ref, halo, st):
        @pl.when(pl.program_id(0) == 0)
        def _():
            halo[...] = jnp.zeros_like(halo)
            st[...] = jnp.zeros_like(st)

        xb = x_ref[...]
        taps = _rows_down(xb, halo[...], (3, 2, 1)) + [xb.astype(F32)]
        xc = cb_ref[...]
        for k in range(SK):
            xc = xc + cw_ref[k:k + 1, :] * taps[k]
        xc_ref[...] = xc.astype(BF16)
        tril, expand = _ssd_consts()
        pre = _ssd_pre(xc, dt_ref, dtb_ref, alog_ref, tril, expand)
        lane = lax.broadcasted_iota(jnp.int32, (LS, LANE), 1)
        lo = lane < SP
        zf = z_ref[...].astype(F32)
        siluz = zf * _sigmoid(zf)
        for g in range(SG):
            gs = slice(g * SGW, (g + 1) * SGW)
            bg = pre["bm"][:, g * SN:(g + 1) * SN].astype(BF16)
            cg = pre["cm"][:, g * SN:(g + 1) * SN].astype(BF16)
            gmat = _dot(cg, bg, NT)
            sg = st[g]
            sp_ref[0, g] = sg
            yoff = _dot(cg, sg.astype(BF16))
            parts = []
            for j in range(SGW // LANE):
                h0 = g * (SGW // SP) + 2 * j
                m0 = gmat * _head_decay(pre, tril, h0)
                m1 = gmat * _head_decay(pre, tril, h0 + 1)
                xp = pre["xdt"][:, g * SGW + j * LANE:g * SGW + (j + 1) * LANE]
                rhs = jnp.concatenate([jnp.where(lo, xp, 0.0), jnp.where(lo, 0.0, xp)], axis=0).astype(BF16)
                parts.append(_dot(jnp.concatenate([m0, m1], axis=1).astype(BF16), rhs))
            y = (jnp.concatenate(parts, axis=1) + pre["ea"][:, gs] * yoff + dexp_ref[:, gs] * pre["xs"][:, gs])
            st[g] = pre["eal"][:, gs] * sg + _dot(bg, (pre["w"][:, gs] * pre["xdt"][:, gs]).astype(BF16), TN)
            y_ref[:, gs] = y
            yg = y * siluz[:, gs]
            r = lax.rsqrt(jnp.mean(yg * yg, axis=-1, keepdims=True) + EPS)
            yb_ref[:, gs] = (yg * r * nw_ref[:, gs]).astype(BF16)
        halo[...] = xb[LS - 2 * H:].astype(F32)[H:]

    vec = lambda w: pl.BlockSpec((1, w), lambda c: (0, 0))
    return pl.pallas_call(
        body, grid=(nc,),
        in_specs=[pl.BlockSpec((LS, SI), lambda c: (c, 2)), pl.BlockSpec((LS, SXBC), lambda c: (c, 2)),
                  pl.BlockSpec((LS, LANE), lambda c: (c, 0)),
                  pl.BlockSpec((SK, SXBC), lambda c: (0, 0)), vec(SXBC), vec(LANE), vec(LANE), vec(SI), vec(SI)],
        out_specs=[pl.BlockSpec((LS, SI), lambda c: (c, 0)), pl.BlockSpec((LS, SI), lambda c: (c, 0)),
                   pl.BlockSpec((1, SG, SN, SGW), lambda c: (c, 0, 0, 0)), pl.BlockSpec((LS, SXBC), lambda c: (c, 0))],
        out_shape=[jax.ShapeDtypeStruct((T, SI), BF16), jax.ShapeDtypeStruct((T, SI), F32),
                   jax.ShapeDtypeStruct((nc, SG, SN, SGW), F32), jax.ShapeDtypeStruct((T, SXBC), BF16)],
        scratch_shapes=[pltpu.VMEM((H, SXBC), F32), pltpu.VMEM((SG, SN, SGW), F32)], name="ssd_fwd",
        compiler_params=_params(("arbitrary",), VMEM_CAP_MB))(proj, proj, dtraw, cw, cb, dtb, alog, dexp, nw)


def _ssd_bwd(proj, xcs, dtraw, y, sprev, dyb, dproj, cw, dtb, alog, dexp, nw):
    T = proj.shape[0]
    nc = T // LS
    H = SUBLANE
    NJ = 1

    def body(z_ref, x_ref, xc_ref, dt_ref, y_ref, sp_ref, dyb_ref, dproj_in,
             cw_ref, dtb_ref, alog_ref, dexp_ref, nw_ref,
             dp_ref, ddt_ref, dcw_ref, dcb_ref, ddtb_ref, da_ref, dd_ref, dnw_ref,
             ahead, ds, stage):
        del dproj_in
        i = pl.program_id(0)
        j = pl.program_id(1)

        @pl.when(jnp.logical_and(i == 0, j == 0))
        def _():
            ahead[...] = jnp.zeros_like(ahead)
            ds[...] = jnp.zeros_like(ds)
            for r in (dcw_ref, dcb_ref, ddtb_ref, da_ref, dd_ref, dnw_ref):
                r[...] = jnp.zeros_like(r)

        @pl.when(j == 0)
        def _():
            tril, expand = _ssd_consts()
            pre = _ssd_pre(xc_ref[...].astype(F32), dt_ref, dtb_ref, alog_ref, tril, expand)
            lane = lax.broadcasted_iota(jnp.int32, (LS, LANE), 1)
            sub = lax.broadcasted_iota(jnp.int32, (LANE, LS), 0)
            rowi = lax.broadcasted_iota(jnp.int32, (LS, 1), 0)
            lo = lane < SP
            xs, xdt, ea, w, eal = pre["xs"], pre["xdt"], pre["ea"], pre["w"], pre["eal"]

            zf = z_ref[...].astype(F32)
            sz = _sigmoid(zf)
            siluz = zf * sz
            yv = y_ref[...]
            yg = yv * siluz
            dout = dyb_ref[...].astype(F32)
            dyg_parts, dnw_parts = [], []
            for g in range(SG):
                gs = slice(g * SGW, (g + 1) * SGW)
                ygg = yg[:, gs]
                r = lax.rsqrt(jnp.mean(ygg * ygg, axis=-1, keepdims=True) + EPS)
                yhat = ygg * r
                dn = dout[:, gs] * nw_ref[:, gs]
                dnw_parts.append(jnp.sum(dout[:, gs] * yhat, axis=0, keepdims=True))
                dyg_parts.append(r * (dn - yhat * jnp.mean(dn * yhat, axis=-1, keepdims=True)))
            dyg = jnp.concatenate(dyg_parts, axis=1)
            dnw_ref[...] += jnp.concatenate(dnw_parts, axis=1)
            dy = dyg * siluz
            stage[:, 0:SI] = (dyg * yv * sz * (1.0 + zf * (1.0 - sz))).astype(BF16)
            dd_ref[...] += jnp.sum(dy * xs, axis=0, keepdims=True)
            tt = ea * dy

            da_rows = jnp.zeros((LS, LANE), F32)
            da_cols = jnp.zeros((LANE, LS), F32)
            dxdt_parts, db_parts, dc_parts, daexp_parts = [], [], [], []
            for g in range(SG):
                gs = slice(g * SGW, (g + 1) * SGW)
                bg = pre["bm"][:, g * SN:(g + 1) * SN].astype(BF16)
                cg = pre["cm"][:, g * SN:(g + 1) * SN].astype(BF16)
                sg = sp_ref[0, g]
                sgb = sg.astype(BF16)
                dsg = ds[g]
                dsgb = dsg.astype(BF16)
                ttg = tt[:, gs].astype(BF16)
                yoff = _dot(cg, sgb)
                dc = _dot(ttg, sgb, NT)
                gmat = _dot(cg, bg, NT)
                dgm = jnp.zeros((LS, LS), F32)
                dxdt_pairs = []
                for jj in range(SGW // LANE):
                    h0 = g * (SGW // SP) + 2 * jj
                    ps = slice(g * SGW + jj * LANE, g * SGW + (jj + 1) * LANE)
                    l0 = _head_decay(pre, tril, h0)
                    l1 = _head_decay(pre, tril, h0 + 1)
                    m0 = gmat * l0
                    m1 = gmat * l1
                    dyp = dy[:, ps]
                    dy_lo = jnp.where(lo, dyp, 0.0).astype(BF16)
                    dy_hi = jnp.where(lo, 0.0, dyp).astype(BF16)
                    xpb = xdt[:, ps].astype(BF16)
                    dm0 = _dot(dy_lo, xpb, NT)
                    dm1 = _dot(dy_hi, xpb, NT)
                    q0 = dm0 * m0
                    q1 = dm1 * m1
                    da_rows = da_rows + jnp.where(lane == h0, jnp.sum(q0, axis=1, keepdims=True), 0.0)
                    da_rows = da_rows + jnp.where(lane == h0 + 1, jnp.sum(q1, axis=1, keepdims=True), 0.0)
                    da_cols = da_cols + jnp.where(sub == h0, jnp.sum(q0, axis=0, keepdims=True), 0.0)
                    da_cols = da_cols + jnp.where(sub == h0 + 1, jnp.sum(q1, axis=0, keepdims=True), 0.0)
                    dgm = dgm + dm0 * l0 + dm1 * l1
                    mcat = jnp.concatenate([m0, m1], axis=0).astype(BF16)
                    dycat = jnp.concatenate([dy_lo, dy_hi], axis=0)
                    dxdt_pairs.append(_dot(mcat, dycat, TN))
                dgb = dgm.astype(BF16)
                dc = dc + _dot(dgb, bg)
                db = _dot(dgb, cg, TN)
                zg = _dot(bg, dsgb)
                wg, xdtg = w[:, gs], xdt[:, gs]
                dxdt_g = jnp.concatenate(dxdt_pairs, axis=1) + wg * zg
                qg = zg * xdtg * wg
                last = (jnp.sum(qg, axis=0, keepdims=True)
                        + jnp.sum(dsg * sg, axis=0, keepdims=True) * eal[:, gs])
                daexp_parts.append(dy[:, gs] * ea[:, gs] * yoff - qg + jnp.where(rowi == LS - 1, last, 0.0))
                db = db + _dot((wg * xdtg).astype(BF16), dsgb, NT)
                ds[g] = eal[:, gs] * dsg + _dot(cg, ttg, TN)
                dxdt_parts.append(dxdt_g)
                db_parts.append(db)
                dc_parts.append(dc)
            dxdt = jnp.concatenate(dxdt_parts, axis=1)
            da_exp = jnp.concatenate(daexp_parts, axis=1)
            da = _dot3(da_exp, expand, NT) + da_rows - da_cols.T
            triub = jnp.where(tril, 1.0, 0.0).astype(BF16)
            ddta = _dot3_rhs(triub, da, TN)
            ddt = ddta * pre["a_neg"] + _dot3(dxdt * xs, expand, NT)
            da_ref[...] += jnp.sum(ddta * pre["dt"], axis=0, keepdims=True)
            ddt_raw = ddt * _sigmoid(pre["dtin"])
            ddt_ref[...] = ddt_raw
            ddtb_ref[...] += jnp.sum(ddt_raw, axis=0, keepdims=True)
            dxs = dexp_ref[...] * dy + dxdt * pre["dt_exp"]
            dxbc = jnp.concatenate([dxs] + db_parts + dc_parts, axis=1)
            sx, xc = pre["sx"], pre["xc"]
            dxc = dxbc * sx * (1.0 + xc * (1.0 - sx))
            taps = _rows_up(dxc.astype(BF16), ahead[...], (3, 2, 1)) + [dxc]
            xr = x_ref[...].astype(F32)
            dcb_ref[...] += jnp.sum(dxc, axis=0, keepdims=True)
            dcw_ref[...] += jnp.concatenate([jnp.sum(t * xr, axis=0, keepdims=True) for t in taps], axis=0)
            dxr = cw_ref[0:1, :] * taps[0]
            for k in range(1, SK):
                dxr = dxr + cw_ref[k:k + 1, :] * taps[k]
            stage[:, SI:] = dxr.astype(BF16)
            ahead[...] = dxc[0:H, :]

        dp_ref[...] = stage[...]

    vec = lambda w: pl.BlockSpec((1, w), lambda i, j: (0, 0))
    rev = lambda w, cb_: pl.BlockSpec((LS, w), lambda i, j: (nc - 1 - i, cb_))
    outs = pl.pallas_call(
        body, grid=(nc, NJ),
        in_specs=[rev(SI, 2), rev(SXBC, 2), rev(SXBC, 0),
                  rev(LANE, 0), rev(SI, 0),
                  pl.BlockSpec((1, SG, SN, SGW), lambda i, j: (nc - 1 - i, 0, 0, 0)),
                  rev(SI, 0), pl.BlockSpec(memory_space=pl.ANY),
                  pl.BlockSpec((SK, SXBC), lambda i, j: (0, 0)), vec(LANE), vec(LANE), vec(SI), vec(SI)],
        out_specs=[rev(DP_SSM, 1), rev(LANE, 0),
                   pl.BlockSpec((SK, SXBC), lambda i, j: (0, 0)), vec(SXBC), vec(LANE), vec(LANE), vec(SI), vec(SI)],
        out_shape=[jax.ShapeDtypeStruct(dproj.shape, dproj.dtype), jax.ShapeDtypeStruct((T, LANE), F32),
                   jax.ShapeDtypeStruct((SK, SXBC), F32), jax.ShapeDtypeStruct((1, SXBC), F32),
                   jax.ShapeDtypeStruct((1, LANE), F32), jax.ShapeDtypeStruct((1, LANE), F32),
                   jax.ShapeDtypeStruct((1, SI), F32), jax.ShapeDtypeStruct((1, SI), F32)],
        scratch_shapes=[pltpu.VMEM((H, SXBC), F32),
                        pltpu.VMEM((SG, SN, SGW), F32), pltpu.VMEM((LS, SI + SXBC), BF16)],
        input_output_aliases={7: 0}, name="ssd_bwd",
        compiler_params=_params(("arbitrary", "arbitrary"), VMEM_CAP_MB))(
            proj, proj, xcs, dtraw, y, sprev, dyb, dproj, cw, dtb, alog, dexp, nw)
    return outs


def _perm_ffn_cols(a):
    lead = a.shape[:-1]
    return a.reshape(lead + (2, 2, FT)).swapaxes(-3, -2).reshape(lead + (2 * DFF,))


def _perm_ffn_rows(a):
    return a.reshape((2, 2, FT) + a.shape[1:]).swapaxes(0, 1).reshape(a.shape)


def _pad_lanes(v, n=LANE):
    return jnp.pad(v, ((0, 0), (0, n - v.shape[-1])))


LATE = ["w_proj_a", "w_proj_b", "w_out", "ffn_w_up_t", "ffn_w_down"]


class _NoExchange:
    def gather_start(self):
        return None

    def gather_pass_on(self, outs):
        return None

    def late_weights(self, w, outs):
        return w

    def reduce_late(self, grads):
        return None

    def reduce_w_in(self, grad):
        return None

    def reduced(self, late_outs, w_in_outs):
        pass


def _local_step(x, tgt, w, hooks=None):
    hooks = hooks or _NoExchange()

    def mm(*args, side=None, **kw):
        out = _matmul(*args, side=side, **kw)
        return out if side is not None else (out, [])

    win_t = w["w_in_t"]
    win_dt = jnp.pad(w["w_in_t"][PMAIN:], ((0, LANE - SH), (0, 0)))
    fcw = _perm_ffn_cols(w["ffn_conv_w"])
    fcb = _perm_ffn_cols(w["ffn_conv_b"][None, :])
    mixw = w["mix_norm_w"][None, :]
    ffnw = w["ffn_norm_w"][None, :]
    finw = w["final_norm_w"][None, :]
    bst = w["gmlp_bs"].T
    scb = w["ssm_conv_b"][None, :]
    dtb = _pad_lanes(w["ssm_dt_bias"][None, :])
    alog = _pad_lanes(w["ssm_a_log"][None, :])
    dexp = jnp.repeat(w["ssm_d"], SP)[None, :]
    snw = w["ssm_norm_w"][None, :]

    xn = _rms_fwd(x, mixw, name="mix_norm")
    proj, got = mm(xn, win_t, name="in_proj", out_dtype=BF16, tb=True, tn=1536, j_outer=True, b_rows=PMAIN,
                   side=hooks.gather_start())
    dtraw, got = mm(xn, win_dt, name="in_proj_dt", out_dtype=F32, tb=True, side=hooks.gather_pass_on(got))
    w = hooks.late_weights(w, got)
    wup = _perm_ffn_rows(w["ffn_w_up_t"])
    ya_pre = _gmlp_fwd(proj, w["gmlp_ln_w"], w["gmlp_ln_b"], w["gmlp_ws"], bst)
    yb_pre, y_ssd, sprev, ssm_xc = _ssd_fwd(proj, dtraw, w["ssm_conv_w"], scb, dtb, alog, dexp, snw)
    merged, ya, yb = _merge_fwd(ya_pre, yb_pre, proj, w["gate_bias"], w["w_proj_a"], w["w_proj_b"])
    h1 = _matmul(merged, w["w_out"], name="out_proj", out_dtype=F32, add=x)
    hn = _rms_fwd(h1, ffnw, name="ffn_norm")
    up = _matmul(hn, wup, name="ffn_up", out_dtype=BF16, tb=True, tn=FT, j_outer=True)
    act, ffn_xc = _ffn_act_fwd(up, fcw, fcb)
    h2 = _matmul(act, w["ffn_w_down"], name="ffn_down", out_dtype=F32, tk=FT, add=h1)

    loss_row, dh2, d_finw = _loss_head(h2, tgt, finw)
    dact = _matmul(dh2, w["ffn_w_down"], name="ffn_down_dx", out_dtype=BF16, tb=True, tn=FT)
    d_wdown = _matmul(act, dh2, name="ffn_down_dw", out_dtype=F32, ta=True, tm=FT)
    dup, d_fcw, d_fcb = _ffn_act_bwd(up, ffn_xc, dact, fcw)
    dhn = _matmul(dup, wup, name="ffn_up_dx", out_dtype=F32, tk=FT)
    d_wup = _matmul(dup, hn, name="ffn_up_dw", out_dtype=F32, ta=True, tm=FT)
    dh1, d_ffnw = _rms_bwd(h1, ffnw, dhn, dh2, name="ffn_norm_bwd")
    dmerged = _matmul(dh1, w["w_out"], name="out_proj_dx", out_dtype=BF16, tb=True)
    d_wout = _matmul(merged, dh1, name="out_proj_dw", out_dtype=F32, ta=True)
    dproj, dya, dyb, dya_pre, dyb_pre, d_gbias = _merge_bwd(dmerged, proj, w["gate_bias"], ya, yb,
                                                           w["w_proj_a"], w["w_proj_b"])
    d_wpa = _matmul(ya_pre, dya, name="proj_a_dw", out_dtype=F32, ta=True)
    d_wpb = _matmul(yb_pre, dyb, name="proj_b_dw", out_dtype=F32, ta=True)
    dproj, d_lnw, d_lnb, d_ws, d_bst = _gmlp_bwd(proj, dya_pre, dproj, w["gmlp_ln_w"], w["gmlp_ln_b"],
                                                 w["gmlp_ws"], bst)
    dproj, ddt, d_scw, d_scb, d_dtb, d_a, d_dch, d_snw = _ssd_bwd(
        proj, ssm_xc, dtraw, y_ssd, sprev, dyb_pre, dproj, w["ssm_conv_w"], dtb, alog, dexp, snw)
    late = {"w_proj_a": d_wpa, "w_proj_b": d_wpb, "w_out": d_wout, "ffn_w_up_t": _perm_ffn_rows(d_wup),
            "ffn_w_down": d_wdown}
    gap = (2 * D + 2 * GW, DP_GAP)
    d_win_main, late_outs = mm(dproj, xn, name="in_proj_dw", out_dtype=F32, ta=True, a_gap=gap,
                               side=hooks.reduce_late(late))
    d_win_dt = _matmul(ddt, xn, name="in_proj_dt_dw", out_dtype=F32, ta=True)
    d_win_t = jnp.concatenate([d_win_main, d_win_dt[:SH]], axis=0)
    dxn = _matmul(ddt, win_dt, name="in_proj_dt_dx", out_dtype=F32)
    dxn, w_in_outs = mm(dproj, win_t, name="in_proj_dx", out_dtype=F32, add=dxn, b_rows=PMAIN, a_gap=gap,
                        side=hooks.reduce_w_in(d_win_t))
    hooks.reduced(late_outs, w_in_outs)
    grad_x, d_mixw = _rms_bwd(x, mixw, dxn, dh1, name="mix_norm_bwd")

    a_neg = -jnp.exp(w["ssm_a_log"])
    grads = {
        "mix_norm_w": d_mixw[0],
        "w_in_t": d_win_t,
        "gate_bias": d_gbias,
        "gmlp_ln_w": d_lnw, "gmlp_ln_b": d_lnb, "gmlp_ws": d_ws, "gmlp_bs": d_bst[:, :GG].T,
        "ssm_conv_w": d_scw, "ssm_conv_b": d_scb[0],
        "ssm_dt_bias": d_dtb[0, :SH], "ssm_a_log": d_a[0, :SH] * a_neg,
        "ssm_d": d_dch.reshape(SH, SP).sum(axis=-1), "ssm_norm_w": d_snw[0],
        **late,
        "ffn_norm_w": d_ffnw[0],
        "ffn_conv_w": _perm_ffn_cols(d_fcw), "ffn_conv_b": _perm_ffn_cols(d_fcb)[0],
        "ffn_w_down": d_wdown, "final_norm_w": d_finw[0],
    }
    return loss_row, grad_x, grads


MESH = pl.DeviceIdType.MESH
HBM_SPEC = pl.BlockSpec(memory_space=pltpu.HBM)


def _axes():
    return lax.axis_index("x"), lax.axis_index("y"), lax.axis_index("c")


def _all_gather(shards, *, name):
    na = len(shards)

    def body(*refs):
        x_refs, out_refs = refs[:na], refs[na:2 * na]
        send_sems, recv_sems, local_sems = refs[2 * na:]
        x, y, c = _axes()
        me, sibling = (x, y, c), (x, y, 1 - c)
        chips = [(1 - x, y), (x, 1 - y), (1 - x, 1 - y)]

        def slot(a, px, py, pc):
            return out_refs[a].at[4 * px + 2 * py + pc]

        def copy(a, k, block, to, src=None):
            return pltpu.make_async_remote_copy(
                src_ref=slot(a, *block) if src is None else src, dst_ref=slot(a, *block),
                send_sem=send_sems.at[7 * a + k], recv_sem=recv_sems.at[7 * a + k], device_id=to, device_id_type=MESH)

        mine = [pltpu.make_async_copy(x_refs[a], slot(a, *me), local_sems.at[a]) for a in range(na)]
        for cp in mine:
            cp.start()
        first = []
        for a in range(na):
            first.append(copy(a, 0, me, sibling, src=x_refs[a]))
            first += [copy(a, 1 + j, me, (*chip, c), src=x_refs[a]) for j, chip in enumerate(chips)]
        for cp in first:
            cp.start()
        passed = []
        for j, chip in enumerate(chips):
            for a in range(na):
                copy(a, 1 + j, (*chip, c), me).wait_recv()
                cp = copy(a, 4 + j, (*chip, c), sibling)
                cp.start()
                passed.append(cp)
        for a in range(na):
            copy(a, 0, sibling, me).wait_recv()
        for j, chip in enumerate(chips):
            for a in range(na):
                copy(a, 4 + j, (*chip, 1 - c), me).wait_recv()
        for cp in first + passed:
            cp.wait_send()
        for cp in mine:
            cp.wait()

    return pl.pallas_call(
        body, out_shape=[jax.ShapeDtypeStruct((NDEV,) + s.shape, s.dtype) for s in shards],
        in_specs=[HBM_SPEC] * na, out_specs=[HBM_SPEC] * na,
        scratch_shapes=[pltpu.SemaphoreType.DMA((7 * na,)), pltpu.SemaphoreType.DMA((7 * na,)),
                        pltpu.SemaphoreType.DMA((na,))],
        name=name)(*shards)


def _exchange(srcs, plan, *, name):
    na = len(srcs)
    n = len(plan(0, 0, 0))

    def body(*refs):
        src_refs, out_refs = refs[:na], refs[na:2 * na]
        send_sems, recv_sems = refs[2 * na:]
        x, y, c = _axes()
        copies = []
        for k, (slab, peer) in enumerate(plan(x, y, c)):
            for a in range(na):
                cp = pltpu.make_async_remote_copy(
                    src_ref=src_refs[a].at[slab], dst_ref=out_refs[a].at[k], send_sem=send_sems.at[n * a + k],
                    recv_sem=recv_sems.at[n * a + k], device_id=peer, device_id_type=MESH)
                cp.start()
                copies.append(cp)
        for cp in copies:
            cp.wait()

    return pl.pallas_call(
        body, out_shape=[jax.ShapeDtypeStruct((n,) + s.shape[1:], s.dtype) for s in srcs],
        in_specs=[HBM_SPEC] * na, out_specs=[HBM_SPEC] * na,
        scratch_shapes=[pltpu.SemaphoreType.DMA((n * na,)), pltpu.SemaphoreType.DMA((n * na,))], name=name)(*srcs)


def _to_sibling_plan(x, y, c):
    return [(2 * q + (1 - c), (x, y, 1 - c)) for q in range(4)]


def _to_chips_plan(x, y, c):
    q = 2 * x + y
    return [(q ^ 2, (1 - x, y, c)), (q ^ 1, (x, 1 - y, c)), (q ^ 3, (1 - x, 1 - y, c))]


def _row_tile(rows, row_bytes, budget=2 * 2 ** 20, align=2 * SUBLANE):
    if rows * row_bytes <= 2 * budget:
        return rows
    best = None
    for d in range(align, rows + 1, align):
        if rows % d == 0 and d * row_bytes <= budget:
            best = d
    return best or rows


def _pair_add(g, ra, c_idx, *, name):
    _, _, R, C = g.shape
    tr = _row_tile(R, C * 4, budget=3 * 2 ** 20)

    def body(c_ref, g_ref, ra_ref, o_ref):
        del c_ref
        o_ref[...] = (g_ref[0].astype(F32) + ra_ref[...].astype(F32)).astype(o_ref.dtype)

    return pl.pallas_call(
        body,
        grid_spec=pltpu.PrefetchScalarGridSpec(
            num_scalar_prefetch=1, grid=(4, R // tr),
            in_specs=[pl.BlockSpec((1, 1, tr, C), lambda q, r, cr: (q, cr[0], r, 0)),
                      pl.BlockSpec((1, tr, C), lambda q, r, cr: (q, r, 0))],
            out_specs=pl.BlockSpec((1, tr, C), lambda q, r, cr: (q, r, 0))),
        out_shape=jax.ShapeDtypeStruct((4, R, C), g.dtype), name=name,
        compiler_params=_params(("arbitrary", "arbitrary"), 24))(c_idx, g, ra)


def _grad_sum(p, rb, q_idx, *, name):
    _, R, C = p.shape
    tr = _row_tile(R, C * 4, budget=3 * 2 ** 20)

    def body(q_ref, p_ref, rb_ref, o_ref):
        del q_ref
        g = p_ref[0].astype(F32)
        for k in range(3):
            g = g + rb_ref[k].astype(F32)
        o_ref[...] = g

    return pl.pallas_call(
        body,
        grid_spec=pltpu.PrefetchScalarGridSpec(
            num_scalar_prefetch=1, grid=(R // tr,),
            in_specs=[pl.BlockSpec((1, tr, C), lambda r, qr: (qr[0], r, 0)),
                      pl.BlockSpec((3, tr, C), lambda r, qr: (0, r, 0))],
            out_specs=pl.BlockSpec((tr, C), lambda r, qr: (r, 0))),
        out_shape=jax.ShapeDtypeStruct((R, C), F32), name=name,
        compiler_params=_params(("arbitrary",), 40))(q_idx, p, rb)


def _adamw(g, w, m, v):
    m = ADAM_B1 * m + (1.0 - ADAM_B1) * g
    v = ADAM_B2 * v + (1.0 - ADAM_B2) * (g * g)
    m_hat = m / (1.0 - ADAM_B1 ** ADAM_STEP)
    v_hat = v / (1.0 - ADAM_B2 ** ADAM_STEP)
    delta = -ADAM_LR * (m_hat / (jnp.sqrt(v_hat) + ADAM_EPS) + ADAM_WD * w)
    return delta, m, v


def _adam(g, w, m, v, *, name):
    _, R, C = w.shape
    tr = _row_tile(R, C * 4, budget=2 ** 20, align=SUBLANE)

    def body(g_ref, w_ref, m_ref, v_ref, d_out, m_out, v_out):
        delta, mn, vn = _adamw(g_ref[...], w_ref[...], m_ref[...], v_ref[...])
        d_out[...] = delta
        m_out[...] = mn
        v_out[...] = vn

    row = pl.BlockSpec((1, tr, C), lambda r: (0, r, 0))
    o = jax.ShapeDtypeStruct((1, R, C), F32)
    return pl.pallas_call(
        body, grid=(R // tr,), in_specs=[row, row, row, row], out_specs=[row, row, row], out_shape=[o, o, o],
        name=name, compiler_params=_params(("arbitrary",), 32))(g, w, m, v)


def _vmem_specs(n):
    return [pl.BlockSpec(memory_space=pltpu.VMEM)] * n


def _pair_sum_many(mine, theirs, *, name):
    n = len(mine)

    def body(*refs):
        for a in range(n):
            refs[2 * n + a][...] = refs[a][...] + refs[n + a][0]

    return pl.pallas_call(
        body, out_shape=[jax.ShapeDtypeStruct(m.shape, m.dtype) for m in mine], in_specs=_vmem_specs(2 * n),
        out_specs=_vmem_specs(n), name=name)(*mine, *theirs)


def _chip_sum_many(own, recv, q_idx, *, name):
    n = len(own)

    def body(q_ref, *refs):
        q = q_ref[0]
        for a in range(n):
            mine, r = refs[a][...], refs[n + a]
            total = None
            for chip in range(4):
                e = q ^ chip
                term = jnp.where(e == 0, mine, jnp.where(e == 2, r[0], jnp.where(e == 1, r[1], r[2])))
                total = term if total is None else total + term
            refs[2 * n + a][...] = total

    return pl.pallas_call(
        body, out_shape=[jax.ShapeDtypeStruct(m.shape, m.dtype) for m in own],
        in_specs=[pl.BlockSpec(memory_space=pltpu.SMEM)] + _vmem_specs(2 * n), out_specs=_vmem_specs(n),
        name=name)(q_idx, *own, *recv)


def _adam_many(gs, ws, ms, vs, *, name):
    n = len(gs)

    def body(*refs):
        for a in range(n):
            delta, mn, vn = _adamw(*(refs[k * n + a][...] for k in range(4)))
            refs[4 * n + a][...] = delta
            refs[5 * n + a][...] = mn
            refs[6 * n + a][...] = vn

    shapes = [jax.ShapeDtypeStruct(w.shape, w.dtype) for w in ws]
    out = pl.pallas_call(body, out_shape=shapes * 3, in_specs=_vmem_specs(4 * n), out_specs=_vmem_specs(3 * n),
                         name=name)(*gs, *ws, *ms, *vs)
    return out[:n], out[n:2 * n], out[2 * n:]


WEIGHTS = ["mix_norm_w", "w_in", "gate_bias", "gmlp_ln_w", "gmlp_ln_b", "gmlp_ws", "gmlp_bs", "ssm_conv_w",
           "ssm_conv_b", "ssm_dt_bias", "ssm_a_log", "ssm_d", "ssm_norm_w", "w_proj_a", "w_proj_b", "w_out",
           "ffn_norm_w", "ffn_w_up", "ffn_conv_w", "ffn_conv_b", "ffn_w_down", "final_norm_w"]
SHARDED = {"w_in": ((D, IN_COLS), 1), "gate_bias": ((2, D), 1), "ssm_conv_w": ((SK, SXBC), 1),
           "w_proj_a": ((GW, D), 0), "w_proj_b": ((SI, D), 0), "w_out": ((D, D), 0),
           "ffn_w_up": ((D, 2 * DFF), 1), "ffn_conv_w": ((FK, 2 * DFF), 1), "ffn_w_down": ((DFF, D), 0)}
REPLICATED = {"mix_norm_w": (D,), "gmlp_ln_w": (GG, GD), "gmlp_ln_b": (GG, GD), "gmlp_ws": (GG, GB, GB),
              "gmlp_bs": (GG, GB), "ssm_conv_b": (SXBC,), "ssm_dt_bias": (SH,), "ssm_a_log": (SH,), "ssm_d": (SH,),
              "ssm_norm_w": (SI,), "ffn_norm_w": (D,), "ffn_conv_b": (2 * DFF,), "final_norm_w": (D,)}
REPL_ORDER = [n for n in WEIGHTS if n in REPLICATED]
BTILE = 2 * SUBLANE
WIN_R = IN_COLS // NDEV
WIN_P = WIN_R + BTILE - WIN_R % BTILE
WIN_A = [WIN_R * d // BTILE * BTILE for d in range(NDEV)]
assert all(WIN_A[d] + WIN_P >= WIN_R * (d + 1) for d in range(NDEV)) and WIN_A[-1] + WIN_P == IN_COLS
BIG = [("w_proj_a", GW // NDEV, False), ("w_proj_b", SI // NDEV, False), ("w_out", D // NDEV, False),
       ("ffn_w_up", 2 * DFF // NDEV, True), ("ffn_w_down", DFF // NDEV, False), ("w_in", WIN_P, True)]
VECTORS = ["gate_bias", "ssm_conv_w", "ffn_conv_w"]


def _round_up(n, k):
    return (n + k - 1) // k * k


BIG_OFF = {}
_off = 0
for _n, _r, _t in BIG:
    BIG_OFF[_n] = _off
    _off += _r
BIG_USED = _off
BIG_ROWS = _round_up(BIG_USED, 2 * SUBLANE)
assert all(BIG_OFF[n] % (2 * SUBLANE) == 0 for n, _, _ in BIG)
VEC_SHAPE = {n: (SHARDED[n][0][0], SHARDED[n][0][1] // NDEV) for n in VECTORS}


def _win_offset(dev):
    return WIN_R * dev - WIN_R * dev // BTILE * BTILE


def _pack_big(arrs, dtype, dev):
    parts = []
    for n, r, t in BIG:
        a = (arrs[n].T if t else arrs[n]).astype(dtype)
        if n == "w_in":
            a = lax.dynamic_update_slice(jnp.zeros((WIN_P, D), dtype), a, (_win_offset(dev), 0))
        parts.append(a)
    parts.append(jnp.zeros((BIG_ROWS - BIG_USED, D), dtype))
    return jnp.concatenate(parts, axis=0)


def _join_windows(win):
    parts = []
    for d in range(NDEV):
        lo = BTILE if WIN_A[d] % WIN_R else 0
        if lo:
            parts.append(win[d - 1, WIN_P - BTILE:] + win[d, :BTILE])
        hi = WIN_P - BTILE if d + 1 < NDEV and WIN_A[d + 1] < WIN_A[d] + WIN_P else WIN_P
        parts.append(win[d, lo:hi])
    return jnp.concatenate(parts, axis=0)


def _split_windows(g):
    return jnp.stack([g[a:a + WIN_P] for a in WIN_A])


LATE_ROWS = BIG_OFF["w_in"]
assert LATE_ROWS + WIN_P == BIG_ROWS and BIG[-1][0] == "w_in"


def _remote(src, dst, send_sems, recv_sems, k, to):
    return pltpu.make_async_remote_copy(src_ref=src, dst_ref=dst, send_sem=send_sems.at[k], recv_sem=recv_sems.at[k],
                                        device_id=to, device_id_type=MESH)


class _Exchange:
    def __init__(self, late_shard, c_idx):
        self.late_shard, self.c_idx = late_shard, c_idx

    def gather_start(self):
        shard = self.late_shard

        def make(ins, outs, send_sems, recv_sems):
            (x_ref,), (out,) = ins, outs
            x, y, c = _axes()
            mine = out.at[4 * x + 2 * y + c]
            peers = [(x, y, 1 - c), (1 - x, y, c), (x, 1 - y, c), (1 - x, 1 - y, c)]
            copies = [_remote(x_ref, mine, send_sems, recv_sems, k, p) for k, p in enumerate(peers)]
            return copies + [pltpu.make_async_copy(x_ref, mine, send_sems.at[len(peers)])]

        return _Side([shard], [jax.ShapeDtypeStruct((NDEV,) + shard.shape, shard.dtype)], 5, make)

    def gather_pass_on(self, outs):
        (buf,) = outs

        def make(ins, outs, send_sems, recv_sems):
            (src,), (dst,) = ins, outs
            x, y, c = _axes()
            slots = [4 * px + 2 * py + c for px, py in [(1 - x, y), (x, 1 - y), (1 - x, 1 - y)]]
            return [_remote(src.at[s], dst.at[s], send_sems, recv_sems, k, (x, y, 1 - c)) for k, s in enumerate(slots)]

        return _Side([buf], [jax.ShapeDtypeStruct(buf.shape, buf.dtype)], 3, make, aliases=[(0, 0)])

    def late_weights(self, w, outs):
        (buf,) = outs
        w = dict(w)
        for n, r, t in BIG[:-1]:
            w[n + "_t" if t else n] = buf[:, BIG_OFF[n]:BIG_OFF[n] + r].reshape(NDEV * r, D)
        return w

    def _reduce(self, send, tag):
        (sib,) = _exchange([send], _to_sibling_plan, name=tag + "_grads_to_sibling")
        sums = _pair_add(send.reshape((4, 2) + send.shape[1:]), sib, self.c_idx, name=tag + "_grad_pair_add")

        def make(ins, outs, send_sems, recv_sems):
            (src,), (dst,) = ins, outs
            return [_remote(src.at[slab], dst.at[k], send_sems, recv_sems, k, peer)
                    for k, (slab, peer) in enumerate(_to_chips_plan(*_axes()))]

        return sums, _Side([sums], [jax.ShapeDtypeStruct((3,) + sums.shape[1:], sums.dtype)], 3, make)

    def reduce_late(self, grads):
        send = jnp.concatenate([grads[n + "_t" if t else n].reshape(NDEV, r, D) for n, r, t in BIG[:-1]], axis=1)
        self.late_sum, side = self._reduce(send.astype(BF16), "late")
        return side

    def reduce_w_in(self, grad):
        self.w_in_sum, side = self._reduce(_split_windows(grad).astype(BF16), "w_in")
        return side

    def reduced(self, late_outs, w_in_outs):
        (self.late_from_chips,), (self.w_in_from_chips,) = late_outs, w_in_outs


def kernel(x, mix_norm_w, w_in, gate_bias, gmlp_ln_w, gmlp_ln_b, gmlp_ws, gmlp_bs, ssm_conv_w, ssm_conv_b, ssm_dt_bias, ssm_a_log, ssm_d, ssm_norm_w, w_proj_a, w_proj_b, w_out, ffn_norm_w, ffn_w_up, ffn_conv_w, ffn_conv_b, ffn_w_down, final_norm_w, loss_target, m_mix_norm_w, m_w_in, m_gate_bias, m_gmlp_ln_w, m_gmlp_ln_b, m_gmlp_ws, m_gmlp_bs, m_ssm_conv_w, m_ssm_conv_b, m_ssm_dt_bias, m_ssm_a_log, m_ssm_d, m_ssm_norm_w, m_w_proj_a, m_w_proj_b, m_w_out, m_ffn_norm_w, m_ffn_w_up, m_ffn_conv_w, m_ffn_conv_b, m_ffn_w_down, m_final_norm_w, v_mix_norm_w, v_w_in, v_gate_bias, v_gmlp_ln_w, v_gmlp_ln_b, v_gmlp_ws, v_gmlp_bs, v_ssm_conv_w, v_ssm_conv_b, v_ssm_dt_bias, v_ssm_a_log, v_ssm_d, v_ssm_norm_w, v_w_proj_a, v_w_proj_b, v_w_out, v_ffn_norm_w, v_ffn_w_up, v_ffn_conv_w, v_ffn_conv_b, v_ffn_w_down, v_final_norm_w):
    given = dict(locals())
    wts = {n: given[n] for n in WEIGHTS}
    mom = {n: given["m_" + n] for n in WEIGHTS}
    var = {n: given["v_" + n] for n in WEIGHTS}
    xi, yi, ci = _axes()
    c_idx = jnp.reshape(ci, (1,)).astype(jnp.int32)
    q_idx = jnp.reshape(2 * xi + yi, (1,)).astype(jnp.int32)
    big_names = [n for n, _, _ in BIG]
    drop = lambda d, names: {n: d[n][0] for n in names}

    dev = 4 * xi + 2 * yi + ci
    packed = _pack_big(drop(wts, big_names), BF16, dev)
    gathered = _all_gather([packed[LATE_ROWS:]] + [wts[n] for n in VECTORS], name="w_in_all_gather")
    full = {"w_in_t": _join_windows(gathered[0])}
    for n, a in zip(VECTORS, gathered[1:]):
        r, c = VEC_SHAPE[n]
        full[n] = a[:, 0].transpose(1, 0, 2).reshape(r, NDEV * c)
    for n in REPL_ORDER:
        full[n] = wts[n].reshape(REPLICATED[n])

    hooks = _Exchange(packed[:LATE_ROWS], c_idx)
    loss_local, grad_x, grads = _local_step(x[0], loss_target[0], full, hooks)
    g_late = _grad_sum(hooks.late_sum, hooks.late_from_chips, q_idx, name="late_grad_sum")
    g_win = _grad_sum(hooks.w_in_sum, hooks.w_in_from_chips, q_idx, name="w_in_grad_sum")

    small = VECTORS + REPL_ORDER
    as_2d = lambda a: a if a.ndim >= 2 else a[None]
    part = [grads[n].reshape((1,) + SHARDED[n][0] if n in VECTORS else as_2d(wts[n]).shape) for n in small]
    part.append(loss_local)
    from_sibling = _exchange([p[None] for p in part], lambda x, y, c: [(0, (x, y, 1 - c))],
                             name="small_grads_to_sibling")
    chip_sums = _pair_sum_many(part, from_sibling, name="small_grad_pair_sum")
    from_chips = _exchange([s[None] for s in chip_sums],
                           lambda x, y, c: [(0, (1 - x, y, c)), (0, (x, 1 - y, c)), (0, (1 - x, 1 - y, c))],
                           name="small_grads_to_chips")
    totals = _chip_sum_many(chip_sums, from_chips, q_idx, name="small_grad_chip_sum")
    g_small, loss = dict(zip(small, totals)), totals[-1][0, 0]
    for n in VECTORS:
        c = VEC_SHAPE[n][1]
        g_small[n] = lax.dynamic_slice_in_dim(g_small[n], dev * c, c, axis=2)

    outs = {}
    small_g = [g_small[n] for n in small]
    small_out = _adam_many(small_g, *[[as_2d(d[n]) for n in small] for d in (wts, mom, var)], name="adam_small")
    for i, n in enumerate(small):
        outs[n] = tuple(a[i].reshape(wts[n].shape) for a in (small_g,) + tuple(small_out))
    for n, r, t in BIG:
        if n == "w_in":
            g = lax.dynamic_slice(g_win, (_win_offset(dev), 0), (WIN_R, D))
        else:
            g = g_late[BIG_OFF[n]:BIG_OFF[n] + r]
        flip = (lambda a: a.transpose(0, 2, 1)) if t else (lambda a: a)
        g = g[None]
        new = _adam(g, flip(wts[n]), flip(mom[n]), flip(var[n]), name="adam_" + n)
        outs[n] = tuple(flip(a) for a in (g,) + tuple(new))
    return (loss, grad_x[None]) + tuple(outs[n][k] for k in range(4) for n in WEIGHTS)
```

```python
import functools

import jax
import jax.numpy as jnp
from jax import lax
from jax.experimental import pallas as pl
from jax.experimental.pallas import tpu as pltpu

F32 = jnp.float32
BF16 = jnp.bfloat16

D = 1024
EPS = 1e-5
GW = 1024
GB = 128
GG = 8
GD = 128
GCH = 64
SI = 2048
SH = 32
SP = 64
SG = 4
SN = 128
SGW = SI // SG
SK = 4
SXBC = SI + 2 * SG * SN
DFF = 2816
FK = 3
PMAIN = 2 * D + 2 * GW + SI + SXBC
IN_COLS = PMAIN + SH
DP_SSM = SI + SXBC
DP_GAP = (DP_SSM - (2 * D + 2 * GW) % DP_SSM) % DP_SSM
DP_COLS = 2 * D + 2 * GW + DP_GAP + DP_SSM
assert DP_GAP % D == 0 and (2 * D + 2 * GW) % D == 0
NDEV = 8
ADAM_LR, ADAM_B1, ADAM_B2, ADAM_EPS, ADAM_WD, ADAM_STEP = 0.001, 0.9, 0.999, 1e-08, 0.01, 10

LANE = 128
SUBLANE = 8
VMEM_MB_V7X = 64
VMEM_CAP_MB = VMEM_MB_V7X - 8

LS = 128
FT = DFF // 2

NN = (((1,), (0,)), ((), ()))
NT = (((1,), (1,)), ((), ()))
TN = (((0,), (0,)), ((), ()))


def _params(sem, vmem_mb):
    return pltpu.CompilerParams(dimension_semantics=sem,
                                vmem_limit_bytes=min(int(vmem_mb), VMEM_CAP_MB) * 1024 * 1024)


def _dot(a, b, dims=NN):
    return lax.dot_general(a, b, dims, preferred_element_type=F32)


def _sigmoid(x):
    return 1.0 / (1.0 + jnp.exp(-x))


def _split3(v):
    hi = v.astype(BF16)
    r = v - hi.astype(F32)
    mid = r.astype(BF16)
    lo = (r - mid.astype(F32)).astype(BF16)
    return hi, mid, lo


def _dot3(a_f32, b_bf16, dims):
    hi, mid, lo = _split3(a_f32)
    return _dot(hi, b_bf16, dims) + _dot(mid, b_bf16, dims) + _dot(lo, b_bf16, dims)


def _dot2(a_f32, b_bf16, dims):
    hi, mid, _ = _split3(a_f32)
    return _dot(hi, b_bf16, dims) + _dot(mid, b_bf16, dims)


def _dot3_rhs(a_bf16, b_f32, dims):
    hi, mid, lo = _split3(b_f32)
    return _dot(a_bf16, hi, dims) + _dot(a_bf16, mid, dims) + _dot(a_bf16, lo, dims)


def _matmul(a, b, *, name, out_dtype, ta=False, tb=False, tm=1024, tn=1024, tk=1024, add=None,
            j_outer=False, b_rows=None, a_gap=None, side=None):
    gap0, gapw = a_gap or (0, 0)
    if ta:
        K, M = a.shape
        M -= gapw
    else:
        M, K = a.shape
        K -= gapw
    if tb:
        N, K2 = b.shape
        N = b_rows or N
    else:
        K2, N = b.shape
        K2 = b_rows or K2
    assert K == K2, (a.shape, b.shape, ta, tb)
    tm, tn, tk = min(tm, M), min(tn, N), min(tk, K)
    assert M % tm == 0 and N % tn == 0 and K % tk == 0, (M, N, K, tm, tn, tk)
    nk = K // tk
    dims = (((0 if ta else 1,), (1 if tb else 0,)), ((), ()))
    has_add = add is not None
    n_in = 3 if has_add else 2
    s_in = len(side.inputs) if side else 0
    s_out = len(side.out_shapes) if side else 0
    grid = (N // tn, M // tm, nk) if j_outer else (M // tm, N // tn, nk)

    def body(*refs):
        a_ref, b_ref = refs[:2]
        add_ref = refs[2] if has_add else None
        o_ref = refs[n_in + s_in]
        if side:
            side_refs = (refs[n_in:n_in + s_in], refs[n_in + s_in + 1:n_in + s_in + 1 + s_out]) + tuple(refs[-2:])
            ids = [pl.program_id(d) for d in range(3)]
            first = functools.reduce(jnp.logical_and, [i == 0 for i in ids])
            last = functools.reduce(jnp.logical_and, [i == g - 1 for i, g in zip(ids, grid)])

            @pl.when(first)
            def _():
                for cp in side.make(*side_refs):
                    cp.start()

            @pl.when(last)
            def _():
                for cp in side.make(*side_refs):
                    cp.wait()

        p = lax.dot_general(a_ref[...].astype(BF16), b_ref[...].astype(BF16), dims,
                            preferred_element_type=F32)

        def finish(acc):
            if has_add:
                acc = acc + add_ref[...].astype(F32)
            o_ref[...] = acc.astype(o_ref.dtype)

        if nk == 1:
            finish(p)
        else:
            acc_ref = refs[n_in + s_in + 1 + s_out]
            k = pl.program_id(2)

            @pl.when(k == 0)
            def _():
                acc_ref[...] = p

            @pl.when(jnp.logical_and(k > 0, k < nk - 1))
            def _():
                acc_ref[...] += p

            @pl.when(k == nk - 1)
            def _():
                finish(acc_ref[...] + p)

    if j_outer:
        ij = lambda g0, g1: (g1, g0)
    else:
        ij = lambda g0, g1: (g0, g1)

    ta_col = tm if ta else tk
    assert gap0 % ta_col == 0 and gapw % ta_col == 0, (a_gap, ta_col)

    def a_map(g0, g1, k):
        i, _ = ij(g0, g1)
        col = i if ta else k
        col = col + jnp.where(col >= gap0 // ta_col, gapw // ta_col, 0) if gapw else col
        return (k, col) if ta else (i, col)

    def b_map(g0, g1, k):
        _, j = ij(g0, g1)
        return (j, k) if tb else (k, j)

    def o_map(g0, g1, k):
        return ij(g0, g1)

    in_specs = [pl.BlockSpec((tk, tm) if ta else (tm, tk), a_map),
                pl.BlockSpec((tn, tk) if tb else (tk, tn), b_map)]
    args = [a, b]
    if has_add:
        in_specs.append(pl.BlockSpec((tm, tn), o_map))
        args.append(add)
    scratch = [pltpu.VMEM((tm, tn), F32)] if nk > 1 else []
    osz = jnp.dtype(out_dtype).itemsize
    est = (2 * (tm * tk * a.dtype.itemsize + tk * tn * b.dtype.itemsize) + 2 * tm * tn * osz
           + (2 * tm * tn * add.dtype.itemsize if has_add else 0)
           + 3 * tm * tn * 4 + 2 * (tm * tk + tk * tn)) / 2 ** 20 + 4
    out_specs = [pl.BlockSpec((tm, tn), o_map)]
    out_shape = [jax.ShapeDtypeStruct((M, N), out_dtype)]
    aliases = {}
    if side:
        hbm = pl.BlockSpec(memory_space=pltpu.HBM)
        in_specs += [hbm] * s_in
        args += list(side.inputs)
        out_specs += [hbm] * s_out
        out_shape += list(side.out_shapes)
        scratch += [pltpu.SemaphoreType.DMA((side.nsem,)), pltpu.SemaphoreType.DMA((side.nsem,))]
        aliases = {n_in + i: 1 + j for i, j in side.aliases}
    outs = pl.pallas_call(
        body, grid=grid, in_specs=in_specs, out_specs=out_specs, out_shape=out_shape, scratch_shapes=scratch,
        input_output_aliases=aliases, name=name,
        compiler_params=_params(("arbitrary", "arbitrary", "arbitrary"), est))(*args)
    return (outs[0], list(outs[1:])) if side else outs[0]


class _Side:
    def __init__(self, inputs, out_shapes, nsem, make, aliases=()):
        self.inputs, self.out_shapes, self.nsem, self.make, self.aliases = inputs, out_shapes, nsem, make, aliases


def _rms_fwd(x, w, *, name):
    T = x.shape[0]
    tm = min(512, T)

    def body(x_ref, w_ref, o_ref):
        xv = x_ref[...]
        r = lax.rsqrt(jnp.mean(xv * xv, axis=-1, keepdims=True) + EPS)
        o_ref[...] = (xv * r * w_ref[...]).astype(BF16)

    return pl.pallas_call(
        body, grid=(T // tm,),
        in_specs=[pl.BlockSpec((tm, D), lambda i: (i, 0)), pl.BlockSpec((1, D), lambda i: (0, 0))],
        out_specs=pl.BlockSpec((tm, D), lambda i: (i, 0)),
        out_shape=jax.ShapeDtypeStruct((T, D), BF16), name=name,
        compiler_params=_params(("arbitrary",), 24))(x, w)


def _rms_bwd(x, w, dy, dres, *, name):
    T = x.shape[0]
    tm = min(512, T)

    def body(x_ref, w_ref, dy_ref, dres_ref, dx_ref, dw_ref):
        xv = x_ref[...]
        r = lax.rsqrt(jnp.mean(xv * xv, axis=-1, keepdims=True) + EPS)
        xhat = xv * r
        dyv = dy_ref[...].astype(F32)
        g = dyv * w_ref[...]
        dx_ref[...] = dres_ref[...] + r * (g - xhat * jnp.mean(g * xhat, axis=-1, keepdims=True))
        part = jnp.sum(dyv * xhat, axis=0, keepdims=True)

        @pl.when(pl.program_id(0) == 0)
        def _():
            dw_ref[...] = part

        @pl.when(pl.program_id(0) > 0)
        def _():
            dw_ref[...] += part

    row = pl.BlockSpec((tm, D), lambda i: (i, 0))
    vec = pl.BlockSpec((1, D), lambda i: (0, 0))
    return pl.pallas_call(
        body, grid=(T // tm,), in_specs=[row, vec, row, row], out_specs=[row, vec],
        out_shape=[jax.ShapeDtypeStruct((T, D), F32), jax.ShapeDtypeStruct((1, D), F32)], name=name,
        compiler_params=_params(("arbitrary",), 32))(x, w, dy, dres)


def _loss_head(h, tgt, w):
    T = h.shape[0]
    tm = min(512, T)

    def body(h_ref, t_ref, w_ref, loss_ref, dh_ref, dw_ref):
        hv = h_ref[...]
        r = lax.rsqrt(jnp.mean(hv * hv, axis=-1, keepdims=True) + EPS)
        xhat = hv * r
        wv = w_ref[...]
        err = xhat * wv - t_ref[...]
        lpart = 0.5 * jnp.sum(jnp.mean(err * err, axis=-1, keepdims=True), axis=0, keepdims=True)
        dy = err * (1.0 / D)
        g = dy * wv
        dh_ref[...] = r * (g - xhat * jnp.mean(g * xhat, axis=-1, keepdims=True))
        wpart = jnp.sum(dy * xhat, axis=0, keepdims=True)
        lrow = jnp.broadcast_to(lpart, (1, LANE))

        @pl.when(pl.program_id(0) == 0)
        def _():
            dw_ref[...] = wpart
            loss_ref[...] = lrow

        @pl.when(pl.program_id(0) > 0)
        def _():
            dw_ref[...] += wpart
            loss_ref[...] += lrow

    row = pl.BlockSpec((tm, D), lambda i: (i, 0))
    vec = pl.BlockSpec((1, D), lambda i: (0, 0))
    return pl.pallas_call(
        body, grid=(T // tm,), in_specs=[row, row, vec],
        out_specs=[pl.BlockSpec((1, LANE), lambda i: (0, 0)), row, vec],
        out_shape=[jax.ShapeDtypeStruct((1, LANE), F32), jax.ShapeDtypeStruct((T, D), F32),
                   jax.ShapeDtypeStruct((1, D), F32)], name="loss_head",
        compiler_params=_params(("arbitrary",), 32))(h, tgt, w)


_GELU_C = 0.7978845608028654
_GELU_A = 0.044715


def _gelu(x, with_grad=False):
    x2 = x * x
    cx = _GELU_C * x
    t = jnp.tanh(cx * (1.0 + _GELU_A * x2))
    h = 0.5 * (1.0 + t)
    if not with_grad:
        return x * h
    return x * h, h + 0.5 * cx * (1.0 - t * t) * (1.0 + 3.0 * _GELU_A * x2)


def _gmlp_mask():
    r = lax.broadcasted_iota(jnp.int32, (GB, GB), 0) // GCH
    c = lax.broadcasted_iota(jnp.int32, (GB, GB), 1) // GCH
    return c <= r


def _gmlp_fwd(proj, lnw, lnb, ws, bst):
    T = proj.shape[0]
    tm = min(512, T)
    nblk = tm // GB

    def body(u_ref, v_ref, lnw_ref, lnb_ref, ws_ref, bst_ref, o_ref):
        mask = _gmlp_mask()
        u = _gelu(u_ref[...].astype(F32))
        v = _gelu(v_ref[...].astype(F32))
        for g in range(GG):
            cs = slice(g * GD, (g + 1) * GD)
            vg = v[:, cs]
            mu = jnp.mean(vg, axis=-1, keepdims=True)
            vc = vg - mu
            var = jnp.mean(vc * vc, axis=-1, keepdims=True)
            vn = (vc * lax.rsqrt(var + EPS) * lnw_ref[g:g + 1, :] + lnb_ref[g:g + 1, :]).astype(BF16)
            wsg = jnp.where(mask, ws_ref[g], 0.0).astype(BF16)
            bcol = bst_ref[:, g:g + 1]
            for blk in range(nblk):
                rs = slice(blk * GB, (blk + 1) * GB)
                sv = _dot(wsg, vn[rs, :]) + bcol
                o_ref[rs, cs] = (u[rs, cs] * sv).astype(BF16)

    full = lambda shape: pl.BlockSpec(shape, lambda i: tuple(0 for _ in shape))
    return pl.pallas_call(
        body, grid=(T // tm,),
        in_specs=[pl.BlockSpec((tm, GW), lambda i: (i, 2)), pl.BlockSpec((tm, GW), lambda i: (i, 3)),
                  full((GG, GD)), full((GG, GD)), full((GG, GB, GB)), full((GB, GG))],
        out_specs=pl.BlockSpec((tm, GW), lambda i: (i, 0)),
        out_shape=jax.ShapeDtypeStruct((T, GW), BF16), name="gmlp_fwd",
        compiler_params=_params(("arbitrary",), 40))(proj, proj, lnw, lnb, ws, bst)


def _gmlp_bwd(proj, dya, dproj, lnw, lnb, ws, bst):
    T = proj.shape[0]
    tm = min(512, T)
    nblk = tm // GB

    def body(u_ref, v_ref, dya_ref, dproj_in, lnw_ref, lnb_ref, ws_ref, bst_ref,
             dz_ref, dlnw_ref, dlnb_ref, dws_ref, dbst_ref):
        del dproj_in
        first = pl.program_id(0) == 0

        @pl.when(first)
        def _():
            dlnw_ref[...] = jnp.zeros_like(dlnw_ref)
            dlnb_ref[...] = jnp.zeros_like(dlnb_ref)
            dws_ref[...] = jnp.zeros_like(dws_ref)
            dbst_ref[...] = jnp.zeros_like(dbst_ref)

        mask = _gmlp_mask()
        lane = lax.broadcasted_iota(jnp.int32, (GB, LANE), 1)
        ur = u_ref[...].astype(F32)
        vr = v_ref[...].astype(F32)
        u, gu = _gelu(ur, with_grad=True)
        v, gv = _gelu(vr, with_grad=True)
        dy = dya_ref[...].astype(F32)
        dbst = jnp.zeros((GB, LANE), F32)
        dlnw_rows, dlnb_rows = [], []
        for g in range(GG):
            cs = slice(g * GD, (g + 1) * GD)
            vg = v[:, cs]
            mu = jnp.mean(vg, axis=-1, keepdims=True)
            vc = vg - mu
            var = jnp.mean(vc * vc, axis=-1, keepdims=True)
            rstd = lax.rsqrt(var + EPS)
            xhat = vc * rstd
            lw = lnw_ref[g:g + 1, :]
            vn = (xhat * lw + lnb_ref[g:g + 1, :]).astype(BF16)
            wsg = jnp.where(mask, ws_ref[g], 0.0).astype(BF16)
            bcol = bst_ref[:, g:g + 1]
            dyg = dy[:, cs]
            ug = u[:, cs]
            dsv = dyg * ug
            dsv_b = dsv.astype(BF16)
            dws_g = jnp.zeros((GB, GB), F32)
            bsum = jnp.zeros((GB, 1), F32)
            dvn_parts = []
            for blk in range(nblk):
                rs = slice(blk * GB, (blk + 1) * GB)
                sv = _dot(wsg, vn[rs, :]) + bcol
                dz_ref[rs, cs] = (dyg[rs, :] * sv * gu[rs, cs]).astype(BF16)
                dws_g = dws_g + _dot(dsv_b[rs, :], vn[rs, :], NT)
                bsum = bsum + jnp.sum(dsv[rs, :], axis=-1, keepdims=True)
                dvn_parts.append(_dot(wsg, dsv_b[rs, :], TN))
            dvn = jnp.concatenate(dvn_parts, axis=0)
            dws_ref[g] += jnp.where(mask, dws_g, 0.0)
            dbst = dbst + jnp.where(lane == g, bsum, 0.0)
            dlnw_rows.append(jnp.sum(dvn * xhat, axis=0, keepdims=True))
            dlnb_rows.append(jnp.sum(dvn, axis=0, keepdims=True))
            dxh = dvn * lw
            dvg = rstd * (dxh - jnp.mean(dxh, axis=-1, keepdims=True)
                          - xhat * jnp.mean(dxh * xhat, axis=-1, keepdims=True))
            dz_ref[:, GW + g * GD:GW + (g + 1) * GD] = (dvg * gv[:, cs]).astype(BF16)
        dlnw_ref[...] += jnp.concatenate(dlnw_rows, axis=0)
        dlnb_ref[...] += jnp.concatenate(dlnb_rows, axis=0)
        dbst_ref[...] += dbst

    full = lambda shape: pl.BlockSpec(shape, lambda i: tuple(0 for _ in shape))
    outs = pl.pallas_call(
        body, grid=(T // tm,),
        in_specs=[pl.BlockSpec((tm, GW), lambda i: (i, 2)), pl.BlockSpec((tm, GW), lambda i: (i, 3)),
                  pl.BlockSpec((tm, GW), lambda i: (i, 0)), pl.BlockSpec(memory_space=pl.ANY),
                  full((GG, GD)), full((GG, GD)), full((GG, GB, GB)), full((GB, GG))],
        out_specs=[pl.BlockSpec((tm, 2 * GW), lambda i: (i, 1)), full((GG, GD)), full((GG, GD)),
                   full((GG, GB, GB)), full((GB, LANE))],
        out_shape=[jax.ShapeDtypeStruct(dproj.shape, dproj.dtype), jax.ShapeDtypeStruct((GG, GD), F32),
                   jax.ShapeDtypeStruct((GG, GD), F32), jax.ShapeDtypeStruct((GG, GB, GB), F32),
                   jax.ShapeDtypeStruct((GB, LANE), F32)],
        input_output_aliases={3: 0}, name="gmlp_bwd",
        compiler_params=_params(("arbitrary",), 48))(proj, proj, dya, dproj, lnw, lnb, ws, bst)
    return outs


def _merge_fwd(ya_pre, yb_pre, proj, bias, wpa, wpb):
    T = proj.shape[0]
    tm = min(512, T)

    def body(ya_ref, yb_ref, g_ref, b_ref, wpa_ref, wpb_ref, m_ref, oa_ref, ob_ref):
        ya = _dot(ya_ref[...], wpa_ref[...])
        yb = _dot(yb_ref[...], wpb_ref[...])
        g = g_ref[...].astype(F32)
        sa = _sigmoid(g[:, :D] + b_ref[0:1, :])
        sb = _sigmoid(g[:, D:] + b_ref[1:2, :])
        m_ref[...] = (sa * ya + sb * yb).astype(BF16)
        oa_ref[...] = ya.astype(BF16)
        ob_ref[...] = yb.astype(BF16)

    row = lambda w: pl.BlockSpec((tm, w), lambda i: (i, 0))
    full = lambda shape: pl.BlockSpec(shape, lambda i: tuple(0 for _ in shape))
    o = jax.ShapeDtypeStruct((T, D), BF16)
    return pl.pallas_call(
        body, grid=(T // tm,),
        in_specs=[row(GW), row(SI), row(2 * D), full((2, D)), full((GW, D)), full((SI, D))],
        out_specs=[row(D), row(D), row(D)], out_shape=[o, o, o], name="merge_fwd",
        compiler_params=_params(("arbitrary",), 40))(ya_pre, yb_pre, proj, bias, wpa, wpb)


def _merge_bwd(dm, proj, bias, ya, yb, wpa, wpb):
    T = proj.shape[0]
    tm = min(512, T)

    def body(dm_ref, g_ref, b_ref, ya_ref, yb_ref, wpa_ref, wpb_ref,
             dg_ref, dya_ref, dyb_ref, dpa_ref, dpb_ref, db_ref):
        dmv = dm_ref[...].astype(F32)
        g = g_ref[...].astype(F32)
        sa = _sigmoid(g[:, :D] + b_ref[0:1, :])
        sb = _sigmoid(g[:, D:] + b_ref[1:2, :])
        dya = (dmv * sa).astype(BF16)
        dyb = (dmv * sb).astype(BF16)
        dga = dmv * ya_ref[...].astype(F32) * sa * (1.0 - sa)
        dgb = dmv * yb_ref[...].astype(F32) * sb * (1.0 - sb)
        dg_ref[:, :D] = dga.astype(BF16)
        dg_ref[:, D:] = dgb.astype(BF16)
        dya_ref[...] = dya
        dyb_ref[...] = dyb
        dpa_ref[...] = _dot(dya, wpa_ref[...], NT).astype(BF16)
        dpb_ref[...] = _dot(dyb, wpb_ref[...], NT).astype(BF16)
        part = jnp.concatenate([jnp.sum(dga, axis=0, keepdims=True), jnp.sum(dgb, axis=0, keepdims=True)], axis=0)

        @pl.when(pl.program_id(0) == 0)
        def _():
            db_ref[...] = part

        @pl.when(pl.program_id(0) > 0)
        def _():
            db_ref[...] += part

    row = lambda w: pl.BlockSpec((tm, w), lambda i: (i, 0))
    full = lambda shape: pl.BlockSpec(shape, lambda i: tuple(0 for _ in shape))
    o = lambda w: jax.ShapeDtypeStruct((T, w), BF16)
    return pl.pallas_call(
        body, grid=(T // tm,),
        in_specs=[row(D), row(2 * D), full((2, D)), row(D), row(D), full((GW, D)), full((SI, D))],
        out_specs=[row(2 * D), row(D), row(D), row(GW), row(SI), full((2, D))],
        out_shape=[o(DP_COLS), o(D), o(D), o(GW), o(SI), jax.ShapeDtypeStruct((2, D), F32)], name="merge_bwd",
        compiler_params=_params(("arbitrary",), 48))(dm, proj, bias, ya, yb, wpa, wpb)


RB = 128


def _shift_matrix(j):
    r = lax.broadcasted_iota(jnp.int32, (RB, RB), 0)
    c = lax.broadcasted_iota(jnp.int32, (RB, RB), 1)
    return jnp.where(c == r - j, 1.0, 0.0).astype(BF16)


def _rows_down(xb, before, shifts):
    H = SUBLANE
    mats = [_shift_matrix(j) for j in shifts]
    outs = [[] for _ in shifts]
    for b in range(xb.shape[0] // RB):
        blk = xb[b * RB:(b + 1) * RB]
        edge = jnp.concatenate([before, blk[:2 * H].astype(F32)[:H]], axis=0)
        for i, j in enumerate(shifts):
            outs[i] += [edge[H - j:2 * H - j], _dot(mats[i], blk)[H:]]
        before = blk[RB - 2 * H:].astype(F32)[H:]
    return [jnp.concatenate(o, axis=0) for o in outs]


def _rows_up(xb, after, shifts):
    H = SUBLANE
    nb = xb.shape[0] // RB
    mats = [_shift_matrix(-j) for j in shifts]
    outs = [[] for _ in shifts]
    for b in range(nb):
        blk = xb[b * RB:(b + 1) * RB]
        nxt = xb[(b + 1) * RB:(b + 1) * RB + 2 * H].astype(F32)[:H] if b + 1 < nb else after
        edge = jnp.concatenate([blk[RB - 2 * H:].astype(F32)[H:], nxt], axis=0)
        for i, j in enumerate(shifts):
            outs[i] += [_dot(mats[i], blk)[:RB - H], edge[j:H + j]]
    return [jnp.concatenate(o, axis=0) for o in outs]


def _ffn_act_fwd(up, cw, cb):
    T = up.shape[0]
    tm = min(512, T)
    H = SUBLANE

    def body(up_ref, cw_ref, cb_ref, o_ref, xc_ref, halo):
        @pl.when(pl.program_id(1) == 0)
        def _():
            halo[...] = jnp.zeros_like(halo)

        xb = up_ref[...]
        x2, x1 = _rows_down(xb, halo[...], (2, 1))
        xc = cb_ref[...] + cw_ref[0:1, :] * x2 + cw_ref[1:2, :] * x1 + cw_ref[2:3, :] * xb.astype(F32)
        xc_ref[...] = xc.astype(BF16)
        gate = xc[:, :FT]
        o_ref[...] = (gate * _sigmoid(gate) * xc[:, FT:]).astype(BF16)
        halo[...] = xb[tm - 2 * H:].astype(F32)[H:]

    tile = pl.BlockSpec((tm, 2 * FT), lambda j, i: (i, j))
    return pl.pallas_call(
        body, grid=(2, T // tm),
        in_specs=[tile, pl.BlockSpec((FK, 2 * FT), lambda j, i: (0, j)), pl.BlockSpec((1, 2 * FT), lambda j, i: (0, j))],
        out_specs=[pl.BlockSpec((tm, FT), lambda j, i: (i, j)), tile],
        out_shape=[jax.ShapeDtypeStruct((T, DFF), BF16), jax.ShapeDtypeStruct((T, 2 * DFF), BF16)],
        scratch_shapes=[pltpu.VMEM((H, 2 * FT), F32)], name="ffn_act_fwd",
        compiler_params=_params(("arbitrary", "arbitrary"), 48))(up, cw, cb)


def _ffn_act_bwd(up, xc, dact, cw):
    T = up.shape[0]
    tm = min(512, T)
    nt = T // tm
    H = SUBLANE

    def body(up_ref, xc_ref, da_ref, cw_ref, dup_ref, dcw_ref, dcb_ref, ahead):
        @pl.when(pl.program_id(1) == 0)
        def _():
            ahead[...] = jnp.zeros_like(ahead)
            dcw_ref[...] = jnp.zeros_like(dcw_ref)
            dcb_ref[...] = jnp.zeros_like(dcb_ref)

        xcv = xc_ref[...].astype(F32)
        gate, val = xcv[:, :FT], xcv[:, FT:]
        sg = _sigmoid(gate)
        dav = da_ref[...].astype(F32)
        dgate = dav * val * sg * (1.0 + gate * (1.0 - sg))
        dval = dav * gate * sg
        dxc = jnp.concatenate([dgate, dval], axis=1)
        d1, d2 = _rows_up(dxc.astype(BF16), ahead[...], (1, 2))
        x = up_ref[...].astype(F32)
        dcb_ref[...] += jnp.sum(dxc, axis=0, keepdims=True)
        dcw_ref[...] += jnp.concatenate([jnp.sum(d * x, axis=0, keepdims=True) for d in (d2, d1, dxc)], axis=0)
        dup_ref[...] = (cw_ref[2:3, :] * dxc + cw_ref[1:2, :] * d1 + cw_ref[0:1, :] * d2).astype(BF16)
        ahead[...] = dxc[0:H, :]

    tile = pl.BlockSpec((tm, 2 * FT), lambda j, i: (nt - 1 - i, j))
    return pl.pallas_call(
        body, grid=(2, nt),
        in_specs=[tile, tile, pl.BlockSpec((tm, FT), lambda j, i: (nt - 1 - i, j)),
                  pl.BlockSpec((FK, 2 * FT), lambda j, i: (0, j))],
        out_specs=[tile, pl.BlockSpec((FK, 2 * FT), lambda j, i: (0, j)), pl.BlockSpec((1, 2 * FT), lambda j, i: (0, j))],
        out_shape=[jax.ShapeDtypeStruct((T, 2 * DFF), BF16), jax.ShapeDtypeStruct((FK, 2 * DFF), F32),
                   jax.ShapeDtypeStruct((1, 2 * DFF), F32)],
        scratch_shapes=[pltpu.VMEM((H, 2 * FT), F32)], name="ffn_act_bwd",
        compiler_params=_params(("arbitrary", "arbitrary"), 56))(up, xc, dact, cw)


def _softplus(x):
    e = jnp.exp(-jnp.abs(x))
    return jnp.maximum(x, 0.0) + jnp.where(e < 1e-4, e * (1.0 - 0.5 * e), jnp.log(1.0 + e))


def _ssd_consts():
    li = lax.broadcasted_iota(jnp.int32, (LS, LS), 0)
    si = lax.broadcasted_iota(jnp.int32, (LS, LS), 1)
    tril = si <= li
    hh = lax.broadcasted_iota(jnp.int32, (LANE, SI), 0)
    cc = lax.broadcasted_iota(jnp.int32, (LANE, SI), 1) // SP
    expand = jnp.where(hh == cc, 1.0, 0.0).astype(BF16)
    return tril, expand


def _ssd_pre(xc, dt_ref, dtb_ref, alog_ref, tril, expand):
    sx = _sigmoid(xc)
    xbc = xc * sx
    xs, bm, cm = xbc[:, :SI], xbc[:, SI:SI + SG * SN], xbc[:, SI + SG * SN:]
    dtin = dt_ref[...] + dtb_ref[...]
    dt = _softplus(dtin)
    a_neg = -jnp.exp(alog_ref[...])
    dta = dt * a_neg
    trilb = jnp.where(tril, 1.0, 0.0).astype(BF16)
    a = _dot3_rhs(trilb, dta, NN)
    a_exp = _dot3(a, expand, NN)
    dt_exp = _dot2(dt, expand, NN)
    xdt = xs * dt_exp
    a_last = a_exp[LS - 1:LS, :]
    return dict(xc=xc, sx=sx, xs=xs, bm=bm, cm=cm, dtin=dt_ref[...] + dtb_ref[...], dt=dt, a_neg=a_neg,
                a=a, a_t=a.T, a_exp=a_exp, dt_exp=dt_exp, xdt=xdt, ea=jnp.exp(a_exp),
                w=jnp.exp(a_last - a_exp), eal=jnp.exp(a_last))


def _head_decay(pre, tril, h):
    seg = pre["a"][:, h:h + 1] - pre["a_t"][h:h + 1, :]
    return jnp.exp(jnp.where(tril, seg, -1e30))


def _ssd_fwd(proj, dtraw, cw, cb, dtb, alog, dexp, nw):
    T = proj.shape[0]
    nc = T // LS
    H = SUBLANE

    def body(z_ref, x_ref, dt_ref, cw_ref, cb_ref, dtb_ref, alog_ref, dexp_ref, nw_ref,
             yb_ref, y_ref, sp_ref, xc_ref, halo, st):
        @pl.when(pl.program_id(0) == 0)
        def _():
            halo[...] = jnp.zeros_like(halo)
            st[...] = jnp.zeros_like(st)

        xb = x_ref[...]
        taps = _rows_down(xb, halo[...], (3, 2, 1)) + [xb.astype(F32)]
        xc = cb_ref[...]
        for k in range(SK):
            xc = xc + cw_ref[k:k + 1, :] * taps[k]
        xc_ref[...] = xc.astype(BF16)
        tril, expand = _ssd_consts()
        pre = _ssd_pre(xc, dt_ref, dtb_ref, alog_ref, tril, expand)
        lane = lax.broadcasted_iota(jnp.int32, (LS, LANE), 1)
        lo = lane < SP
        zf = z_ref[...].astype(F32)
        siluz = zf * _sigmoid(zf)
        for g in range(SG):
            gs = slice(g * SGW, (g + 1) * SGW)
            bg = pre["bm"][:, g * SN:(g + 1) * SN].astype(BF16)
            cg = pre["cm"][:, g * SN:(g + 1) * SN].astype(BF16)
            gmat = _dot(cg, bg, NT)
            sg = st[g]
            sp_ref[0, g] = sg
            yoff = _dot(cg, sg.astype(BF16))
            parts = []
            for j in range(SGW // LANE):
                h0 = g * (SGW // SP) + 2 * j
                m0 = gmat * _head_decay(pre, tril, h0)
                m1 = gmat * _head_decay(pre, tril, h0 + 1)
                xp = pre["xdt"][:, g * SGW + j * LANE:g * SGW + (j + 1) * LANE]
                rhs = jnp.concatenate([jnp.where(lo, xp, 0.0), jnp.where(lo, 0.0, xp)], axis=0).astype(BF16)
                parts.append(_dot(jnp.concatenate([m0, m1], axis=1).astype(BF16), rhs))
            y = (jnp.concatenate(parts, axis=1) + pre["ea"][:, gs] * yoff + dexp_ref[:, gs] * pre["xs"][:, gs])
            st[g] = pre["eal"][:, gs] * sg + _dot(bg, (pre["w"][:, gs] * pre["xdt"][:, gs]).astype(BF16), TN)
            y_ref[:, gs] = y
            yg = y * siluz[:, gs]
            r = lax.rsqrt(jnp.mean(yg * yg, axis=-1, keepdims=True) + EPS)
            yb_ref[:, gs] = (yg * r * nw_ref[:, gs]).astype(BF16)
        halo[...] = xb[LS - 2 * H:].astype(F32)[H:]

    vec = lambda w: pl.BlockSpec((1, w), lambda c: (0, 0))
    return pl.pallas_call(
        body, grid=(nc,),
        in_specs=[pl.BlockSpec((LS, SI), lambda c: (c, 2)), pl.BlockSpec((LS, SXBC), lambda c: (c, 2)),
                  pl.BlockSpec((LS, LANE), lambda c: (c, 0)),
                  pl.BlockSpec((SK, SXBC), lambda c: (0, 0)), vec(SXBC), vec(LANE), vec(LANE), vec(SI), vec(SI)],
        out_specs=[pl.BlockSpec((LS, SI), lambda c: (c, 0)), pl.BlockSpec((LS, SI), lambda c: (c, 0)),
                   pl.BlockSpec((1, SG, SN, SGW), lambda c: (c, 0, 0, 0)), pl.BlockSpec((LS, SXBC), lambda c: (c, 0))],
        out_shape=[jax.ShapeDtypeStruct((T, SI), BF16), jax.ShapeDtypeStruct((T, SI), F32),
                   jax.ShapeDtypeStruct((nc, SG, SN, SGW), F32), jax.ShapeDtypeStruct((T, SXBC), BF16)],
        scratch_shapes=[pltpu.VMEM((H, SXBC), F32), pltpu.VMEM((SG, SN, SGW), F32)], name="ssd_fwd",
        compiler_params=_params(("arbitrary",), VMEM_CAP_MB))(proj, proj, dtraw, cw, cb, dtb, alog, dexp, nw)


def _ssd_bwd(proj, xcs, dtraw, y, sprev, dyb, dproj, cw, dtb, alog, dexp, nw):
    T = proj.shape[0]
    nc = T // LS
    H = SUBLANE
    NJ = 1

    def body(z_ref, x_ref, xc_ref, dt_ref, y_ref, sp_ref, dyb_ref, dproj_in,
             cw_ref, dtb_ref, alog_ref, dexp_ref, nw_ref,
             dp_ref, ddt_ref, dcw_ref, dcb_ref, ddtb_ref, da_ref, dd_ref, dnw_ref,
             ahead, ds, stage):
        del dproj_in
        i = pl.program_id(0)
        j = pl.program_id(1)

        @pl.when(jnp.logical_and(i == 0, j == 0))
        def _():
            ahead[...] = jnp.zeros_like(ahead)
            ds[...] = jnp.zeros_like(ds)
            for r in (dcw_ref, dcb_ref, ddtb_ref, da_ref, dd_ref, dnw_ref):
                r[...] = jnp.zeros_like(r)

        @pl.when(j == 0)
        def _():
            tril, expand = _ssd_consts()
            pre = _ssd_pre(xc_ref[...].astype(F32), dt_ref, dtb_ref, alog_ref, tril, expand)
            lane = lax.broadcasted_iota(jnp.int32, (LS, LANE), 1)
            sub = lax.broadcasted_iota(jnp.int32, (LANE, LS), 0)
            rowi = lax.broadcasted_iota(jnp.int32, (LS, 1), 0)
            lo = lane < SP
            xs, xdt, ea, w, eal = pre["xs"], pre["xdt"], pre["ea"], pre["w"], pre["eal"]

            zf = z_ref[...].astype(F32)
            sz = _sigmoid(zf)
            siluz = zf * sz
            yv = y_ref[...]
            yg = yv * siluz
            dout = dyb_ref[...].astype(F32)
            dyg_parts, dnw_parts = [], []
            for g in range(SG):
                gs = slice(g * SGW, (g + 1) * SGW)
                ygg = yg[:, gs]
                r = lax.rsqrt(jnp.mean(ygg * ygg, axis=-1, keepdims=True) + EPS)
                yhat = ygg * r
                dn = dout[:, gs] * nw_ref[:, gs]
                dnw_parts.append(jnp.sum(dout[:, gs] * yhat, axis=0, keepdims=True))
                dyg_parts.append(r * (dn - yhat * jnp.mean(dn * yhat, axis=-1, keepdims=True)))
            dyg = jnp.concatenate(dyg_parts, axis=1)
            dnw_ref[...] += jnp.concatenate(dnw_parts, axis=1)
            dy = dyg * siluz
            stage[:, 0:SI] = (dyg * yv * sz * (1.0 + zf * (1.0 - sz))).astype(BF16)
            dd_ref[...] += jnp.sum(dy * xs, axis=0, keepdims=True)
            tt = ea * dy

            da_rows = jnp.zeros((LS, LANE), F32)
            da_cols = jnp.zeros((LANE, LS), F32)
            dxdt_parts, db_parts, dc_parts, daexp_parts = [], [], [], []
            for g in range(SG):
                gs = slice(g * SGW, (g + 1) * SGW)
                bg = pre["bm"][:, g * SN:(g + 1) * SN].astype(BF16)
                cg = pre["cm"][:, g * SN:(g + 1) * SN].astype(BF16)
                sg = sp_ref[0, g]
                sgb = sg.astype(BF16)
                dsg = ds[g]
                dsgb = dsg.astype(BF16)
                ttg = tt[:, gs].astype(BF16)
                yoff = _dot(cg, sgb)
                dc = _dot(ttg, sgb, NT)
                gmat = _dot(cg, bg, NT)
                dgm = jnp.zeros((LS, LS), F32)
                dxdt_pairs = []
                for jj in range(SGW // LANE):
                    h0 = g * (SGW // SP) + 2 * jj
                    ps = slice(g * SGW + jj * LANE, g * SGW + (jj + 1) * LANE)
                    l0 = _head_decay(pre, tril, h0)
                    l1 = _head_decay(pre, tril, h0 + 1)
                    m0 = gmat * l0
                    m1 = gmat * l1
                    dyp = dy[:, ps]
                    dy_lo = jnp.where(lo, dyp, 0.0).astype(BF16)
                    dy_hi = jnp.where(lo, 0.0, dyp).astype(BF16)
                    xpb = xdt[:, ps].astype(BF16)
                    dm0 = _dot(dy_lo, xpb, NT)
                    dm1 = _dot(dy_hi, xpb, NT)
                    q0 = dm0 * m0
                    q1 = dm1 * m1
                    da_rows = da_rows + jnp.where(lane == h0, jnp.sum(q0, axis=1, keepdims=True), 0.0)
                    da_rows = da_rows + jnp.where(lane == h0 + 1, jnp.sum(q1, axis=1, keepdims=True), 0.0)
                    da_cols = da_cols + jnp.where(sub == h0, jnp.sum(q0, axis=0, keepdims=True), 0.0)
                    da_cols = da_cols + jnp.where(sub == h0 + 1, jnp.sum(q1, axis=0, keepdims=True), 0.0)
                    dgm = dgm + dm0 * l0 + dm1 * l1
                    mcat = jnp.concatenate([m0, m1], axis=0).astype(BF16)
                    dycat = jnp.concatenate([dy_lo, dy_hi], axis=0)
                    dxdt_pairs.append(_dot(mcat, dycat, TN))
                dgb = dgm.astype(BF16)
                dc = dc + _dot(dgb, bg)
                db = _dot(dgb, cg, TN)
                zg = _dot(bg, dsgb)
                wg, xdtg = w[:, gs], xdt[:, gs]
                dxdt_g = jnp.concatenate(dxdt_pairs, axis=1) + wg * zg
                qg = zg * xdtg * wg
                last = (jnp.sum(qg, axis=0, keepdims=True)
                        + jnp.sum(dsg * sg, axis=0, keepdims=True) * eal[:, gs])
                daexp_parts.append(dy[:, gs] * ea[:, gs] * yoff - qg + jnp.where(rowi == LS - 1, last, 0.0))
                db = db + _dot((wg * xdtg).astype(BF16), dsgb, NT)
                ds[g] = eal[:, gs] * dsg + _dot(cg, ttg, TN)
                dxdt_parts.append(dxdt_g)
                db_parts.append(db)
                dc_parts.append(dc)
            dxdt = jnp.concatenate(dxdt_parts, axis=1)
            da_exp = jnp.concatenate(daexp_parts, axis=1)
            da = _dot2(da_exp, expand, NT) + da_rows - da_cols.T
            triub = jnp.where(tril, 1.0, 0.0).astype(BF16)
            ddta = _dot3_rhs(triub, da, TN)
            ddt = ddta * pre["a_neg"] + _dot2(dxdt * xs, expand, NT)
            da_ref[...] += jnp.sum(ddta * pre["dt"], axis=0, keepdims=True)
            ddt_raw = ddt * _sigmoid(pre["dtin"])
            ddt_ref[...] = ddt_raw
            ddtb_ref[...] += jnp.sum(ddt_raw, axis=0, keepdims=True)
            dxs = dexp_ref[...] * dy + dxdt * pre["dt_exp"]
            dxbc = jnp.concatenate([dxs] + db_parts + dc_parts, axis=1)
            sx, xc = pre["sx"], pre["xc"]
            dxc = dxbc * sx * (1.0 + xc * (1.0 - sx))
            taps = _rows_up(dxc.astype(BF16), ahead[...], (3, 2, 1)) + [dxc]
            xr = x_ref[...].astype(F32)
            dcb_ref[...] += jnp.sum(dxc, axis=0, keepdims=True)
            dcw_ref[...] += jnp.concatenate([jnp.sum(t * xr, axis=0, keepdims=True) for t in taps], axis=0)
            dxr = cw_ref[0:1, :] * taps[0]
            for k in range(1, SK):
                dxr = dxr + cw_ref[k:k + 1, :] * taps[k]
            stage[:, SI:] = dxr.astype(BF16)
            ahead[...] = dxc[0:H, :]

        dp_ref[...] = stage[...]

    vec = lambda w: pl.BlockSpec((1, w), lambda i, j: (0, 0))
    rev = lambda w, cb_: pl.BlockSpec((LS, w), lambda i, j: (nc - 1 - i, cb_))
    outs = pl.pallas_call(
        body, grid=(nc, NJ),
        in_specs=[rev(SI, 2), rev(SXBC, 2), rev(SXBC, 0),
                  rev(LANE, 0), rev(SI, 0),
                  pl.BlockSpec((1, SG, SN, SGW), lambda i, j: (nc - 1 - i, 0, 0, 0)),
                  rev(SI, 0), pl.BlockSpec(memory_space=pl.ANY),
                  pl.BlockSpec((SK, SXBC), lambda i, j: (0, 0)), vec(LANE), vec(LANE), vec(SI), vec(SI)],
        out_specs=[rev(DP_SSM, 1), rev(LANE, 0),
                   pl.BlockSpec((SK, SXBC), lambda i, j: (0, 0)), vec(SXBC), vec(LANE), vec(LANE), vec(SI), vec(SI)],
        out_shape=[jax.ShapeDtypeStruct(dproj.shape, dproj.dtype), jax.ShapeDtypeStruct((T, LANE), F32),
                   jax.ShapeDtypeStruct((SK, SXBC), F32), jax.ShapeDtypeStruct((1, SXBC), F32),
                   jax.ShapeDtypeStruct((1, LANE), F32), jax.ShapeDtypeStruct((1, LANE), F32),
                   jax.ShapeDtypeStruct((1, SI), F32), jax.ShapeDtypeStruct((1, SI), F32)],
        scratch_shapes=[pltpu.VMEM((H, SXBC), F32),
                        pltpu.VMEM((SG, SN, SGW), F32), pltpu.VMEM((LS, SI + SXBC), BF16)],
        input_output_aliases={7: 0}, name="ssd_bwd",
        compiler_params=_params(("arbitrary", "arbitrary"), VMEM_CAP_MB))(
            proj, proj, xcs, dtraw, y, sprev, dyb, dproj, cw, dtb, alog, dexp, nw)
    return outs


def _perm_ffn_cols(a):
    lead = a.shape[:-1]
    return a.reshape(lead + (2, 2, FT)).swapaxes(-3, -2).reshape(lead + (2 * DFF,))


def _perm_ffn_rows(a):
    return a.reshape((2, 2, FT) + a.shape[1:]).swapaxes(0, 1).reshape(a.shape)


def _pad_lanes(v, n=LANE):
    return jnp.pad(v, ((0, 0), (0, n - v.shape[-1])))


LATE = ["w_proj_a", "w_proj_b", "w_out", "ffn_w_up_t", "ffn_w_down"]


class _NoExchange:
    def gather_start(self):
        return None

    def gather_pass_on(self, outs):
        return None

    def late_weights(self, w, outs):
        return w

    def reduce_late(self, grads):
        return None

    def w_in_to_sibling(self, grad):
        return None

    def reduce_w_in(self, outs):
        return None

    def reduced(self, late_outs, w_in_outs):
        pass


def _local_step(x, tgt, w, hooks=None):
    hooks = hooks or _NoExchange()

    def mm(*args, side=None, **kw):
        out = _matmul(*args, side=side, **kw)
        return out if side is not None else (out, [])

    win_t = w["w_in_t"]
    win_dt = jnp.pad(w["w_in_t"][PMAIN:], ((0, LANE - SH), (0, 0)))
    fcw = _perm_ffn_cols(w["ffn_conv_w"])
    fcb = _perm_ffn_cols(w["ffn_conv_b"][None, :])
    mixw = w["mix_norm_w"][None, :]
    ffnw = w["ffn_norm_w"][None, :]
    finw = w["final_norm_w"][None, :]
    bst = w["gmlp_bs"].T
    scb = w["ssm_conv_b"][None, :]
    dtb = _pad_lanes(w["ssm_dt_bias"][None, :])
    alog = _pad_lanes(w["ssm_a_log"][None, :])
    dexp = jnp.repeat(w["ssm_d"], SP)[None, :]
    snw = w["ssm_norm_w"][None, :]

    xn = _rms_fwd(x, mixw, name="mix_norm")
    proj, got = mm(xn, win_t, name="in_proj", out_dtype=BF16, tb=True, tn=1536, j_outer=True, b_rows=PMAIN,
                   side=hooks.gather_start())
    dtraw, got = mm(xn, win_dt, name="in_proj_dt", out_dtype=F32, tb=True, side=hooks.gather_pass_on(got))
    w = hooks.late_weights(w, got)
    wup = _perm_ffn_rows(w["ffn_w_up_t"])
    ya_pre = _gmlp_fwd(proj, w["gmlp_ln_w"], w["gmlp_ln_b"], w["gmlp_ws"], bst)
    yb_pre, y_ssd, sprev, ssm_xc = _ssd_fwd(proj, dtraw, w["ssm_conv_w"], scb, dtb, alog, dexp, snw)
    merged, ya, yb = _merge_fwd(ya_pre, yb_pre, proj, w["gate_bias"], w["w_proj_a"], w["w_proj_b"])
    h1 = _matmul(merged, w["w_out"], name="out_proj", out_dtype=F32, add=x)
    hn = _rms_fwd(h1, ffnw, name="ffn_norm")
    up = _matmul(hn, wup, name="ffn_up", out_dtype=BF16, tb=True, tn=FT, j_outer=True)
    act, ffn_xc = _ffn_act_fwd(up, fcw, fcb)
    h2 = _matmul(act, w["ffn_w_down"], name="ffn_down", out_dtype=F32, tk=FT, add=h1)

    loss_row, dh2, d_finw = _loss_head(h2, tgt, finw)
    dact = _matmul(dh2, w["ffn_w_down"], name="ffn_down_dx", out_dtype=BF16, tb=True, tn=FT)
    d_wdown = _matmul(act, dh2, name="ffn_down_dw", out_dtype=F32, ta=True, tm=FT)
    dup, d_fcw, d_fcb = _ffn_act_bwd(up, ffn_xc, dact, fcw)
    dhn = _matmul(dup, wup, name="ffn_up_dx", out_dtype=F32, tk=FT)
    d_wup = _matmul(dup, hn, name="ffn_up_dw", out_dtype=F32, ta=True, tm=FT, tk=2048)
    dh1, d_ffnw = _rms_bwd(h1, ffnw, dhn, dh2, name="ffn_norm_bwd")
    dmerged = _matmul(dh1, w["w_out"], name="out_proj_dx", out_dtype=BF16, tb=True)
    d_wout = _matmul(merged, dh1, name="out_proj_dw", out_dtype=F32, ta=True)
    dproj, dya, dyb, dya_pre, dyb_pre, d_gbias = _merge_bwd(dmerged, proj, w["gate_bias"], ya, yb,
                                                           w["w_proj_a"], w["w_proj_b"])
    d_wpa = _matmul(ya_pre, dya, name="proj_a_dw", out_dtype=F32, ta=True, tk=2048)
    d_wpb = _matmul(yb_pre, dyb, name="proj_b_dw", out_dtype=F32, ta=True, tk=2048)
    dproj, d_lnw, d_lnb, d_ws, d_bst = _gmlp_bwd(proj, dya_pre, dproj, w["gmlp_ln_w"], w["gmlp_ln_b"],
                                                 w["gmlp_ws"], bst)
    dproj, ddt, d_scw, d_scb, d_dtb, d_a, d_dch, d_snw = _ssd_bwd(
        proj, ssm_xc, dtraw, y_ssd, sprev, dyb_pre, dproj, w["ssm_conv_w"], dtb, alog, dexp, snw)
    late = {"w_proj_a": d_wpa, "w_proj_b": d_wpb, "w_out": d_wout, "ffn_w_up_t": _perm_ffn_rows(d_wup),
            "ffn_w_down": d_wdown}
    gap = (2 * D + 2 * GW, DP_GAP)
    d_win_main, late_outs = mm(dproj, xn, name="in_proj_dw", out_dtype=F32, ta=True, a_gap=gap, tk=2048,
                               side=hooks.reduce_late(late))
    d_win_dt = _matmul(ddt, xn, name="in_proj_dt_dw", out_dtype=F32, ta=True)
    d_win_t = jnp.concatenate([d_win_main, d_win_dt[:SH]], axis=0)
    dxn, got = mm(ddt, win_dt, name="in_proj_dt_dx", out_dtype=F32, side=hooks.w_in_to_sibling(d_win_t))
    dxn, w_in_outs = mm(dproj, win_t, name="in_proj_dx", out_dtype=F32, add=dxn, b_rows=PMAIN, a_gap=gap,
                        side=hooks.reduce_w_in(got))
    hooks.reduced(late_outs, w_in_outs)
    grad_x, d_mixw = _rms_bwd(x, mixw, dxn, dh1, name="mix_norm_bwd")

    a_neg = -jnp.exp(w["ssm_a_log"])
    grads = {
        "mix_norm_w": d_mixw[0],
        "w_in_t": d_win_t,
        "gate_bias": d_gbias,
        "gmlp_ln_w": d_lnw, "gmlp_ln_b": d_lnb, "gmlp_ws": d_ws, "gmlp_bs": d_bst[:, :GG].T,
        "ssm_conv_w": d_scw, "ssm_conv_b": d_scb[0],
        "ssm_dt_bias": d_dtb[0, :SH], "ssm_a_log": d_a[0, :SH] * a_neg,
        "ssm_d": d_dch.reshape(SH, SP).sum(axis=-1), "ssm_norm_w": d_snw[0],
        **late,
        "ffn_norm_w": d_ffnw[0],
        "ffn_conv_w": _perm_ffn_cols(d_fcw), "ffn_conv_b": _perm_ffn_cols(d_fcb)[0],
        "ffn_w_down": d_wdown, "final_norm_w": d_finw[0],
    }
    return loss_row, grad_x, grads


MESH = pl.DeviceIdType.MESH
HBM_SPEC = pl.BlockSpec(memory_space=pltpu.HBM)


def _axes():
    return lax.axis_index("x"), lax.axis_index("y"), lax.axis_index("c")


def _all_gather(shards, *, name):
    na = len(shards)

    def body(*refs):
        x_refs, out_refs = refs[:na], refs[na:2 * na]
        send_sems, recv_sems, local_sems = refs[2 * na:]
        x, y, c = _axes()
        me, sibling = (x, y, c), (x, y, 1 - c)
        chips = [(1 - x, y), (x, 1 - y), (1 - x, 1 - y)]

        def slot(a, px, py, pc):
            return out_refs[a].at[4 * px + 2 * py + pc]

        def copy(a, k, block, to, src=None):
            return pltpu.make_async_remote_copy(
                src_ref=slot(a, *block) if src is None else src, dst_ref=slot(a, *block),
                send_sem=send_sems.at[7 * a + k], recv_sem=recv_sems.at[7 * a + k], device_id=to, device_id_type=MESH)

        mine = [pltpu.make_async_copy(x_refs[a], slot(a, *me), local_sems.at[a]) for a in range(na)]
        for cp in mine:
            cp.start()
        first = []
        for a in range(na):
            first.append(copy(a, 0, me, sibling, src=x_refs[a]))
            first += [copy(a, 1 + j, me, (*chip, c), src=x_refs[a]) for j, chip in enumerate(chips)]
        for cp in first:
            cp.start()
        passed = []
        for j, chip in enumerate(chips):
            for a in range(na):
                copy(a, 1 + j, (*chip, c), me).wait_recv()
                cp = copy(a, 4 + j, (*chip, c), sibling)
                cp.start()
                passed.append(cp)
        for a in range(na):
            copy(a, 0, sibling, me).wait_recv()
        for j, chip in enumerate(chips):
            for a in range(na):
                copy(a, 4 + j, (*chip, 1 - c), me).wait_recv()
        for cp in first + passed:
            cp.wait_send()
        for cp in mine:
            cp.wait()

    return pl.pallas_call(
        body, out_shape=[jax.ShapeDtypeStruct((NDEV,) + s.shape, s.dtype) for s in shards],
        in_specs=[HBM_SPEC] * na, out_specs=[HBM_SPEC] * na,
        scratch_shapes=[pltpu.SemaphoreType.DMA((7 * na,)), pltpu.SemaphoreType.DMA((7 * na,)),
                        pltpu.SemaphoreType.DMA((na,))],
        name=name)(*shards)


def _exchange(srcs, plan, *, name):
    na = len(srcs)
    n = len(plan(0, 0, 0))

    def body(*refs):
        src_refs, out_refs = refs[:na], refs[na:2 * na]
        send_sems, recv_sems = refs[2 * na:]
        x, y, c = _axes()
        copies = []
        for k, (slab, peer) in enumerate(plan(x, y, c)):
            for a in range(na):
                cp = pltpu.make_async_remote_copy(
                    src_ref=src_refs[a].at[slab], dst_ref=out_refs[a].at[k], send_sem=send_sems.at[n * a + k],
                    recv_sem=recv_sems.at[n * a + k], device_id=peer, device_id_type=MESH)
                cp.start()
                copies.append(cp)
        for cp in copies:
            cp.wait()

    return pl.pallas_call(
        body, out_shape=[jax.ShapeDtypeStruct((n,) + s.shape[1:], s.dtype) for s in srcs],
        in_specs=[HBM_SPEC] * na, out_specs=[HBM_SPEC] * na,
        scratch_shapes=[pltpu.SemaphoreType.DMA((n * na,)), pltpu.SemaphoreType.DMA((n * na,))], name=name)(*srcs)


def _to_sibling_plan(x, y, c):
    return [(2 * q + (1 - c), (x, y, 1 - c)) for q in range(4)]


def _to_chips_plan(x, y, c):
    q = 2 * x + y
    return [(q ^ 2, (1 - x, y, c)), (q ^ 1, (x, 1 - y, c)), (q ^ 3, (1 - x, 1 - y, c))]


def _row_tile(rows, row_bytes, budget=2 * 2 ** 20, align=2 * SUBLANE):
    if rows * row_bytes <= 2 * budget:
        return rows
    best = None
    for d in range(align, rows + 1, align):
        if rows % d == 0 and d * row_bytes <= budget:
            best = d
    return best or rows


def _pair_add(g, ra, c_idx, *, name):
    _, _, R, C = g.shape
    tr = _row_tile(R, C * 4, budget=3 * 2 ** 20)

    def body(c_ref, g_ref, ra_ref, o_ref):
        del c_ref
        o_ref[...] = (g_ref[0].astype(F32) + ra_ref[...].astype(F32)).astype(o_ref.dtype)

    return pl.pallas_call(
        body,
        grid_spec=pltpu.PrefetchScalarGridSpec(
            num_scalar_prefetch=1, grid=(4, R // tr),
            in_specs=[pl.BlockSpec((1, 1, tr, C), lambda q, r, cr: (q, cr[0], r, 0)),
                      pl.BlockSpec((1, tr, C), lambda q, r, cr: (q, r, 0))],
            out_specs=pl.BlockSpec((1, tr, C), lambda q, r, cr: (q, r, 0))),
        out_shape=jax.ShapeDtypeStruct((4, R, C), g.dtype), name=name,
        compiler_params=_params(("arbitrary", "arbitrary"), 24))(c_idx, g, ra)


def _grad_sum(p, rb, q_idx, *, name):
    _, R, C = p.shape
    tr = _row_tile(R, C * 4, budget=3 * 2 ** 20)

    def body(q_ref, p_ref, rb_ref, o_ref):
        del q_ref
        g = p_ref[0].astype(F32)
        for k in range(3):
            g = g + rb_ref[k].astype(F32)
        o_ref[...] = g

    return pl.pallas_call(
        body,
        grid_spec=pltpu.PrefetchScalarGridSpec(
            num_scalar_prefetch=1, grid=(R // tr,),
            in_specs=[pl.BlockSpec((1, tr, C), lambda r, qr: (qr[0], r, 0)),
                      pl.BlockSpec((3, tr, C), lambda r, qr: (0, r, 0))],
            out_specs=pl.BlockSpec((tr, C), lambda r, qr: (r, 0))),
        out_shape=jax.ShapeDtypeStruct((R, C), F32), name=name,
        compiler_params=_params(("arbitrary",), 40))(q_idx, p, rb)


def _adamw(g, w, m, v):
    m = ADAM_B1 * m + (1.0 - ADAM_B1) * g
    v = ADAM_B2 * v + (1.0 - ADAM_B2) * (g * g)
    m_hat = m / (1.0 - ADAM_B1 ** ADAM_STEP)
    v_hat = v / (1.0 - ADAM_B2 ** ADAM_STEP)
    delta = -ADAM_LR * (m_hat / (jnp.sqrt(v_hat) + ADAM_EPS) + ADAM_WD * w)
    return delta, m, v


def _adam(g, w, m, v, *, name):
    _, R, C = w.shape
    tr = _row_tile(R, C * 4, budget=2 ** 20, align=SUBLANE)

    def body(g_ref, w_ref, m_ref, v_ref, d_out, m_out, v_out):
        delta, mn, vn = _adamw(g_ref[...], w_ref[...], m_ref[...], v_ref[...])
        d_out[...] = delta
        m_out[...] = mn
        v_out[...] = vn

    row = pl.BlockSpec((1, tr, C), lambda r: (0, r, 0))
    o = jax.ShapeDtypeStruct((1, R, C), F32)
    return pl.pallas_call(
        body, grid=(R // tr,), in_specs=[row, row, row, row], out_specs=[row, row, row], out_shape=[o, o, o],
        name=name, compiler_params=_params(("arbitrary",), 32))(g, w, m, v)


def _vmem_specs(n):
    return [pl.BlockSpec(memory_space=pltpu.VMEM)] * n


def _pair_sum_many(mine, theirs, *, name):
    n = len(mine)

    def body(*refs):
        for a in range(n):
            refs[2 * n + a][...] = refs[a][...] + refs[n + a][0]

    return pl.pallas_call(
        body, out_shape=[jax.ShapeDtypeStruct(m.shape, m.dtype) for m in mine], in_specs=_vmem_specs(2 * n),
        out_specs=_vmem_specs(n), name=name)(*mine, *theirs)


def _chip_sum_many(own, recv, q_idx, *, name):
    n = len(own)

    def body(q_ref, *refs):
        q = q_ref[0]
        for a in range(n):
            mine, r = refs[a][...], refs[n + a]
            total = None
            for chip in range(4):
                e = q ^ chip
                term = jnp.where(e == 0, mine, jnp.where(e == 2, r[0], jnp.where(e == 1, r[1], r[2])))
                total = term if total is None else total + term
            refs[2 * n + a][...] = total

    return pl.pallas_call(
        body, out_shape=[jax.ShapeDtypeStruct(m.shape, m.dtype) for m in own],
        in_specs=[pl.BlockSpec(memory_space=pltpu.SMEM)] + _vmem_specs(2 * n), out_specs=_vmem_specs(n),
        name=name)(q_idx, *own, *recv)


def _adam_many(gs, ws, ms, vs, *, name):
    n = len(gs)

    def body(*refs):
        for a in range(n):
            delta, mn, vn = _adamw(*(refs[k * n + a][...] for k in range(4)))
            refs[4 * n + a][...] = delta
            refs[5 * n + a][...] = mn
            refs[6 * n + a][...] = vn

    shapes = [jax.ShapeDtypeStruct(w.shape, w.dtype) for w in ws]
    out = pl.pallas_call(body, out_shape=shapes * 3, in_specs=_vmem_specs(4 * n), out_specs=_vmem_specs(3 * n),
                         name=name)(*gs, *ws, *ms, *vs)
    return out[:n], out[n:2 * n], out[2 * n:]


WEIGHTS = ["mix_norm_w", "w_in", "gate_bias", "gmlp_ln_w", "gmlp_ln_b", "gmlp_ws", "gmlp_bs", "ssm_conv_w",
           "ssm_conv_b", "ssm_dt_bias", "ssm_a_log", "ssm_d", "ssm_norm_w", "w_proj_a", "w_proj_b", "w_out",
           "ffn_norm_w", "ffn_w_up", "ffn_conv_w", "ffn_conv_b", "ffn_w_down", "final_norm_w"]
SHARDED = {"w_in": ((D, IN_COLS), 1), "gate_bias": ((2, D), 1), "ssm_conv_w": ((SK, SXBC), 1),
           "w_proj_a": ((GW, D), 0), "w_proj_b": ((SI, D), 0), "w_out": ((D, D), 0),
           "ffn_w_up": ((D, 2 * DFF), 1), "ffn_conv_w": ((FK, 2 * DFF), 1), "ffn_w_down": ((DFF, D), 0)}
REPLICATED = {"mix_norm_w": (D,), "gmlp_ln_w": (GG, GD), "gmlp_ln_b": (GG, GD), "gmlp_ws": (GG, GB, GB),
              "gmlp_bs": (GG, GB), "ssm_conv_b": (SXBC,), "ssm_dt_bias": (SH,), "ssm_a_log": (SH,), "ssm_d": (SH,),
              "ssm_norm_w": (SI,), "ffn_norm_w": (D,), "ffn_conv_b": (2 * DFF,), "final_norm_w": (D,)}
REPL_ORDER = [n for n in WEIGHTS if n in REPLICATED]
BTILE = 2 * SUBLANE
WIN_R = IN_COLS // NDEV
WIN_P = WIN_R + BTILE - WIN_R % BTILE
WIN_A = [WIN_R * d // BTILE * BTILE for d in range(NDEV)]
assert all(WIN_A[d] + WIN_P >= WIN_R * (d + 1) for d in range(NDEV)) and WIN_A[-1] + WIN_P == IN_COLS
BIG = [("w_proj_a", GW // NDEV, False), ("w_proj_b", SI // NDEV, False), ("w_out", D // NDEV, False),
       ("ffn_w_up", 2 * DFF // NDEV, True), ("ffn_w_down", DFF // NDEV, False), ("w_in", WIN_P, True)]
VECTORS = ["gate_bias", "ssm_conv_w", "ffn_conv_w"]


def _round_up(n, k):
    return (n + k - 1) // k * k


BIG_OFF = {}
_off = 0
for _n, _r, _t in BIG:
    BIG_OFF[_n] = _off
    _off += _r
BIG_USED = _off
BIG_ROWS = _round_up(BIG_USED, 2 * SUBLANE)
assert all(BIG_OFF[n] % (2 * SUBLANE) == 0 for n, _, _ in BIG)
VEC_SHAPE = {n: (SHARDED[n][0][0], SHARDED[n][0][1] // NDEV) for n in VECTORS}


def _win_offset(dev):
    return WIN_R * dev - WIN_R * dev // BTILE * BTILE


def _pack_big(arrs, dtype, dev):
    parts = []
    for n, r, t in BIG:
        a = (arrs[n].T if t else arrs[n]).astype(dtype)
        if n == "w_in":
            a = lax.dynamic_update_slice(jnp.zeros((WIN_P, D), dtype), a, (_win_offset(dev), 0))
        parts.append(a)
    parts.append(jnp.zeros((BIG_ROWS - BIG_USED, D), dtype))
    return jnp.concatenate(parts, axis=0)


def _join_windows(win):
    parts = []
    for d in range(NDEV):
        lo = BTILE if WIN_A[d] % WIN_R else 0
        if lo:
            parts.append(win[d - 1, WIN_P - BTILE:] + win[d, :BTILE])
        hi = WIN_P - BTILE if d + 1 < NDEV and WIN_A[d + 1] < WIN_A[d] + WIN_P else WIN_P
        parts.append(win[d, lo:hi])
    return jnp.concatenate(parts, axis=0)


def _split_windows(g):
    return jnp.stack([g[a:a + WIN_P] for a in WIN_A])


LATE_ROWS = BIG_OFF["w_in"]
assert LATE_ROWS + WIN_P == BIG_ROWS and BIG[-1][0] == "w_in"


def _remote(src, dst, send_sems, recv_sems, k, to):
    return pltpu.make_async_remote_copy(src_ref=src, dst_ref=dst, send_sem=send_sems.at[k], recv_sem=recv_sems.at[k],
                                        device_id=to, device_id_type=MESH)


class _Exchange:
    def __init__(self, late_shard, c_idx):
        self.late_shard, self.c_idx = late_shard, c_idx

    def gather_start(self):
        shard = self.late_shard

        def make(ins, outs, send_sems, recv_sems):
            (x_ref,), (out,) = ins, outs
            x, y, c = _axes()
            mine = out.at[4 * x + 2 * y + c]
            peers = [(x, y, 1 - c), (1 - x, y, c), (x, 1 - y, c), (1 - x, 1 - y, c)]
            copies = [_remote(x_ref, mine, send_sems, recv_sems, k, p) for k, p in enumerate(peers)]
            return copies + [pltpu.make_async_copy(x_ref, mine, send_sems.at[len(peers)])]

        return _Side([shard], [jax.ShapeDtypeStruct((NDEV,) + shard.shape, shard.dtype)], 5, make)

    def gather_pass_on(self, outs):
        (buf,) = outs

        def make(ins, outs, send_sems, recv_sems):
            (src,), (dst,) = ins, outs
            x, y, c = _axes()
            slots = [4 * px + 2 * py + c for px, py in [(1 - x, y), (x, 1 - y), (1 - x, 1 - y)]]
            return [_remote(src.at[s], dst.at[s], send_sems, recv_sems, k, (x, y, 1 - c)) for k, s in enumerate(slots)]

        return _Side([buf], [jax.ShapeDtypeStruct(buf.shape, buf.dtype)], 3, make, aliases=[(0, 0)])

    def late_weights(self, w, outs):
        (buf,) = outs
        w = dict(w)
        for n, r, t in BIG[:-1]:
            w[n + "_t" if t else n] = buf[:, BIG_OFF[n]:BIG_OFF[n] + r].reshape(NDEV * r, D)
        return w

    @staticmethod
    def _plan_side(src, plan):
        n = len(plan(0, 0, 0))

        def make(ins, outs, send_sems, recv_sems):
            (s,), (dst,) = ins, outs
            return [_remote(s.at[slab], dst.at[k], send_sems, recv_sems, k, peer)
                    for k, (slab, peer) in enumerate(plan(*_axes()))]

        return _Side([src], [jax.ShapeDtypeStruct((n,) + src.shape[1:], src.dtype)], n, make)

    def _to_chips(self, send, sib, tag):
        sums = _pair_add(send.reshape((4, 2) + send.shape[1:]), sib, self.c_idx, name=tag + "_grad_pair_add")
        return sums, self._plan_side(sums, _to_chips_plan)

    def reduce_late(self, grads):
        send = jnp.concatenate([grads[n + "_t" if t else n].reshape(NDEV, r, D) for n, r, t in BIG[:-1]], axis=1)
        send = send.astype(BF16)
        (sib,) = _exchange([send], _to_sibling_plan, name="late_grads_to_sibling")
        self.late_sum, side = self._to_chips(send, sib, "late")
        return side

    def w_in_to_sibling(self, grad):
        self.w_in_send = _split_windows(grad).astype(BF16)
        return self._plan_side(self.w_in_send, _to_sibling_plan)

    def reduce_w_in(self, outs):
        self.w_in_sum, side = self._to_chips(self.w_in_send, outs[0], "w_in")
        return side

    def reduced(self, late_outs, w_in_outs):
        (self.late_from_chips,), (self.w_in_from_chips,) = late_outs, w_in_outs


def kernel(x, mix_norm_w, w_in, gate_bias, gmlp_ln_w, gmlp_ln_b, gmlp_ws, gmlp_bs, ssm_conv_w, ssm_conv_b, ssm_dt_bias, ssm_a_log, ssm_d, ssm_norm_w, w_proj_a, w_proj_b, w_out, ffn_norm_w, ffn_w_up, ffn_conv_w, ffn_conv_b, ffn_w_down, final_norm_w, loss_target, m_mix_norm_w, m_w_in, m_gate_bias, m_gmlp_ln_w, m_gmlp_ln_b, m_gmlp_ws, m_gmlp_bs, m_ssm_conv_w, m_ssm_conv_b, m_ssm_dt_bias, m_ssm_a_log, m_ssm_d, m_ssm_norm_w, m_w_proj_a, m_w_proj_b, m_w_out, m_ffn_norm_w, m_ffn_w_up, m_ffn_conv_w, m_ffn_conv_b, m_ffn_w_down, m_final_norm_w, v_mix_norm_w, v_w_in, v_gate_bias, v_gmlp_ln_w, v_gmlp_ln_b, v_gmlp_ws, v_gmlp_bs, v_ssm_conv_w, v_ssm_conv_b, v_ssm_dt_bias, v_ssm_a_log, v_ssm_d, v_ssm_norm_w, v_w_proj_a, v_w_proj_b, v_w_out, v_ffn_norm_w, v_ffn_w_up, v_ffn_conv_w, v_ffn_conv_b, v_ffn_w_down, v_final_norm_w):
    given = dict(locals())
    wts = {n: given[n] for n in WEIGHTS}
    mom = {n: given["m_" + n] for n in WEIGHTS}
    var = {n: given["v_" + n] for n in WEIGHTS}
    xi, yi, ci = _axes()
    c_idx = jnp.reshape(ci, (1,)).astype(jnp.int32)
    q_idx = jnp.reshape(2 * xi + yi, (1,)).astype(jnp.int32)
    big_names = [n for n, _, _ in BIG]
    drop = lambda d, names: {n: d[n][0] for n in names}

    dev = 4 * xi + 2 * yi + ci
    packed = _pack_big(drop(wts, big_names), BF16, dev)
    gathered = _all_gather([packed[LATE_ROWS:]] + [wts[n] for n in VECTORS], name="w_in_all_gather")
    full = {"w_in_t": _join_windows(gathered[0])}
    for n, a in zip(VECTORS, gathered[1:]):
        r, c = VEC_SHAPE[n]
        full[n] = a[:, 0].transpose(1, 0, 2).reshape(r, NDEV * c)
    for n in REPL_ORDER:
        full[n] = wts[n].reshape(REPLICATED[n])

    hooks = _Exchange(packed[:LATE_ROWS], c_idx)
    loss_local, grad_x, grads = _local_step(x[0], loss_target[0], full, hooks)
    g_late = _grad_sum(hooks.late_sum, hooks.late_from_chips, q_idx, name="late_grad_sum")
    g_win = _grad_sum(hooks.w_in_sum, hooks.w_in_from_chips, q_idx, name="w_in_grad_sum")

    small = VECTORS + REPL_ORDER
    as_2d = lambda a: a if a.ndim >= 2 else a[None]
    part = [grads[n].reshape((1,) + SHARDED[n][0] if n in VECTORS else as_2d(wts[n]).shape) for n in small]
    part.append(loss_local)
    from_sibling = _exchange([p[None] for p in part], lambda x, y, c: [(0, (x, y, 1 - c))],
                             name="small_grads_to_sibling")
    chip_sums = _pair_sum_many(part, from_sibling, name="small_grad_pair_sum")
    from_chips = _exchange([s[None] for s in chip_sums],
                           lambda x, y, c: [(0, (1 - x, y, c)), (0, (x, 1 - y, c)), (0, (1 - x, 1 - y, c))],
                           name="small_grads_to_chips")
    totals = _chip_sum_many(chip_sums, from_chips, q_idx, name="small_grad_chip_sum")
    g_small, loss = dict(zip(small, totals)), totals[-1][0, 0]
    for n in VECTORS:
        c = VEC_SHAPE[n][1]
        g_small[n] = lax.dynamic_slice_in_dim(g_small[n], dev * c, c, axis=2)

    outs = {}
    small_g = [g_small[n] for n in small]
    small_out = _adam_many(small_g, *[[as_2d(d[n]) for n in small] for d in (wts, mom, var)], name="adam_small")
    for i, n in enumerate(small):
        outs[n] = tuple(a[i].reshape(wts[n].shape) for a in (small_g,) + tuple(small_out))
    for n, r, t in BIG:
        if n == "w_in":
            g = lax.dynamic_slice(g_win, (_win_offset(dev), 0), (WIN_R, D))
        else:
            g = g_late[BIG_OFF[n]:BIG_OFF[n] + r]
        flip = (lambda a: a.transpose(0, 2, 1)) if t else (lambda a: a)
        g = g[None]
        new = _adam(g, flip(wts[n]), flip(mom[n]), flip(var[n]), name="adam_" + n)
        outs[n] = tuple(flip(a) for a in (g,) + tuple(new))
    return (loss, grad_x[None]) + tuple(outs[n][k] for k in range(4) for n in WEIGHTS)
```

```python
import functools

import jax
import jax.numpy as jnp
from jax import lax
from jax.experimental import pallas as pl
from jax.experimental.pallas import tpu as pltpu

F32 = jnp.float32
BF16 = jnp.bfloat16

D = 1024
EPS = 1e-5
GW = 1024
GB = 128
GG = 8
GD = 128
GCH = 64
SI = 2048
SH = 32
SP = 64
SG = 4
SN = 128
SGW = SI // SG
SK = 4
SXBC = SI + 2 * SG * SN
DFF = 2816
FK = 3
PMAIN = 2 * D + 2 * GW + SI + SXBC
IN_COLS = PMAIN + SH
DP_SSM = SI + SXBC
DP_GAP = (DP_SSM - (2 * D + 2 * GW) % DP_SSM) % DP_SSM
DP_COLS = 2 * D + 2 * GW + DP_GAP + DP_SSM
assert DP_GAP % D == 0 and (2 * D + 2 * GW) % D == 0
NDEV = 8
ADAM_LR, ADAM_B1, ADAM_B2, ADAM_EPS, ADAM_WD, ADAM_STEP = 0.001, 0.9, 0.999, 1e-08, 0.01, 10

LANE = 128
SUBLANE = 8
VMEM_MB_V7X = 64
VMEM_CAP_MB = VMEM_MB_V7X - 8

LS = 128
FT = DFF // 2

NN = (((1,), (0,)), ((), ()))
NT = (((1,), (1,)), ((), ()))
TN = (((0,), (0,)), ((), ()))


def _params(sem, vmem_mb):
    return pltpu.CompilerParams(dimension_semantics=sem,
                                vmem_limit_bytes=min(int(vmem_mb), VMEM_CAP_MB) * 1024 * 1024)


def _dot(a, b, dims=NN):
    return lax.dot_general(a, b, dims, preferred_element_type=F32)


def _sigmoid(x):
    return 1.0 / (1.0 + jnp.exp(-x))


def _split3(v):
    hi = v.astype(BF16)
    r = v - hi.astype(F32)
    mid = r.astype(BF16)
    lo = (r - mid.astype(F32)).astype(BF16)
    return hi, mid, lo


def _dot3(a_f32, b_bf16, dims):
    hi, mid, lo = _split3(a_f32)
    return _dot(hi, b_bf16, dims) + _dot(mid, b_bf16, dims) + _dot(lo, b_bf16, dims)


def _dot2(a_f32, b_bf16, dims):
    hi, mid, _ = _split3(a_f32)
    return _dot(hi, b_bf16, dims) + _dot(mid, b_bf16, dims)


def _dot3_rhs(a_bf16, b_f32, dims):
    hi, mid, lo = _split3(b_f32)
    return _dot(a_bf16, hi, dims) + _dot(a_bf16, mid, dims) + _dot(a_bf16, lo, dims)


def _matmul(a, b, *, name, out_dtype, ta=False, tb=False, tm=1024, tn=1024, tk=1024, add=None,
            j_outer=False, b_rows=None, a_gap=None, side=None):
    gap0, gapw = a_gap or (0, 0)
    if ta:
        K, M = a.shape
        M -= gapw
    else:
        M, K = a.shape
        K -= gapw
    if tb:
        N, K2 = b.shape
        N = b_rows or N
    else:
        K2, N = b.shape
        K2 = b_rows or K2
    assert K == K2, (a.shape, b.shape, ta, tb)
    tm, tn, tk = min(tm, M), min(tn, N), min(tk, K)
    assert M % tm == 0 and N % tn == 0 and K % tk == 0, (M, N, K, tm, tn, tk)
    nk = K // tk
    dims = (((0 if ta else 1,), (1 if tb else 0,)), ((), ()))
    has_add = add is not None
    n_in = 3 if has_add else 2
    s_in = len(side.inputs) if side else 0
    s_out = len(side.out_shapes) if side else 0
    grid = (N // tn, M // tm, nk) if j_outer else (M // tm, N // tn, nk)

    def body(*refs):
        a_ref, b_ref = refs[:2]
        add_ref = refs[2] if has_add else None
        o_ref = refs[n_in + s_in]
        if side:
            side_refs = (refs[n_in:n_in + s_in], refs[n_in + s_in + 1:n_in + s_in + 1 + s_out]) + tuple(refs[-2:])
            ids = [pl.program_id(d) for d in range(3)]
            first = functools.reduce(jnp.logical_and, [i == 0 for i in ids])
            last = functools.reduce(jnp.logical_and, [i == g - 1 for i, g in zip(ids, grid)])

            @pl.when(first)
            def _():
                for cp in side.make(*side_refs):
                    cp.start()

            @pl.when(last)
            def _():
                for cp in side.make(*side_refs):
                    cp.wait()

        p = lax.dot_general(a_ref[...].astype(BF16), b_ref[...].astype(BF16), dims,
                            preferred_element_type=F32)

        def finish(acc):
            if has_add:
                acc = acc + add_ref[...].astype(F32)
            o_ref[...] = acc.astype(o_ref.dtype)

        if nk == 1:
            finish(p)
        else:
            acc_ref = refs[n_in + s_in + 1 + s_out]
            k = pl.program_id(2)

            @pl.when(k == 0)
            def _():
                acc_ref[...] = p

            @pl.when(jnp.logical_and(k > 0, k < nk - 1))
            def _():
                acc_ref[...] += p

            @pl.when(k == nk - 1)
            def _():
                finish(acc_ref[...] + p)

    if j_outer:
        ij = lambda g0, g1: (g1, g0)
    else:
        ij = lambda g0, g1: (g0, g1)

    ta_col = tm if ta else tk
    assert gap0 % ta_col == 0 and gapw % ta_col == 0, (a_gap, ta_col)

    def a_map(g0, g1, k):
        i, _ = ij(g0, g1)
        col = i if ta else k
        col = col + jnp.where(col >= gap0 // ta_col, gapw // ta_col, 0) if gapw else col
        return (k, col) if ta else (i, col)

    def b_map(g0, g1, k):
        _, j = ij(g0, g1)
        return (j, k) if tb else (k, j)

    def o_map(g0, g1, k):
        return ij(g0, g1)

    in_specs = [pl.BlockSpec((tk, tm) if ta else (tm, tk), a_map),
                pl.BlockSpec((tn, tk) if tb else (tk, tn), b_map)]
    args = [a, b]
    if has_add:
        in_specs.append(pl.BlockSpec((tm, tn), o_map))
        args.append(add)
    scratch = [pltpu.VMEM((tm, tn), F32)] if nk > 1 else []
    osz = jnp.dtype(out_dtype).itemsize
    est = (2 * (tm * tk * a.dtype.itemsize + tk * tn * b.dtype.itemsize) + 2 * tm * tn * osz
           + (2 * tm * tn * add.dtype.itemsize if has_add else 0)
           + 3 * tm * tn * 4 + 2 * (tm * tk + tk * tn)) / 2 ** 20 + 4
    out_specs = [pl.BlockSpec((tm, tn), o_map)]
    out_shape = [jax.ShapeDtypeStruct((M, N), out_dtype)]
    aliases = {}
    if side:
        hbm = pl.BlockSpec(memory_space=pltpu.HBM)
        in_specs += [hbm] * s_in
        args += list(side.inputs)
        out_specs += [hbm] * s_out
        out_shape += list(side.out_shapes)
        scratch += [pltpu.SemaphoreType.DMA((side.nsem,)), pltpu.SemaphoreType.DMA((side.nsem,))]
        aliases = {n_in + i: 1 + j for i, j in side.aliases}
    outs = pl.pallas_call(
        body, grid=grid, in_specs=in_specs, out_specs=out_specs, out_shape=out_shape, scratch_shapes=scratch,
        input_output_aliases=aliases, name=name,
        compiler_params=_params(("arbitrary", "arbitrary", "arbitrary"), est))(*args)
    return (outs[0], list(outs[1:])) if side else outs[0]


class _Side:
    def __init__(self, inputs, out_shapes, nsem, make, aliases=()):
        self.inputs, self.out_shapes, self.nsem, self.make, self.aliases = inputs, out_shapes, nsem, make, aliases


def _rms_fwd(x, w, *, name):
    T = x.shape[0]
    tm = min(512, T)

    def body(x_ref, w_ref, o_ref):
        xv = x_ref[...]
        r = lax.rsqrt(jnp.mean(xv * xv, axis=-1, keepdims=True) + EPS)
        o_ref[...] = (xv * r * w_ref[...]).astype(BF16)

    return pl.pallas_call(
        body, grid=(T // tm,),
        in_specs=[pl.BlockSpec((tm, D), lambda i: (i, 0)), pl.BlockSpec((1, D), lambda i: (0, 0))],
        out_specs=pl.BlockSpec((tm, D), lambda i: (i, 0)),
        out_shape=jax.ShapeDtypeStruct((T, D), BF16), name=name,
        compiler_params=_params(("arbitrary",), 24))(x, w)


def _rms_bwd(x, w, dy, dres, *, name):
    T = x.shape[0]
    tm = min(512, T)

    def body(x_ref, w_ref, dy_ref, dres_ref, dx_ref, dw_ref):
        xv = x_ref[...]
        r = lax.rsqrt(jnp.mean(xv * xv, axis=-1, keepdims=True) + EPS)
        xhat = xv * r
        dyv = dy_ref[...].astype(F32)
        g = dyv * w_ref[...]
        dx_ref[...] = dres_ref[...] + r * (g - xhat * jnp.mean(g * xhat, axis=-1, keepdims=True))
        part = jnp.sum(dyv * xhat, axis=0, keepdims=True)

        @pl.when(pl.program_id(0) == 0)
        def _():
            dw_ref[...] = part

        @pl.when(pl.program_id(0) > 0)
        def _():
            dw_ref[...] += part

    row = pl.BlockSpec((tm, D), lambda i: (i, 0))
    vec = pl.BlockSpec((1, D), lambda i: (0, 0))
    return pl.pallas_call(
        body, grid=(T // tm,), in_specs=[row, vec, row, row], out_specs=[row, vec],
        out_shape=[jax.ShapeDtypeStruct((T, D), F32), jax.ShapeDtypeStruct((1, D), F32)], name=name,
        compiler_params=_params(("arbitrary",), 32))(x, w, dy, dres)


def _loss_head(h, tgt, w):
    T = h.shape[0]
    tm = min(512, T)

    def body(h_ref, t_ref, w_ref, loss_ref, dh_ref, dw_ref):
        hv = h_ref[...]
        r = lax.rsqrt(jnp.mean(hv * hv, axis=-1, keepdims=True) + EPS)
        xhat = hv * r
        wv = w_ref[...]
        err = xhat * wv - t_ref[...]
        lpart = 0.5 * jnp.sum(jnp.mean(err * err, axis=-1, keepdims=True), axis=0, keepdims=True)
        dy = err * (1.0 / D)
        g = dy * wv
        dh_ref[...] = r * (g - xhat * jnp.mean(g * xhat, axis=-1, keepdims=True))
        wpart = jnp.sum(dy * xhat, axis=0, keepdims=True)
        lrow = jnp.broadcast_to(lpart, (1, LANE))

        @pl.when(pl.program_id(0) == 0)
        def _():
            dw_ref[...] = wpart
            loss_ref[...] = lrow

        @pl.when(pl.program_id(0) > 0)
        def _():
            dw_ref[...] += wpart
            loss_ref[...] += lrow

    row = pl.BlockSpec((tm, D), lambda i: (i, 0))
    vec = pl.BlockSpec((1, D), lambda i: (0, 0))
    return pl.pallas_call(
        body, grid=(T // tm,), in_specs=[row, row, vec],
        out_specs=[pl.BlockSpec((1, LANE), lambda i: (0, 0)), row, vec],
        out_shape=[jax.ShapeDtypeStruct((1, LANE), F32), jax.ShapeDtypeStruct((T, D), F32),
                   jax.ShapeDtypeStruct((1, D), F32)], name="loss_head",
        compiler_params=_params(("arbitrary",), 32))(h, tgt, w)


_GELU_C = 0.7978845608028654
_GELU_A = 0.044715


def _gelu(x, with_grad=False):
    x2 = x * x
    cx = _GELU_C * x
    t = jnp.tanh(cx * (1.0 + _GELU_A * x2))
    h = 0.5 * (1.0 + t)
    if not with_grad:
        return x * h
    return x * h, h + 0.5 * cx * (1.0 - t * t) * (1.0 + 3.0 * _GELU_A * x2)


def _gmlp_mask():
    r = lax.broadcasted_iota(jnp.int32, (GB, GB), 0) // GCH
    c = lax.broadcasted_iota(jnp.int32, (GB, GB), 1) // GCH
    return c <= r


def _gmlp_fwd(proj, lnw, lnb, ws, bst):
    T = proj.shape[0]
    tm = min(512, T)
    nblk = tm // GB

    def body(u_ref, v_ref, lnw_ref, lnb_ref, ws_ref, bst_ref, o_ref):
        mask = _gmlp_mask()
        u = _gelu(u_ref[...].astype(F32))
        v = _gelu(v_ref[...].astype(F32))
        for g in range(GG):
            cs = slice(g * GD, (g + 1) * GD)
            vg = v[:, cs]
            mu = jnp.mean(vg, axis=-1, keepdims=True)
            vc = vg - mu
            var = jnp.mean(vc * vc, axis=-1, keepdims=True)
            vn = (vc * lax.rsqrt(var + EPS) * lnw_ref[g:g + 1, :] + lnb_ref[g:g + 1, :]).astype(BF16)
            wsg = jnp.where(mask, ws_ref[g], 0.0).astype(BF16)
            bcol = bst_ref[:, g:g + 1]
            for blk in range(nblk):
                rs = slice(blk * GB, (blk + 1) * GB)
                sv = _dot(wsg, vn[rs, :]) + bcol
                o_ref[rs, cs] = (u[rs, cs] * sv).astype(BF16)

    full = lambda shape: pl.BlockSpec(shape, lambda i: tuple(0 for _ in shape))
    return pl.pallas_call(
        body, grid=(T // tm,),
        in_specs=[pl.BlockSpec((tm, GW), lambda i: (i, 2)), pl.BlockSpec((tm, GW), lambda i: (i, 3)),
                  full((GG, GD)), full((GG, GD)), full((GG, GB, GB)), full((GB, GG))],
        out_specs=pl.BlockSpec((tm, GW), lambda i: (i, 0)),
        out_shape=jax.ShapeDtypeStruct((T, GW), BF16), name="gmlp_fwd",
        compiler_params=_params(("arbitrary",), 40))(proj, proj, lnw, lnb, ws, bst)


def _gmlp_bwd(proj, dya, dproj, lnw, lnb, ws, bst):
    T = proj.shape[0]
    tm = min(512, T)
    nblk = tm // GB

    def body(u_ref, v_ref, dya_ref, dproj_in, lnw_ref, lnb_ref, ws_ref, bst_ref,
             dz_ref, dlnw_ref, dlnb_ref, dws_ref, dbst_ref):
        del dproj_in
        first = pl.program_id(0) == 0

        @pl.when(first)
        def _():
            dlnw_ref[...] = jnp.zeros_like(dlnw_ref)
            dlnb_ref[...] = jnp.zeros_like(dlnb_ref)
            dws_ref[...] = jnp.zeros_like(dws_ref)
            dbst_ref[...] = jnp.zeros_like(dbst_ref)

        mask = _gmlp_mask()
        lane = lax.broadcasted_iota(jnp.int32, (GB, LANE), 1)
        ur = u_ref[...].astype(F32)
        vr = v_ref[...].astype(F32)
        u, gu = _gelu(ur, with_grad=True)
        v, gv = _gelu(vr, with_grad=True)
        dy = dya_ref[...].astype(F32)
        dbst = jnp.zeros((GB, LANE), F32)
        dlnw_rows, dlnb_rows = [], []
        for g in range(GG):
            cs = slice(g * GD, (g + 1) * GD)
            vg = v[:, cs]
            mu = jnp.mean(vg, axis=-1, keepdims=True)
            vc = vg - mu
            var = jnp.mean(vc * vc, axis=-1, keepdims=True)
            rstd = lax.rsqrt(var + EPS)
            xhat = vc * rstd
            lw = lnw_ref[g:g + 1, :]
            vn = (xhat * lw + lnb_ref[g:g + 1, :]).astype(BF16)
            wsg = jnp.where(mask, ws_ref[g], 0.0).astype(BF16)
            bcol = bst_ref[:, g:g + 1]
            dyg = dy[:, cs]
            ug = u[:, cs]
            dsv = dyg * ug
            dsv_b = dsv.astype(BF16)
            dws_g = jnp.zeros((GB, GB), F32)
            bsum = jnp.zeros((GB, 1), F32)
            dvn_parts = []
            for blk in range(nblk):
                rs = slice(blk * GB, (blk + 1) * GB)
                sv = _dot(wsg, vn[rs, :]) + bcol
                dz_ref[rs, cs] = (dyg[rs, :] * sv * gu[rs, cs]).astype(BF16)
                dws_g = dws_g + _dot(dsv_b[rs, :], vn[rs, :], NT)
                bsum = bsum + jnp.sum(dsv[rs, :], axis=-1, keepdims=True)
                dvn_parts.append(_dot(wsg, dsv_b[rs, :], TN))
            dvn = jnp.concatenate(dvn_parts, axis=0)
            dws_ref[g] += jnp.where(mask, dws_g, 0.0)
            dbst = dbst + jnp.where(lane == g, bsum, 0.0)
            dlnw_rows.append(jnp.sum(dvn * xhat, axis=0, keepdims=True))
            dlnb_rows.append(jnp.sum(dvn, axis=0, keepdims=True))
            dxh = dvn * lw
            dvg = rstd * (dxh - jnp.mean(dxh, axis=-1, keepdims=True)
                          - xhat * jnp.mean(dxh * xhat, axis=-1, keepdims=True))
            dz_ref[:, GW + g * GD:GW + (g + 1) * GD] = (dvg * gv[:, cs]).astype(BF16)
        dlnw_ref[...] += jnp.concatenate(dlnw_rows, axis=0)
        dlnb_ref[...] += jnp.concatenate(dlnb_rows, axis=0)
        dbst_ref[...] += dbst

    full = lambda shape: pl.BlockSpec(shape, lambda i: tuple(0 for _ in shape))
    outs = pl.pallas_call(
        body, grid=(T // tm,),
        in_specs=[pl.BlockSpec((tm, GW), lambda i: (i, 2)), pl.BlockSpec((tm, GW), lambda i: (i, 3)),
                  pl.BlockSpec((tm, GW), lambda i: (i, 0)), pl.BlockSpec(memory_space=pl.ANY),
                  full((GG, GD)), full((GG, GD)), full((GG, GB, GB)), full((GB, GG))],
        out_specs=[pl.BlockSpec((tm, 2 * GW), lambda i: (i, 1)), full((GG, GD)), full((GG, GD)),
                   full((GG, GB, GB)), full((GB, LANE))],
        out_shape=[jax.ShapeDtypeStruct(dproj.shape, dproj.dtype), jax.ShapeDtypeStruct((GG, GD), F32),
                   jax.ShapeDtypeStruct((GG, GD), F32), jax.ShapeDtypeStruct((GG, GB, GB), F32),
                   jax.ShapeDtypeStruct((GB, LANE), F32)],
        input_output_aliases={3: 0}, name="gmlp_bwd",
        compiler_params=_params(("arbitrary",), 48))(proj, proj, dya, dproj, lnw, lnb, ws, bst)
    return outs


def _merge_fwd(ya_pre, yb_pre, proj, bias, wpa, wpb):
    T = proj.shape[0]
    tm = min(512, T)

    def body(ya_ref, yb_ref, g_ref, b_ref, wpa_ref, wpb_ref, m_ref, oa_ref, ob_ref):
        ya = _dot(ya_ref[...], wpa_ref[...])
        yb = _dot(yb_ref[...], wpb_ref[...])
        g = g_ref[...].astype(F32)
        sa = _sigmoid(g[:, :D] + b_ref[0:1, :])
        sb = _sigmoid(g[:, D:] + b_ref[1:2, :])
        m_ref[...] = (sa * ya + sb * yb).astype(BF16)
        oa_ref[...] = ya.astype(BF16)
        ob_ref[...] = yb.astype(BF16)

    row = lambda w: pl.BlockSpec((tm, w), lambda i: (i, 0))
    full = lambda shape: pl.BlockSpec(shape, lambda i: tuple(0 for _ in shape))
    o = jax.ShapeDtypeStruct((T, D), BF16)
    return pl.pallas_call(
        body, grid=(T // tm,),
        in_specs=[row(GW), row(SI), row(2 * D), full((2, D)), full((GW, D)), full((SI, D))],
        out_specs=[row(D), row(D), row(D)], out_shape=[o, o, o], name="merge_fwd",
        compiler_params=_params(("arbitrary",), 40))(ya_pre, yb_pre, proj, bias, wpa, wpb)


def _merge_bwd(dm, proj, bias, ya, yb, wpa, wpb):
    T = proj.shape[0]
    tm = min(512, T)

    def body(dm_ref, g_ref, b_ref, ya_ref, yb_ref, wpa_ref, wpb_ref,
             dg_ref, dya_ref, dyb_ref, dpa_ref, dpb_ref, db_ref):
        dmv = dm_ref[...].astype(F32)
        g = g_ref[...].astype(F32)
        sa = _sigmoid(g[:, :D] + b_ref[0:1, :])
        sb = _sigmoid(g[:, D:] + b_ref[1:2, :])
        dya = (dmv * sa).astype(BF16)
        dyb = (dmv * sb).astype(BF16)
        dga = dmv * ya_ref[...].astype(F32) * sa * (1.0 - sa)
        dgb = dmv * yb_ref[...].astype(F32) * sb * (1.0 - sb)
        dg_ref[:, :D] = dga.astype(BF16)
        dg_ref[:, D:] = dgb.astype(BF16)
        dya_ref[...] = dya
        dyb_ref[...] = dyb
        dpa_ref[...] = _dot(dya, wpa_ref[...], NT).astype(BF16)
        dpb_ref[...] = _dot(dyb, wpb_ref[...], NT).astype(BF16)
        part = jnp.concatenate([jnp.sum(dga, axis=0, keepdims=True), jnp.sum(dgb, axis=0, keepdims=True)], axis=0)

        @pl.when(pl.program_id(0) == 0)
        def _():
            db_ref[...] = part

        @pl.when(pl.program_id(0) > 0)
        def _():
            db_ref[...] += part

    row = lambda w: pl.BlockSpec((tm, w), lambda i: (i, 0))
    full = lambda shape: pl.BlockSpec(shape, lambda i: tuple(0 for _ in shape))
    o = lambda w: jax.ShapeDtypeStruct((T, w), BF16)
    return pl.pallas_call(
        body, grid=(T // tm,),
        in_specs=[row(D), row(2 * D), full((2, D)), row(D), row(D), full((GW, D)), full((SI, D))],
        out_specs=[row(2 * D), row(D), row(D), row(GW), row(SI), full((2, D))],
        out_shape=[o(DP_COLS), o(D), o(D), o(GW), o(SI), jax.ShapeDtypeStruct((2, D), F32)], name="merge_bwd",
        compiler_params=_params(("arbitrary",), 48))(dm, proj, bias, ya, yb, wpa, wpb)


RB = 128


def _shift_matrix(j):
    r = lax.broadcasted_iota(jnp.int32, (RB, RB), 0)
    c = lax.broadcasted_iota(jnp.int32, (RB, RB), 1)
    return jnp.where(c == r - j, 1.0, 0.0).astype(BF16)


def _rows_down(xb, before, shifts):
    H = SUBLANE
    mats = [_shift_matrix(j) for j in shifts]
    outs = [[] for _ in shifts]
    for b in range(xb.shape[0] // RB):
        blk = xb[b * RB:(b + 1) * RB]
        edge = jnp.concatenate([before, blk[:2 * H].astype(F32)[:H]], axis=0)
        for i, j in enumerate(shifts):
            outs[i] += [edge[H - j:2 * H - j], _dot(mats[i], blk)[H:]]
        before = blk[RB - 2 * H:].astype(F32)[H:]
    return [jnp.concatenate(o, axis=0) for o in outs]


def _rows_up(xb, after, shifts):
    H = SUBLANE
    nb = xb.shape[0] // RB
    mats = [_shift_matrix(-j) for j in shifts]
    outs = [[] for _ in shifts]
    for b in range(nb):
        blk = xb[b * RB:(b + 1) * RB]
        nxt = xb[(b + 1) * RB:(b + 1) * RB + 2 * H].astype(F32)[:H] if b + 1 < nb else after
        edge = jnp.concatenate([blk[RB - 2 * H:].astype(F32)[H:], nxt], axis=0)
        for i, j in enumerate(shifts):
            outs[i] += [_dot(mats[i], blk)[:RB - H], edge[j:H + j]]
    return [jnp.concatenate(o, axis=0) for o in outs]


def _ffn_act_fwd(up, cw, cb):
    T = up.shape[0]
    tm = min(512, T)
    H = SUBLANE

    def body(up_ref, cw_ref, cb_ref, o_ref, xc_ref, halo):
        @pl.when(pl.program_id(1) == 0)
        def _():
            halo[...] = jnp.zeros_like(halo)

        xb = up_ref[...]
        x2, x1 = _rows_down(xb, halo[...], (2, 1))
        xc = cb_ref[...] + cw_ref[0:1, :] * x2 + cw_ref[1:2, :] * x1 + cw_ref[2:3, :] * xb.astype(F32)
        xc_ref[...] = xc.astype(BF16)
        gate = xc[:, :FT]
        o_ref[...] = (gate * _sigmoid(gate) * xc[:, FT:]).astype(BF16)
        halo[...] = xb[tm - 2 * H:].astype(F32)[H:]

    tile = pl.BlockSpec((tm, 2 * FT), lambda j, i: (i, j))
    return pl.pallas_call(
        body, grid=(2, T // tm),
        in_specs=[tile, pl.BlockSpec((FK, 2 * FT), lambda j, i: (0, j)), pl.BlockSpec((1, 2 * FT), lambda j, i: (0, j))],
        out_specs=[pl.BlockSpec((tm, FT), lambda j, i: (i, j)), tile],
        out_shape=[jax.ShapeDtypeStruct((T, DFF), BF16), jax.ShapeDtypeStruct((T, 2 * DFF), BF16)],
        scratch_shapes=[pltpu.VMEM((H, 2 * FT), F32)], name="ffn_act_fwd",
        compiler_params=_params(("arbitrary", "arbitrary"), 48))(up, cw, cb)


def _ffn_act_bwd(up, xc, dact, cw):
    T = up.shape[0]
    tm = min(512, T)
    nt = T // tm
    H = SUBLANE

    def body(up_ref, xc_ref, da_ref, cw_ref, dup_ref, dcw_ref, dcb_ref, ahead):
        @pl.when(pl.program_id(1) == 0)
        def _():
            ahead[...] = jnp.zeros_like(ahead)
            dcw_ref[...] = jnp.zeros_like(dcw_ref)
            dcb_ref[...] = jnp.zeros_like(dcb_ref)

        xcv = xc_ref[...].astype(F32)
        gate, val = xcv[:, :FT], xcv[:, FT:]
        sg = _sigmoid(gate)
        dav = da_ref[...].astype(F32)
        dgate = dav * val * sg * (1.0 + gate * (1.0 - sg))
        dval = dav * gate * sg
        dxc = jnp.concatenate([dgate, dval], axis=1)
        d1, d2 = _rows_up(dxc.astype(BF16), ahead[...], (1, 2))
        x = up_ref[...].astype(F32)
        dcb_ref[...] += jnp.sum(dxc, axis=0, keepdims=True)
        dcw_ref[...] += jnp.concatenate([jnp.sum(d * x, axis=0, keepdims=True) for d in (d2, d1, dxc)], axis=0)
        dup_ref[...] = (cw_ref[2:3, :] * dxc + cw_ref[1:2, :] * d1 + cw_ref[0:1, :] * d2).astype(BF16)
        ahead[...] = dxc[0:H, :]

    tile = pl.BlockSpec((tm, 2 * FT), lambda j, i: (nt - 1 - i, j))
    return pl.pallas_call(
        body, grid=(2, nt),
        in_specs=[tile, tile, pl.BlockSpec((tm, FT), lambda j, i: (nt - 1 - i, j)),
                  pl.BlockSpec((FK, 2 * FT), lambda j, i: (0, j))],
        out_specs=[tile, pl.BlockSpec((FK, 2 * FT), lambda j, i: (0, j)), pl.BlockSpec((1, 2 * FT), lambda j, i: (0, j))],
        out_shape=[jax.ShapeDtypeStruct((T, 2 * DFF), BF16), jax.ShapeDtypeStruct((FK, 2 * DFF), F32),
                   jax.ShapeDtypeStruct((1, 2 * DFF), F32)],
        scratch_shapes=[pltpu.VMEM((H, 2 * FT), F32)], name="ffn_act_bwd",
        compiler_params=_params(("arbitrary", "arbitrary"), 56))(up, xc, dact, cw)


def _softplus(x):
    e = jnp.exp(-jnp.abs(x))
    return jnp.maximum(x, 0.0) + jnp.where(e < 1e-4, e * (1.0 - 0.5 * e), jnp.log(1.0 + e))


def _ssd_consts():
    li = lax.broadcasted_iota(jnp.int32, (LS, LS), 0)
    si = lax.broadcasted_iota(jnp.int32, (LS, LS), 1)
    tril = si <= li
    hh = lax.broadcasted_iota(jnp.int32, (LANE, SI), 0)
    cc = lax.broadcasted_iota(jnp.int32, (LANE, SI), 1) // SP
    expand = jnp.where(hh == cc, 1.0, 0.0).astype(BF16)
    return tril, expand


def _ssd_pre(xc, dt_ref, dtb_ref, alog_ref, tril, expand):
    sx = _sigmoid(xc)
    xbc = xc * sx
    xs, bm, cm = xbc[:, :SI], xbc[:, SI:SI + SG * SN], xbc[:, SI + SG * SN:]
    dtin = dt_ref[...] + dtb_ref[...]
    dt = _softplus(dtin)
    a_neg = -jnp.exp(alog_ref[...])
    dta = dt * a_neg
    trilb = jnp.where(tril, 1.0, 0.0).astype(BF16)
    a = _dot3_rhs(trilb, dta, NN)
    a_exp = _dot3(a, expand, NN)
    dt_exp = _dot2(dt, expand, NN)
    xdt = xs * dt_exp
    a_last = a_exp[LS - 1:LS, :]
    return dict(xc=xc, sx=sx, xs=xs, bm=bm, cm=cm, dtin=dt_ref[...] + dtb_ref[...], dt=dt, a_neg=a_neg,
                a=a, a_t=a.T, a_exp=a_exp, dt_exp=dt_exp, xdt=xdt, ea=jnp.exp(a_exp),
                w=jnp.exp(a_last - a_exp), eal=jnp.exp(a_last))


def _head_decay(pre, tril, h):
    seg = pre["a"][:, h:h + 1] - pre["a_t"][h:h + 1, :]
    return jnp.exp(jnp.where(tril, seg, -1e30))


def _ssd_fwd(proj, dtraw, cw, cb, dtb, alog, dexp, nw):
    T = proj.shape[0]
    nc = T // LS
    H = SUBLANE

    def body(z_ref, x_ref, dt_ref, cw_ref, cb_ref, dtb_ref, alog_ref, dexp_ref, nw_ref,
             yb_ref, y_ref, sp_ref, xc_ref, halo, st):
        @pl.when(pl.program_id(0) == 0)
        def _():
            halo[...] = jnp.zeros_like(halo)
            st[...] = jnp.zeros_like(st)

        xb = x_ref[...]
        taps = _rows_down(xb, halo[...], (3, 2, 1)) + [xb.astype(F32)]
        xc = cb_ref[...]
        for k in range(SK):
            xc = xc + cw_ref[k:k + 1, :] * taps[k]
        xc_ref[...] = xc.astype(BF16)
        tril, expand = _ssd_consts()
        pre = _ssd_pre(xc, dt_ref, dtb_ref, alog_ref, tril, expand)
        lane = lax.broadcasted_iota(jnp.int32, (LS, LANE), 1)
        lo = lane < SP
        zf = z_ref[...].astype(F32)
        siluz = zf * _sigmoid(zf)
        for g in range(SG):
            gs = slice(g * SGW, (g + 1) * SGW)
            bg = pre["bm"][:, g * SN:(g + 1) * SN].astype(BF16)
            cg = pre["cm"][:, g * SN:(g + 1) * SN].astype(BF16)
            gmat = _dot(cg, bg, NT)
            sg = st[g]
            sp_ref[0, g] = sg
            yoff = _dot(cg, sg.astype(BF16))
            parts = []
            for j in range(SGW // LANE):
                h0 = g * (SGW // SP) + 2 * j
                m0 = gmat * _head_decay(pre, tril, h0)
                m1 = gmat * _head_decay(pre, tril, h0 + 1)
                xp = pre["xdt"][:, g * SGW + j * LANE:g * SGW + (j + 1) * LANE]
                rhs = jnp.concatenate([jnp.where(lo, xp, 0.0), jnp.where(lo, 0.0, xp)], axis=0).astype(BF16)
                parts.append(_dot(jnp.concatenate([m0, m1], axis=1).astype(BF16), rhs))
            y = (jnp.concatenate(parts, axis=1) + pre["ea"][:, gs] * yoff + dexp_ref[:, gs] * pre["xs"][:, gs])
            st[g] = pre["eal"][:, gs] * sg + _dot(bg, (pre["w"][:, gs] * pre["xdt"][:, gs]).astype(BF16), TN)
            y_ref[:, gs] = y
            yg = y * siluz[:, gs]
            r = lax.rsqrt(jnp.mean(yg * yg, axis=-1, keepdims=True) + EPS)
            yb_ref[:, gs] = (yg * r * nw_ref[:, gs]).astype(BF16)
        halo[...] = xb[LS - 2 * H:].astype(F32)[H:]

    vec = lambda w: pl.BlockSpec((1, w), lambda c: (0, 0))
    return pl.pallas_call(
        body, grid=(nc,),
        in_specs=[pl.BlockSpec((LS, SI), lambda c: (c, 2)), pl.BlockSpec((LS, SXBC), lambda c: (c, 2)),
                  pl.BlockSpec((LS, LANE), lambda c: (c, 0)),
                  pl.BlockSpec((SK, SXBC), lambda c: (0, 0)), vec(SXBC), vec(LANE), vec(LANE), vec(SI), vec(SI)],
        out_specs=[pl.BlockSpec((LS, SI), lambda c: (c, 0)), pl.BlockSpec((LS, SI), lambda c: (c, 0)),
                   pl.BlockSpec((1, SG, SN, SGW), lambda c: (c, 0, 0, 0)), pl.BlockSpec((LS, SXBC), lambda c: (c, 0))],
        out_shape=[jax.ShapeDtypeStruct((T, SI), BF16), jax.ShapeDtypeStruct((T, SI), F32),
                   jax.ShapeDtypeStruct((nc, SG, SN, SGW), F32), jax.ShapeDtypeStruct((T, SXBC), BF16)],
        scratch_shapes=[pltpu.VMEM((H, SXBC), F32), pltpu.VMEM((SG, SN, SGW), F32)], name="ssd_fwd",
        compiler_params=_params(("arbitrary",), VMEM_CAP_MB))(proj, proj, dtraw, cw, cb, dtb, alog, dexp, nw)


def _ssd_bwd(proj, xcs, dtraw, y, sprev, dyb, dproj, cw, dtb, alog, dexp, nw):
    T = proj.shape[0]
    nc = T // LS
    H = SUBLANE
    NJ = 1

    def body(z_ref, x_ref, xc_ref, dt_ref, y_ref, sp_ref, dyb_ref, dproj_in,
             cw_ref, dtb_ref, alog_ref, dexp_ref, nw_ref,
             dp_ref, ddt_ref, dcw_ref, dcb_ref, ddtb_ref, da_ref, dd_ref, dnw_ref,
             ahead, ds, stage):
        del dproj_in
        i = pl.program_id(0)
        j = pl.program_id(1)

        @pl.when(jnp.logical_and(i == 0, j == 0))
        def _():
            ahead[...] = jnp.zeros_like(ahead)
            ds[...] = jnp.zeros_like(ds)
            for r in (dcw_ref, dcb_ref, ddtb_ref, da_ref, dd_ref, dnw_ref):
                r[...] = jnp.zeros_like(r)

        @pl.when(j == 0)
        def _():
            tril, expand = _ssd_consts()
            pre = _ssd_pre(xc_ref[...].astype(F32), dt_ref, dtb_ref, alog_ref, tril, expand)
            lane = lax.broadcasted_iota(jnp.int32, (LS, LANE), 1)
            sub = lax.broadcasted_iota(jnp.int32, (LANE, LS), 0)
            rowi = lax.broadcasted_iota(jnp.int32, (LS, 1), 0)
            lo = lane < SP
            xs, xdt, ea, w, eal = pre["xs"], pre["xdt"], pre["ea"], pre["w"], pre["eal"]

            zf = z_ref[...].astype(F32)
            sz = _sigmoid(zf)
            siluz = zf * sz
            yv = y_ref[...]
            yg = yv * siluz
            dout = dyb_ref[...].astype(F32)
            dyg_parts, dnw_parts = [], []
            for g in range(SG):
                gs = slice(g * SGW, (g + 1) * SGW)
                ygg = yg[:, gs]
                r = lax.rsqrt(jnp.mean(ygg * ygg, axis=-1, keepdims=True) + EPS)
                yhat = ygg * r
                dn = dout[:, gs] * nw_ref[:, gs]
                dnw_parts.append(jnp.sum(dout[:, gs] * yhat, axis=0, keepdims=True))
                dyg_parts.append(r * (dn - yhat * jnp.mean(dn * yhat, axis=-1, keepdims=True)))
            dyg = jnp.concatenate(dyg_parts, axis=1)
            dnw_ref[...] += jnp.concatenate(dnw_parts, axis=1)
            dy = dyg * siluz
            stage[:, 0:SI] = (dyg * yv * sz * (1.0 + zf * (1.0 - sz))).astype(BF16)
            dd_ref[...] += jnp.sum(dy * xs, axis=0, keepdims=True)
            tt = ea * dy

            da_rows = jnp.zeros((LS, LANE), F32)
            da_cols = jnp.zeros((LANE, LS), F32)
            dxdt_parts, db_parts, dc_parts, daexp_parts = [], [], [], []
            for g in range(SG):
                gs = slice(g * SGW, (g + 1) * SGW)
                bg = pre["bm"][:, g * SN:(g + 1) * SN].astype(BF16)
                cg = pre["cm"][:, g * SN:(g + 1) * SN].astype(BF16)
                sg = sp_ref[0, g]
                sgb = sg.astype(BF16)
                dsg = ds[g]
                dsgb = dsg.astype(BF16)
                ttg = tt[:, gs].astype(BF16)
                yoff = _dot(cg, sgb)
                dc = _dot(ttg, sgb, NT)
                gmat = _dot(cg, bg, NT)
                dgm = jnp.zeros((LS, LS), F32)
                dxdt_pairs = []
                for jj in range(SGW // LANE):
                    h0 = g * (SGW // SP) + 2 * jj
                    ps = slice(g * SGW + jj * LANE, g * SGW + (jj + 1) * LANE)
                    l0 = _head_decay(pre, tril, h0)
                    l1 = _head_decay(pre, tril, h0 + 1)
                    m0 = gmat * l0
                    m1 = gmat * l1
                    dyp = dy[:, ps]
                    dy_lo = jnp.where(lo, dyp, 0.0).astype(BF16)
                    dy_hi = jnp.where(lo, 0.0, dyp).astype(BF16)
                    xpb = xdt[:, ps].astype(BF16)
                    dm0 = _dot(dy_lo, xpb, NT)
                    dm1 = _dot(dy_hi, xpb, NT)
                    q0 = dm0 * m0
                    q1 = dm1 * m1
                    da_rows = da_rows + jnp.where(lane == h0, jnp.sum(q0, axis=1, keepdims=True), 0.0)
                    da_rows = da_rows + jnp.where(lane == h0 + 1, jnp.sum(q1, axis=1, keepdims=True), 0.0)
                    da_cols = da_cols + jnp.where(sub == h0, jnp.sum(q0, axis=0, keepdims=True), 0.0)
                    da_cols = da_cols + jnp.where(sub == h0 + 1, jnp.sum(q1, axis=0, keepdims=True), 0.0)
                    dgm = dgm + dm0 * l0 + dm1 * l1
                    mcat = jnp.concatenate([m0, m1], axis=0).astype(BF16)
                    dycat = jnp.concatenate([dy_lo, dy_hi], axis=0)
                    dxdt_pairs.append(_dot(mcat, dycat, TN))
                dgb = dgm.astype(BF16)
                dc = dc + _dot(dgb, bg)
                db = _dot(dgb, cg, TN)
                zg = _dot(bg, dsgb)
                wg, xdtg = w[:, gs], xdt[:, gs]
                dxdt_g = jnp.concatenate(dxdt_pairs, axis=1) + wg * zg
                qg = zg * xdtg * wg
                last = (jnp.sum(qg, axis=0, keepdims=True)
                        + jnp.sum(dsg * sg, axis=0, keepdims=True) * eal[:, gs])
                daexp_parts.append(dy[:, gs] * ea[:, gs] * yoff - qg + jnp.where(rowi == LS - 1, last, 0.0))
                db = db + _dot((wg * xdtg).astype(BF16), dsgb, NT)
                ds[g] = eal[:, gs] * dsg + _dot(cg, ttg, TN)
                dxdt_parts.append(dxdt_g)
                db_parts.append(db)
                dc_parts.append(dc)
            dxdt = jnp.concatenate(dxdt_parts, axis=1)
            da_exp = jnp.concatenate(daexp_parts, axis=1)
            da = _dot2(da_exp, expand, NT) + da_rows - da_cols.T
            triub = jnp.where(tril, 1.0, 0.0).astype(BF16)
            ddta = _dot3_rhs(triub, da, TN)
            ddt = ddta * pre["a_neg"] + _dot2(dxdt * xs, expand, NT)
            da_ref[...] += jnp.sum(ddta * pre["dt"], axis=0, keepdims=True)
            ddt_raw = ddt * _sigmoid(pre["dtin"])
            ddt_ref[...] = ddt_raw
            ddtb_ref[...] += jnp.sum(ddt_raw, axis=0, keepdims=True)
            dxs = dexp_ref[...] * dy + dxdt * pre["dt_exp"]
            dxbc = jnp.concatenate([dxs] + db_parts + dc_parts, axis=1)
            sx, xc = pre["sx"], pre["xc"]
            dxc = dxbc * sx * (1.0 + xc * (1.0 - sx))
            taps = _rows_up(dxc.astype(BF16), ahead[...], (3, 2, 1)) + [dxc]
            xr = x_ref[...].astype(F32)
            dcb_ref[...] += jnp.sum(dxc, axis=0, keepdims=True)
            dcw_ref[...] += jnp.concatenate([jnp.sum(t * xr, axis=0, keepdims=True) for t in taps], axis=0)
            dxr = cw_ref[0:1, :] * taps[0]
            for k in range(1, SK):
                dxr = dxr + cw_ref[k:k + 1, :] * taps[k]
            stage[:, SI:] = dxr.astype(BF16)
            ahead[...] = dxc[0:H, :]

        dp_ref[...] = stage[...]

    vec = lambda w: pl.BlockSpec((1, w), lambda i, j: (0, 0))
    rev = lambda w, cb_: pl.BlockSpec((LS, w), lambda i, j: (nc - 1 - i, cb_))
    outs = pl.pallas_call(
        body, grid=(nc, NJ),
        in_specs=[rev(SI, 2), rev(SXBC, 2), rev(SXBC, 0),
                  rev(LANE, 0), rev(SI, 0),
                  pl.BlockSpec((1, SG, SN, SGW), lambda i, j: (nc - 1 - i, 0, 0, 0)),
                  rev(SI, 0), pl.BlockSpec(memory_space=pl.ANY),
                  pl.BlockSpec((SK, SXBC), lambda i, j: (0, 0)), vec(LANE), vec(LANE), vec(SI), vec(SI)],
        out_specs=[rev(DP_SSM, 1), rev(LANE, 0),
                   pl.BlockSpec((SK, SXBC), lambda i, j: (0, 0)), vec(SXBC), vec(LANE), vec(LANE), vec(SI), vec(SI)],
        out_shape=[jax.ShapeDtypeStruct(dproj.shape, dproj.dtype), jax.ShapeDtypeStruct((T, LANE), F32),
                   jax.ShapeDtypeStruct((SK, SXBC), F32), jax.ShapeDtypeStruct((1, SXBC), F32),
                   jax.ShapeDtypeStruct((1, LANE), F32), jax.ShapeDtypeStruct((1, LANE), F32),
                   jax.ShapeDtypeStruct((1, SI), F32), jax.ShapeDtypeStruct((1, SI), F32)],
        scratch_shapes=[pltpu.VMEM((H, SXBC), F32),
                        pltpu.VMEM((SG, SN, SGW), F32), pltpu.VMEM((LS, SI + SXBC), BF16)],
        input_output_aliases={7: 0}, name="ssd_bwd",
        compiler_params=_params(("arbitrary", "arbitrary"), VMEM_CAP_MB))(
            proj, proj, xcs, dtraw, y, sprev, dyb, dproj, cw, dtb, alog, dexp, nw)
    return outs


def _perm_ffn_cols(a):
    lead = a.shape[:-1]
    return a.reshape(lead + (2, 2, FT)).swapaxes(-3, -2).reshape(lead + (2 * DFF,))


def _perm_ffn_rows(a):
    return a.reshape((2, 2, FT) + a.shape[1:]).swapaxes(0, 1).reshape(a.shape)


def _pad_lanes(v, n=LANE):
    return jnp.pad(v, ((0, 0), (0, n - v.shape[-1])))


LATE = ["w_proj_a", "w_proj_b", "w_out", "ffn_w_up_t", "ffn_w_down"]
WGRAD = BF16


class _NoExchange:
    def gather_start(self):
        return None

    def gather_pass_on(self, outs):
        return None

    def late_weights(self, w, outs):
        return w

    def reduce_late(self, grads):
        return None

    def w_in_to_sibling(self, grad_main, grad_dt):
        return None

    def reduce_w_in(self, outs):
        return None

    def reduced(self, late_outs, w_in_outs):
        pass


def _local_step(x, tgt, w, hooks=None):
    hooks = hooks or _NoExchange()

    def mm(*args, side=None, **kw):
        out = _matmul(*args, side=side, **kw)
        return out if side is not None else (out, [])

    win_t = w["w_in_t"]
    win_dt = jnp.pad(w["w_in_t"][PMAIN:], ((0, LANE - SH), (0, 0)))
    fcw = _perm_ffn_cols(w["ffn_conv_w"])
    fcb = _perm_ffn_cols(w["ffn_conv_b"][None, :])
    mixw = w["mix_norm_w"][None, :]
    ffnw = w["ffn_norm_w"][None, :]
    finw = w["final_norm_w"][None, :]
    bst = w["gmlp_bs"].T
    scb = w["ssm_conv_b"][None, :]
    dtb = _pad_lanes(w["ssm_dt_bias"][None, :])
    alog = _pad_lanes(w["ssm_a_log"][None, :])
    dexp = jnp.repeat(w["ssm_d"], SP)[None, :]
    snw = w["ssm_norm_w"][None, :]

    xn = _rms_fwd(x, mixw, name="mix_norm")
    proj, got = mm(xn, win_t, name="in_proj", out_dtype=BF16, tb=True, tn=1536, j_outer=True, b_rows=PMAIN,
                   side=hooks.gather_start())
    dtraw, got = mm(xn, win_dt, name="in_proj_dt", out_dtype=F32, tb=True, side=hooks.gather_pass_on(got))
    w = hooks.late_weights(w, got)
    wup = _perm_ffn_rows(w["ffn_w_up_t"])
    ya_pre = _gmlp_fwd(proj, w["gmlp_ln_w"], w["gmlp_ln_b"], w["gmlp_ws"], bst)
    yb_pre, y_ssd, sprev, ssm_xc = _ssd_fwd(proj, dtraw, w["ssm_conv_w"], scb, dtb, alog, dexp, snw)
    merged, ya, yb = _merge_fwd(ya_pre, yb_pre, proj, w["gate_bias"], w["w_proj_a"], w["w_proj_b"])
    h1 = _matmul(merged, w["w_out"], name="out_proj", out_dtype=F32, add=x)
    hn = _rms_fwd(h1, ffnw, name="ffn_norm")
    up = _matmul(hn, wup, name="ffn_up", out_dtype=BF16, tb=True, tn=FT, j_outer=True)
    act, ffn_xc = _ffn_act_fwd(up, fcw, fcb)
    h2 = _matmul(act, w["ffn_w_down"], name="ffn_down", out_dtype=F32, tk=DFF, add=h1)

    loss_row, dh2, d_finw = _loss_head(h2, tgt, finw)
    dact = _matmul(dh2, w["ffn_w_down"], name="ffn_down_dx", out_dtype=BF16, tb=True, tn=FT)
    d_wdown = _matmul(act, dh2, name="ffn_down_dw", out_dtype=WGRAD, ta=True, tm=FT)
    dup, d_fcw, d_fcb = _ffn_act_bwd(up, ffn_xc, dact, fcw)
    dhn = _matmul(dup, wup, name="ffn_up_dx", out_dtype=F32, tk=2 * FT)
    d_wup = _matmul(dup, hn, name="ffn_up_dw", out_dtype=WGRAD, ta=True, tm=FT, tk=2048)
    dh1, d_ffnw = _rms_bwd(h1, ffnw, dhn, dh2, name="ffn_norm_bwd")
    dmerged = _matmul(dh1, w["w_out"], name="out_proj_dx", out_dtype=BF16, tb=True)
    d_wout = _matmul(merged, dh1, name="out_proj_dw", out_dtype=WGRAD, ta=True)
    dproj, dya, dyb, dya_pre, dyb_pre, d_gbias = _merge_bwd(dmerged, proj, w["gate_bias"], ya, yb,
                                                           w["w_proj_a"], w["w_proj_b"])
    d_wpa = _matmul(ya_pre, dya, name="proj_a_dw", out_dtype=WGRAD, ta=True, tk=2048)
    d_wpb = _matmul(yb_pre, dyb, name="proj_b_dw", out_dtype=WGRAD, ta=True, tk=2048)
    dproj, d_lnw, d_lnb, d_ws, d_bst = _gmlp_bwd(proj, dya_pre, dproj, w["gmlp_ln_w"], w["gmlp_ln_b"],
                                                 w["gmlp_ws"], bst)
    dproj, ddt, d_scw, d_scb, d_dtb, d_a, d_dch, d_snw = _ssd_bwd(
        proj, ssm_xc, dtraw, y_ssd, sprev, dyb_pre, dproj, w["ssm_conv_w"], dtb, alog, dexp, snw)
    late = {"w_proj_a": d_wpa, "w_proj_b": d_wpb, "w_out": d_wout, "ffn_w_up_t": _perm_ffn_rows(d_wup),
            "ffn_w_down": d_wdown}
    gap = (2 * D + 2 * GW, DP_GAP)
    d_win_main, late_outs = mm(dproj, xn, name="in_proj_dw", out_dtype=WGRAD, ta=True, a_gap=gap, tk=2048,
                               side=hooks.reduce_late(late))
    d_win_dt = _matmul(ddt, xn, name="in_proj_dt_dw", out_dtype=F32, ta=True)
    d_win_t = jnp.concatenate([d_win_main, d_win_dt[:SH]], axis=0)
    dxn, got = mm(ddt, win_dt, name="in_proj_dt_dx", out_dtype=F32,
                  side=hooks.w_in_to_sibling(d_win_main, d_win_dt[:SH]))
    dxn, w_in_outs = mm(dproj, win_t, name="in_proj_dx", out_dtype=F32, add=dxn, b_rows=PMAIN, a_gap=gap,
                        side=hooks.reduce_w_in(got))
    hooks.reduced(late_outs, w_in_outs)
    grad_x, d_mixw = _rms_bwd(x, mixw, dxn, dh1, name="mix_norm_bwd")

    a_neg = -jnp.exp(w["ssm_a_log"])
    grads = {
        "mix_norm_w": d_mixw[0],
        "w_in_t": d_win_t,
        "gate_bias": d_gbias,
        "gmlp_ln_w": d_lnw, "gmlp_ln_b": d_lnb, "gmlp_ws": d_ws, "gmlp_bs": d_bst[:, :GG].T,
        "ssm_conv_w": d_scw, "ssm_conv_b": d_scb[0],
        "ssm_dt_bias": d_dtb[0, :SH], "ssm_a_log": d_a[0, :SH] * a_neg,
        "ssm_d": d_dch.reshape(SH, SP).sum(axis=-1), "ssm_norm_w": d_snw[0],
        **late,
        "ffn_norm_w": d_ffnw[0],
        "ffn_conv_w": _perm_ffn_cols(d_fcw), "ffn_conv_b": _perm_ffn_cols(d_fcb)[0],
        "ffn_w_down": d_wdown, "final_norm_w": d_finw[0],
    }
    return loss_row, grad_x, grads


MESH = pl.DeviceIdType.MESH
HBM_SPEC = pl.BlockSpec(memory_space=pltpu.HBM)


def _axes():
    return lax.axis_index("x"), lax.axis_index("y"), lax.axis_index("c")


def _all_gather(shards, *, name):
    na = len(shards)

    def body(*refs):
        x_refs, out_refs = refs[:na], refs[na:2 * na]
        send_sems, recv_sems, local_sems = refs[2 * na:]
        x, y, c = _axes()
        me, sibling = (x, y, c), (x, y, 1 - c)
        chips = [(1 - x, y), (x, 1 - y), (1 - x, 1 - y)]

        def slot(a, px, py, pc):
            return out_refs[a].at[4 * px + 2 * py + pc]

        def copy(a, k, block, to, src=None):
            return pltpu.make_async_remote_copy(
                src_ref=slot(a, *block) if src is None else src, dst_ref=slot(a, *block),
                send_sem=send_sems.at[7 * a + k], recv_sem=recv_sems.at[7 * a + k], device_id=to, device_id_type=MESH)

        mine = [pltpu.make_async_copy(x_refs[a], slot(a, *me), local_sems.at[a]) for a in range(na)]
        for cp in mine:
            cp.start()
        first = []
        for a in range(na):
            first.append(copy(a, 0, me, sibling, src=x_refs[a]))
            first += [copy(a, 1 + j, me, (*chip, c), src=x_refs[a]) for j, chip in enumerate(chips)]
        for cp in first:
            cp.start()
        passed = []
        for j, chip in enumerate(chips):
            for a in range(na):
                copy(a, 1 + j, (*chip, c), me).wait_recv()
                cp = copy(a, 4 + j, (*chip, c), sibling)
                cp.start()
                passed.append(cp)
        for a in range(na):
            copy(a, 0, sibling, me).wait_recv()
        for j, chip in enumerate(chips):
            for a in range(na):
                copy(a, 4 + j, (*chip, 1 - c), me).wait_recv()
        for cp in first + passed:
            cp.wait_send()
        for cp in mine:
            cp.wait()

    return pl.pallas_call(
        body, out_shape=[jax.ShapeDtypeStruct((NDEV,) + s.shape, s.dtype) for s in shards],
        in_specs=[HBM_SPEC] * na, out_specs=[HBM_SPEC] * na,
        scratch_shapes=[pltpu.SemaphoreType.DMA((7 * na,)), pltpu.SemaphoreType.DMA((7 * na,)),
                        pltpu.SemaphoreType.DMA((na,))],
        name=name)(*shards)


def _exchange(srcs, plan, *, name):
    na = len(srcs)
    n = len(plan(0, 0, 0))

    def body(*refs):
        src_refs, out_refs = refs[:na], refs[na:2 * na]
        send_sems, recv_sems = refs[2 * na:]
        x, y, c = _axes()
        copies = []
        for k, (slab, peer) in enumerate(plan(x, y, c)):
            for a in range(na):
                cp = pltpu.make_async_remote_copy(
                    src_ref=src_refs[a].at[slab], dst_ref=out_refs[a].at[k], send_sem=send_sems.at[n * a + k],
                    recv_sem=recv_sems.at[n * a + k], device_id=peer, device_id_type=MESH)
                cp.start()
                copies.append(cp)
        for cp in copies:
            cp.wait()

    return pl.pallas_call(
        body, out_shape=[jax.ShapeDtypeStruct((n,) + s.shape[1:], s.dtype) for s in srcs],
        in_specs=[HBM_SPEC] * na, out_specs=[HBM_SPEC] * na,
        scratch_shapes=[pltpu.SemaphoreType.DMA((n * na,)), pltpu.SemaphoreType.DMA((n * na,))], name=name)(*srcs)


def _to_sibling_plan(x, y, c):
    return [(2 * q + (1 - c), (x, y, 1 - c)) for q in range(4)]


def _to_chips_plan(x, y, c):
    q = 2 * x + y
    return [(q ^ 2, (1 - x, y, c)), (q ^ 1, (x, 1 - y, c)), (q ^ 3, (1 - x, 1 - y, c))]


def _row_tile(rows, row_bytes, budget=2 * 2 ** 20, align=2 * SUBLANE):
    if rows * row_bytes <= 2 * budget:
        return rows
    best = None
    for d in range(align, rows + 1, align):
        if rows % d == 0 and d * row_bytes <= budget:
            best = d
    return best or rows


def _pair_add(g, ra, c_idx, *, name):
    _, _, R, C = g.shape
    tr = _row_tile(R, C * 4, budget=3 * 2 ** 20)

    def body(c_ref, g_ref, ra_ref, o_ref):
        del c_ref
        o_ref[...] = (g_ref[0].astype(F32) + ra_ref[...].astype(F32)).astype(o_ref.dtype)

    return pl.pallas_call(
        body,
        grid_spec=pltpu.PrefetchScalarGridSpec(
            num_scalar_prefetch=1, grid=(4, R // tr),
            in_specs=[pl.BlockSpec((1, 1, tr, C), lambda q, r, cr: (q, cr[0], r, 0)),
                      pl.BlockSpec((1, tr, C), lambda q, r, cr: (q, r, 0))],
            out_specs=pl.BlockSpec((1, tr, C), lambda q, r, cr: (q, r, 0))),
        out_shape=jax.ShapeDtypeStruct((4, R, C), g.dtype), name=name,
        compiler_params=_params(("arbitrary", "arbitrary"), 24))(c_idx, g, ra)


def _grad_sum(p, rb, q_idx, *, name):
    _, R, C = p.shape
    tr = _row_tile(R, C * 4, budget=3 * 2 ** 20)

    def body(q_ref, p_ref, rb_ref, o_ref):
        del q_ref
        g = p_ref[0].astype(F32)
        for k in range(3):
            g = g + rb_ref[k].astype(F32)
        o_ref[...] = g

    return pl.pallas_call(
        body,
        grid_spec=pltpu.PrefetchScalarGridSpec(
            num_scalar_prefetch=1, grid=(R // tr,),
            in_specs=[pl.BlockSpec((1, tr, C), lambda r, qr: (qr[0], r, 0)),
                      pl.BlockSpec((3, tr, C), lambda r, qr: (0, r, 0))],
            out_specs=pl.BlockSpec((tr, C), lambda r, qr: (r, 0))),
        out_shape=jax.ShapeDtypeStruct((R, C), F32), name=name,
        compiler_params=_params(("arbitrary",), 40))(q_idx, p, rb)


def _adamw(g, w, m, v):
    m = ADAM_B1 * m + (1.0 - ADAM_B1) * g
    v = ADAM_B2 * v + (1.0 - ADAM_B2) * (g * g)
    m_hat = m / (1.0 - ADAM_B1 ** ADAM_STEP)
    v_hat = v / (1.0 - ADAM_B2 ** ADAM_STEP)
    delta = -ADAM_LR * (m_hat / (jnp.sqrt(v_hat) + ADAM_EPS) + ADAM_WD * w)
    return delta, m, v


def _adam(g, w, m, v, *, name):
    _, R, C = w.shape
    tr = _row_tile(R, C * 4, budget=2 ** 20, align=SUBLANE)

    def body(g_ref, w_ref, m_ref, v_ref, d_out, m_out, v_out):
        delta, mn, vn = _adamw(g_ref[...], w_ref[...], m_ref[...], v_ref[...])
        d_out[...] = delta
        m_out[...] = mn
        v_out[...] = vn

    row = pl.BlockSpec((1, tr, C), lambda r: (0, r, 0))
    o = jax.ShapeDtypeStruct((1, R, C), F32)
    return pl.pallas_call(
        body, grid=(R // tr,), in_specs=[row, row, row, row], out_specs=[row, row, row], out_shape=[o, o, o],
        name=name, compiler_params=_params(("arbitrary",), 32))(g, w, m, v)


def _vmem_specs(n):
    return [pl.BlockSpec(memory_space=pltpu.VMEM)] * n


def _pair_sum_many(mine, theirs, *, name):
    n = len(mine)

    def body(*refs):
        for a in range(n):
            refs[2 * n + a][...] = refs[a][...] + refs[n + a][0]

    return pl.pallas_call(
        body, out_shape=[jax.ShapeDtypeStruct(m.shape, m.dtype) for m in mine], in_specs=_vmem_specs(2 * n),
        out_specs=_vmem_specs(n), name=name)(*mine, *theirs)


def _chip_sum_many(own, recv, q_idx, *, name):
    n = len(own)

    def body(q_ref, *refs):
        q = q_ref[0]
        for a in range(n):
            mine, r = refs[a][...], refs[n + a]
            total = None
            for chip in range(4):
                e = q ^ chip
                term = jnp.where(e == 0, mine, jnp.where(e == 2, r[0], jnp.where(e == 1, r[1], r[2])))
                total = term if total is None else total + term
            refs[2 * n + a][...] = total

    return pl.pallas_call(
        body, out_shape=[jax.ShapeDtypeStruct(m.shape, m.dtype) for m in own],
        in_specs=[pl.BlockSpec(memory_space=pltpu.SMEM)] + _vmem_specs(2 * n), out_specs=_vmem_specs(n),
        name=name)(q_idx, *own, *recv)


def _adam_many(gs, ws, ms, vs, *, name):
    n = len(gs)

    def body(*refs):
        for a in range(n):
            delta, mn, vn = _adamw(*(refs[k * n + a][...] for k in range(4)))
            refs[4 * n + a][...] = delta
            refs[5 * n + a][...] = mn
            refs[6 * n + a][...] = vn

    shapes = [jax.ShapeDtypeStruct(w.shape, w.dtype) for w in ws]
    out = pl.pallas_call(body, out_shape=shapes * 3, in_specs=_vmem_specs(4 * n), out_specs=_vmem_specs(3 * n),
                         name=name)(*gs, *ws, *ms, *vs)
    return out[:n], out[n:2 * n], out[2 * n:]


WEIGHTS = ["mix_norm_w", "w_in", "gate_bias", "gmlp_ln_w", "gmlp_ln_b", "gmlp_ws", "gmlp_bs", "ssm_conv_w",
           "ssm_conv_b", "ssm_dt_bias", "ssm_a_log", "ssm_d", "ssm_norm_w", "w_proj_a", "w_proj_b", "w_out",
           "ffn_norm_w", "ffn_w_up", "ffn_conv_w", "ffn_conv_b", "ffn_w_down", "final_norm_w"]
SHARDED = {"w_in": ((D, IN_COLS), 1), "gate_bias": ((2, D), 1), "ssm_conv_w": ((SK, SXBC), 1),
           "w_proj_a": ((GW, D), 0), "w_proj_b": ((SI, D), 0), "w_out": ((D, D), 0),
           "ffn_w_up": ((D, 2 * DFF), 1), "ffn_conv_w": ((FK, 2 * DFF), 1), "ffn_w_down": ((DFF, D), 0)}
REPLICATED = {"mix_norm_w": (D,), "gmlp_ln_w": (GG, GD), "gmlp_ln_b": (GG, GD), "gmlp_ws": (GG, GB, GB),
              "gmlp_bs": (GG, GB), "ssm_conv_b": (SXBC,), "ssm_dt_bias": (SH,), "ssm_a_log": (SH,), "ssm_d": (SH,),
              "ssm_norm_w": (SI,), "ffn_norm_w": (D,), "ffn_conv_b": (2 * DFF,), "final_norm_w": (D,)}
REPL_ORDER = [n for n in WEIGHTS if n in REPLICATED]
BTILE = 2 * SUBLANE
WIN_R = IN_COLS // NDEV
WIN_P = WIN_R + BTILE - WIN_R % BTILE
WIN_A = [WIN_R * d // BTILE * BTILE for d in range(NDEV)]
assert all(WIN_A[d] + WIN_P >= WIN_R * (d + 1) for d in range(NDEV)) and WIN_A[-1] + WIN_P == IN_COLS
BIG = [("w_proj_a", GW // NDEV, False), ("w_proj_b", SI // NDEV, False), ("w_out", D // NDEV, False),
       ("ffn_w_up", 2 * DFF // NDEV, True), ("ffn_w_down", DFF // NDEV, False), ("w_in", WIN_P, True)]
VECTORS = ["gate_bias", "ssm_conv_w", "ffn_conv_w"]


def _round_up(n, k):
    return (n + k - 1) // k * k


BIG_OFF = {}
_off = 0
for _n, _r, _t in BIG:
    BIG_OFF[_n] = _off
    _off += _r
BIG_USED = _off
BIG_ROWS = _round_up(BIG_USED, 2 * SUBLANE)
assert all(BIG_OFF[n] % (2 * SUBLANE) == 0 for n, _, _ in BIG)
VEC_SHAPE = {n: (SHARDED[n][0][0], SHARDED[n][0][1] // NDEV) for n in VECTORS}


def _win_offset(dev):
    return WIN_R * dev - WIN_R * dev // BTILE * BTILE


def _pack_big(arrs, dtype, dev):
    parts = []
    for n, r, t in BIG:
        a = (arrs[n].T if t else arrs[n]).astype(dtype)
        if n == "w_in":
            a = lax.dynamic_update_slice(jnp.zeros((WIN_P, D), dtype), a, (_win_offset(dev), 0))
        parts.append(a)
    parts.append(jnp.zeros((BIG_ROWS - BIG_USED, D), dtype))
    return jnp.concatenate(parts, axis=0)


def _join_windows(win):
    parts = []
    for d in range(NDEV):
        lo = BTILE if WIN_A[d] % WIN_R else 0
        if lo:
            parts.append(win[d - 1, WIN_P - BTILE:] + win[d, :BTILE])
        hi = WIN_P - BTILE if d + 1 < NDEV and WIN_A[d + 1] < WIN_A[d] + WIN_P else WIN_P
        parts.append(win[d, lo:hi])
    return jnp.concatenate(parts, axis=0)


def _split_windows(main, last):
    assert WIN_A[-2] + WIN_P <= PMAIN
    wins = [main[a:a + WIN_P] for a in WIN_A[:-1]]
    return jnp.stack(wins + [jnp.concatenate([main[WIN_A[-1]:], last], axis=0)])


LATE_ROWS = BIG_OFF["w_in"]
assert LATE_ROWS + WIN_P == BIG_ROWS and BIG[-1][0] == "w_in"


def _remote(src, dst, send_sems, recv_sems, k, to):
    return pltpu.make_async_remote_copy(src_ref=src, dst_ref=dst, send_sem=send_sems.at[k], recv_sem=recv_sems.at[k],
                                        device_id=to, device_id_type=MESH)


class _Exchange:
    def __init__(self, late_shard, c_idx):
        self.late_shard, self.c_idx = late_shard, c_idx

    def gather_start(self):
        shard = self.late_shard

        def make(ins, outs, send_sems, recv_sems):
            (x_ref,), (out,) = ins, outs
            x, y, c = _axes()
            mine = out.at[4 * x + 2 * y + c]
            peers = [(x, y, 1 - c), (1 - x, y, c), (x, 1 - y, c), (1 - x, 1 - y, c)]
            copies = [_remote(x_ref, mine, send_sems, recv_sems, k, p) for k, p in enumerate(peers)]
            return copies + [pltpu.make_async_copy(x_ref, mine, send_sems.at[len(peers)])]

        return _Side([shard], [jax.ShapeDtypeStruct((NDEV,) + shard.shape, shard.dtype)], 5, make)

    def gather_pass_on(self, outs):
        (buf,) = outs

        def make(ins, outs, send_sems, recv_sems):
            (src,), (dst,) = ins, outs
            x, y, c = _axes()
            slots = [4 * px + 2 * py + c for px, py in [(1 - x, y), (x, 1 - y), (1 - x, 1 - y)]]
            return [_remote(src.at[s], dst.at[s], send_sems, recv_sems, k, (x, y, 1 - c)) for k, s in enumerate(slots)]

        return _Side([buf], [jax.ShapeDtypeStruct(buf.shape, buf.dtype)], 3, make, aliases=[(0, 0)])

    def late_weights(self, w, outs):
        (buf,) = outs
        w = dict(w)
        for n, r, t in BIG[:-1]:
            w[n + "_t" if t else n] = buf[:, BIG_OFF[n]:BIG_OFF[n] + r].reshape(NDEV * r, D)
        return w

    @staticmethod
    def _plan_side(src, plan):
        n = len(plan(0, 0, 0))

        def make(ins, outs, send_sems, recv_sems):
            (s,), (dst,) = ins, outs
            return [_remote(s.at[slab], dst.at[k], send_sems, recv_sems, k, peer)
                    for k, (slab, peer) in enumerate(plan(*_axes()))]

        return _Side([src], [jax.ShapeDtypeStruct((n,) + src.shape[1:], src.dtype)], n, make)

    def _to_chips(self, send, sib, tag):
        sums = _pair_add(send.reshape((4, 2) + send.shape[1:]), sib, self.c_idx, name=tag + "_grad_pair_add")
        return sums, self._plan_side(sums, _to_chips_plan)

    def reduce_late(self, grads):
        send = jnp.concatenate([grads[n + "_t" if t else n].reshape(NDEV, r, D) for n, r, t in BIG[:-1]], axis=1)
        send = send.astype(BF16)
        (sib,) = _exchange([send], _to_sibling_plan, name="late_grads_to_sibling")
        self.late_sum, side = self._to_chips(send, sib, "late")
        return side

    def w_in_to_sibling(self, grad_main, grad_dt):
        self.w_in_send = _split_windows(grad_main.astype(BF16), grad_dt.astype(BF16))
        return self._plan_side(self.w_in_send, _to_sibling_plan)

    def reduce_w_in(self, outs):
        self.w_in_sum, side = self._to_chips(self.w_in_send, outs[0], "w_in")
        return side

    def reduced(self, late_outs, w_in_outs):
        (self.late_from_chips,), (self.w_in_from_chips,) = late_outs, w_in_outs


def kernel(x, mix_norm_w, w_in, gate_bias, gmlp_ln_w, gmlp_ln_b, gmlp_ws, gmlp_bs, ssm_conv_w, ssm_conv_b, ssm_dt_bias, ssm_a_log, ssm_d, ssm_norm_w, w_proj_a, w_proj_b, w_out, ffn_norm_w, ffn_w_up, ffn_conv_w, ffn_conv_b, ffn_w_down, final_norm_w, loss_target, m_mix_norm_w, m_w_in, m_gate_bias, m_gmlp_ln_w, m_gmlp_ln_b, m_gmlp_ws, m_gmlp_bs, m_ssm_conv_w, m_ssm_conv_b, m_ssm_dt_bias, m_ssm_a_log, m_ssm_d, m_ssm_norm_w, m_w_proj_a, m_w_proj_b, m_w_out, m_ffn_norm_w, m_ffn_w_up, m_ffn_conv_w, m_ffn_conv_b, m_ffn_w_down, m_final_norm_w, v_mix_norm_w, v_w_in, v_gate_bias, v_gmlp_ln_w, v_gmlp_ln_b, v_gmlp_ws, v_gmlp_bs, v_ssm_conv_w, v_ssm_conv_b, v_ssm_dt_bias, v_ssm_a_log, v_ssm_d, v_ssm_norm_w, v_w_proj_a, v_w_proj_b, v_w_out, v_ffn_norm_w, v_ffn_w_up, v_ffn_conv_w, v_ffn_conv_b, v_ffn_w_down, v_final_norm_w):
    given = dict(locals())
    wts = {n: given[n] for n in WEIGHTS}
    mom = {n: given["m_" + n] for n in WEIGHTS}
    var = {n: given["v_" + n] for n in WEIGHTS}
    xi, yi, ci = _axes()
    c_idx = jnp.reshape(ci, (1,)).astype(jnp.int32)
    q_idx = jnp.reshape(2 * xi + yi, (1,)).astype(jnp.int32)
    big_names = [n for n, _, _ in BIG]
    drop = lambda d, names: {n: d[n][0] for n in names}

    dev = 4 * xi + 2 * yi + ci
    packed = _pack_big(drop(wts, big_names), BF16, dev)
    gathered = _all_gather([packed[LATE_ROWS:]] + [wts[n] for n in VECTORS], name="w_in_all_gather")
    full = {"w_in_t": _join_windows(gathered[0])}
    for n, a in zip(VECTORS, gathered[1:]):
        r, c = VEC_SHAPE[n]
        full[n] = a[:, 0].transpose(1, 0, 2).reshape(r, NDEV * c)
    for n in REPL_ORDER:
        full[n] = wts[n].reshape(REPLICATED[n])

    hooks = _Exchange(packed[:LATE_ROWS], c_idx)
    loss_local, grad_x, grads = _local_step(x[0], loss_target[0], full, hooks)
    g_late = _grad_sum(hooks.late_sum, hooks.late_from_chips, q_idx, name="late_grad_sum")
    g_win = _grad_sum(hooks.w_in_sum, hooks.w_in_from_chips, q_idx, name="w_in_grad_sum")

    small = VECTORS + REPL_ORDER
    as_2d = lambda a: a if a.ndim >= 2 else a[None]
    part = [grads[n].reshape((1,) + SHARDED[n][0] if n in VECTORS else as_2d(wts[n]).shape) for n in small]
    part.append(loss_local)
    from_sibling = _exchange([p[None] for p in part], lambda x, y, c: [(0, (x, y, 1 - c))],
                             name="small_grads_to_sibling")
    chip_sums = _pair_sum_many(part, from_sibling, name="small_grad_pair_sum")
    from_chips = _exchange([s[None] for s in chip_sums],
                           lambda x, y, c: [(0, (1 - x, y, c)), (0, (x, 1 - y, c)), (0, (1 - x, 1 - y, c))],
                           name="small_grads_to_chips")
    totals = _chip_sum_many(chip_sums, from_chips, q_idx, name="small_grad_chip_sum")
    g_small, loss = dict(zip(small, totals)), totals[-1][0, 0]
    for n in VECTORS:
        c = VEC_SHAPE[n][1]
        g_small[n] = lax.dynamic_slice_in_dim(g_small[n], dev * c, c, axis=2)

    outs = {}
    small_g = [g_small[n] for n in small]
    small_out = _adam_many(small_g, *[[as_2d(d[n]) for n in small] for d in (wts, mom, var)], name="adam_small")
    for i, n in enumerate(small):
        outs[n] = tuple(a[i].reshape(wts[n].shape) for a in (small_g,) + tuple(small_out))
    for n, r, t in BIG:
        if n == "w_in":
            g = lax.dynamic_slice(g_win, (_win_offset(dev), 0), (WIN_R, D))
        else:
            g = g_late[BIG_OFF[n]:BIG_OFF[n] + r]
        flip = (lambda a: a.transpose(0, 2, 1)) if t else (lambda a: a)
        g = g[None]
        new = _adam(g, flip(wts[n]), flip(mom[n]), flip(var[n]), name="adam_" + n)
        outs[n] = tuple(flip(a) for a in (g,) + tuple(new))
    return (loss, grad_x[None]) + tuple(outs[n][k] for k in range(4) for n in WEIGHTS)
```

```python
import functools

import jax
import jax.numpy as jnp
from jax import lax
from jax.experimental import pallas as pl
from jax.experimental.pallas import tpu as pltpu

F32 = jnp.float32
BF16 = jnp.bfloat16

D = 1024
EPS = 1e-5
GW = 1024
GB = 128
GG = 8
GD = 128
GCH = 64
SI = 2048
SH = 32
SP = 64
SG = 4
SN = 128
SGW = SI // SG
SK = 4
SXBC = SI + 2 * SG * SN
DFF = 2816
FK = 3
PMAIN = 2 * D + 2 * GW + SI + SXBC
IN_COLS = PMAIN + SH
DP_SSM = SI + SXBC
DP_GAP = (DP_SSM - (2 * D + 2 * GW) % DP_SSM) % DP_SSM
DP_COLS = 2 * D + 2 * GW + DP_GAP + DP_SSM
assert DP_GAP % D == 0 and (2 * D + 2 * GW) % D == 0
NDEV = 8
ADAM_LR, ADAM_B1, ADAM_B2, ADAM_EPS, ADAM_WD, ADAM_STEP = 0.001, 0.9, 0.999, 1e-08, 0.01, 10

LANE = 128
SUBLANE = 8
VMEM_MB_V7X = 64
VMEM_CAP_MB = VMEM_MB_V7X - 8

LS = 128
FT = DFF // 2

NN = (((1,), (0,)), ((), ()))
NT = (((1,), (1,)), ((), ()))
TN = (((0,), (0,)), ((), ()))


def _params(sem, vmem_mb):
    return pltpu.CompilerParams(dimension_semantics=sem,
                                vmem_limit_bytes=min(int(vmem_mb), VMEM_CAP_MB) * 1024 * 1024)


def _dot(a, b, dims=NN):
    return lax.dot_general(a, b, dims, preferred_element_type=F32)


def _sigmoid(x):
    return 1.0 / (1.0 + jnp.exp(-x))


def _split3(v):
    hi = v.astype(BF16)
    r = v - hi.astype(F32)
    mid = r.astype(BF16)
    lo = (r - mid.astype(F32)).astype(BF16)
    return hi, mid, lo


def _dot3(a_f32, b_bf16, dims):
    hi, mid, lo = _split3(a_f32)
    return _dot(hi, b_bf16, dims) + _dot(mid, b_bf16, dims) + _dot(lo, b_bf16, dims)


def _dot2(a_f32, b_bf16, dims):
    hi, mid, _ = _split3(a_f32)
    return _dot(hi, b_bf16, dims) + _dot(mid, b_bf16, dims)


def _dot3_rhs(a_bf16, b_f32, dims):
    hi, mid, lo = _split3(b_f32)
    return _dot(a_bf16, hi, dims) + _dot(a_bf16, mid, dims) + _dot(a_bf16, lo, dims)


def _matmul(a, b, *, name, out_dtype, ta=False, tb=False, tm=1024, tn=1024, tk=1024, add=None,
            j_outer=False, b_rows=None, a_gap=None, side=None):
    gap0, gapw = a_gap or (0, 0)
    if ta:
        K, M = a.shape
        M -= gapw
    else:
        M, K = a.shape
        K -= gapw
    if tb:
        N, K2 = b.shape
        N = b_rows or N
    else:
        K2, N = b.shape
        K2 = b_rows or K2
    assert K == K2, (a.shape, b.shape, ta, tb)
    tm, tn, tk = min(tm, M), min(tn, N), min(tk, K)
    assert M % tm == 0 and N % tn == 0 and K % tk == 0, (M, N, K, tm, tn, tk)
    nk = K // tk
    dims = (((0 if ta else 1,), (1 if tb else 0,)), ((), ()))
    has_add = add is not None
    n_in = 3 if has_add else 2
    s_in = len(side.inputs) if side else 0
    s_out = len(side.out_shapes) if side else 0
    grid = (N // tn, M // tm, nk) if j_outer else (M // tm, N // tn, nk)

    def body(*refs):
        a_ref, b_ref = refs[:2]
        add_ref = refs[2] if has_add else None
        o_ref = refs[n_in + s_in]
        if side:
            side_refs = (refs[n_in:n_in + s_in], refs[n_in + s_in + 1:n_in + s_in + 1 + s_out]) + tuple(refs[-2:])
            ids = [pl.program_id(d) for d in range(3)]
            first = functools.reduce(jnp.logical_and, [i == 0 for i in ids])
            last = functools.reduce(jnp.logical_and, [i == g - 1 for i, g in zip(ids, grid)])

            @pl.when(first)
            def _():
                for cp in side.make(*side_refs):
                    cp.start()

            @pl.when(last)
            def _():
                for cp in side.make(*side_refs):
                    cp.wait()

        p = lax.dot_general(a_ref[...].astype(BF16), b_ref[...].astype(BF16), dims,
                            preferred_element_type=F32)

        def finish(acc):
            if has_add:
                acc = acc + add_ref[...].astype(F32)
            o_ref[...] = acc.astype(o_ref.dtype)

        if nk == 1:
            finish(p)
        else:
            acc_ref = refs[n_in + s_in + 1 + s_out]
            k = pl.program_id(2)

            @pl.when(k == 0)
            def _():
                acc_ref[...] = p

            @pl.when(jnp.logical_and(k > 0, k < nk - 1))
            def _():
                acc_ref[...] += p

            @pl.when(k == nk - 1)
            def _():
                finish(acc_ref[...] + p)

    if j_outer:
        ij = lambda g0, g1: (g1, g0)
    else:
        ij = lambda g0, g1: (g0, g1)

    ta_col = tm if ta else tk
    assert gap0 % ta_col == 0 and gapw % ta_col == 0, (a_gap, ta_col)

    def a_map(g0, g1, k):
        i, _ = ij(g0, g1)
        col = i if ta else k
        col = col + jnp.where(col >= gap0 // ta_col, gapw // ta_col, 0) if gapw else col
        return (k, col) if ta else (i, col)

    def b_map(g0, g1, k):
        _, j = ij(g0, g1)
        return (j, k) if tb else (k, j)

    def o_map(g0, g1, k):
        return ij(g0, g1)

    in_specs = [pl.BlockSpec((tk, tm) if ta else (tm, tk), a_map),
                pl.BlockSpec((tn, tk) if tb else (tk, tn), b_map)]
    args = [a, b]
    if has_add:
        in_specs.append(pl.BlockSpec((tm, tn), o_map))
        args.append(add)
    scratch = [pltpu.VMEM((tm, tn), F32)] if nk > 1 else []
    osz = jnp.dtype(out_dtype).itemsize
    est = (2 * (tm * tk * a.dtype.itemsize + tk * tn * b.dtype.itemsize) + 2 * tm * tn * osz
           + (2 * tm * tn * add.dtype.itemsize if has_add else 0)
           + 3 * tm * tn * 4 + 2 * (tm * tk + tk * tn)) / 2 ** 20 + 4
    out_specs = [pl.BlockSpec((tm, tn), o_map)]
    out_shape = [jax.ShapeDtypeStruct((M, N), out_dtype)]
    aliases = {}
    if side:
        hbm = pl.BlockSpec(memory_space=pltpu.HBM)
        in_specs += [hbm] * s_in
        args += list(side.inputs)
        out_specs += [hbm] * s_out
        out_shape += list(side.out_shapes)
        scratch += [pltpu.SemaphoreType.DMA((side.nsem,)), pltpu.SemaphoreType.DMA((side.nsem,))]
        aliases = {n_in + i: 1 + j for i, j in side.aliases}
    outs = pl.pallas_call(
        body, grid=grid, in_specs=in_specs, out_specs=out_specs, out_shape=out_shape, scratch_shapes=scratch,
        input_output_aliases=aliases, name=name,
        compiler_params=_params(("arbitrary", "arbitrary", "arbitrary"), est))(*args)
    return (outs[0], list(outs[1:])) if side else outs[0]


class _Side:
    def __init__(self, inputs, out_shapes, nsem, make, aliases=()):
        self.inputs, self.out_shapes, self.nsem, self.make, self.aliases = inputs, out_shapes, nsem, make, aliases


def _rms_fwd(x, w, *, name):
    T = x.shape[0]
    tm = min(512, T)

    def body(x_ref, w_ref, o_ref):
        xv = x_ref[...]
        r = lax.rsqrt(jnp.mean(xv * xv, axis=-1, keepdims=True) + EPS)
        o_ref[...] = (xv * r * w_ref[...]).astype(BF16)

    return pl.pallas_call(
        body, grid=(T // tm,),
        in_specs=[pl.BlockSpec((tm, D), lambda i: (i, 0)), pl.BlockSpec((1, D), lambda i: (0, 0))],
        out_specs=pl.BlockSpec((tm, D), lambda i: (i, 0)),
        out_shape=jax.ShapeDtypeStruct((T, D), BF16), name=name,
        compiler_params=_params(("arbitrary",), 24))(x, w)


def _rms_bwd(x, w, dy, dres, *, name):
    T = x.shape[0]
    tm = min(512, T)

    def body(x_ref, w_ref, dy_ref, dres_ref, dx_ref, dw_ref):
        xv = x_ref[...]
        r = lax.rsqrt(jnp.mean(xv * xv, axis=-1, keepdims=True) + EPS)
        xhat = xv * r
        dyv = dy_ref[...].astype(F32)
        g = dyv * w_ref[...]
        dx_ref[...] = dres_ref[...] + r * (g - xhat * jnp.mean(g * xhat, axis=-1, keepdims=True))
        part = jnp.sum(dyv * xhat, axis=0, keepdims=True)

        @pl.when(pl.program_id(0) == 0)
        def _():
            dw_ref[...] = part

        @pl.when(pl.program_id(0) > 0)
        def _():
            dw_ref[...] += part

    row = pl.BlockSpec((tm, D), lambda i: (i, 0))
    vec = pl.BlockSpec((1, D), lambda i: (0, 0))
    return pl.pallas_call(
        body, grid=(T // tm,), in_specs=[row, vec, row, row], out_specs=[row, vec],
        out_shape=[jax.ShapeDtypeStruct((T, D), F32), jax.ShapeDtypeStruct((1, D), F32)], name=name,
        compiler_params=_params(("arbitrary",), 32))(x, w, dy, dres)


def _loss_head(h, tgt, w):
    T = h.shape[0]
    tm = min(512, T)

    def body(h_ref, t_ref, w_ref, loss_ref, dh_ref, dw_ref):
        hv = h_ref[...]
        r = lax.rsqrt(jnp.mean(hv * hv, axis=-1, keepdims=True) + EPS)
        xhat = hv * r
        wv = w_ref[...]
        err = xhat * wv - t_ref[...]
        lpart = 0.5 * jnp.sum(jnp.mean(err * err, axis=-1, keepdims=True), axis=0, keepdims=True)
        dy = err * (1.0 / D)
        g = dy * wv
        dh_ref[...] = r * (g - xhat * jnp.mean(g * xhat, axis=-1, keepdims=True))
        wpart = jnp.sum(dy * xhat, axis=0, keepdims=True)
        lrow = jnp.broadcast_to(lpart, (1, LANE))

        @pl.when(pl.program_id(0) == 0)
        def _():
            dw_ref[...] = wpart
            loss_ref[...] = lrow

        @pl.when(pl.program_id(0) > 0)
        def _():
            dw_ref[...] += wpart
            loss_ref[...] += lrow

    row = pl.BlockSpec((tm, D), lambda i: (i, 0))
    vec = pl.BlockSpec((1, D), lambda i: (0, 0))
    return pl.pallas_call(
        body, grid=(T // tm,), in_specs=[row, row, vec],
        out_specs=[pl.BlockSpec((1, LANE), lambda i: (0, 0)), row, vec],
        out_shape=[jax.ShapeDtypeStruct((1, LANE), F32), jax.ShapeDtypeStruct((T, D), F32),
                   jax.ShapeDtypeStruct((1, D), F32)], name="loss_head",
        compiler_params=_params(("arbitrary",), 32))(h, tgt, w)


_GELU_C = 0.7978845608028654
_GELU_A = 0.044715


def _gelu(x, with_grad=False):
    x2 = x * x
    cx = _GELU_C * x
    t = jnp.tanh(cx * (1.0 + _GELU_A * x2))
    h = 0.5 * (1.0 + t)
    if not with_grad:
        return x * h
    return x * h, h + 0.5 * cx * (1.0 - t * t) * (1.0 + 3.0 * _GELU_A * x2)


def _gmlp_mask():
    r = lax.broadcasted_iota(jnp.int32, (GB, GB), 0) // GCH
    c = lax.broadcasted_iota(jnp.int32, (GB, GB), 1) // GCH
    return c <= r


def _gmlp_fwd(proj, lnw, lnb, ws, bst):
    T = proj.shape[0]
    tm = min(512, T)
    nblk = tm // GB

    def body(u_ref, v_ref, lnw_ref, lnb_ref, ws_ref, bst_ref, o_ref):
        mask = _gmlp_mask()
        u = _gelu(u_ref[...].astype(F32))
        v = _gelu(v_ref[...].astype(F32))
        for g in range(GG):
            cs = slice(g * GD, (g + 1) * GD)
            vg = v[:, cs]
            mu = jnp.mean(vg, axis=-1, keepdims=True)
            vc = vg - mu
            var = jnp.mean(vc * vc, axis=-1, keepdims=True)
            vn = (vc * lax.rsqrt(var + EPS) * lnw_ref[g:g + 1, :] + lnb_ref[g:g + 1, :]).astype(BF16)
            wsg = jnp.where(mask, ws_ref[g], 0.0).astype(BF16)
            bcol = bst_ref[:, g:g + 1]
            for blk in range(nblk):
                rs = slice(blk * GB, (blk + 1) * GB)
                sv = _dot(wsg, vn[rs, :]) + bcol
                o_ref[rs, cs] = (u[rs, cs] * sv).astype(BF16)

    full = lambda shape: pl.BlockSpec(shape, lambda i: tuple(0 for _ in shape))
    return pl.pallas_call(
        body, grid=(T // tm,),
        in_specs=[pl.BlockSpec((tm, GW), lambda i: (i, 2)), pl.BlockSpec((tm, GW), lambda i: (i, 3)),
                  full((GG, GD)), full((GG, GD)), full((GG, GB, GB)), full((GB, GG))],
        out_specs=pl.BlockSpec((tm, GW), lambda i: (i, 0)),
        out_shape=jax.ShapeDtypeStruct((T, GW), BF16), name="gmlp_fwd",
        compiler_params=_params(("arbitrary",), 40))(proj, proj, lnw, lnb, ws, bst)


def _gmlp_bwd(proj, dya, dproj, lnw, lnb, ws, bst):
    T = proj.shape[0]
    tm = min(512, T)
    nblk = tm // GB

    def body(u_ref, v_ref, dya_ref, dproj_in, lnw_ref, lnb_ref, ws_ref, bst_ref,
             dz_ref, dlnw_ref, dlnb_ref, dws_ref, dbst_ref):
        del dproj_in
        first = pl.program_id(0) == 0

        @pl.when(first)
        def _():
            dlnw_ref[...] = jnp.zeros_like(dlnw_ref)
            dlnb_ref[...] = jnp.zeros_like(dlnb_ref)
            dws_ref[...] = jnp.zeros_like(dws_ref)
            dbst_ref[...] = jnp.zeros_like(dbst_ref)

        mask = _gmlp_mask()
        lane = lax.broadcasted_iota(jnp.int32, (GB, LANE), 1)
        ur = u_ref[...].astype(F32)
        vr = v_ref[...].astype(F32)
        u, gu = _gelu(ur, with_grad=True)
        v, gv = _gelu(vr, with_grad=True)
        dy = dya_ref[...].astype(F32)
        dbst = jnp.zeros((GB, LANE), F32)
        dlnw_rows, dlnb_rows = [], []
        for g in range(GG):
            cs = slice(g * GD, (g + 1) * GD)
            vg = v[:, cs]
            mu = jnp.mean(vg, axis=-1, keepdims=True)
            vc = vg - mu
            var = jnp.mean(vc * vc, axis=-1, keepdims=True)
            rstd = lax.rsqrt(var + EPS)
            xhat = vc * rstd
            lw = lnw_ref[g:g + 1, :]
            vn = (xhat * lw + lnb_ref[g:g + 1, :]).astype(BF16)
            wsg = jnp.where(mask, ws_ref[g], 0.0).astype(BF16)
            bcol = bst_ref[:, g:g + 1]
            dyg = dy[:, cs]
            ug = u[:, cs]
            dsv = dyg * ug
            dsv_b = dsv.astype(BF16)
            dws_g = jnp.zeros((GB, GB), F32)
            bsum = jnp.zeros((GB, 1), F32)
            dvn_parts = []
            for blk in range(nblk):
                rs = slice(blk * GB, (blk + 1) * GB)
                sv = _dot(wsg, vn[rs, :]) + bcol
                dz_ref[rs, cs] = (dyg[rs, :] * sv * gu[rs, cs]).astype(BF16)
                dws_g = dws_g + _dot(dsv_b[rs, :], vn[rs, :], NT)
                bsum = bsum + jnp.sum(dsv[rs, :], axis=-1, keepdims=True)
                dvn_parts.append(_dot(wsg, dsv_b[rs, :], TN))
            dvn = jnp.concatenate(dvn_parts, axis=0)
            dws_ref[g] += jnp.where(mask, dws_g, 0.0)
            dbst = dbst + jnp.where(lane == g, bsum, 0.0)
            dlnw_rows.append(jnp.sum(dvn * xhat, axis=0, keepdims=True))
            dlnb_rows.append(jnp.sum(dvn, axis=0, keepdims=True))
            dxh = dvn * lw
            dvg = rstd * (dxh - jnp.mean(dxh, axis=-1, keepdims=True)
                          - xhat * jnp.mean(dxh * xhat, axis=-1, keepdims=True))
            dz_ref[:, GW + g * GD:GW + (g + 1) * GD] = (dvg * gv[:, cs]).astype(BF16)
        dlnw_ref[...] += jnp.concatenate(dlnw_rows, axis=0)
        dlnb_ref[...] += jnp.concatenate(dlnb_rows, axis=0)
        dbst_ref[...] += dbst

    full = lambda shape: pl.BlockSpec(shape, lambda i: tuple(0 for _ in shape))
    outs = pl.pallas_call(
        body, grid=(T // tm,),
        in_specs=[pl.BlockSpec((tm, GW), lambda i: (i, 2)), pl.BlockSpec((tm, GW), lambda i: (i, 3)),
                  pl.BlockSpec((tm, GW), lambda i: (i, 0)), pl.BlockSpec(memory_space=pl.ANY),
                  full((GG, GD)), full((GG, GD)), full((GG, GB, GB)), full((GB, GG))],
        out_specs=[pl.BlockSpec((tm, 2 * GW), lambda i: (i, 1)), full((GG, GD)), full((GG, GD)),
                   full((GG, GB, GB)), full((GB, LANE))],
        out_shape=[jax.ShapeDtypeStruct(dproj.shape, dproj.dtype), jax.ShapeDtypeStruct((GG, GD), F32),
                   jax.ShapeDtypeStruct((GG, GD), F32), jax.ShapeDtypeStruct((GG, GB, GB), F32),
                   jax.ShapeDtypeStruct((GB, LANE), F32)],
        input_output_aliases={3: 0}, name="gmlp_bwd",
        compiler_params=_params(("arbitrary",), 48))(proj, proj, dya, dproj, lnw, lnb, ws, bst)
    return outs


def _merge_fwd(ya_pre, yb_pre, proj, bias, wpa, wpb):
    T = proj.shape[0]
    tm = min(512, T)

    def body(ya_ref, yb_ref, g_ref, b_ref, wpa_ref, wpb_ref, m_ref, oa_ref, ob_ref):
        ya = _dot(ya_ref[...], wpa_ref[...])
        yb = _dot(yb_ref[...], wpb_ref[...])
        g = g_ref[...].astype(F32)
        sa = _sigmoid(g[:, :D] + b_ref[0:1, :])
        sb = _sigmoid(g[:, D:] + b_ref[1:2, :])
        m_ref[...] = (sa * ya + sb * yb).astype(BF16)
        oa_ref[...] = ya.astype(BF16)
        ob_ref[...] = yb.astype(BF16)

    row = lambda w: pl.BlockSpec((tm, w), lambda i: (i, 0))
    full = lambda shape: pl.BlockSpec(shape, lambda i: tuple(0 for _ in shape))
    o = jax.ShapeDtypeStruct((T, D), BF16)
    return pl.pallas_call(
        body, grid=(T // tm,),
        in_specs=[row(GW), row(SI), row(2 * D), full((2, D)), full((GW, D)), full((SI, D))],
        out_specs=[row(D), row(D), row(D)], out_shape=[o, o, o], name="merge_fwd",
        compiler_params=_params(("arbitrary",), 40))(ya_pre, yb_pre, proj, bias, wpa, wpb)


def _merge_bwd(dm, proj, bias, ya, yb, wpa, wpb):
    T = proj.shape[0]
    tm = min(512, T)

    def body(dm_ref, g_ref, b_ref, ya_ref, yb_ref, wpa_ref, wpb_ref,
             dg_ref, dya_ref, dyb_ref, dpa_ref, dpb_ref, db_ref):
        dmv = dm_ref[...].astype(F32)
        g = g_ref[...].astype(F32)
        sa = _sigmoid(g[:, :D] + b_ref[0:1, :])
        sb = _sigmoid(g[:, D:] + b_ref[1:2, :])
        dya = (dmv * sa).astype(BF16)
        dyb = (dmv * sb).astype(BF16)
        dga = dmv * ya_ref[...].astype(F32) * sa * (1.0 - sa)
        dgb = dmv * yb_ref[...].astype(F32) * sb * (1.0 - sb)
        dg_ref[:, :D] = dga.astype(BF16)
        dg_ref[:, D:] = dgb.astype(BF16)
        dya_ref[...] = dya
        dyb_ref[...] = dyb
        dpa_ref[...] = _dot(dya, wpa_ref[...], NT).astype(BF16)
        dpb_ref[...] = _dot(dyb, wpb_ref[...], NT).astype(BF16)
        part = jnp.concatenate([jnp.sum(dga, axis=0, keepdims=True), jnp.sum(dgb, axis=0, keepdims=True)], axis=0)

        @pl.when(pl.program_id(0) == 0)
        def _():
            db_ref[...] = part

        @pl.when(pl.program_id(0) > 0)
        def _():
            db_ref[...] += part

    row = lambda w: pl.BlockSpec((tm, w), lambda i: (i, 0))
    full = lambda shape: pl.BlockSpec(shape, lambda i: tuple(0 for _ in shape))
    o = lambda w: jax.ShapeDtypeStruct((T, w), BF16)
    return pl.pallas_call(
        body, grid=(T // tm,),
        in_specs=[row(D), row(2 * D), full((2, D)), row(D), row(D), full((GW, D)), full((SI, D))],
        out_specs=[row(2 * D), row(D), row(D), row(GW), row(SI), full((2, D))],
        out_shape=[o(DP_COLS), o(D), o(D), o(GW), o(SI), jax.ShapeDtypeStruct((2, D), F32)], name="merge_bwd",
        compiler_params=_params(("arbitrary",), 48))(dm, proj, bias, ya, yb, wpa, wpb)


RB = 128


def _shift_matrix(j):
    r = lax.broadcasted_iota(jnp.int32, (RB, RB), 0)
    c = lax.broadcasted_iota(jnp.int32, (RB, RB), 1)
    return jnp.where(c == r - j, 1.0, 0.0).astype(BF16)


def _rows_down(xb, before, shifts):
    H = SUBLANE
    mats = [_shift_matrix(j) for j in shifts]
    outs = [[] for _ in shifts]
    for b in range(xb.shape[0] // RB):
        blk = xb[b * RB:(b + 1) * RB]
        edge = jnp.concatenate([before, blk[:2 * H].astype(F32)[:H]], axis=0)
        for i, j in enumerate(shifts):
            outs[i] += [edge[H - j:2 * H - j], _dot(mats[i], blk)[H:]]
        before = blk[RB - 2 * H:].astype(F32)[H:]
    return [jnp.concatenate(o, axis=0) for o in outs]


def _rows_up(xb, after, shifts):
    H = SUBLANE
    nb = xb.shape[0] // RB
    mats = [_shift_matrix(-j) for j in shifts]
    outs = [[] for _ in shifts]
    for b in range(nb):
        blk = xb[b * RB:(b + 1) * RB]
        nxt = xb[(b + 1) * RB:(b + 1) * RB + 2 * H].astype(F32)[:H] if b + 1 < nb else after
        edge = jnp.concatenate([blk[RB - 2 * H:].astype(F32)[H:], nxt], axis=0)
        for i, j in enumerate(shifts):
            outs[i] += [_dot(mats[i], blk)[:RB - H], edge[j:H + j]]
    return [jnp.concatenate(o, axis=0) for o in outs]


def _ffn_act_fwd(up, cw, cb):
    T = up.shape[0]
    tm = min(512, T)
    H = SUBLANE

    def body(up_ref, cw_ref, cb_ref, o_ref, xc_ref, halo):
        @pl.when(pl.program_id(1) == 0)
        def _():
            halo[...] = jnp.zeros_like(halo)

        xb = up_ref[...]
        x2, x1 = _rows_down(xb, halo[...], (2, 1))
        xc = cb_ref[...] + cw_ref[0:1, :] * x2 + cw_ref[1:2, :] * x1 + cw_ref[2:3, :] * xb.astype(F32)
        xc_ref[...] = xc.astype(BF16)
        gate = xc[:, :FT]
        o_ref[...] = (gate * _sigmoid(gate) * xc[:, FT:]).astype(BF16)
        halo[...] = xb[tm - 2 * H:].astype(F32)[H:]

    tile = pl.BlockSpec((tm, 2 * FT), lambda j, i: (i, j))
    return pl.pallas_call(
        body, grid=(2, T // tm),
        in_specs=[tile, pl.BlockSpec((FK, 2 * FT), lambda j, i: (0, j)), pl.BlockSpec((1, 2 * FT), lambda j, i: (0, j))],
        out_specs=[pl.BlockSpec((tm, FT), lambda j, i: (i, j)), tile],
        out_shape=[jax.ShapeDtypeStruct((T, DFF), BF16), jax.ShapeDtypeStruct((T, 2 * DFF), BF16)],
        scratch_shapes=[pltpu.VMEM((H, 2 * FT), F32)], name="ffn_act_fwd",
        compiler_params=_params(("arbitrary", "arbitrary"), 48))(up, cw, cb)


def _ffn_act_bwd(up, xc, dact, cw):
    T = up.shape[0]
    tm = min(512, T)
    nt = T // tm
    H = SUBLANE

    def body(up_ref, xc_ref, da_ref, cw_ref, dup_ref, dcw_ref, dcb_ref, ahead):
        @pl.when(pl.program_id(1) == 0)
        def _():
            ahead[...] = jnp.zeros_like(ahead)
            dcw_ref[...] = jnp.zeros_like(dcw_ref)
            dcb_ref[...] = jnp.zeros_like(dcb_ref)

        xcv = xc_ref[...].astype(F32)
        gate, val = xcv[:, :FT], xcv[:, FT:]
        sg = _sigmoid(gate)
        dav = da_ref[...].astype(F32)
        dgate = dav * val * sg * (1.0 + gate * (1.0 - sg))
        dval = dav * gate * sg
        dxc = jnp.concatenate([dgate, dval], axis=1)
        d1, d2 = _rows_up(dxc.astype(BF16), ahead[...], (1, 2))
        x = up_ref[...].astype(F32)
        dcb_ref[...] += jnp.sum(dxc, axis=0, keepdims=True)
        dcw_ref[...] += jnp.concatenate([jnp.sum(d * x, axis=0, keepdims=True) for d in (d2, d1, dxc)], axis=0)
        dup_ref[...] = (cw_ref[2:3, :] * dxc + cw_ref[1:2, :] * d1 + cw_ref[0:1, :] * d2).astype(BF16)
        ahead[...] = dxc[0:H, :]

    tile = pl.BlockSpec((tm, 2 * FT), lambda j, i: (nt - 1 - i, j))
    return pl.pallas_call(
        body, grid=(2, nt),
        in_specs=[tile, tile, pl.BlockSpec((tm, FT), lambda j, i: (nt - 1 - i, j)),
                  pl.BlockSpec((FK, 2 * FT), lambda j, i: (0, j))],
        out_specs=[tile, pl.BlockSpec((FK, 2 * FT), lambda j, i: (0, j)), pl.BlockSpec((1, 2 * FT), lambda j, i: (0, j))],
        out_shape=[jax.ShapeDtypeStruct((T, 2 * DFF), BF16), jax.ShapeDtypeStruct((FK, 2 * DFF), F32),
                   jax.ShapeDtypeStruct((1, 2 * DFF), F32)],
        scratch_shapes=[pltpu.VMEM((H, 2 * FT), F32)], name="ffn_act_bwd",
        compiler_params=_params(("arbitrary", "arbitrary"), 56))(up, xc, dact, cw)


def _softplus(x):
    e = jnp.exp(-jnp.abs(x))
    return jnp.maximum(x, 0.0) + jnp.where(e < 1e-4, e * (1.0 - 0.5 * e), jnp.log(1.0 + e))


def _ssd_consts():
    li = lax.broadcasted_iota(jnp.int32, (LS, LS), 0)
    si = lax.broadcasted_iota(jnp.int32, (LS, LS), 1)
    tril = si <= li
    hh = lax.broadcasted_iota(jnp.int32, (LANE, SI), 0)
    cc = lax.broadcasted_iota(jnp.int32, (LANE, SI), 1) // SP
    expand = jnp.where(hh == cc, 1.0, 0.0).astype(BF16)
    return tril, expand


def _ssd_pre(xc, dt_ref, dtb_ref, alog_ref, tril, expand):
    sx = _sigmoid(xc)
    xbc = xc * sx
    xs, bm, cm = xbc[:, :SI], xbc[:, SI:SI + SG * SN], xbc[:, SI + SG * SN:]
    dtin = dt_ref[...] + dtb_ref[...]
    dt = _softplus(dtin)
    a_neg = -jnp.exp(alog_ref[...])
    dta = dt * a_neg
    trilb = jnp.where(tril, 1.0, 0.0).astype(BF16)
    a = _dot3_rhs(trilb, dta, NN)
    a_exp = _dot3(a, expand, NN)
    dt_exp = _dot2(dt, expand, NN)
    xdt = xs * dt_exp
    a_last = a_exp[LS - 1:LS, :]
    return dict(xc=xc, sx=sx, xs=xs, bm=bm, cm=cm, dtin=dt_ref[...] + dtb_ref[...], dt=dt, a_neg=a_neg,
                a=a, a_t=a.T, a_exp=a_exp, dt_exp=dt_exp, xdt=xdt, ea=jnp.exp(a_exp),
                w=jnp.exp(a_last - a_exp), eal=jnp.exp(a_last))


def _head_decay(pre, tril, h):
    seg = pre["a"][:, h:h + 1] - pre["a_t"][h:h + 1, :]
    return jnp.exp(jnp.where(tril, seg, -1e30))


def _ssd_fwd(proj, dtraw, cw, cb, dtb, alog, dexp, nw):
    T = proj.shape[0]
    nc = T // LS
    H = SUBLANE

    def body(z_ref, x_ref, dt_ref, cw_ref, cb_ref, dtb_ref, alog_ref, dexp_ref, nw_ref,
             yb_ref, y_ref, sp_ref, xc_ref, halo, st):
        @pl.when(pl.program_id(0) == 0)
        def _():
            halo[...] = jnp.zeros_like(halo)
            st[...] = jnp.zeros_like(st)

        xb = x_ref[...]
        taps = _rows_down(xb, halo[...], (3, 2, 1)) + [xb.astype(F32)]
        xc = cb_ref[...]
        for k in range(SK):
            xc = xc + cw_ref[k:k + 1, :] * taps[k]
        xc_ref[...] = xc.astype(BF16)
        tril, expand = _ssd_consts()
        pre = _ssd_pre(xc, dt_ref, dtb_ref, alog_ref, tril, expand)
        lane = lax.broadcasted_iota(jnp.int32, (LS, LANE), 1)
        lo = lane < SP
        zf = z_ref[...].astype(F32)
        siluz = zf * _sigmoid(zf)
        for g in range(SG):
            gs = slice(g * SGW, (g + 1) * SGW)
            bg = pre["bm"][:, g * SN:(g + 1) * SN].astype(BF16)
            cg = pre["cm"][:, g * SN:(g + 1) * SN].astype(BF16)
            gmat = _dot(cg, bg, NT)
            sg = st[g]
            sp_ref[0, g] = sg
            yoff = _dot(cg, sg.astype(BF16))
            parts = []
            for j in range(SGW // LANE):
                h0 = g * (SGW // SP) + 2 * j
                m0 = gmat * _head_decay(pre, tril, h0)
                m1 = gmat * _head_decay(pre, tril, h0 + 1)
                xp = pre["xdt"][:, g * SGW + j * LANE:g * SGW + (j + 1) * LANE]
                rhs = jnp.concatenate([jnp.where(lo, xp, 0.0), jnp.where(lo, 0.0, xp)], axis=0).astype(BF16)
                parts.append(_dot(jnp.concatenate([m0, m1], axis=1).astype(BF16), rhs))
            y = (jnp.concatenate(parts, axis=1) + pre["ea"][:, gs] * yoff + dexp_ref[:, gs] * pre["xs"][:, gs])
            st[g] = pre["eal"][:, gs] * sg + _dot(bg, (pre["w"][:, gs] * pre["xdt"][:, gs]).astype(BF16), TN)
            y_ref[:, gs] = y
            yg = y * siluz[:, gs]
            r = lax.rsqrt(jnp.mean(yg * yg, axis=-1, keepdims=True) + EPS)
            yb_ref[:, gs] = (yg * r * nw_ref[:, gs]).astype(BF16)
        halo[...] = xb[LS - 2 * H:].astype(F32)[H:]

    vec = lambda w: pl.BlockSpec((1, w), lambda c: (0, 0))
    return pl.pallas_call(
        body, grid=(nc,),
        in_specs=[pl.BlockSpec((LS, SI), lambda c: (c, 2)), pl.BlockSpec((LS, SXBC), lambda c: (c, 2)),
                  pl.BlockSpec((LS, LANE), lambda c: (c, 0)),
                  pl.BlockSpec((SK, SXBC), lambda c: (0, 0)), vec(SXBC), vec(LANE), vec(LANE), vec(SI), vec(SI)],
        out_specs=[pl.BlockSpec((LS, SI), lambda c: (c, 0)), pl.BlockSpec((LS, SI), lambda c: (c, 0)),
                   pl.BlockSpec((1, SG, SN, SGW), lambda c: (c, 0, 0, 0)), pl.BlockSpec((LS, SXBC), lambda c: (c, 0))],
        out_shape=[jax.ShapeDtypeStruct((T, SI), BF16), jax.ShapeDtypeStruct((T, SI), F32),
                   jax.ShapeDtypeStruct((nc, SG, SN, SGW), F32), jax.ShapeDtypeStruct((T, SXBC), BF16)],
        scratch_shapes=[pltpu.VMEM((H, SXBC), F32), pltpu.VMEM((SG, SN, SGW), F32)], name="ssd_fwd",
        compiler_params=_params(("arbitrary",), VMEM_CAP_MB))(proj, proj, dtraw, cw, cb, dtb, alog, dexp, nw)


def _ssd_bwd(proj, xcs, dtraw, y, sprev, dyb, dproj, cw, dtb, alog, dexp, nw):
    T = proj.shape[0]
    nc = T // LS
    H = SUBLANE
    NJ = 1

    def body(z_ref, x_ref, xc_ref, dt_ref, y_ref, sp_ref, dyb_ref, dproj_in,
             cw_ref, dtb_ref, alog_ref, dexp_ref, nw_ref,
             dp_ref, ddt_ref, dcw_ref, dcb_ref, ddtb_ref, da_ref, dd_ref, dnw_ref,
             ahead, ds, stage):
        del dproj_in
        i = pl.program_id(0)
        j = pl.program_id(1)

        @pl.when(jnp.logical_and(i == 0, j == 0))
        def _():
            ahead[...] = jnp.zeros_like(ahead)
            ds[...] = jnp.zeros_like(ds)
            for r in (dcw_ref, dcb_ref, ddtb_ref, da_ref, dd_ref, dnw_ref):
                r[...] = jnp.zeros_like(r)

        @pl.when(j == 0)
        def _():
            tril, expand = _ssd_consts()
            pre = _ssd_pre(xc_ref[...].astype(F32), dt_ref, dtb_ref, alog_ref, tril, expand)
            lane = lax.broadcasted_iota(jnp.int32, (LS, LANE), 1)
            sub = lax.broadcasted_iota(jnp.int32, (LANE, LS), 0)
            rowi = lax.broadcasted_iota(jnp.int32, (LS, 1), 0)
            lo = lane < SP
            xs, xdt, ea, w, eal = pre["xs"], pre["xdt"], pre["ea"], pre["w"], pre["eal"]

            zf = z_ref[...].astype(F32)
            sz = _sigmoid(zf)
            siluz = zf * sz
            yv = y_ref[...]
            yg = yv * siluz
            dout = dyb_ref[...].astype(F32)
            dyg_parts, dnw_parts = [], []
            for g in range(SG):
                gs = slice(g * SGW, (g + 1) * SGW)
                ygg = yg[:, gs]
                r = lax.rsqrt(jnp.mean(ygg * ygg, axis=-1, keepdims=True) + EPS)
                yhat = ygg * r
                dn = dout[:, gs] * nw_ref[:, gs]
                dnw_parts.append(jnp.sum(dout[:, gs] * yhat, axis=0, keepdims=True))
                dyg_parts.append(r * (dn - yhat * jnp.mean(dn * yhat, axis=-1, keepdims=True)))
            dyg = jnp.concatenate(dyg_parts, axis=1)
            dnw_ref[...] += jnp.concatenate(dnw_parts, axis=1)
            dy = dyg * siluz
            stage[:, 0:SI] = (dyg * yv * sz * (1.0 + zf * (1.0 - sz))).astype(BF16)
            dd_ref[...] += jnp.sum(dy * xs, axis=0, keepdims=True)
            tt = ea * dy

            da_rows = jnp.zeros((LS, LANE), F32)
            da_cols = jnp.zeros((LANE, LS), F32)
            dxdt_parts, db_parts, dc_parts, daexp_parts = [], [], [], []
            for g in range(SG):
                gs = slice(g * SGW, (g + 1) * SGW)
                bg = pre["bm"][:, g * SN:(g + 1) * SN].astype(BF16)
                cg = pre["cm"][:, g * SN:(g + 1) * SN].astype(BF16)
                sg = sp_ref[0, g]
                sgb = sg.astype(BF16)
                dsg = ds[g]
                dsgb = dsg.astype(BF16)
                ttg = tt[:, gs].astype(BF16)
                yoff = _dot(cg, sgb)
                dc = _dot(ttg, sgb, NT)
                gmat = _dot(cg, bg, NT)
                dgm = jnp.zeros((LS, LS), F32)
                dxdt_pairs = []
                for jj in range(SGW // LANE):
                    h0 = g * (SGW // SP) + 2 * jj
                    ps = slice(g * SGW + jj * LANE, g * SGW + (jj + 1) * LANE)
                    l0 = _head_decay(pre, tril, h0)
                    l1 = _head_decay(pre, tril, h0 + 1)
                    m0 = gmat * l0
                    m1 = gmat * l1
                    dyp = dy[:, ps]
                    dy_lo = jnp.where(lo, dyp, 0.0).astype(BF16)
                    dy_hi = jnp.where(lo, 0.0, dyp).astype(BF16)
                    xpb = xdt[:, ps].astype(BF16)
                    dm0 = _dot(dy_lo, xpb, NT)
                    dm1 = _dot(dy_hi, xpb, NT)
                    q0 = dm0 * m0
                    q1 = dm1 * m1
                    da_rows = da_rows + jnp.where(lane == h0, jnp.sum(q0, axis=1, keepdims=True), 0.0)
                    da_rows = da_rows + jnp.where(lane == h0 + 1, jnp.sum(q1, axis=1, keepdims=True), 0.0)
                    da_cols = da_cols + jnp.where(sub == h0, jnp.sum(q0, axis=0, keepdims=True), 0.0)
                    da_cols = da_cols + jnp.where(sub == h0 + 1, jnp.sum(q1, axis=0, keepdims=True), 0.0)
                    dgm = dgm + dm0 * l0 + dm1 * l1
                    mcat = jnp.concatenate([m0, m1], axis=0).astype(BF16)
                    dycat = jnp.concatenate([dy_lo, dy_hi], axis=0)
                    dxdt_pairs.append(_dot(mcat, dycat, TN))
                dgb = dgm.astype(BF16)
                dc = dc + _dot(dgb, bg)
                db = _dot(dgb, cg, TN)
                zg = _dot(bg, dsgb)
                wg, xdtg = w[:, gs], xdt[:, gs]
                dxdt_g = jnp.concatenate(dxdt_pairs, axis=1) + wg * zg
                qg = zg * xdtg * wg
                last = (jnp.sum(qg, axis=0, keepdims=True)
                        + jnp.sum(dsg * sg, axis=0, keepdims=True) * eal[:, gs])
                daexp_parts.append(dy[:, gs] * ea[:, gs] * yoff - qg + jnp.where(rowi == LS - 1, last, 0.0))
                db = db + _dot((wg * xdtg).astype(BF16), dsgb, NT)
                ds[g] = eal[:, gs] * dsg + _dot(cg, ttg, TN)
                dxdt_parts.append(dxdt_g)
                db_parts.append(db)
                dc_parts.append(dc)
            dxdt = jnp.concatenate(dxdt_parts, axis=1)
            da_exp = jnp.concatenate(daexp_parts, axis=1)
            da = _dot2(da_exp, expand, NT) + da_rows - da_cols.T
            triub = jnp.where(tril, 1.0, 0.0).astype(BF16)
            ddta = _dot3_rhs(triub, da, TN)
            ddt = ddta * pre["a_neg"] + _dot2(dxdt * xs, expand, NT)
            da_ref[...] += jnp.sum(ddta * pre["dt"], axis=0, keepdims=True)
            ddt_raw = ddt * _sigmoid(pre["dtin"])
            ddt_ref[...] = ddt_raw
            ddtb_ref[...] += jnp.sum(ddt_raw, axis=0, keepdims=True)
            dxs = dexp_ref[...] * dy + dxdt * pre["dt_exp"]
            dxbc = jnp.concatenate([dxs] + db_parts + dc_parts, axis=1)
            sx, xc = pre["sx"], pre["xc"]
            dxc = dxbc * sx * (1.0 + xc * (1.0 - sx))
            taps = _rows_up(dxc.astype(BF16), ahead[...], (3, 2, 1)) + [dxc]
            xr = x_ref[...].astype(F32)
            dcb_ref[...] += jnp.sum(dxc, axis=0, keepdims=True)
            dcw_ref[...] += jnp.concatenate([jnp.sum(t * xr, axis=0, keepdims=True) for t in taps], axis=0)
            dxr = cw_ref[0:1, :] * taps[0]
            for k in range(1, SK):
                dxr = dxr + cw_ref[k:k + 1, :] * taps[k]
            stage[:, SI:] = dxr.astype(BF16)
            ahead[...] = dxc[0:H, :]

        dp_ref[...] = stage[...]

    vec = lambda w: pl.BlockSpec((1, w), lambda i, j: (0, 0))
    rev = lambda w, cb_: pl.BlockSpec((LS, w), lambda i, j: (nc - 1 - i, cb_))
    outs = pl.pallas_call(
        body, grid=(nc, NJ),
        in_specs=[rev(SI, 2), rev(SXBC, 2), rev(SXBC, 0),
                  rev(LANE, 0), rev(SI, 0),
                  pl.BlockSpec((1, SG, SN, SGW), lambda i, j: (nc - 1 - i, 0, 0, 0)),
                  rev(SI, 0), pl.BlockSpec(memory_space=pl.ANY),
                  pl.BlockSpec((SK, SXBC), lambda i, j: (0, 0)), vec(LANE), vec(LANE), vec(SI), vec(SI)],
        out_specs=[rev(DP_SSM, 1), rev(LANE, 0),
                   pl.BlockSpec((SK, SXBC), lambda i, j: (0, 0)), vec(SXBC), vec(LANE), vec(LANE), vec(SI), vec(SI)],
        out_shape=[jax.ShapeDtypeStruct(dproj.shape, dproj.dtype), jax.ShapeDtypeStruct((T, LANE), F32),
                   jax.ShapeDtypeStruct((SK, SXBC), F32), jax.ShapeDtypeStruct((1, SXBC), F32),
                   jax.ShapeDtypeStruct((1, LANE), F32), jax.ShapeDtypeStruct((1, LANE), F32),
                   jax.ShapeDtypeStruct((1, SI), F32), jax.ShapeDtypeStruct((1, SI), F32)],
        scratch_shapes=[pltpu.VMEM((H, SXBC), F32),
                        pltpu.VMEM((SG, SN, SGW), F32), pltpu.VMEM((LS, SI + SXBC), BF16)],
        input_output_aliases={7: 0}, name="ssd_bwd",
        compiler_params=_params(("arbitrary", "arbitrary"), VMEM_CAP_MB))(
            proj, proj, xcs, dtraw, y, sprev, dyb, dproj, cw, dtb, alog, dexp, nw)
    return outs


def _perm_ffn_cols(a):
    lead = a.shape[:-1]
    return a.reshape(lead + (2, 2, FT)).swapaxes(-3, -2).reshape(lead + (2 * DFF,))


def _perm_ffn_rows(a):
    return a.reshape((2, 2, FT) + a.shape[1:]).swapaxes(0, 1).reshape(a.shape)


def _pad_lanes(v, n=LANE):
    return jnp.pad(v, ((0, 0), (0, n - v.shape[-1])))


LATE = ["w_proj_a", "w_proj_b", "w_out", "ffn_w_up_t", "ffn_w_down"]
WGRAD = BF16


class _NoExchange:
    def gather_start(self):
        return None

    def gather_pass_on(self, outs):
        return None

    def late_weights(self, w, outs):
        return w

    def reduce_late(self, grads):
        return None

    def w_in_to_sibling(self, grad_main, grad_dt):
        return None

    def reduce_w_in(self, outs):
        return None

    def reduced(self, late_outs, w_in_outs):
        pass


def _local_step(x, tgt, w, hooks=None):
    hooks = hooks or _NoExchange()

    def mm(*args, side=None, **kw):
        out = _matmul(*args, side=side, **kw)
        return out if side is not None else (out, [])

    win_t = w["w_in_t"]
    win_dt = jnp.pad(w["w_in_t"][PMAIN:], ((0, LANE - SH), (0, 0)))
    fcw = _perm_ffn_cols(w["ffn_conv_w"])
    fcb = _perm_ffn_cols(w["ffn_conv_b"][None, :])
    mixw = w["mix_norm_w"][None, :]
    ffnw = w["ffn_norm_w"][None, :]
    finw = w["final_norm_w"][None, :]
    bst = w["gmlp_bs"].T
    scb = w["ssm_conv_b"][None, :]
    dtb = _pad_lanes(w["ssm_dt_bias"][None, :])
    alog = _pad_lanes(w["ssm_a_log"][None, :])
    dexp = jnp.repeat(w["ssm_d"], SP)[None, :]
    snw = w["ssm_norm_w"][None, :]

    xn = _rms_fwd(x, mixw, name="mix_norm")
    proj, got = mm(xn, win_t, name="in_proj", out_dtype=BF16, tb=True, tn=1536, j_outer=True, b_rows=PMAIN,
                   side=hooks.gather_start())
    dtraw, got = mm(xn, win_dt, name="in_proj_dt", out_dtype=F32, tb=True, side=hooks.gather_pass_on(got))
    w = hooks.late_weights(w, got)
    wup = _perm_ffn_rows(w["ffn_w_up_t"])
    ya_pre = _gmlp_fwd(proj, w["gmlp_ln_w"], w["gmlp_ln_b"], w["gmlp_ws"], bst)
    yb_pre, y_ssd, sprev, ssm_xc = _ssd_fwd(proj, dtraw, w["ssm_conv_w"], scb, dtb, alog, dexp, snw)
    merged, ya, yb = _merge_fwd(ya_pre, yb_pre, proj, w["gate_bias"], w["w_proj_a"], w["w_proj_b"])
    h1 = _matmul(merged, w["w_out"], name="out_proj", out_dtype=F32, add=x)
    hn = _rms_fwd(h1, ffnw, name="ffn_norm")
    up = _matmul(hn, wup, name="ffn_up", out_dtype=BF16, tb=True, tn=FT, j_outer=True)
    act, ffn_xc = _ffn_act_fwd(up, fcw, fcb)
    h2 = _matmul(act, w["ffn_w_down"], name="ffn_down", out_dtype=F32, tk=DFF, add=h1)

    loss_row, dh2, d_finw = _loss_head(h2, tgt, finw)
    dact = _matmul(dh2, w["ffn_w_down"], name="ffn_down_dx", out_dtype=BF16, tb=True, tn=FT)
    d_wdown = _matmul(act, dh2, name="ffn_down_dw", out_dtype=WGRAD, ta=True, tm=FT)
    dup, d_fcw, d_fcb = _ffn_act_bwd(up, ffn_xc, dact, fcw)
    dhn = _matmul(dup, wup, name="ffn_up_dx", out_dtype=F32, tk=2 * FT)
    d_wup = _matmul(dup, hn, name="ffn_up_dw", out_dtype=WGRAD, ta=True, tm=FT, tk=2048)
    dh1, d_ffnw = _rms_bwd(h1, ffnw, dhn, dh2, name="ffn_norm_bwd")
    dmerged = _matmul(dh1, w["w_out"], name="out_proj_dx", out_dtype=BF16, tb=True)
    d_wout = _matmul(merged, dh1, name="out_proj_dw", out_dtype=WGRAD, ta=True)
    dproj, dya, dyb, dya_pre, dyb_pre, d_gbias = _merge_bwd(dmerged, proj, w["gate_bias"], ya, yb,
                                                           w["w_proj_a"], w["w_proj_b"])
    d_wpa = _matmul(ya_pre, dya, name="proj_a_dw", out_dtype=WGRAD, ta=True, tk=2048)
    d_wpb = _matmul(yb_pre, dyb, name="proj_b_dw", out_dtype=WGRAD, ta=True, tk=2048)
    dproj, d_lnw, d_lnb, d_ws, d_bst = _gmlp_bwd(proj, dya_pre, dproj, w["gmlp_ln_w"], w["gmlp_ln_b"],
                                                 w["gmlp_ws"], bst)
    dproj, ddt, d_scw, d_scb, d_dtb, d_a, d_dch, d_snw = _ssd_bwd(
        proj, ssm_xc, dtraw, y_ssd, sprev, dyb_pre, dproj, w["ssm_conv_w"], dtb, alog, dexp, snw)
    late = {"w_proj_a": d_wpa, "w_proj_b": d_wpb, "w_out": d_wout, "ffn_w_up_t": _perm_ffn_rows(d_wup),
            "ffn_w_down": d_wdown}
    gap = (2 * D + 2 * GW, DP_GAP)
    d_win_main, late_outs = mm(dproj, xn, name="in_proj_dw", out_dtype=WGRAD, ta=True, a_gap=gap, tk=2048,
                               side=hooks.reduce_late(late))
    d_win_dt = _matmul(ddt, xn, name="in_proj_dt_dw", out_dtype=F32, ta=True)
    d_win_t = jnp.concatenate([d_win_main, d_win_dt[:SH]], axis=0)
    dxn, got = mm(ddt, win_dt, name="in_proj_dt_dx", out_dtype=F32,
                  side=hooks.w_in_to_sibling(d_win_main, d_win_dt[:SH]))
    dxn, w_in_outs = mm(dproj, win_t, name="in_proj_dx", out_dtype=F32, add=dxn, b_rows=PMAIN, a_gap=gap,
                        side=hooks.reduce_w_in(got))
    hooks.reduced(late_outs, w_in_outs)
    grad_x, d_mixw = _rms_bwd(x, mixw, dxn, dh1, name="mix_norm_bwd")

    a_neg = -jnp.exp(w["ssm_a_log"])
    grads = {
        "mix_norm_w": d_mixw[0],
        "w_in_t": d_win_t,
        "gate_bias": d_gbias,
        "gmlp_ln_w": d_lnw, "gmlp_ln_b": d_lnb, "gmlp_ws": d_ws, "gmlp_bs": d_bst[:, :GG].T,
        "ssm_conv_w": d_scw, "ssm_conv_b": d_scb[0],
        "ssm_dt_bias": d_dtb[0, :SH], "ssm_a_log": d_a[0, :SH] * a_neg,
        "ssm_d": d_dch.reshape(SH, SP).sum(axis=-1), "ssm_norm_w": d_snw[0],
        **late,
        "ffn_norm_w": d_ffnw[0],
        "ffn_conv_w": _perm_ffn_cols(d_fcw), "ffn_conv_b": _perm_ffn_cols(d_fcb)[0],
        "ffn_w_down": d_wdown, "final_norm_w": d_finw[0],
    }
    return loss_row, grad_x, grads


MESH = pl.DeviceIdType.MESH
HBM_SPEC = pl.BlockSpec(memory_space=pltpu.HBM)


def _axes():
    return lax.axis_index("x"), lax.axis_index("y"), lax.axis_index("c")


def _all_gather(shards, *, name):
    na = len(shards)

    def body(*refs):
        x_refs, out_refs = refs[:na], refs[na:2 * na]
        send_sems, recv_sems, local_sems = refs[2 * na:]
        x, y, c = _axes()
        me, sibling = (x, y, c), (x, y, 1 - c)
        chips = [(1 - x, y), (x, 1 - y), (1 - x, 1 - y)]

        def slot(a, px, py, pc):
            return out_refs[a].at[4 * px + 2 * py + pc]

        def copy(a, k, block, to, src=None):
            return pltpu.make_async_remote_copy(
                src_ref=slot(a, *block) if src is None else src, dst_ref=slot(a, *block),
                send_sem=send_sems.at[7 * a + k], recv_sem=recv_sems.at[7 * a + k], device_id=to, device_id_type=MESH)

        mine = [pltpu.make_async_copy(x_refs[a], slot(a, *me), local_sems.at[a]) for a in range(na)]
        for cp in mine:
            cp.start()
        first = []
        for a in range(na):
            first.append(copy(a, 0, me, sibling, src=x_refs[a]))
            first += [copy(a, 1 + j, me, (*chip, c), src=x_refs[a]) for j, chip in enumerate(chips)]
        for cp in first:
            cp.start()
        passed = []
        for j, chip in enumerate(chips):
            for a in range(na):
                copy(a, 1 + j, (*chip, c), me).wait_recv()
                cp = copy(a, 4 + j, (*chip, c), sibling)
                cp.start()
                passed.append(cp)
        for a in range(na):
            copy(a, 0, sibling, me).wait_recv()
        for j, chip in enumerate(chips):
            for a in range(na):
                copy(a, 4 + j, (*chip, 1 - c), me).wait_recv()
        for cp in first + passed:
            cp.wait_send()
        for cp in mine:
            cp.wait()

    return pl.pallas_call(
        body, out_shape=[jax.ShapeDtypeStruct((NDEV,) + s.shape, s.dtype) for s in shards],
        in_specs=[HBM_SPEC] * na, out_specs=[HBM_SPEC] * na,
        scratch_shapes=[pltpu.SemaphoreType.DMA((7 * na,)), pltpu.SemaphoreType.DMA((7 * na,)),
                        pltpu.SemaphoreType.DMA((na,))],
        name=name)(*shards)


def _exchange(srcs, plan, *, name):
    na = len(srcs)
    n = len(plan(0, 0, 0))

    def body(*refs):
        src_refs, out_refs = refs[:na], refs[na:2 * na]
        send_sems, recv_sems = refs[2 * na:]
        x, y, c = _axes()
        copies = []
        for k, (slab, peer) in enumerate(plan(x, y, c)):
            for a in range(na):
                cp = pltpu.make_async_remote_copy(
                    src_ref=src_refs[a].at[slab], dst_ref=out_refs[a].at[k], send_sem=send_sems.at[n * a + k],
                    recv_sem=recv_sems.at[n * a + k], device_id=peer, device_id_type=MESH)
                cp.start()
                copies.append(cp)
        for cp in copies:
            cp.wait()

    return pl.pallas_call(
        body, out_shape=[jax.ShapeDtypeStruct((n,) + s.shape[1:], s.dtype) for s in srcs],
        in_specs=[HBM_SPEC] * na, out_specs=[HBM_SPEC] * na,
        scratch_shapes=[pltpu.SemaphoreType.DMA((n * na,)), pltpu.SemaphoreType.DMA((n * na,))], name=name)(*srcs)


def _to_sibling_plan(x, y, c):
    return [(2 * q + (1 - c), (x, y, 1 - c)) for q in range(4)]


def _to_chips_plan(x, y, c):
    q = 2 * x + y
    return [(q ^ 2, (1 - x, y, c)), (q ^ 1, (x, 1 - y, c)), (q ^ 3, (1 - x, 1 - y, c))]


def _row_tile(rows, row_bytes, budget=2 * 2 ** 20, align=2 * SUBLANE):
    if rows * row_bytes <= 2 * budget:
        return rows
    best = None
    for d in range(align, rows + 1, align):
        if rows % d == 0 and d * row_bytes <= budget:
            best = d
    return best or rows


def _pair_add(g, ra, c_idx, *, name):
    _, _, R, C = g.shape
    tr = _row_tile(R, C * 4, budget=3 * 2 ** 20)

    def body(c_ref, g_ref, ra_ref, o_ref):
        del c_ref
        o_ref[...] = (g_ref[0].astype(F32) + ra_ref[...].astype(F32)).astype(o_ref.dtype)

    return pl.pallas_call(
        body,
        grid_spec=pltpu.PrefetchScalarGridSpec(
            num_scalar_prefetch=1, grid=(4, R // tr),
            in_specs=[pl.BlockSpec((1, 1, tr, C), lambda q, r, cr: (q, cr[0], r, 0)),
                      pl.BlockSpec((1, tr, C), lambda q, r, cr: (q, r, 0))],
            out_specs=pl.BlockSpec((1, tr, C), lambda q, r, cr: (q, r, 0))),
        out_shape=jax.ShapeDtypeStruct((4, R, C), g.dtype), name=name,
        compiler_params=_params(("arbitrary", "arbitrary"), 24))(c_idx, g, ra)


def _grad_sum(p, rb, q_idx, *, name):
    _, R, C = p.shape
    tr = _row_tile(R, C * 4, budget=3 * 2 ** 20)

    def body(q_ref, p_ref, rb_ref, o_ref):
        del q_ref
        g = p_ref[0].astype(F32)
        for k in range(3):
            g = g + rb_ref[k].astype(F32)
        o_ref[...] = g

    return pl.pallas_call(
        body,
        grid_spec=pltpu.PrefetchScalarGridSpec(
            num_scalar_prefetch=1, grid=(R // tr,),
            in_specs=[pl.BlockSpec((1, tr, C), lambda r, qr: (qr[0], r, 0)),
                      pl.BlockSpec((3, tr, C), lambda r, qr: (0, r, 0))],
            out_specs=pl.BlockSpec((tr, C), lambda r, qr: (r, 0))),
        out_shape=jax.ShapeDtypeStruct((R, C), F32), name=name,
        compiler_params=_params(("arbitrary",), 40))(q_idx, p, rb)


def _adamw(g, w, m, v):
    m = ADAM_B1 * m + (1.0 - ADAM_B1) * g
    v = ADAM_B2 * v + (1.0 - ADAM_B2) * (g * g)
    m_hat = m / (1.0 - ADAM_B1 ** ADAM_STEP)
    v_hat = v / (1.0 - ADAM_B2 ** ADAM_STEP)
    delta = -ADAM_LR * (m_hat / (jnp.sqrt(v_hat) + ADAM_EPS) + ADAM_WD * w)
    return delta, m, v


def _adam(g, w, m, v, *, name):
    _, R, C = w.shape
    tr = _row_tile(R, C * 4, budget=2 ** 20, align=SUBLANE)

    def body(g_ref, w_ref, m_ref, v_ref, d_out, m_out, v_out):
        delta, mn, vn = _adamw(g_ref[...], w_ref[...], m_ref[...], v_ref[...])
        d_out[...] = delta
        m_out[...] = mn
        v_out[...] = vn

    row = pl.BlockSpec((1, tr, C), lambda r: (0, r, 0))
    o = jax.ShapeDtypeStruct((1, R, C), F32)
    return pl.pallas_call(
        body, grid=(R // tr,), in_specs=[row, row, row, row], out_specs=[row, row, row], out_shape=[o, o, o],
        name=name, compiler_params=_params(("arbitrary",), 32))(g, w, m, v)


def _vmem_specs(n):
    return [pl.BlockSpec(memory_space=pltpu.VMEM)] * n


def _pair_sum_many(mine, theirs, *, name):
    n = len(mine)

    def body(*refs):
        for a in range(n):
            refs[2 * n + a][...] = refs[a][...] + refs[n + a][0]

    return pl.pallas_call(
        body, out_shape=[jax.ShapeDtypeStruct(m.shape, m.dtype) for m in mine], in_specs=_vmem_specs(2 * n),
        out_specs=_vmem_specs(n), name=name)(*mine, *theirs)


def _chip_sum_many(own, recv, q_idx, *, name):
    n = len(own)

    def body(q_ref, *refs):
        q = q_ref[0]
        for a in range(n):
            mine, r = refs[a][...], refs[n + a]
            total = None
            for chip in range(4):
                e = q ^ chip
                term = jnp.where(e == 0, mine, jnp.where(e == 2, r[0], jnp.where(e == 1, r[1], r[2])))
                total = term if total is None else total + term
            refs[2 * n + a][...] = total

    return pl.pallas_call(
        body, out_shape=[jax.ShapeDtypeStruct(m.shape, m.dtype) for m in own],
        in_specs=[pl.BlockSpec(memory_space=pltpu.SMEM)] + _vmem_specs(2 * n), out_specs=_vmem_specs(n),
        name=name)(q_idx, *own, *recv)


def _adam_many(gs, ws, ms, vs, *, name):
    n = len(gs)

    def body(*refs):
        for a in range(n):
            delta, mn, vn = _adamw(*(refs[k * n + a][...] for k in range(4)))
            refs[4 * n + a][...] = delta
            refs[5 * n + a][...] = mn
            refs[6 * n + a][...] = vn

    shapes = [jax.ShapeDtypeStruct(w.shape, w.dtype) for w in ws]
    out = pl.pallas_call(body, out_shape=shapes * 3, in_specs=_vmem_specs(4 * n), out_specs=_vmem_specs(3 * n),
                         name=name)(*gs, *ws, *ms, *vs)
    return out[:n], out[n:2 * n], out[2 * n:]


WEIGHTS = ["mix_norm_w", "w_in", "gate_bias", "gmlp_ln_w", "gmlp_ln_b", "gmlp_ws", "gmlp_bs", "ssm_conv_w",
           "ssm_conv_b", "ssm_dt_bias", "ssm_a_log", "ssm_d", "ssm_norm_w", "w_proj_a", "w_proj_b", "w_out",
           "ffn_norm_w", "ffn_w_up", "ffn_conv_w", "ffn_conv_b", "ffn_w_down", "final_norm_w"]
SHARDED = {"w_in": ((D, IN_COLS), 1), "gate_bias": ((2, D), 1), "ssm_conv_w": ((SK, SXBC), 1),
           "w_proj_a": ((GW, D), 0), "w_proj_b": ((SI, D), 0), "w_out": ((D, D), 0),
           "ffn_w_up": ((D, 2 * DFF), 1), "ffn_conv_w": ((FK, 2 * DFF), 1), "ffn_w_down": ((DFF, D), 0)}
REPLICATED = {"mix_norm_w": (D,), "gmlp_ln_w": (GG, GD), "gmlp_ln_b": (GG, GD), "gmlp_ws": (GG, GB, GB),
              "gmlp_bs": (GG, GB), "ssm_conv_b": (SXBC,), "ssm_dt_bias": (SH,), "ssm_a_log": (SH,), "ssm_d": (SH,),
              "ssm_norm_w": (SI,), "ffn_norm_w": (D,), "ffn_conv_b": (2 * DFF,), "final_norm_w": (D,)}
REPL_ORDER = [n for n in WEIGHTS if n in REPLICATED]
BTILE = 2 * SUBLANE
WIN_R = IN_COLS // NDEV
WIN_P = WIN_R + BTILE - WIN_R % BTILE
WIN_A = [WIN_R * d // BTILE * BTILE for d in range(NDEV)]
assert all(WIN_A[d] + WIN_P >= WIN_R * (d + 1) for d in range(NDEV)) and WIN_A[-1] + WIN_P == IN_COLS
BIG = [("w_proj_a", GW // NDEV, False), ("w_proj_b", SI // NDEV, False), ("w_out", D // NDEV, False),
       ("ffn_w_up", 2 * DFF // NDEV, True), ("ffn_w_down", DFF // NDEV, False), ("w_in", WIN_P, True)]
VECTORS = ["gate_bias", "ssm_conv_w", "ffn_conv_w"]


def _round_up(n, k):
    return (n + k - 1) // k * k


BIG_OFF = {}
_off = 0
for _n, _r, _t in BIG:
    BIG_OFF[_n] = _off
    _off += _r
BIG_USED = _off
BIG_ROWS = _round_up(BIG_USED, 2 * SUBLANE)
assert all(BIG_OFF[n] % (2 * SUBLANE) == 0 for n, _, _ in BIG)
VEC_SHAPE = {n: (SHARDED[n][0][0], SHARDED[n][0][1] // NDEV) for n in VECTORS}


def _win_offset(dev):
    return WIN_R * dev - WIN_R * dev // BTILE * BTILE


def _row_shards(arrs, dtype, dev):
    parts = []
    for n, r, t in BIG:
        a = (arrs[n].T if t else arrs[n]).astype(dtype)
        if n == "w_in":
            a = lax.dynamic_update_slice(jnp.zeros((WIN_P, D), dtype), a, (_win_offset(dev), 0))
        parts.append(a)
    return parts


def _join_windows(win):
    shared = [d for d in range(1, NDEV) if WIN_A[d] % WIN_R]

    def body(win_ref, out_ref, sems, tile_a, tile_b):
        copies = []
        for d in range(NDEV):
            lo = BTILE if d in shared else 0
            hi = WIN_P - BTILE if d + 1 in shared else WIN_P
            cp = pltpu.make_async_copy(win_ref.at[d, pl.ds(lo, hi - lo)], out_ref.at[pl.ds(WIN_A[d] + lo, hi - lo)],
                                       sems.at[d])
            cp.start()
            copies.append(cp)
        for d in shared:
            pltpu.sync_copy(win_ref.at[d - 1, pl.ds(WIN_P - BTILE, BTILE)], tile_a)
            pltpu.sync_copy(win_ref.at[d, pl.ds(0, BTILE)], tile_b)
            tile_a[...] = tile_a[...] + tile_b[...]
            pltpu.sync_copy(tile_a, out_ref.at[pl.ds(WIN_A[d], BTILE)])
        for cp in copies:
            cp.wait()

    return pl.pallas_call(
        body, out_shape=jax.ShapeDtypeStruct((IN_COLS, D), win.dtype),
        in_specs=[pl.BlockSpec(memory_space=pltpu.HBM)], out_specs=pl.BlockSpec(memory_space=pltpu.HBM),
        scratch_shapes=[pltpu.SemaphoreType.DMA((NDEV,)), pltpu.VMEM((BTILE, D), win.dtype),
                        pltpu.VMEM((BTILE, D), win.dtype)], name="w_in_join")(win)


def _split_windows(main, last):
    assert WIN_A[-2] + WIN_P <= PMAIN
    wins = [main[a:a + WIN_P] for a in WIN_A[:-1]]
    return jnp.stack(wins + [jnp.concatenate([main[WIN_A[-1]:], last], axis=0)])


LATE_ROWS = BIG_OFF["w_in"]
assert LATE_ROWS + WIN_P == BIG_ROWS and BIG[-1][0] == "w_in"


def _remote(src, dst, send_sems, recv_sems, k, to):
    return pltpu.make_async_remote_copy(src_ref=src, dst_ref=dst, send_sem=send_sems.at[k], recv_sem=recv_sems.at[k],
                                        device_id=to, device_id_type=MESH)


class _Exchange:
    def __init__(self, late_shards, c_idx):
        self.late_shards, self.c_idx = late_shards, c_idx

    def gather_start(self):
        shards = self.late_shards
        na = len(shards)

        def make(ins, outs, send_sems, recv_sems):
            x, y, c = _axes()
            peers = [(x, y, 1 - c), (1 - x, y, c), (x, 1 - y, c), (1 - x, 1 - y, c)]
            copies = []
            for a, (x_ref, out) in enumerate(zip(ins, outs)):
                mine = out.at[4 * x + 2 * y + c]
                copies += [_remote(x_ref, mine, send_sems, recv_sems, 5 * a + k, p) for k, p in enumerate(peers)]
                copies.append(pltpu.make_async_copy(x_ref, mine, send_sems.at[5 * a + 4]))
            return copies

        return _Side(shards, [jax.ShapeDtypeStruct((NDEV,) + s.shape, s.dtype) for s in shards], 5 * na, make)

    def gather_pass_on(self, outs):
        na = len(outs)

        def make(ins, outs, send_sems, recv_sems):
            x, y, c = _axes()
            slots = [4 * px + 2 * py + c for px, py in [(1 - x, y), (x, 1 - y), (1 - x, 1 - y)]]
            return [_remote(src.at[s], dst.at[s], send_sems, recv_sems, 3 * a + k, (x, y, 1 - c))
                    for a, (src, dst) in enumerate(zip(ins, outs)) for k, s in enumerate(slots)]

        return _Side(outs, [jax.ShapeDtypeStruct(b.shape, b.dtype) for b in outs], 3 * na, make,
                     aliases=[(a, a) for a in range(na)])

    def late_weights(self, w, outs):
        w = dict(w)
        for (n, r, t), buf in zip(BIG[:-1], outs):
            w[n + "_t" if t else n] = buf.reshape(NDEV * r, D)
        return w

    @staticmethod
    def _plan_side(src, plan):
        n = len(plan(0, 0, 0))

        def make(ins, outs, send_sems, recv_sems):
            (s,), (dst,) = ins, outs
            return [_remote(s.at[slab], dst.at[k], send_sems, recv_sems, k, peer)
                    for k, (slab, peer) in enumerate(plan(*_axes()))]

        return _Side([src], [jax.ShapeDtypeStruct((n,) + src.shape[1:], src.dtype)], n, make)

    def _to_chips(self, send, sib, tag):
        sums = _pair_add(send.reshape((4, 2) + send.shape[1:]), sib, self.c_idx, name=tag + "_grad_pair_add")
        return sums, self._plan_side(sums, _to_chips_plan)

    def reduce_late(self, grads):
        send = jnp.concatenate([grads[n + "_t" if t else n].reshape(NDEV, r, D) for n, r, t in BIG[:-1]], axis=1)
        send = send.astype(BF16)
        (sib,) = _exchange([send], _to_sibling_plan, name="late_grads_to_sibling")
        self.late_sum, side = self._to_chips(send, sib, "late")
        return side

    def w_in_to_sibling(self, grad_main, grad_dt):
        self.w_in_send = _split_windows(grad_main.astype(BF16), grad_dt.astype(BF16))
        return self._plan_side(self.w_in_send, _to_sibling_plan)

    def reduce_w_in(self, outs):
        self.w_in_sum, side = self._to_chips(self.w_in_send, outs[0], "w_in")
        return side

    def reduced(self, late_outs, w_in_outs):
        (self.late_from_chips,), (self.w_in_from_chips,) = late_outs, w_in_outs


def kernel(x, mix_norm_w, w_in, gate_bias, gmlp_ln_w, gmlp_ln_b, gmlp_ws, gmlp_bs, ssm_conv_w, ssm_conv_b, ssm_dt_bias, ssm_a_log, ssm_d, ssm_norm_w, w_proj_a, w_proj_b, w_out, ffn_norm_w, ffn_w_up, ffn_conv_w, ffn_conv_b, ffn_w_down, final_norm_w, loss_target, m_mix_norm_w, m_w_in, m_gate_bias, m_gmlp_ln_w, m_gmlp_ln_b, m_gmlp_ws, m_gmlp_bs, m_ssm_conv_w, m_ssm_conv_b, m_ssm_dt_bias, m_ssm_a_log, m_ssm_d, m_ssm_norm_w, m_w_proj_a, m_w_proj_b, m_w_out, m_ffn_norm_w, m_ffn_w_up, m_ffn_conv_w, m_ffn_conv_b, m_ffn_w_down, m_final_norm_w, v_mix_norm_w, v_w_in, v_gate_bias, v_gmlp_ln_w, v_gmlp_ln_b, v_gmlp_ws, v_gmlp_bs, v_ssm_conv_w, v_ssm_conv_b, v_ssm_dt_bias, v_ssm_a_log, v_ssm_d, v_ssm_norm_w, v_w_proj_a, v_w_proj_b, v_w_out, v_ffn_norm_w, v_ffn_w_up, v_ffn_conv_w, v_ffn_conv_b, v_ffn_w_down, v_final_norm_w):
    given = dict(locals())
    wts = {n: given[n] for n in WEIGHTS}
    mom = {n: given["m_" + n] for n in WEIGHTS}
    var = {n: given["v_" + n] for n in WEIGHTS}
    xi, yi, ci = _axes()
    c_idx = jnp.reshape(ci, (1,)).astype(jnp.int32)
    q_idx = jnp.reshape(2 * xi + yi, (1,)).astype(jnp.int32)
    big_names = [n for n, _, _ in BIG]
    drop = lambda d, names: {n: d[n][0] for n in names}

    dev = 4 * xi + 2 * yi + ci
    shards = _row_shards(drop(wts, big_names), BF16, dev)
    gathered = _all_gather(shards[-1:] + [wts[n] for n in VECTORS], name="w_in_all_gather")
    full = {"w_in_t": _join_windows(gathered[0])}
    for n, a in zip(VECTORS, gathered[1:]):
        r, c = VEC_SHAPE[n]
        full[n] = a[:, 0].transpose(1, 0, 2).reshape(r, NDEV * c)
    for n in REPL_ORDER:
        full[n] = wts[n].reshape(REPLICATED[n])

    hooks = _Exchange(shards[:-1], c_idx)
    loss_local, grad_x, grads = _local_step(x[0], loss_target[0], full, hooks)
    g_late = _grad_sum(hooks.late_sum, hooks.late_from_chips, q_idx, name="late_grad_sum")
    g_win = _grad_sum(hooks.w_in_sum, hooks.w_in_from_chips, q_idx, name="w_in_grad_sum")

    small = VECTORS + REPL_ORDER
    as_2d = lambda a: a if a.ndim >= 2 else a[None]
    part = [grads[n].reshape((1,) + SHARDED[n][0] if n in VECTORS else as_2d(wts[n]).shape) for n in small]
    part.append(loss_local)
    from_sibling = _exchange([p[None] for p in part], lambda x, y, c: [(0, (x, y, 1 - c))],
                             name="small_grads_to_sibling")
    chip_sums = _pair_sum_many(part, from_sibling, name="small_grad_pair_sum")
    from_chips = _exchange([s[None] for s in chip_sums],
                           lambda x, y, c: [(0, (1 - x, y, c)), (0, (x, 1 - y, c)), (0, (1 - x, 1 - y, c))],
                           name="small_grads_to_chips")
    totals = _chip_sum_many(chip_sums, from_chips, q_idx, name="small_grad_chip_sum")
    g_small, loss = dict(zip(small, totals)), totals[-1][0, 0]
    for n in VECTORS:
        c = VEC_SHAPE[n][1]
        g_small[n] = lax.dynamic_slice_in_dim(g_small[n], dev * c, c, axis=2)

    outs = {}
    small_g = [g_small[n] for n in small]
    small_out = _adam_many(small_g, *[[as_2d(d[n]) for n in small] for d in (wts, mom, var)], name="adam_small")
    for i, n in enumerate(small):
        outs[n] = tuple(a[i].reshape(wts[n].shape) for a in (small_g,) + tuple(small_out))
    for n, r, t in BIG:
        if n == "w_in":
            g = lax.dynamic_slice(g_win, (_win_offset(dev), 0), (WIN_R, D))
        else:
            g = g_late[BIG_OFF[n]:BIG_OFF[n] + r]
        flip = (lambda a: a.transpose(0, 2, 1)) if t else (lambda a: a)
        g = g[None]
        new = _adam(g, flip(wts[n]), flip(mom[n]), flip(var[n]), name="adam_" + n)
        outs[n] = tuple(flip(a) for a in (g,) + tuple(new))
    return (loss, grad_x[None]) + tuple(outs[n][k] for k in range(4) for n in WEIGHTS)
```

```python
import functools

import jax
import jax.numpy as jnp
from jax import lax
from jax.experimental import pallas as pl
from jax.experimental.pallas import tpu as pltpu

F32 = jnp.float32
BF16 = jnp.bfloat16

D = 1024
EPS = 1e-5
GW = 1024
GB = 128
GG = 8
GD = 128
GCH = 64
SI = 2048
SH = 32
SP = 64
SG = 4
SN = 128
SGW = SI // SG
SK = 4
SXBC = SI + 2 * SG * SN
DFF = 2816
FK = 3
PMAIN = 2 * D + 2 * GW + SI + SXBC
IN_COLS = PMAIN + SH
DP_SSM = SI + SXBC
DP_GAP = (DP_SSM - (2 * D + 2 * GW) % DP_SSM) % DP_SSM
DP_COLS = 2 * D + 2 * GW + DP_GAP + DP_SSM
assert DP_GAP % D == 0 and (2 * D + 2 * GW) % D == 0
NDEV = 8
ADAM_LR, ADAM_B1, ADAM_B2, ADAM_EPS, ADAM_WD, ADAM_STEP = 0.001, 0.9, 0.999, 1e-08, 0.01, 10

LANE = 128
SUBLANE = 8
VMEM_MB_V7X = 64
VMEM_CAP_MB = VMEM_MB_V7X - 8

LS = 128
FT = DFF // 2

NN = (((1,), (0,)), ((), ()))
NT = (((1,), (1,)), ((), ()))
TN = (((0,), (0,)), ((), ()))


def _params(sem, vmem_mb):
    return pltpu.CompilerParams(dimension_semantics=sem,
                                vmem_limit_bytes=min(int(vmem_mb), VMEM_CAP_MB) * 1024 * 1024)


def _dot(a, b, dims=NN):
    return lax.dot_general(a, b, dims, preferred_element_type=F32)


def _sigmoid(x):
    return 1.0 / (1.0 + jnp.exp(-x))


def _split3(v):
    hi = v.astype(BF16)
    r = v - hi.astype(F32)
    mid = r.astype(BF16)
    lo = (r - mid.astype(F32)).astype(BF16)
    return hi, mid, lo


def _dot3(a_f32, b_bf16, dims):
    hi, mid, lo = _split3(a_f32)
    return _dot(hi, b_bf16, dims) + _dot(mid, b_bf16, dims) + _dot(lo, b_bf16, dims)


def _dot2(a_f32, b_bf16, dims):
    hi, mid, _ = _split3(a_f32)
    return _dot(hi, b_bf16, dims) + _dot(mid, b_bf16, dims)


def _dot3_rhs(a_bf16, b_f32, dims):
    hi, mid, lo = _split3(b_f32)
    return _dot(a_bf16, hi, dims) + _dot(a_bf16, mid, dims) + _dot(a_bf16, lo, dims)


def _matmul(a, b, *, name, out_dtype, ta=False, tb=False, tm=1024, tn=1024, tk=1024, add=None,
            j_outer=False, b_rows=None, a_gap=None, side=None):
    gap0, gapw = a_gap or (0, 0)
    if ta:
        K, M = a.shape
        M -= gapw
    else:
        M, K = a.shape
        K -= gapw
    if tb:
        N, K2 = b.shape
        N = b_rows or N
    else:
        K2, N = b.shape
        K2 = b_rows or K2
    assert K == K2, (a.shape, b.shape, ta, tb)
    tm, tn, tk = min(tm, M), min(tn, N), min(tk, K)
    assert M % tm == 0 and N % tn == 0 and K % tk == 0, (M, N, K, tm, tn, tk)
    nk = K // tk
    dims = (((0 if ta else 1,), (1 if tb else 0,)), ((), ()))
    has_add = add is not None
    n_in = 3 if has_add else 2
    s_in = len(side.inputs) if side else 0
    s_out = len(side.out_shapes) if side else 0
    grid = (N // tn, M // tm, nk) if j_outer else (M // tm, N // tn, nk)

    def body(*refs):
        a_ref, b_ref = refs[:2]
        add_ref = refs[2] if has_add else None
        o_ref = refs[n_in + s_in]
        if side:
            side_refs = (refs[n_in:n_in + s_in], refs[n_in + s_in + 1:n_in + s_in + 1 + s_out]) + tuple(refs[-2:])
            ids = [pl.program_id(d) for d in range(3)]
            first = functools.reduce(jnp.logical_and, [i == 0 for i in ids])
            last = functools.reduce(jnp.logical_and, [i == g - 1 for i, g in zip(ids, grid)])

            @pl.when(first)
            def _():
                for cp in side.make(*side_refs):
                    cp.start()

            @pl.when(last)
            def _():
                for cp in side.make(*side_refs):
                    cp.wait()

        p = lax.dot_general(a_ref[...].astype(BF16), b_ref[...].astype(BF16), dims,
                            preferred_element_type=F32)

        def finish(acc):
            if has_add:
                acc = acc + add_ref[...].astype(F32)
            o_ref[...] = acc.astype(o_ref.dtype)

        if nk == 1:
            finish(p)
        else:
            acc_ref = refs[n_in + s_in + 1 + s_out]
            k = pl.program_id(2)

            @pl.when(k == 0)
            def _():
                acc_ref[...] = p

            @pl.when(jnp.logical_and(k > 0, k < nk - 1))
            def _():
                acc_ref[...] += p

            @pl.when(k == nk - 1)
            def _():
                finish(acc_ref[...] + p)

    if j_outer:
        ij = lambda g0, g1: (g1, g0)
    else:
        ij = lambda g0, g1: (g0, g1)

    ta_col = tm if ta else tk
    assert gap0 % ta_col == 0 and gapw % ta_col == 0, (a_gap, ta_col)

    def a_map(g0, g1, k):
        i, _ = ij(g0, g1)
        col = i if ta else k
        col = col + jnp.where(col >= gap0 // ta_col, gapw // ta_col, 0) if gapw else col
        return (k, col) if ta else (i, col)

    def b_map(g0, g1, k):
        _, j = ij(g0, g1)
        return (j, k) if tb else (k, j)

    def o_map(g0, g1, k):
        return ij(g0, g1)

    in_specs = [pl.BlockSpec((tk, tm) if ta else (tm, tk), a_map),
                pl.BlockSpec((tn, tk) if tb else (tk, tn), b_map)]
    args = [a, b]
    if has_add:
        in_specs.append(pl.BlockSpec((tm, tn), o_map))
        args.append(add)
    scratch = [pltpu.VMEM((tm, tn), F32)] if nk > 1 else []
    osz = jnp.dtype(out_dtype).itemsize
    est = (2 * (tm * tk * a.dtype.itemsize + tk * tn * b.dtype.itemsize) + 2 * tm * tn * osz
           + (2 * tm * tn * add.dtype.itemsize if has_add else 0)
           + 3 * tm * tn * 4 + 2 * (tm * tk + tk * tn)) / 2 ** 20 + 4
    out_specs = [pl.BlockSpec((tm, tn), o_map)]
    out_shape = [jax.ShapeDtypeStruct((M, N), out_dtype)]
    aliases = {}
    if side:
        hbm = pl.BlockSpec(memory_space=pltpu.HBM)
        in_specs += [hbm] * s_in
        args += list(side.inputs)
        out_specs += [hbm] * s_out
        out_shape += list(side.out_shapes)
        scratch += [pltpu.SemaphoreType.DMA((side.nsem,)), pltpu.SemaphoreType.DMA((side.nsem,))]
        aliases = {n_in + i: 1 + j for i, j in side.aliases}
    outs = pl.pallas_call(
        body, grid=grid, in_specs=in_specs, out_specs=out_specs, out_shape=out_shape, scratch_shapes=scratch,
        input_output_aliases=aliases, name=name,
        compiler_params=_params(("arbitrary", "arbitrary", "arbitrary"), est))(*args)
    return (outs[0], list(outs[1:])) if side else outs[0]


class _Side:
    def __init__(self, inputs, out_shapes, nsem, make, aliases=()):
        self.inputs, self.out_shapes, self.nsem, self.make, self.aliases = inputs, out_shapes, nsem, make, aliases


def _rms_fwd(x, w, *, name):
    T = x.shape[0]
    tm = min(512, T)

    def body(x_ref, w_ref, o_ref):
        xv = x_ref[...]
        r = lax.rsqrt(jnp.mean(xv * xv, axis=-1, keepdims=True) + EPS)
        o_ref[...] = (xv * r * w_ref[...]).astype(BF16)

    return pl.pallas_call(
        body, grid=(T // tm,),
        in_specs=[pl.BlockSpec((tm, D), lambda i: (i, 0)), pl.BlockSpec((1, D), lambda i: (0, 0))],
        out_specs=pl.BlockSpec((tm, D), lambda i: (i, 0)),
        out_shape=jax.ShapeDtypeStruct((T, D), BF16), name=name,
        compiler_params=_params(("arbitrary",), 24))(x, w)


def _rms_bwd(x, w, dy, dres, *, name):
    T = x.shape[0]
    tm = min(512, T)

    def body(x_ref, w_ref, dy_ref, dres_ref, dx_ref, dw_ref):
        xv = x_ref[...]
        r = lax.rsqrt(jnp.mean(xv * xv, axis=-1, keepdims=True) + EPS)
        xhat = xv * r
        dyv = dy_ref[...].astype(F32)
        g = dyv * w_ref[...]
        dx_ref[...] = dres_ref[...] + r * (g - xhat * jnp.mean(g * xhat, axis=-1, keepdims=True))
        part = jnp.sum(dyv * xhat, axis=0, keepdims=True)

        @pl.when(pl.program_id(0) == 0)
        def _():
            dw_ref[...] = part

        @pl.when(pl.program_id(0) > 0)
        def _():
            dw_ref[...] += part

    row = pl.BlockSpec((tm, D), lambda i: (i, 0))
    vec = pl.BlockSpec((1, D), lambda i: (0, 0))
    return pl.pallas_call(
        body, grid=(T // tm,), in_specs=[row, vec, row, row], out_specs=[row, vec],
        out_shape=[jax.ShapeDtypeStruct((T, D), F32), jax.ShapeDtypeStruct((1, D), F32)], name=name,
        compiler_params=_params(("arbitrary",), 32))(x, w, dy, dres)


def _loss_head(h, tgt, w):
    T = h.shape[0]
    tm = min(512, T)

    def body(h_ref, t_ref, w_ref, loss_ref, dh_ref, dw_ref):
        hv = h_ref[...]
        r = lax.rsqrt(jnp.mean(hv * hv, axis=-1, keepdims=True) + EPS)
        xhat = hv * r
        wv = w_ref[...]
        err = xhat * wv - t_ref[...]
        lpart = 0.5 * jnp.sum(jnp.mean(err * err, axis=-1, keepdims=True), axis=0, keepdims=True)
        dy = err * (1.0 / D)
        g = dy * wv
        dh_ref[...] = r * (g - xhat * jnp.mean(g * xhat, axis=-1, keepdims=True))
        wpart = jnp.sum(dy * xhat, axis=0, keepdims=True)
        lrow = jnp.broadcast_to(lpart, (1, LANE))

        @pl.when(pl.program_id(0) == 0)
        def _():
            dw_ref[...] = wpart
            loss_ref[...] = lrow

        @pl.when(pl.program_id(0) > 0)
        def _():
            dw_ref[...] += wpart
            loss_ref[...] += lrow

    row = pl.BlockSpec((tm, D), lambda i: (i, 0))
    vec = pl.BlockSpec((1, D), lambda i: (0, 0))
    return pl.pallas_call(
        body, grid=(T // tm,), in_specs=[row, row, vec],
        out_specs=[pl.BlockSpec((1, LANE), lambda i: (0, 0)), row, vec],
        out_shape=[jax.ShapeDtypeStruct((1, LANE), F32), jax.ShapeDtypeStruct((T, D), F32),
                   jax.ShapeDtypeStruct((1, D), F32)], name="loss_head",
        compiler_params=_params(("arbitrary",), 32))(h, tgt, w)


_GELU_C = 0.7978845608028654
_GELU_A = 0.044715


def _gelu(x, with_grad=False):
    x2 = x * x
    cx = _GELU_C * x
    t = jnp.tanh(cx * (1.0 + _GELU_A * x2))
    h = 0.5 * (1.0 + t)
    if not with_grad:
        return x * h
    return x * h, h + 0.5 * cx * (1.0 - t * t) * (1.0 + 3.0 * _GELU_A * x2)


def _gmlp_mask():
    r = lax.broadcasted_iota(jnp.int32, (GB, GB), 0) // GCH
    c = lax.broadcasted_iota(jnp.int32, (GB, GB), 1) // GCH
    return c <= r


def _gmlp_fwd(proj, lnw, lnb, ws, bst):
    T = proj.shape[0]
    tm = min(512, T)
    nblk = tm // GB

    def body(u_ref, v_ref, lnw_ref, lnb_ref, ws_ref, bst_ref, o_ref):
        mask = _gmlp_mask()
        u = _gelu(u_ref[...].astype(F32))
        v = _gelu(v_ref[...].astype(F32))
        for g in range(GG):
            cs = slice(g * GD, (g + 1) * GD)
            vg = v[:, cs]
            mu = jnp.mean(vg, axis=-1, keepdims=True)
            vc = vg - mu
            var = jnp.mean(vc * vc, axis=-1, keepdims=True)
            vn = (vc * lax.rsqrt(var + EPS) * lnw_ref[g:g + 1, :] + lnb_ref[g:g + 1, :]).astype(BF16)
            wsg = jnp.where(mask, ws_ref[g], 0.0).astype(BF16)
            bcol = bst_ref[:, g:g + 1]
            for blk in range(nblk):
                rs = slice(blk * GB, (blk + 1) * GB)
                sv = _dot(wsg, vn[rs, :]) + bcol
                o_ref[rs, cs] = (u[rs, cs] * sv).astype(BF16)

    full = lambda shape: pl.BlockSpec(shape, lambda i: tuple(0 for _ in shape))
    return pl.pallas_call(
        body, grid=(T // tm,),
        in_specs=[pl.BlockSpec((tm, GW), lambda i: (i, 2)), pl.BlockSpec((tm, GW), lambda i: (i, 3)),
                  full((GG, GD)), full((GG, GD)), full((GG, GB, GB)), full((GB, GG))],
        out_specs=pl.BlockSpec((tm, GW), lambda i: (i, 0)),
        out_shape=jax.ShapeDtypeStruct((T, GW), BF16), name="gmlp_fwd",
        compiler_params=_params(("arbitrary",), 40))(proj, proj, lnw, lnb, ws, bst)


def _gmlp_bwd(proj, dya, dproj, lnw, lnb, ws, bst):
    T = proj.shape[0]
    tm = min(512, T)
    nblk = tm // GB

    def body(u_ref, v_ref, dya_ref, dproj_in, lnw_ref, lnb_ref, ws_ref, bst_ref,
             dz_ref, dlnw_ref, dlnb_ref, dws_ref, dbst_ref):
        del dproj_in
        first = pl.program_id(0) == 0

        @pl.when(first)
        def _():
            dlnw_ref[...] = jnp.zeros_like(dlnw_ref)
            dlnb_ref[...] = jnp.zeros_like(dlnb_ref)
            dws_ref[...] = jnp.zeros_like(dws_ref)
            dbst_ref[...] = jnp.zeros_like(dbst_ref)

        mask = _gmlp_mask()
        lane = lax.broadcasted_iota(jnp.int32, (GB, LANE), 1)
        ur = u_ref[...].astype(F32)
        vr = v_ref[...].astype(F32)
        u, gu = _gelu(ur, with_grad=True)
        v, gv = _gelu(vr, with_grad=True)
        dy = dya_ref[...].astype(F32)
        dbst = jnp.zeros((GB, LANE), F32)
        dlnw_rows, dlnb_rows = [], []
        for g in range(GG):
            cs = slice(g * GD, (g + 1) * GD)
            vg = v[:, cs]
            mu = jnp.mean(vg, axis=-1, keepdims=True)
            vc = vg - mu
            var = jnp.mean(vc * vc, axis=-1, keepdims=True)
            rstd = lax.rsqrt(var + EPS)
            xhat = vc * rstd
            lw = lnw_ref[g:g + 1, :]
            vn = (xhat * lw + lnb_ref[g:g + 1, :]).astype(BF16)
            wsg = jnp.where(mask, ws_ref[g], 0.0).astype(BF16)
            bcol = bst_ref[:, g:g + 1]
            dyg = dy[:, cs]
            ug = u[:, cs]
            dsv = dyg * ug
            dsv_b = dsv.astype(BF16)
            dws_g = jnp.zeros((GB, GB), F32)
            bsum = jnp.zeros((GB, 1), F32)
            dvn_parts = []
            for blk in range(nblk):
                rs = slice(blk * GB, (blk + 1) * GB)
                sv = _dot(wsg, vn[rs, :]) + bcol
                dz_ref[rs, cs] = (dyg[rs, :] * sv * gu[rs, cs]).astype(BF16)
                dws_g = dws_g + _dot(dsv_b[rs, :], vn[rs, :], NT)
                bsum = bsum + jnp.sum(dsv[rs, :], axis=-1, keepdims=True)
                dvn_parts.append(_dot(wsg, dsv_b[rs, :], TN))
            dvn = jnp.concatenate(dvn_parts, axis=0)
            dws_ref[g] += jnp.where(mask, dws_g, 0.0)
            dbst = dbst + jnp.where(lane == g, bsum, 0.0)
            dlnw_rows.append(jnp.sum(dvn * xhat, axis=0, keepdims=True))
            dlnb_rows.append(jnp.sum(dvn, axis=0, keepdims=True))
            dxh = dvn * lw
            dvg = rstd * (dxh - jnp.mean(dxh, axis=-1, keepdims=True)
                          - xhat * jnp.mean(dxh * xhat, axis=-1, keepdims=True))
            dz_ref[:, GW + g * GD:GW + (g + 1) * GD] = (dvg * gv[:, cs]).astype(BF16)
        dlnw_ref[...] += jnp.concatenate(dlnw_rows, axis=0)
        dlnb_ref[...] += jnp.concatenate(dlnb_rows, axis=0)
        dbst_ref[...] += dbst

    full = lambda shape: pl.BlockSpec(shape, lambda i: tuple(0 for _ in shape))
    outs = pl.pallas_call(
        body, grid=(T // tm,),
        in_specs=[pl.BlockSpec((tm, GW), lambda i: (i, 2)), pl.BlockSpec((tm, GW), lambda i: (i, 3)),
                  pl.BlockSpec((tm, GW), lambda i: (i, 0)), pl.BlockSpec(memory_space=pl.ANY),
                  full((GG, GD)), full((GG, GD)), full((GG, GB, GB)), full((GB, GG))],
        out_specs=[pl.BlockSpec((tm, 2 * GW), lambda i: (i, 1)), full((GG, GD)), full((GG, GD)),
                   full((GG, GB, GB)), full((GB, LANE))],
        out_shape=[jax.ShapeDtypeStruct(dproj.shape, dproj.dtype), jax.ShapeDtypeStruct((GG, GD), F32),
                   jax.ShapeDtypeStruct((GG, GD), F32), jax.ShapeDtypeStruct((GG, GB, GB), F32),
                   jax.ShapeDtypeStruct((GB, LANE), F32)],
        input_output_aliases={3: 0}, name="gmlp_bwd",
        compiler_params=_params(("arbitrary",), 48))(proj, proj, dya, dproj, lnw, lnb, ws, bst)
    return outs


def _merge_fwd(ya_pre, yb_pre, proj, bias, wpa, wpb):
    T = proj.shape[0]
    tm = min(512, T)

    def body(ya_ref, yb_ref, g_ref, b_ref, wpa_ref, wpb_ref, m_ref, oa_ref, ob_ref):
        ya = _dot(ya_ref[...], wpa_ref[...])
        yb = _dot(yb_ref[...], wpb_ref[...])
        g = g_ref[...].astype(F32)
        sa = _sigmoid(g[:, :D] + b_ref[0:1, :])
        sb = _sigmoid(g[:, D:] + b_ref[1:2, :])
        m_ref[...] = (sa * ya + sb * yb).astype(BF16)
        oa_ref[...] = ya.astype(BF16)
        ob_ref[...] = yb.astype(BF16)

    row = lambda w: pl.BlockSpec((tm, w), lambda i: (i, 0))
    full = lambda shape: pl.BlockSpec(shape, lambda i: tuple(0 for _ in shape))
    o = jax.ShapeDtypeStruct((T, D), BF16)
    return pl.pallas_call(
        body, grid=(T // tm,),
        in_specs=[row(GW), row(SI), row(2 * D), full((2, D)), full((GW, D)), full((SI, D))],
        out_specs=[row(D), row(D), row(D)], out_shape=[o, o, o], name="merge_fwd",
        compiler_params=_params(("arbitrary",), 40))(ya_pre, yb_pre, proj, bias, wpa, wpb)


def _merge_bwd(dm, proj, bias, ya, yb, wpa, wpb):
    T = proj.shape[0]
    tm = min(512, T)

    def body(dm_ref, g_ref, b_ref, ya_ref, yb_ref, wpa_ref, wpb_ref,
             dg_ref, dya_ref, dyb_ref, dpa_ref, dpb_ref, db_ref):
        dmv = dm_ref[...].astype(F32)
        g = g_ref[...].astype(F32)
        sa = _sigmoid(g[:, :D] + b_ref[0:1, :])
        sb = _sigmoid(g[:, D:] + b_ref[1:2, :])
        dya = (dmv * sa).astype(BF16)
        dyb = (dmv * sb).astype(BF16)
        dga = dmv * ya_ref[...].astype(F32) * sa * (1.0 - sa)
        dgb = dmv * yb_ref[...].astype(F32) * sb * (1.0 - sb)
        dg_ref[:, :D] = dga.astype(BF16)
        dg_ref[:, D:] = dgb.astype(BF16)
        dya_ref[...] = dya
        dyb_ref[...] = dyb
        dpa_ref[...] = _dot(dya, wpa_ref[...], NT).astype(BF16)
        dpb_ref[...] = _dot(dyb, wpb_ref[...], NT).astype(BF16)
        part = jnp.concatenate([jnp.sum(dga, axis=0, keepdims=True), jnp.sum(dgb, axis=0, keepdims=True)], axis=0)

        @pl.when(pl.program_id(0) == 0)
        def _():
            db_ref[...] = part

        @pl.when(pl.program_id(0) > 0)
        def _():
            db_ref[...] += part

    row = lambda w: pl.BlockSpec((tm, w), lambda i: (i, 0))
    full = lambda shape: pl.BlockSpec(shape, lambda i: tuple(0 for _ in shape))
    o = lambda w: jax.ShapeDtypeStruct((T, w), BF16)
    return pl.pallas_call(
        body, grid=(T // tm,),
        in_specs=[row(D), row(2 * D), full((2, D)), row(D), row(D), full((GW, D)), full((SI, D))],
        out_specs=[row(2 * D), row(D), row(D), row(GW), row(SI), full((2, D))],
        out_shape=[o(DP_COLS), o(D), o(D), o(GW), o(SI), jax.ShapeDtypeStruct((2, D), F32)], name="merge_bwd",
        compiler_params=_params(("arbitrary",), 48))(dm, proj, bias, ya, yb, wpa, wpb)


RB = 128


def _shift_matrix(j):
    r = lax.broadcasted_iota(jnp.int32, (RB, RB), 0)
    c = lax.broadcasted_iota(jnp.int32, (RB, RB), 1)
    return jnp.where(c == r - j, 1.0, 0.0).astype(BF16)


def _rows_down(xb, before, shifts):
    H = SUBLANE
    mats = [_shift_matrix(j) for j in shifts]
    outs = [[] for _ in shifts]
    for b in range(xb.shape[0] // RB):
        blk = xb[b * RB:(b + 1) * RB]
        edge = jnp.concatenate([before, blk[:2 * H].astype(F32)[:H]], axis=0)
        for i, j in enumerate(shifts):
            outs[i] += [edge[H - j:2 * H - j], _dot(mats[i], blk)[H:]]
        before = blk[RB - 2 * H:].astype(F32)[H:]
    return [jnp.concatenate(o, axis=0) for o in outs]


def _rows_up(xb, after, shifts):
    H = SUBLANE
    nb = xb.shape[0] // RB
    mats = [_shift_matrix(-j) for j in shifts]
    outs = [[] for _ in shifts]
    for b in range(nb):
        blk = xb[b * RB:(b + 1) * RB]
        nxt = xb[(b + 1) * RB:(b + 1) * RB + 2 * H].astype(F32)[:H] if b + 1 < nb else after
        edge = jnp.concatenate([blk[RB - 2 * H:].astype(F32)[H:], nxt], axis=0)
        for i, j in enumerate(shifts):
            outs[i] += [_dot(mats[i], blk)[:RB - H], edge[j:H + j]]
    return [jnp.concatenate(o, axis=0) for o in outs]


def _ffn_act_fwd(up, cw, cb):
    T = up.shape[0]
    tm = min(512, T)
    H = SUBLANE

    def body(up_ref, cw_ref, cb_ref, o_ref, xc_ref, halo):
        @pl.when(pl.program_id(1) == 0)
        def _():
            halo[...] = jnp.zeros_like(halo)

        xb = up_ref[...]
        x2, x1 = _rows_down(xb, halo[...], (2, 1))
        xc = cb_ref[...] + cw_ref[0:1, :] * x2 + cw_ref[1:2, :] * x1 + cw_ref[2:3, :] * xb.astype(F32)
        xc_ref[...] = xc.astype(BF16)
        gate = xc[:, :FT]
        o_ref[...] = (gate * _sigmoid(gate) * xc[:, FT:]).astype(BF16)
        halo[...] = xb[tm - 2 * H:].astype(F32)[H:]

    tile = pl.BlockSpec((tm, 2 * FT), lambda j, i: (i, j))
    return pl.pallas_call(
        body, grid=(2, T // tm),
        in_specs=[tile, pl.BlockSpec((FK, 2 * FT), lambda j, i: (0, j)), pl.BlockSpec((1, 2 * FT), lambda j, i: (0, j))],
        out_specs=[pl.BlockSpec((tm, FT), lambda j, i: (i, j)), tile],
        out_shape=[jax.ShapeDtypeStruct((T, DFF), BF16), jax.ShapeDtypeStruct((T, 2 * DFF), BF16)],
        scratch_shapes=[pltpu.VMEM((H, 2 * FT), F32)], name="ffn_act_fwd",
        compiler_params=_params(("arbitrary", "arbitrary"), 48))(up, cw, cb)


def _ffn_act_bwd(up, xc, dact, cw):
    T = up.shape[0]
    tm = min(512, T)
    nt = T // tm
    H = SUBLANE

    def body(up_ref, xc_ref, da_ref, cw_ref, dup_ref, dcw_ref, dcb_ref, ahead):
        @pl.when(pl.program_id(1) == 0)
        def _():
            ahead[...] = jnp.zeros_like(ahead)
            dcw_ref[...] = jnp.zeros_like(dcw_ref)
            dcb_ref[...] = jnp.zeros_like(dcb_ref)

        xcv = xc_ref[...].astype(F32)
        gate, val = xcv[:, :FT], xcv[:, FT:]
        sg = _sigmoid(gate)
        dav = da_ref[...].astype(F32)
        dgate = dav * val * sg * (1.0 + gate * (1.0 - sg))
        dval = dav * gate * sg
        dxc = jnp.concatenate([dgate, dval], axis=1)
        d1, d2 = _rows_up(dxc.astype(BF16), ahead[...], (1, 2))
        x = up_ref[...].astype(F32)
        dcb_ref[...] += jnp.sum(dxc, axis=0, keepdims=True)
        dcw_ref[...] += jnp.concatenate([jnp.sum(d * x, axis=0, keepdims=True) for d in (d2, d1, dxc)], axis=0)
        dup_ref[...] = (cw_ref[2:3, :] * dxc + cw_ref[1:2, :] * d1 + cw_ref[0:1, :] * d2).astype(BF16)
        ahead[...] = dxc[0:H, :]

    tile = pl.BlockSpec((tm, 2 * FT), lambda j, i: (nt - 1 - i, j))
    return pl.pallas_call(
        body, grid=(2, nt),
        in_specs=[tile, tile, pl.BlockSpec((tm, FT), lambda j, i: (nt - 1 - i, j)),
                  pl.BlockSpec((FK, 2 * FT), lambda j, i: (0, j))],
        out_specs=[tile, pl.BlockSpec((FK, 2 * FT), lambda j, i: (0, j)), pl.BlockSpec((1, 2 * FT), lambda j, i: (0, j))],
        out_shape=[jax.ShapeDtypeStruct((T, 2 * DFF), BF16), jax.ShapeDtypeStruct((FK, 2 * DFF), F32),
                   jax.ShapeDtypeStruct((1, 2 * DFF), F32)],
        scratch_shapes=[pltpu.VMEM((H, 2 * FT), F32)], name="ffn_act_bwd",
        compiler_params=_params(("arbitrary", "arbitrary"), 56))(up, xc, dact, cw)


def _softplus(x):
    e = jnp.exp(-jnp.abs(x))
    return jnp.maximum(x, 0.0) + jnp.where(e < 1e-4, e * (1.0 - 0.5 * e), jnp.log(1.0 + e))


def _ssd_tril():
    li = lax.broadcasted_iota(jnp.int32, (LS, LS), 0)
    si = lax.broadcasted_iota(jnp.int32, (LS, LS), 1)
    return si <= li


def _head_expansion():
    hh = lax.broadcasted_iota(jnp.int32, (LANE, SI), 0)
    cc = lax.broadcasted_iota(jnp.int32, (LANE, SI), 1) // SP
    return jnp.where(hh == cc, 1.0, 0.0).astype(BF16)


def _ssd_pre(xc, dt_ref, dtb_ref, alog_ref, tril, expand):
    sx = _sigmoid(xc)
    xbc = xc * sx
    xs, bm, cm = xbc[:, :SI], xbc[:, SI:SI + SG * SN], xbc[:, SI + SG * SN:]
    dtin = dt_ref[...] + dtb_ref[...]
    dt = _softplus(dtin)
    a_neg = -jnp.exp(alog_ref[...])
    dta = dt * a_neg
    trilb = jnp.where(tril, 1.0, 0.0).astype(BF16)
    a = _dot3_rhs(trilb, dta, NN)
    a_exp = _dot3(a, expand, NN)
    dt_exp = _dot2(dt, expand, NN)
    xdt = xs * dt_exp
    a_last = a_exp[LS - 1:LS, :]
    return dict(xc=xc, sx=sx, xs=xs, bm=bm, cm=cm, dtin=dt_ref[...] + dtb_ref[...], dt=dt, a_neg=a_neg,
                a=a, a_t=a.T, a_exp=a_exp, dt_exp=dt_exp, xdt=xdt, ea=jnp.exp(a_exp),
                w=jnp.exp(a_last - a_exp), eal=jnp.exp(a_last))


def _head_decay(pre, tril, h):
    seg = pre["a"][:, h:h + 1] - pre["a_t"][h:h + 1, :]
    return jnp.exp(jnp.where(tril, seg, -1e30))


def _ssd_fwd(proj, dtraw, cw, cb, dtb, alog, dexp, nw):
    T = proj.shape[0]
    nc = T // LS
    H = SUBLANE

    def body(z_ref, x_ref, dt_ref, cw_ref, cb_ref, dtb_ref, alog_ref, dexp_ref, nw_ref, ex_ref,
             yb_ref, y_ref, sp_ref, xc_ref, halo, st):
        @pl.when(pl.program_id(0) == 0)
        def _():
            halo[...] = jnp.zeros_like(halo)
            st[...] = jnp.zeros_like(st)

        xb = x_ref[...]
        taps = _rows_down(xb, halo[...], (3, 2, 1)) + [xb.astype(F32)]
        xc = cb_ref[...]
        for k in range(SK):
            xc = xc + cw_ref[k:k + 1, :] * taps[k]
        xc_ref[...] = xc.astype(BF16)
        tril, expand = _ssd_tril(), ex_ref[...]
        pre = _ssd_pre(xc, dt_ref, dtb_ref, alog_ref, tril, expand)
        lane = lax.broadcasted_iota(jnp.int32, (LS, LANE), 1)
        lo = lane < SP
        zf = z_ref[...].astype(F32)
        siluz = zf * _sigmoid(zf)
        for g in range(SG):
            gs = slice(g * SGW, (g + 1) * SGW)
            bg = pre["bm"][:, g * SN:(g + 1) * SN].astype(BF16)
            cg = pre["cm"][:, g * SN:(g + 1) * SN].astype(BF16)
            gmat = _dot(cg, bg, NT)
            sg = st[g]
            sp_ref[0, g] = sg
            yoff = _dot(cg, sg.astype(BF16))
            parts = []
            for j in range(SGW // LANE):
                h0 = g * (SGW // SP) + 2 * j
                m0 = gmat * _head_decay(pre, tril, h0)
                m1 = gmat * _head_decay(pre, tril, h0 + 1)
                xp = pre["xdt"][:, g * SGW + j * LANE:g * SGW + (j + 1) * LANE]
                rhs = jnp.concatenate([jnp.where(lo, xp, 0.0), jnp.where(lo, 0.0, xp)], axis=0).astype(BF16)
                parts.append(_dot(jnp.concatenate([m0, m1], axis=1).astype(BF16), rhs))
            y = (jnp.concatenate(parts, axis=1) + pre["ea"][:, gs] * yoff + dexp_ref[:, gs] * pre["xs"][:, gs])
            st[g] = pre["eal"][:, gs] * sg + _dot(bg, (pre["w"][:, gs] * pre["xdt"][:, gs]).astype(BF16), TN)
            y_ref[:, gs] = y
            yg = y * siluz[:, gs]
            r = lax.rsqrt(jnp.mean(yg * yg, axis=-1, keepdims=True) + EPS)
            yb_ref[:, gs] = (yg * r * nw_ref[:, gs]).astype(BF16)
        halo[...] = xb[LS - 2 * H:].astype(F32)[H:]

    vec = lambda w: pl.BlockSpec((1, w), lambda c: (0, 0))
    return pl.pallas_call(
        body, grid=(nc,),
        in_specs=[pl.BlockSpec((LS, SI), lambda c: (c, 2)), pl.BlockSpec((LS, SXBC), lambda c: (c, 2)),
                  pl.BlockSpec((LS, LANE), lambda c: (c, 0)),
                  pl.BlockSpec((SK, SXBC), lambda c: (0, 0)), vec(SXBC), vec(LANE), vec(LANE), vec(SI), vec(SI),
                  pl.BlockSpec((LANE, SI), lambda c: (0, 0))],
        out_specs=[pl.BlockSpec((LS, SI), lambda c: (c, 0)), pl.BlockSpec((LS, SI), lambda c: (c, 0)),
                   pl.BlockSpec((1, SG, SN, SGW), lambda c: (c, 0, 0, 0)), pl.BlockSpec((LS, SXBC), lambda c: (c, 0))],
        out_shape=[jax.ShapeDtypeStruct((T, SI), BF16), jax.ShapeDtypeStruct((T, SI), F32),
                   jax.ShapeDtypeStruct((nc, SG, SN, SGW), F32), jax.ShapeDtypeStruct((T, SXBC), BF16)],
        scratch_shapes=[pltpu.VMEM((H, SXBC), F32), pltpu.VMEM((SG, SN, SGW), F32)], name="ssd_fwd",
        compiler_params=_params(("arbitrary",), VMEM_CAP_MB))(
            proj, proj, dtraw, cw, cb, dtb, alog, dexp, nw, _head_expansion())


def _ssd_bwd(proj, xcs, dtraw, y, sprev, dyb, dproj, cw, dtb, alog, dexp, nw):
    T = proj.shape[0]
    nc = T // LS
    H = SUBLANE
    NJ = 1

    def body(z_ref, x_ref, xc_ref, dt_ref, y_ref, sp_ref, dyb_ref, dproj_in,
             cw_ref, dtb_ref, alog_ref, dexp_ref, nw_ref, ex_ref,
             dp_ref, ddt_ref, dcw_ref, dcb_ref, ddtb_ref, da_ref, dd_ref, dnw_ref,
             ahead, ds, stage):
        del dproj_in
        i = pl.program_id(0)
        j = pl.program_id(1)

        @pl.when(jnp.logical_and(i == 0, j == 0))
        def _():
            ahead[...] = jnp.zeros_like(ahead)
            ds[...] = jnp.zeros_like(ds)
            for r in (dcw_ref, dcb_ref, ddtb_ref, da_ref, dd_ref, dnw_ref):
                r[...] = jnp.zeros_like(r)

        @pl.when(j == 0)
        def _():
            tril, expand = _ssd_tril(), ex_ref[...]
            pre = _ssd_pre(xc_ref[...].astype(F32), dt_ref, dtb_ref, alog_ref, tril, expand)
            lane = lax.broadcasted_iota(jnp.int32, (LS, LANE), 1)
            sub = lax.broadcasted_iota(jnp.int32, (LANE, LS), 0)
            rowi = lax.broadcasted_iota(jnp.int32, (LS, 1), 0)
            lo = lane < SP
            xs, xdt, ea, w, eal = pre["xs"], pre["xdt"], pre["ea"], pre["w"], pre["eal"]

            zf = z_ref[...].astype(F32)
            sz = _sigmoid(zf)
            siluz = zf * sz
            yv = y_ref[...]
            yg = yv * siluz
            dout = dyb_ref[...].astype(F32)
            dyg_parts, dnw_parts = [], []
            for g in range(SG):
                gs = slice(g * SGW, (g + 1) * SGW)
                ygg = yg[:, gs]
                r = lax.rsqrt(jnp.mean(ygg * ygg, axis=-1, keepdims=True) + EPS)
                yhat = ygg * r
                dn = dout[:, gs] * nw_ref[:, gs]
                dnw_parts.append(jnp.sum(dout[:, gs] * yhat, axis=0, keepdims=True))
                dyg_parts.append(r * (dn - yhat * jnp.mean(dn * yhat, axis=-1, keepdims=True)))
            dyg = jnp.concatenate(dyg_parts, axis=1)
            dnw_ref[...] += jnp.concatenate(dnw_parts, axis=1)
            dy = dyg * siluz
            stage[:, 0:SI] = (dyg * yv * sz * (1.0 + zf * (1.0 - sz))).astype(BF16)
            dd_ref[...] += jnp.sum(dy * xs, axis=0, keepdims=True)
            tt = ea * dy

            da_rows = jnp.zeros((LS, LANE), F32)
            da_cols = jnp.zeros((LANE, LS), F32)
            dxdt_parts, db_parts, dc_parts, daexp_parts = [], [], [], []
            for g in range(SG):
                gs = slice(g * SGW, (g + 1) * SGW)
                bg = pre["bm"][:, g * SN:(g + 1) * SN].astype(BF16)
                cg = pre["cm"][:, g * SN:(g + 1) * SN].astype(BF16)
                sg = sp_ref[0, g]
                sgb = sg.astype(BF16)
                dsg = ds[g]
                dsgb = dsg.astype(BF16)
                ttg = tt[:, gs].astype(BF16)
                yoff = _dot(cg, sgb)
                dc = _dot(ttg, sgb, NT)
                gmat = _dot(cg, bg, NT)
                dgm = jnp.zeros((LS, LS), F32)
                dxdt_pairs = []
                for jj in range(SGW // LANE):
                    h0 = g * (SGW // SP) + 2 * jj
                    ps = slice(g * SGW + jj * LANE, g * SGW + (jj + 1) * LANE)
                    l0 = _head_decay(pre, tril, h0)
                    l1 = _head_decay(pre, tril, h0 + 1)
                    m0 = gmat * l0
                    m1 = gmat * l1
                    dyp = dy[:, ps]
                    dy_lo = jnp.where(lo, dyp, 0.0).astype(BF16)
                    dy_hi = jnp.where(lo, 0.0, dyp).astype(BF16)
                    xpb = xdt[:, ps].astype(BF16)
                    dm0 = _dot(dy_lo, xpb, NT)
                    dm1 = _dot(dy_hi, xpb, NT)
                    q0 = dm0 * m0
                    q1 = dm1 * m1
                    da_rows = da_rows + jnp.where(lane == h0, jnp.sum(q0, axis=1, keepdims=True), 0.0)
                    da_rows = da_rows + jnp.where(lane == h0 + 1, jnp.sum(q1, axis=1, keepdims=True), 0.0)
                    da_cols = da_cols + jnp.where(sub == h0, jnp.sum(q0, axis=0, keepdims=True), 0.0)
                    da_cols = da_cols + jnp.where(sub == h0 + 1, jnp.sum(q1, axis=0, keepdims=True), 0.0)
                    dgm = dgm + dm0 * l0 + dm1 * l1
                    mcat = jnp.concatenate([m0, m1], axis=0).astype(BF16)
                    dycat = jnp.concatenate([dy_lo, dy_hi], axis=0)
                    dxdt_pairs.append(_dot(mcat, dycat, TN))
                dgb = dgm.astype(BF16)
                dc = dc + _dot(dgb, bg)
                db = _dot(dgb, cg, TN)
                zg = _dot(bg, dsgb)
                wg, xdtg = w[:, gs], xdt[:, gs]
                dxdt_g = jnp.concatenate(dxdt_pairs, axis=1) + wg * zg
                qg = zg * xdtg * wg
                last = (jnp.sum(qg, axis=0, keepdims=True)
                        + jnp.sum(dsg * sg, axis=0, keepdims=True) * eal[:, gs])
                daexp_parts.append(dy[:, gs] * ea[:, gs] * yoff - qg + jnp.where(rowi == LS - 1, last, 0.0))
                db = db + _dot((wg * xdtg).astype(BF16), dsgb, NT)
                ds[g] = eal[:, gs] * dsg + _dot(cg, ttg, TN)
                dxdt_parts.append(dxdt_g)
                db_parts.append(db)
                dc_parts.append(dc)
            dxdt = jnp.concatenate(dxdt_parts, axis=1)
            da_exp = jnp.concatenate(daexp_parts, axis=1)
            da = _dot2(da_exp, expand, NT) + da_rows - da_cols.T
            triub = jnp.where(tril, 1.0, 0.0).astype(BF16)
            ddta = _dot3_rhs(triub, da, TN)
            ddt = ddta * pre["a_neg"] + _dot2(dxdt * xs, expand, NT)
            da_ref[...] += jnp.sum(ddta * pre["dt"], axis=0, keepdims=True)
            ddt_raw = ddt * _sigmoid(pre["dtin"])
            ddt_ref[...] = ddt_raw
            ddtb_ref[...] += jnp.sum(ddt_raw, axis=0, keepdims=True)
            dxs = dexp_ref[...] * dy + dxdt * pre["dt_exp"]
            dxbc = jnp.concatenate([dxs] + db_parts + dc_parts, axis=1)
            sx, xc = pre["sx"], pre["xc"]
            dxc = dxbc * sx * (1.0 + xc * (1.0 - sx))
            taps = _rows_up(dxc.astype(BF16), ahead[...], (3, 2, 1)) + [dxc]
            xr = x_ref[...].astype(F32)
            dcb_ref[...] += jnp.sum(dxc, axis=0, keepdims=True)
            dcw_ref[...] += jnp.concatenate([jnp.sum(t * xr, axis=0, keepdims=True) for t in taps], axis=0)
            dxr = cw_ref[0:1, :] * taps[0]
            for k in range(1, SK):
                dxr = dxr + cw_ref[k:k + 1, :] * taps[k]
            stage[:, SI:] = dxr.astype(BF16)
            ahead[...] = dxc[0:H, :]

        dp_ref[...] = stage[...]

    vec = lambda w: pl.BlockSpec((1, w), lambda i, j: (0, 0))
    rev = lambda w, cb_: pl.BlockSpec((LS, w), lambda i, j: (nc - 1 - i, cb_))
    outs = pl.pallas_call(
        body, grid=(nc, NJ),
        in_specs=[rev(SI, 2), rev(SXBC, 2), rev(SXBC, 0),
                  rev(LANE, 0), rev(SI, 0),
                  pl.BlockSpec((1, SG, SN, SGW), lambda i, j: (nc - 1 - i, 0, 0, 0)),
                  rev(SI, 0), pl.BlockSpec(memory_space=pl.ANY),
                  pl.BlockSpec((SK, SXBC), lambda i, j: (0, 0)), vec(LANE), vec(LANE), vec(SI), vec(SI),
                  pl.BlockSpec((LANE, SI), lambda i, j: (0, 0))],
        out_specs=[rev(DP_SSM, 1), rev(LANE, 0),
                   pl.BlockSpec((SK, SXBC), lambda i, j: (0, 0)), vec(SXBC), vec(LANE), vec(LANE), vec(SI), vec(SI)],
        out_shape=[jax.ShapeDtypeStruct(dproj.shape, dproj.dtype), jax.ShapeDtypeStruct((T, LANE), F32),
                   jax.ShapeDtypeStruct((SK, SXBC), F32), jax.ShapeDtypeStruct((1, SXBC), F32),
                   jax.ShapeDtypeStruct((1, LANE), F32), jax.ShapeDtypeStruct((1, LANE), F32),
                   jax.ShapeDtypeStruct((1, SI), F32), jax.ShapeDtypeStruct((1, SI), F32)],
        scratch_shapes=[pltpu.VMEM((H, SXBC), F32),
                        pltpu.VMEM((SG, SN, SGW), F32), pltpu.VMEM((LS, SI + SXBC), BF16)],
        input_output_aliases={7: 0}, name="ssd_bwd",
        compiler_params=_params(("arbitrary", "arbitrary"), VMEM_CAP_MB))(
            proj, proj, xcs, dtraw, y, sprev, dyb, dproj, cw, dtb, alog, dexp, nw, _head_expansion())
    return outs


def _perm_ffn_cols(a):
    lead = a.shape[:-1]
    return a.reshape(lead + (2, 2, FT)).swapaxes(-3, -2).reshape(lead + (2 * DFF,))


def _perm_ffn_rows(a):
    return a.reshape((2, 2, FT) + a.shape[1:]).swapaxes(0, 1).reshape(a.shape)


def _pad_lanes(v, n=LANE):
    return jnp.pad(v, ((0, 0), (0, n - v.shape[-1])))


LATE = ["w_proj_a", "w_proj_b", "w_out", "ffn_w_up_t", "ffn_w_down"]
WGRAD = BF16


class _NoExchange:
    def gather_start(self):
        return None

    def gather_pass_on(self, outs):
        return None

    def late_weights(self, w, outs):
        return w

    def reduce_late(self, grads):
        return None

    def w_in_to_sibling(self, grad_main, grad_dt):
        return None

    def reduce_w_in(self, outs):
        return None

    def reduced(self, late_outs, w_in_outs):
        pass


def _local_step(x, tgt, w, hooks=None):
    hooks = hooks or _NoExchange()

    def mm(*args, side=None, **kw):
        out = _matmul(*args, side=side, **kw)
        return out if side is not None else (out, [])

    win_t = w["w_in_t"]
    win_dt = jnp.pad(w["w_in_t"][PMAIN:], ((0, LANE - SH), (0, 0)))
    fcw = _perm_ffn_cols(w["ffn_conv_w"])
    fcb = _perm_ffn_cols(w["ffn_conv_b"][None, :])
    mixw = w["mix_norm_w"][None, :]
    ffnw = w["ffn_norm_w"][None, :]
    finw = w["final_norm_w"][None, :]
    bst = w["gmlp_bs"].T
    scb = w["ssm_conv_b"][None, :]
    dtb = _pad_lanes(w["ssm_dt_bias"][None, :])
    alog = _pad_lanes(w["ssm_a_log"][None, :])
    dexp = jnp.repeat(w["ssm_d"], SP)[None, :]
    snw = w["ssm_norm_w"][None, :]

    xn = _rms_fwd(x, mixw, name="mix_norm")
    proj, got = mm(xn, win_t, name="in_proj", out_dtype=BF16, tb=True, tn=2304, j_outer=True, b_rows=PMAIN,
                   side=hooks.gather_start())
    dtraw, got = mm(xn, win_dt, name="in_proj_dt", out_dtype=F32, tb=True, side=hooks.gather_pass_on(got))
    w = hooks.late_weights(w, got)
    wup = _perm_ffn_rows(w["ffn_w_up_t"])
    ya_pre = _gmlp_fwd(proj, w["gmlp_ln_w"], w["gmlp_ln_b"], w["gmlp_ws"], bst)
    yb_pre, y_ssd, sprev, ssm_xc = _ssd_fwd(proj, dtraw, w["ssm_conv_w"], scb, dtb, alog, dexp, snw)
    merged, ya, yb = _merge_fwd(ya_pre, yb_pre, proj, w["gate_bias"], w["w_proj_a"], w["w_proj_b"])
    h1 = _matmul(merged, w["w_out"], name="out_proj", out_dtype=F32, add=x)
    hn = _rms_fwd(h1, ffnw, name="ffn_norm")
    up = _matmul(hn, wup, name="ffn_up", out_dtype=BF16, tb=True, tn=2 * FT, j_outer=True)
    act, ffn_xc = _ffn_act_fwd(up, fcw, fcb)
    h2 = _matmul(act, w["ffn_w_down"], name="ffn_down", out_dtype=F32, tk=DFF, add=h1)

    loss_row, dh2, d_finw = _loss_head(h2, tgt, finw)
    dact = _matmul(dh2, w["ffn_w_down"], name="ffn_down_dx", out_dtype=BF16, tb=True, tn=FT)
    d_wdown = _matmul(act, dh2, name="ffn_down_dw", out_dtype=WGRAD, ta=True, tm=FT)
    dup, d_fcw, d_fcb = _ffn_act_bwd(up, ffn_xc, dact, fcw)
    dhn = _matmul(dup, wup, name="ffn_up_dx", out_dtype=F32, tk=2 * FT)
    d_wup = _matmul(dup, hn, name="ffn_up_dw", out_dtype=WGRAD, ta=True, tm=FT, tk=2048)
    dh1, d_ffnw = _rms_bwd(h1, ffnw, dhn, dh2, name="ffn_norm_bwd")
    dmerged = _matmul(dh1, w["w_out"], name="out_proj_dx", out_dtype=BF16, tb=True)
    d_wout = _matmul(merged, dh1, name="out_proj_dw", out_dtype=WGRAD, ta=True)
    dproj, dya, dyb, dya_pre, dyb_pre, d_gbias = _merge_bwd(dmerged, proj, w["gate_bias"], ya, yb,
                                                           w["w_proj_a"], w["w_proj_b"])
    d_wpa = _matmul(ya_pre, dya, name="proj_a_dw", out_dtype=WGRAD, ta=True, tk=2048)
    d_wpb = _matmul(yb_pre, dyb, name="proj_b_dw", out_dtype=WGRAD, ta=True, tk=2048)
    dproj, d_lnw, d_lnb, d_ws, d_bst = _gmlp_bwd(proj, dya_pre, dproj, w["gmlp_ln_w"], w["gmlp_ln_b"],
                                                 w["gmlp_ws"], bst)
    dproj, ddt, d_scw, d_scb, d_dtb, d_a, d_dch, d_snw = _ssd_bwd(
        proj, ssm_xc, dtraw, y_ssd, sprev, dyb_pre, dproj, w["ssm_conv_w"], dtb, alog, dexp, snw)
    late = {"w_proj_a": d_wpa, "w_proj_b": d_wpb, "w_out": d_wout, "ffn_w_up_t": _perm_ffn_rows(d_wup),
            "ffn_w_down": d_wdown}
    gap = (2 * D + 2 * GW, DP_GAP)
    d_win_main, late_outs = mm(dproj, xn, name="in_proj_dw", out_dtype=WGRAD, ta=True, a_gap=gap, tk=2048,
                               side=hooks.reduce_late(late))
    d_win_dt = _matmul(ddt, xn, name="in_proj_dt_dw", out_dtype=F32, ta=True)
    d_win_t = jnp.concatenate([d_win_main, d_win_dt[:SH]], axis=0)
    dxn, got = mm(ddt, win_dt, name="in_proj_dt_dx", out_dtype=F32,
                  side=hooks.w_in_to_sibling(d_win_main, d_win_dt[:SH]))
    dxn, w_in_outs = mm(dproj, win_t, name="in_proj_dx", out_dtype=F32, add=dxn, b_rows=PMAIN, a_gap=gap,
                        side=hooks.reduce_w_in(got))
    hooks.reduced(late_outs, w_in_outs)
    grad_x, d_mixw = _rms_bwd(x, mixw, dxn, dh1, name="mix_norm_bwd")

    a_neg = -jnp.exp(w["ssm_a_log"])
    grads = {
        "mix_norm_w": d_mixw[0],
        "w_in_t": d_win_t,
        "gate_bias": d_gbias,
        "gmlp_ln_w": d_lnw, "gmlp_ln_b": d_lnb, "gmlp_ws": d_ws, "gmlp_bs": d_bst[:, :GG].T,
        "ssm_conv_w": d_scw, "ssm_conv_b": d_scb[0],
        "ssm_dt_bias": d_dtb[0, :SH], "ssm_a_log": d_a[0, :SH] * a_neg,
        "ssm_d": d_dch.reshape(SH, SP).sum(axis=-1), "ssm_norm_w": d_snw[0],
        **late,
        "ffn_norm_w": d_ffnw[0],
        "ffn_conv_w": _perm_ffn_cols(d_fcw), "ffn_conv_b": _perm_ffn_cols(d_fcb)[0],
        "ffn_w_down": d_wdown, "final_norm_w": d_finw[0],
    }
    return loss_row, grad_x, grads


MESH = pl.DeviceIdType.MESH
HBM_SPEC = pl.BlockSpec(memory_space=pltpu.HBM)


def _axes():
    return lax.axis_index("x"), lax.axis_index("y"), lax.axis_index("c")


def _all_gather(shards, *, name):
    na = len(shards)

    def body(*refs):
        x_refs, out_refs = refs[:na], refs[na:2 * na]
        send_sems, recv_sems, local_sems = refs[2 * na:]
        x, y, c = _axes()
        me, sibling = (x, y, c), (x, y, 1 - c)
        chips = [(1 - x, y), (x, 1 - y), (1 - x, 1 - y)]

        def slot(a, px, py, pc):
            return out_refs[a].at[4 * px + 2 * py + pc]

        def copy(a, k, block, to, src=None):
            return pltpu.make_async_remote_copy(
                src_ref=slot(a, *block) if src is None else src, dst_ref=slot(a, *block),
                send_sem=send_sems.at[7 * a + k], recv_sem=recv_sems.at[7 * a + k], device_id=to, device_id_type=MESH)

        mine = [pltpu.make_async_copy(x_refs[a], slot(a, *me), local_sems.at[a]) for a in range(na)]
        for cp in mine:
            cp.start()
        first = []
        for a in range(na):
            first.append(copy(a, 0, me, sibling, src=x_refs[a]))
            first += [copy(a, 1 + j, me, (*chip, c), src=x_refs[a]) for j, chip in enumerate(chips)]
        for cp in first:
            cp.start()
        passed = []
        for j, chip in enumerate(chips):
            for a in range(na):
                copy(a, 1 + j, (*chip, c), me).wait_recv()
                cp = copy(a, 4 + j, (*chip, c), sibling)
                cp.start()
                passed.append(cp)
        for a in range(na):
            copy(a, 0, sibling, me).wait_recv()
        for j, chip in enumerate(chips):
            for a in range(na):
                copy(a, 4 + j, (*chip, 1 - c), me).wait_recv()
        for cp in first + passed:
            cp.wait_send()
        for cp in mine:
            cp.wait()

    return pl.pallas_call(
        body, out_shape=[jax.ShapeDtypeStruct((NDEV,) + s.shape, s.dtype) for s in shards],
        in_specs=[HBM_SPEC] * na, out_specs=[HBM_SPEC] * na,
        scratch_shapes=[pltpu.SemaphoreType.DMA((7 * na,)), pltpu.SemaphoreType.DMA((7 * na,)),
                        pltpu.SemaphoreType.DMA((na,))],
        name=name)(*shards)


def _exchange(srcs, plan, *, name):
    na = len(srcs)
    n = len(plan(0, 0, 0))

    def body(*refs):
        src_refs, out_refs = refs[:na], refs[na:2 * na]
        send_sems, recv_sems = refs[2 * na:]
        x, y, c = _axes()
        copies = []
        for k, (slab, peer) in enumerate(plan(x, y, c)):
            for a in range(na):
                cp = pltpu.make_async_remote_copy(
                    src_ref=src_refs[a].at[slab], dst_ref=out_refs[a].at[k], send_sem=send_sems.at[n * a + k],
                    recv_sem=recv_sems.at[n * a + k], device_id=peer, device_id_type=MESH)
                cp.start()
                copies.append(cp)
        for cp in copies:
            cp.wait()

    return pl.pallas_call(
        body, out_shape=[jax.ShapeDtypeStruct((n,) + s.shape[1:], s.dtype) for s in srcs],
        in_specs=[HBM_SPEC] * na, out_specs=[HBM_SPEC] * na,
        scratch_shapes=[pltpu.SemaphoreType.DMA((n * na,)), pltpu.SemaphoreType.DMA((n * na,))], name=name)(*srcs)


def _to_sibling_plan(x, y, c):
    return [(2 * q + (1 - c), (x, y, 1 - c)) for q in range(4)]


def _to_chips_plan(x, y, c):
    q = 2 * x + y
    return [(q ^ 2, (1 - x, y, c)), (q ^ 1, (x, 1 - y, c)), (q ^ 3, (1 - x, 1 - y, c))]


def _row_tile(rows, row_bytes, budget=2 * 2 ** 20, align=2 * SUBLANE):
    if rows * row_bytes <= 2 * budget:
        return rows
    best = None
    for d in range(align, rows + 1, align):
        if rows % d == 0 and d * row_bytes <= budget:
            best = d
    return best or rows


def _pair_add(g, ra, c_idx, *, name):
    _, _, R, C = g.shape
    tr = _row_tile(R, C * 4, budget=3 * 2 ** 20)

    def body(c_ref, g_ref, ra_ref, o_ref):
        del c_ref
        o_ref[...] = (g_ref[0].astype(F32) + ra_ref[...].astype(F32)).astype(o_ref.dtype)

    return pl.pallas_call(
        body,
        grid_spec=pltpu.PrefetchScalarGridSpec(
            num_scalar_prefetch=1, grid=(4, R // tr),
            in_specs=[pl.BlockSpec((1, 1, tr, C), lambda q, r, cr: (q, cr[0], r, 0)),
                      pl.BlockSpec((1, tr, C), lambda q, r, cr: (q, r, 0))],
            out_specs=pl.BlockSpec((1, tr, C), lambda q, r, cr: (q, r, 0))),
        out_shape=jax.ShapeDtypeStruct((4, R, C), g.dtype), name=name,
        compiler_params=_params(("arbitrary", "arbitrary"), 24))(c_idx, g, ra)


def _grad_sum(p, rb, q_idx, *, name):
    _, R, C = p.shape
    tr = _row_tile(R, C * 4, budget=3 * 2 ** 20)

    def body(q_ref, p_ref, rb_ref, o_ref):
        del q_ref
        g = p_ref[0].astype(F32)
        for k in range(3):
            g = g + rb_ref[k].astype(F32)
        o_ref[...] = g

    return pl.pallas_call(
        body,
        grid_spec=pltpu.PrefetchScalarGridSpec(
            num_scalar_prefetch=1, grid=(R // tr,),
            in_specs=[pl.BlockSpec((1, tr, C), lambda r, qr: (qr[0], r, 0)),
                      pl.BlockSpec((3, tr, C), lambda r, qr: (0, r, 0))],
            out_specs=pl.BlockSpec((tr, C), lambda r, qr: (r, 0))),
        out_shape=jax.ShapeDtypeStruct((R, C), F32), name=name,
        compiler_params=_params(("arbitrary",), 40))(q_idx, p, rb)


def _adamw(g, w, m, v):
    m = ADAM_B1 * m + (1.0 - ADAM_B1) * g
    v = ADAM_B2 * v + (1.0 - ADAM_B2) * (g * g)
    m_hat = m / (1.0 - ADAM_B1 ** ADAM_STEP)
    v_hat = v / (1.0 - ADAM_B2 ** ADAM_STEP)
    delta = -ADAM_LR * (m_hat / (jnp.sqrt(v_hat) + ADAM_EPS) + ADAM_WD * w)
    return delta, m, v


def _adam(g, w, m, v, *, name):
    _, R, C = w.shape
    tr = _row_tile(R, C * 4, budget=2 ** 20, align=SUBLANE)

    def body(g_ref, w_ref, m_ref, v_ref, d_out, m_out, v_out):
        delta, mn, vn = _adamw(g_ref[...], w_ref[...], m_ref[...], v_ref[...])
        d_out[...] = delta
        m_out[...] = mn
        v_out[...] = vn

    row = pl.BlockSpec((1, tr, C), lambda r: (0, r, 0))
    o = jax.ShapeDtypeStruct((1, R, C), F32)
    return pl.pallas_call(
        body, grid=(R // tr,), in_specs=[row, row, row, row], out_specs=[row, row, row], out_shape=[o, o, o],
        name=name, compiler_params=_params(("arbitrary",), 32))(g, w, m, v)


def _vmem_specs(n):
    return [pl.BlockSpec(memory_space=pltpu.VMEM)] * n


def _pair_sum_many(mine, theirs, *, name):
    n = len(mine)

    def body(*refs):
        for a in range(n):
            refs[2 * n + a][...] = refs[a][...] + refs[n + a][0]

    return pl.pallas_call(
        body, out_shape=[jax.ShapeDtypeStruct(m.shape, m.dtype) for m in mine], in_specs=_vmem_specs(2 * n),
        out_specs=_vmem_specs(n), name=name)(*mine, *theirs)


def _chip_sum_many(own, recv, q_idx, *, name):
    n = len(own)

    def body(q_ref, *refs):
        q = q_ref[0]
        for a in range(n):
            mine, r = refs[a][...], refs[n + a]
            total = None
            for chip in range(4):
                e = q ^ chip
                term = jnp.where(e == 0, mine, jnp.where(e == 2, r[0], jnp.where(e == 1, r[1], r[2])))
                total = term if total is None else total + term
            refs[2 * n + a][...] = total

    return pl.pallas_call(
        body, out_shape=[jax.ShapeDtypeStruct(m.shape, m.dtype) for m in own],
        in_specs=[pl.BlockSpec(memory_space=pltpu.SMEM)] + _vmem_specs(2 * n), out_specs=_vmem_specs(n),
        name=name)(q_idx, *own, *recv)


def _adam_many(gs, ws, ms, vs, *, name):
    n = len(gs)

    def body(*refs):
        for a in range(n):
            delta, mn, vn = _adamw(*(refs[k * n + a][...] for k in range(4)))
            refs[4 * n + a][...] = delta
            refs[5 * n + a][...] = mn
            refs[6 * n + a][...] = vn

    shapes = [jax.ShapeDtypeStruct(w.shape, w.dtype) for w in ws]
    out = pl.pallas_call(body, out_shape=shapes * 3, in_specs=_vmem_specs(4 * n), out_specs=_vmem_specs(3 * n),
                         name=name)(*gs, *ws, *ms, *vs)
    return out[:n], out[n:2 * n], out[2 * n:]


WEIGHTS = ["mix_norm_w", "w_in", "gate_bias", "gmlp_ln_w", "gmlp_ln_b", "gmlp_ws", "gmlp_bs", "ssm_conv_w",
           "ssm_conv_b", "ssm_dt_bias", "ssm_a_log", "ssm_d", "ssm_norm_w", "w_proj_a", "w_proj_b", "w_out",
           "ffn_norm_w", "ffn_w_up", "ffn_conv_w", "ffn_conv_b", "ffn_w_down", "final_norm_w"]
SHARDED = {"w_in": ((D, IN_COLS), 1), "gate_bias": ((2, D), 1), "ssm_conv_w": ((SK, SXBC), 1),
           "w_proj_a": ((GW, D), 0), "w_proj_b": ((SI, D), 0), "w_out": ((D, D), 0),
           "ffn_w_up": ((D, 2 * DFF), 1), "ffn_conv_w": ((FK, 2 * DFF), 1), "ffn_w_down": ((DFF, D), 0)}
REPLICATED = {"mix_norm_w": (D,), "gmlp_ln_w": (GG, GD), "gmlp_ln_b": (GG, GD), "gmlp_ws": (GG, GB, GB),
              "gmlp_bs": (GG, GB), "ssm_conv_b": (SXBC,), "ssm_dt_bias": (SH,), "ssm_a_log": (SH,), "ssm_d": (SH,),
              "ssm_norm_w": (SI,), "ffn_norm_w": (D,), "ffn_conv_b": (2 * DFF,), "final_norm_w": (D,)}
REPL_ORDER = [n for n in WEIGHTS if n in REPLICATED]
BTILE = 2 * SUBLANE
WIN_R = IN_COLS // NDEV
WIN_P = WIN_R + BTILE - WIN_R % BTILE
WIN_A = [WIN_R * d // BTILE * BTILE for d in range(NDEV)]
assert all(WIN_A[d] + WIN_P >= WIN_R * (d + 1) for d in range(NDEV)) and WIN_A[-1] + WIN_P == IN_COLS
BIG = [("w_proj_a", GW // NDEV, False), ("w_proj_b", SI // NDEV, False), ("w_out", D // NDEV, False),
       ("ffn_w_up", 2 * DFF // NDEV, True), ("ffn_w_down", DFF // NDEV, False), ("w_in", WIN_P, True)]
VECTORS = ["gate_bias", "ssm_conv_w", "ffn_conv_w"]


def _round_up(n, k):
    return (n + k - 1) // k * k


BIG_OFF = {}
_off = 0
for _n, _r, _t in BIG:
    BIG_OFF[_n] = _off
    _off += _r
BIG_USED = _off
BIG_ROWS = _round_up(BIG_USED, 2 * SUBLANE)
assert all(BIG_OFF[n] % (2 * SUBLANE) == 0 for n, _, _ in BIG)
VEC_SHAPE = {n: (SHARDED[n][0][0], SHARDED[n][0][1] // NDEV) for n in VECTORS}


def _win_offset(dev):
    return WIN_R * dev - WIN_R * dev // BTILE * BTILE


def _pack_big(arrs, dtype, dev):
    parts = []
    for n, r, t in BIG:
        a = (arrs[n].T if t else arrs[n]).astype(dtype)
        if n == "w_in":
            a = lax.dynamic_update_slice(jnp.zeros((WIN_P, D), dtype), a, (_win_offset(dev), 0))
        parts.append(a)
    parts.append(jnp.zeros((BIG_ROWS - BIG_USED, D), dtype))
    return jnp.concatenate(parts, axis=0)


def _join_windows(win):
    parts = []
    for d in range(NDEV):
        lo = BTILE if WIN_A[d] % WIN_R else 0
        if lo:
            parts.append(win[d - 1, WIN_P - BTILE:] + win[d, :BTILE])
        hi = WIN_P - BTILE if d + 1 < NDEV and WIN_A[d + 1] < WIN_A[d] + WIN_P else WIN_P
        parts.append(win[d, lo:hi])
    return jnp.concatenate(parts, axis=0)


def _split_windows(main, last):
    assert WIN_A[-2] + WIN_P <= PMAIN
    wins = [main[a:a + WIN_P] for a in WIN_A[:-1]]
    return jnp.stack(wins + [jnp.concatenate([main[WIN_A[-1]:], last], axis=0)])


LATE_ROWS = BIG_OFF["w_in"]
assert LATE_ROWS + WIN_P == BIG_ROWS and BIG[-1][0] == "w_in"


def _remote(src, dst, send_sems, recv_sems, k, to):
    return pltpu.make_async_remote_copy(src_ref=src, dst_ref=dst, send_sem=send_sems.at[k], recv_sem=recv_sems.at[k],
                                        device_id=to, device_id_type=MESH)


class _Exchange:
    def __init__(self, late_shard, c_idx):
        self.late_shard, self.c_idx = late_shard, c_idx

    def gather_start(self):
        shard = self.late_shard

        def make(ins, outs, send_sems, recv_sems):
            (x_ref,), (out,) = ins, outs
            x, y, c = _axes()
            mine = out.at[4 * x + 2 * y + c]
            peers = [(x, y, 1 - c), (1 - x, y, c), (x, 1 - y, c), (1 - x, 1 - y, c)]
            copies = [_remote(x_ref, mine, send_sems, recv_sems, k, p) for k, p in enumerate(peers)]
            return copies + [pltpu.make_async_copy(x_ref, mine, send_sems.at[len(peers)])]

        return _Side([shard], [jax.ShapeDtypeStruct((NDEV,) + shard.shape, shard.dtype)], 5, make)

    def gather_pass_on(self, outs):
        (buf,) = outs

        def make(ins, outs, send_sems, recv_sems):
            (src,), (dst,) = ins, outs
            x, y, c = _axes()
            slots = [4 * px + 2 * py + c for px, py in [(1 - x, y), (x, 1 - y), (1 - x, 1 - y)]]
            return [_remote(src.at[s], dst.at[s], send_sems, recv_sems, k, (x, y, 1 - c)) for k, s in enumerate(slots)]

        return _Side([buf], [jax.ShapeDtypeStruct(buf.shape, buf.dtype)], 3, make, aliases=[(0, 0)])

    def late_weights(self, w, outs):
        (buf,) = outs
        w = dict(w)
        for n, r, t in BIG[:-1]:
            w[n + "_t" if t else n] = buf[:, BIG_OFF[n]:BIG_OFF[n] + r].reshape(NDEV * r, D)
        return w

    @staticmethod
    def _plan_side(src, plan):
        n = len(plan(0, 0, 0))

        def make(ins, outs, send_sems, recv_sems):
            (s,), (dst,) = ins, outs
            return [_remote(s.at[slab], dst.at[k], send_sems, recv_sems, k, peer)
                    for k, (slab, peer) in enumerate(plan(*_axes()))]

        return _Side([src], [jax.ShapeDtypeStruct((n,) + src.shape[1:], src.dtype)], n, make)

    def _to_chips(self, send, sib, tag):
        sums = _pair_add(send.reshape((4, 2) + send.shape[1:]), sib, self.c_idx, name=tag + "_grad_pair_add")
        return sums, self._plan_side(sums, _to_chips_plan)

    def reduce_late(self, grads):
        send = jnp.concatenate([grads[n + "_t" if t else n].reshape(NDEV, r, D) for n, r, t in BIG[:-1]], axis=1)
        send = send.astype(BF16)
        (sib,) = _exchange([send], _to_sibling_plan, name="late_grads_to_sibling")
        self.late_sum, side = self._to_chips(send, sib, "late")
        return side

    def w_in_to_sibling(self, grad_main, grad_dt):
        self.w_in_send = _split_windows(grad_main.astype(BF16), grad_dt.astype(BF16))
        return self._plan_side(self.w_in_send, _to_sibling_plan)

    def reduce_w_in(self, outs):
        self.w_in_sum, side = self._to_chips(self.w_in_send, outs[0], "w_in")
        return side

    def reduced(self, late_outs, w_in_outs):
        (self.late_from_chips,), (self.w_in_from_chips,) = late_outs, w_in_outs


def kernel(x, mix_norm_w, w_in, gate_bias, gmlp_ln_w, gmlp_ln_b, gmlp_ws, gmlp_bs, ssm_conv_w, ssm_conv_b, ssm_dt_bias, ssm_a_log, ssm_d, ssm_norm_w, w_proj_a, w_proj_b, w_out, ffn_norm_w, ffn_w_up, ffn_conv_w, ffn_conv_b, ffn_w_down, final_norm_w, loss_target, m_mix_norm_w, m_w_in, m_gate_bias, m_gmlp_ln_w, m_gmlp_ln_b, m_gmlp_ws, m_gmlp_bs, m_ssm_conv_w, m_ssm_conv_b, m_ssm_dt_bias, m_ssm_a_log, m_ssm_d, m_ssm_norm_w, m_w_proj_a, m_w_proj_b, m_w_out, m_ffn_norm_w, m_ffn_w_up, m_ffn_conv_w, m_ffn_conv_b, m_ffn_w_down, m_final_norm_w, v_mix_norm_w, v_w_in, v_gate_bias, v_gmlp_ln_w, v_gmlp_ln_b, v_gmlp_ws, v_gmlp_bs, v_ssm_conv_w, v_ssm_conv_b, v_ssm_dt_bias, v_ssm_a_log, v_ssm_d, v_ssm_norm_w, v_w_proj_a, v_w_proj_b, v_w_out, v_ffn_norm_w, v_ffn_w_up, v_ffn_conv_w, v_ffn_conv_b, v_ffn_w_down, v_final_norm_w):
    given = dict(locals())
    wts = {n: given[n] for n in WEIGHTS}
    mom = {n: given["m_" + n] for n in WEIGHTS}
    var = {n: given["v_" + n] for n in WEIGHTS}
    xi, yi, ci = _axes()
    c_idx = jnp.reshape(ci, (1,)).astype(jnp.int32)
    q_idx = jnp.reshape(2 * xi + yi, (1,)).astype(jnp.int32)
    big_names = [n for n, _, _ in BIG]
    drop = lambda d, names: {n: d[n][0] for n in names}

    dev = 4 * xi + 2 * yi + ci
    packed = _pack_big(drop(wts, big_names), BF16, dev)
    gathered = _all_gather([packed[LATE_ROWS:]] + [wts[n] for n in VECTORS], name="w_in_all_gather")
    full = {"w_in_t": _join_windows(gathered[0])}
    for n, a in zip(VECTORS, gathered[1:]):
        r, c = VEC_SHAPE[n]
        full[n] = a[:, 0].transpose(1, 0, 2).reshape(r, NDEV * c)
    for n in REPL_ORDER:
        full[n] = wts[n].reshape(REPLICATED[n])

    hooks = _Exchange(packed[:LATE_ROWS], c_idx)
    loss_local, grad_x, grads = _local_step(x[0], loss_target[0], full, hooks)
    g_late = _grad_sum(hooks.late_sum, hooks.late_from_chips, q_idx, name="late_grad_sum")
    g_win = _grad_sum(hooks.w_in_sum, hooks.w_in_from_chips, q_idx, name="w_in_grad_sum")

    small = VECTORS + REPL_ORDER
    as_2d = lambda a: a if a.ndim >= 2 else a[None]
    part = [grads[n].reshape((1,) + SHARDED[n][0] if n in VECTORS else as_2d(wts[n]).shape) for n in small]
    part.append(loss_local)
    from_sibling = _exchange([p[None] for p in part], lambda x, y, c: [(0, (x, y, 1 - c))],
                             name="small_grads_to_sibling")
    chip_sums = _pair_sum_many(part, from_sibling, name="small_grad_pair_sum")
    from_chips = _exchange([s[None] for s in chip_sums],
                           lambda x, y, c: [(0, (1 - x, y, c)), (0, (x, 1 - y, c)), (0, (1 - x, 1 - y, c))],
                           name="small_grads_to_chips")
    totals = _chip_sum_many(chip_sums, from_chips, q_idx, name="small_grad_chip_sum")
    g_small, loss = dict(zip(small, totals)), totals[-1][0, 0]
    for n in VECTORS:
        c = VEC_SHAPE[n][1]
        g_small[n] = lax.dynamic_slice_in_dim(g_small[n], dev * c, c, axis=2)

    outs = {}
    small_g = [g_small[n] for n in small]
    small_out = _adam_many(small_g, *[[as_2d(d[n]) for n in small] for d in (wts, mom, var)], name="adam_small")
    for i, n in enumerate(small):
        outs[n] = tuple(a[i].reshape(wts[n].shape) for a in (small_g,) + tuple(small_out))
    for n, r, t in BIG:
        if n == "w_in":
            g = lax.dynamic_slice(g_win, (_win_offset(dev), 0), (WIN_R, D))
        else:
            g = g_late[BIG_OFF[n]:BIG_OFF[n] + r]
        flip = (lambda a: a.transpose(0, 2, 1)) if t else (lambda a: a)
        g = g[None]
        new = _adam(g, flip(wts[n]), flip(mom[n]), flip(var[n]), name="adam_" + n)
        outs[n] = tuple(flip(a) for a in (g,) + tuple(new))
    return (loss, grad_x[None]) + tuple(outs[n][k] for k in range(4) for n in WEIGHTS)
```

```python
import functools

import jax
import jax.numpy as jnp
from jax import lax
from jax.experimental import pallas as pl
from jax.experimental.pallas import tpu as pltpu

F32 = jnp.float32
BF16 = jnp.bfloat16

D = 1024
EPS = 1e-5
GW = 1024
GB = 128
GG = 8
GD = 128
GCH = 64
SI = 2048
SH = 32
SP = 64
SG = 4
SN = 128
SGW = SI // SG
SK = 4
SXBC = SI + 2 * SG * SN
DFF = 2816
FK = 3
PMAIN = 2 * D + 2 * GW + SI + SXBC
IN_COLS = PMAIN + SH
DP_SSM = SI + SXBC
DP_GAP = (DP_SSM - (2 * D + 2 * GW) % DP_SSM) % DP_SSM
DP_COLS = 2 * D + 2 * GW + DP_GAP + DP_SSM
assert DP_GAP % D == 0 and (2 * D + 2 * GW) % D == 0
NDEV = 8
ADAM_LR, ADAM_B1, ADAM_B2, ADAM_EPS, ADAM_WD, ADAM_STEP = 0.001, 0.9, 0.999, 1e-08, 0.01, 10

LANE = 128
SUBLANE = 8
VMEM_MB_V7X = 64
VMEM_CAP_MB = VMEM_MB_V7X - 8

LS = 128
FT = DFF // 2

NN = (((1,), (0,)), ((), ()))
NT = (((1,), (1,)), ((), ()))
TN = (((0,), (0,)), ((), ()))


def _params(sem, vmem_mb):
    return pltpu.CompilerParams(dimension_semantics=sem,
                                vmem_limit_bytes=min(int(vmem_mb), VMEM_CAP_MB) * 1024 * 1024)


def _dot(a, b, dims=NN):
    return lax.dot_general(a, b, dims, preferred_element_type=F32)


def _sigmoid(x):
    return 1.0 / (1.0 + jnp.exp(-x))


def _split3(v):
    hi = v.astype(BF16)
    r = v - hi.astype(F32)
    mid = r.astype(BF16)
    lo = (r - mid.astype(F32)).astype(BF16)
    return hi, mid, lo


def _dot3(a_f32, b_bf16, dims):
    hi, mid, lo = _split3(a_f32)
    return _dot(hi, b_bf16, dims) + _dot(mid, b_bf16, dims) + _dot(lo, b_bf16, dims)


def _dot2(a_f32, b_bf16, dims):
    hi, mid, _ = _split3(a_f32)
    return _dot(hi, b_bf16, dims) + _dot(mid, b_bf16, dims)


def _dot3_rhs(a_bf16, b_f32, dims):
    hi, mid, lo = _split3(b_f32)
    return _dot(a_bf16, hi, dims) + _dot(a_bf16, mid, dims) + _dot(a_bf16, lo, dims)


def _matmul(a, b, *, name, out_dtype, ta=False, tb=False, tm=1024, tn=1024, tk=1024, add=None,
            j_outer=False, b_rows=None, a_gap=None, side=None):
    gap0, gapw = a_gap or (0, 0)
    if ta:
        K, M = a.shape
        M -= gapw
    else:
        M, K = a.shape
        K -= gapw
    if tb:
        N, K2 = b.shape
        N = b_rows or N
    else:
        K2, N = b.shape
        K2 = b_rows or K2
    assert K == K2, (a.shape, b.shape, ta, tb)
    tm, tn, tk = min(tm, M), min(tn, N), min(tk, K)
    assert M % tm == 0 and N % tn == 0 and K % tk == 0, (M, N, K, tm, tn, tk)
    nk = K // tk
    dims = (((0 if ta else 1,), (1 if tb else 0,)), ((), ()))
    has_add = add is not None
    n_in = 3 if has_add else 2
    s_in = len(side.inputs) if side else 0
    s_out = len(side.out_shapes) if side else 0
    grid = (N // tn, M // tm, nk) if j_outer else (M // tm, N // tn, nk)

    def body(*refs):
        a_ref, b_ref = refs[:2]
        add_ref = refs[2] if has_add else None
        o_ref = refs[n_in + s_in]
        if side:
            side_refs = (refs[n_in:n_in + s_in], refs[n_in + s_in + 1:n_in + s_in + 1 + s_out]) + tuple(refs[-2:])
            ids = [pl.program_id(d) for d in range(3)]
            first = functools.reduce(jnp.logical_and, [i == 0 for i in ids])
            last = functools.reduce(jnp.logical_and, [i == g - 1 for i, g in zip(ids, grid)])

            @pl.when(first)
            def _():
                for cp in side.make(*side_refs):
                    cp.start()

            @pl.when(last)
            def _():
                for cp in side.make(*side_refs):
                    cp.wait()

        p = lax.dot_general(a_ref[...].astype(BF16), b_ref[...].astype(BF16), dims,
                            preferred_element_type=F32)

        def finish(acc):
            if has_add:
                acc = acc + add_ref[...].astype(F32)
            o_ref[...] = acc.astype(o_ref.dtype)

        if nk == 1:
            finish(p)
        else:
            acc_ref = refs[n_in + s_in + 1 + s_out]
            k = pl.program_id(2)

            @pl.when(k == 0)
            def _():
                acc_ref[...] = p

            @pl.when(jnp.logical_and(k > 0, k < nk - 1))
            def _():
                acc_ref[...] += p

            @pl.when(k == nk - 1)
            def _():
                finish(acc_ref[...] + p)

    if j_outer:
        ij = lambda g0, g1: (g1, g0)
    else:
        ij = lambda g0, g1: (g0, g1)

    ta_col = tm if ta else tk
    assert gap0 % ta_col == 0 and gapw % ta_col == 0, (a_gap, ta_col)

    def a_map(g0, g1, k):
        i, _ = ij(g0, g1)
        col = i if ta else k
        col = col + jnp.where(col >= gap0 // ta_col, gapw // ta_col, 0) if gapw else col
        return (k, col) if ta else (i, col)

    def b_map(g0, g1, k):
        _, j = ij(g0, g1)
        return (j, k) if tb else (k, j)

    def o_map(g0, g1, k):
        return ij(g0, g1)

    in_specs = [pl.BlockSpec((tk, tm) if ta else (tm, tk), a_map),
                pl.BlockSpec((tn, tk) if tb else (tk, tn), b_map)]
    args = [a, b]
    if has_add:
        in_specs.append(pl.BlockSpec((tm, tn), o_map))
        args.append(add)
    scratch = [pltpu.VMEM((tm, tn), F32)] if nk > 1 else []
    osz = jnp.dtype(out_dtype).itemsize
    est = (2 * (tm * tk * a.dtype.itemsize + tk * tn * b.dtype.itemsize) + 2 * tm * tn * osz
           + (2 * tm * tn * add.dtype.itemsize if has_add else 0)
           + 3 * tm * tn * 4 + 2 * (tm * tk + tk * tn)) / 2 ** 20 + 4
    out_specs = [pl.BlockSpec((tm, tn), o_map)]
    out_shape = [jax.ShapeDtypeStruct((M, N), out_dtype)]
    aliases = {}
    if side:
        hbm = pl.BlockSpec(memory_space=pltpu.HBM)
        in_specs += [hbm] * s_in
        args += list(side.inputs)
        out_specs += [hbm] * s_out
        out_shape += list(side.out_shapes)
        scratch += [pltpu.SemaphoreType.DMA((side.nsem,)), pltpu.SemaphoreType.DMA((side.nsem,))]
        aliases = {n_in + i: 1 + j for i, j in side.aliases}
    outs = pl.pallas_call(
        body, grid=grid, in_specs=in_specs, out_specs=out_specs, out_shape=out_shape, scratch_shapes=scratch,
        input_output_aliases=aliases, name=name,
        compiler_params=_params(("arbitrary", "arbitrary", "arbitrary"), est))(*args)
    return (outs[0], list(outs[1:])) if side else outs[0]


class _Side:
    def __init__(self, inputs, out_shapes, nsem, make, aliases=()):
        self.inputs, self.out_shapes, self.nsem, self.make, self.aliases = inputs, out_shapes, nsem, make, aliases


def _rms_fwd(x, w, *, name):
    T = x.shape[0]
    tm = min(512, T)

    def body(x_ref, w_ref, o_ref):
        xv = x_ref[...]
        r = lax.rsqrt(jnp.mean(xv * xv, axis=-1, keepdims=True) + EPS)
        o_ref[...] = (xv * r * w_ref[...]).astype(BF16)

    return pl.pallas_call(
        body, grid=(T // tm,),
        in_specs=[pl.BlockSpec((tm, D), lambda i: (i, 0)), pl.BlockSpec((1, D), lambda i: (0, 0))],
        out_specs=pl.BlockSpec((tm, D), lambda i: (i, 0)),
        out_shape=jax.ShapeDtypeStruct((T, D), BF16), name=name,
        compiler_params=_params(("arbitrary",), 24))(x, w)


def _rms_bwd(x, w, dy, dres, *, name):
    T = x.shape[0]
    tm = min(512, T)

    def body(x_ref, w_ref, dy_ref, dres_ref, dx_ref, dw_ref):
        xv = x_ref[...]
        r = lax.rsqrt(jnp.mean(xv * xv, axis=-1, keepdims=True) + EPS)
        xhat = xv * r
        dyv = dy_ref[...].astype(F32)
        g = dyv * w_ref[...]
        dx_ref[...] = dres_ref[...] + r * (g - xhat * jnp.mean(g * xhat, axis=-1, keepdims=True))
        part = jnp.sum(dyv * xhat, axis=0, keepdims=True)

        @pl.when(pl.program_id(0) == 0)
        def _():
            dw_ref[...] = part

        @pl.when(pl.program_id(0) > 0)
        def _():
            dw_ref[...] += part

    row = pl.BlockSpec((tm, D), lambda i: (i, 0))
    vec = pl.BlockSpec((1, D), lambda i: (0, 0))
    return pl.pallas_call(
        body, grid=(T // tm,), in_specs=[row, vec, row, row], out_specs=[row, vec],
        out_shape=[jax.ShapeDtypeStruct((T, D), F32), jax.ShapeDtypeStruct((1, D), F32)], name=name,
        compiler_params=_params(("arbitrary",), 32))(x, w, dy, dres)


def _loss_head(h, tgt, w):
    T = h.shape[0]
    tm = min(512, T)

    def body(h_ref, t_ref, w_ref, loss_ref, dh_ref, dw_ref):
        hv = h_ref[...]
        r = lax.rsqrt(jnp.mean(hv * hv, axis=-1, keepdims=True) + EPS)
        xhat = hv * r
        wv = w_ref[...]
        err = xhat * wv - t_ref[...]
        lpart = 0.5 * jnp.sum(jnp.mean(err * err, axis=-1, keepdims=True), axis=0, keepdims=True)
        dy = err * (1.0 / D)
        g = dy * wv
        dh_ref[...] = r * (g - xhat * jnp.mean(g * xhat, axis=-1, keepdims=True))
        wpart = jnp.sum(dy * xhat, axis=0, keepdims=True)
        lrow = jnp.broadcast_to(lpart, (1, LANE))

        @pl.when(pl.program_id(0) == 0)
        def _():
            dw_ref[...] = wpart
            loss_ref[...] = lrow

        @pl.when(pl.program_id(0) > 0)
        def _():
            dw_ref[...] += wpart
            loss_ref[...] += lrow

    row = pl.BlockSpec((tm, D), lambda i: (i, 0))
    vec = pl.BlockSpec((1, D), lambda i: (0, 0))
    return pl.pallas_call(
        body, grid=(T // tm,), in_specs=[row, row, vec],
        out_specs=[pl.BlockSpec((1, LANE), lambda i: (0, 0)), row, vec],
        out_shape=[jax.ShapeDtypeStruct((1, LANE), F32), jax.ShapeDtypeStruct((T, D), F32),
                   jax.ShapeDtypeStruct((1, D), F32)], name="loss_head",
        compiler_params=_params(("arbitrary",), 32))(h, tgt, w)


_GELU_C = 0.7978845608028654
_GELU_A = 0.044715


def _gelu(x, with_grad=False):
    x2 = x * x
    cx = _GELU_C * x
    t = jnp.tanh(cx * (1.0 + _GELU_A * x2))
    h = 0.5 * (1.0 + t)
    if not with_grad:
        return x * h
    return x * h, h + 0.5 * cx * (1.0 - t * t) * (1.0 + 3.0 * _GELU_A * x2)


def _gmlp_mask():
    r = lax.broadcasted_iota(jnp.int32, (GB, GB), 0) // GCH
    c = lax.broadcasted_iota(jnp.int32, (GB, GB), 1) // GCH
    return c <= r


def _gmlp_fwd(proj, lnw, lnb, ws, bst):
    T = proj.shape[0]
    tm = min(512, T)
    nblk = tm // GB

    def body(u_ref, v_ref, lnw_ref, lnb_ref, ws_ref, bst_ref, o_ref):
        mask = _gmlp_mask()
        u = _gelu(u_ref[...].astype(F32))
        v = _gelu(v_ref[...].astype(F32))
        for g in range(GG):
            cs = slice(g * GD, (g + 1) * GD)
            vg = v[:, cs]
            mu = jnp.mean(vg, axis=-1, keepdims=True)
            vc = vg - mu
            var = jnp.mean(vc * vc, axis=-1, keepdims=True)
            vn = (vc * lax.rsqrt(var + EPS) * lnw_ref[g:g + 1, :] + lnb_ref[g:g + 1, :]).astype(BF16)
            wsg = jnp.where(mask, ws_ref[g], 0.0).astype(BF16)
            bcol = bst_ref[:, g:g + 1]
            for blk in range(nblk):
                rs = slice(blk * GB, (blk + 1) * GB)
                sv = _dot(wsg, vn[rs, :]) + bcol
                o_ref[rs, cs] = (u[rs, cs] * sv).astype(BF16)

    full = lambda shape: pl.BlockSpec(shape, lambda i: tuple(0 for _ in shape))
    return pl.pallas_call(
        body, grid=(T // tm,),
        in_specs=[pl.BlockSpec((tm, GW), lambda i: (i, 2)), pl.BlockSpec((tm, GW), lambda i: (i, 3)),
                  full((GG, GD)), full((GG, GD)), full((GG, GB, GB)), full((GB, GG))],
        out_specs=pl.BlockSpec((tm, GW), lambda i: (i, 0)),
        out_shape=jax.ShapeDtypeStruct((T, GW), BF16), name="gmlp_fwd",
        compiler_params=_params(("arbitrary",), 40))(proj, proj, lnw, lnb, ws, bst)


def _gmlp_bwd(proj, dya, dproj, lnw, lnb, ws, bst):
    T = proj.shape[0]
    tm = min(512, T)
    nblk = tm // GB

    def body(u_ref, v_ref, dya_ref, dproj_in, lnw_ref, lnb_ref, ws_ref, bst_ref,
             dz_ref, dlnw_ref, dlnb_ref, dws_ref, dbst_ref):
        del dproj_in
        first = pl.program_id(0) == 0

        @pl.when(first)
        def _():
            dlnw_ref[...] = jnp.zeros_like(dlnw_ref)
            dlnb_ref[...] = jnp.zeros_like(dlnb_ref)
            dws_ref[...] = jnp.zeros_like(dws_ref)
            dbst_ref[...] = jnp.zeros_like(dbst_ref)

        mask = _gmlp_mask()
        lane = lax.broadcasted_iota(jnp.int32, (GB, LANE), 1)
        ur = u_ref[...].astype(F32)
        vr = v_ref[...].astype(F32)
        u, gu = _gelu(ur, with_grad=True)
        v, gv = _gelu(vr, with_grad=True)
        dy = dya_ref[...].astype(F32)
        dbst = jnp.zeros((GB, LANE), F32)
        dlnw_rows, dlnb_rows = [], []
        for g in range(GG):
            cs = slice(g * GD, (g + 1) * GD)
            vg = v[:, cs]
            mu = jnp.mean(vg, axis=-1, keepdims=True)
            vc = vg - mu
            var = jnp.mean(vc * vc, axis=-1, keepdims=True)
            rstd = lax.rsqrt(var + EPS)
            xhat = vc * rstd
            lw = lnw_ref[g:g + 1, :]
            vn = (xhat * lw + lnb_ref[g:g + 1, :]).astype(BF16)
            wsg = jnp.where(mask, ws_ref[g], 0.0).astype(BF16)
            bcol = bst_ref[:, g:g + 1]
            dyg = dy[:, cs]
            ug = u[:, cs]
            dsv = dyg * ug
            dsv_b = dsv.astype(BF16)
            dws_g = jnp.zeros((GB, GB), F32)
            bsum = jnp.zeros((GB, 1), F32)
            dvn_parts = []
            for blk in range(nblk):
                rs = slice(blk * GB, (blk + 1) * GB)
                sv = _dot(wsg, vn[rs, :]) + bcol
                dz_ref[rs, cs] = (dyg[rs, :] * sv * gu[rs, cs]).astype(BF16)
                dws_g = dws_g + _dot(dsv_b[rs, :], vn[rs, :], NT)
                bsum = bsum + jnp.sum(dsv[rs, :], axis=-1, keepdims=True)
                dvn_parts.append(_dot(wsg, dsv_b[rs, :], TN))
            dvn = jnp.concatenate(dvn_parts, axis=0)
            dws_ref[g] += jnp.where(mask, dws_g, 0.0)
            dbst = dbst + jnp.where(lane == g, bsum, 0.0)
            dlnw_rows.append(jnp.sum(dvn * xhat, axis=0, keepdims=True))
            dlnb_rows.append(jnp.sum(dvn, axis=0, keepdims=True))
            dxh = dvn * lw
            dvg = rstd * (dxh - jnp.mean(dxh, axis=-1, keepdims=True)
                          - xhat * jnp.mean(dxh * xhat, axis=-1, keepdims=True))
            dz_ref[:, GW + g * GD:GW + (g + 1) * GD] = (dvg * gv[:, cs]).astype(BF16)
        dlnw_ref[...] += jnp.concatenate(dlnw_rows, axis=0)
        dlnb_ref[...] += jnp.concatenate(dlnb_rows, axis=0)
        dbst_ref[...] += dbst

    full = lambda shape: pl.BlockSpec(shape, lambda i: tuple(0 for _ in shape))
    outs = pl.pallas_call(
        body, grid=(T // tm,),
        in_specs=[pl.BlockSpec((tm, GW), lambda i: (i, 2)), pl.BlockSpec((tm, GW), lambda i: (i, 3)),
                  pl.BlockSpec((tm, GW), lambda i: (i, 0)), pl.BlockSpec(memory_space=pl.ANY),
                  full((GG, GD)), full((GG, GD)), full((GG, GB, GB)), full((GB, GG))],
        out_specs=[pl.BlockSpec((tm, 2 * GW), lambda i: (i, 1)), full((GG, GD)), full((GG, GD)),
                   full((GG, GB, GB)), full((GB, LANE))],
        out_shape=[jax.ShapeDtypeStruct(dproj.shape, dproj.dtype), jax.ShapeDtypeStruct((GG, GD), F32),
                   jax.ShapeDtypeStruct((GG, GD), F32), jax.ShapeDtypeStruct((GG, GB, GB), F32),
                   jax.ShapeDtypeStruct((GB, LANE), F32)],
        input_output_aliases={3: 0}, name="gmlp_bwd",
        compiler_params=_params(("arbitrary",), 48))(proj, proj, dya, dproj, lnw, lnb, ws, bst)
    return outs


def _merge_fwd(ya_pre, yb_pre, proj, bias, wpa, wpb):
    T = proj.shape[0]
    tm = min(512, T)

    def body(ya_ref, yb_ref, g_ref, b_ref, wpa_ref, wpb_ref, m_ref, oa_ref, ob_ref):
        ya = _dot(ya_ref[...], wpa_ref[...])
        yb = _dot(yb_ref[...], wpb_ref[...])
        g = g_ref[...].astype(F32)
        sa = _sigmoid(g[:, :D] + b_ref[0:1, :])
        sb = _sigmoid(g[:, D:] + b_ref[1:2, :])
        m_ref[...] = (sa * ya + sb * yb).astype(BF16)
        oa_ref[...] = ya.astype(BF16)
        ob_ref[...] = yb.astype(BF16)

    row = lambda w: pl.BlockSpec((tm, w), lambda i: (i, 0))
    full = lambda shape: pl.BlockSpec(shape, lambda i: tuple(0 for _ in shape))
    o = jax.ShapeDtypeStruct((T, D), BF16)
    return pl.pallas_call(
        body, grid=(T // tm,),
        in_specs=[row(GW), row(SI), row(2 * D), full((2, D)), full((GW, D)), full((SI, D))],
        out_specs=[row(D), row(D), row(D)], out_shape=[o, o, o], name="merge_fwd",
        compiler_params=_params(("arbitrary",), 40))(ya_pre, yb_pre, proj, bias, wpa, wpb)


def _merge_bwd(dm, proj, bias, ya, yb, wpa, wpb):
    T = proj.shape[0]
    tm = min(512, T)

    def body(dm_ref, g_ref, b_ref, ya_ref, yb_ref, wpa_ref, wpb_ref,
             dg_ref, dya_ref, dyb_ref, dpa_ref, dpb_ref, db_ref):
        dmv = dm_ref[...].astype(F32)
        g = g_ref[...].astype(F32)
        sa = _sigmoid(g[:, :D] + b_ref[0:1, :])
        sb = _sigmoid(g[:, D:] + b_ref[1:2, :])
        dya = (dmv * sa).astype(BF16)
        dyb = (dmv * sb).astype(BF16)
        dga = dmv * ya_ref[...].astype(F32) * sa * (1.0 - sa)
        dgb = dmv * yb_ref[...].astype(F32) * sb * (1.0 - sb)
        dg_ref[:, :D] = dga.astype(BF16)
        dg_ref[:, D:] = dgb.astype(BF16)
        dya_ref[...] = dya
        dyb_ref[...] = dyb
        dpa_ref[...] = _dot(dya, wpa_ref[...], NT).astype(BF16)
        dpb_ref[...] = _dot(dyb, wpb_ref[...], NT).astype(BF16)
        part = jnp.concatenate([jnp.sum(dga, axis=0, keepdims=True), jnp.sum(dgb, axis=0, keepdims=True)], axis=0)

        @pl.when(pl.program_id(0) == 0)
        def _():
            db_ref[...] = part

        @pl.when(pl.program_id(0) > 0)
        def _():
            db_ref[...] += part

    row = lambda w: pl.BlockSpec((tm, w), lambda i: (i, 0))
    full = lambda shape: pl.BlockSpec(shape, lambda i: tuple(0 for _ in shape))
    o = lambda w: jax.ShapeDtypeStruct((T, w), BF16)
    return pl.pallas_call(
        body, grid=(T // tm,),
        in_specs=[row(D), row(2 * D), full((2, D)), row(D), row(D), full((GW, D)), full((SI, D))],
        out_specs=[row(2 * D), row(D), row(D), row(GW), row(SI), full((2, D))],
        out_shape=[o(DP_COLS), o(D), o(D), o(GW), o(SI), jax.ShapeDtypeStruct((2, D), F32)], name="merge_bwd",
        compiler_params=_params(("arbitrary",), 48))(dm, proj, bias, ya, yb, wpa, wpb)


RB = 128


def _shift_matrix(j):
    r = lax.broadcasted_iota(jnp.int32, (RB, RB), 0)
    c = lax.broadcasted_iota(jnp.int32, (RB, RB), 1)
    return jnp.where(c == r - j, 1.0, 0.0).astype(BF16)


def _rows_down(xb, before, shifts):
    H = SUBLANE
    mats = [_shift_matrix(j) for j in shifts]
    outs = [[] for _ in shifts]
    for b in range(xb.shape[0] // RB):
        blk = xb[b * RB:(b + 1) * RB]
        edge = jnp.concatenate([before, blk[:2 * H].astype(F32)[:H]], axis=0)
        for i, j in enumerate(shifts):
            outs[i] += [edge[H - j:2 * H - j], _dot(mats[i], blk)[H:]]
        before = blk[RB - 2 * H:].astype(F32)[H:]
    return [jnp.concatenate(o, axis=0) for o in outs]


def _rows_up(xb, after, shifts):
    H = SUBLANE
    nb = xb.shape[0] // RB
    mats = [_shift_matrix(-j) for j in shifts]
    outs = [[] for _ in shifts]
    for b in range(nb):
        blk = xb[b * RB:(b + 1) * RB]
        nxt = xb[(b + 1) * RB:(b + 1) * RB + 2 * H].astype(F32)[:H] if b + 1 < nb else after
        edge = jnp.concatenate([blk[RB - 2 * H:].astype(F32)[H:], nxt], axis=0)
        for i, j in enumerate(shifts):
            outs[i] += [_dot(mats[i], blk)[:RB - H], edge[j:H + j]]
    return [jnp.concatenate(o, axis=0) for o in outs]


def _ffn_act_fwd(up, cw, cb):
    T = up.shape[0]
    tm = min(512, T)
    H = SUBLANE

    def body(up_ref, cw_ref, cb_ref, o_ref, xc_ref, halo):
        @pl.when(pl.program_id(1) == 0)
        def _():
            halo[...] = jnp.zeros_like(halo)

        xb = up_ref[...]
        x2, x1 = _rows_down(xb, halo[...], (2, 1))
        xc = cb_ref[...] + cw_ref[0:1, :] * x2 + cw_ref[1:2, :] * x1 + cw_ref[2:3, :] * xb.astype(F32)
        xc_ref[...] = xc.astype(BF16)
        gate = xc[:, :FT]
        o_ref[...] = (gate * _sigmoid(gate) * xc[:, FT:]).astype(BF16)
        halo[...] = xb[tm - 2 * H:].astype(F32)[H:]

    tile = pl.BlockSpec((tm, 2 * FT), lambda j, i: (i, j))
    return pl.pallas_call(
        body, grid=(2, T // tm),
        in_specs=[tile, pl.BlockSpec((FK, 2 * FT), lambda j, i: (0, j)), pl.BlockSpec((1, 2 * FT), lambda j, i: (0, j))],
        out_specs=[pl.BlockSpec((tm, FT), lambda j, i: (i, j)), tile],
        out_shape=[jax.ShapeDtypeStruct((T, DFF), BF16), jax.ShapeDtypeStruct((T, 2 * DFF), BF16)],
        scratch_shapes=[pltpu.VMEM((H, 2 * FT), F32)], name="ffn_act_fwd",
        compiler_params=_params(("arbitrary", "arbitrary"), 48))(up, cw, cb)


def _ffn_act_bwd(up, xc, dact, cw):
    T = up.shape[0]
    tm = min(512, T)
    nt = T // tm
    H = SUBLANE

    def body(up_ref, xc_ref, da_ref, cw_ref, dup_ref, dcw_ref, dcb_ref, ahead):
        @pl.when(pl.program_id(1) == 0)
        def _():
            ahead[...] = jnp.zeros_like(ahead)
            dcw_ref[...] = jnp.zeros_like(dcw_ref)
            dcb_ref[...] = jnp.zeros_like(dcb_ref)

        xcv = xc_ref[...].astype(F32)
        gate, val = xcv[:, :FT], xcv[:, FT:]
        sg = _sigmoid(gate)
        dav = da_ref[...].astype(F32)
        dgate = dav * val * sg * (1.0 + gate * (1.0 - sg))
        dval = dav * gate * sg
        dxc = jnp.concatenate([dgate, dval], axis=1)
        d1, d2 = _rows_up(dxc.astype(BF16), ahead[...], (1, 2))
        x = up_ref[...].astype(F32)
        dcb_ref[...] += jnp.sum(dxc, axis=0, keepdims=True)
        dcw_ref[...] += jnp.concatenate([jnp.sum(d * x, axis=0, keepdims=True) for d in (d2, d1, dxc)], axis=0)
        dup_ref[...] = (cw_ref[2:3, :] * dxc + cw_ref[1:2, :] * d1 + cw_ref[0:1, :] * d2).astype(BF16)
        ahead[...] = dxc[0:H, :]

    tile = pl.BlockSpec((tm, 2 * FT), lambda j, i: (nt - 1 - i, j))
    return pl.pallas_call(
        body, grid=(2, nt),
        in_specs=[tile, tile, pl.BlockSpec((tm, FT), lambda j, i: (nt - 1 - i, j)),
                  pl.BlockSpec((FK, 2 * FT), lambda j, i: (0, j))],
        out_specs=[tile, pl.BlockSpec((FK, 2 * FT), lambda j, i: (0, j)), pl.BlockSpec((1, 2 * FT), lambda j, i: (0, j))],
        out_shape=[jax.ShapeDtypeStruct((T, 2 * DFF), BF16), jax.ShapeDtypeStruct((FK, 2 * DFF), F32),
                   jax.ShapeDtypeStruct((1, 2 * DFF), F32)],
        scratch_shapes=[pltpu.VMEM((H, 2 * FT), F32)], name="ffn_act_bwd",
        compiler_params=_params(("arbitrary", "arbitrary"), 56))(up, xc, dact, cw)


def _softplus(x):
    e = jnp.exp(-jnp.abs(x))
    return jnp.maximum(x, 0.0) + jnp.where(e < 1e-4, e * (1.0 - 0.5 * e), jnp.log(1.0 + e))


def _ssd_tril():
    li = lax.broadcasted_iota(jnp.int32, (LS, LS), 0)
    si = lax.broadcasted_iota(jnp.int32, (LS, LS), 1)
    return si <= li


def _head_expansion():
    hh = lax.broadcasted_iota(jnp.int32, (LANE, SI), 0)
    cc = lax.broadcasted_iota(jnp.int32, (LANE, SI), 1) // SP
    return jnp.where(hh == cc, 1.0, 0.0).astype(BF16)


def _ssd_pre(xc, dt_ref, dtb_ref, alog_ref, tril, expand):
    sx = _sigmoid(xc)
    xbc = xc * sx
    xs, bm, cm = xbc[:, :SI], xbc[:, SI:SI + SG * SN], xbc[:, SI + SG * SN:]
    dtin = dt_ref[...] + dtb_ref[...]
    dt = _softplus(dtin)
    a_neg = -jnp.exp(alog_ref[...])
    dta = dt * a_neg
    trilb = jnp.where(tril, 1.0, 0.0).astype(BF16)
    a = _dot3_rhs(trilb, dta, NN)
    a_exp = _dot3(a, expand, NN)
    dt_exp = _dot2(dt, expand, NN)
    xdt = xs * dt_exp
    a_last = a_exp[LS - 1:LS, :]
    return dict(xc=xc, sx=sx, xs=xs, bm=bm, cm=cm, dtin=dt_ref[...] + dtb_ref[...], dt=dt, a_neg=a_neg,
                a=a, a_t=a.T, a_exp=a_exp, dt_exp=dt_exp, xdt=xdt, ea=jnp.exp(a_exp),
                w=jnp.exp(a_last - a_exp), eal=jnp.exp(a_last))


def _head_decay(pre, tril, h):
    seg = pre["a"][:, h:h + 1] - pre["a_t"][h:h + 1, :]
    return jnp.exp(jnp.where(tril, seg, -1e30))


def _ssd_fwd(proj, dtraw, cw, cb, dtb, alog, dexp, nw):
    T = proj.shape[0]
    nc = T // LS
    H = SUBLANE

    def body(z_ref, x_ref, dt_ref, cw_ref, cb_ref, dtb_ref, alog_ref, dexp_ref, nw_ref, ex_ref,
             yb_ref, y_ref, sp_ref, xc_ref, halo, st):
        @pl.when(pl.program_id(0) == 0)
        def _():
            halo[...] = jnp.zeros_like(halo)
            st[...] = jnp.zeros_like(st)

        xb = x_ref[...]
        taps = _rows_down(xb, halo[...], (3, 2, 1)) + [xb.astype(F32)]
        xc = cb_ref[...]
        for k in range(SK):
            xc = xc + cw_ref[k:k + 1, :] * taps[k]
        xc_ref[...] = xc.astype(BF16)
        tril, expand = _ssd_tril(), ex_ref[...]
        pre = _ssd_pre(xc, dt_ref, dtb_ref, alog_ref, tril, expand)
        lane = lax.broadcasted_iota(jnp.int32, (LS, LANE), 1)
        lo = lane < SP
        zf = z_ref[...].astype(F32)
        siluz = zf * _sigmoid(zf)
        for g in range(SG):
            gs = slice(g * SGW, (g + 1) * SGW)
            bg = pre["bm"][:, g * SN:(g + 1) * SN].astype(BF16)
            cg = pre["cm"][:, g * SN:(g + 1) * SN].astype(BF16)
            gmat = _dot(cg, bg, NT)
            sg = st[g]
            sp_ref[0, g] = sg
            yoff = _dot(cg, sg.astype(BF16))
            parts = []
            for j in range(SGW // LANE):
                h0 = g * (SGW // SP) + 2 * j
                m0 = gmat * _head_decay(pre, tril, h0)
                m1 = gmat * _head_decay(pre, tril, h0 + 1)
                xp = pre["xdt"][:, g * SGW + j * LANE:g * SGW + (j + 1) * LANE]
                rhs = jnp.concatenate([jnp.where(lo, xp, 0.0), jnp.where(lo, 0.0, xp)], axis=0).astype(BF16)
                parts.append(_dot(jnp.concatenate([m0, m1], axis=1).astype(BF16), rhs))
            y = (jnp.concatenate(parts, axis=1) + pre["ea"][:, gs] * yoff + dexp_ref[:, gs] * pre["xs"][:, gs])
            st[g] = pre["eal"][:, gs] * sg + _dot(bg, (pre["w"][:, gs] * pre["xdt"][:, gs]).astype(BF16), TN)
            y_ref[:, gs] = y
            yg = y * siluz[:, gs]
            r = lax.rsqrt(jnp.mean(yg * yg, axis=-1, keepdims=True) + EPS)
            yb_ref[:, gs] = (yg * r * nw_ref[:, gs]).astype(BF16)
        halo[...] = xb[LS - 2 * H:].astype(F32)[H:]

    vec = lambda w: pl.BlockSpec((1, w), lambda c: (0, 0))
    return pl.pallas_call(
        body, grid=(nc,),
        in_specs=[pl.BlockSpec((LS, SI), lambda c: (c, 2)), pl.BlockSpec((LS, SXBC), lambda c: (c, 2)),
                  pl.BlockSpec((LS, LANE), lambda c: (c, 0)),
                  pl.BlockSpec((SK, SXBC), lambda c: (0, 0)), vec(SXBC), vec(LANE), vec(LANE), vec(SI), vec(SI),
                  pl.BlockSpec((LANE, SI), lambda c: (0, 0))],
        out_specs=[pl.BlockSpec((LS, SI), lambda c: (c, 0)), pl.BlockSpec((LS, SI), lambda c: (c, 0)),
                   pl.BlockSpec((1, SG, SN, SGW), lambda c: (c, 0, 0, 0)), pl.BlockSpec((LS, SXBC), lambda c: (c, 0))],
        out_shape=[jax.ShapeDtypeStruct((T, SI), BF16), jax.ShapeDtypeStruct((T, SI), F32),
                   jax.ShapeDtypeStruct((nc, SG, SN, SGW), F32), jax.ShapeDtypeStruct((T, SXBC), BF16)],
        scratch_shapes=[pltpu.VMEM((H, SXBC), F32), pltpu.VMEM((SG, SN, SGW), F32)], name="ssd_fwd",
        compiler_params=_params(("arbitrary",), VMEM_CAP_MB))(
            proj, proj, dtraw, cw, cb, dtb, alog, dexp, nw, _head_expansion())


def _ssd_bwd(proj, xcs, dtraw, y, sprev, dyb, dproj, cw, dtb, alog, dexp, nw):
    T = proj.shape[0]
    nc = T // LS
    H = SUBLANE
    NJ = 1

    def body(z_ref, x_ref, xc_ref, dt_ref, y_ref, sp_ref, dyb_ref, dproj_in,
             cw_ref, dtb_ref, alog_ref, dexp_ref, nw_ref, ex_ref,
             dp_ref, ddt_ref, dcw_ref, dcb_ref, ddtb_ref, da_ref, dd_ref, dnw_ref,
             ahead, ds, stage):
        del dproj_in
        i = pl.program_id(0)
        j = pl.program_id(1)

        @pl.when(jnp.logical_and(i == 0, j == 0))
        def _():
            ahead[...] = jnp.zeros_like(ahead)
            ds[...] = jnp.zeros_like(ds)
            for r in (dcw_ref, dcb_ref, ddtb_ref, da_ref, dd_ref, dnw_ref):
                r[...] = jnp.zeros_like(r)

        @pl.when(j == 0)
        def _():
            tril, expand = _ssd_tril(), ex_ref[...]
            pre = _ssd_pre(xc_ref[...].astype(F32), dt_ref, dtb_ref, alog_ref, tril, expand)
            lane = lax.broadcasted_iota(jnp.int32, (LS, LANE), 1)
            sub = lax.broadcasted_iota(jnp.int32, (LANE, LS), 0)
            rowi = lax.broadcasted_iota(jnp.int32, (LS, 1), 0)
            lo = lane < SP
            xs, xdt, ea, w, eal = pre["xs"], pre["xdt"], pre["ea"], pre["w"], pre["eal"]

            zf = z_ref[...].astype(F32)
            sz = _sigmoid(zf)
            siluz = zf * sz
            yv = y_ref[...]
            yg = yv * siluz
            dout = dyb_ref[...].astype(F32)
            dyg_parts, dnw_parts = [], []
            for g in range(SG):
                gs = slice(g * SGW, (g + 1) * SGW)
                ygg = yg[:, gs]
                r = lax.rsqrt(jnp.mean(ygg * ygg, axis=-1, keepdims=True) + EPS)
                yhat = ygg * r
                dn = dout[:, gs] * nw_ref[:, gs]
                dnw_parts.append(jnp.sum(dout[:, gs] * yhat, axis=0, keepdims=True))
                dyg_parts.append(r * (dn - yhat * jnp.mean(dn * yhat, axis=-1, keepdims=True)))
            dyg = jnp.concatenate(dyg_parts, axis=1)
            dnw_ref[...] += jnp.concatenate(dnw_parts, axis=1)
            dy = dyg * siluz
            stage[:, 0:SI] = (dyg * yv * sz * (1.0 + zf * (1.0 - sz))).astype(BF16)
            dd_ref[...] += jnp.sum(dy * xs, axis=0, keepdims=True)
            tt = ea * dy

            da_rows = jnp.zeros((LS, LANE), F32)
            da_cols = jnp.zeros((LANE, LS), F32)
            dxdt_parts, db_parts, dc_parts, daexp_parts = [], [], [], []
            for g in range(SG):
                gs = slice(g * SGW, (g + 1) * SGW)
                bg = pre["bm"][:, g * SN:(g + 1) * SN].astype(BF16)
                cg = pre["cm"][:, g * SN:(g + 1) * SN].astype(BF16)
                sg = sp_ref[0, g]
                sgb = sg.astype(BF16)
                dsg = ds[g]
                dsgb = dsg.astype(BF16)
                ttg = tt[:, gs].astype(BF16)
                yoff = _dot(cg, sgb)
                dc = _dot(ttg, sgb, NT)
                gmat = _dot(cg, bg, NT)
                dgm = jnp.zeros((LS, LS), F32)
                dxdt_pairs = []
                for jj in range(SGW // LANE):
                    h0 = g * (SGW // SP) + 2 * jj
                    ps = slice(g * SGW + jj * LANE, g * SGW + (jj + 1) * LANE)
                    l0 = _head_decay(pre, tril, h0)
                    l1 = _head_decay(pre, tril, h0 + 1)
                    m0 = gmat * l0
                    m1 = gmat * l1
                    dyp = dy[:, ps]
                    dy_lo = jnp.where(lo, dyp, 0.0).astype(BF16)
                    dy_hi = jnp.where(lo, 0.0, dyp).astype(BF16)
                    xpb = xdt[:, ps].astype(BF16)
                    dm0 = _dot(dy_lo, xpb, NT)
                    dm1 = _dot(dy_hi, xpb, NT)
                    q0 = dm0 * m0
                    q1 = dm1 * m1
                    da_rows = da_rows + jnp.where(lane == h0, jnp.sum(q0, axis=1, keepdims=True), 0.0)
                    da_rows = da_rows + jnp.where(lane == h0 + 1, jnp.sum(q1, axis=1, keepdims=True), 0.0)
                    da_cols = da_cols + jnp.where(sub == h0, jnp.sum(q0, axis=0, keepdims=True), 0.0)
                    da_cols = da_cols + jnp.where(sub == h0 + 1, jnp.sum(q1, axis=0, keepdims=True), 0.0)
                    dgm = dgm + dm0 * l0 + dm1 * l1
                    mcat = jnp.concatenate([m0, m1], axis=0).astype(BF16)
                    dycat = jnp.concatenate([dy_lo, dy_hi], axis=0)
                    dxdt_pairs.append(_dot(mcat, dycat, TN))
                dgb = dgm.astype(BF16)
                dc = dc + _dot(dgb, bg)
                db = _dot(dgb, cg, TN)
                zg = _dot(bg, dsgb)
                wg, xdtg = w[:, gs], xdt[:, gs]
                dxdt_g = jnp.concatenate(dxdt_pairs, axis=1) + wg * zg
                qg = zg * xdtg * wg
                last = (jnp.sum(qg, axis=0, keepdims=True)
                        + jnp.sum(dsg * sg, axis=0, keepdims=True) * eal[:, gs])
                daexp_parts.append(dy[:, gs] * ea[:, gs] * yoff - qg + jnp.where(rowi == LS - 1, last, 0.0))
                db = db + _dot((wg * xdtg).astype(BF16), dsgb, NT)
                ds[g] = eal[:, gs] * dsg + _dot(cg, ttg, TN)
                dxdt_parts.append(dxdt_g)
                db_parts.append(db)
                dc_parts.append(dc)
            dxdt = jnp.concatenate(dxdt_parts, axis=1)
            da_exp = jnp.concatenate(daexp_parts, axis=1)
            da = _dot2(da_exp, expand, NT) + da_rows - da_cols.T
            triub = jnp.where(tril, 1.0, 0.0).astype(BF16)
            ddta = _dot3_rhs(triub, da, TN)
            ddt = ddta * pre["a_neg"] + _dot2(dxdt * xs, expand, NT)
            da_ref[...] += jnp.sum(ddta * pre["dt"], axis=0, keepdims=True)
            ddt_raw = ddt * _sigmoid(pre["dtin"])
            ddt_ref[...] = ddt_raw
            ddtb_ref[...] += jnp.sum(ddt_raw, axis=0, keepdims=True)
            dxs = dexp_ref[...] * dy + dxdt * pre["dt_exp"]
            dxbc = jnp.concatenate([dxs] + db_parts + dc_parts, axis=1)
            sx, xc = pre["sx"], pre["xc"]
            dxc = dxbc * sx * (1.0 + xc * (1.0 - sx))
            taps = _rows_up(dxc.astype(BF16), ahead[...], (3, 2, 1)) + [dxc]
            xr = x_ref[...].astype(F32)
            dcb_ref[...] += jnp.sum(dxc, axis=0, keepdims=True)
            dcw_ref[...] += jnp.concatenate([jnp.sum(t * xr, axis=0, keepdims=True) for t in taps], axis=0)
            dxr = cw_ref[0:1, :] * taps[0]
            for k in range(1, SK):
                dxr = dxr + cw_ref[k:k + 1, :] * taps[k]
            stage[:, SI:] = dxr.astype(BF16)
            ahead[...] = dxc[0:H, :]

        dp_ref[...] = stage[...]

    vec = lambda w: pl.BlockSpec((1, w), lambda i, j: (0, 0))
    rev = lambda w, cb_: pl.BlockSpec((LS, w), lambda i, j: (nc - 1 - i, cb_))
    outs = pl.pallas_call(
        body, grid=(nc, NJ),
        in_specs=[rev(SI, 2), rev(SXBC, 2), rev(SXBC, 0),
                  rev(LANE, 0), rev(SI, 0),
                  pl.BlockSpec((1, SG, SN, SGW), lambda i, j: (nc - 1 - i, 0, 0, 0)),
                  rev(SI, 0), pl.BlockSpec(memory_space=pl.ANY),
                  pl.BlockSpec((SK, SXBC), lambda i, j: (0, 0)), vec(LANE), vec(LANE), vec(SI), vec(SI),
                  pl.BlockSpec((LANE, SI), lambda i, j: (0, 0))],
        out_specs=[rev(DP_SSM, 1), rev(LANE, 0),
                   pl.BlockSpec((SK, SXBC), lambda i, j: (0, 0)), vec(SXBC), vec(LANE), vec(LANE), vec(SI), vec(SI)],
        out_shape=[jax.ShapeDtypeStruct(dproj.shape, dproj.dtype), jax.ShapeDtypeStruct((T, LANE), F32),
                   jax.ShapeDtypeStruct((SK, SXBC), F32), jax.ShapeDtypeStruct((1, SXBC), F32),
                   jax.ShapeDtypeStruct((1, LANE), F32), jax.ShapeDtypeStruct((1, LANE), F32),
                   jax.ShapeDtypeStruct((1, SI), F32), jax.ShapeDtypeStruct((1, SI), F32)],
        scratch_shapes=[pltpu.VMEM((H, SXBC), F32),
                        pltpu.VMEM((SG, SN, SGW), F32), pltpu.VMEM((LS, SI + SXBC), BF16)],
        input_output_aliases={7: 0}, name="ssd_bwd",
        compiler_params=_params(("arbitrary", "arbitrary"), VMEM_CAP_MB))(
            proj, proj, xcs, dtraw, y, sprev, dyb, dproj, cw, dtb, alog, dexp, nw, _head_expansion())
    return outs


def _perm_ffn_cols(a):
    lead = a.shape[:-1]
    return a.reshape(lead + (2, 2, FT)).swapaxes(-3, -2).reshape(lead + (2 * DFF,))


def _perm_ffn_rows(a):
    return a.reshape((2, 2, FT) + a.shape[1:]).swapaxes(0, 1).reshape(a.shape)


def _pad_lanes(v, n=LANE):
    return jnp.pad(v, ((0, 0), (0, n - v.shape[-1])))


LATE = ["w_proj_a", "w_proj_b", "w_out", "ffn_w_up_t", "ffn_w_down"]
WGRAD = BF16


class _NoExchange:
    def gather_start(self):
        return None

    def gather_pass_on(self, outs):
        return None

    def late_weights(self, w, outs):
        return w

    def reduce_late(self, grads):
        return None

    def w_in_to_sibling(self, grad_main, grad_dt):
        return None

    def reduce_w_in(self, outs):
        return None

    def reduced(self, late_outs, w_in_outs):
        pass


def _local_step(x, tgt, w, hooks=None):
    hooks = hooks or _NoExchange()

    def mm(*args, side=None, **kw):
        out = _matmul(*args, side=side, **kw)
        return out if side is not None else (out, [])

    win_t = w["w_in_t"]
    win_dt = jnp.pad(w["w_in_t"][PMAIN:], ((0, LANE - SH), (0, 0)))
    fcw = _perm_ffn_cols(w["ffn_conv_w"])
    fcb = _perm_ffn_cols(w["ffn_conv_b"][None, :])
    mixw = w["mix_norm_w"][None, :]
    ffnw = w["ffn_norm_w"][None, :]
    finw = w["final_norm_w"][None, :]
    bst = w["gmlp_bs"].T
    scb = w["ssm_conv_b"][None, :]
    dtb = _pad_lanes(w["ssm_dt_bias"][None, :])
    alog = _pad_lanes(w["ssm_a_log"][None, :])
    dexp = jnp.repeat(w["ssm_d"], SP)[None, :]
    snw = w["ssm_norm_w"][None, :]

    xn = _rms_fwd(x, mixw, name="mix_norm")
    proj, got = mm(xn, win_t, name="in_proj", out_dtype=BF16, tb=True, tn=3072, j_outer=True, b_rows=PMAIN,
                   side=hooks.gather_start())
    dtraw, got = mm(xn, win_dt, name="in_proj_dt", out_dtype=F32, tb=True, side=hooks.gather_pass_on(got))
    w = hooks.late_weights(w, got)
    wup = _perm_ffn_rows(w["ffn_w_up_t"])
    ya_pre = _gmlp_fwd(proj, w["gmlp_ln_w"], w["gmlp_ln_b"], w["gmlp_ws"], bst)
    yb_pre, y_ssd, sprev, ssm_xc = _ssd_fwd(proj, dtraw, w["ssm_conv_w"], scb, dtb, alog, dexp, snw)
    merged, ya, yb = _merge_fwd(ya_pre, yb_pre, proj, w["gate_bias"], w["w_proj_a"], w["w_proj_b"])
    h1 = _matmul(merged, w["w_out"], name="out_proj", out_dtype=F32, add=x)
    hn = _rms_fwd(h1, ffnw, name="ffn_norm")
    up = _matmul(hn, wup, name="ffn_up", out_dtype=BF16, tb=True, tn=2 * FT, j_outer=True)
    act, ffn_xc = _ffn_act_fwd(up, fcw, fcb)
    h2 = _matmul(act, w["ffn_w_down"], name="ffn_down", out_dtype=F32, tk=DFF, add=h1)

    loss_row, dh2, d_finw = _loss_head(h2, tgt, finw)
    dact = _matmul(dh2, w["ffn_w_down"], name="ffn_down_dx", out_dtype=BF16, tb=True, tn=DFF)
    d_wdown = _matmul(act, dh2, name="ffn_down_dw", out_dtype=WGRAD, ta=True, tm=FT, tk=2048)
    dup, d_fcw, d_fcb = _ffn_act_bwd(up, ffn_xc, dact, fcw)
    dhn = _matmul(dup, wup, name="ffn_up_dx", out_dtype=F32, tk=2 * FT)
    d_wup = _matmul(dup, hn, name="ffn_up_dw", out_dtype=WGRAD, ta=True, tm=FT, tk=2048)
    dh1, d_ffnw = _rms_bwd(h1, ffnw, dhn, dh2, name="ffn_norm_bwd")
    dmerged = _matmul(dh1, w["w_out"], name="out_proj_dx", out_dtype=BF16, tb=True)
    d_wout = _matmul(merged, dh1, name="out_proj_dw", out_dtype=WGRAD, ta=True, tk=2048)
    dproj, dya, dyb, dya_pre, dyb_pre, d_gbias = _merge_bwd(dmerged, proj, w["gate_bias"], ya, yb,
                                                           w["w_proj_a"], w["w_proj_b"])
    d_wpa = _matmul(ya_pre, dya, name="proj_a_dw", out_dtype=WGRAD, ta=True, tk=2048)
    d_wpb = _matmul(yb_pre, dyb, name="proj_b_dw", out_dtype=WGRAD, ta=True, tk=2048)
    dproj, d_lnw, d_lnb, d_ws, d_bst = _gmlp_bwd(proj, dya_pre, dproj, w["gmlp_ln_w"], w["gmlp_ln_b"],
                                                 w["gmlp_ws"], bst)
    dproj, ddt, d_scw, d_scb, d_dtb, d_a, d_dch, d_snw = _ssd_bwd(
        proj, ssm_xc, dtraw, y_ssd, sprev, dyb_pre, dproj, w["ssm_conv_w"], dtb, alog, dexp, snw)
    late = {"w_proj_a": d_wpa, "w_proj_b": d_wpb, "w_out": d_wout, "ffn_w_up_t": _perm_ffn_rows(d_wup),
            "ffn_w_down": d_wdown}
    gap = (2 * D + 2 * GW, DP_GAP)
    d_win_main, late_outs = mm(dproj, xn, name="in_proj_dw", out_dtype=WGRAD, ta=True, a_gap=gap, tk=2048,
                               side=hooks.reduce_late(late))
    d_win_dt = _matmul(ddt, xn, name="in_proj_dt_dw", out_dtype=F32, ta=True)
    d_win_t = jnp.concatenate([d_win_main, d_win_dt[:SH]], axis=0)
    dxn, got = mm(ddt, win_dt, name="in_proj_dt_dx", out_dtype=F32,
                  side=hooks.w_in_to_sibling(d_win_main, d_win_dt[:SH]))
    dxn, w_in_outs = mm(dproj, win_t, name="in_proj_dx", out_dtype=F32, add=dxn, b_rows=PMAIN, a_gap=gap,
                        side=hooks.reduce_w_in(got))
    hooks.reduced(late_outs, w_in_outs)
    grad_x, d_mixw = _rms_bwd(x, mixw, dxn, dh1, name="mix_norm_bwd")

    a_neg = -jnp.exp(w["ssm_a_log"])
    grads = {
        "mix_norm_w": d_mixw[0],
        "w_in_t": d_win_t,
        "gate_bias": d_gbias,
        "gmlp_ln_w": d_lnw, "gmlp_ln_b": d_lnb, "gmlp_ws": d_ws, "gmlp_bs": d_bst[:, :GG].T,
        "ssm_conv_w": d_scw, "ssm_conv_b": d_scb[0],
        "ssm_dt_bias": d_dtb[0, :SH], "ssm_a_log": d_a[0, :SH] * a_neg,
        "ssm_d": d_dch.reshape(SH, SP).sum(axis=-1), "ssm_norm_w": d_snw[0],
        **late,
        "ffn_norm_w": d_ffnw[0],
        "ffn_conv_w": _perm_ffn_cols(d_fcw), "ffn_conv_b": _perm_ffn_cols(d_fcb)[0],
        "ffn_w_down": d_wdown, "final_norm_w": d_finw[0],
    }
    return loss_row, grad_x, grads


MESH = pl.DeviceIdType.MESH
HBM_SPEC = pl.BlockSpec(memory_space=pltpu.HBM)


def _axes():
    return lax.axis_index("x"), lax.axis_index("y"), lax.axis_index("c")


def _all_gather(shards, *, name):
    na = len(shards)

    def body(*refs):
        x_refs, out_refs = refs[:na], refs[na:2 * na]
        send_sems, recv_sems, local_sems = refs[2 * na:]
        x, y, c = _axes()
        me, sibling = (x, y, c), (x, y, 1 - c)
        chips = [(1 - x, y), (x, 1 - y), (1 - x, 1 - y)]

        def slot(a, px, py, pc):
            return out_refs[a].at[4 * px + 2 * py + pc]

        def copy(a, k, block, to, src=None):
            return pltpu.make_async_remote_copy(
                src_ref=slot(a, *block) if src is None else src, dst_ref=slot(a, *block),
                send_sem=send_sems.at[7 * a + k], recv_sem=recv_sems.at[7 * a + k], device_id=to, device_id_type=MESH)

        mine = [pltpu.make_async_copy(x_refs[a], slot(a, *me), local_sems.at[a]) for a in range(na)]
        for cp in mine:
            cp.start()
        first = []
        for a in range(na):
            first.append(copy(a, 0, me, sibling, src=x_refs[a]))
            first += [copy(a, 1 + j, me, (*chip, c), src=x_refs[a]) for j, chip in enumerate(chips)]
        for cp in first:
            cp.start()
        passed = []
        for j, chip in enumerate(chips):
            for a in range(na):
                copy(a, 1 + j, (*chip, c), me).wait_recv()
                cp = copy(a, 4 + j, (*chip, c), sibling)
                cp.start()
                passed.append(cp)
        for a in range(na):
            copy(a, 0, sibling, me).wait_recv()
        for j, chip in enumerate(chips):
            for a in range(na):
                copy(a, 4 + j, (*chip, 1 - c), me).wait_recv()
        for cp in first + passed:
            cp.wait_send()
        for cp in mine:
            cp.wait()

    return pl.pallas_call(
        body, out_shape=[jax.ShapeDtypeStruct((NDEV,) + s.shape, s.dtype) for s in shards],
        in_specs=[HBM_SPEC] * na, out_specs=[HBM_SPEC] * na,
        scratch_shapes=[pltpu.SemaphoreType.DMA((7 * na,)), pltpu.SemaphoreType.DMA((7 * na,)),
                        pltpu.SemaphoreType.DMA((na,))],
        name=name)(*shards)


def _exchange(srcs, plan, *, name):
    na = len(srcs)
    n = len(plan(0, 0, 0))

    def body(*refs):
        src_refs, out_refs = refs[:na], refs[na:2 * na]
        send_sems, recv_sems = refs[2 * na:]
        x, y, c = _axes()
        copies = []
        for k, (slab, peer) in enumerate(plan(x, y, c)):
            for a in range(na):
                cp = pltpu.make_async_remote_copy(
                    src_ref=src_refs[a].at[slab], dst_ref=out_refs[a].at[k], send_sem=send_sems.at[n * a + k],
                    recv_sem=recv_sems.at[n * a + k], device_id=peer, device_id_type=MESH)
                cp.start()
                copies.append(cp)
        for cp in copies:
            cp.wait()

    return pl.pallas_call(
        body, out_shape=[jax.ShapeDtypeStruct((n,) + s.shape[1:], s.dtype) for s in srcs],
        in_specs=[HBM_SPEC] * na, out_specs=[HBM_SPEC] * na,
        scratch_shapes=[pltpu.SemaphoreType.DMA((n * na,)), pltpu.SemaphoreType.DMA((n * na,))], name=name)(*srcs)


def _to_sibling_plan(x, y, c):
    return [(2 * q + (1 - c), (x, y, 1 - c)) for q in range(4)]


def _to_chips_plan(x, y, c):
    q = 2 * x + y
    return [(q ^ 2, (1 - x, y, c)), (q ^ 1, (x, 1 - y, c)), (q ^ 3, (1 - x, 1 - y, c))]


def _row_tile(rows, row_bytes, budget=2 * 2 ** 20, align=2 * SUBLANE):
    if rows * row_bytes <= 2 * budget:
        return rows
    best = None
    for d in range(align, rows + 1, align):
        if rows % d == 0 and d * row_bytes <= budget:
            best = d
    return best or rows


def _pair_add(g, ra, c_idx, *, name):
    _, _, R, C = g.shape
    tr = _row_tile(R, C * 4, budget=3 * 2 ** 20)

    def body(c_ref, g_ref, ra_ref, o_ref):
        del c_ref
        o_ref[...] = (g_ref[0].astype(F32) + ra_ref[...].astype(F32)).astype(o_ref.dtype)

    return pl.pallas_call(
        body,
        grid_spec=pltpu.PrefetchScalarGridSpec(
            num_scalar_prefetch=1, grid=(4, R // tr),
            in_specs=[pl.BlockSpec((1, 1, tr, C), lambda q, r, cr: (q, cr[0], r, 0)),
                      pl.BlockSpec((1, tr, C), lambda q, r, cr: (q, r, 0))],
            out_specs=pl.BlockSpec((1, tr, C), lambda q, r, cr: (q, r, 0))),
        out_shape=jax.ShapeDtypeStruct((4, R, C), g.dtype), name=name,
        compiler_params=_params(("arbitrary", "arbitrary"), 24))(c_idx, g, ra)


def _grad_sum(p, rb, q_idx, *, name):
    _, R, C = p.shape
    tr = _row_tile(R, C * 4, budget=3 * 2 ** 20)

    def body(q_ref, p_ref, rb_ref, o_ref):
        del q_ref
        g = p_ref[0].astype(F32)
        for k in range(3):
            g = g + rb_ref[k].astype(F32)
        o_ref[...] = g

    return pl.pallas_call(
        body,
        grid_spec=pltpu.PrefetchScalarGridSpec(
            num_scalar_prefetch=1, grid=(R // tr,),
            in_specs=[pl.BlockSpec((1, tr, C), lambda r, qr: (qr[0], r, 0)),
                      pl.BlockSpec((3, tr, C), lambda r, qr: (0, r, 0))],
            out_specs=pl.BlockSpec((tr, C), lambda r, qr: (r, 0))),
        out_shape=jax.ShapeDtypeStruct((R, C), F32), name=name,
        compiler_params=_params(("arbitrary",), 40))(q_idx, p, rb)


def _adamw(g, w, m, v):
    m = ADAM_B1 * m + (1.0 - ADAM_B1) * g
    v = ADAM_B2 * v + (1.0 - ADAM_B2) * (g * g)
    m_hat = m / (1.0 - ADAM_B1 ** ADAM_STEP)
    v_hat = v / (1.0 - ADAM_B2 ** ADAM_STEP)
    delta = -ADAM_LR * (m_hat / (jnp.sqrt(v_hat) + ADAM_EPS) + ADAM_WD * w)
    return delta, m, v


def _adam(g, w, m, v, *, name):
    _, R, C = w.shape
    tr = _row_tile(R, C * 4, budget=2 ** 20, align=SUBLANE)

    def body(g_ref, w_ref, m_ref, v_ref, d_out, m_out, v_out):
        delta, mn, vn = _adamw(g_ref[...], w_ref[...], m_ref[...], v_ref[...])
        d_out[...] = delta
        m_out[...] = mn
        v_out[...] = vn

    row = pl.BlockSpec((1, tr, C), lambda r: (0, r, 0))
    o = jax.ShapeDtypeStruct((1, R, C), F32)
    return pl.pallas_call(
        body, grid=(R // tr,), in_specs=[row, row, row, row], out_specs=[row, row, row], out_shape=[o, o, o],
        name=name, compiler_params=_params(("arbitrary",), 32))(g, w, m, v)


def _vmem_specs(n):
    return [pl.BlockSpec(memory_space=pltpu.VMEM)] * n


def _pair_sum_many(mine, theirs, *, name):
    n = len(mine)

    def body(*refs):
        for a in range(n):
            refs[2 * n + a][...] = refs[a][...] + refs[n + a][0]

    return pl.pallas_call(
        body, out_shape=[jax.ShapeDtypeStruct(m.shape, m.dtype) for m in mine], in_specs=_vmem_specs(2 * n),
        out_specs=_vmem_specs(n), name=name)(*mine, *theirs)


def _chip_sum_many(own, recv, q_idx, *, name):
    n = len(own)

    def body(q_ref, *refs):
        q = q_ref[0]
        for a in range(n):
            mine, r = refs[a][...], refs[n + a]
            total = None
            for chip in range(4):
                e = q ^ chip
                term = jnp.where(e == 0, mine, jnp.where(e == 2, r[0], jnp.where(e == 1, r[1], r[2])))
                total = term if total is None else total + term
            refs[2 * n + a][...] = total

    return pl.pallas_call(
        body, out_shape=[jax.ShapeDtypeStruct(m.shape, m.dtype) for m in own],
        in_specs=[pl.BlockSpec(memory_space=pltpu.SMEM)] + _vmem_specs(2 * n), out_specs=_vmem_specs(n),
        name=name)(q_idx, *own, *recv)


def _adam_many(gs, ws, ms, vs, *, name):
    n = len(gs)

    def body(*refs):
        for a in range(n):
            delta, mn, vn = _adamw(*(refs[k * n + a][...] for k in range(4)))
            refs[4 * n + a][...] = delta
            refs[5 * n + a][...] = mn
            refs[6 * n + a][...] = vn

    shapes = [jax.ShapeDtypeStruct(w.shape, w.dtype) for w in ws]
    out = pl.pallas_call(body, out_shape=shapes * 3, in_specs=_vmem_specs(4 * n), out_specs=_vmem_specs(3 * n),
                         name=name)(*gs, *ws, *ms, *vs)
    return out[:n], out[n:2 * n], out[2 * n:]


WEIGHTS = ["mix_norm_w", "w_in", "gate_bias", "gmlp_ln_w", "gmlp_ln_b", "gmlp_ws", "gmlp_bs", "ssm_conv_w",
           "ssm_conv_b", "ssm_dt_bias", "ssm_a_log", "ssm_d", "ssm_norm_w", "w_proj_a", "w_proj_b", "w_out",
           "ffn_norm_w", "ffn_w_up", "ffn_conv_w", "ffn_conv_b", "ffn_w_down", "final_norm_w"]
SHARDED = {"w_in": ((D, IN_COLS), 1), "gate_bias": ((2, D), 1), "ssm_conv_w": ((SK, SXBC), 1),
           "w_proj_a": ((GW, D), 0), "w_proj_b": ((SI, D), 0), "w_out": ((D, D), 0),
           "ffn_w_up": ((D, 2 * DFF), 1), "ffn_conv_w": ((FK, 2 * DFF), 1), "ffn_w_down": ((DFF, D), 0)}
REPLICATED = {"mix_norm_w": (D,), "gmlp_ln_w": (GG, GD), "gmlp_ln_b": (GG, GD), "gmlp_ws": (GG, GB, GB),
              "gmlp_bs": (GG, GB), "ssm_conv_b": (SXBC,), "ssm_dt_bias": (SH,), "ssm_a_log": (SH,), "ssm_d": (SH,),
              "ssm_norm_w": (SI,), "ffn_norm_w": (D,), "ffn_conv_b": (2 * DFF,), "final_norm_w": (D,)}
REPL_ORDER = [n for n in WEIGHTS if n in REPLICATED]
BTILE = 2 * SUBLANE
WIN_R = IN_COLS // NDEV
WIN_P = WIN_R + BTILE - WIN_R % BTILE
WIN_A = [WIN_R * d // BTILE * BTILE for d in range(NDEV)]
assert all(WIN_A[d] + WIN_P >= WIN_R * (d + 1) for d in range(NDEV)) and WIN_A[-1] + WIN_P == IN_COLS
BIG = [("w_proj_a", GW // NDEV, False), ("w_proj_b", SI // NDEV, False), ("w_out", D // NDEV, False),
       ("ffn_w_up", 2 * DFF // NDEV, True), ("ffn_w_down", DFF // NDEV, False), ("w_in", WIN_P, True)]
VECTORS = ["gate_bias", "ssm_conv_w", "ffn_conv_w"]


def _round_up(n, k):
    return (n + k - 1) // k * k


BIG_OFF = {}
_off = 0
for _n, _r, _t in BIG:
    BIG_OFF[_n] = _off
    _off += _r
BIG_USED = _off
BIG_ROWS = _round_up(BIG_USED, 2 * SUBLANE)
assert all(BIG_OFF[n] % (2 * SUBLANE) == 0 for n, _, _ in BIG)
VEC_SHAPE = {n: (SHARDED[n][0][0], SHARDED[n][0][1] // NDEV) for n in VECTORS}


def _win_offset(dev):
    return WIN_R * dev - WIN_R * dev // BTILE * BTILE


def _pack_big(arrs, dtype, dev):
    parts = []
    for n, r, t in BIG:
        a = (arrs[n].T if t else arrs[n]).astype(dtype)
        if n == "w_in":
            a = lax.dynamic_update_slice(jnp.zeros((WIN_P, D), dtype), a, (_win_offset(dev), 0))
        parts.append(a)
    parts.append(jnp.zeros((BIG_ROWS - BIG_USED, D), dtype))
    return jnp.concatenate(parts, axis=0)


def _join_windows(win):
    parts = []
    for d in range(NDEV):
        lo = BTILE if WIN_A[d] % WIN_R else 0
        if lo:
            parts.append(win[d - 1, WIN_P - BTILE:] + win[d, :BTILE])
        hi = WIN_P - BTILE if d + 1 < NDEV and WIN_A[d + 1] < WIN_A[d] + WIN_P else WIN_P
        parts.append(win[d, lo:hi])
    return jnp.concatenate(parts, axis=0)


def _split_windows(main, last):
    assert WIN_A[-2] + WIN_P <= PMAIN
    wins = [main[a:a + WIN_P] for a in WIN_A[:-1]]
    return jnp.stack(wins + [jnp.concatenate([main[WIN_A[-1]:], last], axis=0)])


LATE_ROWS = BIG_OFF["w_in"]
assert LATE_ROWS + WIN_P == BIG_ROWS and BIG[-1][0] == "w_in"


def _remote(src, dst, send_sems, recv_sems, k, to):
    return pltpu.make_async_remote_copy(src_ref=src, dst_ref=dst, send_sem=send_sems.at[k], recv_sem=recv_sems.at[k],
                                        device_id=to, device_id_type=MESH)


class _Exchange:
    def __init__(self, late_shard, c_idx):
        self.late_shard, self.c_idx = late_shard, c_idx

    def gather_start(self):
        shard = self.late_shard

        def make(ins, outs, send_sems, recv_sems):
            (x_ref,), (out,) = ins, outs
            x, y, c = _axes()
            mine = out.at[4 * x + 2 * y + c]
            peers = [(x, y, 1 - c), (1 - x, y, c), (x, 1 - y, c), (1 - x, 1 - y, c)]
            copies = [_remote(x_ref, mine, send_sems, recv_sems, k, p) for k, p in enumerate(peers)]
            return copies + [pltpu.make_async_copy(x_ref, mine, send_sems.at[len(peers)])]

        return _Side([shard], [jax.ShapeDtypeStruct((NDEV,) + shard.shape, shard.dtype)], 5, make)

    def gather_pass_on(self, outs):
        (buf,) = outs

        def make(ins, outs, send_sems, recv_sems):
            (src,), (dst,) = ins, outs
            x, y, c = _axes()
            slots = [4 * px + 2 * py + c for px, py in [(1 - x, y), (x, 1 - y), (1 - x, 1 - y)]]
            return [_remote(src.at[s], dst.at[s], send_sems, recv_sems, k, (x, y, 1 - c)) for k, s in enumerate(slots)]

        return _Side([buf], [jax.ShapeDtypeStruct(buf.shape, buf.dtype)], 3, make, aliases=[(0, 0)])

    def late_weights(self, w, outs):
        (buf,) = outs
        w = dict(w)
        for n, r, t in BIG[:-1]:
            w[n + "_t" if t else n] = buf[:, BIG_OFF[n]:BIG_OFF[n] + r].reshape(NDEV * r, D)
        return w

    @staticmethod
    def _plan_side(src, plan):
        n = len(plan(0, 0, 0))

        def make(ins, outs, send_sems, recv_sems):
            (s,), (dst,) = ins, outs
            return [_remote(s.at[slab], dst.at[k], send_sems, recv_sems, k, peer)
                    for k, (slab, peer) in enumerate(plan(*_axes()))]

        return _Side([src], [jax.ShapeDtypeStruct((n,) + src.shape[1:], src.dtype)], n, make)

    def _to_chips(self, send, sib, tag):
        sums = _pair_add(send.reshape((4, 2) + send.shape[1:]), sib, self.c_idx, name=tag + "_grad_pair_add")
        return sums, self._plan_side(sums, _to_chips_plan)

    def reduce_late(self, grads):
        send = jnp.concatenate([grads[n + "_t" if t else n].reshape(NDEV, r, D) for n, r, t in BIG[:-1]], axis=1)
        send = send.astype(BF16)
        (sib,) = _exchange([send], _to_sibling_plan, name="late_grads_to_sibling")
        self.late_sum, side = self._to_chips(send, sib, "late")
        return side

    def w_in_to_sibling(self, grad_main, grad_dt):
        self.w_in_send = _split_windows(grad_main.astype(BF16), grad_dt.astype(BF16))
        return self._plan_side(self.w_in_send, _to_sibling_plan)

    def reduce_w_in(self, outs):
        self.w_in_sum, side = self._to_chips(self.w_in_send, outs[0], "w_in")
        return side

    def reduced(self, late_outs, w_in_outs):
        (self.late_from_chips,), (self.w_in_from_chips,) = late_outs, w_in_outs


def kernel(x, mix_norm_w, w_in, gate_bias, gmlp_ln_w, gmlp_ln_b, gmlp_ws, gmlp_bs, ssm_conv_w, ssm_conv_b, ssm_dt_bias, ssm_a_log, ssm_d, ssm_norm_w, w_proj_a, w_proj_b, w_out, ffn_norm_w, ffn_w_up, ffn_conv_w, ffn_conv_b, ffn_w_down, final_norm_w, loss_target, m_mix_norm_w, m_w_in, m_gate_bias, m_gmlp_ln_w, m_gmlp_ln_b, m_gmlp_ws, m_gmlp_bs, m_ssm_conv_w, m_ssm_conv_b, m_ssm_dt_bias, m_ssm_a_log, m_ssm_d, m_ssm_norm_w, m_w_proj_a, m_w_proj_b, m_w_out, m_ffn_norm_w, m_ffn_w_up, m_ffn_conv_w, m_ffn_conv_b, m_ffn_w_down, m_final_norm_w, v_mix_norm_w, v_w_in, v_gate_bias, v_gmlp_ln_w, v_gmlp_ln_b, v_gmlp_ws, v_gmlp_bs, v_ssm_conv_w, v_ssm_conv_b, v_ssm_dt_bias, v_ssm_a_log, v_ssm_d, v_ssm_norm_w, v_w_proj_a, v_w_proj_b, v_w_out, v_ffn_norm_w, v_ffn_w_up, v_ffn_conv_w, v_ffn_conv_b, v_ffn_w_down, v_final_norm_w):
    given = dict(locals())
    wts = {n: given[n] for n in WEIGHTS}
    mom = {n: given["m_" + n] for n in WEIGHTS}
    var = {n: given["v_" + n] for n in WEIGHTS}
    xi, yi, ci = _axes()
    c_idx = jnp.reshape(ci, (1,)).astype(jnp.int32)
    q_idx = jnp.reshape(2 * xi + yi, (1,)).astype(jnp.int32)
    big_names = [n for n, _, _ in BIG]
    drop = lambda d, names: {n: d[n][0] for n in names}

    dev = 4 * xi + 2 * yi + ci
    packed = _pack_big(drop(wts, big_names), BF16, dev)
    gathered = _all_gather([packed[LATE_ROWS:]] + [wts[n] for n in VECTORS], name="w_in_all_gather")
    full = {"w_in_t": _join_windows(gathered[0])}
    for n, a in zip(VECTORS, gathered[1:]):
        r, c = VEC_SHAPE[n]
        full[n] = a[:, 0].transpose(1, 0, 2).reshape(r, NDEV * c)
    for n in REPL_ORDER:
        full[n] = wts[n].reshape(REPLICATED[n])

    hooks = _Exchange(packed[:LATE_ROWS], c_idx)
    loss_local, grad_x, grads = _local_step(x[0], loss_target[0], full, hooks)
    g_late = _grad_sum(hooks.late_sum, hooks.late_from_chips, q_idx, name="late_grad_sum")
    g_win = _grad_sum(hooks.w_in_sum, hooks.w_in_from_chips, q_idx, name="w_in_grad_sum")

    small = VECTORS + REPL_ORDER
    as_2d = lambda a: a if a.ndim >= 2 else a[None]
    part = [grads[n].reshape((1,) + SHARDED[n][0] if n in VECTORS else as_2d(wts[n]).shape) for n in small]
    part.append(loss_local)
    from_sibling = _exchange([p[None] for p in part], lambda x, y, c: [(0, (x, y, 1 - c))],
                             name="small_grads_to_sibling")
    chip_sums = _pair_sum_many(part, from_sibling, name="small_grad_pair_sum")
    from_chips = _exchange([s[None] for s in chip_sums],
                           lambda x, y, c: [(0, (1 - x, y, c)), (0, (x, 1 - y, c)), (0, (1 - x, 1 - y, c))],
                           name="small_grads_to_chips")
    totals = _chip_sum_many(chip_sums, from_chips, q_idx, name="small_grad_chip_sum")
    g_small, loss = dict(zip(small, totals)), totals[-1][0, 0]
    for n in VECTORS:
        c = VEC_SHAPE[n][1]
        g_small[n] = lax.dynamic_slice_in_dim(g_small[n], dev * c, c, axis=2)

    outs = {}
    small_g = [g_small[n] for n in small]
    small_out = _adam_many(small_g, *[[as_2d(d[n]) for n in small] for d in (wts, mom, var)], name="adam_small")
    for i, n in enumerate(small):
        outs[n] = tuple(a[i].reshape(wts[n].shape) for a in (small_g,) + tuple(small_out))
    for n, r, t in BIG:
        if n == "w_in":
            g = lax.dynamic_slice(g_win, (_win_offset(dev), 0), (WIN_R, D))
        else:
            g = g_late[BIG_OFF[n]:BIG_OFF[n] + r]
        flip = (lambda a: a.transpose(0, 2, 1)) if t else (lambda a: a)
        g = g[None]
        new = _adam(g, flip(wts[n]), flip(mom[n]), flip(var[n]), name="adam_" + n)
        outs[n] = tuple(flip(a) for a in (g,) + tuple(new))
    return (loss, grad_x[None]) + tuple(outs[n][k] for k in range(4) for n in WEIGHTS)
```

```python
import functools

import jax
import jax.numpy as jnp
from jax import lax
from jax.experimental import pallas as pl
from jax.experimental.pallas import tpu as pltpu

F32 = jnp.float32
BF16 = jnp.bfloat16

D = 1024
EPS = 1e-5
GW = 1024
GB = 128
GG = 8
GD = 128
GCH = 64
SI = 2048
SH = 32
SP = 64
SG = 4
SN = 128
SGW = SI // SG
SK = 4
SXBC = SI + 2 * SG * SN
DFF = 2816
FK = 3
PMAIN = 2 * D + 2 * GW + SI + SXBC
IN_COLS = PMAIN + SH
DP_SSM = SI + SXBC
DP_GAP = (DP_SSM - (2 * D + 2 * GW) % DP_SSM) % DP_SSM
DP_COLS = 2 * D + 2 * GW + DP_GAP + DP_SSM
assert DP_GAP % D == 0 and (2 * D + 2 * GW) % D == 0
NDEV = 8
ADAM_LR, ADAM_B1, ADAM_B2, ADAM_EPS, ADAM_WD, ADAM_STEP = 0.001, 0.9, 0.999, 1e-08, 0.01, 10

LANE = 128
SUBLANE = 8
VMEM_MB_V7X = 64
VMEM_CAP_MB = VMEM_MB_V7X - 8

LS = 128
FT = DFF // 2

NN = (((1,), (0,)), ((), ()))
NT = (((1,), (1,)), ((), ()))
TN = (((0,), (0,)), ((), ()))


def _params(sem, vmem_mb):
    return pltpu.CompilerParams(dimension_semantics=sem,
                                vmem_limit_bytes=min(int(vmem_mb), VMEM_CAP_MB) * 1024 * 1024)


def _dot(a, b, dims=NN):
    return lax.dot_general(a, b, dims, preferred_element_type=F32)


def _sigmoid(x):
    return 1.0 / (1.0 + jnp.exp(-x))


def _split3(v):
    hi = v.astype(BF16)
    r = v - hi.astype(F32)
    mid = r.astype(BF16)
    lo = (r - mid.astype(F32)).astype(BF16)
    return hi, mid, lo


def _dot3(a_f32, b_bf16, dims):
    hi, mid, lo = _split3(a_f32)
    return _dot(hi, b_bf16, dims) + _dot(mid, b_bf16, dims) + _dot(lo, b_bf16, dims)


def _dot2(a_f32, b_bf16, dims):
    hi, mid, _ = _split3(a_f32)
    return _dot(hi, b_bf16, dims) + _dot(mid, b_bf16, dims)


def _dot3_rhs(a_bf16, b_f32, dims):
    hi, mid, lo = _split3(b_f32)
    return _dot(a_bf16, hi, dims) + _dot(a_bf16, mid, dims) + _dot(a_bf16, lo, dims)


def _matmul(a, b, *, name, out_dtype, ta=False, tb=False, tm=1024, tn=1024, tk=1024, add=None,
            j_outer=False, b_rows=None, a_gap=None, o_row=None, side=None):
    assert not (o_row and add is not None)
    gap0, gapw = a_gap or (0, 0)
    if ta:
        K, M = a.shape
        M -= gapw
    else:
        M, K = a.shape
        K -= gapw
    if tb:
        N, K2 = b.shape
        N = b_rows or N
    else:
        K2, N = b.shape
        K2 = b_rows or K2
    assert K == K2, (a.shape, b.shape, ta, tb)
    tm, tn, tk = min(tm, M), min(tn, N), min(tk, K)
    assert M % tm == 0 and N % tn == 0 and K % tk == 0, (M, N, K, tm, tn, tk)
    nk = K // tk
    dims = (((0 if ta else 1,), (1 if tb else 0,)), ((), ()))
    has_add = add is not None
    n_in = 3 if has_add else 2
    s_in = len(side.inputs) if side else 0
    s_out = len(side.out_shapes) if side else 0
    grid = (N // tn, M // tm, nk) if j_outer else (M // tm, N // tn, nk)

    def body(*refs):
        a_ref, b_ref = refs[:2]
        add_ref = refs[2] if has_add else None
        o_ref = refs[n_in + s_in]
        if side:
            side_refs = (refs[n_in:n_in + s_in], refs[n_in + s_in + 1:n_in + s_in + 1 + s_out]) + tuple(refs[-2:])
            ids = [pl.program_id(d) for d in range(3)]
            first = functools.reduce(jnp.logical_and, [i == 0 for i in ids])
            last = functools.reduce(jnp.logical_and, [i == g - 1 for i, g in zip(ids, grid)])

            @pl.when(first)
            def _():
                for cp in side.make(*side_refs):
                    cp.start()

            @pl.when(last)
            def _():
                for cp in side.make(*side_refs):
                    cp.wait()

        p = lax.dot_general(a_ref[...].astype(BF16), b_ref[...].astype(BF16), dims,
                            preferred_element_type=F32)

        def finish(acc):
            if has_add:
                acc = acc + add_ref[...].astype(F32)
            o_ref[...] = acc.astype(o_ref.dtype)

        if nk == 1:
            finish(p)
        else:
            acc_ref = refs[n_in + s_in + 1 + s_out]
            k = pl.program_id(2)

            @pl.when(k == 0)
            def _():
                acc_ref[...] = p

            @pl.when(jnp.logical_and(k > 0, k < nk - 1))
            def _():
                acc_ref[...] += p

            @pl.when(k == nk - 1)
            def _():
                finish(acc_ref[...] + p)

    if j_outer:
        ij = lambda g0, g1: (g1, g0)
    else:
        ij = lambda g0, g1: (g0, g1)

    ta_col = tm if ta else tk
    assert gap0 % ta_col == 0 and gapw % ta_col == 0, (a_gap, ta_col)

    def a_map(g0, g1, k):
        i, _ = ij(g0, g1)
        col = i if ta else k
        col = col + jnp.where(col >= gap0 // ta_col, gapw // ta_col, 0) if gapw else col
        return (k, col) if ta else (i, col)

    def b_map(g0, g1, k):
        _, j = ij(g0, g1)
        return (j, k) if tb else (k, j)

    def o_map(g0, g1, k):
        i, j = ij(g0, g1)
        return (o_row(i) if o_row else i, j)

    in_specs = [pl.BlockSpec((tk, tm) if ta else (tm, tk), a_map),
                pl.BlockSpec((tn, tk) if tb else (tk, tn), b_map)]
    args = [a, b]
    if has_add:
        in_specs.append(pl.BlockSpec((tm, tn), o_map))
        args.append(add)
    scratch = [pltpu.VMEM((tm, tn), F32)] if nk > 1 else []
    osz = jnp.dtype(out_dtype).itemsize
    est = (2 * (tm * tk * a.dtype.itemsize + tk * tn * b.dtype.itemsize) + 2 * tm * tn * osz
           + (2 * tm * tn * add.dtype.itemsize if has_add else 0)
           + 3 * tm * tn * 4 + 2 * (tm * tk + tk * tn)) / 2 ** 20 + 4
    out_specs = [pl.BlockSpec((tm, tn), o_map)]
    out_shape = [jax.ShapeDtypeStruct((M, N), out_dtype)]
    aliases = {}
    if side:
        hbm = pl.BlockSpec(memory_space=pltpu.HBM)
        in_specs += [hbm] * s_in
        args += list(side.inputs)
        out_specs += [hbm] * s_out
        out_shape += list(side.out_shapes)
        scratch += [pltpu.SemaphoreType.DMA((side.nsem,)), pltpu.SemaphoreType.DMA((side.nsem,))]
        aliases = {n_in + i: 1 + j for i, j in side.aliases}
    outs = pl.pallas_call(
        body, grid=grid, in_specs=in_specs, out_specs=out_specs, out_shape=out_shape, scratch_shapes=scratch,
        input_output_aliases=aliases, name=name,
        compiler_params=_params(("arbitrary", "arbitrary", "arbitrary"), est))(*args)
    return (outs[0], list(outs[1:])) if side else outs[0]


class _Side:
    def __init__(self, inputs, out_shapes, nsem, make, aliases=()):
        self.inputs, self.out_shapes, self.nsem, self.make, self.aliases = inputs, out_shapes, nsem, make, aliases


def _rms_fwd(x, w, *, name):
    T = x.shape[0]
    tm = min(512, T)

    def body(x_ref, w_ref, o_ref):
        xv = x_ref[...]
        r = lax.rsqrt(jnp.mean(xv * xv, axis=-1, keepdims=True) + EPS)
        o_ref[...] = (xv * r * w_ref[...]).astype(BF16)

    return pl.pallas_call(
        body, grid=(T // tm,),
        in_specs=[pl.BlockSpec((tm, D), lambda i: (i, 0)), pl.BlockSpec((1, D), lambda i: (0, 0))],
        out_specs=pl.BlockSpec((tm, D), lambda i: (i, 0)),
        out_shape=jax.ShapeDtypeStruct((T, D), BF16), name=name,
        compiler_params=_params(("arbitrary",), 24))(x, w)


def _rms_bwd(x, w, dy, dres, *, name):
    T = x.shape[0]
    tm = min(512, T)

    def body(x_ref, w_ref, dy_ref, dres_ref, dx_ref, dw_ref):
        xv = x_ref[...]
        r = lax.rsqrt(jnp.mean(xv * xv, axis=-1, keepdims=True) + EPS)
        xhat = xv * r
        dyv = dy_ref[...].astype(F32)
        g = dyv * w_ref[...]
        dx_ref[...] = dres_ref[...] + r * (g - xhat * jnp.mean(g * xhat, axis=-1, keepdims=True))
        part = jnp.sum(dyv * xhat, axis=0, keepdims=True)

        @pl.when(pl.program_id(0) == 0)
        def _():
            dw_ref[...] = part

        @pl.when(pl.program_id(0) > 0)
        def _():
            dw_ref[...] += part

    row = pl.BlockSpec((tm, D), lambda i: (i, 0))
    vec = pl.BlockSpec((1, D), lambda i: (0, 0))
    return pl.pallas_call(
        body, grid=(T // tm,), in_specs=[row, vec, row, row], out_specs=[row, vec],
        out_shape=[jax.ShapeDtypeStruct((T, D), F32), jax.ShapeDtypeStruct((1, D), F32)], name=name,
        compiler_params=_params(("arbitrary",), 32))(x, w, dy, dres)


def _loss_head(h, tgt, w):
    T = h.shape[0]
    tm = min(512, T)

    def body(h_ref, t_ref, w_ref, loss_ref, dh_ref, dw_ref):
        hv = h_ref[...]
        r = lax.rsqrt(jnp.mean(hv * hv, axis=-1, keepdims=True) + EPS)
        xhat = hv * r
        wv = w_ref[...]
        err = xhat * wv - t_ref[...]
        lpart = 0.5 * jnp.sum(jnp.mean(err * err, axis=-1, keepdims=True), axis=0, keepdims=True)
        dy = err * (1.0 / D)
        g = dy * wv
        dh_ref[...] = r * (g - xhat * jnp.mean(g * xhat, axis=-1, keepdims=True))
        wpart = jnp.sum(dy * xhat, axis=0, keepdims=True)
        lrow = jnp.broadcast_to(lpart, (1, LANE))

        @pl.when(pl.program_id(0) == 0)
        def _():
            dw_ref[...] = wpart
            loss_ref[...] = lrow

        @pl.when(pl.program_id(0) > 0)
        def _():
            dw_ref[...] += wpart
            loss_ref[...] += lrow

    row = pl.BlockSpec((tm, D), lambda i: (i, 0))
    vec = pl.BlockSpec((1, D), lambda i: (0, 0))
    return pl.pallas_call(
        body, grid=(T // tm,), in_specs=[row, row, vec],
        out_specs=[pl.BlockSpec((1, LANE), lambda i: (0, 0)), row, vec],
        out_shape=[jax.ShapeDtypeStruct((1, LANE), F32), jax.ShapeDtypeStruct((T, D), F32),
                   jax.ShapeDtypeStruct((1, D), F32)], name="loss_head",
        compiler_params=_params(("arbitrary",), 32))(h, tgt, w)


_GELU_C = 0.7978845608028654
_GELU_A = 0.044715


def _gelu(x, with_grad=False):
    x2 = x * x
    cx = _GELU_C * x
    t = jnp.tanh(cx * (1.0 + _GELU_A * x2))
    h = 0.5 * (1.0 + t)
    if not with_grad:
        return x * h
    return x * h, h + 0.5 * cx * (1.0 - t * t) * (1.0 + 3.0 * _GELU_A * x2)


def _gmlp_mask():
    r = lax.broadcasted_iota(jnp.int32, (GB, GB), 0) // GCH
    c = lax.broadcasted_iota(jnp.int32, (GB, GB), 1) // GCH
    return c <= r


def _gmlp_fwd(proj, lnw, lnb, ws, bst):
    T = proj.shape[0]
    tm = min(512, T)
    nblk = tm // GB

    def body(u_ref, v_ref, lnw_ref, lnb_ref, ws_ref, bst_ref, o_ref):
        mask = _gmlp_mask()
        u = _gelu(u_ref[...].astype(F32))
        v = _gelu(v_ref[...].astype(F32))
        for g in range(GG):
            cs = slice(g * GD, (g + 1) * GD)
            vg = v[:, cs]
            mu = jnp.mean(vg, axis=-1, keepdims=True)
            vc = vg - mu
            var = jnp.mean(vc * vc, axis=-1, keepdims=True)
            vn = (vc * lax.rsqrt(var + EPS) * lnw_ref[g:g + 1, :] + lnb_ref[g:g + 1, :]).astype(BF16)
            wsg = jnp.where(mask, ws_ref[g], 0.0).astype(BF16)
            bcol = bst_ref[:, g:g + 1]
            for blk in range(nblk):
                rs = slice(blk * GB, (blk + 1) * GB)
                sv = _dot(wsg, vn[rs, :]) + bcol
                o_ref[rs, cs] = (u[rs, cs] * sv).astype(BF16)

    full = lambda shape: pl.BlockSpec(shape, lambda i: tuple(0 for _ in shape))
    return pl.pallas_call(
        body, grid=(T // tm,),
        in_specs=[pl.BlockSpec((tm, GW), lambda i: (i, 2)), pl.BlockSpec((tm, GW), lambda i: (i, 3)),
                  full((GG, GD)), full((GG, GD)), full((GG, GB, GB)), full((GB, GG))],
        out_specs=pl.BlockSpec((tm, GW), lambda i: (i, 0)),
        out_shape=jax.ShapeDtypeStruct((T, GW), BF16), name="gmlp_fwd",
        compiler_params=_params(("arbitrary",), 40))(proj, proj, lnw, lnb, ws, bst)


def _gmlp_bwd(proj, dya, dproj, lnw, lnb, ws, bst):
    T = proj.shape[0]
    tm = min(512, T)
    nblk = tm // GB

    def body(u_ref, v_ref, dya_ref, dproj_in, lnw_ref, lnb_ref, ws_ref, bst_ref,
             dz_ref, dlnw_ref, dlnb_ref, dws_ref, dbst_ref):
        del dproj_in
        first = pl.program_id(0) == 0

        @pl.when(first)
        def _():
            dlnw_ref[...] = jnp.zeros_like(dlnw_ref)
            dlnb_ref[...] = jnp.zeros_like(dlnb_ref)
            dws_ref[...] = jnp.zeros_like(dws_ref)
            dbst_ref[...] = jnp.zeros_like(dbst_ref)

        mask = _gmlp_mask()
        lane = lax.broadcasted_iota(jnp.int32, (GB, LANE), 1)
        ur = u_ref[...].astype(F32)
        vr = v_ref[...].astype(F32)
        u, gu = _gelu(ur, with_grad=True)
        v, gv = _gelu(vr, with_grad=True)
        dy = dya_ref[...].astype(F32)
        dbst = jnp.zeros((GB, LANE), F32)
        dlnw_rows, dlnb_rows = [], []
        for g in range(GG):
            cs = slice(g * GD, (g + 1) * GD)
            vg = v[:, cs]
            mu = jnp.mean(vg, axis=-1, keepdims=True)
            vc = vg - mu
            var = jnp.mean(vc * vc, axis=-1, keepdims=True)
            rstd = lax.rsqrt(var + EPS)
            xhat = vc * rstd
            lw = lnw_ref[g:g + 1, :]
            vn = (xhat * lw + lnb_ref[g:g + 1, :]).astype(BF16)
            wsg = jnp.where(mask, ws_ref[g], 0.0).astype(BF16)
            bcol = bst_ref[:, g:g + 1]
            dyg = dy[:, cs]
            ug = u[:, cs]
            dsv = dyg * ug
            dsv_b = dsv.astype(BF16)
            dws_g = jnp.zeros((GB, GB), F32)
            bsum = jnp.zeros((GB, 1), F32)
            dvn_parts = []
            for blk in range(nblk):
                rs = slice(blk * GB, (blk + 1) * GB)
                sv = _dot(wsg, vn[rs, :]) + bcol
                dz_ref[rs, cs] = (dyg[rs, :] * sv * gu[rs, cs]).astype(BF16)
                dws_g = dws_g + _dot(dsv_b[rs, :], vn[rs, :], NT)
                bsum = bsum + jnp.sum(dsv[rs, :], axis=-1, keepdims=True)
                dvn_parts.append(_dot(wsg, dsv_b[rs, :], TN))
            dvn = jnp.concatenate(dvn_parts, axis=0)
            dws_ref[g] += jnp.where(mask, dws_g, 0.0)
            dbst = dbst + jnp.where(lane == g, bsum, 0.0)
            dlnw_rows.append(jnp.sum(dvn * xhat, axis=0, keepdims=True))
            dlnb_rows.append(jnp.sum(dvn, axis=0, keepdims=True))
            dxh = dvn * lw
            dvg = rstd * (dxh - jnp.mean(dxh, axis=-1, keepdims=True)
                          - xhat * jnp.mean(dxh * xhat, axis=-1, keepdims=True))
            dz_ref[:, GW + g * GD:GW + (g + 1) * GD] = (dvg * gv[:, cs]).astype(BF16)
        dlnw_ref[...] += jnp.concatenate(dlnw_rows, axis=0)
        dlnb_ref[...] += jnp.concatenate(dlnb_rows, axis=0)
        dbst_ref[...] += dbst

    full = lambda shape: pl.BlockSpec(shape, lambda i: tuple(0 for _ in shape))
    outs = pl.pallas_call(
        body, grid=(T // tm,),
        in_specs=[pl.BlockSpec((tm, GW), lambda i: (i, 2)), pl.BlockSpec((tm, GW), lambda i: (i, 3)),
                  pl.BlockSpec((tm, GW), lambda i: (i, 0)), pl.BlockSpec(memory_space=pl.ANY),
                  full((GG, GD)), full((GG, GD)), full((GG, GB, GB)), full((GB, GG))],
        out_specs=[pl.BlockSpec((tm, 2 * GW), lambda i: (i, 1)), full((GG, GD)), full((GG, GD)),
                   full((GG, GB, GB)), full((GB, LANE))],
        out_shape=[jax.ShapeDtypeStruct(dproj.shape, dproj.dtype), jax.ShapeDtypeStruct((GG, GD), F32),
                   jax.ShapeDtypeStruct((GG, GD), F32), jax.ShapeDtypeStruct((GG, GB, GB), F32),
                   jax.ShapeDtypeStruct((GB, LANE), F32)],
        input_output_aliases={3: 0}, name="gmlp_bwd",
        compiler_params=_params(("arbitrary",), 48))(proj, proj, dya, dproj, lnw, lnb, ws, bst)
    return outs


def _merge_fwd(ya_pre, yb_pre, proj, bias, wpa, wpb):
    T = proj.shape[0]
    tm = min(512, T)

    def body(ya_ref, yb_ref, g_ref, b_ref, wpa_ref, wpb_ref, m_ref, oa_ref, ob_ref):
        ya = _dot(ya_ref[...], wpa_ref[...])
        yb = _dot(yb_ref[...], wpb_ref[...])
        g = g_ref[...].astype(F32)
        sa = _sigmoid(g[:, :D] + b_ref[0:1, :])
        sb = _sigmoid(g[:, D:] + b_ref[1:2, :])
        m_ref[...] = (sa * ya + sb * yb).astype(BF16)
        oa_ref[...] = ya.astype(BF16)
        ob_ref[...] = yb.astype(BF16)

    row = lambda w: pl.BlockSpec((tm, w), lambda i: (i, 0))
    full = lambda shape: pl.BlockSpec(shape, lambda i: tuple(0 for _ in shape))
    o = jax.ShapeDtypeStruct((T, D), BF16)
    return pl.pallas_call(
        body, grid=(T // tm,),
        in_specs=[row(GW), row(SI), row(2 * D), full((2, D)), full((GW, D)), full((SI, D))],
        out_specs=[row(D), row(D), row(D)], out_shape=[o, o, o], name="merge_fwd",
        compiler_params=_params(("arbitrary",), 40))(ya_pre, yb_pre, proj, bias, wpa, wpb)


def _merge_bwd(dm, proj, bias, ya, yb, wpa, wpb):
    T = proj.shape[0]
    tm = min(512, T)

    def body(dm_ref, g_ref, b_ref, ya_ref, yb_ref, wpa_ref, wpb_ref,
             dg_ref, dya_ref, dyb_ref, dpa_ref, dpb_ref, db_ref):
        dmv = dm_ref[...].astype(F32)
        g = g_ref[...].astype(F32)
        sa = _sigmoid(g[:, :D] + b_ref[0:1, :])
        sb = _sigmoid(g[:, D:] + b_ref[1:2, :])
        dya = (dmv * sa).astype(BF16)
        dyb = (dmv * sb).astype(BF16)
        dga = dmv * ya_ref[...].astype(F32) * sa * (1.0 - sa)
        dgb = dmv * yb_ref[...].astype(F32) * sb * (1.0 - sb)
        dg_ref[:, :D] = dga.astype(BF16)
        dg_ref[:, D:] = dgb.astype(BF16)
        dya_ref[...] = dya
        dyb_ref[...] = dyb
        dpa_ref[...] = _dot(dya, wpa_ref[...], NT).astype(BF16)
        dpb_ref[...] = _dot(dyb, wpb_ref[...], NT).astype(BF16)
        part = jnp.concatenate([jnp.sum(dga, axis=0, keepdims=True), jnp.sum(dgb, axis=0, keepdims=True)], axis=0)

        @pl.when(pl.program_id(0) == 0)
        def _():
            db_ref[...] = part

        @pl.when(pl.program_id(0) > 0)
        def _():
            db_ref[...] += part

    row = lambda w: pl.BlockSpec((tm, w), lambda i: (i, 0))
    full = lambda shape: pl.BlockSpec(shape, lambda i: tuple(0 for _ in shape))
    o = lambda w: jax.ShapeDtypeStruct((T, w), BF16)
    return pl.pallas_call(
        body, grid=(T // tm,),
        in_specs=[row(D), row(2 * D), full((2, D)), row(D), row(D), full((GW, D)), full((SI, D))],
        out_specs=[row(2 * D), row(D), row(D), row(GW), row(SI), full((2, D))],
        out_shape=[o(DP_COLS), o(D), o(D), o(GW), o(SI), jax.ShapeDtypeStruct((2, D), F32)], name="merge_bwd",
        compiler_params=_params(("arbitrary",), 48))(dm, proj, bias, ya, yb, wpa, wpb)


RB = 128


def _shift_matrix(j):
    r = lax.broadcasted_iota(jnp.int32, (RB, RB), 0)
    c = lax.broadcasted_iota(jnp.int32, (RB, RB), 1)
    return jnp.where(c == r - j, 1.0, 0.0).astype(BF16)


def _rows_down(xb, before, shifts):
    H = SUBLANE
    mats = [_shift_matrix(j) for j in shifts]
    outs = [[] for _ in shifts]
    for b in range(xb.shape[0] // RB):
        blk = xb[b * RB:(b + 1) * RB]
        edge = jnp.concatenate([before, blk[:2 * H].astype(F32)[:H]], axis=0)
        for i, j in enumerate(shifts):
            outs[i] += [edge[H - j:2 * H - j], _dot(mats[i], blk)[H:]]
        before = blk[RB - 2 * H:].astype(F32)[H:]
    return [jnp.concatenate(o, axis=0) for o in outs]


def _rows_up(xb, after, shifts):
    H = SUBLANE
    nb = xb.shape[0] // RB
    mats = [_shift_matrix(-j) for j in shifts]
    outs = [[] for _ in shifts]
    for b in range(nb):
        blk = xb[b * RB:(b + 1) * RB]
        nxt = xb[(b + 1) * RB:(b + 1) * RB + 2 * H].astype(F32)[:H] if b + 1 < nb else after
        edge = jnp.concatenate([blk[RB - 2 * H:].astype(F32)[H:], nxt], axis=0)
        for i, j in enumerate(shifts):
            outs[i] += [_dot(mats[i], blk)[:RB - H], edge[j:H + j]]
    return [jnp.concatenate(o, axis=0) for o in outs]


def _ffn_act_fwd(up, cw, cb):
    T = up.shape[0]
    tm = min(512, T)
    H = SUBLANE

    def body(up_ref, cw_ref, cb_ref, o_ref, xc_ref, halo):
        @pl.when(pl.program_id(1) == 0)
        def _():
            halo[...] = jnp.zeros_like(halo)

        xb = up_ref[...]
        x2, x1 = _rows_down(xb, halo[...], (2, 1))
        xc = cb_ref[...] + cw_ref[0:1, :] * x2 + cw_ref[1:2, :] * x1 + cw_ref[2:3, :] * xb.astype(F32)
        xc_ref[...] = xc.astype(BF16)
        gate = xc[:, :FT]
        o_ref[...] = (gate * _sigmoid(gate) * xc[:, FT:]).astype(BF16)
        halo[...] = xb[tm - 2 * H:].astype(F32)[H:]

    tile = pl.BlockSpec((tm, 2 * FT), lambda j, i: (i, j))
    return pl.pallas_call(
        body, grid=(2, T // tm),
        in_specs=[tile, pl.BlockSpec((FK, 2 * FT), lambda j, i: (0, j)), pl.BlockSpec((1, 2 * FT), lambda j, i: (0, j))],
        out_specs=[pl.BlockSpec((tm, FT), lambda j, i: (i, j)), tile],
        out_shape=[jax.ShapeDtypeStruct((T, DFF), BF16), jax.ShapeDtypeStruct((T, 2 * DFF), BF16)],
        scratch_shapes=[pltpu.VMEM((H, 2 * FT), F32)], name="ffn_act_fwd",
        compiler_params=_params(("arbitrary", "arbitrary"), 48))(up, cw, cb)


def _ffn_act_bwd(up, xc, dact, cw):
    T = up.shape[0]
    tm = min(512, T)
    nt = T // tm
    H = SUBLANE

    def body(up_ref, xc_ref, da_ref, cw_ref, dup_ref, dcw_ref, dcb_ref, ahead):
        @pl.when(pl.program_id(1) == 0)
        def _():
            ahead[...] = jnp.zeros_like(ahead)
            dcw_ref[...] = jnp.zeros_like(dcw_ref)
            dcb_ref[...] = jnp.zeros_like(dcb_ref)

        xcv = xc_ref[...].astype(F32)
        gate, val = xcv[:, :FT], xcv[:, FT:]
        sg = _sigmoid(gate)
        dav = da_ref[...].astype(F32)
        dgate = dav * val * sg * (1.0 + gate * (1.0 - sg))
        dval = dav * gate * sg
        dxc = jnp.concatenate([dgate, dval], axis=1)
        d1, d2 = _rows_up(dxc.astype(BF16), ahead[...], (1, 2))
        x = up_ref[...].astype(F32)
        dcb_ref[...] += jnp.sum(dxc, axis=0, keepdims=True)
        dcw_ref[...] += jnp.concatenate([jnp.sum(d * x, axis=0, keepdims=True) for d in (d2, d1, dxc)], axis=0)
        dup_ref[...] = (cw_ref[2:3, :] * dxc + cw_ref[1:2, :] * d1 + cw_ref[0:1, :] * d2).astype(BF16)
        ahead[...] = dxc[0:H, :]

    tile = pl.BlockSpec((tm, 2 * FT), lambda j, i: (nt - 1 - i, j))
    return pl.pallas_call(
        body, grid=(2, nt),
        in_specs=[tile, tile, pl.BlockSpec((tm, FT), lambda j, i: (nt - 1 - i, j)),
                  pl.BlockSpec((FK, 2 * FT), lambda j, i: (0, j))],
        out_specs=[tile, pl.BlockSpec((FK, 2 * FT), lambda j, i: (0, j)), pl.BlockSpec((1, 2 * FT), lambda j, i: (0, j))],
        out_shape=[jax.ShapeDtypeStruct((T, 2 * DFF), BF16), jax.ShapeDtypeStruct((FK, 2 * DFF), F32),
                   jax.ShapeDtypeStruct((1, 2 * DFF), F32)],
        scratch_shapes=[pltpu.VMEM((H, 2 * FT), F32)], name="ffn_act_bwd",
        compiler_params=_params(("arbitrary", "arbitrary"), 56))(up, xc, dact, cw)


def _softplus(x):
    e = jnp.exp(-jnp.abs(x))
    return jnp.maximum(x, 0.0) + jnp.where(e < 1e-4, e * (1.0 - 0.5 * e), jnp.log(1.0 + e))


def _ssd_tril():
    li = lax.broadcasted_iota(jnp.int32, (LS, LS), 0)
    si = lax.broadcasted_iota(jnp.int32, (LS, LS), 1)
    return si <= li


def _head_expansion():
    hh = lax.broadcasted_iota(jnp.int32, (LANE, SI), 0)
    cc = lax.broadcasted_iota(jnp.int32, (LANE, SI), 1) // SP
    return jnp.where(hh == cc, 1.0, 0.0).astype(BF16)


def _ssd_pre(xc, dt_ref, dtb_ref, alog_ref, tril, expand):
    sx = _sigmoid(xc)
    xbc = xc * sx
    xs, bm, cm = xbc[:, :SI], xbc[:, SI:SI + SG * SN], xbc[:, SI + SG * SN:]
    dtin = dt_ref[...] + dtb_ref[...]
    dt = _softplus(dtin)
    a_neg = -jnp.exp(alog_ref[...])
    dta = dt * a_neg
    trilb = jnp.where(tril, 1.0, 0.0).astype(BF16)
    a = _dot3_rhs(trilb, dta, NN)
    a_exp = _dot3(a, expand, NN)
    dt_exp = _dot2(dt, expand, NN)
    xdt = xs * dt_exp
    a_last = a_exp[LS - 1:LS, :]
    return dict(xc=xc, sx=sx, xs=xs, bm=bm, cm=cm, dtin=dt_ref[...] + dtb_ref[...], dt=dt, a_neg=a_neg,
                a=a, a_t=a.T, a_exp=a_exp, dt_exp=dt_exp, xdt=xdt, ea=jnp.exp(a_exp),
                w=jnp.exp(a_last - a_exp), eal=jnp.exp(a_last))


def _head_decay(pre, tril, h):
    seg = pre["a"][:, h:h + 1] - pre["a_t"][h:h + 1, :]
    return jnp.exp(jnp.where(tril, seg, -1e30))


def _ssd_fwd(proj, dtraw, cw, cb, dtb, alog, dexp, nw):
    T = proj.shape[0]
    nc = T // LS
    H = SUBLANE

    def body(z_ref, x_ref, dt_ref, cw_ref, cb_ref, dtb_ref, alog_ref, dexp_ref, nw_ref, ex_ref,
             yb_ref, y_ref, sp_ref, xc_ref, halo, st):
        @pl.when(pl.program_id(0) == 0)
        def _():
            halo[...] = jnp.zeros_like(halo)
            st[...] = jnp.zeros_like(st)

        xb = x_ref[...]
        taps = _rows_down(xb, halo[...], (3, 2, 1)) + [xb.astype(F32)]
        xc = cb_ref[...]
        for k in range(SK):
            xc = xc + cw_ref[k:k + 1, :] * taps[k]
        xc_ref[...] = xc.astype(BF16)
        tril, expand = _ssd_tril(), ex_ref[...]
        pre = _ssd_pre(xc, dt_ref, dtb_ref, alog_ref, tril, expand)
        lane = lax.broadcasted_iota(jnp.int32, (LS, LANE), 1)
        lo = lane < SP
        zf = z_ref[...].astype(F32)
        siluz = zf * _sigmoid(zf)
        for g in range(SG):
            gs = slice(g * SGW, (g + 1) * SGW)
            bg = pre["bm"][:, g * SN:(g + 1) * SN].astype(BF16)
            cg = pre["cm"][:, g * SN:(g + 1) * SN].astype(BF16)
            gmat = _dot(cg, bg, NT)
            sg = st[g]
            sp_ref[0, g] = sg
            yoff = _dot(cg, sg.astype(BF16))
            parts = []
            for j in range(SGW // LANE):
                h0 = g * (SGW // SP) + 2 * j
                m0 = gmat * _head_decay(pre, tril, h0)
                m1 = gmat * _head_decay(pre, tril, h0 + 1)
                xp = pre["xdt"][:, g * SGW + j * LANE:g * SGW + (j + 1) * LANE]
                rhs = jnp.concatenate([jnp.where(lo, xp, 0.0), jnp.where(lo, 0.0, xp)], axis=0).astype(BF16)
                parts.append(_dot(jnp.concatenate([m0, m1], axis=1).astype(BF16), rhs))
            y = (jnp.concatenate(parts, axis=1) + pre["ea"][:, gs] * yoff + dexp_ref[:, gs] * pre["xs"][:, gs])
            st[g] = pre["eal"][:, gs] * sg + _dot(bg, (pre["w"][:, gs] * pre["xdt"][:, gs]).astype(BF16), TN)
            y_ref[:, gs] = y
            yg = y * siluz[:, gs]
            r = lax.rsqrt(jnp.mean(yg * yg, axis=-1, keepdims=True) + EPS)
            yb_ref[:, gs] = (yg * r * nw_ref[:, gs]).astype(BF16)
        halo[...] = xb[LS - 2 * H:].astype(F32)[H:]

    vec = lambda w: pl.BlockSpec((1, w), lambda c: (0, 0))
    return pl.pallas_call(
        body, grid=(nc,),
        in_specs=[pl.BlockSpec((LS, SI), lambda c: (c, 2)), pl.BlockSpec((LS, SXBC), lambda c: (c, 2)),
                  pl.BlockSpec((LS, LANE), lambda c: (c, 0)),
                  pl.BlockSpec((SK, SXBC), lambda c: (0, 0)), vec(SXBC), vec(LANE), vec(LANE), vec(SI), vec(SI),
                  pl.BlockSpec((LANE, SI), lambda c: (0, 0))],
        out_specs=[pl.BlockSpec((LS, SI), lambda c: (c, 0)), pl.BlockSpec((LS, SI), lambda c: (c, 0)),
                   pl.BlockSpec((1, SG, SN, SGW), lambda c: (c, 0, 0, 0)), pl.BlockSpec((LS, SXBC), lambda c: (c, 0))],
        out_shape=[jax.ShapeDtypeStruct((T, SI), BF16), jax.ShapeDtypeStruct((T, SI), F32),
                   jax.ShapeDtypeStruct((nc, SG, SN, SGW), F32), jax.ShapeDtypeStruct((T, SXBC), BF16)],
        scratch_shapes=[pltpu.VMEM((H, SXBC), F32), pltpu.VMEM((SG, SN, SGW), F32)], name="ssd_fwd",
        compiler_params=_params(("arbitrary",), VMEM_CAP_MB))(
            proj, proj, dtraw, cw, cb, dtb, alog, dexp, nw, _head_expansion())


def _ssd_bwd(proj, xcs, dtraw, y, sprev, dyb, dproj, cw, dtb, alog, dexp, nw):
    T = proj.shape[0]
    nc = T // LS
    H = SUBLANE
    NJ = 1

    def body(z_ref, x_ref, xc_ref, dt_ref, y_ref, sp_ref, dyb_ref, dproj_in,
             cw_ref, dtb_ref, alog_ref, dexp_ref, nw_ref, ex_ref,
             dp_ref, ddt_ref, dcw_ref, dcb_ref, ddtb_ref, da_ref, dd_ref, dnw_ref,
             ahead, ds, stage):
        del dproj_in
        i = pl.program_id(0)
        j = pl.program_id(1)

        @pl.when(jnp.logical_and(i == 0, j == 0))
        def _():
            ahead[...] = jnp.zeros_like(ahead)
            ds[...] = jnp.zeros_like(ds)
            for r in (dcw_ref, dcb_ref, ddtb_ref, da_ref, dd_ref, dnw_ref):
                r[...] = jnp.zeros_like(r)

        @pl.when(j == 0)
        def _():
            tril, expand = _ssd_tril(), ex_ref[...]
            pre = _ssd_pre(xc_ref[...].astype(F32), dt_ref, dtb_ref, alog_ref, tril, expand)
            lane = lax.broadcasted_iota(jnp.int32, (LS, LANE), 1)
            sub = lax.broadcasted_iota(jnp.int32, (LANE, LS), 0)
            rowi = lax.broadcasted_iota(jnp.int32, (LS, 1), 0)
            lo = lane < SP
            xs, xdt, ea, w, eal = pre["xs"], pre["xdt"], pre["ea"], pre["w"], pre["eal"]

            zf = z_ref[...].astype(F32)
            sz = _sigmoid(zf)
            siluz = zf * sz
            yv = y_ref[...]
            yg = yv * siluz
            dout = dyb_ref[...].astype(F32)
            dyg_parts, dnw_parts = [], []
            for g in range(SG):
                gs = slice(g * SGW, (g + 1) * SGW)
                ygg = yg[:, gs]
                r = lax.rsqrt(jnp.mean(ygg * ygg, axis=-1, keepdims=True) + EPS)
                yhat = ygg * r
                dn = dout[:, gs] * nw_ref[:, gs]
                dnw_parts.append(jnp.sum(dout[:, gs] * yhat, axis=0, keepdims=True))
                dyg_parts.append(r * (dn - yhat * jnp.mean(dn * yhat, axis=-1, keepdims=True)))
            dyg = jnp.concatenate(dyg_parts, axis=1)
            dnw_ref[...] += jnp.concatenate(dnw_parts, axis=1)
            dy = dyg * siluz
            stage[:, 0:SI] = (dyg * yv * sz * (1.0 + zf * (1.0 - sz))).astype(BF16)
            dd_ref[...] += jnp.sum(dy * xs, axis=0, keepdims=True)
            tt = ea * dy

            da_rows = jnp.zeros((LS, LANE), F32)
            da_cols = jnp.zeros((LANE, LS), F32)
            dxdt_parts, db_parts, dc_parts, daexp_parts = [], [], [], []
            for g in range(SG):
                gs = slice(g * SGW, (g + 1) * SGW)
                bg = pre["bm"][:, g * SN:(g + 1) * SN].astype(BF16)
                cg = pre["cm"][:, g * SN:(g + 1) * SN].astype(BF16)
                sg = sp_ref[0, g]
                sgb = sg.astype(BF16)
                dsg = ds[g]
                dsgb = dsg.astype(BF16)
                ttg = tt[:, gs].astype(BF16)
                yoff = _dot(cg, sgb)
                dc = _dot(ttg, sgb, NT)
                gmat = _dot(cg, bg, NT)
                dgm = jnp.zeros((LS, LS), F32)
                dxdt_pairs = []
                for jj in range(SGW // LANE):
                    h0 = g * (SGW // SP) + 2 * jj
                    ps = slice(g * SGW + jj * LANE, g * SGW + (jj + 1) * LANE)
                    l0 = _head_decay(pre, tril, h0)
                    l1 = _head_decay(pre, tril, h0 + 1)
                    m0 = gmat * l0
                    m1 = gmat * l1
                    dyp = dy[:, ps]
                    dy_lo = jnp.where(lo, dyp, 0.0).astype(BF16)
                    dy_hi = jnp.where(lo, 0.0, dyp).astype(BF16)
                    xpb = xdt[:, ps].astype(BF16)
                    dm0 = _dot(dy_lo, xpb, NT)
                    dm1 = _dot(dy_hi, xpb, NT)
                    q0 = dm0 * m0
                    q1 = dm1 * m1
                    da_rows = da_rows + jnp.where(lane == h0, jnp.sum(q0, axis=1, keepdims=True), 0.0)
                    da_rows = da_rows + jnp.where(lane == h0 + 1, jnp.sum(q1, axis=1, keepdims=True), 0.0)
                    da_cols = da_cols + jnp.where(sub == h0, jnp.sum(q0, axis=0, keepdims=True), 0.0)
                    da_cols = da_cols + jnp.where(sub == h0 + 1, jnp.sum(q1, axis=0, keepdims=True), 0.0)
                    dgm = dgm + dm0 * l0 + dm1 * l1
                    mcat = jnp.concatenate([m0, m1], axis=0).astype(BF16)
                    dycat = jnp.concatenate([dy_lo, dy_hi], axis=0)
                    dxdt_pairs.append(_dot(mcat, dycat, TN))
                dgb = dgm.astype(BF16)
                dc = dc + _dot(dgb, bg)
                db = _dot(dgb, cg, TN)
                zg = _dot(bg, dsgb)
                wg, xdtg = w[:, gs], xdt[:, gs]
                dxdt_g = jnp.concatenate(dxdt_pairs, axis=1) + wg * zg
                qg = zg * xdtg * wg
                last = (jnp.sum(qg, axis=0, keepdims=True)
                        + jnp.sum(dsg * sg, axis=0, keepdims=True) * eal[:, gs])
                daexp_parts.append(dy[:, gs] * ea[:, gs] * yoff - qg + jnp.where(rowi == LS - 1, last, 0.0))
                db = db + _dot((wg * xdtg).astype(BF16), dsgb, NT)
                ds[g] = eal[:, gs] * dsg + _dot(cg, ttg, TN)
                dxdt_parts.append(dxdt_g)
                db_parts.append(db)
                dc_parts.append(dc)
            dxdt = jnp.concatenate(dxdt_parts, axis=1)
            da_exp = jnp.concatenate(daexp_parts, axis=1)
            da = _dot2(da_exp, expand, NT) + da_rows - da_cols.T
            triub = jnp.where(tril, 1.0, 0.0).astype(BF16)
            ddta = _dot3_rhs(triub, da, TN)
            ddt = ddta * pre["a_neg"] + _dot2(dxdt * xs, expand, NT)
            da_ref[...] += jnp.sum(ddta * pre["dt"], axis=0, keepdims=True)
            ddt_raw = ddt * _sigmoid(pre["dtin"])
            ddt_ref[...] = ddt_raw
            ddtb_ref[...] += jnp.sum(ddt_raw, axis=0, keepdims=True)
            dxs = dexp_ref[...] * dy + dxdt * pre["dt_exp"]
            dxbc = jnp.concatenate([dxs] + db_parts + dc_parts, axis=1)
            sx, xc = pre["sx"], pre["xc"]
            dxc = dxbc * sx * (1.0 + xc * (1.0 - sx))
            taps = _rows_up(dxc.astype(BF16), ahead[...], (3, 2, 1)) + [dxc]
            xr = x_ref[...].astype(F32)
            dcb_ref[...] += jnp.sum(dxc, axis=0, keepdims=True)
            dcw_ref[...] += jnp.concatenate([jnp.sum(t * xr, axis=0, keepdims=True) for t in taps], axis=0)
            dxr = cw_ref[0:1, :] * taps[0]
            for k in range(1, SK):
                dxr = dxr + cw_ref[k:k + 1, :] * taps[k]
            stage[:, SI:] = dxr.astype(BF16)
            ahead[...] = dxc[0:H, :]

        dp_ref[...] = stage[...]

    vec = lambda w: pl.BlockSpec((1, w), lambda i, j: (0, 0))
    rev = lambda w, cb_: pl.BlockSpec((LS, w), lambda i, j: (nc - 1 - i, cb_))
    outs = pl.pallas_call(
        body, grid=(nc, NJ),
        in_specs=[rev(SI, 2), rev(SXBC, 2), rev(SXBC, 0),
                  rev(LANE, 0), rev(SI, 0),
                  pl.BlockSpec((1, SG, SN, SGW), lambda i, j: (nc - 1 - i, 0, 0, 0)),
                  rev(SI, 0), pl.BlockSpec(memory_space=pl.ANY),
                  pl.BlockSpec((SK, SXBC), lambda i, j: (0, 0)), vec(LANE), vec(LANE), vec(SI), vec(SI),
                  pl.BlockSpec((LANE, SI), lambda i, j: (0, 0))],
        out_specs=[rev(DP_SSM, 1), rev(LANE, 0),
                   pl.BlockSpec((SK, SXBC), lambda i, j: (0, 0)), vec(SXBC), vec(LANE), vec(LANE), vec(SI), vec(SI)],
        out_shape=[jax.ShapeDtypeStruct(dproj.shape, dproj.dtype), jax.ShapeDtypeStruct((T, LANE), F32),
                   jax.ShapeDtypeStruct((SK, SXBC), F32), jax.ShapeDtypeStruct((1, SXBC), F32),
                   jax.ShapeDtypeStruct((1, LANE), F32), jax.ShapeDtypeStruct((1, LANE), F32),
                   jax.ShapeDtypeStruct((1, SI), F32), jax.ShapeDtypeStruct((1, SI), F32)],
        scratch_shapes=[pltpu.VMEM((H, SXBC), F32),
                        pltpu.VMEM((SG, SN, SGW), F32), pltpu.VMEM((LS, SI + SXBC), BF16)],
        input_output_aliases={7: 0}, name="ssd_bwd",
        compiler_params=_params(("arbitrary", "arbitrary"), VMEM_CAP_MB))(
            proj, proj, xcs, dtraw, y, sprev, dyb, dproj, cw, dtb, alog, dexp, nw, _head_expansion())
    return outs


def _perm_ffn_cols(a):
    lead = a.shape[:-1]
    return a.reshape(lead + (2, 2, FT)).swapaxes(-3, -2).reshape(lead + (2 * DFF,))


def _perm_ffn_rows(a):
    return a.reshape((2, 2, FT) + a.shape[1:]).swapaxes(0, 1).reshape(a.shape)


def _pad_lanes(v, n=LANE):
    return jnp.pad(v, ((0, 0), (0, n - v.shape[-1])))


LATE = ["w_proj_a", "w_proj_b", "w_out", "ffn_w_up_t", "ffn_w_down"]
WGRAD = BF16
SMALL_BF16_FROM = 2 ** 16


class _NoExchange:
    def gather_start(self):
        return None

    def gather_pass_on(self, outs):
        return None

    def late_weights(self, w, outs):
        return w

    def reduce_late(self, grads):
        return None

    def w_in_to_sibling(self, grad_main, grad_dt):
        return None

    def reduce_w_in(self, outs):
        return None

    def reduced(self, late_outs, w_in_outs):
        pass


def _local_step(x, tgt, w, hooks=None):
    hooks = hooks or _NoExchange()

    def mm(*args, side=None, **kw):
        out = _matmul(*args, side=side, **kw)
        return out if side is not None else (out, [])

    win_t = w["w_in_t"]
    win_dt = jnp.pad(w["w_in_t"][PMAIN:], ((0, LANE - SH), (0, 0)))
    fcw = _perm_ffn_cols(w["ffn_conv_w"])
    fcb = _perm_ffn_cols(w["ffn_conv_b"][None, :])
    mixw = w["mix_norm_w"][None, :]
    ffnw = w["ffn_norm_w"][None, :]
    finw = w["final_norm_w"][None, :]
    bst = w["gmlp_bs"].T
    scb = w["ssm_conv_b"][None, :]
    dtb = _pad_lanes(w["ssm_dt_bias"][None, :])
    alog = _pad_lanes(w["ssm_a_log"][None, :])
    dexp = jnp.repeat(w["ssm_d"], SP)[None, :]
    snw = w["ssm_norm_w"][None, :]

    xn = _rms_fwd(x, mixw, name="mix_norm")
    proj, got = mm(xn, win_t, name="in_proj", out_dtype=BF16, tb=True, tn=3072, j_outer=True, b_rows=PMAIN,
                   side=hooks.gather_start())
    dtraw, got = mm(xn, win_dt, name="in_proj_dt", out_dtype=F32, tb=True, side=hooks.gather_pass_on(got))
    w = hooks.late_weights(w, got)
    wup = _perm_ffn_rows(w["ffn_w_up_t"])
    ya_pre = _gmlp_fwd(proj, w["gmlp_ln_w"], w["gmlp_ln_b"], w["gmlp_ws"], bst)
    yb_pre, y_ssd, sprev, ssm_xc = _ssd_fwd(proj, dtraw, w["ssm_conv_w"], scb, dtb, alog, dexp, snw)
    merged, ya, yb = _merge_fwd(ya_pre, yb_pre, proj, w["gate_bias"], w["w_proj_a"], w["w_proj_b"])
    h1 = _matmul(merged, w["w_out"], name="out_proj", out_dtype=F32, add=x)
    hn = _rms_fwd(h1, ffnw, name="ffn_norm")
    up = _matmul(hn, wup, name="ffn_up", out_dtype=BF16, tb=True, tn=2 * FT, j_outer=True)
    act, ffn_xc = _ffn_act_fwd(up, fcw, fcb)
    h2 = _matmul(act, w["ffn_w_down"], name="ffn_down", out_dtype=F32, tk=DFF, add=h1)

    loss_row, dh2, d_finw = _loss_head(h2, tgt, finw)
    dact = _matmul(dh2, w["ffn_w_down"], name="ffn_down_dx", out_dtype=BF16, tb=True, tn=DFF)
    d_wdown = _matmul(act, dh2, name="ffn_down_dw", out_dtype=WGRAD, ta=True, tm=FT, tk=2048)
    dup, d_fcw, d_fcb = _ffn_act_bwd(up, ffn_xc, dact, fcw)
    dhn = _matmul(dup, wup, name="ffn_up_dx", out_dtype=F32, tk=2 * FT)
    d_wup = _matmul(dup, hn, name="ffn_up_dw", out_dtype=WGRAD, ta=True, tm=FT, tk=2048,
                    o_row=lambda i: (i % 2) * 2 + i // 2)
    dh1, d_ffnw = _rms_bwd(h1, ffnw, dhn, dh2, name="ffn_norm_bwd")
    dmerged = _matmul(dh1, w["w_out"], name="out_proj_dx", out_dtype=BF16, tb=True)
    d_wout = _matmul(merged, dh1, name="out_proj_dw", out_dtype=WGRAD, ta=True, tk=2048)
    dproj, dya, dyb, dya_pre, dyb_pre, d_gbias = _merge_bwd(dmerged, proj, w["gate_bias"], ya, yb,
                                                           w["w_proj_a"], w["w_proj_b"])
    d_wpa = _matmul(ya_pre, dya, name="proj_a_dw", out_dtype=WGRAD, ta=True, tk=2048)
    d_wpb = _matmul(yb_pre, dyb, name="proj_b_dw", out_dtype=WGRAD, ta=True, tk=2048)
    dproj, d_lnw, d_lnb, d_ws, d_bst = _gmlp_bwd(proj, dya_pre, dproj, w["gmlp_ln_w"], w["gmlp_ln_b"],
                                                 w["gmlp_ws"], bst)
    dproj, ddt, d_scw, d_scb, d_dtb, d_a, d_dch, d_snw = _ssd_bwd(
        proj, ssm_xc, dtraw, y_ssd, sprev, dyb_pre, dproj, w["ssm_conv_w"], dtb, alog, dexp, snw)
    late = {"w_proj_a": d_wpa, "w_proj_b": d_wpb, "w_out": d_wout, "ffn_w_up_t": d_wup,
            "ffn_w_down": d_wdown}
    gap = (2 * D + 2 * GW, DP_GAP)
    d_win_main, late_outs = mm(dproj, xn, name="in_proj_dw", out_dtype=WGRAD, ta=True, a_gap=gap, tk=2048,
                               side=hooks.reduce_late(late))
    d_win_dt = _matmul(ddt, xn, name="in_proj_dt_dw", out_dtype=F32, ta=True)
    d_win_t = jnp.concatenate([d_win_main, d_win_dt[:SH]], axis=0)
    dxn, got = mm(ddt, win_dt, name="in_proj_dt_dx", out_dtype=F32,
                  side=hooks.w_in_to_sibling(d_win_main, d_win_dt[:SH]))
    dxn, w_in_outs = mm(dproj, win_t, name="in_proj_dx", out_dtype=F32, add=dxn, b_rows=PMAIN, a_gap=gap,
                        side=hooks.reduce_w_in(got))
    hooks.reduced(late_outs, w_in_outs)
    grad_x, d_mixw = _rms_bwd(x, mixw, dxn, dh1, name="mix_norm_bwd")

    a_neg = -jnp.exp(w["ssm_a_log"])
    grads = {
        "mix_norm_w": d_mixw[0],
        "w_in_t": d_win_t,
        "gate_bias": d_gbias,
        "gmlp_ln_w": d_lnw, "gmlp_ln_b": d_lnb, "gmlp_ws": d_ws, "gmlp_bs": d_bst[:, :GG].T,
        "ssm_conv_w": d_scw, "ssm_conv_b": d_scb[0],
        "ssm_dt_bias": d_dtb[0, :SH], "ssm_a_log": d_a[0, :SH] * a_neg,
        "ssm_d": d_dch.reshape(SH, SP).sum(axis=-1), "ssm_norm_w": d_snw[0],
        **late,
        "ffn_norm_w": d_ffnw[0],
        "ffn_conv_w": _perm_ffn_cols(d_fcw), "ffn_conv_b": _perm_ffn_cols(d_fcb)[0],
        "ffn_w_down": d_wdown, "final_norm_w": d_finw[0],
    }
    return loss_row, grad_x, grads


MESH = pl.DeviceIdType.MESH
HBM_SPEC = pl.BlockSpec(memory_space=pltpu.HBM)


def _axes():
    return lax.axis_index("x"), lax.axis_index("y"), lax.axis_index("c")


def _all_gather(shards, *, name):
    na = len(shards)

    def body(*refs):
        x_refs, out_refs = refs[:na], refs[na:2 * na]
        send_sems, recv_sems, local_sems = refs[2 * na:]
        x, y, c = _axes()
        me, sibling = (x, y, c), (x, y, 1 - c)
        chips = [(1 - x, y), (x, 1 - y), (1 - x, 1 - y)]

        def slot(a, px, py, pc):
            return out_refs[a].at[4 * px + 2 * py + pc]

        def copy(a, k, block, to, src=None):
            return pltpu.make_async_remote_copy(
                src_ref=slot(a, *block) if src is None else src, dst_ref=slot(a, *block),
                send_sem=send_sems.at[7 * a + k], recv_sem=recv_sems.at[7 * a + k], device_id=to, device_id_type=MESH)

        mine = [pltpu.make_async_copy(x_refs[a], slot(a, *me), local_sems.at[a]) for a in range(na)]
        for cp in mine:
            cp.start()
        first = []
        for a in range(na):
            first.append(copy(a, 0, me, sibling, src=x_refs[a]))
            first += [copy(a, 1 + j, me, (*chip, c), src=x_refs[a]) for j, chip in enumerate(chips)]
        for cp in first:
            cp.start()
        passed = []
        for j, chip in enumerate(chips):
            for a in range(na):
                copy(a, 1 + j, (*chip, c), me).wait_recv()
                cp = copy(a, 4 + j, (*chip, c), sibling)
                cp.start()
                passed.append(cp)
        for a in range(na):
            copy(a, 0, sibling, me).wait_recv()
        for j, chip in enumerate(chips):
            for a in range(na):
                copy(a, 4 + j, (*chip, 1 - c), me).wait_recv()
        for cp in first + passed:
            cp.wait_send()
        for cp in mine:
            cp.wait()

    return pl.pallas_call(
        body, out_shape=[jax.ShapeDtypeStruct((NDEV,) + s.shape, s.dtype) for s in shards],
        in_specs=[HBM_SPEC] * na, out_specs=[HBM_SPEC] * na,
        scratch_shapes=[pltpu.SemaphoreType.DMA((7 * na,)), pltpu.SemaphoreType.DMA((7 * na,)),
                        pltpu.SemaphoreType.DMA((na,))],
        name=name)(*shards)


def _exchange(srcs, plan, *, name):
    na = len(srcs)
    n = len(plan(0, 0, 0))

    def body(*refs):
        src_refs, out_refs = refs[:na], refs[na:2 * na]
        send_sems, recv_sems = refs[2 * na:]
        x, y, c = _axes()
        copies = []
        for k, (slab, peer) in enumerate(plan(x, y, c)):
            for a in range(na):
                cp = pltpu.make_async_remote_copy(
                    src_ref=src_refs[a].at[slab], dst_ref=out_refs[a].at[k], send_sem=send_sems.at[n * a + k],
                    recv_sem=recv_sems.at[n * a + k], device_id=peer, device_id_type=MESH)
                cp.start()
                copies.append(cp)
        for cp in copies:
            cp.wait()

    return pl.pallas_call(
        body, out_shape=[jax.ShapeDtypeStruct((n,) + s.shape[1:], s.dtype) for s in srcs],
        in_specs=[HBM_SPEC] * na, out_specs=[HBM_SPEC] * na,
        scratch_shapes=[pltpu.SemaphoreType.DMA((n * na,)), pltpu.SemaphoreType.DMA((n * na,))], name=name)(*srcs)


def _to_sibling_plan(x, y, c):
    return [(2 * q + (1 - c), (x, y, 1 - c)) for q in range(4)]


def _to_chips_plan(x, y, c):
    q = 2 * x + y
    return [(q ^ 2, (1 - x, y, c)), (q ^ 1, (x, 1 - y, c)), (q ^ 3, (1 - x, 1 - y, c))]


def _row_tile(rows, row_bytes, budget=2 * 2 ** 20, align=2 * SUBLANE):
    if rows * row_bytes <= 2 * budget:
        return rows
    best = None
    for d in range(align, rows + 1, align):
        if rows % d == 0 and d * row_bytes <= budget:
            best = d
    return best or rows


def _pair_add(g, ra, c_idx, *, name):
    _, _, R, C = g.shape
    tr = _row_tile(R, C * 4, budget=3 * 2 ** 20)

    def body(c_ref, g_ref, ra_ref, o_ref):
        del c_ref
        o_ref[...] = (g_ref[0].astype(F32) + ra_ref[...].astype(F32)).astype(o_ref.dtype)

    return pl.pallas_call(
        body,
        grid_spec=pltpu.PrefetchScalarGridSpec(
            num_scalar_prefetch=1, grid=(4, R // tr),
            in_specs=[pl.BlockSpec((1, 1, tr, C), lambda q, r, cr: (q, cr[0], r, 0)),
                      pl.BlockSpec((1, tr, C), lambda q, r, cr: (q, r, 0))],
            out_specs=pl.BlockSpec((1, tr, C), lambda q, r, cr: (q, r, 0))),
        out_shape=jax.ShapeDtypeStruct((4, R, C), g.dtype), name=name,
        compiler_params=_params(("arbitrary", "arbitrary"), 24))(c_idx, g, ra)


def _grad_sum(p, rb, q_idx, *, name):
    _, R, C = p.shape
    tr = _row_tile(R, C * 4, budget=3 * 2 ** 20)

    def body(q_ref, p_ref, rb_ref, o_ref):
        del q_ref
        g = p_ref[0].astype(F32)
        for k in range(3):
            g = g + rb_ref[k].astype(F32)
        o_ref[...] = g

    return pl.pallas_call(
        body,
        grid_spec=pltpu.PrefetchScalarGridSpec(
            num_scalar_prefetch=1, grid=(R // tr,),
            in_specs=[pl.BlockSpec((1, tr, C), lambda r, qr: (qr[0], r, 0)),
                      pl.BlockSpec((3, tr, C), lambda r, qr: (0, r, 0))],
            out_specs=pl.BlockSpec((tr, C), lambda r, qr: (r, 0))),
        out_shape=jax.ShapeDtypeStruct((R, C), F32), name=name,
        compiler_params=_params(("arbitrary",), 40))(q_idx, p, rb)


def _adamw(g, w, m, v):
    m = ADAM_B1 * m + (1.0 - ADAM_B1) * g
    v = ADAM_B2 * v + (1.0 - ADAM_B2) * (g * g)
    m_hat = m / (1.0 - ADAM_B1 ** ADAM_STEP)
    v_hat = v / (1.0 - ADAM_B2 ** ADAM_STEP)
    delta = -ADAM_LR * (m_hat / (jnp.sqrt(v_hat) + ADAM_EPS) + ADAM_WD * w)
    return delta, m, v


def _adam(g, w, m, v, *, name):
    _, R, C = w.shape
    tr = _row_tile(R, C * 4, budget=2 ** 20, align=SUBLANE)

    def body(g_ref, w_ref, m_ref, v_ref, d_out, m_out, v_out):
        delta, mn, vn = _adamw(g_ref[...], w_ref[...], m_ref[...], v_ref[...])
        d_out[...] = delta
        m_out[...] = mn
        v_out[...] = vn

    row = pl.BlockSpec((1, tr, C), lambda r: (0, r, 0))
    o = jax.ShapeDtypeStruct((1, R, C), F32)
    return pl.pallas_call(
        body, grid=(R // tr,), in_specs=[row, row, row, row], out_specs=[row, row, row], out_shape=[o, o, o],
        name=name, compiler_params=_params(("arbitrary",), 32))(g, w, m, v)


def _vmem_specs(n):
    return [pl.BlockSpec(memory_space=pltpu.VMEM)] * n


def _pair_sum_many(mine, theirs, *, name):
    n = len(mine)

    def body(*refs):
        for a in range(n):
            refs[2 * n + a][...] = refs[a][...] + refs[n + a][0]

    return pl.pallas_call(
        body, out_shape=[jax.ShapeDtypeStruct(m.shape, m.dtype) for m in mine], in_specs=_vmem_specs(2 * n),
        out_specs=_vmem_specs(n), name=name)(*mine, *theirs)


def _chip_sum_many(own, recv, q_idx, *, name):
    n = len(own)

    def body(q_ref, *refs):
        q = q_ref[0]
        for a in range(n):
            mine, r = refs[a][...], refs[n + a]
            total = None
            for chip in range(4):
                e = q ^ chip
                term = jnp.where(e == 0, mine, jnp.where(e == 2, r[0], jnp.where(e == 1, r[1], r[2]))).astype(F32)
                total = term if total is None else total + term
            refs[2 * n + a][...] = total

    return pl.pallas_call(
        body, out_shape=[jax.ShapeDtypeStruct(m.shape, F32) for m in own],
        in_specs=[pl.BlockSpec(memory_space=pltpu.SMEM)] + _vmem_specs(2 * n), out_specs=_vmem_specs(n),
        name=name)(q_idx, *own, *recv)


def _adam_many(gs, ws, ms, vs, *, name):
    n = len(gs)

    def body(*refs):
        for a in range(n):
            delta, mn, vn = _adamw(*(refs[k * n + a][...] for k in range(4)))
            refs[4 * n + a][...] = delta
            refs[5 * n + a][...] = mn
            refs[6 * n + a][...] = vn

    shapes = [jax.ShapeDtypeStruct(w.shape, w.dtype) for w in ws]
    out = pl.pallas_call(body, out_shape=shapes * 3, in_specs=_vmem_specs(4 * n), out_specs=_vmem_specs(3 * n),
                         name=name)(*gs, *ws, *ms, *vs)
    return out[:n], out[n:2 * n], out[2 * n:]


WEIGHTS = ["mix_norm_w", "w_in", "gate_bias", "gmlp_ln_w", "gmlp_ln_b", "gmlp_ws", "gmlp_bs", "ssm_conv_w",
           "ssm_conv_b", "ssm_dt_bias", "ssm_a_log", "ssm_d", "ssm_norm_w", "w_proj_a", "w_proj_b", "w_out",
           "ffn_norm_w", "ffn_w_up", "ffn_conv_w", "ffn_conv_b", "ffn_w_down", "final_norm_w"]
SHARDED = {"w_in": ((D, IN_COLS), 1), "gate_bias": ((2, D), 1), "ssm_conv_w": ((SK, SXBC), 1),
           "w_proj_a": ((GW, D), 0), "w_proj_b": ((SI, D), 0), "w_out": ((D, D), 0),
           "ffn_w_up": ((D, 2 * DFF), 1), "ffn_conv_w": ((FK, 2 * DFF), 1), "ffn_w_down": ((DFF, D), 0)}
REPLICATED = {"mix_norm_w": (D,), "gmlp_ln_w": (GG, GD), "gmlp_ln_b": (GG, GD), "gmlp_ws": (GG, GB, GB),
              "gmlp_bs": (GG, GB), "ssm_conv_b": (SXBC,), "ssm_dt_bias": (SH,), "ssm_a_log": (SH,), "ssm_d": (SH,),
              "ssm_norm_w": (SI,), "ffn_norm_w": (D,), "ffn_conv_b": (2 * DFF,), "final_norm_w": (D,)}
REPL_ORDER = [n for n in WEIGHTS if n in REPLICATED]
BTILE = 2 * SUBLANE
WIN_R = IN_COLS // NDEV
WIN_P = WIN_R + BTILE - WIN_R % BTILE
WIN_A = [WIN_R * d // BTILE * BTILE for d in range(NDEV)]
assert all(WIN_A[d] + WIN_P >= WIN_R * (d + 1) for d in range(NDEV)) and WIN_A[-1] + WIN_P == IN_COLS
BIG = [("w_proj_a", GW // NDEV, False), ("w_proj_b", SI // NDEV, False), ("w_out", D // NDEV, False),
       ("ffn_w_up", 2 * DFF // NDEV, True), ("ffn_w_down", DFF // NDEV, False), ("w_in", WIN_P, True)]
VECTORS = ["gate_bias", "ssm_conv_w", "ffn_conv_w"]


def _round_up(n, k):
    return (n + k - 1) // k * k


BIG_OFF = {}
_off = 0
for _n, _r, _t in BIG:
    BIG_OFF[_n] = _off
    _off += _r
BIG_USED = _off
BIG_ROWS = _round_up(BIG_USED, 2 * SUBLANE)
assert all(BIG_OFF[n] % (2 * SUBLANE) == 0 for n, _, _ in BIG)
VEC_SHAPE = {n: (SHARDED[n][0][0], SHARDED[n][0][1] // NDEV) for n in VECTORS}


def _win_offset(dev):
    return WIN_R * dev - WIN_R * dev // BTILE * BTILE


def _pack_big(arrs, dtype, dev):
    parts = []
    for n, r, t in BIG:
        a = (arrs[n].T if t else arrs[n]).astype(dtype)
        if n == "w_in":
            a = lax.dynamic_update_slice(jnp.zeros((WIN_P, D), dtype), a, (_win_offset(dev), 0))
        parts.append(a)
    parts.append(jnp.zeros((BIG_ROWS - BIG_USED, D), dtype))
    return jnp.concatenate(parts, axis=0)


def _join_windows(win):
    parts = []
    for d in range(NDEV):
        lo = BTILE if WIN_A[d] % WIN_R else 0
        if lo:
            parts.append(win[d - 1, WIN_P - BTILE:] + win[d, :BTILE])
        hi = WIN_P - BTILE if d + 1 < NDEV and WIN_A[d + 1] < WIN_A[d] + WIN_P else WIN_P
        parts.append(win[d, lo:hi])
    return jnp.concatenate(parts, axis=0)


def _split_windows(main, last):
    assert WIN_A[-2] + WIN_P <= PMAIN
    wins = [main[a:a + WIN_P] for a in WIN_A[:-1]]
    return jnp.stack(wins + [jnp.concatenate([main[WIN_A[-1]:], last], axis=0)])


LATE_ROWS = BIG_OFF["w_in"]
assert LATE_ROWS + WIN_P == BIG_ROWS and BIG[-1][0] == "w_in"


def _remote(src, dst, send_sems, recv_sems, k, to):
    return pltpu.make_async_remote_copy(src_ref=src, dst_ref=dst, send_sem=send_sems.at[k], recv_sem=recv_sems.at[k],
                                        device_id=to, device_id_type=MESH)


class _Exchange:
    def __init__(self, late_shard, c_idx):
        self.late_shard, self.c_idx = late_shard, c_idx

    def gather_start(self):
        shard = self.late_shard

        def make(ins, outs, send_sems, recv_sems):
            (x_ref,), (out,) = ins, outs
            x, y, c = _axes()
            mine = out.at[4 * x + 2 * y + c]
            peers = [(x, y, 1 - c), (1 - x, y, c), (x, 1 - y, c), (1 - x, 1 - y, c)]
            copies = [_remote(x_ref, mine, send_sems, recv_sems, k, p) for k, p in enumerate(peers)]
            return copies + [pltpu.make_async_copy(x_ref, mine, send_sems.at[len(peers)])]

        return _Side([shard], [jax.ShapeDtypeStruct((NDEV,) + shard.shape, shard.dtype)], 5, make)

    def gather_pass_on(self, outs):
        (buf,) = outs

        def make(ins, outs, send_sems, recv_sems):
            (src,), (dst,) = ins, outs
            x, y, c = _axes()
            slots = [4 * px + 2 * py + c for px, py in [(1 - x, y), (x, 1 - y), (1 - x, 1 - y)]]
            return [_remote(src.at[s], dst.at[s], send_sems, recv_sems, k, (x, y, 1 - c)) for k, s in enumerate(slots)]

        return _Side([buf], [jax.ShapeDtypeStruct(buf.shape, buf.dtype)], 3, make, aliases=[(0, 0)])

    def late_weights(self, w, outs):
        (buf,) = outs
        w = dict(w)
        for n, r, t in BIG[:-1]:
            w[n + "_t" if t else n] = buf[:, BIG_OFF[n]:BIG_OFF[n] + r].reshape(NDEV * r, D)
        return w

    @staticmethod
    def _plan_side(src, plan):
        n = len(plan(0, 0, 0))

        def make(ins, outs, send_sems, recv_sems):
            (s,), (dst,) = ins, outs
            return [_remote(s.at[slab], dst.at[k], send_sems, recv_sems, k, peer)
                    for k, (slab, peer) in enumerate(plan(*_axes()))]

        return _Side([src], [jax.ShapeDtypeStruct((n,) + src.shape[1:], src.dtype)], n, make)

    def _to_chips(self, send, sib, tag):
        sums = _pair_add(send.reshape((4, 2) + send.shape[1:]), sib, self.c_idx, name=tag + "_grad_pair_add")
        return sums, self._plan_side(sums, _to_chips_plan)

    def reduce_late(self, grads):
        send = jnp.concatenate([grads[n + "_t" if t else n].reshape(NDEV, r, D) for n, r, t in BIG[:-1]], axis=1)
        send = send.astype(BF16)
        (sib,) = _exchange([send], _to_sibling_plan, name="late_grads_to_sibling")
        self.late_sum, side = self._to_chips(send, sib, "late")
        return side

    def w_in_to_sibling(self, grad_main, grad_dt):
        self.w_in_send = _split_windows(grad_main.astype(BF16), grad_dt.astype(BF16))
        return self._plan_side(self.w_in_send, _to_sibling_plan)

    def reduce_w_in(self, outs):
        self.w_in_sum, side = self._to_chips(self.w_in_send, outs[0], "w_in")
        return side

    def reduced(self, late_outs, w_in_outs):
        (self.late_from_chips,), (self.w_in_from_chips,) = late_outs, w_in_outs


def kernel(x, mix_norm_w, w_in, gate_bias, gmlp_ln_w, gmlp_ln_b, gmlp_ws, gmlp_bs, ssm_conv_w, ssm_conv_b, ssm_dt_bias, ssm_a_log, ssm_d, ssm_norm_w, w_proj_a, w_proj_b, w_out, ffn_norm_w, ffn_w_up, ffn_conv_w, ffn_conv_b, ffn_w_down, final_norm_w, loss_target, m_mix_norm_w, m_w_in, m_gate_bias, m_gmlp_ln_w, m_gmlp_ln_b, m_gmlp_ws, m_gmlp_bs, m_ssm_conv_w, m_ssm_conv_b, m_ssm_dt_bias, m_ssm_a_log, m_ssm_d, m_ssm_norm_w, m_w_proj_a, m_w_proj_b, m_w_out, m_ffn_norm_w, m_ffn_w_up, m_ffn_conv_w, m_ffn_conv_b, m_ffn_w_down, m_final_norm_w, v_mix_norm_w, v_w_in, v_gate_bias, v_gmlp_ln_w, v_gmlp_ln_b, v_gmlp_ws, v_gmlp_bs, v_ssm_conv_w, v_ssm_conv_b, v_ssm_dt_bias, v_ssm_a_log, v_ssm_d, v_ssm_norm_w, v_w_proj_a, v_w_proj_b, v_w_out, v_ffn_norm_w, v_ffn_w_up, v_ffn_conv_w, v_ffn_conv_b, v_ffn_w_down, v_final_norm_w):
    given = dict(locals())
    wts = {n: given[n] for n in WEIGHTS}
    mom = {n: given["m_" + n] for n in WEIGHTS}
    var = {n: given["v_" + n] for n in WEIGHTS}
    xi, yi, ci = _axes()
    c_idx = jnp.reshape(ci, (1,)).astype(jnp.int32)
    q_idx = jnp.reshape(2 * xi + yi, (1,)).astype(jnp.int32)
    big_names = [n for n, _, _ in BIG]
    drop = lambda d, names: {n: d[n][0] for n in names}

    dev = 4 * xi + 2 * yi + ci
    packed = _pack_big(drop(wts, big_names), BF16, dev)
    gathered = _all_gather([packed[LATE_ROWS:]] + [wts[n] for n in VECTORS], name="w_in_all_gather")
    full = {"w_in_t": _join_windows(gathered[0])}
    for n, a in zip(VECTORS, gathered[1:]):
        r, c = VEC_SHAPE[n]
        full[n] = a[:, 0].transpose(1, 0, 2).reshape(r, NDEV * c)
    for n in REPL_ORDER:
        full[n] = wts[n].reshape(REPLICATED[n])

    hooks = _Exchange(packed[:LATE_ROWS], c_idx)
    loss_local, grad_x, grads = _local_step(x[0], loss_target[0], full, hooks)
    g_late = _grad_sum(hooks.late_sum, hooks.late_from_chips, q_idx, name="late_grad_sum")
    g_win = _grad_sum(hooks.w_in_sum, hooks.w_in_from_chips, q_idx, name="w_in_grad_sum")

    small = VECTORS + REPL_ORDER
    as_2d = lambda a: a if a.ndim >= 2 else a[None]
    part = [grads[n].reshape((1,) + SHARDED[n][0] if n in VECTORS else as_2d(wts[n]).shape) for n in small]
    part.append(loss_local)
    from_sibling = _exchange([p[None] for p in part], lambda x, y, c: [(0, (x, y, 1 - c))],
                             name="small_grads_to_sibling")
    chip_sums = _pair_sum_many(part, from_sibling, name="small_grad_pair_sum")
    chip_sums = [s.astype(BF16) if s.size >= SMALL_BF16_FROM else s for s in chip_sums]
    from_chips = _exchange([s[None] for s in chip_sums],
                           lambda x, y, c: [(0, (1 - x, y, c)), (0, (x, 1 - y, c)), (0, (1 - x, 1 - y, c))],
                           name="small_grads_to_chips")
    totals = _chip_sum_many(chip_sums, from_chips, q_idx, name="small_grad_chip_sum")
    g_small, loss = dict(zip(small, totals)), totals[-1][0, 0]
    for n in VECTORS:
        c = VEC_SHAPE[n][1]
        g_small[n] = lax.dynamic_slice_in_dim(g_small[n], dev * c, c, axis=2)

    outs = {}
    small_g = [g_small[n] for n in small]
    small_out = _adam_many(small_g, *[[as_2d(d[n]) for n in small] for d in (wts, mom, var)], name="adam_small")
    for i, n in enumerate(small):
        outs[n] = tuple(a[i].reshape(wts[n].shape) for a in (small_g,) + tuple(small_out))
    for n, r, t in BIG:
        if n == "w_in":
            g = lax.dynamic_slice(g_win, (_win_offset(dev), 0), (WIN_R, D))
        else:
            g = g_late[BIG_OFF[n]:BIG_OFF[n] + r]
        flip = (lambda a: a.transpose(0, 2, 1)) if t else (lambda a: a)
        g = g[None]
        new = _adam(g, flip(wts[n]), flip(mom[n]), flip(var[n]), name="adam_" + n)
        outs[n] = tuple(flip(a) for a in (g,) + tuple(new))
    return (loss, grad_x[None]) + tuple(outs[n][k] for k in range(4) for n in WEIGHTS)
```

```python
import functools

import jax
import jax.numpy as jnp
from jax import lax
from jax.experimental import pallas as pl
from jax.experimental.pallas import tpu as pltpu

F32 = jnp.float32
BF16 = jnp.bfloat16

D = 1024
EPS = 1e-5
GW = 1024
GB = 128
GG = 8
GD = 128
GCH = 64
SI = 2048
SH = 32
SP = 64
SG = 4
SN = 128
SGW = SI // SG
SK = 4
SXBC = SI + 2 * SG * SN
DFF = 2816
FK = 3
PMAIN = 2 * D + 2 * GW + SI + SXBC
IN_COLS = PMAIN + SH
DP_SSM = SI + SXBC
DP_GAP = (DP_SSM - (2 * D + 2 * GW) % DP_SSM) % DP_SSM
DP_COLS = 2 * D + 2 * GW + DP_GAP + DP_SSM
assert DP_GAP % D == 0 and (2 * D + 2 * GW) % D == 0
NDEV = 8
ADAM_LR, ADAM_B1, ADAM_B2, ADAM_EPS, ADAM_WD, ADAM_STEP = 0.001, 0.9, 0.999, 1e-08, 0.01, 10

LANE = 128
SUBLANE = 8
VMEM_MB_V7X = 64
VMEM_CAP_MB = VMEM_MB_V7X - 8

LS = 128
FT = DFF // 2

NN = (((1,), (0,)), ((), ()))
NT = (((1,), (1,)), ((), ()))
TN = (((0,), (0,)), ((), ()))


def _params(sem, vmem_mb):
    return pltpu.CompilerParams(dimension_semantics=sem,
                                vmem_limit_bytes=min(int(vmem_mb), VMEM_CAP_MB) * 1024 * 1024)


def _dot(a, b, dims=NN):
    return lax.dot_general(a, b, dims, preferred_element_type=F32)


def _sigmoid(x):
    return 1.0 / (1.0 + jnp.exp(-x))


def _split3(v):
    hi = v.astype(BF16)
    r = v - hi.astype(F32)
    mid = r.astype(BF16)
    lo = (r - mid.astype(F32)).astype(BF16)
    return hi, mid, lo


def _dot3(a_f32, b_bf16, dims):
    hi, mid, lo = _split3(a_f32)
    return _dot(hi, b_bf16, dims) + _dot(mid, b_bf16, dims) + _dot(lo, b_bf16, dims)


def _dot2(a_f32, b_bf16, dims):
    hi, mid, _ = _split3(a_f32)
    return _dot(hi, b_bf16, dims) + _dot(mid, b_bf16, dims)


def _dot3_rhs(a_bf16, b_f32, dims):
    hi, mid, lo = _split3(b_f32)
    return _dot(a_bf16, hi, dims) + _dot(a_bf16, mid, dims) + _dot(a_bf16, lo, dims)


def _matmul(a, b, *, name, out_dtype, ta=False, tb=False, tm=1024, tn=1024, tk=1024, add=None,
            j_outer=False, b_rows=None, a_gap=None, o_row=None, side=None):
    assert not (o_row and add is not None)
    gap0, gapw = a_gap or (0, 0)
    if ta:
        K, M = a.shape
        M -= gapw
    else:
        M, K = a.shape
        K -= gapw
    if tb:
        N, K2 = b.shape
        N = b_rows or N
    else:
        K2, N = b.shape
        K2 = b_rows or K2
    assert K == K2, (a.shape, b.shape, ta, tb)
    tm, tn, tk = min(tm, M), min(tn, N), min(tk, K)
    assert M % tm == 0 and N % tn == 0 and K % tk == 0, (M, N, K, tm, tn, tk)
    nk = K // tk
    dims = (((0 if ta else 1,), (1 if tb else 0,)), ((), ()))
    has_add = add is not None
    n_in = 3 if has_add else 2
    s_in = len(side.inputs) if side else 0
    s_out = len(side.out_shapes) if side else 0
    grid = (N // tn, M // tm, nk) if j_outer else (M // tm, N // tn, nk)

    def body(*refs):
        a_ref, b_ref = refs[:2]
        add_ref = refs[2] if has_add else None
        o_ref = refs[n_in + s_in]
        if side:
            side_refs = (refs[n_in:n_in + s_in], refs[n_in + s_in + 1:n_in + s_in + 1 + s_out]) + tuple(refs[-2:])
            ids = [pl.program_id(d) for d in range(3)]
            first = functools.reduce(jnp.logical_and, [i == 0 for i in ids])
            last = functools.reduce(jnp.logical_and, [i == g - 1 for i, g in zip(ids, grid)])

            @pl.when(first)
            def _():
                for cp in side.make(*side_refs):
                    cp.start()

            @pl.when(last)
            def _():
                for cp in side.make(*side_refs):
                    cp.wait()

        p = lax.dot_general(a_ref[...].astype(BF16), b_ref[...].astype(BF16), dims,
                            preferred_element_type=F32)

        def finish(acc):
            if has_add:
                acc = acc + add_ref[...].astype(F32)
            o_ref[...] = acc.astype(o_ref.dtype)

        if nk == 1:
            finish(p)
        else:
            acc_ref = refs[n_in + s_in + 1 + s_out]
            k = pl.program_id(2)

            @pl.when(k == 0)
            def _():
                acc_ref[...] = p

            @pl.when(jnp.logical_and(k > 0, k < nk - 1))
            def _():
                acc_ref[...] += p

            @pl.when(k == nk - 1)
            def _():
                finish(acc_ref[...] + p)

    if j_outer:
        ij = lambda g0, g1: (g1, g0)
    else:
        ij = lambda g0, g1: (g0, g1)

    ta_col = tm if ta else tk
    assert gap0 % ta_col == 0 and gapw % ta_col == 0, (a_gap, ta_col)

    def a_map(g0, g1, k):
        i, _ = ij(g0, g1)
        col = i if ta else k
        col = col + jnp.where(col >= gap0 // ta_col, gapw // ta_col, 0) if gapw else col
        return (k, col) if ta else (i, col)

    def b_map(g0, g1, k):
        _, j = ij(g0, g1)
        return (j, k) if tb else (k, j)

    def o_map(g0, g1, k):
        i, j = ij(g0, g1)
        return (o_row(i) if o_row else i, j)

    in_specs = [pl.BlockSpec((tk, tm) if ta else (tm, tk), a_map),
                pl.BlockSpec((tn, tk) if tb else (tk, tn), b_map)]
    args = [a, b]
    if has_add:
        in_specs.append(pl.BlockSpec((tm, tn), o_map))
        args.append(add)
    scratch = [pltpu.VMEM((tm, tn), F32)] if nk > 1 else []
    osz = jnp.dtype(out_dtype).itemsize
    est = (2 * (tm * tk * a.dtype.itemsize + tk * tn * b.dtype.itemsize) + 2 * tm * tn * osz
           + (2 * tm * tn * add.dtype.itemsize if has_add else 0)
           + 3 * tm * tn * 4 + 2 * (tm * tk + tk * tn)) / 2 ** 20 + 4
    out_specs = [pl.BlockSpec((tm, tn), o_map)]
    out_shape = [jax.ShapeDtypeStruct((M, N), out_dtype)]
    aliases = {}
    if side:
        hbm = pl.BlockSpec(memory_space=pltpu.HBM)
        in_specs += [hbm] * s_in
        args += list(side.inputs)
        out_specs += [hbm] * s_out
        out_shape += list(side.out_shapes)
        scratch += [pltpu.SemaphoreType.DMA((side.nsem,)), pltpu.SemaphoreType.DMA((side.nsem,))]
        aliases = {n_in + i: 1 + j for i, j in side.aliases}
    outs = pl.pallas_call(
        body, grid=grid, in_specs=in_specs, out_specs=out_specs, out_shape=out_shape, scratch_shapes=scratch,
        input_output_aliases=aliases, name=name,
        compiler_params=_params(("arbitrary", "arbitrary", "arbitrary"), est))(*args)
    return (outs[0], list(outs[1:])) if side else outs[0]


class _Side:
    def __init__(self, inputs, out_shapes, nsem, make, aliases=()):
        self.inputs, self.out_shapes, self.nsem, self.make, self.aliases = inputs, out_shapes, nsem, make, aliases


def _rms_fwd(x, w, *, name):
    T = x.shape[0]
    tm = min(512, T)

    def body(x_ref, w_ref, o_ref):
        xv = x_ref[...]
        r = lax.rsqrt(jnp.mean(xv * xv, axis=-1, keepdims=True) + EPS)
        o_ref[...] = (xv * r * w_ref[...]).astype(BF16)

    return pl.pallas_call(
        body, grid=(T // tm,),
        in_specs=[pl.BlockSpec((tm, D), lambda i: (i, 0)), pl.BlockSpec((1, D), lambda i: (0, 0))],
        out_specs=pl.BlockSpec((tm, D), lambda i: (i, 0)),
        out_shape=jax.ShapeDtypeStruct((T, D), BF16), name=name,
        compiler_params=_params(("arbitrary",), 24))(x, w)


def _rms_bwd(x, w, dy, dres, *, name):
    T = x.shape[0]
    tm = min(512, T)

    def body(x_ref, w_ref, dy_ref, dres_ref, dx_ref, dw_ref):
        xv = x_ref[...]
        r = lax.rsqrt(jnp.mean(xv * xv, axis=-1, keepdims=True) + EPS)
        xhat = xv * r
        dyv = dy_ref[...].astype(F32)
        g = dyv * w_ref[...]
        dx_ref[...] = dres_ref[...] + r * (g - xhat * jnp.mean(g * xhat, axis=-1, keepdims=True))
        part = jnp.sum(dyv * xhat, axis=0, keepdims=True)

        @pl.when(pl.program_id(0) == 0)
        def _():
            dw_ref[...] = part

        @pl.when(pl.program_id(0) > 0)
        def _():
            dw_ref[...] += part

    row = pl.BlockSpec((tm, D), lambda i: (i, 0))
    vec = pl.BlockSpec((1, D), lambda i: (0, 0))
    return pl.pallas_call(
        body, grid=(T // tm,), in_specs=[row, vec, row, row], out_specs=[row, vec],
        out_shape=[jax.ShapeDtypeStruct((T, D), F32), jax.ShapeDtypeStruct((1, D), F32)], name=name,
        compiler_params=_params(("arbitrary",), 32))(x, w, dy, dres)


def _loss_head(h, tgt, w):
    T = h.shape[0]
    tm = min(512, T)

    def body(h_ref, t_ref, w_ref, loss_ref, dh_ref, dw_ref):
        hv = h_ref[...]
        r = lax.rsqrt(jnp.mean(hv * hv, axis=-1, keepdims=True) + EPS)
        xhat = hv * r
        wv = w_ref[...]
        err = xhat * wv - t_ref[...]
        lpart = 0.5 * jnp.sum(jnp.mean(err * err, axis=-1, keepdims=True), axis=0, keepdims=True)
        dy = err * (1.0 / D)
        g = dy * wv
        dh_ref[...] = r * (g - xhat * jnp.mean(g * xhat, axis=-1, keepdims=True))
        wpart = jnp.sum(dy * xhat, axis=0, keepdims=True)
        lrow = jnp.broadcast_to(lpart, (1, LANE))

        @pl.when(pl.program_id(0) == 0)
        def _():
            dw_ref[...] = wpart
            loss_ref[...] = lrow

        @pl.when(pl.program_id(0) > 0)
        def _():
            dw_ref[...] += wpart
            loss_ref[...] += lrow

    row = pl.BlockSpec((tm, D), lambda i: (i, 0))
    vec = pl.BlockSpec((1, D), lambda i: (0, 0))
    return pl.pallas_call(
        body, grid=(T // tm,), in_specs=[row, row, vec],
        out_specs=[pl.BlockSpec((1, LANE), lambda i: (0, 0)), row, vec],
        out_shape=[jax.ShapeDtypeStruct((1, LANE), F32), jax.ShapeDtypeStruct((T, D), F32),
                   jax.ShapeDtypeStruct((1, D), F32)], name="loss_head",
        compiler_params=_params(("arbitrary",), 32))(h, tgt, w)


_GELU_C = 0.7978845608028654
_GELU_A = 0.044715


def _gelu(x, with_grad=False):
    x2 = x * x
    cx = _GELU_C * x
    t = jnp.tanh(cx * (1.0 + _GELU_A * x2))
    h = 0.5 * (1.0 + t)
    if not with_grad:
        return x * h
    return x * h, h + 0.5 * cx * (1.0 - t * t) * (1.0 + 3.0 * _GELU_A * x2)


def _gmlp_mask():
    r = lax.broadcasted_iota(jnp.int32, (GB, GB), 0) // GCH
    c = lax.broadcasted_iota(jnp.int32, (GB, GB), 1) // GCH
    return c <= r


def _gmlp_fwd(proj, lnw, lnb, ws, bst):
    T = proj.shape[0]
    tm = min(512, T)
    nblk = tm // GB

    def body(u_ref, v_ref, lnw_ref, lnb_ref, ws_ref, bst_ref, o_ref):
        mask = _gmlp_mask()
        u = _gelu(u_ref[...].astype(F32))
        v = _gelu(v_ref[...].astype(F32))
        for g in range(GG):
            cs = slice(g * GD, (g + 1) * GD)
            vg = v[:, cs]
            mu = jnp.mean(vg, axis=-1, keepdims=True)
            vc = vg - mu
            var = jnp.mean(vc * vc, axis=-1, keepdims=True)
            vn = (vc * lax.rsqrt(var + EPS) * lnw_ref[g:g + 1, :] + lnb_ref[g:g + 1, :]).astype(BF16)
            wsg = jnp.where(mask, ws_ref[g], 0.0).astype(BF16)
            bcol = bst_ref[:, g:g + 1]
            for blk in range(nblk):
                rs = slice(blk * GB, (blk + 1) * GB)
                sv = _dot(wsg, vn[rs, :]) + bcol
                o_ref[rs, cs] = (u[rs, cs] * sv).astype(BF16)

    full = lambda shape: pl.BlockSpec(shape, lambda i: tuple(0 for _ in shape))
    return pl.pallas_call(
        body, grid=(T // tm,),
        in_specs=[pl.BlockSpec((tm, GW), lambda i: (i, 2)), pl.BlockSpec((tm, GW), lambda i: (i, 3)),
                  full((GG, GD)), full((GG, GD)), full((GG, GB, GB)), full((GB, GG))],
        out_specs=pl.BlockSpec((tm, GW), lambda i: (i, 0)),
        out_shape=jax.ShapeDtypeStruct((T, GW), BF16), name="gmlp_fwd",
        compiler_params=_params(("arbitrary",), 40))(proj, proj, lnw, lnb, ws, bst)


def _gmlp_bwd(proj, dya, dproj, lnw, lnb, ws, bst, side=None):
    T = proj.shape[0]
    tm = min(512, T)
    nblk = tm // GB
    n_in, n_out = 8, 5
    s_in = len(side.inputs) if side else 0
    s_out = len(side.out_shapes) if side else 0

    def outer(*refs):
        ins, side_ins = refs[:n_in], refs[n_in:n_in + s_in]
        outs = refs[n_in + s_in:n_in + s_in + n_out]
        if side:
            side_refs = (side_ins, refs[n_in + s_in + n_out:n_in + s_in + n_out + s_out]) + tuple(refs[-2:])

            @pl.when(pl.program_id(0) == 0)
            def _():
                for cp in side.make(*side_refs):
                    cp.start()

            @pl.when(pl.program_id(0) == T // tm - 1)
            def _():
                for cp in side.make(*side_refs):
                    cp.wait()

        body(*ins, *outs)

    def body(u_ref, v_ref, dya_ref, dproj_in, lnw_ref, lnb_ref, ws_ref, bst_ref,
             dz_ref, dlnw_ref, dlnb_ref, dws_ref, dbst_ref):
        del dproj_in
        first = pl.program_id(0) == 0

        @pl.when(first)
        def _():
            dlnw_ref[...] = jnp.zeros_like(dlnw_ref)
            dlnb_ref[...] = jnp.zeros_like(dlnb_ref)
            dws_ref[...] = jnp.zeros_like(dws_ref)
            dbst_ref[...] = jnp.zeros_like(dbst_ref)

        mask = _gmlp_mask()
        lane = lax.broadcasted_iota(jnp.int32, (GB, LANE), 1)
        ur = u_ref[...].astype(F32)
        vr = v_ref[...].astype(F32)
        u, gu = _gelu(ur, with_grad=True)
        v, gv = _gelu(vr, with_grad=True)
        dy = dya_ref[...].astype(F32)
        dbst = jnp.zeros((GB, LANE), F32)
        dlnw_rows, dlnb_rows = [], []
        for g in range(GG):
            cs = slice(g * GD, (g + 1) * GD)
            vg = v[:, cs]
            mu = jnp.mean(vg, axis=-1, keepdims=True)
            vc = vg - mu
            var = jnp.mean(vc * vc, axis=-1, keepdims=True)
            rstd = lax.rsqrt(var + EPS)
            xhat = vc * rstd
            lw = lnw_ref[g:g + 1, :]
            vn = (xhat * lw + lnb_ref[g:g + 1, :]).astype(BF16)
            wsg = jnp.where(mask, ws_ref[g], 0.0).astype(BF16)
            bcol = bst_ref[:, g:g + 1]
            dyg = dy[:, cs]
            ug = u[:, cs]
            dsv = dyg * ug
            dsv_b = dsv.astype(BF16)
            dws_g = jnp.zeros((GB, GB), F32)
            bsum = jnp.zeros((GB, 1), F32)
            dvn_parts = []
            for blk in range(nblk):
                rs = slice(blk * GB, (blk + 1) * GB)
                sv = _dot(wsg, vn[rs, :]) + bcol
                dz_ref[rs, cs] = (dyg[rs, :] * sv * gu[rs, cs]).astype(BF16)
                dws_g = dws_g + _dot(dsv_b[rs, :], vn[rs, :], NT)
                bsum = bsum + jnp.sum(dsv[rs, :], axis=-1, keepdims=True)
                dvn_parts.append(_dot(wsg, dsv_b[rs, :], TN))
            dvn = jnp.concatenate(dvn_parts, axis=0)
            dws_ref[g] += jnp.where(mask, dws_g, 0.0)
            dbst = dbst + jnp.where(lane == g, bsum, 0.0)
            dlnw_rows.append(jnp.sum(dvn * xhat, axis=0, keepdims=True))
            dlnb_rows.append(jnp.sum(dvn, axis=0, keepdims=True))
            dxh = dvn * lw
            dvg = rstd * (dxh - jnp.mean(dxh, axis=-1, keepdims=True)
                          - xhat * jnp.mean(dxh * xhat, axis=-1, keepdims=True))
            dz_ref[:, GW + g * GD:GW + (g + 1) * GD] = (dvg * gv[:, cs]).astype(BF16)
        dlnw_ref[...] += jnp.concatenate(dlnw_rows, axis=0)
        dlnb_ref[...] += jnp.concatenate(dlnb_rows, axis=0)
        dbst_ref[...] += dbst

    full = lambda shape: pl.BlockSpec(shape, lambda i: tuple(0 for _ in shape))
    hbm = pl.BlockSpec(memory_space=pltpu.HBM)
    outs = pl.pallas_call(
        outer, grid=(T // tm,),
        in_specs=[pl.BlockSpec((tm, GW), lambda i: (i, 2)), pl.BlockSpec((tm, GW), lambda i: (i, 3)),
                  pl.BlockSpec((tm, GW), lambda i: (i, 0)), pl.BlockSpec(memory_space=pl.ANY),
                  full((GG, GD)), full((GG, GD)), full((GG, GB, GB)), full((GB, GG))] + [hbm] * s_in,
        out_specs=[pl.BlockSpec((tm, 2 * GW), lambda i: (i, 1)), full((GG, GD)), full((GG, GD)),
                   full((GG, GB, GB)), full((GB, LANE))] + [hbm] * s_out,
        out_shape=[jax.ShapeDtypeStruct(dproj.shape, dproj.dtype), jax.ShapeDtypeStruct((GG, GD), F32),
                   jax.ShapeDtypeStruct((GG, GD), F32), jax.ShapeDtypeStruct((GG, GB, GB), F32),
                   jax.ShapeDtypeStruct((GB, LANE), F32)] + (list(side.out_shapes) if side else []),
        scratch_shapes=[pltpu.SemaphoreType.DMA((side.nsem,)), pltpu.SemaphoreType.DMA((side.nsem,))] if side else [],
        input_output_aliases={3: 0}, name="gmlp_bwd",
        compiler_params=_params(("arbitrary",), 48))(proj, proj, dya, dproj, lnw, lnb, ws, bst,
                                                     *(side.inputs if side else []))
    return tuple(outs[:n_out]), list(outs[n_out:])


def _merge_fwd(ya_pre, yb_pre, proj, bias, wpa, wpb):
    T = proj.shape[0]
    tm = min(512, T)

    def body(ya_ref, yb_ref, g_ref, b_ref, wpa_ref, wpb_ref, m_ref, oa_ref, ob_ref):
        ya = _dot(ya_ref[...], wpa_ref[...])
        yb = _dot(yb_ref[...], wpb_ref[...])
        g = g_ref[...].astype(F32)
        sa = _sigmoid(g[:, :D] + b_ref[0:1, :])
        sb = _sigmoid(g[:, D:] + b_ref[1:2, :])
        m_ref[...] = (sa * ya + sb * yb).astype(BF16)
        oa_ref[...] = ya.astype(BF16)
        ob_ref[...] = yb.astype(BF16)

    row = lambda w: pl.BlockSpec((tm, w), lambda i: (i, 0))
    full = lambda shape: pl.BlockSpec(shape, lambda i: tuple(0 for _ in shape))
    o = jax.ShapeDtypeStruct((T, D), BF16)
    return pl.pallas_call(
        body, grid=(T // tm,),
        in_specs=[row(GW), row(SI), row(2 * D), full((2, D)), full((GW, D)), full((SI, D))],
        out_specs=[row(D), row(D), row(D)], out_shape=[o, o, o], name="merge_fwd",
        compiler_params=_params(("arbitrary",), 40))(ya_pre, yb_pre, proj, bias, wpa, wpb)


def _merge_bwd(dm, proj, bias, ya, yb, wpa, wpb):
    T = proj.shape[0]
    tm = min(512, T)

    def body(dm_ref, g_ref, b_ref, ya_ref, yb_ref, wpa_ref, wpb_ref,
             dg_ref, dya_ref, dyb_ref, dpa_ref, dpb_ref, db_ref):
        dmv = dm_ref[...].astype(F32)
        g = g_ref[...].astype(F32)
        sa = _sigmoid(g[:, :D] + b_ref[0:1, :])
        sb = _sigmoid(g[:, D:] + b_ref[1:2, :])
        dya = (dmv * sa).astype(BF16)
        dyb = (dmv * sb).astype(BF16)
        dga = dmv * ya_ref[...].astype(F32) * sa * (1.0 - sa)
        dgb = dmv * yb_ref[...].astype(F32) * sb * (1.0 - sb)
        dg_ref[:, :D] = dga.astype(BF16)
        dg_ref[:, D:] = dgb.astype(BF16)
        dya_ref[...] = dya
        dyb_ref[...] = dyb
        dpa_ref[...] = _dot(dya, wpa_ref[...], NT).astype(BF16)
        dpb_ref[...] = _dot(dyb, wpb_ref[...], NT).astype(BF16)
        part = jnp.concatenate([jnp.sum(dga, axis=0, keepdims=True), jnp.sum(dgb, axis=0, keepdims=True)], axis=0)

        @pl.when(pl.program_id(0) == 0)
        def _():
            db_ref[...] = part

        @pl.when(pl.program_id(0) > 0)
        def _():
            db_ref[...] += part

    row = lambda w: pl.BlockSpec((tm, w), lambda i: (i, 0))
    full = lambda shape: pl.BlockSpec(shape, lambda i: tuple(0 for _ in shape))
    o = lambda w: jax.ShapeDtypeStruct((T, w), BF16)
    return pl.pallas_call(
        body, grid=(T // tm,),
        in_specs=[row(D), row(2 * D), full((2, D)), row(D), row(D), full((GW, D)), full((SI, D))],
        out_specs=[row(2 * D), row(D), row(D), row(GW), row(SI), full((2, D))],
        out_shape=[o(DP_COLS), o(D), o(D), o(GW), o(SI), jax.ShapeDtypeStruct((2, D), F32)], name="merge_bwd",
        compiler_params=_params(("arbitrary",), 48))(dm, proj, bias, ya, yb, wpa, wpb)


RB = 128


def _shift_matrix(j):
    r = lax.broadcasted_iota(jnp.int32, (RB, RB), 0)
    c = lax.broadcasted_iota(jnp.int32, (RB, RB), 1)
    return jnp.where(c == r - j, 1.0, 0.0).astype(BF16)


def _rows_down(xb, before, shifts):
    H = SUBLANE
    mats = [_shift_matrix(j) for j in shifts]
    outs = [[] for _ in shifts]
    for b in range(xb.shape[0] // RB):
        blk = xb[b * RB:(b + 1) * RB]
        edge = jnp.concatenate([before, blk[:2 * H].astype(F32)[:H]], axis=0)
        for i, j in enumerate(shifts):
            outs[i] += [edge[H - j:2 * H - j], _dot(mats[i], blk)[H:]]
        before = blk[RB - 2 * H:].astype(F32)[H:]
    return [jnp.concatenate(o, axis=0) for o in outs]


def _rows_up(xb, after, shifts):
    H = SUBLANE
    nb = xb.shape[0] // RB
    mats = [_shift_matrix(-j) for j in shifts]
    outs = [[] for _ in shifts]
    for b in range(nb):
        blk = xb[b * RB:(b + 1) * RB]
        nxt = xb[(b + 1) * RB:(b + 1) * RB + 2 * H].astype(F32)[:H] if b + 1 < nb else after
        edge = jnp.concatenate([blk[RB - 2 * H:].astype(F32)[H:], nxt], axis=0)
        for i, j in enumerate(shifts):
            outs[i] += [_dot(mats[i], blk)[:RB - H], edge[j:H + j]]
    return [jnp.concatenate(o, axis=0) for o in outs]


def _ffn_act_fwd(up, cw, cb):
    T = up.shape[0]
    tm = min(512, T)
    H = SUBLANE

    def body(up_ref, cw_ref, cb_ref, o_ref, xc_ref, halo):
        @pl.when(pl.program_id(1) == 0)
        def _():
            halo[...] = jnp.zeros_like(halo)

        xb = up_ref[...]
        x2, x1 = _rows_down(xb, halo[...], (2, 1))
        xc = cb_ref[...] + cw_ref[0:1, :] * x2 + cw_ref[1:2, :] * x1 + cw_ref[2:3, :] * xb.astype(F32)
        xc_ref[...] = xc.astype(BF16)
        gate = xc[:, :FT]
        o_ref[...] = (gate * _sigmoid(gate) * xc[:, FT:]).astype(BF16)
        halo[...] = xb[tm - 2 * H:].astype(F32)[H:]

    tile = pl.BlockSpec((tm, 2 * FT), lambda j, i: (i, j))
    return pl.pallas_call(
        body, grid=(2, T // tm),
        in_specs=[tile, pl.BlockSpec((FK, 2 * FT), lambda j, i: (0, j)), pl.BlockSpec((1, 2 * FT), lambda j, i: (0, j))],
        out_specs=[pl.BlockSpec((tm, FT), lambda j, i: (i, j)), tile],
        out_shape=[jax.ShapeDtypeStruct((T, DFF), BF16), jax.ShapeDtypeStruct((T, 2 * DFF), BF16)],
        scratch_shapes=[pltpu.VMEM((H, 2 * FT), F32)], name="ffn_act_fwd",
        compiler_params=_params(("arbitrary", "arbitrary"), 48))(up, cw, cb)


def _ffn_act_bwd(up, xc, dact, cw):
    T = up.shape[0]
    tm = min(512, T)
    nt = T // tm
    H = SUBLANE

    def body(up_ref, xc_ref, da_ref, cw_ref, dup_ref, dcw_ref, dcb_ref, ahead):
        @pl.when(pl.program_id(1) == 0)
        def _():
            ahead[...] = jnp.zeros_like(ahead)
            dcw_ref[...] = jnp.zeros_like(dcw_ref)
            dcb_ref[...] = jnp.zeros_like(dcb_ref)

        xcv = xc_ref[...].astype(F32)
        gate, val = xcv[:, :FT], xcv[:, FT:]
        sg = _sigmoid(gate)
        dav = da_ref[...].astype(F32)
        dgate = dav * val * sg * (1.0 + gate * (1.0 - sg))
        dval = dav * gate * sg
        dxc = jnp.concatenate([dgate, dval], axis=1)
        d1, d2 = _rows_up(dxc.astype(BF16), ahead[...], (1, 2))
        x = up_ref[...].astype(F32)
        dcb_ref[...] += jnp.sum(dxc, axis=0, keepdims=True)
        dcw_ref[...] += jnp.concatenate([jnp.sum(d * x, axis=0, keepdims=True) for d in (d2, d1, dxc)], axis=0)
        dup_ref[...] = (cw_ref[2:3, :] * dxc + cw_ref[1:2, :] * d1 + cw_ref[0:1, :] * d2).astype(BF16)
        ahead[...] = dxc[0:H, :]

    tile = pl.BlockSpec((tm, 2 * FT), lambda j, i: (nt - 1 - i, j))
    return pl.pallas_call(
        body, grid=(2, nt),
        in_specs=[tile, tile, pl.BlockSpec((tm, FT), lambda j, i: (nt - 1 - i, j)),
                  pl.BlockSpec((FK, 2 * FT), lambda j, i: (0, j))],
        out_specs=[tile, pl.BlockSpec((FK, 2 * FT), lambda j, i: (0, j)), pl.BlockSpec((1, 2 * FT), lambda j, i: (0, j))],
        out_shape=[jax.ShapeDtypeStruct((T, 2 * DFF), BF16), jax.ShapeDtypeStruct((FK, 2 * DFF), F32),
                   jax.ShapeDtypeStruct((1, 2 * DFF), F32)],
        scratch_shapes=[pltpu.VMEM((H, 2 * FT), F32)], name="ffn_act_bwd",
        compiler_params=_params(("arbitrary", "arbitrary"), 56))(up, xc, dact, cw)


def _softplus(x):
    e = jnp.exp(-jnp.abs(x))
    return jnp.maximum(x, 0.0) + jnp.where(e < 1e-4, e * (1.0 - 0.5 * e), jnp.log(1.0 + e))


def _ssd_tril():
    li = lax.broadcasted_iota(jnp.int32, (LS, LS), 0)
    si = lax.broadcasted_iota(jnp.int32, (LS, LS), 1)
    return si <= li


def _head_expansion():
    hh = lax.broadcasted_iota(jnp.int32, (LANE, SI), 0)
    cc = lax.broadcasted_iota(jnp.int32, (LANE, SI), 1) // SP
    return jnp.where(hh == cc, 1.0, 0.0).astype(BF16)


def _ssd_pre(xc, dt_ref, dtb_ref, alog_ref, tril, expand):
    sx = _sigmoid(xc)
    xbc = xc * sx
    xs, bm, cm = xbc[:, :SI], xbc[:, SI:SI + SG * SN], xbc[:, SI + SG * SN:]
    dtin = dt_ref[...] + dtb_ref[...]
    dt = _softplus(dtin)
    a_neg = -jnp.exp(alog_ref[...])
    dta = dt * a_neg
    trilb = jnp.where(tril, 1.0, 0.0).astype(BF16)
    a = _dot3_rhs(trilb, dta, NN)
    a_exp = _dot3(a, expand, NN)
    dt_exp = _dot2(dt, expand, NN)
    xdt = xs * dt_exp
    a_last = a_exp[LS - 1:LS, :]
    return dict(xc=xc, sx=sx, xs=xs, bm=bm, cm=cm, dtin=dt_ref[...] + dtb_ref[...], dt=dt, a_neg=a_neg,
                a=a, a_t=a.T, a_exp=a_exp, dt_exp=dt_exp, xdt=xdt, ea=jnp.exp(a_exp),
                w=jnp.exp(a_last - a_exp), eal=jnp.exp(a_last))


def _head_decay(pre, tril, h):
    seg = pre["a"][:, h:h + 1] - pre["a_t"][h:h + 1, :]
    return jnp.exp(jnp.where(tril, seg, -1e30))


def _ssd_fwd(proj, dtraw, cw, cb, dtb, alog, dexp, nw):
    T = proj.shape[0]
    nc = T // LS
    H = SUBLANE

    def body(z_ref, x_ref, dt_ref, cw_ref, cb_ref, dtb_ref, alog_ref, dexp_ref, nw_ref, ex_ref,
             yb_ref, y_ref, sp_ref, xc_ref, halo, st):
        @pl.when(pl.program_id(0) == 0)
        def _():
            halo[...] = jnp.zeros_like(halo)
            st[...] = jnp.zeros_like(st)

        xb = x_ref[...]
        taps = _rows_down(xb, halo[...], (3, 2, 1)) + [xb.astype(F32)]
        xc = cb_ref[...]
        for k in range(SK):
            xc = xc + cw_ref[k:k + 1, :] * taps[k]
        xc_ref[...] = xc.astype(BF16)
        tril, expand = _ssd_tril(), ex_ref[...]
        pre = _ssd_pre(xc, dt_ref, dtb_ref, alog_ref, tril, expand)
        lane = lax.broadcasted_iota(jnp.int32, (LS, LANE), 1)
        lo = lane < SP
        zf = z_ref[...].astype(F32)
        siluz = zf * _sigmoid(zf)
        for g in range(SG):
            gs = slice(g * SGW, (g + 1) * SGW)
            bg = pre["bm"][:, g * SN:(g + 1) * SN].astype(BF16)
            cg = pre["cm"][:, g * SN:(g + 1) * SN].astype(BF16)
            gmat = _dot(cg, bg, NT)
            sg = st[g]
            sp_ref[0, g] = sg
            yoff = _dot(cg, sg.astype(BF16))
            parts = []
            for j in range(SGW // LANE):
                h0 = g * (SGW // SP) + 2 * j
                m0 = gmat * _head_decay(pre, tril, h0)
                m1 = gmat * _head_decay(pre, tril, h0 + 1)
                xp = pre["xdt"][:, g * SGW + j * LANE:g * SGW + (j + 1) * LANE]
                rhs = jnp.concatenate([jnp.where(lo, xp, 0.0), jnp.where(lo, 0.0, xp)], axis=0).astype(BF16)
                parts.append(_dot(jnp.concatenate([m0, m1], axis=1).astype(BF16), rhs))
            y = (jnp.concatenate(parts, axis=1) + pre["ea"][:, gs] * yoff + dexp_ref[:, gs] * pre["xs"][:, gs])
            st[g] = pre["eal"][:, gs] * sg + _dot(bg, (pre["w"][:, gs] * pre["xdt"][:, gs]).astype(BF16), TN)
            y_ref[:, gs] = y
            yg = y * siluz[:, gs]
            r = lax.rsqrt(jnp.mean(yg * yg, axis=-1, keepdims=True) + EPS)
            yb_ref[:, gs] = (yg * r * nw_ref[:, gs]).astype(BF16)
        halo[...] = xb[LS - 2 * H:].astype(F32)[H:]

    vec = lambda w: pl.BlockSpec((1, w), lambda c: (0, 0))
    return pl.pallas_call(
        body, grid=(nc,),
        in_specs=[pl.BlockSpec((LS, SI), lambda c: (c, 2)), pl.BlockSpec((LS, SXBC), lambda c: (c, 2)),
                  pl.BlockSpec((LS, LANE), lambda c: (c, 0)),
                  pl.BlockSpec((SK, SXBC), lambda c: (0, 0)), vec(SXBC), vec(LANE), vec(LANE), vec(SI), vec(SI),
                  pl.BlockSpec((LANE, SI), lambda c: (0, 0))],
        out_specs=[pl.BlockSpec((LS, SI), lambda c: (c, 0)), pl.BlockSpec((LS, SI), lambda c: (c, 0)),
                   pl.BlockSpec((1, SG, SN, SGW), lambda c: (c, 0, 0, 0)), pl.BlockSpec((LS, SXBC), lambda c: (c, 0))],
        out_shape=[jax.ShapeDtypeStruct((T, SI), BF16), jax.ShapeDtypeStruct((T, SI), F32),
                   jax.ShapeDtypeStruct((nc, SG, SN, SGW), F32), jax.ShapeDtypeStruct((T, SXBC), BF16)],
        scratch_shapes=[pltpu.VMEM((H, SXBC), F32), pltpu.VMEM((SG, SN, SGW), F32)], name="ssd_fwd",
        compiler_params=_params(("arbitrary",), VMEM_CAP_MB))(
            proj, proj, dtraw, cw, cb, dtb, alog, dexp, nw, _head_expansion())


def _ssd_bwd(proj, xcs, dtraw, y, sprev, dyb, dproj, cw, dtb, alog, dexp, nw):
    T = proj.shape[0]
    nc = T // LS
    H = SUBLANE
    NJ = 1

    def body(z_ref, x_ref, xc_ref, dt_ref, y_ref, sp_ref, dyb_ref, dproj_in,
             cw_ref, dtb_ref, alog_ref, dexp_ref, nw_ref, ex_ref,
             dp_ref, ddt_ref, dcw_ref, dcb_ref, ddtb_ref, da_ref, dd_ref, dnw_ref,
             ahead, ds, stage):
        del dproj_in
        i = pl.program_id(0)
        j = pl.program_id(1)

        @pl.when(jnp.logical_and(i == 0, j == 0))
        def _():
            ahead[...] = jnp.zeros_like(ahead)
            ds[...] = jnp.zeros_like(ds)
            for r in (dcw_ref, dcb_ref, ddtb_ref, da_ref, dd_ref, dnw_ref):
                r[...] = jnp.zeros_like(r)

        @pl.when(j == 0)
        def _():
            tril, expand = _ssd_tril(), ex_ref[...]
            pre = _ssd_pre(xc_ref[...].astype(F32), dt_ref, dtb_ref, alog_ref, tril, expand)
            lane = lax.broadcasted_iota(jnp.int32, (LS, LANE), 1)
            sub = lax.broadcasted_iota(jnp.int32, (LANE, LS), 0)
            rowi = lax.broadcasted_iota(jnp.int32, (LS, 1), 0)
            lo = lane < SP
            xs, xdt, ea, w, eal = pre["xs"], pre["xdt"], pre["ea"], pre["w"], pre["eal"]

            zf = z_ref[...].astype(F32)
            sz = _sigmoid(zf)
            siluz = zf * sz
            yv = y_ref[...]
            yg = yv * siluz
            dout = dyb_ref[...].astype(F32)
            dyg_parts, dnw_parts = [], []
            for g in range(SG):
                gs = slice(g * SGW, (g + 1) * SGW)
                ygg = yg[:, gs]
                r = lax.rsqrt(jnp.mean(ygg * ygg, axis=-1, keepdims=True) + EPS)
                yhat = ygg * r
                dn = dout[:, gs] * nw_ref[:, gs]
                dnw_parts.append(jnp.sum(dout[:, gs] * yhat, axis=0, keepdims=True))
                dyg_parts.append(r * (dn - yhat * jnp.mean(dn * yhat, axis=-1, keepdims=True)))
            dyg = jnp.concatenate(dyg_parts, axis=1)
            dnw_ref[...] += jnp.concatenate(dnw_parts, axis=1)
            dy = dyg * siluz
            stage[:, 0:SI] = (dyg * yv * sz * (1.0 + zf * (1.0 - sz))).astype(BF16)
            dd_ref[...] += jnp.sum(dy * xs, axis=0, keepdims=True)
            tt = ea * dy

            da_rows = jnp.zeros((LS, LANE), F32)
            da_cols = jnp.zeros((LANE, LS), F32)
            dxdt_parts, db_parts, dc_parts, daexp_parts = [], [], [], []
            for g in range(SG):
                gs = slice(g * SGW, (g + 1) * SGW)
                bg = pre["bm"][:, g * SN:(g + 1) * SN].astype(BF16)
                cg = pre["cm"][:, g * SN:(g + 1) * SN].astype(BF16)
                sg = sp_ref[0, g]
                sgb = sg.astype(BF16)
                dsg = ds[g]
                dsgb = dsg.astype(BF16)
                ttg = tt[:, gs].astype(BF16)
                yoff = _dot(cg, sgb)
                dc = _dot(ttg, sgb, NT)
                gmat = _dot(cg, bg, NT)
                dgm = jnp.zeros((LS, LS), F32)
                dxdt_pairs = []
                for jj in range(SGW // LANE):
                    h0 = g * (SGW // SP) + 2 * jj
                    ps = slice(g * SGW + jj * LANE, g * SGW + (jj + 1) * LANE)
                    l0 = _head_decay(pre, tril, h0)
                    l1 = _head_decay(pre, tril, h0 + 1)
                    m0 = gmat * l0
                    m1 = gmat * l1
                    dyp = dy[:, ps]
                    dy_lo = jnp.where(lo, dyp, 0.0).astype(BF16)
                    dy_hi = jnp.where(lo, 0.0, dyp).astype(BF16)
                    xpb = xdt[:, ps].astype(BF16)
                    dm0 = _dot(dy_lo, xpb, NT)
                    dm1 = _dot(dy_hi, xpb, NT)
                    q0 = dm0 * m0
                    q1 = dm1 * m1
                    da_rows = da_rows + jnp.where(lane == h0, jnp.sum(q0, axis=1, keepdims=True), 0.0)
                    da_rows = da_rows + jnp.where(lane == h0 + 1, jnp.sum(q1, axis=1, keepdims=True), 0.0)
                    da_cols = da_cols + jnp.where(sub == h0, jnp.sum(q0, axis=0, keepdims=True), 0.0)
                    da_cols = da_cols + jnp.where(sub == h0 + 1, jnp.sum(q1, axis=0, keepdims=True), 0.0)
                    dgm = dgm + dm0 * l0 + dm1 * l1
                    mcat = jnp.concatenate([m0, m1], axis=0).astype(BF16)
                    dycat = jnp.concatenate([dy_lo, dy_hi], axis=0)
                    dxdt_pairs.append(_dot(mcat, dycat, TN))
                dgb = dgm.astype(BF16)
                dc = dc + _dot(dgb, bg)
                db = _dot(dgb, cg, TN)
                zg = _dot(bg, dsgb)
                wg, xdtg = w[:, gs], xdt[:, gs]
                dxdt_g = jnp.concatenate(dxdt_pairs, axis=1) + wg * zg
                qg = zg * xdtg * wg
                last = (jnp.sum(qg, axis=0, keepdims=True)
                        + jnp.sum(dsg * sg, axis=0, keepdims=True) * eal[:, gs])
                daexp_parts.append(dy[:, gs] * ea[:, gs] * yoff - qg + jnp.where(rowi == LS - 1, last, 0.0))
                db = db + _dot((wg * xdtg).astype(BF16), dsgb, NT)
                ds[g] = eal[:, gs] * dsg + _dot(cg, ttg, TN)
                dxdt_parts.append(dxdt_g)
                db_parts.append(db)
                dc_parts.append(dc)
            dxdt = jnp.concatenate(dxdt_parts, axis=1)
            da_exp = jnp.concatenate(daexp_parts, axis=1)
            da = _dot2(da_exp, expand, NT) + da_rows - da_cols.T
            triub = jnp.where(tril, 1.0, 0.0).astype(BF16)
            ddta = _dot3_rhs(triub, da, TN)
            ddt = ddta * pre["a_neg"] + _dot2(dxdt * xs, expand, NT)
            da_ref[...] += jnp.sum(ddta * pre["dt"], axis=0, keepdims=True)
            ddt_raw = ddt * _sigmoid(pre["dtin"])
            ddt_ref[...] = ddt_raw
            ddtb_ref[...] += jnp.sum(ddt_raw, axis=0, keepdims=True)
            dxs = dexp_ref[...] * dy + dxdt * pre["dt_exp"]
            dxbc = jnp.concatenate([dxs] + db_parts + dc_parts, axis=1)
            sx, xc = pre["sx"], pre["xc"]
            dxc = dxbc * sx * (1.0 + xc * (1.0 - sx))
            taps = _rows_up(dxc.astype(BF16), ahead[...], (3, 2, 1)) + [dxc]
            xr = x_ref[...].astype(F32)
            dcb_ref[...] += jnp.sum(dxc, axis=0, keepdims=True)
            dcw_ref[...] += jnp.concatenate([jnp.sum(t * xr, axis=0, keepdims=True) for t in taps], axis=0)
            dxr = cw_ref[0:1, :] * taps[0]
            for k in range(1, SK):
                dxr = dxr + cw_ref[k:k + 1, :] * taps[k]
            stage[:, SI:] = dxr.astype(BF16)
            ahead[...] = dxc[0:H, :]

        dp_ref[...] = stage[...]

    vec = lambda w: pl.BlockSpec((1, w), lambda i, j: (0, 0))
    rev = lambda w, cb_: pl.BlockSpec((LS, w), lambda i, j: (nc - 1 - i, cb_))
    outs = pl.pallas_call(
        body, grid=(nc, NJ),
        in_specs=[rev(SI, 2), rev(SXBC, 2), rev(SXBC, 0),
                  rev(LANE, 0), rev(SI, 0),
                  pl.BlockSpec((1, SG, SN, SGW), lambda i, j: (nc - 1 - i, 0, 0, 0)),
                  rev(SI, 0), pl.BlockSpec(memory_space=pl.ANY),
                  pl.BlockSpec((SK, SXBC), lambda i, j: (0, 0)), vec(LANE), vec(LANE), vec(SI), vec(SI),
                  pl.BlockSpec((LANE, SI), lambda i, j: (0, 0))],
        out_specs=[rev(DP_SSM, 1), rev(LANE, 0),
                   pl.BlockSpec((SK, SXBC), lambda i, j: (0, 0)), vec(SXBC), vec(LANE), vec(LANE), vec(SI), vec(SI)],
        out_shape=[jax.ShapeDtypeStruct(dproj.shape, dproj.dtype), jax.ShapeDtypeStruct((T, LANE), F32),
                   jax.ShapeDtypeStruct((SK, SXBC), F32), jax.ShapeDtypeStruct((1, SXBC), F32),
                   jax.ShapeDtypeStruct((1, LANE), F32), jax.ShapeDtypeStruct((1, LANE), F32),
                   jax.ShapeDtypeStruct((1, SI), F32), jax.ShapeDtypeStruct((1, SI), F32)],
        scratch_shapes=[pltpu.VMEM((H, SXBC), F32),
                        pltpu.VMEM((SG, SN, SGW), F32), pltpu.VMEM((LS, SI + SXBC), BF16)],
        input_output_aliases={7: 0}, name="ssd_bwd",
        compiler_params=_params(("arbitrary", "arbitrary"), VMEM_CAP_MB))(
            proj, proj, xcs, dtraw, y, sprev, dyb, dproj, cw, dtb, alog, dexp, nw, _head_expansion())
    return outs


def _perm_ffn_cols(a):
    lead = a.shape[:-1]
    return a.reshape(lead + (2, 2, FT)).swapaxes(-3, -2).reshape(lead + (2 * DFF,))


def _perm_ffn_rows(a):
    return a.reshape((2, 2, FT) + a.shape[1:]).swapaxes(0, 1).reshape(a.shape)


def _pad_lanes(v, n=LANE):
    return jnp.pad(v, ((0, 0), (0, n - v.shape[-1])))


LATE = ["w_proj_a", "w_proj_b", "w_out", "ffn_w_up_t", "ffn_w_down"]
WGRAD = BF16
SMALL_BF16_FROM = 2 ** 16


class _NoExchange:
    def gather_start(self):
        return None

    def gather_pass_on(self, outs):
        return None

    def late_weights(self, w, outs):
        return w

    def late_to_sibling(self, grads):
        return None

    def reduce_late(self, outs):
        return None

    def w_in_to_sibling(self, grad_main, grad_dt):
        return None

    def reduce_w_in(self, outs):
        return None

    def reduced(self, late_outs, w_in_outs):
        pass


def _local_step(x, tgt, w, hooks=None):
    hooks = hooks or _NoExchange()

    def mm(*args, side=None, **kw):
        out = _matmul(*args, side=side, **kw)
        return out if side is not None else (out, [])

    win_t = w["w_in_t"]
    win_dt = jnp.pad(w["w_in_t"][PMAIN:], ((0, LANE - SH), (0, 0)))
    fcw = _perm_ffn_cols(w["ffn_conv_w"])
    fcb = _perm_ffn_cols(w["ffn_conv_b"][None, :])
    mixw = w["mix_norm_w"][None, :]
    ffnw = w["ffn_norm_w"][None, :]
    finw = w["final_norm_w"][None, :]
    bst = w["gmlp_bs"].T
    scb = w["ssm_conv_b"][None, :]
    dtb = _pad_lanes(w["ssm_dt_bias"][None, :])
    alog = _pad_lanes(w["ssm_a_log"][None, :])
    dexp = jnp.repeat(w["ssm_d"], SP)[None, :]
    snw = w["ssm_norm_w"][None, :]

    xn = _rms_fwd(x, mixw, name="mix_norm")
    proj, got = mm(xn, win_t, name="in_proj", out_dtype=BF16, tb=True, tn=3072, j_outer=True, b_rows=PMAIN,
                   side=hooks.gather_start())
    dtraw, got = mm(xn, win_dt, name="in_proj_dt", out_dtype=F32, tb=True, side=hooks.gather_pass_on(got))
    w = hooks.late_weights(w, got)
    wup = _perm_ffn_rows(w["ffn_w_up_t"])
    ya_pre = _gmlp_fwd(proj, w["gmlp_ln_w"], w["gmlp_ln_b"], w["gmlp_ws"], bst)
    yb_pre, y_ssd, sprev, ssm_xc = _ssd_fwd(proj, dtraw, w["ssm_conv_w"], scb, dtb, alog, dexp, snw)
    merged, ya, yb = _merge_fwd(ya_pre, yb_pre, proj, w["gate_bias"], w["w_proj_a"], w["w_proj_b"])
    h1 = _matmul(merged, w["w_out"], name="out_proj", out_dtype=F32, add=x)
    hn = _rms_fwd(h1, ffnw, name="ffn_norm")
    up = _matmul(hn, wup, name="ffn_up", out_dtype=BF16, tb=True, tn=2 * FT, j_outer=True)
    act, ffn_xc = _ffn_act_fwd(up, fcw, fcb)
    h2 = _matmul(act, w["ffn_w_down"], name="ffn_down", out_dtype=F32, tk=DFF, add=h1)

    loss_row, dh2, d_finw = _loss_head(h2, tgt, finw)
    dact = _matmul(dh2, w["ffn_w_down"], name="ffn_down_dx", out_dtype=BF16, tb=True, tn=DFF)
    d_wdown = _matmul(act, dh2, name="ffn_down_dw", out_dtype=WGRAD, ta=True, tm=FT, tk=2048)
    dup, d_fcw, d_fcb = _ffn_act_bwd(up, ffn_xc, dact, fcw)
    dhn = _matmul(dup, wup, name="ffn_up_dx", out_dtype=F32, tk=2 * FT)
    d_wup = _matmul(dup, hn, name="ffn_up_dw", out_dtype=WGRAD, ta=True, tm=FT, tk=2048,
                    o_row=lambda i: (i % 2) * 2 + i // 2)
    dh1, d_ffnw = _rms_bwd(h1, ffnw, dhn, dh2, name="ffn_norm_bwd")
    dmerged = _matmul(dh1, w["w_out"], name="out_proj_dx", out_dtype=BF16, tb=True)
    d_wout = _matmul(merged, dh1, name="out_proj_dw", out_dtype=WGRAD, ta=True, tk=2048)
    dproj, dya, dyb, dya_pre, dyb_pre, d_gbias = _merge_bwd(dmerged, proj, w["gate_bias"], ya, yb,
                                                           w["w_proj_a"], w["w_proj_b"])
    d_wpa = _matmul(ya_pre, dya, name="proj_a_dw", out_dtype=WGRAD, ta=True, tk=2048)
    d_wpb = _matmul(yb_pre, dyb, name="proj_b_dw", out_dtype=WGRAD, ta=True, tk=2048)
    late = {"w_proj_a": d_wpa, "w_proj_b": d_wpb, "w_out": d_wout, "ffn_w_up_t": d_wup,
            "ffn_w_down": d_wdown}
    (dproj, d_lnw, d_lnb, d_ws, d_bst), got = _gmlp_bwd(proj, dya_pre, dproj, w["gmlp_ln_w"], w["gmlp_ln_b"],
                                                        w["gmlp_ws"], bst, side=hooks.late_to_sibling(late))
    dproj, ddt, d_scw, d_scb, d_dtb, d_a, d_dch, d_snw = _ssd_bwd(
        proj, ssm_xc, dtraw, y_ssd, sprev, dyb_pre, dproj, w["ssm_conv_w"], dtb, alog, dexp, snw)
    gap = (2 * D + 2 * GW, DP_GAP)
    d_win_main, late_outs = mm(dproj, xn, name="in_proj_dw", out_dtype=WGRAD, ta=True, a_gap=gap, tk=2048,
                               side=hooks.reduce_late(got))
    d_win_dt = _matmul(ddt, xn, name="in_proj_dt_dw", out_dtype=F32, ta=True)
    d_win_t = jnp.concatenate([d_win_main, d_win_dt[:SH]], axis=0)
    dxn, got = mm(ddt, win_dt, name="in_proj_dt_dx", out_dtype=F32,
                  side=hooks.w_in_to_sibling(d_win_main, d_win_dt[:SH]))
    dxn, w_in_outs = mm(dproj, win_t, name="in_proj_dx", out_dtype=F32, add=dxn, b_rows=PMAIN, a_gap=gap,
                        side=hooks.reduce_w_in(got))
    hooks.reduced(late_outs, w_in_outs)
    grad_x, d_mixw = _rms_bwd(x, mixw, dxn, dh1, name="mix_norm_bwd")

    a_neg = -jnp.exp(w["ssm_a_log"])
    grads = {
        "mix_norm_w": d_mixw[0],
        "w_in_t": d_win_t,
        "gate_bias": d_gbias,
        "gmlp_ln_w": d_lnw, "gmlp_ln_b": d_lnb, "gmlp_ws": d_ws, "gmlp_bs": d_bst[:, :GG].T,
        "ssm_conv_w": d_scw, "ssm_conv_b": d_scb[0],
        "ssm_dt_bias": d_dtb[0, :SH], "ssm_a_log": d_a[0, :SH] * a_neg,
        "ssm_d": d_dch.reshape(SH, SP).sum(axis=-1), "ssm_norm_w": d_snw[0],
        **late,
        "ffn_norm_w": d_ffnw[0],
        "ffn_conv_w": _perm_ffn_cols(d_fcw), "ffn_conv_b": _perm_ffn_cols(d_fcb)[0],
        "ffn_w_down": d_wdown, "final_norm_w": d_finw[0],
    }
    return loss_row, grad_x, grads


MESH = pl.DeviceIdType.MESH
HBM_SPEC = pl.BlockSpec(memory_space=pltpu.HBM)


def _axes():
    return lax.axis_index("x"), lax.axis_index("y"), lax.axis_index("c")


def _all_gather(shards, *, name):
    na = len(shards)

    def body(*refs):
        x_refs, out_refs = refs[:na], refs[na:2 * na]
        send_sems, recv_sems, local_sems = refs[2 * na:]
        x, y, c = _axes()
        me, sibling = (x, y, c), (x, y, 1 - c)
        chips = [(1 - x, y), (x, 1 - y), (1 - x, 1 - y)]

        def slot(a, px, py, pc):
            return out_refs[a].at[4 * px + 2 * py + pc]

        def copy(a, k, block, to, src=None):
            return pltpu.make_async_remote_copy(
                src_ref=slot(a, *block) if src is None else src, dst_ref=slot(a, *block),
                send_sem=send_sems.at[7 * a + k], recv_sem=recv_sems.at[7 * a + k], device_id=to, device_id_type=MESH)

        mine = [pltpu.make_async_copy(x_refs[a], slot(a, *me), local_sems.at[a]) for a in range(na)]
        for cp in mine:
            cp.start()
        first = []
        for a in range(na):
            first.append(copy(a, 0, me, sibling, src=x_refs[a]))
            first += [copy(a, 1 + j, me, (*chip, c), src=x_refs[a]) for j, chip in enumerate(chips)]
        for cp in first:
            cp.start()
        passed = []
        for j, chip in enumerate(chips):
            for a in range(na):
                copy(a, 1 + j, (*chip, c), me).wait_recv()
                cp = copy(a, 4 + j, (*chip, c), sibling)
                cp.start()
                passed.append(cp)
        for a in range(na):
            copy(a, 0, sibling, me).wait_recv()
        for j, chip in enumerate(chips):
            for a in range(na):
                copy(a, 4 + j, (*chip, 1 - c), me).wait_recv()
        for cp in first + passed:
            cp.wait_send()
        for cp in mine:
            cp.wait()

    return pl.pallas_call(
        body, out_shape=[jax.ShapeDtypeStruct((NDEV,) + s.shape, s.dtype) for s in shards],
        in_specs=[HBM_SPEC] * na, out_specs=[HBM_SPEC] * na,
        scratch_shapes=[pltpu.SemaphoreType.DMA((7 * na,)), pltpu.SemaphoreType.DMA((7 * na,)),
                        pltpu.SemaphoreType.DMA((na,))],
        name=name)(*shards)


def _exchange(srcs, plan, *, name):
    na = len(srcs)
    n = len(plan(0, 0, 0))

    def body(*refs):
        src_refs, out_refs = refs[:na], refs[na:2 * na]
        send_sems, recv_sems = refs[2 * na:]
        x, y, c = _axes()
        copies = []
        for k, (slab, peer) in enumerate(plan(x, y, c)):
            for a in range(na):
                cp = pltpu.make_async_remote_copy(
                    src_ref=src_refs[a].at[slab], dst_ref=out_refs[a].at[k], send_sem=send_sems.at[n * a + k],
                    recv_sem=recv_sems.at[n * a + k], device_id=peer, device_id_type=MESH)
                cp.start()
                copies.append(cp)
        for cp in copies:
            cp.wait()

    return pl.pallas_call(
        body, out_shape=[jax.ShapeDtypeStruct((n,) + s.shape[1:], s.dtype) for s in srcs],
        in_specs=[HBM_SPEC] * na, out_specs=[HBM_SPEC] * na,
        scratch_shapes=[pltpu.SemaphoreType.DMA((n * na,)), pltpu.SemaphoreType.DMA((n * na,))], name=name)(*srcs)


def _to_sibling_plan(x, y, c):
    return [(2 * q + (1 - c), (x, y, 1 - c)) for q in range(4)]


def _to_chips_plan(x, y, c):
    q = 2 * x + y
    return [(q ^ 2, (1 - x, y, c)), (q ^ 1, (x, 1 - y, c)), (q ^ 3, (1 - x, 1 - y, c))]


def _row_tile(rows, row_bytes, budget=2 * 2 ** 20, align=2 * SUBLANE):
    if rows * row_bytes <= 2 * budget:
        return rows
    best = None
    for d in range(align, rows + 1, align):
        if rows % d == 0 and d * row_bytes <= budget:
            best = d
    return best or rows


def _pair_add(g, ra, c_idx, *, name):
    _, _, R, C = g.shape
    tr = _row_tile(R, C * 4, budget=3 * 2 ** 20)

    def body(c_ref, g_ref, ra_ref, o_ref):
        del c_ref
        o_ref[...] = (g_ref[0].astype(F32) + ra_ref[...].astype(F32)).astype(o_ref.dtype)

    return pl.pallas_call(
        body,
        grid_spec=pltpu.PrefetchScalarGridSpec(
            num_scalar_prefetch=1, grid=(4, R // tr),
            in_specs=[pl.BlockSpec((1, 1, tr, C), lambda q, r, cr: (q, cr[0], r, 0)),
                      pl.BlockSpec((1, tr, C), lambda q, r, cr: (q, r, 0))],
            out_specs=pl.BlockSpec((1, tr, C), lambda q, r, cr: (q, r, 0))),
        out_shape=jax.ShapeDtypeStruct((4, R, C), g.dtype), name=name,
        compiler_params=_params(("arbitrary", "arbitrary"), 24))(c_idx, g, ra)


def _grad_sum(p, rb, q_idx, *, name):
    _, R, C = p.shape
    tr = _row_tile(R, C * 4, budget=3 * 2 ** 20)

    def body(q_ref, p_ref, rb_ref, o_ref):
        del q_ref
        g = p_ref[0].astype(F32)
        for k in range(3):
            g = g + rb_ref[k].astype(F32)
        o_ref[...] = g

    return pl.pallas_call(
        body,
        grid_spec=pltpu.PrefetchScalarGridSpec(
            num_scalar_prefetch=1, grid=(R // tr,),
            in_specs=[pl.BlockSpec((1, tr, C), lambda r, qr: (qr[0], r, 0)),
                      pl.BlockSpec((3, tr, C), lambda r, qr: (0, r, 0))],
            out_specs=pl.BlockSpec((tr, C), lambda r, qr: (r, 0))),
        out_shape=jax.ShapeDtypeStruct((R, C), F32), name=name,
        compiler_params=_params(("arbitrary",), 40))(q_idx, p, rb)


def _adamw(g, w, m, v):
    m = ADAM_B1 * m + (1.0 - ADAM_B1) * g
    v = ADAM_B2 * v + (1.0 - ADAM_B2) * (g * g)
    m_hat = m / (1.0 - ADAM_B1 ** ADAM_STEP)
    v_hat = v / (1.0 - ADAM_B2 ** ADAM_STEP)
    delta = -ADAM_LR * (m_hat / (jnp.sqrt(v_hat) + ADAM_EPS) + ADAM_WD * w)
    return delta, m, v


def _adam(g, w, m, v, *, name):
    _, R, C = w.shape
    tr = _row_tile(R, C * 4, budget=2 ** 20, align=SUBLANE)

    def body(g_ref, w_ref, m_ref, v_ref, d_out, m_out, v_out):
        delta, mn, vn = _adamw(g_ref[...], w_ref[...], m_ref[...], v_ref[...])
        d_out[...] = delta
        m_out[...] = mn
        v_out[...] = vn

    row = pl.BlockSpec((1, tr, C), lambda r: (0, r, 0))
    o = jax.ShapeDtypeStruct((1, R, C), F32)
    return pl.pallas_call(
        body, grid=(R // tr,), in_specs=[row, row, row, row], out_specs=[row, row, row], out_shape=[o, o, o],
        name=name, compiler_params=_params(("arbitrary",), 32))(g, w, m, v)


def _vmem_specs(n):
    return [pl.BlockSpec(memory_space=pltpu.VMEM)] * n


def _pair_sum_many(mine, theirs, *, name):
    n = len(mine)

    def body(*refs):
        for a in range(n):
            refs[2 * n + a][...] = refs[a][...] + refs[n + a][0]

    return pl.pallas_call(
        body, out_shape=[jax.ShapeDtypeStruct(m.shape, m.dtype) for m in mine], in_specs=_vmem_specs(2 * n),
        out_specs=_vmem_specs(n), name=name)(*mine, *theirs)


def _chip_sum_many(own, recv, q_idx, *, name):
    n = len(own)

    def body(q_ref, *refs):
        q = q_ref[0]
        for a in range(n):
            mine, r = refs[a][...], refs[n + a]
            total = None
            for chip in range(4):
                e = q ^ chip
                term = jnp.where(e == 0, mine, jnp.where(e == 2, r[0], jnp.where(e == 1, r[1], r[2]))).astype(F32)
                total = term if total is None else total + term
            refs[2 * n + a][...] = total

    return pl.pallas_call(
        body, out_shape=[jax.ShapeDtypeStruct(m.shape, F32) for m in own],
        in_specs=[pl.BlockSpec(memory_space=pltpu.SMEM)] + _vmem_specs(2 * n), out_specs=_vmem_specs(n),
        name=name)(q_idx, *own, *recv)


def _adam_many(gs, ws, ms, vs, *, name):
    n = len(gs)

    def body(*refs):
        for a in range(n):
            delta, mn, vn = _adamw(*(refs[k * n + a][...] for k in range(4)))
            refs[4 * n + a][...] = delta
            refs[5 * n + a][...] = mn
            refs[6 * n + a][...] = vn

    shapes = [jax.ShapeDtypeStruct(w.shape, w.dtype) for w in ws]
    out = pl.pallas_call(body, out_shape=shapes * 3, in_specs=_vmem_specs(4 * n), out_specs=_vmem_specs(3 * n),
                         name=name)(*gs, *ws, *ms, *vs)
    return out[:n], out[n:2 * n], out[2 * n:]


WEIGHTS = ["mix_norm_w", "w_in", "gate_bias", "gmlp_ln_w", "gmlp_ln_b", "gmlp_ws", "gmlp_bs", "ssm_conv_w",
           "ssm_conv_b", "ssm_dt_bias", "ssm_a_log", "ssm_d", "ssm_norm_w", "w_proj_a", "w_proj_b", "w_out",
           "ffn_norm_w", "ffn_w_up", "ffn_conv_w", "ffn_conv_b", "ffn_w_down", "final_norm_w"]
SHARDED = {"w_in": ((D, IN_COLS), 1), "gate_bias": ((2, D), 1), "ssm_conv_w": ((SK, SXBC), 1),
           "w_proj_a": ((GW, D), 0), "w_proj_b": ((SI, D), 0), "w_out": ((D, D), 0),
           "ffn_w_up": ((D, 2 * DFF), 1), "ffn_conv_w": ((FK, 2 * DFF), 1), "ffn_w_down": ((DFF, D), 0)}
REPLICATED = {"mix_norm_w": (D,), "gmlp_ln_w": (GG, GD), "gmlp_ln_b": (GG, GD), "gmlp_ws": (GG, GB, GB),
              "gmlp_bs": (GG, GB), "ssm_conv_b": (SXBC,), "ssm_dt_bias": (SH,), "ssm_a_log": (SH,), "ssm_d": (SH,),
              "ssm_norm_w": (SI,), "ffn_norm_w": (D,), "ffn_conv_b": (2 * DFF,), "final_norm_w": (D,)}
REPL_ORDER = [n for n in WEIGHTS if n in REPLICATED]
BTILE = 2 * SUBLANE
WIN_R = IN_COLS // NDEV
WIN_P = WIN_R + BTILE - WIN_R % BTILE
WIN_A = [WIN_R * d // BTILE * BTILE for d in range(NDEV)]
assert all(WIN_A[d] + WIN_P >= WIN_R * (d + 1) for d in range(NDEV)) and WIN_A[-1] + WIN_P == IN_COLS
BIG = [("w_proj_a", GW // NDEV, False), ("w_proj_b", SI // NDEV, False), ("w_out", D // NDEV, False),
       ("ffn_w_up", 2 * DFF // NDEV, True), ("ffn_w_down", DFF // NDEV, False), ("w_in", WIN_P, True)]
VECTORS = ["gate_bias", "ssm_conv_w", "ffn_conv_w"]


def _round_up(n, k):
    return (n + k - 1) // k * k


BIG_OFF = {}
_off = 0
for _n, _r, _t in BIG:
    BIG_OFF[_n] = _off
    _off += _r
BIG_USED = _off
BIG_ROWS = _round_up(BIG_USED, 2 * SUBLANE)
assert all(BIG_OFF[n] % (2 * SUBLANE) == 0 for n, _, _ in BIG)
VEC_SHAPE = {n: (SHARDED[n][0][0], SHARDED[n][0][1] // NDEV) for n in VECTORS}


def _win_offset(dev):
    return WIN_R * dev - WIN_R * dev // BTILE * BTILE


def _pack_big(arrs, dtype, dev):
    parts = []
    for n, r, t in BIG:
        a = (arrs[n].T if t else arrs[n]).astype(dtype)
        if n == "w_in":
            a = lax.dynamic_update_slice(jnp.zeros((WIN_P, D), dtype), a, (_win_offset(dev), 0))
        parts.append(a)
    parts.append(jnp.zeros((BIG_ROWS - BIG_USED, D), dtype))
    return jnp.concatenate(parts, axis=0)


def _join_windows(win):
    parts = []
    for d in range(NDEV):
        lo = BTILE if WIN_A[d] % WIN_R else 0
        if lo:
            parts.append(win[d - 1, WIN_P - BTILE:] + win[d, :BTILE])
        hi = WIN_P - BTILE if d + 1 < NDEV and WIN_A[d + 1] < WIN_A[d] + WIN_P else WIN_P
        parts.append(win[d, lo:hi])
    return jnp.concatenate(parts, axis=0)


def _split_windows(main, last):
    assert WIN_A[-2] + WIN_P <= PMAIN
    wins = [main[a:a + WIN_P] for a in WIN_A[:-1]]
    return jnp.stack(wins + [jnp.concatenate([main[WIN_A[-1]:], last], axis=0)])


LATE_ROWS = BIG_OFF["w_in"]
assert LATE_ROWS + WIN_P == BIG_ROWS and BIG[-1][0] == "w_in"


def _remote(src, dst, send_sems, recv_sems, k, to):
    return pltpu.make_async_remote_copy(src_ref=src, dst_ref=dst, send_sem=send_sems.at[k], recv_sem=recv_sems.at[k],
                                        device_id=to, device_id_type=MESH)


class _Exchange:
    def __init__(self, late_shard, c_idx):
        self.late_shard, self.c_idx = late_shard, c_idx

    def gather_start(self):
        shard = self.late_shard

        def make(ins, outs, send_sems, recv_sems):
            (x_ref,), (out,) = ins, outs
            x, y, c = _axes()
            mine = out.at[4 * x + 2 * y + c]
            peers = [(x, y, 1 - c), (1 - x, y, c), (x, 1 - y, c), (1 - x, 1 - y, c)]
            copies = [_remote(x_ref, mine, send_sems, recv_sems, k, p) for k, p in enumerate(peers)]
            return copies + [pltpu.make_async_copy(x_ref, mine, send_sems.at[len(peers)])]

        return _Side([shard], [jax.ShapeDtypeStruct((NDEV,) + shard.shape, shard.dtype)], 5, make)

    def gather_pass_on(self, outs):
        (buf,) = outs

        def make(ins, outs, send_sems, recv_sems):
            (src,), (dst,) = ins, outs
            x, y, c = _axes()
            slots = [4 * px + 2 * py + c for px, py in [(1 - x, y), (x, 1 - y), (1 - x, 1 - y)]]
            return [_remote(src.at[s], dst.at[s], send_sems, recv_sems, k, (x, y, 1 - c)) for k, s in enumerate(slots)]

        return _Side([buf], [jax.ShapeDtypeStruct(buf.shape, buf.dtype)], 3, make, aliases=[(0, 0)])

    def late_weights(self, w, outs):
        (buf,) = outs
        w = dict(w)
        for n, r, t in BIG[:-1]:
            w[n + "_t" if t else n] = buf[:, BIG_OFF[n]:BIG_OFF[n] + r].reshape(NDEV * r, D)
        return w

    @staticmethod
    def _plan_side(src, plan):
        n = len(plan(0, 0, 0))

        def make(ins, outs, send_sems, recv_sems):
            (s,), (dst,) = ins, outs
            return [_remote(s.at[slab], dst.at[k], send_sems, recv_sems, k, peer)
                    for k, (slab, peer) in enumerate(plan(*_axes()))]

        return _Side([src], [jax.ShapeDtypeStruct((n,) + src.shape[1:], src.dtype)], n, make)

    def _to_chips(self, send, sib, tag):
        sums = _pair_add(send.reshape((4, 2) + send.shape[1:]), sib, self.c_idx, name=tag + "_grad_pair_add")
        return sums, self._plan_side(sums, _to_chips_plan)

    def late_to_sibling(self, grads):
        send = jnp.concatenate([grads[n + "_t" if t else n].reshape(NDEV, r, D) for n, r, t in BIG[:-1]], axis=1)
        self.late_send = send.astype(BF16)
        return self._plan_side(self.late_send, _to_sibling_plan)

    def reduce_late(self, outs):
        self.late_sum, side = self._to_chips(self.late_send, outs[0], "late")
        return side

    def w_in_to_sibling(self, grad_main, grad_dt):
        self.w_in_send = _split_windows(grad_main.astype(BF16), grad_dt.astype(BF16))
        return self._plan_side(self.w_in_send, _to_sibling_plan)

    def reduce_w_in(self, outs):
        self.w_in_sum, side = self._to_chips(self.w_in_send, outs[0], "w_in")
        return side

    def reduced(self, late_outs, w_in_outs):
        (self.late_from_chips,), (self.w_in_from_chips,) = late_outs, w_in_outs


def kernel(x, mix_norm_w, w_in, gate_bias, gmlp_ln_w, gmlp_ln_b, gmlp_ws, gmlp_bs, ssm_conv_w, ssm_conv_b, ssm_dt_bias, ssm_a_log, ssm_d, ssm_norm_w, w_proj_a, w_proj_b, w_out, ffn_norm_w, ffn_w_up, ffn_conv_w, ffn_conv_b, ffn_w_down, final_norm_w, loss_target, m_mix_norm_w, m_w_in, m_gate_bias, m_gmlp_ln_w, m_gmlp_ln_b, m_gmlp_ws, m_gmlp_bs, m_ssm_conv_w, m_ssm_conv_b, m_ssm_dt_bias, m_ssm_a_log, m_ssm_d, m_ssm_norm_w, m_w_proj_a, m_w_proj_b, m_w_out, m_ffn_norm_w, m_ffn_w_up, m_ffn_conv_w, m_ffn_conv_b, m_ffn_w_down, m_final_norm_w, v_mix_norm_w, v_w_in, v_gate_bias, v_gmlp_ln_w, v_gmlp_ln_b, v_gmlp_ws, v_gmlp_bs, v_ssm_conv_w, v_ssm_conv_b, v_ssm_dt_bias, v_ssm_a_log, v_ssm_d, v_ssm_norm_w, v_w_proj_a, v_w_proj_b, v_w_out, v_ffn_norm_w, v_ffn_w_up, v_ffn_conv_w, v_ffn_conv_b, v_ffn_w_down, v_final_norm_w):
    given = dict(locals())
    wts = {n: given[n] for n in WEIGHTS}
    mom = {n: given["m_" + n] for n in WEIGHTS}
    var = {n: given["v_" + n] for n in WEIGHTS}
    xi, yi, ci = _axes()
    c_idx = jnp.reshape(ci, (1,)).astype(jnp.int32)
    q_idx = jnp.reshape(2 * xi + yi, (1,)).astype(jnp.int32)
    big_names = [n for n, _, _ in BIG]
    drop = lambda d, names: {n: d[n][0] for n in names}

    dev = 4 * xi + 2 * yi + ci
    packed = _pack_big(drop(wts, big_names), BF16, dev)
    gathered = _all_gather([packed[LATE_ROWS:]] + [wts[n] for n in VECTORS], name="w_in_all_gather")
    full = {"w_in_t": _join_windows(gathered[0])}
    for n, a in zip(VECTORS, gathered[1:]):
        r, c = VEC_SHAPE[n]
        full[n] = a[:, 0].transpose(1, 0, 2).reshape(r, NDEV * c)
    for n in REPL_ORDER:
        full[n] = wts[n].reshape(REPLICATED[n])

    hooks = _Exchange(packed[:LATE_ROWS], c_idx)
    loss_local, grad_x, grads = _local_step(x[0], loss_target[0], full, hooks)
    g_late = _grad_sum(hooks.late_sum, hooks.late_from_chips, q_idx, name="late_grad_sum")
    g_win = _grad_sum(hooks.w_in_sum, hooks.w_in_from_chips, q_idx, name="w_in_grad_sum")

    small = VECTORS + REPL_ORDER
    as_2d = lambda a: a if a.ndim >= 2 else a[None]
    part = [grads[n].reshape((1,) + SHARDED[n][0] if n in VECTORS else as_2d(wts[n]).shape) for n in small]
    part.append(loss_local)
    from_sibling = _exchange([p[None] for p in part], lambda x, y, c: [(0, (x, y, 1 - c))],
                             name="small_grads_to_sibling")
    chip_sums = _pair_sum_many(part, from_sibling, name="small_grad_pair_sum")
    chip_sums = [s.astype(BF16) if s.size >= SMALL_BF16_FROM else s for s in chip_sums]
    from_chips = _exchange([s[None] for s in chip_sums],
                           lambda x, y, c: [(0, (1 - x, y, c)), (0, (x, 1 - y, c)), (0, (1 - x, 1 - y, c))],
                           name="small_grads_to_chips")
    totals = _chip_sum_many(chip_sums, from_chips, q_idx, name="small_grad_chip_sum")
    g_small, loss = dict(zip(small, totals)), totals[-1][0, 0]
    for n in VECTORS:
        c = VEC_SHAPE[n][1]
        g_small[n] = lax.dynamic_slice_in_dim(g_small[n], dev * c, c, axis=2)

    outs = {}
    small_g = [g_small[n] for n in small]
    small_out = _adam_many(small_g, *[[as_2d(d[n]) for n in small] for d in (wts, mom, var)], name="adam_small")
    for i, n in enumerate(small):
        outs[n] = tuple(a[i].reshape(wts[n].shape) for a in (small_g,) + tuple(small_out))
    for n, r, t in BIG:
        if n == "w_in":
            g = lax.dynamic_slice(g_win, (_win_offset(dev), 0), (WIN_R, D))
        else:
            g = g_late[BIG_OFF[n]:BIG_OFF[n] + r]
        flip = (lambda a: a.transpose(0, 2, 1)) if t else (lambda a: a)
        g = g[None]
        new = _adam(g, flip(wts[n]), flip(mom[n]), flip(var[n]), name="adam_" + n)
        outs[n] = tuple(flip(a) for a in (g,) + tuple(new))
    return (loss, grad_x[None]) + tuple(outs[n][k] for k in range(4) for n in WEIGHTS)
```

```python
import functools

import jax
import jax.numpy as jnp
from jax import lax
from jax.experimental import pallas as pl
from jax.experimental.pallas import tpu as pltpu

F32 = jnp.float32
BF16 = jnp.bfloat16

D = 1024
EPS = 1e-5
GW = 1024
GB = 128
GG = 8
GD = 128
GCH = 64
SI = 2048
SH = 32
SP = 64
SG = 4
SN = 128
SGW = SI // SG
SK = 4
SXBC = SI + 2 * SG * SN
DFF = 2816
FK = 3
PMAIN = 2 * D + 2 * GW + SI + SXBC
IN_COLS = PMAIN + SH
DP_SSM = SI + SXBC
DP_GAP = (DP_SSM - (2 * D + 2 * GW) % DP_SSM) % DP_SSM
DP_COLS = 2 * D + 2 * GW + DP_GAP + DP_SSM
assert DP_GAP % D == 0 and (2 * D + 2 * GW) % D == 0
NDEV = 8
ADAM_LR, ADAM_B1, ADAM_B2, ADAM_EPS, ADAM_WD, ADAM_STEP = 0.001, 0.9, 0.999, 1e-08, 0.01, 10

LANE = 128
SUBLANE = 8
VMEM_MB_V7X = 64
VMEM_CAP_MB = VMEM_MB_V7X - 8

LS = 128
FT = DFF // 2

NN = (((1,), (0,)), ((), ()))
NT = (((1,), (1,)), ((), ()))
TN = (((0,), (0,)), ((), ()))


def _params(sem, vmem_mb):
    return pltpu.CompilerParams(dimension_semantics=sem,
                                vmem_limit_bytes=min(int(vmem_mb), VMEM_CAP_MB) * 1024 * 1024)


def _dot(a, b, dims=NN):
    return lax.dot_general(a, b, dims, preferred_element_type=F32)


def _sigmoid(x):
    return 1.0 / (1.0 + jnp.exp(-x))


def _split3(v):
    hi = v.astype(BF16)
    r = v - hi.astype(F32)
    mid = r.astype(BF16)
    lo = (r - mid.astype(F32)).astype(BF16)
    return hi, mid, lo


def _dot3(a_f32, b_bf16, dims):
    hi, mid, lo = _split3(a_f32)
    return _dot(hi, b_bf16, dims) + _dot(mid, b_bf16, dims) + _dot(lo, b_bf16, dims)


def _dot2(a_f32, b_bf16, dims):
    hi, mid, _ = _split3(a_f32)
    return _dot(hi, b_bf16, dims) + _dot(mid, b_bf16, dims)


def _dot3_rhs(a_bf16, b_f32, dims):
    hi, mid, lo = _split3(b_f32)
    return _dot(a_bf16, hi, dims) + _dot(a_bf16, mid, dims) + _dot(a_bf16, lo, dims)


def _matmul(a, b, *, name, out_dtype, ta=False, tb=False, tm=1024, tn=1024, tk=1024, add=None,
            j_outer=False, b_rows=None, a_gap=None, o_row=None, side=None):
    assert not (o_row and add is not None)
    gap0, gapw = a_gap or (0, 0)
    if ta:
        K, M = a.shape
        M -= gapw
    else:
        M, K = a.shape
        K -= gapw
    if tb:
        N, K2 = b.shape
        N = b_rows or N
    else:
        K2, N = b.shape
        K2 = b_rows or K2
    assert K == K2, (a.shape, b.shape, ta, tb)
    tm, tn, tk = min(tm, M), min(tn, N), min(tk, K)
    assert M % tm == 0 and N % tn == 0 and K % tk == 0, (M, N, K, tm, tn, tk)
    nk = K // tk
    dims = (((0 if ta else 1,), (1 if tb else 0,)), ((), ()))
    has_add = add is not None
    n_in = 3 if has_add else 2
    s_in = len(side.inputs) if side else 0
    s_out = len(side.out_shapes) if side else 0
    grid = (N // tn, M // tm, nk) if j_outer else (M // tm, N // tn, nk)

    def body(*refs):
        a_ref, b_ref = refs[:2]
        add_ref = refs[2] if has_add else None
        o_ref = refs[n_in + s_in]
        if side:
            side_refs = (refs[n_in:n_in + s_in], refs[n_in + s_in + 1:n_in + s_in + 1 + s_out]) + tuple(refs[-2:])
            ids = [pl.program_id(d) for d in range(3)]
            first = functools.reduce(jnp.logical_and, [i == 0 for i in ids])
            last = functools.reduce(jnp.logical_and, [i == g - 1 for i, g in zip(ids, grid)])

            @pl.when(first)
            def _():
                for cp in side.make(*side_refs):
                    cp.start()

            @pl.when(last)
            def _():
                for cp in side.make(*side_refs):
                    cp.wait()

        p = lax.dot_general(a_ref[...].astype(BF16), b_ref[...].astype(BF16), dims,
                            preferred_element_type=F32)

        def finish(acc):
            if has_add:
                acc = acc + add_ref[...].astype(F32)
            o_ref[...] = acc.astype(o_ref.dtype)

        if nk == 1:
            finish(p)
        else:
            acc_ref = refs[n_in + s_in + 1 + s_out]
            k = pl.program_id(2)

            @pl.when(k == 0)
            def _():
                acc_ref[...] = p

            @pl.when(jnp.logical_and(k > 0, k < nk - 1))
            def _():
                acc_ref[...] += p

            @pl.when(k == nk - 1)
            def _():
                finish(acc_ref[...] + p)

    if j_outer:
        ij = lambda g0, g1: (g1, g0)
    else:
        ij = lambda g0, g1: (g0, g1)

    ta_col = tm if ta else tk
    assert gap0 % ta_col == 0 and gapw % ta_col == 0, (a_gap, ta_col)

    def a_map(g0, g1, k):
        i, _ = ij(g0, g1)
        col = i if ta else k
        col = col + jnp.where(col >= gap0 // ta_col, gapw // ta_col, 0) if gapw else col
        return (k, col) if ta else (i, col)

    def b_map(g0, g1, k):
        _, j = ij(g0, g1)
        return (j, k) if tb else (k, j)

    def o_map(g0, g1, k):
        i, j = ij(g0, g1)
        return (o_row(i) if o_row else i, j)

    in_specs = [pl.BlockSpec((tk, tm) if ta else (tm, tk), a_map),
                pl.BlockSpec((tn, tk) if tb else (tk, tn), b_map)]
    args = [a, b]
    if has_add:
        in_specs.append(pl.BlockSpec((tm, tn), o_map))
        args.append(add)
    scratch = [pltpu.VMEM((tm, tn), F32)] if nk > 1 else []
    osz = jnp.dtype(out_dtype).itemsize
    est = (2 * (tm * tk * a.dtype.itemsize + tk * tn * b.dtype.itemsize) + 2 * tm * tn * osz
           + (2 * tm * tn * add.dtype.itemsize if has_add else 0)
           + 3 * tm * tn * 4 + 2 * (tm * tk + tk * tn)) / 2 ** 20 + 4
    out_specs = [pl.BlockSpec((tm, tn), o_map)]
    out_shape = [jax.ShapeDtypeStruct((M, N), out_dtype)]
    aliases = {}
    if side:
        hbm = pl.BlockSpec(memory_space=pltpu.HBM)
        in_specs += [hbm] * s_in
        args += list(side.inputs)
        out_specs += [hbm] * s_out
        out_shape += list(side.out_shapes)
        scratch += [pltpu.SemaphoreType.DMA((side.nsem,)), pltpu.SemaphoreType.DMA((side.nsem,))]
        aliases = {n_in + i: 1 + j for i, j in side.aliases}
    outs = pl.pallas_call(
        body, grid=grid, in_specs=in_specs, out_specs=out_specs, out_shape=out_shape, scratch_shapes=scratch,
        input_output_aliases=aliases, name=name,
        compiler_params=_params(("arbitrary", "arbitrary", "arbitrary"), est))(*args)
    return (outs[0], list(outs[1:])) if side else outs[0]


class _Side:
    def __init__(self, inputs, out_shapes, nsem, make, aliases=()):
        self.inputs, self.out_shapes, self.nsem, self.make, self.aliases = inputs, out_shapes, nsem, make, aliases


def _rms_fwd(x, w, *, name):
    T = x.shape[0]
    tm = min(512, T)

    def body(x_ref, w_ref, o_ref):
        xv = x_ref[...]
        r = lax.rsqrt(jnp.mean(xv * xv, axis=-1, keepdims=True) + EPS)
        o_ref[...] = (xv * r * w_ref[...]).astype(BF16)

    return pl.pallas_call(
        body, grid=(T // tm,),
        in_specs=[pl.BlockSpec((tm, D), lambda i: (i, 0)), pl.BlockSpec((1, D), lambda i: (0, 0))],
        out_specs=pl.BlockSpec((tm, D), lambda i: (i, 0)),
        out_shape=jax.ShapeDtypeStruct((T, D), BF16), name=name,
        compiler_params=_params(("arbitrary",), 24))(x, w)


def _rms_bwd(x, w, dy, dres, *, name):
    T = x.shape[0]
    tm = min(512, T)

    def body(x_ref, w_ref, dy_ref, dres_ref, dx_ref, dw_ref):
        xv = x_ref[...]
        r = lax.rsqrt(jnp.mean(xv * xv, axis=-1, keepdims=True) + EPS)
        xhat = xv * r
        dyv = dy_ref[...].astype(F32)
        g = dyv * w_ref[...]
        dx_ref[...] = dres_ref[...] + r * (g - xhat * jnp.mean(g * xhat, axis=-1, keepdims=True))
        part = jnp.sum(dyv * xhat, axis=0, keepdims=True)

        @pl.when(pl.program_id(0) == 0)
        def _():
            dw_ref[...] = part

        @pl.when(pl.program_id(0) > 0)
        def _():
            dw_ref[...] += part

    row = pl.BlockSpec((tm, D), lambda i: (i, 0))
    vec = pl.BlockSpec((1, D), lambda i: (0, 0))
    return pl.pallas_call(
        body, grid=(T // tm,), in_specs=[row, vec, row, row], out_specs=[row, vec],
        out_shape=[jax.ShapeDtypeStruct((T, D), F32), jax.ShapeDtypeStruct((1, D), F32)], name=name,
        compiler_params=_params(("arbitrary",), 32))(x, w, dy, dres)


def _loss_head(h, tgt, w):
    T = h.shape[0]
    tm = min(512, T)

    def body(h_ref, t_ref, w_ref, loss_ref, dh_ref, dw_ref):
        hv = h_ref[...]
        r = lax.rsqrt(jnp.mean(hv * hv, axis=-1, keepdims=True) + EPS)
        xhat = hv * r
        wv = w_ref[...]
        err = xhat * wv - t_ref[...]
        lpart = 0.5 * jnp.sum(jnp.mean(err * err, axis=-1, keepdims=True), axis=0, keepdims=True)
        dy = err * (1.0 / D)
        g = dy * wv
        dh_ref[...] = r * (g - xhat * jnp.mean(g * xhat, axis=-1, keepdims=True))
        wpart = jnp.sum(dy * xhat, axis=0, keepdims=True)
        lrow = jnp.broadcast_to(lpart, (1, LANE))

        @pl.when(pl.program_id(0) == 0)
        def _():
            dw_ref[...] = wpart
            loss_ref[...] = lrow

        @pl.when(pl.program_id(0) > 0)
        def _():
            dw_ref[...] += wpart
            loss_ref[...] += lrow

    row = pl.BlockSpec((tm, D), lambda i: (i, 0))
    vec = pl.BlockSpec((1, D), lambda i: (0, 0))
    return pl.pallas_call(
        body, grid=(T // tm,), in_specs=[row, row, vec],
        out_specs=[pl.BlockSpec((1, LANE), lambda i: (0, 0)), row, vec],
        out_shape=[jax.ShapeDtypeStruct((1, LANE), F32), jax.ShapeDtypeStruct((T, D), F32),
                   jax.ShapeDtypeStruct((1, D), F32)], name="loss_head",
        compiler_params=_params(("arbitrary",), 32))(h, tgt, w)


_GELU_C = 0.7978845608028654
_GELU_A = 0.044715


def _gelu(x, with_grad=False):
    x2 = x * x
    cx = _GELU_C * x
    t = jnp.tanh(cx * (1.0 + _GELU_A * x2))
    h = 0.5 * (1.0 + t)
    if not with_grad:
        return x * h
    return x * h, h + 0.5 * cx * (1.0 - t * t) * (1.0 + 3.0 * _GELU_A * x2)


def _gmlp_mask():
    r = lax.broadcasted_iota(jnp.int32, (GB, GB), 0) // GCH
    c = lax.broadcasted_iota(jnp.int32, (GB, GB), 1) // GCH
    return c <= r


def _ride(body, n_in, n_out, side, nsteps):
    if not side:
        return body
    s_in, s_out = len(side.inputs), len(side.out_shapes)

    def outer(*refs):
        ins, side_ins = refs[:n_in], refs[n_in:n_in + s_in]
        outs = refs[n_in + s_in:n_in + s_in + n_out]
        side_refs = (side_ins, refs[n_in + s_in + n_out:n_in + s_in + n_out + s_out]) + tuple(refs[-2:])

        @pl.when(pl.program_id(0) == 0)
        def _():
            for cp in side.make(*side_refs):
                cp.start()

        @pl.when(pl.program_id(0) == nsteps - 1)
        def _():
            for cp in side.make(*side_refs):
                cp.wait()

        body(*ins, *outs)

    return outer


def _ride_args(side):
    if not side:
        return [], [], [], [], []
    hbm = pl.BlockSpec(memory_space=pltpu.HBM)
    sems = [pltpu.SemaphoreType.DMA((side.nsem,)), pltpu.SemaphoreType.DMA((side.nsem,))]
    return [hbm] * len(side.inputs), list(side.inputs), [hbm] * len(side.out_shapes), list(side.out_shapes), sems


def _gmlp_fwd(proj, lnw, lnb, ws, bst, side=None):
    T = proj.shape[0]
    tm = min(512, T)
    nblk = tm // GB

    def body(u_ref, v_ref, lnw_ref, lnb_ref, ws_ref, bst_ref, o_ref):
        mask = _gmlp_mask()
        u = _gelu(u_ref[...].astype(F32))
        v = _gelu(v_ref[...].astype(F32))
        for g in range(GG):
            cs = slice(g * GD, (g + 1) * GD)
            vg = v[:, cs]
            mu = jnp.mean(vg, axis=-1, keepdims=True)
            vc = vg - mu
            var = jnp.mean(vc * vc, axis=-1, keepdims=True)
            vn = (vc * lax.rsqrt(var + EPS) * lnw_ref[g:g + 1, :] + lnb_ref[g:g + 1, :]).astype(BF16)
            wsg = jnp.where(mask, ws_ref[g], 0.0).astype(BF16)
            bcol = bst_ref[:, g:g + 1]
            for blk in range(nblk):
                rs = slice(blk * GB, (blk + 1) * GB)
                sv = _dot(wsg, vn[rs, :]) + bcol
                o_ref[rs, cs] = (u[rs, cs] * sv).astype(BF16)

    full = lambda shape: pl.BlockSpec(shape, lambda i: tuple(0 for _ in shape))
    x_in, x_args, x_out, x_shapes, x_scratch = _ride_args(side)
    outs = pl.pallas_call(
        _ride(body, 6, 1, side, T // tm), grid=(T // tm,),
        in_specs=[pl.BlockSpec((tm, GW), lambda i: (i, 2)), pl.BlockSpec((tm, GW), lambda i: (i, 3)),
                  full((GG, GD)), full((GG, GD)), full((GG, GB, GB)), full((GB, GG))] + x_in,
        out_specs=[pl.BlockSpec((tm, GW), lambda i: (i, 0))] + x_out,
        out_shape=[jax.ShapeDtypeStruct((T, GW), BF16)] + x_shapes, scratch_shapes=x_scratch,
        input_output_aliases={6 + i: 1 + j for i, j in (side.aliases if side else ())}, name="gmlp_fwd",
        compiler_params=_params(("arbitrary",), 40))(proj, proj, lnw, lnb, ws, bst, *x_args)
    return outs[0], list(outs[1:])


def _gmlp_bwd(proj, dya, dproj, lnw, lnb, ws, bst, side=None):
    T = proj.shape[0]
    tm = min(512, T)
    nblk = tm // GB
    n_in, n_out = 8, 5

    def body(u_ref, v_ref, dya_ref, dproj_in, lnw_ref, lnb_ref, ws_ref, bst_ref,
             dz_ref, dlnw_ref, dlnb_ref, dws_ref, dbst_ref):
        del dproj_in
        first = pl.program_id(0) == 0

        @pl.when(first)
        def _():
            dlnw_ref[...] = jnp.zeros_like(dlnw_ref)
            dlnb_ref[...] = jnp.zeros_like(dlnb_ref)
            dws_ref[...] = jnp.zeros_like(dws_ref)
            dbst_ref[...] = jnp.zeros_like(dbst_ref)

        mask = _gmlp_mask()
        lane = lax.broadcasted_iota(jnp.int32, (GB, LANE), 1)
        ur = u_ref[...].astype(F32)
        vr = v_ref[...].astype(F32)
        u, gu = _gelu(ur, with_grad=True)
        v, gv = _gelu(vr, with_grad=True)
        dy = dya_ref[...].astype(F32)
        dbst = jnp.zeros((GB, LANE), F32)
        dlnw_rows, dlnb_rows = [], []
        for g in range(GG):
            cs = slice(g * GD, (g + 1) * GD)
            vg = v[:, cs]
            mu = jnp.mean(vg, axis=-1, keepdims=True)
            vc = vg - mu
            var = jnp.mean(vc * vc, axis=-1, keepdims=True)
            rstd = lax.rsqrt(var + EPS)
            xhat = vc * rstd
            lw = lnw_ref[g:g + 1, :]
            vn = (xhat * lw + lnb_ref[g:g + 1, :]).astype(BF16)
            wsg = jnp.where(mask, ws_ref[g], 0.0).astype(BF16)
            bcol = bst_ref[:, g:g + 1]
            dyg = dy[:, cs]
            ug = u[:, cs]
            dsv = dyg * ug
            dsv_b = dsv.astype(BF16)
            dws_g = jnp.zeros((GB, GB), F32)
            bsum = jnp.zeros((GB, 1), F32)
            dvn_parts = []
            for blk in range(nblk):
                rs = slice(blk * GB, (blk + 1) * GB)
                sv = _dot(wsg, vn[rs, :]) + bcol
                dz_ref[rs, cs] = (dyg[rs, :] * sv * gu[rs, cs]).astype(BF16)
                dws_g = dws_g + _dot(dsv_b[rs, :], vn[rs, :], NT)
                bsum = bsum + jnp.sum(dsv[rs, :], axis=-1, keepdims=True)
                dvn_parts.append(_dot(wsg, dsv_b[rs, :], TN))
            dvn = jnp.concatenate(dvn_parts, axis=0)
            dws_ref[g] += jnp.where(mask, dws_g, 0.0)
            dbst = dbst + jnp.where(lane == g, bsum, 0.0)
            dlnw_rows.append(jnp.sum(dvn * xhat, axis=0, keepdims=True))
            dlnb_rows.append(jnp.sum(dvn, axis=0, keepdims=True))
            dxh = dvn * lw
            dvg = rstd * (dxh - jnp.mean(dxh, axis=-1, keepdims=True)
                          - xhat * jnp.mean(dxh * xhat, axis=-1, keepdims=True))
            dz_ref[:, GW + g * GD:GW + (g + 1) * GD] = (dvg * gv[:, cs]).astype(BF16)
        dlnw_ref[...] += jnp.concatenate(dlnw_rows, axis=0)
        dlnb_ref[...] += jnp.concatenate(dlnb_rows, axis=0)
        dbst_ref[...] += dbst

    full = lambda shape: pl.BlockSpec(shape, lambda i: tuple(0 for _ in shape))
    x_in, x_args, x_out, x_shapes, x_scratch = _ride_args(side)
    outs = pl.pallas_call(
        _ride(body, n_in, n_out, side, T // tm), grid=(T // tm,),
        in_specs=[pl.BlockSpec((tm, GW), lambda i: (i, 2)), pl.BlockSpec((tm, GW), lambda i: (i, 3)),
                  pl.BlockSpec((tm, GW), lambda i: (i, 0)), pl.BlockSpec(memory_space=pl.ANY),
                  full((GG, GD)), full((GG, GD)), full((GG, GB, GB)), full((GB, GG))] + x_in,
        out_specs=[pl.BlockSpec((tm, 2 * GW), lambda i: (i, 1)), full((GG, GD)), full((GG, GD)),
                   full((GG, GB, GB)), full((GB, LANE))] + x_out,
        out_shape=[jax.ShapeDtypeStruct(dproj.shape, dproj.dtype), jax.ShapeDtypeStruct((GG, GD), F32),
                   jax.ShapeDtypeStruct((GG, GD), F32), jax.ShapeDtypeStruct((GG, GB, GB), F32),
                   jax.ShapeDtypeStruct((GB, LANE), F32)] + x_shapes,
        scratch_shapes=x_scratch, input_output_aliases={3: 0}, name="gmlp_bwd",
        compiler_params=_params(("arbitrary",), 48))(proj, proj, dya, dproj, lnw, lnb, ws, bst, *x_args)
    return tuple(outs[:n_out]), list(outs[n_out:])


def _merge_fwd(ya_pre, yb_pre, proj, bias, wpa, wpb):
    T = proj.shape[0]
    tm = min(512, T)

    def body(ya_ref, yb_ref, g_ref, b_ref, wpa_ref, wpb_ref, m_ref, oa_ref, ob_ref):
        ya = _dot(ya_ref[...], wpa_ref[...])
        yb = _dot(yb_ref[...], wpb_ref[...])
        g = g_ref[...].astype(F32)
        sa = _sigmoid(g[:, :D] + b_ref[0:1, :])
        sb = _sigmoid(g[:, D:] + b_ref[1:2, :])
        m_ref[...] = (sa * ya + sb * yb).astype(BF16)
        oa_ref[...] = ya.astype(BF16)
        ob_ref[...] = yb.astype(BF16)

    row = lambda w: pl.BlockSpec((tm, w), lambda i: (i, 0))
    full = lambda shape: pl.BlockSpec(shape, lambda i: tuple(0 for _ in shape))
    o = jax.ShapeDtypeStruct((T, D), BF16)
    return pl.pallas_call(
        body, grid=(T // tm,),
        in_specs=[row(GW), row(SI), row(2 * D), full((2, D)), full((GW, D)), full((SI, D))],
        out_specs=[row(D), row(D), row(D)], out_shape=[o, o, o], name="merge_fwd",
        compiler_params=_params(("arbitrary",), 40))(ya_pre, yb_pre, proj, bias, wpa, wpb)


def _merge_bwd(dm, proj, bias, ya, yb, wpa, wpb):
    T = proj.shape[0]
    tm = min(512, T)

    def body(dm_ref, g_ref, b_ref, ya_ref, yb_ref, wpa_ref, wpb_ref,
             dg_ref, dya_ref, dyb_ref, dpa_ref, dpb_ref, db_ref):
        dmv = dm_ref[...].astype(F32)
        g = g_ref[...].astype(F32)
        sa = _sigmoid(g[:, :D] + b_ref[0:1, :])
        sb = _sigmoid(g[:, D:] + b_ref[1:2, :])
        dya = (dmv * sa).astype(BF16)
        dyb = (dmv * sb).astype(BF16)
        dga = dmv * ya_ref[...].astype(F32) * sa * (1.0 - sa)
        dgb = dmv * yb_ref[...].astype(F32) * sb * (1.0 - sb)
        dg_ref[:, :D] = dga.astype(BF16)
        dg_ref[:, D:] = dgb.astype(BF16)
        dya_ref[...] = dya
        dyb_ref[...] = dyb
        dpa_ref[...] = _dot(dya, wpa_ref[...], NT).astype(BF16)
        dpb_ref[...] = _dot(dyb, wpb_ref[...], NT).astype(BF16)
        part = jnp.concatenate([jnp.sum(dga, axis=0, keepdims=True), jnp.sum(dgb, axis=0, keepdims=True)], axis=0)

        @pl.when(pl.program_id(0) == 0)
        def _():
            db_ref[...] = part

        @pl.when(pl.program_id(0) > 0)
        def _():
            db_ref[...] += part

    row = lambda w: pl.BlockSpec((tm, w), lambda i: (i, 0))
    full = lambda shape: pl.BlockSpec(shape, lambda i: tuple(0 for _ in shape))
    o = lambda w: jax.ShapeDtypeStruct((T, w), BF16)
    return pl.pallas_call(
        body, grid=(T // tm,),
        in_specs=[row(D), row(2 * D), full((2, D)), row(D), row(D), full((GW, D)), full((SI, D))],
        out_specs=[row(2 * D), row(D), row(D), row(GW), row(SI), full((2, D))],
        out_shape=[o(DP_COLS), o(D), o(D), o(GW), o(SI), jax.ShapeDtypeStruct((2, D), F32)], name="merge_bwd",
        compiler_params=_params(("arbitrary",), 48))(dm, proj, bias, ya, yb, wpa, wpb)


RB = 128


def _shift_matrix(j):
    r = lax.broadcasted_iota(jnp.int32, (RB, RB), 0)
    c = lax.broadcasted_iota(jnp.int32, (RB, RB), 1)
    return jnp.where(c == r - j, 1.0, 0.0).astype(BF16)


def _rows_down(xb, before, shifts):
    H = SUBLANE
    mats = [_shift_matrix(j) for j in shifts]
    outs = [[] for _ in shifts]
    for b in range(xb.shape[0] // RB):
        blk = xb[b * RB:(b + 1) * RB]
        edge = jnp.concatenate([before, blk[:2 * H].astype(F32)[:H]], axis=0)
        for i, j in enumerate(shifts):
            outs[i] += [edge[H - j:2 * H - j], _dot(mats[i], blk)[H:]]
        before = blk[RB - 2 * H:].astype(F32)[H:]
    return [jnp.concatenate(o, axis=0) for o in outs]


def _rows_up(xb, after, shifts):
    H = SUBLANE
    nb = xb.shape[0] // RB
    mats = [_shift_matrix(-j) for j in shifts]
    outs = [[] for _ in shifts]
    for b in range(nb):
        blk = xb[b * RB:(b + 1) * RB]
        nxt = xb[(b + 1) * RB:(b + 1) * RB + 2 * H].astype(F32)[:H] if b + 1 < nb else after
        edge = jnp.concatenate([blk[RB - 2 * H:].astype(F32)[H:], nxt], axis=0)
        for i, j in enumerate(shifts):
            outs[i] += [_dot(mats[i], blk)[:RB - H], edge[j:H + j]]
    return [jnp.concatenate(o, axis=0) for o in outs]


def _ffn_act_fwd(up, cw, cb):
    T = up.shape[0]
    tm = min(512, T)
    H = SUBLANE

    def body(up_ref, cw_ref, cb_ref, o_ref, xc_ref, halo):
        @pl.when(pl.program_id(1) == 0)
        def _():
            halo[...] = jnp.zeros_like(halo)

        xb = up_ref[...]
        x2, x1 = _rows_down(xb, halo[...], (2, 1))
        xc = cb_ref[...] + cw_ref[0:1, :] * x2 + cw_ref[1:2, :] * x1 + cw_ref[2:3, :] * xb.astype(F32)
        xc_ref[...] = xc.astype(BF16)
        gate = xc[:, :FT]
        o_ref[...] = (gate * _sigmoid(gate) * xc[:, FT:]).astype(BF16)
        halo[...] = xb[tm - 2 * H:].astype(F32)[H:]

    tile = pl.BlockSpec((tm, 2 * FT), lambda j, i: (i, j))
    return pl.pallas_call(
        body, grid=(2, T // tm),
        in_specs=[tile, pl.BlockSpec((FK, 2 * FT), lambda j, i: (0, j)), pl.BlockSpec((1, 2 * FT), lambda j, i: (0, j))],
        out_specs=[pl.BlockSpec((tm, FT), lambda j, i: (i, j)), tile],
        out_shape=[jax.ShapeDtypeStruct((T, DFF), BF16), jax.ShapeDtypeStruct((T, 2 * DFF), BF16)],
        scratch_shapes=[pltpu.VMEM((H, 2 * FT), F32)], name="ffn_act_fwd",
        compiler_params=_params(("arbitrary", "arbitrary"), 48))(up, cw, cb)


def _ffn_act_bwd(up, xc, dact, cw):
    T = up.shape[0]
    tm = min(512, T)
    nt = T // tm
    H = SUBLANE

    def body(up_ref, xc_ref, da_ref, cw_ref, dup_ref, dcw_ref, dcb_ref, ahead):
        @pl.when(pl.program_id(1) == 0)
        def _():
            ahead[...] = jnp.zeros_like(ahead)
            dcw_ref[...] = jnp.zeros_like(dcw_ref)
            dcb_ref[...] = jnp.zeros_like(dcb_ref)

        xcv = xc_ref[...].astype(F32)
        gate, val = xcv[:, :FT], xcv[:, FT:]
        sg = _sigmoid(gate)
        dav = da_ref[...].astype(F32)
        dgate = dav * val * sg * (1.0 + gate * (1.0 - sg))
        dval = dav * gate * sg
        dxc = jnp.concatenate([dgate, dval], axis=1)
        d1, d2 = _rows_up(dxc.astype(BF16), ahead[...], (1, 2))
        x = up_ref[...].astype(F32)
        dcb_ref[...] += jnp.sum(dxc, axis=0, keepdims=True)
        dcw_ref[...] += jnp.concatenate([jnp.sum(d * x, axis=0, keepdims=True) for d in (d2, d1, dxc)], axis=0)
        dup_ref[...] = (cw_ref[2:3, :] * dxc + cw_ref[1:2, :] * d1 + cw_ref[0:1, :] * d2).astype(BF16)
        ahead[...] = dxc[0:H, :]

    tile = pl.BlockSpec((tm, 2 * FT), lambda j, i: (nt - 1 - i, j))
    return pl.pallas_call(
        body, grid=(2, nt),
        in_specs=[tile, tile, pl.BlockSpec((tm, FT), lambda j, i: (nt - 1 - i, j)),
                  pl.BlockSpec((FK, 2 * FT), lambda j, i: (0, j))],
        out_specs=[tile, pl.BlockSpec((FK, 2 * FT), lambda j, i: (0, j)), pl.BlockSpec((1, 2 * FT), lambda j, i: (0, j))],
        out_shape=[jax.ShapeDtypeStruct((T, 2 * DFF), BF16), jax.ShapeDtypeStruct((FK, 2 * DFF), F32),
                   jax.ShapeDtypeStruct((1, 2 * DFF), F32)],
        scratch_shapes=[pltpu.VMEM((H, 2 * FT), F32)], name="ffn_act_bwd",
        compiler_params=_params(("arbitrary", "arbitrary"), 56))(up, xc, dact, cw)


def _softplus(x):
    e = jnp.exp(-jnp.abs(x))
    return jnp.maximum(x, 0.0) + jnp.where(e < 1e-4, e * (1.0 - 0.5 * e), jnp.log(1.0 + e))


def _ssd_tril():
    li = lax.broadcasted_iota(jnp.int32, (LS, LS), 0)
    si = lax.broadcasted_iota(jnp.int32, (LS, LS), 1)
    return si <= li


def _head_expansion():
    hh = lax.broadcasted_iota(jnp.int32, (LANE, SI), 0)
    cc = lax.broadcasted_iota(jnp.int32, (LANE, SI), 1) // SP
    return jnp.where(hh == cc, 1.0, 0.0).astype(BF16)


def _ssd_pre(xc, dt_ref, dtb_ref, alog_ref, tril, expand):
    sx = _sigmoid(xc)
    xbc = xc * sx
    xs, bm, cm = xbc[:, :SI], xbc[:, SI:SI + SG * SN], xbc[:, SI + SG * SN:]
    dtin = dt_ref[...] + dtb_ref[...]
    dt = _softplus(dtin)
    a_neg = -jnp.exp(alog_ref[...])
    dta = dt * a_neg
    trilb = jnp.where(tril, 1.0, 0.0).astype(BF16)
    a = _dot3_rhs(trilb, dta, NN)
    a_exp = _dot3(a, expand, NN)
    dt_exp = _dot2(dt, expand, NN)
    xdt = xs * dt_exp
    a_last = a_exp[LS - 1:LS, :]
    return dict(xc=xc, sx=sx, xs=xs, bm=bm, cm=cm, dtin=dt_ref[...] + dtb_ref[...], dt=dt, a_neg=a_neg,
                a=a, a_t=a.T, a_exp=a_exp, dt_exp=dt_exp, xdt=xdt, ea=jnp.exp(a_exp),
                w=jnp.exp(a_last - a_exp), eal=jnp.exp(a_last))


def _head_decay(pre, tril, h):
    seg = pre["a"][:, h:h + 1] - pre["a_t"][h:h + 1, :]
    return jnp.exp(jnp.where(tril, seg, -1e30))


def _ssd_fwd(proj, dtraw, cw, cb, dtb, alog, dexp, nw):
    T = proj.shape[0]
    nc = T // LS
    H = SUBLANE

    def body(z_ref, x_ref, dt_ref, cw_ref, cb_ref, dtb_ref, alog_ref, dexp_ref, nw_ref, ex_ref,
             yb_ref, y_ref, sp_ref, xc_ref, halo, st):
        @pl.when(pl.program_id(0) == 0)
        def _():
            halo[...] = jnp.zeros_like(halo)
            st[...] = jnp.zeros_like(st)

        xb = x_ref[...]
        taps = _rows_down(xb, halo[...], (3, 2, 1)) + [xb.astype(F32)]
        xc = cb_ref[...]
        for k in range(SK):
            xc = xc + cw_ref[k:k + 1, :] * taps[k]
        xc_ref[...] = xc.astype(BF16)
        tril, expand = _ssd_tril(), ex_ref[...]
        pre = _ssd_pre(xc, dt_ref, dtb_ref, alog_ref, tril, expand)
        lane = lax.broadcasted_iota(jnp.int32, (LS, LANE), 1)
        lo = lane < SP
        zf = z_ref[...].astype(F32)
        siluz = zf * _sigmoid(zf)
        for g in range(SG):
            gs = slice(g * SGW, (g + 1) * SGW)
            bg = pre["bm"][:, g * SN:(g + 1) * SN].astype(BF16)
            cg = pre["cm"][:, g * SN:(g + 1) * SN].astype(BF16)
            gmat = _dot(cg, bg, NT)
            sg = st[g]
            sp_ref[0, g] = sg
            yoff = _dot(cg, sg.astype(BF16))
            parts = []
            for j in range(SGW // LANE):
                h0 = g * (SGW // SP) + 2 * j
                m0 = gmat * _head_decay(pre, tril, h0)
                m1 = gmat * _head_decay(pre, tril, h0 + 1)
                xp = pre["xdt"][:, g * SGW + j * LANE:g * SGW + (j + 1) * LANE]
                rhs = jnp.concatenate([jnp.where(lo, xp, 0.0), jnp.where(lo, 0.0, xp)], axis=0).astype(BF16)
                parts.append(_dot(jnp.concatenate([m0, m1], axis=1).astype(BF16), rhs))
            y = (jnp.concatenate(parts, axis=1) + pre["ea"][:, gs] * yoff + dexp_ref[:, gs] * pre["xs"][:, gs])
            st[g] = pre["eal"][:, gs] * sg + _dot(bg, (pre["w"][:, gs] * pre["xdt"][:, gs]).astype(BF16), TN)
            y_ref[:, gs] = y
            yg = y * siluz[:, gs]
            r = lax.rsqrt(jnp.mean(yg * yg, axis=-1, keepdims=True) + EPS)
            yb_ref[:, gs] = (yg * r * nw_ref[:, gs]).astype(BF16)
        halo[...] = xb[LS - 2 * H:].astype(F32)[H:]

    vec = lambda w: pl.BlockSpec((1, w), lambda c: (0, 0))
    return pl.pallas_call(
        body, grid=(nc,),
        in_specs=[pl.BlockSpec((LS, SI), lambda c: (c, 2)), pl.BlockSpec((LS, SXBC), lambda c: (c, 2)),
                  pl.BlockSpec((LS, LANE), lambda c: (c, 0)),
                  pl.BlockSpec((SK, SXBC), lambda c: (0, 0)), vec(SXBC), vec(LANE), vec(LANE), vec(SI), vec(SI),
                  pl.BlockSpec((LANE, SI), lambda c: (0, 0))],
        out_specs=[pl.BlockSpec((LS, SI), lambda c: (c, 0)), pl.BlockSpec((LS, SI), lambda c: (c, 0)),
                   pl.BlockSpec((1, SG, SN, SGW), lambda c: (c, 0, 0, 0)), pl.BlockSpec((LS, SXBC), lambda c: (c, 0))],
        out_shape=[jax.ShapeDtypeStruct((T, SI), BF16), jax.ShapeDtypeStruct((T, SI), F32),
                   jax.ShapeDtypeStruct((nc, SG, SN, SGW), F32), jax.ShapeDtypeStruct((T, SXBC), BF16)],
        scratch_shapes=[pltpu.VMEM((H, SXBC), F32), pltpu.VMEM((SG, SN, SGW), F32)], name="ssd_fwd",
        compiler_params=_params(("arbitrary",), VMEM_CAP_MB))(
            proj, proj, dtraw, cw, cb, dtb, alog, dexp, nw, _head_expansion())


def _ssd_bwd(proj, xcs, dtraw, y, sprev, dyb, dproj, cw, dtb, alog, dexp, nw):
    T = proj.shape[0]
    nc = T // LS
    H = SUBLANE
    NJ = 1

    def body(z_ref, x_ref, xc_ref, dt_ref, y_ref, sp_ref, dyb_ref, dproj_in,
             cw_ref, dtb_ref, alog_ref, dexp_ref, nw_ref, ex_ref,
             dp_ref, ddt_ref, dcw_ref, dcb_ref, ddtb_ref, da_ref, dd_ref, dnw_ref,
             ahead, ds, stage):
        del dproj_in
        i = pl.program_id(0)
        j = pl.program_id(1)

        @pl.when(jnp.logical_and(i == 0, j == 0))
        def _():
            ahead[...] = jnp.zeros_like(ahead)
            ds[...] = jnp.zeros_like(ds)
            for r in (dcw_ref, dcb_ref, ddtb_ref, da_ref, dd_ref, dnw_ref):
                r[...] = jnp.zeros_like(r)

        @pl.when(j == 0)
        def _():
            tril, expand = _ssd_tril(), ex_ref[...]
            pre = _ssd_pre(xc_ref[...].astype(F32), dt_ref, dtb_ref, alog_ref, tril, expand)
            lane = lax.broadcasted_iota(jnp.int32, (LS, LANE), 1)
            sub = lax.broadcasted_iota(jnp.int32, (LANE, LS), 0)
            rowi = lax.broadcasted_iota(jnp.int32, (LS, 1), 0)
            lo = lane < SP
            xs, xdt, ea, w, eal = pre["xs"], pre["xdt"], pre["ea"], pre["w"], pre["eal"]

            zf = z_ref[...].astype(F32)
            sz = _sigmoid(zf)
            siluz = zf * sz
            yv = y_ref[...]
            yg = yv * siluz
            dout = dyb_ref[...].astype(F32)
            dyg_parts, dnw_parts = [], []
            for g in range(SG):
                gs = slice(g * SGW, (g + 1) * SGW)
                ygg = yg[:, gs]
                r = lax.rsqrt(jnp.mean(ygg * ygg, axis=-1, keepdims=True) + EPS)
                yhat = ygg * r
                dn = dout[:, gs] * nw_ref[:, gs]
                dnw_parts.append(jnp.sum(dout[:, gs] * yhat, axis=0, keepdims=True))
                dyg_parts.append(r * (dn - yhat * jnp.mean(dn * yhat, axis=-1, keepdims=True)))
            dyg = jnp.concatenate(dyg_parts, axis=1)
            dnw_ref[...] += jnp.concatenate(dnw_parts, axis=1)
            dy = dyg * siluz
            stage[:, 0:SI] = (dyg * yv * sz * (1.0 + zf * (1.0 - sz))).astype(BF16)
            dd_ref[...] += jnp.sum(dy * xs, axis=0, keepdims=True)
            tt = ea * dy

            da_rows = jnp.zeros((LS, LANE), F32)
            da_cols = jnp.zeros((LANE, LS), F32)
            dxdt_parts, db_parts, dc_parts, daexp_parts = [], [], [], []
            for g in range(SG):
                gs = slice(g * SGW, (g + 1) * SGW)
                bg = pre["bm"][:, g * SN:(g + 1) * SN].astype(BF16)
                cg = pre["cm"][:, g * SN:(g + 1) * SN].astype(BF16)
                sg = sp_ref[0, g]
                sgb = sg.astype(BF16)
                dsg = ds[g]
                dsgb = dsg.astype(BF16)
                ttg = tt[:, gs].astype(BF16)
                yoff = _dot(cg, sgb)
                dc = _dot(ttg, sgb, NT)
                gmat = _dot(cg, bg, NT)
                dgm = jnp.zeros((LS, LS), F32)
                dxdt_pairs = []
                for jj in range(SGW // LANE):
                    h0 = g * (SGW // SP) + 2 * jj
                    ps = slice(g * SGW + jj * LANE, g * SGW + (jj + 1) * LANE)
                    l0 = _head_decay(pre, tril, h0)
                    l1 = _head_decay(pre, tril, h0 + 1)
                    m0 = gmat * l0
                    m1 = gmat * l1
                    dyp = dy[:, ps]
                    dy_lo = jnp.where(lo, dyp, 0.0).astype(BF16)
                    dy_hi = jnp.where(lo, 0.0, dyp).astype(BF16)
                    xpb = xdt[:, ps].astype(BF16)
                    dm0 = _dot(dy_lo, xpb, NT)
                    dm1 = _dot(dy_hi, xpb, NT)
                    q0 = dm0 * m0
                    q1 = dm1 * m1
                    da_rows = da_rows + jnp.where(lane == h0, jnp.sum(q0, axis=1, keepdims=True), 0.0)
                    da_rows = da_rows + jnp.where(lane == h0 + 1, jnp.sum(q1, axis=1, keepdims=True), 0.0)
                    da_cols = da_cols + jnp.where(sub == h0, jnp.sum(q0, axis=0, keepdims=True), 0.0)
                    da_cols = da_cols + jnp.where(sub == h0 + 1, jnp.sum(q1, axis=0, keepdims=True), 0.0)
                    dgm = dgm + dm0 * l0 + dm1 * l1
                    mcat = jnp.concatenate([m0, m1], axis=0).astype(BF16)
                    dycat = jnp.concatenate([dy_lo, dy_hi], axis=0)
                    dxdt_pairs.append(_dot(mcat, dycat, TN))
                dgb = dgm.astype(BF16)
                dc = dc + _dot(dgb, bg)
                db = _dot(dgb, cg, TN)
                zg = _dot(bg, dsgb)
                wg, xdtg = w[:, gs], xdt[:, gs]
                dxdt_g = jnp.concatenate(dxdt_pairs, axis=1) + wg * zg
                qg = zg * xdtg * wg
                last = (jnp.sum(qg, axis=0, keepdims=True)
                        + jnp.sum(dsg * sg, axis=0, keepdims=True) * eal[:, gs])
                daexp_parts.append(dy[:, gs] * ea[:, gs] * yoff - qg + jnp.where(rowi == LS - 1, last, 0.0))
                db = db + _dot((wg * xdtg).astype(BF16), dsgb, NT)
                ds[g] = eal[:, gs] * dsg + _dot(cg, ttg, TN)
                dxdt_parts.append(dxdt_g)
                db_parts.append(db)
                dc_parts.append(dc)
            dxdt = jnp.concatenate(dxdt_parts, axis=1)
            da_exp = jnp.concatenate(daexp_parts, axis=1)
            da = _dot2(da_exp, expand, NT) + da_rows - da_cols.T
            triub = jnp.where(tril, 1.0, 0.0).astype(BF16)
            ddta = _dot3_rhs(triub, da, TN)
            ddt = ddta * pre["a_neg"] + _dot2(dxdt * xs, expand, NT)
            da_ref[...] += jnp.sum(ddta * pre["dt"], axis=0, keepdims=True)
            ddt_raw = ddt * _sigmoid(pre["dtin"])
            ddt_ref[...] = ddt_raw
            ddtb_ref[...] += jnp.sum(ddt_raw, axis=0, keepdims=True)
            dxs = dexp_ref[...] * dy + dxdt * pre["dt_exp"]
            dxbc = jnp.concatenate([dxs] + db_parts + dc_parts, axis=1)
            sx, xc = pre["sx"], pre["xc"]
            dxc = dxbc * sx * (1.0 + xc * (1.0 - sx))
            taps = _rows_up(dxc.astype(BF16), ahead[...], (3, 2, 1)) + [dxc]
            xr = x_ref[...].astype(F32)
            dcb_ref[...] += jnp.sum(dxc, axis=0, keepdims=True)
            dcw_ref[...] += jnp.concatenate([jnp.sum(t * xr, axis=0, keepdims=True) for t in taps], axis=0)
            dxr = cw_ref[0:1, :] * taps[0]
            for k in range(1, SK):
                dxr = dxr + cw_ref[k:k + 1, :] * taps[k]
            stage[:, SI:] = dxr.astype(BF16)
            ahead[...] = dxc[0:H, :]

        dp_ref[...] = stage[...]

    vec = lambda w: pl.BlockSpec((1, w), lambda i, j: (0, 0))
    rev = lambda w, cb_: pl.BlockSpec((LS, w), lambda i, j: (nc - 1 - i, cb_))
    outs = pl.pallas_call(
        body, grid=(nc, NJ),
        in_specs=[rev(SI, 2), rev(SXBC, 2), rev(SXBC, 0),
                  rev(LANE, 0), rev(SI, 0),
                  pl.BlockSpec((1, SG, SN, SGW), lambda i, j: (nc - 1 - i, 0, 0, 0)),
                  rev(SI, 0), pl.BlockSpec(memory_space=pl.ANY),
                  pl.BlockSpec((SK, SXBC), lambda i, j: (0, 0)), vec(LANE), vec(LANE), vec(SI), vec(SI),
                  pl.BlockSpec((LANE, SI), lambda i, j: (0, 0))],
        out_specs=[rev(DP_SSM, 1), rev(LANE, 0),
                   pl.BlockSpec((SK, SXBC), lambda i, j: (0, 0)), vec(SXBC), vec(LANE), vec(LANE), vec(SI), vec(SI)],
        out_shape=[jax.ShapeDtypeStruct(dproj.shape, dproj.dtype), jax.ShapeDtypeStruct((T, LANE), F32),
                   jax.ShapeDtypeStruct((SK, SXBC), F32), jax.ShapeDtypeStruct((1, SXBC), F32),
                   jax.ShapeDtypeStruct((1, LANE), F32), jax.ShapeDtypeStruct((1, LANE), F32),
                   jax.ShapeDtypeStruct((1, SI), F32), jax.ShapeDtypeStruct((1, SI), F32)],
        scratch_shapes=[pltpu.VMEM((H, SXBC), F32),
                        pltpu.VMEM((SG, SN, SGW), F32), pltpu.VMEM((LS, SI + SXBC), BF16)],
        input_output_aliases={7: 0}, name="ssd_bwd",
        compiler_params=_params(("arbitrary", "arbitrary"), VMEM_CAP_MB))(
            proj, proj, xcs, dtraw, y, sprev, dyb, dproj, cw, dtb, alog, dexp, nw, _head_expansion())
    return outs


def _perm_ffn_cols(a):
    lead = a.shape[:-1]
    return a.reshape(lead + (2, 2, FT)).swapaxes(-3, -2).reshape(lead + (2 * DFF,))


def _perm_ffn_rows(a):
    return a.reshape((2, 2, FT) + a.shape[1:]).swapaxes(0, 1).reshape(a.shape)


def _pad_lanes(v, n=LANE):
    return jnp.pad(v, ((0, 0), (0, n - v.shape[-1])))


LATE = ["w_proj_a", "w_proj_b", "w_out", "ffn_w_up_t", "ffn_w_down"]
WGRAD = BF16
SMALL_BF16_FROM = 2 ** 16


class _NoExchange:
    def gather_start(self):
        return None

    def gather_pass_on(self, outs):
        return None

    def late_weights(self, w, outs):
        return w

    def late_to_sibling(self, grads):
        return None

    def reduce_late(self, outs):
        return None

    def w_in_to_sibling(self, grad_main, grad_dt):
        return None

    def reduce_w_in(self, outs):
        return None

    def reduced(self, late_outs, w_in_outs):
        pass


def _local_step(x, tgt, w, hooks=None):
    hooks = hooks or _NoExchange()

    def mm(*args, side=None, **kw):
        out = _matmul(*args, side=side, **kw)
        return out if side is not None else (out, [])

    win_t = w["w_in_t"]
    win_dt = jnp.pad(w["w_in_t"][PMAIN:], ((0, LANE - SH), (0, 0)))
    fcw = _perm_ffn_cols(w["ffn_conv_w"])
    fcb = _perm_ffn_cols(w["ffn_conv_b"][None, :])
    mixw = w["mix_norm_w"][None, :]
    ffnw = w["ffn_norm_w"][None, :]
    finw = w["final_norm_w"][None, :]
    bst = w["gmlp_bs"].T
    scb = w["ssm_conv_b"][None, :]
    dtb = _pad_lanes(w["ssm_dt_bias"][None, :])
    alog = _pad_lanes(w["ssm_a_log"][None, :])
    dexp = jnp.repeat(w["ssm_d"], SP)[None, :]
    snw = w["ssm_norm_w"][None, :]

    xn = _rms_fwd(x, mixw, name="mix_norm")
    proj, got = mm(xn, win_t, name="in_proj", out_dtype=BF16, tb=True, tn=3072, j_outer=True, b_rows=PMAIN,
                   side=hooks.gather_start())
    dtraw = _matmul(xn, win_dt, name="in_proj_dt", out_dtype=F32, tb=True)
    ya_pre, got = _gmlp_fwd(proj, w["gmlp_ln_w"], w["gmlp_ln_b"], w["gmlp_ws"], bst, side=hooks.gather_pass_on(got))
    w = hooks.late_weights(w, got)
    wup = _perm_ffn_rows(w["ffn_w_up_t"])
    yb_pre, y_ssd, sprev, ssm_xc = _ssd_fwd(proj, dtraw, w["ssm_conv_w"], scb, dtb, alog, dexp, snw)
    merged, ya, yb = _merge_fwd(ya_pre, yb_pre, proj, w["gate_bias"], w["w_proj_a"], w["w_proj_b"])
    h1 = _matmul(merged, w["w_out"], name="out_proj", out_dtype=F32, add=x)
    hn = _rms_fwd(h1, ffnw, name="ffn_norm")
    up = _matmul(hn, wup, name="ffn_up", out_dtype=BF16, tb=True, tn=2 * FT, j_outer=True)
    act, ffn_xc = _ffn_act_fwd(up, fcw, fcb)
    h2 = _matmul(act, w["ffn_w_down"], name="ffn_down", out_dtype=F32, tk=DFF, add=h1)

    loss_row, dh2, d_finw = _loss_head(h2, tgt, finw)
    dact = _matmul(dh2, w["ffn_w_down"], name="ffn_down_dx", out_dtype=BF16, tb=True, tn=DFF)
    d_wdown = _matmul(act, dh2, name="ffn_down_dw", out_dtype=WGRAD, ta=True, tm=FT, tk=2048)
    dup, d_fcw, d_fcb = _ffn_act_bwd(up, ffn_xc, dact, fcw)
    dhn = _matmul(dup, wup, name="ffn_up_dx", out_dtype=F32, tk=2 * FT)
    d_wup = _matmul(dup, hn, name="ffn_up_dw", out_dtype=WGRAD, ta=True, tm=FT, tk=2048,
                    o_row=lambda i: (i % 2) * 2 + i // 2)
    dh1, d_ffnw = _rms_bwd(h1, ffnw, dhn, dh2, name="ffn_norm_bwd")
    dmerged = _matmul(dh1, w["w_out"], name="out_proj_dx", out_dtype=BF16, tb=True)
    d_wout = _matmul(merged, dh1, name="out_proj_dw", out_dtype=WGRAD, ta=True, tk=2048)
    dproj, dya, dyb, dya_pre, dyb_pre, d_gbias = _merge_bwd(dmerged, proj, w["gate_bias"], ya, yb,
                                                           w["w_proj_a"], w["w_proj_b"])
    d_wpa = _matmul(ya_pre, dya, name="proj_a_dw", out_dtype=WGRAD, ta=True, tk=2048)
    d_wpb = _matmul(yb_pre, dyb, name="proj_b_dw", out_dtype=WGRAD, ta=True, tk=2048)
    late = {"w_proj_a": d_wpa, "w_proj_b": d_wpb, "w_out": d_wout, "ffn_w_up_t": d_wup,
            "ffn_w_down": d_wdown}
    (dproj, d_lnw, d_lnb, d_ws, d_bst), got = _gmlp_bwd(proj, dya_pre, dproj, w["gmlp_ln_w"], w["gmlp_ln_b"],
                                                        w["gmlp_ws"], bst, side=hooks.late_to_sibling(late))
    dproj, ddt, d_scw, d_scb, d_dtb, d_a, d_dch, d_snw = _ssd_bwd(
        proj, ssm_xc, dtraw, y_ssd, sprev, dyb_pre, dproj, w["ssm_conv_w"], dtb, alog, dexp, snw)
    gap = (2 * D + 2 * GW, DP_GAP)
    d_win_main, late_outs = mm(dproj, xn, name="in_proj_dw", out_dtype=WGRAD, ta=True, a_gap=gap, tk=4096,
                               side=hooks.reduce_late(got))
    d_win_dt = _matmul(ddt, xn, name="in_proj_dt_dw", out_dtype=F32, ta=True)
    d_win_t = jnp.concatenate([d_win_main, d_win_dt[:SH]], axis=0)
    dxn, got = mm(ddt, win_dt, name="in_proj_dt_dx", out_dtype=F32,
                  side=hooks.w_in_to_sibling(d_win_main, d_win_dt[:SH]))
    dxn, w_in_outs = mm(dproj, win_t, name="in_proj_dx", out_dtype=F32, add=dxn, b_rows=PMAIN, a_gap=gap,
                        side=hooks.reduce_w_in(got))
    hooks.reduced(late_outs, w_in_outs)
    grad_x, d_mixw = _rms_bwd(x, mixw, dxn, dh1, name="mix_norm_bwd")

    a_neg = -jnp.exp(w["ssm_a_log"])
    grads = {
        "mix_norm_w": d_mixw[0],
        "w_in_t": d_win_t,
        "gate_bias": d_gbias,
        "gmlp_ln_w": d_lnw, "gmlp_ln_b": d_lnb, "gmlp_ws": d_ws, "gmlp_bs": d_bst[:, :GG].T,
        "ssm_conv_w": d_scw, "ssm_conv_b": d_scb[0],
        "ssm_dt_bias": d_dtb[0, :SH], "ssm_a_log": d_a[0, :SH] * a_neg,
        "ssm_d": d_dch.reshape(SH, SP).sum(axis=-1), "ssm_norm_w": d_snw[0],
        **late,
        "ffn_norm_w": d_ffnw[0],
        "ffn_conv_w": _perm_ffn_cols(d_fcw), "ffn_conv_b": _perm_ffn_cols(d_fcb)[0],
        "ffn_w_down": d_wdown, "final_norm_w": d_finw[0],
    }
    return loss_row, grad_x, grads


MESH = pl.DeviceIdType.MESH
HBM_SPEC = pl.BlockSpec(memory_space=pltpu.HBM)


def _axes():
    return lax.axis_index("x"), lax.axis_index("y"), lax.axis_index("c")


def _all_gather(shards, *, name):
    na = len(shards)

    def body(*refs):
        x_refs, out_refs = refs[:na], refs[na:2 * na]
        send_sems, recv_sems, local_sems = refs[2 * na:]
        x, y, c = _axes()
        me, sibling = (x, y, c), (x, y, 1 - c)
        chips = [(1 - x, y), (x, 1 - y), (1 - x, 1 - y)]

        def slot(a, px, py, pc):
            return out_refs[a].at[4 * px + 2 * py + pc]

        def copy(a, k, block, to, src=None):
            return pltpu.make_async_remote_copy(
                src_ref=slot(a, *block) if src is None else src, dst_ref=slot(a, *block),
                send_sem=send_sems.at[7 * a + k], recv_sem=recv_sems.at[7 * a + k], device_id=to, device_id_type=MESH)

        mine = [pltpu.make_async_copy(x_refs[a], slot(a, *me), local_sems.at[a]) for a in range(na)]
        for cp in mine:
            cp.start()
        first = []
        for a in range(na):
            first.append(copy(a, 0, me, sibling, src=x_refs[a]))
            first += [copy(a, 1 + j, me, (*chip, c), src=x_refs[a]) for j, chip in enumerate(chips)]
        for cp in first:
            cp.start()
        passed = []
        for j, chip in enumerate(chips):
            for a in range(na):
                copy(a, 1 + j, (*chip, c), me).wait_recv()
                cp = copy(a, 4 + j, (*chip, c), sibling)
                cp.start()
                passed.append(cp)
        for a in range(na):
            copy(a, 0, sibling, me).wait_recv()
        for j, chip in enumerate(chips):
            for a in range(na):
                copy(a, 4 + j, (*chip, 1 - c), me).wait_recv()
        for cp in first + passed:
            cp.wait_send()
        for cp in mine:
            cp.wait()

    return pl.pallas_call(
        body, out_shape=[jax.ShapeDtypeStruct((NDEV,) + s.shape, s.dtype) for s in shards],
        in_specs=[HBM_SPEC] * na, out_specs=[HBM_SPEC] * na,
        scratch_shapes=[pltpu.SemaphoreType.DMA((7 * na,)), pltpu.SemaphoreType.DMA((7 * na,)),
                        pltpu.SemaphoreType.DMA((na,))],
        name=name)(*shards)


def _exchange(srcs, plan, *, name):
    na = len(srcs)
    n = len(plan(0, 0, 0))

    def body(*refs):
        src_refs, out_refs = refs[:na], refs[na:2 * na]
        send_sems, recv_sems = refs[2 * na:]
        x, y, c = _axes()
        copies = []
        for k, (slab, peer) in enumerate(plan(x, y, c)):
            for a in range(na):
                cp = pltpu.make_async_remote_copy(
                    src_ref=src_refs[a].at[slab], dst_ref=out_refs[a].at[k], send_sem=send_sems.at[n * a + k],
                    recv_sem=recv_sems.at[n * a + k], device_id=peer, device_id_type=MESH)
                cp.start()
                copies.append(cp)
        for cp in copies:
            cp.wait()

    return pl.pallas_call(
        body, out_shape=[jax.ShapeDtypeStruct((n,) + s.shape[1:], s.dtype) for s in srcs],
        in_specs=[HBM_SPEC] * na, out_specs=[HBM_SPEC] * na,
        scratch_shapes=[pltpu.SemaphoreType.DMA((n * na,)), pltpu.SemaphoreType.DMA((n * na,))], name=name)(*srcs)


def _to_sibling_plan(x, y, c):
    return [(2 * q + (1 - c), (x, y, 1 - c)) for q in range(4)]


def _to_chips_plan(x, y, c):
    q = 2 * x + y
    return [(q ^ 2, (1 - x, y, c)), (q ^ 1, (x, 1 - y, c)), (q ^ 3, (1 - x, 1 - y, c))]


def _row_tile(rows, row_bytes, budget=2 * 2 ** 20, align=2 * SUBLANE):
    if rows * row_bytes <= 2 * budget:
        return rows
    best = None
    for d in range(align, rows + 1, align):
        if rows % d == 0 and d * row_bytes <= budget:
            best = d
    return best or rows


def _pair_add(g, ra, c_idx, *, name):
    _, _, R, C = g.shape
    tr = _row_tile(R, C * 4, budget=3 * 2 ** 20)

    def body(c_ref, g_ref, ra_ref, o_ref):
        del c_ref
        o_ref[...] = (g_ref[0].astype(F32) + ra_ref[...].astype(F32)).astype(o_ref.dtype)

    return pl.pallas_call(
        body,
        grid_spec=pltpu.PrefetchScalarGridSpec(
            num_scalar_prefetch=1, grid=(4, R // tr),
            in_specs=[pl.BlockSpec((1, 1, tr, C), lambda q, r, cr: (q, cr[0], r, 0)),
                      pl.BlockSpec((1, tr, C), lambda q, r, cr: (q, r, 0))],
            out_specs=pl.BlockSpec((1, tr, C), lambda q, r, cr: (q, r, 0))),
        out_shape=jax.ShapeDtypeStruct((4, R, C), g.dtype), name=name,
        compiler_params=_params(("arbitrary", "arbitrary"), 24))(c_idx, g, ra)


def _grad_sum(p, rb, q_idx, *, name):
    _, R, C = p.shape
    tr = _row_tile(R, C * 4, budget=3 * 2 ** 20)

    def body(q_ref, p_ref, rb_ref, o_ref):
        del q_ref
        g = p_ref[0].astype(F32)
        for k in range(3):
            g = g + rb_ref[k].astype(F32)
        o_ref[...] = g

    return pl.pallas_call(
        body,
        grid_spec=pltpu.PrefetchScalarGridSpec(
            num_scalar_prefetch=1, grid=(R // tr,),
            in_specs=[pl.BlockSpec((1, tr, C), lambda r, qr: (qr[0], r, 0)),
                      pl.BlockSpec((3, tr, C), lambda r, qr: (0, r, 0))],
            out_specs=pl.BlockSpec((tr, C), lambda r, qr: (r, 0))),
        out_shape=jax.ShapeDtypeStruct((R, C), F32), name=name,
        compiler_params=_params(("arbitrary",), 40))(q_idx, p, rb)


def _adamw(g, w, m, v):
    m = ADAM_B1 * m + (1.0 - ADAM_B1) * g
    v = ADAM_B2 * v + (1.0 - ADAM_B2) * (g * g)
    m_hat = m / (1.0 - ADAM_B1 ** ADAM_STEP)
    v_hat = v / (1.0 - ADAM_B2 ** ADAM_STEP)
    delta = -ADAM_LR * (m_hat / (jnp.sqrt(v_hat) + ADAM_EPS) + ADAM_WD * w)
    return delta, m, v


def _adam(g, w, m, v, *, name):
    _, R, C = w.shape
    tr = _row_tile(R, C * 4, budget=2 ** 20, align=SUBLANE)

    def body(g_ref, w_ref, m_ref, v_ref, d_out, m_out, v_out):
        delta, mn, vn = _adamw(g_ref[...], w_ref[...], m_ref[...], v_ref[...])
        d_out[...] = delta
        m_out[...] = mn
        v_out[...] = vn

    row = pl.BlockSpec((1, tr, C), lambda r: (0, r, 0))
    o = jax.ShapeDtypeStruct((1, R, C), F32)
    return pl.pallas_call(
        body, grid=(R // tr,), in_specs=[row, row, row, row], out_specs=[row, row, row], out_shape=[o, o, o],
        name=name, compiler_params=_params(("arbitrary",), 32))(g, w, m, v)


def _vmem_specs(n):
    return [pl.BlockSpec(memory_space=pltpu.VMEM)] * n


def _pair_sum_many(mine, theirs, *, name):
    n = len(mine)

    def body(*refs):
        for a in range(n):
            refs[2 * n + a][...] = refs[a][...] + refs[n + a][0]

    return pl.pallas_call(
        body, out_shape=[jax.ShapeDtypeStruct(m.shape, m.dtype) for m in mine], in_specs=_vmem_specs(2 * n),
        out_specs=_vmem_specs(n), name=name)(*mine, *theirs)


def _chip_sum_many(own, recv, q_idx, *, name):
    n = len(own)

    def body(q_ref, *refs):
        q = q_ref[0]
        for a in range(n):
            mine, r = refs[a][...], refs[n + a]
            total = None
            for chip in range(4):
                e = q ^ chip
                term = jnp.where(e == 0, mine, jnp.where(e == 2, r[0], jnp.where(e == 1, r[1], r[2]))).astype(F32)
                total = term if total is None else total + term
            refs[2 * n + a][...] = total

    return pl.pallas_call(
        body, out_shape=[jax.ShapeDtypeStruct(m.shape, F32) for m in own],
        in_specs=[pl.BlockSpec(memory_space=pltpu.SMEM)] + _vmem_specs(2 * n), out_specs=_vmem_specs(n),
        name=name)(q_idx, *own, *recv)


def _adam_many(gs, ws, ms, vs, *, name):
    n = len(gs)

    def body(*refs):
        for a in range(n):
            delta, mn, vn = _adamw(*(refs[k * n + a][...] for k in range(4)))
            refs[4 * n + a][...] = delta
            refs[5 * n + a][...] = mn
            refs[6 * n + a][...] = vn

    shapes = [jax.ShapeDtypeStruct(w.shape, w.dtype) for w in ws]
    out = pl.pallas_call(body, out_shape=shapes * 3, in_specs=_vmem_specs(4 * n), out_specs=_vmem_specs(3 * n),
                         name=name)(*gs, *ws, *ms, *vs)
    return out[:n], out[n:2 * n], out[2 * n:]


WEIGHTS = ["mix_norm_w", "w_in", "gate_bias", "gmlp_ln_w", "gmlp_ln_b", "gmlp_ws", "gmlp_bs", "ssm_conv_w",
           "ssm_conv_b", "ssm_dt_bias", "ssm_a_log", "ssm_d", "ssm_norm_w", "w_proj_a", "w_proj_b", "w_out",
           "ffn_norm_w", "ffn_w_up", "ffn_conv_w", "ffn_conv_b", "ffn_w_down", "final_norm_w"]
SHARDED = {"w_in": ((D, IN_COLS), 1), "gate_bias": ((2, D), 1), "ssm_conv_w": ((SK, SXBC), 1),
           "w_proj_a": ((GW, D), 0), "w_proj_b": ((SI, D), 0), "w_out": ((D, D), 0),
           "ffn_w_up": ((D, 2 * DFF), 1), "ffn_conv_w": ((FK, 2 * DFF), 1), "ffn_w_down": ((DFF, D), 0)}
REPLICATED = {"mix_norm_w": (D,), "gmlp_ln_w": (GG, GD), "gmlp_ln_b": (GG, GD), "gmlp_ws": (GG, GB, GB),
              "gmlp_bs": (GG, GB), "ssm_conv_b": (SXBC,), "ssm_dt_bias": (SH,), "ssm_a_log": (SH,), "ssm_d": (SH,),
              "ssm_norm_w": (SI,), "ffn_norm_w": (D,), "ffn_conv_b": (2 * DFF,), "final_norm_w": (D,)}
REPL_ORDER = [n for n in WEIGHTS if n in REPLICATED]
BTILE = 2 * SUBLANE
WIN_R = IN_COLS // NDEV
WIN_P = WIN_R + BTILE - WIN_R % BTILE
WIN_A = [WIN_R * d // BTILE * BTILE for d in range(NDEV)]
assert all(WIN_A[d] + WIN_P >= WIN_R * (d + 1) for d in range(NDEV)) and WIN_A[-1] + WIN_P == IN_COLS
BIG = [("w_proj_a", GW // NDEV, False), ("w_proj_b", SI // NDEV, False), ("w_out", D // NDEV, False),
       ("ffn_w_up", 2 * DFF // NDEV, True), ("ffn_w_down", DFF // NDEV, False), ("w_in", WIN_P, True)]
VECTORS = ["gate_bias", "ssm_conv_w", "ffn_conv_w"]


def _round_up(n, k):
    return (n + k - 1) // k * k


BIG_OFF = {}
_off = 0
for _n, _r, _t in BIG:
    BIG_OFF[_n] = _off
    _off += _r
BIG_USED = _off
BIG_ROWS = _round_up(BIG_USED, 2 * SUBLANE)
assert all(BIG_OFF[n] % (2 * SUBLANE) == 0 for n, _, _ in BIG)
VEC_SHAPE = {n: (SHARDED[n][0][0], SHARDED[n][0][1] // NDEV) for n in VECTORS}


def _win_offset(dev):
    return WIN_R * dev - WIN_R * dev // BTILE * BTILE


def _pack_big(arrs, dtype, dev):
    parts = []
    for n, r, t in BIG:
        a = (arrs[n].T if t else arrs[n]).astype(dtype)
        if n == "w_in":
            a = lax.dynamic_update_slice(jnp.zeros((WIN_P, D), dtype), a, (_win_offset(dev), 0))
        parts.append(a)
    parts.append(jnp.zeros((BIG_ROWS - BIG_USED, D), dtype))
    return jnp.concatenate(parts, axis=0)


def _join_windows(win):
    parts = []
    for d in range(NDEV):
        lo = BTILE if WIN_A[d] % WIN_R else 0
        if lo:
            parts.append(win[d - 1, WIN_P - BTILE:] + win[d, :BTILE])
        hi = WIN_P - BTILE if d + 1 < NDEV and WIN_A[d + 1] < WIN_A[d] + WIN_P else WIN_P
        parts.append(win[d, lo:hi])
    return jnp.concatenate(parts, axis=0)


def _split_windows(main, last):
    assert WIN_A[-2] + WIN_P <= PMAIN
    wins = [main[a:a + WIN_P] for a in WIN_A[:-1]]
    return jnp.stack(wins + [jnp.concatenate([main[WIN_A[-1]:], last], axis=0)])


LATE_ROWS = BIG_OFF["w_in"]
assert LATE_ROWS + WIN_P == BIG_ROWS and BIG[-1][0] == "w_in"


def _remote(src, dst, send_sems, recv_sems, k, to):
    return pltpu.make_async_remote_copy(src_ref=src, dst_ref=dst, send_sem=send_sems.at[k], recv_sem=recv_sems.at[k],
                                        device_id=to, device_id_type=MESH)


class _Exchange:
    def __init__(self, late_shard, c_idx):
        self.late_shard, self.c_idx = late_shard, c_idx

    def gather_start(self):
        shard = self.late_shard

        def make(ins, outs, send_sems, recv_sems):
            (x_ref,), (out,) = ins, outs
            x, y, c = _axes()
            mine = out.at[4 * x + 2 * y + c]
            peers = [(x, y, 1 - c), (1 - x, y, c), (x, 1 - y, c), (1 - x, 1 - y, c)]
            copies = [_remote(x_ref, mine, send_sems, recv_sems, k, p) for k, p in enumerate(peers)]
            return copies + [pltpu.make_async_copy(x_ref, mine, send_sems.at[len(peers)])]

        return _Side([shard], [jax.ShapeDtypeStruct((NDEV,) + shard.shape, shard.dtype)], 5, make)

    def gather_pass_on(self, outs):
        (buf,) = outs

        def make(ins, outs, send_sems, recv_sems):
            (src,), (dst,) = ins, outs
            x, y, c = _axes()
            slots = [4 * px + 2 * py + c for px, py in [(1 - x, y), (x, 1 - y), (1 - x, 1 - y)]]
            return [_remote(src.at[s], dst.at[s], send_sems, recv_sems, k, (x, y, 1 - c)) for k, s in enumerate(slots)]

        return _Side([buf], [jax.ShapeDtypeStruct(buf.shape, buf.dtype)], 3, make, aliases=[(0, 0)])

    def late_weights(self, w, outs):
        (buf,) = outs
        w = dict(w)
        for n, r, t in BIG[:-1]:
            w[n + "_t" if t else n] = buf[:, BIG_OFF[n]:BIG_OFF[n] + r].reshape(NDEV * r, D)
        return w

    @staticmethod
    def _plan_side(src, plan):
        n = len(plan(0, 0, 0))

        def make(ins, outs, send_sems, recv_sems):
            (s,), (dst,) = ins, outs
            return [_remote(s.at[slab], dst.at[k], send_sems, recv_sems, k, peer)
                    for k, (slab, peer) in enumerate(plan(*_axes()))]

        return _Side([src], [jax.ShapeDtypeStruct((n,) + src.shape[1:], src.dtype)], n, make)

    def _to_chips(self, send, sib, tag):
        sums = _pair_add(send.reshape((4, 2) + send.shape[1:]), sib, self.c_idx, name=tag + "_grad_pair_add")
        return sums, self._plan_side(sums, _to_chips_plan)

    def late_to_sibling(self, grads):
        send = jnp.concatenate([grads[n + "_t" if t else n].reshape(NDEV, r, D) for n, r, t in BIG[:-1]], axis=1)
        self.late_send = send.astype(BF16)
        return self._plan_side(self.late_send, _to_sibling_plan)

    def reduce_late(self, outs):
        self.late_sum, side = self._to_chips(self.late_send, outs[0], "late")
        return side

    def w_in_to_sibling(self, grad_main, grad_dt):
        self.w_in_send = _split_windows(grad_main.astype(BF16), grad_dt.astype(BF16))
        return self._plan_side(self.w_in_send, _to_sibling_plan)

    def reduce_w_in(self, outs):
        self.w_in_sum, side = self._to_chips(self.w_in_send, outs[0], "w_in")
        return side

    def reduced(self, late_outs, w_in_outs):
        (self.late_from_chips,), (self.w_in_from_chips,) = late_outs, w_in_outs


def kernel(x, mix_norm_w, w_in, gate_bias, gmlp_ln_w, gmlp_ln_b, gmlp_ws, gmlp_bs, ssm_conv_w, ssm_conv_b, ssm_dt_bias, ssm_a_log, ssm_d, ssm_norm_w, w_proj_a, w_proj_b, w_out, ffn_norm_w, ffn_w_up, ffn_conv_w, ffn_conv_b, ffn_w_down, final_norm_w, loss_target, m_mix_norm_w, m_w_in, m_gate_bias, m_gmlp_ln_w, m_gmlp_ln_b, m_gmlp_ws, m_gmlp_bs, m_ssm_conv_w, m_ssm_conv_b, m_ssm_dt_bias, m_ssm_a_log, m_ssm_d, m_ssm_norm_w, m_w_proj_a, m_w_proj_b, m_w_out, m_ffn_norm_w, m_ffn_w_up, m_ffn_conv_w, m_ffn_conv_b, m_ffn_w_down, m_final_norm_w, v_mix_norm_w, v_w_in, v_gate_bias, v_gmlp_ln_w, v_gmlp_ln_b, v_gmlp_ws, v_gmlp_bs, v_ssm_conv_w, v_ssm_conv_b, v_ssm_dt_bias, v_ssm_a_log, v_ssm_d, v_ssm_norm_w, v_w_proj_a, v_w_proj_b, v_w_out, v_ffn_norm_w, v_ffn_w_up, v_ffn_conv_w, v_ffn_conv_b, v_ffn_w_down, v_final_norm_w):
    given = dict(locals())
    wts = {n: given[n] for n in WEIGHTS}
    mom = {n: given["m_" + n] for n in WEIGHTS}
    var = {n: given["v_" + n] for n in WEIGHTS}
    xi, yi, ci = _axes()
    c_idx = jnp.reshape(ci, (1,)).astype(jnp.int32)
    q_idx = jnp.reshape(2 * xi + yi, (1,)).astype(jnp.int32)
    big_names = [n for n, _, _ in BIG]
    drop = lambda d, names: {n: d[n][0] for n in names}

    dev = 4 * xi + 2 * yi + ci
    packed = _pack_big(drop(wts, big_names), BF16, dev)
    gathered = _all_gather([packed[LATE_ROWS:]] + [wts[n] for n in VECTORS], name="w_in_all_gather")
    full = {"w_in_t": _join_windows(gathered[0])}
    for n, a in zip(VECTORS, gathered[1:]):
        r, c = VEC_SHAPE[n]
        full[n] = a[:, 0].transpose(1, 0, 2).reshape(r, NDEV * c)
    for n in REPL_ORDER:
        full[n] = wts[n].reshape(REPLICATED[n])

    hooks = _Exchange(packed[:LATE_ROWS], c_idx)
    loss_local, grad_x, grads = _local_step(x[0], loss_target[0], full, hooks)
    g_late = _grad_sum(hooks.late_sum, hooks.late_from_chips, q_idx, name="late_grad_sum")
    g_win = _grad_sum(hooks.w_in_sum, hooks.w_in_from_chips, q_idx, name="w_in_grad_sum")

    small = VECTORS + REPL_ORDER
    as_2d = lambda a: a if a.ndim >= 2 else a[None]
    part = [grads[n].reshape((1,) + SHARDED[n][0] if n in VECTORS else as_2d(wts[n]).shape) for n in small]
    part.append(loss_local)
    from_sibling = _exchange([p[None] for p in part], lambda x, y, c: [(0, (x, y, 1 - c))],
                             name="small_grads_to_sibling")
    chip_sums = _pair_sum_many(part, from_sibling, name="small_grad_pair_sum")
    chip_sums = [s.astype(BF16) if s.size >= SMALL_BF16_FROM else s for s in chip_sums]
    from_chips = _exchange([s[None] for s in chip_sums],
                           lambda x, y, c: [(0, (1 - x, y, c)), (0, (x, 1 - y, c)), (0, (1 - x, 1 - y, c))],
                           name="small_grads_to_chips")
    totals = _chip_sum_many(chip_sums, from_chips, q_idx, name="small_grad_chip_sum")
    g_small, loss = dict(zip(small, totals)), totals[-1][0, 0]
    for n in VECTORS:
        c = VEC_SHAPE[n][1]
        g_small[n] = lax.dynamic_slice_in_dim(g_small[n], dev * c, c, axis=2)

    outs = {}
    small_g = [g_small[n] for n in small]
    small_out = _adam_many(small_g, *[[as_2d(d[n]) for n in small] for d in (wts, mom, var)], name="adam_small")
    for i, n in enumerate(small):
        outs[n] = tuple(a[i].reshape(wts[n].shape) for a in (small_g,) + tuple(small_out))
    for n, r, t in BIG:
        if n == "w_in":
            g = lax.dynamic_slice(g_win, (_win_offset(dev), 0), (WIN_R, D))
        else:
            g = g_late[BIG_OFF[n]:BIG_OFF[n] + r]
        flip = (lambda a: a.transpose(0, 2, 1)) if t else (lambda a: a)
        g = g[None]
        new = _adam(g, flip(wts[n]), flip(mom[n]), flip(var[n]), name="adam_" + n)
        outs[n] = tuple(flip(a) for a in (g,) + tuple(new))
    return (loss, grad_x[None]) + tuple(outs[n][k] for k in range(4) for n in WEIGHTS)
```

```python
import functools

import jax
import jax.numpy as jnp
from jax import lax
from jax.experimental import pallas as pl
from jax.experimental.pallas import tpu as pltpu

F32 = jnp.float32
BF16 = jnp.bfloat16

D = 1024
EPS = 1e-5
GW = 1024
GB = 128
GG = 8
GD = 128
GCH = 64
SI = 2048
SH = 32
SP = 64
SG = 4
SN = 128
SGW = SI // SG
SK = 4
SXBC = SI + 2 * SG * SN
DFF = 2816
FK = 3
PMAIN = 2 * D + 2 * GW + SI + SXBC
IN_COLS = PMAIN + SH
DP_SSM = SI + SXBC
DP_GAP = (DP_SSM - (2 * D + 2 * GW) % DP_SSM) % DP_SSM
DP_COLS = 2 * D + 2 * GW + DP_GAP + DP_SSM
assert DP_GAP % D == 0 and (2 * D + 2 * GW) % D == 0
NDEV = 8
ADAM_LR, ADAM_B1, ADAM_B2, ADAM_EPS, ADAM_WD, ADAM_STEP = 0.001, 0.9, 0.999, 1e-08, 0.01, 10

LANE = 128
SUBLANE = 8
VMEM_MB_V7X = 64
VMEM_CAP_MB = VMEM_MB_V7X - 8

LS = 128
FT = DFF // 2

NN = (((1,), (0,)), ((), ()))
NT = (((1,), (1,)), ((), ()))
TN = (((0,), (0,)), ((), ()))


def _params(sem, vmem_mb):
    return pltpu.CompilerParams(dimension_semantics=sem,
                                vmem_limit_bytes=min(int(vmem_mb), VMEM_CAP_MB) * 1024 * 1024)


def _dot(a, b, dims=NN):
    return lax.dot_general(a, b, dims, preferred_element_type=F32)


def _sigmoid(x):
    return 1.0 / (1.0 + jnp.exp(-x))


def _split3(v):
    hi = v.astype(BF16)
    r = v - hi.astype(F32)
    mid = r.astype(BF16)
    lo = (r - mid.astype(F32)).astype(BF16)
    return hi, mid, lo


def _dot3(a_f32, b_bf16, dims):
    hi, mid, lo = _split3(a_f32)
    return _dot(hi, b_bf16, dims) + _dot(mid, b_bf16, dims) + _dot(lo, b_bf16, dims)


def _dot2(a_f32, b_bf16, dims):
    hi, mid, _ = _split3(a_f32)
    return _dot(hi, b_bf16, dims) + _dot(mid, b_bf16, dims)


def _dot3_rhs(a_bf16, b_f32, dims):
    hi, mid, lo = _split3(b_f32)
    return _dot(a_bf16, hi, dims) + _dot(a_bf16, mid, dims) + _dot(a_bf16, lo, dims)


def _matmul(a, b, *, name, out_dtype, ta=False, tb=False, tm=1024, tn=1024, tk=1024, add=None,
            j_outer=False, b_rows=None, a_gap=None, o_row=None, side=None):
    assert not (o_row and add is not None)
    gap0, gapw = a_gap or (0, 0)
    if ta:
        K, M = a.shape
        M -= gapw
    else:
        M, K = a.shape
        K -= gapw
    if tb:
        N, K2 = b.shape
        N = b_rows or N
    else:
        K2, N = b.shape
        K2 = b_rows or K2
    assert K == K2, (a.shape, b.shape, ta, tb)
    tm, tn, tk = min(tm, M), min(tn, N), min(tk, K)
    assert M % tm == 0 and N % tn == 0 and K % tk == 0, (M, N, K, tm, tn, tk)
    nk = K // tk
    dims = (((0 if ta else 1,), (1 if tb else 0,)), ((), ()))
    has_add = add is not None
    n_in = 3 if has_add else 2
    s_in = len(side.inputs) if side else 0
    s_out = len(side.out_shapes) if side else 0
    grid = (N // tn, M // tm, nk) if j_outer else (M // tm, N // tn, nk)

    def body(*refs):
        a_ref, b_ref = refs[:2]
        add_ref = refs[2] if has_add else None
        o_ref = refs[n_in + s_in]
        if side:
            side_refs = (refs[n_in:n_in + s_in], refs[n_in + s_in + 1:n_in + s_in + 1 + s_out]) + tuple(refs[-2:])
            ids = [pl.program_id(d) for d in range(3)]
            first = functools.reduce(jnp.logical_and, [i == 0 for i in ids])
            last = functools.reduce(jnp.logical_and, [i == g - 1 for i, g in zip(ids, grid)])

            @pl.when(first)
            def _():
                for cp in side.make(*side_refs):
                    cp.start()

            @pl.when(last)
            def _():
                for cp in side.make(*side_refs):
                    cp.wait()

        p = lax.dot_general(a_ref[...].astype(BF16), b_ref[...].astype(BF16), dims,
                            preferred_element_type=F32)

        def finish(acc):
            if has_add:
                acc = acc + add_ref[...].astype(F32)
            o_ref[...] = acc.astype(o_ref.dtype)

        if nk == 1:
            finish(p)
        else:
            acc_ref = refs[n_in + s_in + 1 + s_out]
            k = pl.program_id(2)

            @pl.when(k == 0)
            def _():
                acc_ref[...] = p

            @pl.when(jnp.logical_and(k > 0, k < nk - 1))
            def _():
                acc_ref[...] += p

            @pl.when(k == nk - 1)
            def _():
                finish(acc_ref[...] + p)

    if j_outer:
        ij = lambda g0, g1: (g1, g0)
    else:
        ij = lambda g0, g1: (g0, g1)

    ta_col = tm if ta else tk
    assert gap0 % ta_col == 0 and gapw % ta_col == 0, (a_gap, ta_col)

    def a_map(g0, g1, k):
        i, _ = ij(g0, g1)
        col = i if ta else k
        col = col + jnp.where(col >= gap0 // ta_col, gapw // ta_col, 0) if gapw else col
        return (k, col) if ta else (i, col)

    def b_map(g0, g1, k):
        _, j = ij(g0, g1)
        return (j, k) if tb else (k, j)

    def o_map(g0, g1, k):
        i, j = ij(g0, g1)
        return (o_row(i) if o_row else i, j)

    in_specs = [pl.BlockSpec((tk, tm) if ta else (tm, tk), a_map),
                pl.BlockSpec((tn, tk) if tb else (tk, tn), b_map)]
    args = [a, b]
    if has_add:
        in_specs.append(pl.BlockSpec((tm, tn), o_map))
        args.append(add)
    scratch = [pltpu.VMEM((tm, tn), F32)] if nk > 1 else []
    osz = jnp.dtype(out_dtype).itemsize
    est = (2 * (tm * tk * a.dtype.itemsize + tk * tn * b.dtype.itemsize) + 2 * tm * tn * osz
           + (2 * tm * tn * add.dtype.itemsize if has_add else 0)
           + 3 * tm * tn * 4 + 2 * (tm * tk + tk * tn)) / 2 ** 20 + 4
    out_specs = [pl.BlockSpec((tm, tn), o_map)]
    out_shape = [jax.ShapeDtypeStruct((M, N), out_dtype)]
    aliases = {}
    if side:
        hbm = pl.BlockSpec(memory_space=pltpu.HBM)
        in_specs += [hbm] * s_in
        args += list(side.inputs)
        out_specs += [hbm] * s_out
        out_shape += list(side.out_shapes)
        scratch += [pltpu.SemaphoreType.DMA((side.nsem,)), pltpu.SemaphoreType.DMA((side.nsem,))]
        aliases = {n_in + i: 1 + j for i, j in side.aliases}
    outs = pl.pallas_call(
        body, grid=grid, in_specs=in_specs, out_specs=out_specs, out_shape=out_shape, scratch_shapes=scratch,
        input_output_aliases=aliases, name=name,
        compiler_params=_params(("arbitrary", "arbitrary", "arbitrary"), est))(*args)
    return (outs[0], list(outs[1:])) if side else outs[0]


class _Side:
    def __init__(self, inputs, out_shapes, nsem, make, aliases=()):
        self.inputs, self.out_shapes, self.nsem, self.make, self.aliases = inputs, out_shapes, nsem, make, aliases


def _rms_fwd(x, w, *, name, side=None):
    T = x.shape[0]
    tm = min(512, T)

    def body(x_ref, w_ref, o_ref):
        xv = x_ref[...]
        r = lax.rsqrt(jnp.mean(xv * xv, axis=-1, keepdims=True) + EPS)
        o_ref[...] = (xv * r * w_ref[...]).astype(BF16)

    x_in, x_args, x_out, x_shapes, x_scratch = _ride_args(side)
    outs = pl.pallas_call(
        _ride(body, 2, 1, side, T // tm), grid=(T // tm,),
        in_specs=[pl.BlockSpec((tm, D), lambda i: (i, 0)), pl.BlockSpec((1, D), lambda i: (0, 0))] + x_in,
        out_specs=[pl.BlockSpec((tm, D), lambda i: (i, 0))] + x_out,
        out_shape=[jax.ShapeDtypeStruct((T, D), BF16)] + x_shapes, scratch_shapes=x_scratch, name=name,
        compiler_params=_params(("arbitrary",), 24))(x, w, *x_args)
    return (outs[0], list(outs[1:])) if side else outs[0]


def _rms_bwd(x, w, dy, dres, *, name):
    T = x.shape[0]
    tm = min(512, T)

    def body(x_ref, w_ref, dy_ref, dres_ref, dx_ref, dw_ref):
        xv = x_ref[...]
        r = lax.rsqrt(jnp.mean(xv * xv, axis=-1, keepdims=True) + EPS)
        xhat = xv * r
        dyv = dy_ref[...].astype(F32)
        g = dyv * w_ref[...]
        dx_ref[...] = dres_ref[...] + r * (g - xhat * jnp.mean(g * xhat, axis=-1, keepdims=True))
        part = jnp.sum(dyv * xhat, axis=0, keepdims=True)

        @pl.when(pl.program_id(0) == 0)
        def _():
            dw_ref[...] = part

        @pl.when(pl.program_id(0) > 0)
        def _():
            dw_ref[...] += part

    row = pl.BlockSpec((tm, D), lambda i: (i, 0))
    vec = pl.BlockSpec((1, D), lambda i: (0, 0))
    return pl.pallas_call(
        body, grid=(T // tm,), in_specs=[row, vec, row, row], out_specs=[row, vec],
        out_shape=[jax.ShapeDtypeStruct((T, D), F32), jax.ShapeDtypeStruct((1, D), F32)], name=name,
        compiler_params=_params(("arbitrary",), 32))(x, w, dy, dres)


def _loss_head(h, tgt, w):
    T = h.shape[0]
    tm = min(512, T)

    def body(h_ref, t_ref, w_ref, loss_ref, dh_ref, dw_ref):
        hv = h_ref[...]
        r = lax.rsqrt(jnp.mean(hv * hv, axis=-1, keepdims=True) + EPS)
        xhat = hv * r
        wv = w_ref[...]
        err = xhat * wv - t_ref[...]
        lpart = 0.5 * jnp.sum(jnp.mean(err * err, axis=-1, keepdims=True), axis=0, keepdims=True)
        dy = err * (1.0 / D)
        g = dy * wv
        dh_ref[...] = r * (g - xhat * jnp.mean(g * xhat, axis=-1, keepdims=True))
        wpart = jnp.sum(dy * xhat, axis=0, keepdims=True)
        lrow = jnp.broadcast_to(lpart, (1, LANE))

        @pl.when(pl.program_id(0) == 0)
        def _():
            dw_ref[...] = wpart
            loss_ref[...] = lrow

        @pl.when(pl.program_id(0) > 0)
        def _():
            dw_ref[...] += wpart
            loss_ref[...] += lrow

    row = pl.BlockSpec((tm, D), lambda i: (i, 0))
    vec = pl.BlockSpec((1, D), lambda i: (0, 0))
    return pl.pallas_call(
        body, grid=(T // tm,), in_specs=[row, row, vec],
        out_specs=[pl.BlockSpec((1, LANE), lambda i: (0, 0)), row, vec],
        out_shape=[jax.ShapeDtypeStruct((1, LANE), F32), jax.ShapeDtypeStruct((T, D), F32),
                   jax.ShapeDtypeStruct((1, D), F32)], name="loss_head",
        compiler_params=_params(("arbitrary",), 32))(h, tgt, w)


_GELU_C = 0.7978845608028654
_GELU_A = 0.044715


def _gelu(x, with_grad=False):
    x2 = x * x
    cx = _GELU_C * x
    t = jnp.tanh(cx * (1.0 + _GELU_A * x2))
    h = 0.5 * (1.0 + t)
    if not with_grad:
        return x * h
    return x * h, h + 0.5 * cx * (1.0 - t * t) * (1.0 + 3.0 * _GELU_A * x2)


def _gmlp_mask():
    r = lax.broadcasted_iota(jnp.int32, (GB, GB), 0) // GCH
    c = lax.broadcasted_iota(jnp.int32, (GB, GB), 1) // GCH
    return c <= r


def _ride(body, n_in, n_out, side, nsteps):
    if not side:
        return body
    s_in, s_out = len(side.inputs), len(side.out_shapes)

    def outer(*refs):
        ins, side_ins = refs[:n_in], refs[n_in:n_in + s_in]
        outs = refs[n_in + s_in:n_in + s_in + n_out]
        side_refs = (side_ins, refs[n_in + s_in + n_out:n_in + s_in + n_out + s_out]) + tuple(refs[-2:])

        @pl.when(pl.program_id(0) == 0)
        def _():
            for cp in side.make(*side_refs):
                cp.start()

        @pl.when(pl.program_id(0) == nsteps - 1)
        def _():
            for cp in side.make(*side_refs):
                cp.wait()

        body(*ins, *outs)

    return outer


def _ride_args(side):
    if not side:
        return [], [], [], [], []
    hbm = pl.BlockSpec(memory_space=pltpu.HBM)
    sems = [pltpu.SemaphoreType.DMA((side.nsem,)), pltpu.SemaphoreType.DMA((side.nsem,))]
    return [hbm] * len(side.inputs), list(side.inputs), [hbm] * len(side.out_shapes), list(side.out_shapes), sems


def _gmlp_fwd(proj, lnw, lnb, ws, bst, side=None):
    T = proj.shape[0]
    tm = min(512, T)
    nblk = tm // GB

    def body(u_ref, v_ref, lnw_ref, lnb_ref, ws_ref, bst_ref, o_ref):
        mask = _gmlp_mask()
        u = _gelu(u_ref[...].astype(F32))
        v = _gelu(v_ref[...].astype(F32))
        for g in range(GG):
            cs = slice(g * GD, (g + 1) * GD)
            vg = v[:, cs]
            mu = jnp.mean(vg, axis=-1, keepdims=True)
            vc = vg - mu
            var = jnp.mean(vc * vc, axis=-1, keepdims=True)
            vn = (vc * lax.rsqrt(var + EPS) * lnw_ref[g:g + 1, :] + lnb_ref[g:g + 1, :]).astype(BF16)
            wsg = jnp.where(mask, ws_ref[g], 0.0).astype(BF16)
            bcol = bst_ref[:, g:g + 1]
            for blk in range(nblk):
                rs = slice(blk * GB, (blk + 1) * GB)
                sv = _dot(wsg, vn[rs, :]) + bcol
                o_ref[rs, cs] = (u[rs, cs] * sv).astype(BF16)

    full = lambda shape: pl.BlockSpec(shape, lambda i: tuple(0 for _ in shape))
    x_in, x_args, x_out, x_shapes, x_scratch = _ride_args(side)
    outs = pl.pallas_call(
        _ride(body, 6, 1, side, T // tm), grid=(T // tm,),
        in_specs=[pl.BlockSpec((tm, GW), lambda i: (i, 2)), pl.BlockSpec((tm, GW), lambda i: (i, 3)),
                  full((GG, GD)), full((GG, GD)), full((GG, GB, GB)), full((GB, GG))] + x_in,
        out_specs=[pl.BlockSpec((tm, GW), lambda i: (i, 0))] + x_out,
        out_shape=[jax.ShapeDtypeStruct((T, GW), BF16)] + x_shapes, scratch_shapes=x_scratch,
        input_output_aliases={6 + i: 1 + j for i, j in (side.aliases if side else ())}, name="gmlp_fwd",
        compiler_params=_params(("arbitrary",), 40))(proj, proj, lnw, lnb, ws, bst, *x_args)
    return outs[0], list(outs[1:])


def _gmlp_bwd(proj, dya, dproj, lnw, lnb, ws, bst, side=None):
    T = proj.shape[0]
    tm = min(512, T)
    nblk = tm // GB
    n_in, n_out = 8, 5

    def body(u_ref, v_ref, dya_ref, dproj_in, lnw_ref, lnb_ref, ws_ref, bst_ref,
             dz_ref, dlnw_ref, dlnb_ref, dws_ref, dbst_ref):
        del dproj_in
        first = pl.program_id(0) == 0

        @pl.when(first)
        def _():
            dlnw_ref[...] = jnp.zeros_like(dlnw_ref)
            dlnb_ref[...] = jnp.zeros_like(dlnb_ref)
            dws_ref[...] = jnp.zeros_like(dws_ref)
            dbst_ref[...] = jnp.zeros_like(dbst_ref)

        mask = _gmlp_mask()
        lane = lax.broadcasted_iota(jnp.int32, (GB, LANE), 1)
        ur = u_ref[...].astype(F32)
        vr = v_ref[...].astype(F32)
        u, gu = _gelu(ur, with_grad=True)
        v, gv = _gelu(vr, with_grad=True)
        dy = dya_ref[...].astype(F32)
        dbst = jnp.zeros((GB, LANE), F32)
        dlnw_rows, dlnb_rows = [], []
        for g in range(GG):
            cs = slice(g * GD, (g + 1) * GD)
            vg = v[:, cs]
            mu = jnp.mean(vg, axis=-1, keepdims=True)
            vc = vg - mu
            var = jnp.mean(vc * vc, axis=-1, keepdims=True)
            rstd = lax.rsqrt(var + EPS)
            xhat = vc * rstd
            lw = lnw_ref[g:g + 1, :]
            vn = (xhat * lw + lnb_ref[g:g + 1, :]).astype(BF16)
            wsg = jnp.where(mask, ws_ref[g], 0.0).astype(BF16)
            bcol = bst_ref[:, g:g + 1]
            dyg = dy[:, cs]
            ug = u[:, cs]
            dsv = dyg * ug
            dsv_b = dsv.astype(BF16)
            dws_g = jnp.zeros((GB, GB), F32)
            bsum = jnp.zeros((GB, 1), F32)
            dvn_parts = []
            for blk in range(nblk):
                rs = slice(blk * GB, (blk + 1) * GB)
                sv = _dot(wsg, vn[rs, :]) + bcol
                dz_ref[rs, cs] = (dyg[rs, :] * sv * gu[rs, cs]).astype(BF16)
                dws_g = dws_g + _dot(dsv_b[rs, :], vn[rs, :], NT)
                bsum = bsum + jnp.sum(dsv[rs, :], axis=-1, keepdims=True)
                dvn_parts.append(_dot(wsg, dsv_b[rs, :], TN))
            dvn = jnp.concatenate(dvn_parts, axis=0)
            dws_ref[g] += jnp.where(mask, dws_g, 0.0)
            dbst = dbst + jnp.where(lane == g, bsum, 0.0)
            dlnw_rows.append(jnp.sum(dvn * xhat, axis=0, keepdims=True))
            dlnb_rows.append(jnp.sum(dvn, axis=0, keepdims=True))
            dxh = dvn * lw
            dvg = rstd * (dxh - jnp.mean(dxh, axis=-1, keepdims=True)
                          - xhat * jnp.mean(dxh * xhat, axis=-1, keepdims=True))
            dz_ref[:, GW + g * GD:GW + (g + 1) * GD] = (dvg * gv[:, cs]).astype(BF16)
        dlnw_ref[...] += jnp.concatenate(dlnw_rows, axis=0)
        dlnb_ref[...] += jnp.concatenate(dlnb_rows, axis=0)
        dbst_ref[...] += dbst

    full = lambda shape: pl.BlockSpec(shape, lambda i: tuple(0 for _ in shape))
    x_in, x_args, x_out, x_shapes, x_scratch = _ride_args(side)
    outs = pl.pallas_call(
        _ride(body, n_in, n_out, side, T // tm), grid=(T // tm,),
        in_specs=[pl.BlockSpec((tm, GW), lambda i: (i, 2)), pl.BlockSpec((tm, GW), lambda i: (i, 3)),
                  pl.BlockSpec((tm, GW), lambda i: (i, 0)), pl.BlockSpec(memory_space=pl.ANY),
                  full((GG, GD)), full((GG, GD)), full((GG, GB, GB)), full((GB, GG))] + x_in,
        out_specs=[pl.BlockSpec((tm, 2 * GW), lambda i: (i, 1)), full((GG, GD)), full((GG, GD)),
                   full((GG, GB, GB)), full((GB, LANE))] + x_out,
        out_shape=[jax.ShapeDtypeStruct(dproj.shape, dproj.dtype), jax.ShapeDtypeStruct((GG, GD), F32),
                   jax.ShapeDtypeStruct((GG, GD), F32), jax.ShapeDtypeStruct((GG, GB, GB), F32),
                   jax.ShapeDtypeStruct((GB, LANE), F32)] + x_shapes,
        scratch_shapes=x_scratch, input_output_aliases={3: 0}, name="gmlp_bwd",
        compiler_params=_params(("arbitrary",), 48))(proj, proj, dya, dproj, lnw, lnb, ws, bst, *x_args)
    return tuple(outs[:n_out]), list(outs[n_out:])


def _merge_fwd(ya_pre, yb_pre, proj, bias, wpa, wpb):
    T = proj.shape[0]
    tm = min(512, T)

    def body(ya_ref, yb_ref, g_ref, b_ref, wpa_ref, wpb_ref, m_ref, oa_ref, ob_ref):
        ya = _dot(ya_ref[...], wpa_ref[...])
        yb = _dot(yb_ref[...], wpb_ref[...])
        g = g_ref[...].astype(F32)
        sa = _sigmoid(g[:, :D] + b_ref[0:1, :])
        sb = _sigmoid(g[:, D:] + b_ref[1:2, :])
        m_ref[...] = (sa * ya + sb * yb).astype(BF16)
        oa_ref[...] = ya.astype(BF16)
        ob_ref[...] = yb.astype(BF16)

    row = lambda w: pl.BlockSpec((tm, w), lambda i: (i, 0))
    full = lambda shape: pl.BlockSpec(shape, lambda i: tuple(0 for _ in shape))
    o = jax.ShapeDtypeStruct((T, D), BF16)
    return pl.pallas_call(
        body, grid=(T // tm,),
        in_specs=[row(GW), row(SI), row(2 * D), full((2, D)), full((GW, D)), full((SI, D))],
        out_specs=[row(D), row(D), row(D)], out_shape=[o, o, o], name="merge_fwd",
        compiler_params=_params(("arbitrary",), 40))(ya_pre, yb_pre, proj, bias, wpa, wpb)


def _merge_bwd(dm, proj, bias, ya, yb, wpa, wpb):
    T = proj.shape[0]
    tm = min(512, T)

    def body(dm_ref, g_ref, b_ref, ya_ref, yb_ref, wpa_ref, wpb_ref,
             dg_ref, dya_ref, dyb_ref, dpa_ref, dpb_ref, db_ref):
        dmv = dm_ref[...].astype(F32)
        g = g_ref[...].astype(F32)
        sa = _sigmoid(g[:, :D] + b_ref[0:1, :])
        sb = _sigmoid(g[:, D:] + b_ref[1:2, :])
        dya = (dmv * sa).astype(BF16)
        dyb = (dmv * sb).astype(BF16)
        dga = dmv * ya_ref[...].astype(F32) * sa * (1.0 - sa)
        dgb = dmv * yb_ref[...].astype(F32) * sb * (1.0 - sb)
        dg_ref[:, :D] = dga.astype(BF16)
        dg_ref[:, D:] = dgb.astype(BF16)
        dya_ref[...] = dya
        dyb_ref[...] = dyb
        dpa_ref[...] = _dot(dya, wpa_ref[...], NT).astype(BF16)
        dpb_ref[...] = _dot(dyb, wpb_ref[...], NT).astype(BF16)
        part = jnp.concatenate([jnp.sum(dga, axis=0, keepdims=True), jnp.sum(dgb, axis=0, keepdims=True)], axis=0)

        @pl.when(pl.program_id(0) == 0)
        def _():
            db_ref[...] = part

        @pl.when(pl.program_id(0) > 0)
        def _():
            db_ref[...] += part

    row = lambda w: pl.BlockSpec((tm, w), lambda i: (i, 0))
    full = lambda shape: pl.BlockSpec(shape, lambda i: tuple(0 for _ in shape))
    o = lambda w: jax.ShapeDtypeStruct((T, w), BF16)
    return pl.pallas_call(
        body, grid=(T // tm,),
        in_specs=[row(D), row(2 * D), full((2, D)), row(D), row(D), full((GW, D)), full((SI, D))],
        out_specs=[row(2 * D), row(D), row(D), row(GW), row(SI), full((2, D))],
        out_shape=[o(DP_COLS), o(D), o(D), o(GW), o(SI), jax.ShapeDtypeStruct((2, D), F32)], name="merge_bwd",
        compiler_params=_params(("arbitrary",), 48))(dm, proj, bias, ya, yb, wpa, wpb)


RB = 128


def _shift_matrix(j):
    r = lax.broadcasted_iota(jnp.int32, (RB, RB), 0)
    c = lax.broadcasted_iota(jnp.int32, (RB, RB), 1)
    return jnp.where(c == r - j, 1.0, 0.0).astype(BF16)


def _rows_down(xb, before, shifts):
    H = SUBLANE
    mats = [_shift_matrix(j) for j in shifts]
    outs = [[] for _ in shifts]
    for b in range(xb.shape[0] // RB):
        blk = xb[b * RB:(b + 1) * RB]
        edge = jnp.concatenate([before, blk[:2 * H].astype(F32)[:H]], axis=0)
        for i, j in enumerate(shifts):
            outs[i] += [edge[H - j:2 * H - j], _dot(mats[i], blk)[H:]]
        before = blk[RB - 2 * H:].astype(F32)[H:]
    return [jnp.concatenate(o, axis=0) for o in outs]


def _rows_up(xb, after, shifts):
    H = SUBLANE
    nb = xb.shape[0] // RB
    mats = [_shift_matrix(-j) for j in shifts]
    outs = [[] for _ in shifts]
    for b in range(nb):
        blk = xb[b * RB:(b + 1) * RB]
        nxt = xb[(b + 1) * RB:(b + 1) * RB + 2 * H].astype(F32)[:H] if b + 1 < nb else after
        edge = jnp.concatenate([blk[RB - 2 * H:].astype(F32)[H:], nxt], axis=0)
        for i, j in enumerate(shifts):
            outs[i] += [_dot(mats[i], blk)[:RB - H], edge[j:H + j]]
    return [jnp.concatenate(o, axis=0) for o in outs]


def _ffn_act_fwd(up, cw, cb):
    T = up.shape[0]
    tm = min(512, T)
    H = SUBLANE

    def body(up_ref, cw_ref, cb_ref, o_ref, xc_ref, halo):
        @pl.when(pl.program_id(1) == 0)
        def _():
            halo[...] = jnp.zeros_like(halo)

        xb = up_ref[...]
        x2, x1 = _rows_down(xb, halo[...], (2, 1))
        xc = cb_ref[...] + cw_ref[0:1, :] * x2 + cw_ref[1:2, :] * x1 + cw_ref[2:3, :] * xb.astype(F32)
        xc_ref[...] = xc.astype(BF16)
        gate = xc[:, :FT]
        o_ref[...] = (gate * _sigmoid(gate) * xc[:, FT:]).astype(BF16)
        halo[...] = xb[tm - 2 * H:].astype(F32)[H:]

    tile = pl.BlockSpec((tm, 2 * FT), lambda j, i: (i, j))
    return pl.pallas_call(
        body, grid=(2, T // tm),
        in_specs=[tile, pl.BlockSpec((FK, 2 * FT), lambda j, i: (0, j)), pl.BlockSpec((1, 2 * FT), lambda j, i: (0, j))],
        out_specs=[pl.BlockSpec((tm, FT), lambda j, i: (i, j)), tile],
        out_shape=[jax.ShapeDtypeStruct((T, DFF), BF16), jax.ShapeDtypeStruct((T, 2 * DFF), BF16)],
        scratch_shapes=[pltpu.VMEM((H, 2 * FT), F32)], name="ffn_act_fwd",
        compiler_params=_params(("arbitrary", "arbitrary"), 48))(up, cw, cb)


def _ffn_act_bwd(up, xc, dact, cw):
    T = up.shape[0]
    tm = min(512, T)
    nt = T // tm
    H = SUBLANE

    def body(up_ref, xc_ref, da_ref, cw_ref, dup_ref, dcw_ref, dcb_ref, ahead):
        @pl.when(pl.program_id(1) == 0)
        def _():
            ahead[...] = jnp.zeros_like(ahead)
            dcw_ref[...] = jnp.zeros_like(dcw_ref)
            dcb_ref[...] = jnp.zeros_like(dcb_ref)

        xcv = xc_ref[...].astype(F32)
        gate, val = xcv[:, :FT], xcv[:, FT:]
        sg = _sigmoid(gate)
        dav = da_ref[...].astype(F32)
        dgate = dav * val * sg * (1.0 + gate * (1.0 - sg))
        dval = dav * gate * sg
        dxc = jnp.concatenate([dgate, dval], axis=1)
        d1, d2 = _rows_up(dxc.astype(BF16), ahead[...], (1, 2))
        x = up_ref[...].astype(F32)
        dcb_ref[...] += jnp.sum(dxc, axis=0, keepdims=True)
        dcw_ref[...] += jnp.concatenate([jnp.sum(d * x, axis=0, keepdims=True) for d in (d2, d1, dxc)], axis=0)
        dup_ref[...] = (cw_ref[2:3, :] * dxc + cw_ref[1:2, :] * d1 + cw_ref[0:1, :] * d2).astype(BF16)
        ahead[...] = dxc[0:H, :]

    tile = pl.BlockSpec((tm, 2 * FT), lambda j, i: (nt - 1 - i, j))
    return pl.pallas_call(
        body, grid=(2, nt),
        in_specs=[tile, tile, pl.BlockSpec((tm, FT), lambda j, i: (nt - 1 - i, j)),
                  pl.BlockSpec((FK, 2 * FT), lambda j, i: (0, j))],
        out_specs=[tile, pl.BlockSpec((FK, 2 * FT), lambda j, i: (0, j)), pl.BlockSpec((1, 2 * FT), lambda j, i: (0, j))],
        out_shape=[jax.ShapeDtypeStruct((T, 2 * DFF), BF16), jax.ShapeDtypeStruct((FK, 2 * DFF), F32),
                   jax.ShapeDtypeStruct((1, 2 * DFF), F32)],
        scratch_shapes=[pltpu.VMEM((H, 2 * FT), F32)], name="ffn_act_bwd",
        compiler_params=_params(("arbitrary", "arbitrary"), 56))(up, xc, dact, cw)


def _softplus(x):
    e = jnp.exp(-jnp.abs(x))
    return jnp.maximum(x, 0.0) + jnp.where(e < 1e-4, e * (1.0 - 0.5 * e), jnp.log(1.0 + e))


def _ssd_tril():
    li = lax.broadcasted_iota(jnp.int32, (LS, LS), 0)
    si = lax.broadcasted_iota(jnp.int32, (LS, LS), 1)
    return si <= li


def _head_expansion():
    hh = lax.broadcasted_iota(jnp.int32, (LANE, SI), 0)
    cc = lax.broadcasted_iota(jnp.int32, (LANE, SI), 1) // SP
    return jnp.where(hh == cc, 1.0, 0.0).astype(BF16)


def _ssd_pre(xc, dt_ref, dtb_ref, alog_ref, tril, expand):
    sx = _sigmoid(xc)
    xbc = xc * sx
    xs, bm, cm = xbc[:, :SI], xbc[:, SI:SI + SG * SN], xbc[:, SI + SG * SN:]
    dtin = dt_ref[...] + dtb_ref[...]
    dt = _softplus(dtin)
    a_neg = -jnp.exp(alog_ref[...])
    dta = dt * a_neg
    trilb = jnp.where(tril, 1.0, 0.0).astype(BF16)
    a = _dot3_rhs(trilb, dta, NN)
    a_exp = _dot3(a, expand, NN)
    dt_exp = _dot2(dt, expand, NN)
    xdt = xs * dt_exp
    a_last = a_exp[LS - 1:LS, :]
    return dict(xc=xc, sx=sx, xs=xs, bm=bm, cm=cm, dtin=dt_ref[...] + dtb_ref[...], dt=dt, a_neg=a_neg,
                a=a, a_t=a.T, a_exp=a_exp, dt_exp=dt_exp, xdt=xdt, ea=jnp.exp(a_exp),
                w=jnp.exp(a_last - a_exp), eal=jnp.exp(a_last))


def _head_decay(pre, tril, h):
    seg = pre["a"][:, h:h + 1] - pre["a_t"][h:h + 1, :]
    return jnp.exp(jnp.where(tril, seg, -1e30))


def _ssd_fwd(proj, dtraw, cw, cb, dtb, alog, dexp, nw):
    T = proj.shape[0]
    nc = T // LS
    H = SUBLANE

    def body(z_ref, x_ref, dt_ref, cw_ref, cb_ref, dtb_ref, alog_ref, dexp_ref, nw_ref, ex_ref,
             yb_ref, y_ref, sp_ref, xc_ref, halo, st):
        @pl.when(pl.program_id(0) == 0)
        def _():
            halo[...] = jnp.zeros_like(halo)
            st[...] = jnp.zeros_like(st)

        xb = x_ref[...]
        taps = _rows_down(xb, halo[...], (3, 2, 1)) + [xb.astype(F32)]
        xc = cb_ref[...]
        for k in range(SK):
            xc = xc + cw_ref[k:k + 1, :] * taps[k]
        xc_ref[...] = xc.astype(BF16)
        tril, expand = _ssd_tril(), ex_ref[...]
        pre = _ssd_pre(xc, dt_ref, dtb_ref, alog_ref, tril, expand)
        lane = lax.broadcasted_iota(jnp.int32, (LS, LANE), 1)
        lo = lane < SP
        zf = z_ref[...].astype(F32)
        siluz = zf * _sigmoid(zf)
        for g in range(SG):
            gs = slice(g * SGW, (g + 1) * SGW)
            bg = pre["bm"][:, g * SN:(g + 1) * SN].astype(BF16)
            cg = pre["cm"][:, g * SN:(g + 1) * SN].astype(BF16)
            gmat = _dot(cg, bg, NT)
            sg = st[g]
            sp_ref[0, g] = sg
            yoff = _dot(cg, sg.astype(BF16))
            parts = []
            for j in range(SGW // LANE):
                h0 = g * (SGW // SP) + 2 * j
                m0 = gmat * _head_decay(pre, tril, h0)
                m1 = gmat * _head_decay(pre, tril, h0 + 1)
                xp = pre["xdt"][:, g * SGW + j * LANE:g * SGW + (j + 1) * LANE]
                rhs = jnp.concatenate([jnp.where(lo, xp, 0.0), jnp.where(lo, 0.0, xp)], axis=0).astype(BF16)
                parts.append(_dot(jnp.concatenate([m0, m1], axis=1).astype(BF16), rhs))
            y = (jnp.concatenate(parts, axis=1) + pre["ea"][:, gs] * yoff + dexp_ref[:, gs] * pre["xs"][:, gs])
            st[g] = pre["eal"][:, gs] * sg + _dot(bg, (pre["w"][:, gs] * pre["xdt"][:, gs]).astype(BF16), TN)
            y_ref[:, gs] = y
            yg = y * siluz[:, gs]
            r = lax.rsqrt(jnp.mean(yg * yg, axis=-1, keepdims=True) + EPS)
            yb_ref[:, gs] = (yg * r * nw_ref[:, gs]).astype(BF16)
        halo[...] = xb[LS - 2 * H:].astype(F32)[H:]

    vec = lambda w: pl.BlockSpec((1, w), lambda c: (0, 0))
    return pl.pallas_call(
        body, grid=(nc,),
        in_specs=[pl.BlockSpec((LS, SI), lambda c: (c, 2)), pl.BlockSpec((LS, SXBC), lambda c: (c, 2)),
                  pl.BlockSpec((LS, LANE), lambda c: (c, 0)),
                  pl.BlockSpec((SK, SXBC), lambda c: (0, 0)), vec(SXBC), vec(LANE), vec(LANE), vec(SI), vec(SI),
                  pl.BlockSpec((LANE, SI), lambda c: (0, 0))],
        out_specs=[pl.BlockSpec((LS, SI), lambda c: (c, 0)), pl.BlockSpec((LS, SI), lambda c: (c, 0)),
                   pl.BlockSpec((1, SG, SN, SGW), lambda c: (c, 0, 0, 0)), pl.BlockSpec((LS, SXBC), lambda c: (c, 0))],
        out_shape=[jax.ShapeDtypeStruct((T, SI), BF16), jax.ShapeDtypeStruct((T, SI), F32),
                   jax.ShapeDtypeStruct((nc, SG, SN, SGW), F32), jax.ShapeDtypeStruct((T, SXBC), BF16)],
        scratch_shapes=[pltpu.VMEM((H, SXBC), F32), pltpu.VMEM((SG, SN, SGW), F32)], name="ssd_fwd",
        compiler_params=_params(("arbitrary",), VMEM_CAP_MB))(
            proj, proj, dtraw, cw, cb, dtb, alog, dexp, nw, _head_expansion())


def _ssd_bwd(proj, xcs, dtraw, y, sprev, dyb, dproj, cw, dtb, alog, dexp, nw):
    T = proj.shape[0]
    nc = T // LS
    H = SUBLANE
    NJ = 1

    def body(z_ref, x_ref, xc_ref, dt_ref, y_ref, sp_ref, dyb_ref, dproj_in,
             cw_ref, dtb_ref, alog_ref, dexp_ref, nw_ref, ex_ref,
             dp_ref, ddt_ref, dcw_ref, dcb_ref, ddtb_ref, da_ref, dd_ref, dnw_ref,
             ahead, ds, stage):
        del dproj_in
        i = pl.program_id(0)
        j = pl.program_id(1)

        @pl.when(jnp.logical_and(i == 0, j == 0))
        def _():
            ahead[...] = jnp.zeros_like(ahead)
            ds[...] = jnp.zeros_like(ds)
            for r in (dcw_ref, dcb_ref, ddtb_ref, da_ref, dd_ref, dnw_ref):
                r[...] = jnp.zeros_like(r)

        @pl.when(j == 0)
        def _():
            tril, expand = _ssd_tril(), ex_ref[...]
            pre = _ssd_pre(xc_ref[...].astype(F32), dt_ref, dtb_ref, alog_ref, tril, expand)
            lane = lax.broadcasted_iota(jnp.int32, (LS, LANE), 1)
            sub = lax.broadcasted_iota(jnp.int32, (LANE, LS), 0)
            rowi = lax.broadcasted_iota(jnp.int32, (LS, 1), 0)
            lo = lane < SP
            xs, xdt, ea, w, eal = pre["xs"], pre["xdt"], pre["ea"], pre["w"], pre["eal"]

            zf = z_ref[...].astype(F32)
            sz = _sigmoid(zf)
            siluz = zf * sz
            yv = y_ref[...]
            yg = yv * siluz
            dout = dyb_ref[...].astype(F32)
            dyg_parts, dnw_parts = [], []
            for g in range(SG):
                gs = slice(g * SGW, (g + 1) * SGW)
                ygg = yg[:, gs]
                r = lax.rsqrt(jnp.mean(ygg * ygg, axis=-1, keepdims=True) + EPS)
                yhat = ygg * r
                dn = dout[:, gs] * nw_ref[:, gs]
                dnw_parts.append(jnp.sum(dout[:, gs] * yhat, axis=0, keepdims=True))
                dyg_parts.append(r * (dn - yhat * jnp.mean(dn * yhat, axis=-1, keepdims=True)))
            dyg = jnp.concatenate(dyg_parts, axis=1)
            dnw_ref[...] += jnp.concatenate(dnw_parts, axis=1)
            dy = dyg * siluz
            stage[:, 0:SI] = (dyg * yv * sz * (1.0 + zf * (1.0 - sz))).astype(BF16)
            dd_ref[...] += jnp.sum(dy * xs, axis=0, keepdims=True)
            tt = ea * dy

            da_rows = jnp.zeros((LS, LANE), F32)
            da_cols = jnp.zeros((LANE, LS), F32)
            dxdt_parts, db_parts, dc_parts, daexp_parts = [], [], [], []
            for g in range(SG):
                gs = slice(g * SGW, (g + 1) * SGW)
                bg = pre["bm"][:, g * SN:(g + 1) * SN].astype(BF16)
                cg = pre["cm"][:, g * SN:(g + 1) * SN].astype(BF16)
                sg = sp_ref[0, g]
                sgb = sg.astype(BF16)
                dsg = ds[g]
                dsgb = dsg.astype(BF16)
                ttg = tt[:, gs].astype(BF16)
                yoff = _dot(cg, sgb)
                dc = _dot(ttg, sgb, NT)
                gmat = _dot(cg, bg, NT)
                dgm = jnp.zeros((LS, LS), F32)
                dxdt_pairs = []
                for jj in range(SGW // LANE):
                    h0 = g * (SGW // SP) + 2 * jj
                    ps = slice(g * SGW + jj * LANE, g * SGW + (jj + 1) * LANE)
                    l0 = _head_decay(pre, tril, h0)
                    l1 = _head_decay(pre, tril, h0 + 1)
                    m0 = gmat * l0
                    m1 = gmat * l1
                    dyp = dy[:, ps]
                    dy_lo = jnp.where(lo, dyp, 0.0).astype(BF16)
                    dy_hi = jnp.where(lo, 0.0, dyp).astype(BF16)
                    xpb = xdt[:, ps].astype(BF16)
                    dm0 = _dot(dy_lo, xpb, NT)
                    dm1 = _dot(dy_hi, xpb, NT)
                    q0 = dm0 * m0
                    q1 = dm1 * m1
                    da_rows = da_rows + jnp.where(lane == h0, jnp.sum(q0, axis=1, keepdims=True), 0.0)
                    da_rows = da_rows + jnp.where(lane == h0 + 1, jnp.sum(q1, axis=1, keepdims=True), 0.0)
                    da_cols = da_cols + jnp.where(sub == h0, jnp.sum(q0, axis=0, keepdims=True), 0.0)
                    da_cols = da_cols + jnp.where(sub == h0 + 1, jnp.sum(q1, axis=0, keepdims=True), 0.0)
                    dgm = dgm + dm0 * l0 + dm1 * l1
                    mcat = jnp.concatenate([m0, m1], axis=0).astype(BF16)
                    dycat = jnp.concatenate([dy_lo, dy_hi], axis=0)
                    dxdt_pairs.append(_dot(mcat, dycat, TN))
                dgb = dgm.astype(BF16)
                dc = dc + _dot(dgb, bg)
                db = _dot(dgb, cg, TN)
                zg = _dot(bg, dsgb)
                wg, xdtg = w[:, gs], xdt[:, gs]
                dxdt_g = jnp.concatenate(dxdt_pairs, axis=1) + wg * zg
                qg = zg * xdtg * wg
                last = (jnp.sum(qg, axis=0, keepdims=True)
                        + jnp.sum(dsg * sg, axis=0, keepdims=True) * eal[:, gs])
                daexp_parts.append(dy[:, gs] * ea[:, gs] * yoff - qg + jnp.where(rowi == LS - 1, last, 0.0))
                db = db + _dot((wg * xdtg).astype(BF16), dsgb, NT)
                ds[g] = eal[:, gs] * dsg + _dot(cg, ttg, TN)
                dxdt_parts.append(dxdt_g)
                db_parts.append(db)
                dc_parts.append(dc)
            dxdt = jnp.concatenate(dxdt_parts, axis=1)
            da_exp = jnp.concatenate(daexp_parts, axis=1)
            da = _dot2(da_exp, expand, NT) + da_rows - da_cols.T
            triub = jnp.where(tril, 1.0, 0.0).astype(BF16)
            ddta = _dot3_rhs(triub, da, TN)
            ddt = ddta * pre["a_neg"] + _dot2(dxdt * xs, expand, NT)
            da_ref[...] += jnp.sum(ddta * pre["dt"], axis=0, keepdims=True)
            ddt_raw = ddt * _sigmoid(pre["dtin"])
            ddt_ref[...] = ddt_raw
            ddtb_ref[...] += jnp.sum(ddt_raw, axis=0, keepdims=True)
            dxs = dexp_ref[...] * dy + dxdt * pre["dt_exp"]
            dxbc = jnp.concatenate([dxs] + db_parts + dc_parts, axis=1)
            sx, xc = pre["sx"], pre["xc"]
            dxc = dxbc * sx * (1.0 + xc * (1.0 - sx))
            taps = _rows_up(dxc.astype(BF16), ahead[...], (3, 2, 1)) + [dxc]
            xr = x_ref[...].astype(F32)
            dcb_ref[...] += jnp.sum(dxc, axis=0, keepdims=True)
            dcw_ref[...] += jnp.concatenate([jnp.sum(t * xr, axis=0, keepdims=True) for t in taps], axis=0)
            dxr = cw_ref[0:1, :] * taps[0]
            for k in range(1, SK):
                dxr = dxr + cw_ref[k:k + 1, :] * taps[k]
            stage[:, SI:] = dxr.astype(BF16)
            ahead[...] = dxc[0:H, :]

        dp_ref[...] = stage[...]

    vec = lambda w: pl.BlockSpec((1, w), lambda i, j: (0, 0))
    rev = lambda w, cb_: pl.BlockSpec((LS, w), lambda i, j: (nc - 1 - i, cb_))
    outs = pl.pallas_call(
        body, grid=(nc, NJ),
        in_specs=[rev(SI, 2), rev(SXBC, 2), rev(SXBC, 0),
                  rev(LANE, 0), rev(SI, 0),
                  pl.BlockSpec((1, SG, SN, SGW), lambda i, j: (nc - 1 - i, 0, 0, 0)),
                  rev(SI, 0), pl.BlockSpec(memory_space=pl.ANY),
                  pl.BlockSpec((SK, SXBC), lambda i, j: (0, 0)), vec(LANE), vec(LANE), vec(SI), vec(SI),
                  pl.BlockSpec((LANE, SI), lambda i, j: (0, 0))],
        out_specs=[rev(DP_SSM, 1), rev(LANE, 0),
                   pl.BlockSpec((SK, SXBC), lambda i, j: (0, 0)), vec(SXBC), vec(LANE), vec(LANE), vec(SI), vec(SI)],
        out_shape=[jax.ShapeDtypeStruct(dproj.shape, dproj.dtype), jax.ShapeDtypeStruct((T, LANE), F32),
                   jax.ShapeDtypeStruct((SK, SXBC), F32), jax.ShapeDtypeStruct((1, SXBC), F32),
                   jax.ShapeDtypeStruct((1, LANE), F32), jax.ShapeDtypeStruct((1, LANE), F32),
                   jax.ShapeDtypeStruct((1, SI), F32), jax.ShapeDtypeStruct((1, SI), F32)],
        scratch_shapes=[pltpu.VMEM((H, SXBC), F32),
                        pltpu.VMEM((SG, SN, SGW), F32), pltpu.VMEM((LS, SI + SXBC), BF16)],
        input_output_aliases={7: 0}, name="ssd_bwd",
        compiler_params=_params(("arbitrary", "arbitrary"), VMEM_CAP_MB))(
            proj, proj, xcs, dtraw, y, sprev, dyb, dproj, cw, dtb, alog, dexp, nw, _head_expansion())
    return outs


def _perm_ffn_cols(a):
    lead = a.shape[:-1]
    return a.reshape(lead + (2, 2, FT)).swapaxes(-3, -2).reshape(lead + (2 * DFF,))


def _perm_ffn_rows(a):
    return a.reshape((2, 2, FT) + a.shape[1:]).swapaxes(0, 1).reshape(a.shape)


def _pad_lanes(v, n=LANE):
    return jnp.pad(v, ((0, 0), (0, n - v.shape[-1])))


LATE = ["w_proj_a", "w_proj_b", "w_out", "ffn_w_up_t", "ffn_w_down"]
WGRAD = BF16
SMALL_BF16_FROM = 2 ** 16


class _NoExchange:
    def early_start(self):
        return None

    def early_weights(self, w, outs):
        return w

    def gather_start(self):
        return None

    def gather_pass_on(self, outs):
        return None

    def late_weights(self, w, outs):
        return w

    def late_to_sibling(self, grads):
        return None

    def reduce_late(self, outs):
        return None

    def w_in_to_sibling(self, grad_main, grad_dt):
        return None

    def reduce_w_in(self, outs):
        return None

    def reduced(self, late_outs, w_in_outs):
        pass


def _local_step(x, tgt, w, hooks=None):
    hooks = hooks or _NoExchange()

    def mm(*args, side=None, **kw):
        out = _matmul(*args, side=side, **kw)
        return out if side is not None else (out, [])

    mixw = w["mix_norm_w"][None, :]
    side = hooks.early_start()
    xn, got = _rms_fwd(x, mixw, name="mix_norm", side=side) if side else (_rms_fwd(x, mixw, name="mix_norm"), [])
    w = hooks.early_weights(w, got)
    win_t = w["w_in_t"]
    win_dt = jnp.pad(w["w_in_t"][PMAIN:], ((0, LANE - SH), (0, 0)))
    fcw = _perm_ffn_cols(w["ffn_conv_w"])
    fcb = _perm_ffn_cols(w["ffn_conv_b"][None, :])
    ffnw = w["ffn_norm_w"][None, :]
    finw = w["final_norm_w"][None, :]
    bst = w["gmlp_bs"].T
    scb = w["ssm_conv_b"][None, :]
    dtb = _pad_lanes(w["ssm_dt_bias"][None, :])
    alog = _pad_lanes(w["ssm_a_log"][None, :])
    dexp = jnp.repeat(w["ssm_d"], SP)[None, :]
    snw = w["ssm_norm_w"][None, :]

    proj, got = mm(xn, win_t, name="in_proj", out_dtype=BF16, tb=True, tn=3072, j_outer=True, b_rows=PMAIN,
                   side=hooks.gather_start())
    dtraw = _matmul(xn, win_dt, name="in_proj_dt", out_dtype=F32, tb=True)
    ya_pre, got = _gmlp_fwd(proj, w["gmlp_ln_w"], w["gmlp_ln_b"], w["gmlp_ws"], bst, side=hooks.gather_pass_on(got))
    w = hooks.late_weights(w, got)
    wup = _perm_ffn_rows(w["ffn_w_up_t"])
    yb_pre, y_ssd, sprev, ssm_xc = _ssd_fwd(proj, dtraw, w["ssm_conv_w"], scb, dtb, alog, dexp, snw)
    merged, ya, yb = _merge_fwd(ya_pre, yb_pre, proj, w["gate_bias"], w["w_proj_a"], w["w_proj_b"])
    h1 = _matmul(merged, w["w_out"], name="out_proj", out_dtype=F32, add=x)
    hn = _rms_fwd(h1, ffnw, name="ffn_norm")
    up = _matmul(hn, wup, name="ffn_up", out_dtype=BF16, tb=True, tn=2 * FT, j_outer=True)
    act, ffn_xc = _ffn_act_fwd(up, fcw, fcb)
    h2 = _matmul(act, w["ffn_w_down"], name="ffn_down", out_dtype=F32, tk=DFF, add=h1)

    loss_row, dh2, d_finw = _loss_head(h2, tgt, finw)
    dact = _matmul(dh2, w["ffn_w_down"], name="ffn_down_dx", out_dtype=BF16, tb=True, tn=DFF)
    d_wdown = _matmul(act, dh2, name="ffn_down_dw", out_dtype=WGRAD, ta=True, tm=FT, tk=2048)
    dup, d_fcw, d_fcb = _ffn_act_bwd(up, ffn_xc, dact, fcw)
    dhn = _matmul(dup, wup, name="ffn_up_dx", out_dtype=F32, tk=2 * FT)
    d_wup = _matmul(dup, hn, name="ffn_up_dw", out_dtype=WGRAD, ta=True, tm=FT, tk=2048,
                    o_row=lambda i: (i % 2) * 2 + i // 2)
    dh1, d_ffnw = _rms_bwd(h1, ffnw, dhn, dh2, name="ffn_norm_bwd")
    dmerged = _matmul(dh1, w["w_out"], name="out_proj_dx", out_dtype=BF16, tb=True)
    d_wout = _matmul(merged, dh1, name="out_proj_dw", out_dtype=WGRAD, ta=True, tk=2048)
    dproj, dya, dyb, dya_pre, dyb_pre, d_gbias = _merge_bwd(dmerged, proj, w["gate_bias"], ya, yb,
                                                           w["w_proj_a"], w["w_proj_b"])
    d_wpa = _matmul(ya_pre, dya, name="proj_a_dw", out_dtype=WGRAD, ta=True, tk=2048)
    d_wpb = _matmul(yb_pre, dyb, name="proj_b_dw", out_dtype=WGRAD, ta=True, tk=2048)
    late = {"w_proj_a": d_wpa, "w_proj_b": d_wpb, "w_out": d_wout, "ffn_w_up_t": d_wup,
            "ffn_w_down": d_wdown}
    (dproj, d_lnw, d_lnb, d_ws, d_bst), got = _gmlp_bwd(proj, dya_pre, dproj, w["gmlp_ln_w"], w["gmlp_ln_b"],
                                                        w["gmlp_ws"], bst, side=hooks.late_to_sibling(late))
    dproj, ddt, d_scw, d_scb, d_dtb, d_a, d_dch, d_snw = _ssd_bwd(
        proj, ssm_xc, dtraw, y_ssd, sprev, dyb_pre, dproj, w["ssm_conv_w"], dtb, alog, dexp, snw)
    gap = (2 * D + 2 * GW, DP_GAP)
    d_win_main, late_outs = mm(dproj, xn, name="in_proj_dw", out_dtype=WGRAD, ta=True, a_gap=gap, tk=4096,
                               side=hooks.reduce_late(got))
    d_win_dt = _matmul(ddt, xn, name="in_proj_dt_dw", out_dtype=F32, ta=True)
    d_win_t = jnp.concatenate([d_win_main, d_win_dt[:SH]], axis=0)
    dxn, got = mm(ddt, win_dt, name="in_proj_dt_dx", out_dtype=F32,
                  side=hooks.w_in_to_sibling(d_win_main, d_win_dt[:SH]))
    dxn, w_in_outs = mm(dproj, win_t, name="in_proj_dx", out_dtype=F32, add=dxn, b_rows=PMAIN, a_gap=gap,
                        side=hooks.reduce_w_in(got))
    hooks.reduced(late_outs, w_in_outs)
    grad_x, d_mixw = _rms_bwd(x, mixw, dxn, dh1, name="mix_norm_bwd")

    a_neg = -jnp.exp(w["ssm_a_log"])
    grads = {
        "mix_norm_w": d_mixw[0],
        "w_in_t": d_win_t,
        "gate_bias": d_gbias,
        "gmlp_ln_w": d_lnw, "gmlp_ln_b": d_lnb, "gmlp_ws": d_ws, "gmlp_bs": d_bst[:, :GG].T,
        "ssm_conv_w": d_scw, "ssm_conv_b": d_scb[0],
        "ssm_dt_bias": d_dtb[0, :SH], "ssm_a_log": d_a[0, :SH] * a_neg,
        "ssm_d": d_dch.reshape(SH, SP).sum(axis=-1), "ssm_norm_w": d_snw[0],
        **late,
        "ffn_norm_w": d_ffnw[0],
        "ffn_conv_w": _perm_ffn_cols(d_fcw), "ffn_conv_b": _perm_ffn_cols(d_fcb)[0],
        "ffn_w_down": d_wdown, "final_norm_w": d_finw[0],
    }
    return loss_row, grad_x, grads


MESH = pl.DeviceIdType.MESH
HBM_SPEC = pl.BlockSpec(memory_space=pltpu.HBM)


def _axes():
    return lax.axis_index("x"), lax.axis_index("y"), lax.axis_index("c")


def _run_side(side, *, name):
    n_in, n_out = len(side.inputs), len(side.out_shapes)

    def body(*refs):
        copies = side.make(refs[:n_in], refs[n_in:n_in + n_out], *refs[-2:])
        for cp in copies:
            cp.start()
        for cp in copies:
            cp.wait()

    return pl.pallas_call(
        body, out_shape=list(side.out_shapes), in_specs=[HBM_SPEC] * n_in, out_specs=[HBM_SPEC] * n_out,
        scratch_shapes=[pltpu.SemaphoreType.DMA((side.nsem,)), pltpu.SemaphoreType.DMA((side.nsem,))],
        input_output_aliases={i: j for i, j in side.aliases}, name=name)(*side.inputs)


def _exchange(srcs, plan, *, name):
    na = len(srcs)
    n = len(plan(0, 0, 0))

    def body(*refs):
        src_refs, out_refs = refs[:na], refs[na:2 * na]
        send_sems, recv_sems = refs[2 * na:]
        x, y, c = _axes()
        copies = []
        for k, (slab, peer) in enumerate(plan(x, y, c)):
            for a in range(na):
                cp = pltpu.make_async_remote_copy(
                    src_ref=src_refs[a].at[slab], dst_ref=out_refs[a].at[k], send_sem=send_sems.at[n * a + k],
                    recv_sem=recv_sems.at[n * a + k], device_id=peer, device_id_type=MESH)
                cp.start()
                copies.append(cp)
        for cp in copies:
            cp.wait()

    return pl.pallas_call(
        body, out_shape=[jax.ShapeDtypeStruct((n,) + s.shape[1:], s.dtype) for s in srcs],
        in_specs=[HBM_SPEC] * na, out_specs=[HBM_SPEC] * na,
        scratch_shapes=[pltpu.SemaphoreType.DMA((n * na,)), pltpu.SemaphoreType.DMA((n * na,))], name=name)(*srcs)


def _to_sibling_plan(x, y, c):
    return [(2 * q + (1 - c), (x, y, 1 - c)) for q in range(4)]


def _to_chips_plan(x, y, c):
    q = 2 * x + y
    return [(q ^ 2, (1 - x, y, c)), (q ^ 1, (x, 1 - y, c)), (q ^ 3, (1 - x, 1 - y, c))]


def _row_tile(rows, row_bytes, budget=2 * 2 ** 20, align=2 * SUBLANE):
    if rows * row_bytes <= 2 * budget:
        return rows
    best = None
    for d in range(align, rows + 1, align):
        if rows % d == 0 and d * row_bytes <= budget:
            best = d
    return best or rows


def _pair_add(g, ra, c_idx, *, name):
    _, _, R, C = g.shape
    tr = _row_tile(R, C * 4, budget=3 * 2 ** 20)

    def body(c_ref, g_ref, ra_ref, o_ref):
        del c_ref
        o_ref[...] = (g_ref[0].astype(F32) + ra_ref[...].astype(F32)).astype(o_ref.dtype)

    return pl.pallas_call(
        body,
        grid_spec=pltpu.PrefetchScalarGridSpec(
            num_scalar_prefetch=1, grid=(4, R // tr),
            in_specs=[pl.BlockSpec((1, 1, tr, C), lambda q, r, cr: (q, cr[0], r, 0)),
                      pl.BlockSpec((1, tr, C), lambda q, r, cr: (q, r, 0))],
            out_specs=pl.BlockSpec((1, tr, C), lambda q, r, cr: (q, r, 0))),
        out_shape=jax.ShapeDtypeStruct((4, R, C), g.dtype), name=name,
        compiler_params=_params(("arbitrary", "arbitrary"), 24))(c_idx, g, ra)


def _grad_sum(p, rb, q_idx, *, name):
    _, R, C = p.shape
    tr = _row_tile(R, C * 4, budget=3 * 2 ** 20)

    def body(q_ref, p_ref, rb_ref, o_ref):
        del q_ref
        g = p_ref[0].astype(F32)
        for k in range(3):
            g = g + rb_ref[k].astype(F32)
        o_ref[...] = g

    return pl.pallas_call(
        body,
        grid_spec=pltpu.PrefetchScalarGridSpec(
            num_scalar_prefetch=1, grid=(R // tr,),
            in_specs=[pl.BlockSpec((1, tr, C), lambda r, qr: (qr[0], r, 0)),
                      pl.BlockSpec((3, tr, C), lambda r, qr: (0, r, 0))],
            out_specs=pl.BlockSpec((tr, C), lambda r, qr: (r, 0))),
        out_shape=jax.ShapeDtypeStruct((R, C), F32), name=name,
        compiler_params=_params(("arbitrary",), 40))(q_idx, p, rb)


def _adamw(g, w, m, v):
    m = ADAM_B1 * m + (1.0 - ADAM_B1) * g
    v = ADAM_B2 * v + (1.0 - ADAM_B2) * (g * g)
    m_hat = m / (1.0 - ADAM_B1 ** ADAM_STEP)
    v_hat = v / (1.0 - ADAM_B2 ** ADAM_STEP)
    delta = -ADAM_LR * (m_hat / (jnp.sqrt(v_hat) + ADAM_EPS) + ADAM_WD * w)
    return delta, m, v


def _adam(g, w, m, v, *, name):
    _, R, C = w.shape
    tr = _row_tile(R, C * 4, budget=2 ** 20, align=SUBLANE)

    def body(g_ref, w_ref, m_ref, v_ref, d_out, m_out, v_out):
        delta, mn, vn = _adamw(g_ref[...], w_ref[...], m_ref[...], v_ref[...])
        d_out[...] = delta
        m_out[...] = mn
        v_out[...] = vn

    row = pl.BlockSpec((1, tr, C), lambda r: (0, r, 0))
    o = jax.ShapeDtypeStruct((1, R, C), F32)
    return pl.pallas_call(
        body, grid=(R // tr,), in_specs=[row, row, row, row], out_specs=[row, row, row], out_shape=[o, o, o],
        name=name, compiler_params=_params(("arbitrary",), 32))(g, w, m, v)


def _vmem_specs(n):
    return [pl.BlockSpec(memory_space=pltpu.VMEM)] * n


def _pair_sum_many(mine, theirs, *, name):
    n = len(mine)

    def body(*refs):
        for a in range(n):
            refs[2 * n + a][...] = refs[a][...] + refs[n + a][0]

    return pl.pallas_call(
        body, out_shape=[jax.ShapeDtypeStruct(m.shape, m.dtype) for m in mine], in_specs=_vmem_specs(2 * n),
        out_specs=_vmem_specs(n), name=name)(*mine, *theirs)


def _chip_sum_many(own, recv, q_idx, *, name):
    n = len(own)

    def body(q_ref, *refs):
        q = q_ref[0]
        for a in range(n):
            mine, r = refs[a][...], refs[n + a]
            total = None
            for chip in range(4):
                e = q ^ chip
                term = jnp.where(e == 0, mine, jnp.where(e == 2, r[0], jnp.where(e == 1, r[1], r[2]))).astype(F32)
                total = term if total is None else total + term
            refs[2 * n + a][...] = total

    return pl.pallas_call(
        body, out_shape=[jax.ShapeDtypeStruct(m.shape, F32) for m in own],
        in_specs=[pl.BlockSpec(memory_space=pltpu.SMEM)] + _vmem_specs(2 * n), out_specs=_vmem_specs(n),
        name=name)(q_idx, *own, *recv)


def _adam_many(gs, ws, ms, vs, *, name):
    n = len(gs)

    def body(*refs):
        for a in range(n):
            delta, mn, vn = _adamw(*(refs[k * n + a][...] for k in range(4)))
            refs[4 * n + a][...] = delta
            refs[5 * n + a][...] = mn
            refs[6 * n + a][...] = vn

    shapes = [jax.ShapeDtypeStruct(w.shape, w.dtype) for w in ws]
    out = pl.pallas_call(body, out_shape=shapes * 3, in_specs=_vmem_specs(4 * n), out_specs=_vmem_specs(3 * n),
                         name=name)(*gs, *ws, *ms, *vs)
    return out[:n], out[n:2 * n], out[2 * n:]


WEIGHTS = ["mix_norm_w", "w_in", "gate_bias", "gmlp_ln_w", "gmlp_ln_b", "gmlp_ws", "gmlp_bs", "ssm_conv_w",
           "ssm_conv_b", "ssm_dt_bias", "ssm_a_log", "ssm_d", "ssm_norm_w", "w_proj_a", "w_proj_b", "w_out",
           "ffn_norm_w", "ffn_w_up", "ffn_conv_w", "ffn_conv_b", "ffn_w_down", "final_norm_w"]
SHARDED = {"w_in": ((D, IN_COLS), 1), "gate_bias": ((2, D), 1), "ssm_conv_w": ((SK, SXBC), 1),
           "w_proj_a": ((GW, D), 0), "w_proj_b": ((SI, D), 0), "w_out": ((D, D), 0),
           "ffn_w_up": ((D, 2 * DFF), 1), "ffn_conv_w": ((FK, 2 * DFF), 1), "ffn_w_down": ((DFF, D), 0)}
REPLICATED = {"mix_norm_w": (D,), "gmlp_ln_w": (GG, GD), "gmlp_ln_b": (GG, GD), "gmlp_ws": (GG, GB, GB),
              "gmlp_bs": (GG, GB), "ssm_conv_b": (SXBC,), "ssm_dt_bias": (SH,), "ssm_a_log": (SH,), "ssm_d": (SH,),
              "ssm_norm_w": (SI,), "ffn_norm_w": (D,), "ffn_conv_b": (2 * DFF,), "final_norm_w": (D,)}
REPL_ORDER = [n for n in WEIGHTS if n in REPLICATED]
BTILE = 2 * SUBLANE
WIN_R = IN_COLS // NDEV
WIN_P = WIN_R + BTILE - WIN_R % BTILE
WIN_A = [WIN_R * d // BTILE * BTILE for d in range(NDEV)]
assert all(WIN_A[d] + WIN_P >= WIN_R * (d + 1) for d in range(NDEV)) and WIN_A[-1] + WIN_P == IN_COLS
BIG = [("w_proj_a", GW // NDEV, False), ("w_proj_b", SI // NDEV, False), ("w_out", D // NDEV, False),
       ("ffn_w_up", 2 * DFF // NDEV, True), ("ffn_w_down", DFF // NDEV, False), ("w_in", WIN_P, True)]
VECTORS = ["gate_bias", "ssm_conv_w", "ffn_conv_w"]


def _round_up(n, k):
    return (n + k - 1) // k * k


BIG_OFF = {}
_off = 0
for _n, _r, _t in BIG:
    BIG_OFF[_n] = _off
    _off += _r
BIG_USED = _off
BIG_ROWS = _round_up(BIG_USED, 2 * SUBLANE)
assert all(BIG_OFF[n] % (2 * SUBLANE) == 0 for n, _, _ in BIG)
VEC_SHAPE = {n: (SHARDED[n][0][0], SHARDED[n][0][1] // NDEV) for n in VECTORS}


def _win_offset(dev):
    return WIN_R * dev - WIN_R * dev // BTILE * BTILE


def _pack_big(arrs, dtype, dev):
    parts = []
    for n, r, t in BIG:
        a = (arrs[n].T if t else arrs[n]).astype(dtype)
        if n == "w_in":
            a = lax.dynamic_update_slice(jnp.zeros((WIN_P, D), dtype), a, (_win_offset(dev), 0))
        parts.append(a)
    parts.append(jnp.zeros((BIG_ROWS - BIG_USED, D), dtype))
    return jnp.concatenate(parts, axis=0)


def _join_windows(win):
    parts = []
    for d in range(NDEV):
        lo = BTILE if WIN_A[d] % WIN_R else 0
        if lo:
            parts.append(win[d - 1, WIN_P - BTILE:] + win[d, :BTILE])
        hi = WIN_P - BTILE if d + 1 < NDEV and WIN_A[d + 1] < WIN_A[d] + WIN_P else WIN_P
        parts.append(win[d, lo:hi])
    return jnp.concatenate(parts, axis=0)


def _split_windows(main, last):
    assert WIN_A[-2] + WIN_P <= PMAIN
    wins = [main[a:a + WIN_P] for a in WIN_A[:-1]]
    return jnp.stack(wins + [jnp.concatenate([main[WIN_A[-1]:], last], axis=0)])


LATE_ROWS = BIG_OFF["w_in"]
assert LATE_ROWS + WIN_P == BIG_ROWS and BIG[-1][0] == "w_in"


def _remote(src, dst, send_sems, recv_sems, k, to):
    return pltpu.make_async_remote_copy(src_ref=src, dst_ref=dst, send_sem=send_sems.at[k], recv_sem=recv_sems.at[k],
                                        device_id=to, device_id_type=MESH)


class _Exchange:
    def __init__(self, early_shards, late_shard, c_idx):
        self.early_shards, self.late_shard, self.c_idx = early_shards, late_shard, c_idx

    @staticmethod
    def _gather_side(shards):
        def make(ins, outs, send_sems, recv_sems):
            x, y, c = _axes()
            peers = [(x, y, 1 - c), (1 - x, y, c), (x, 1 - y, c), (1 - x, 1 - y, c)]
            copies = []
            for a, (x_ref, out) in enumerate(zip(ins, outs)):
                mine = out.at[4 * x + 2 * y + c]
                copies += [_remote(x_ref, mine, send_sems, recv_sems, 5 * a + k, p) for k, p in enumerate(peers)]
                copies.append(pltpu.make_async_copy(x_ref, mine, send_sems.at[5 * a + 4]))
            return copies

        return _Side(shards, [jax.ShapeDtypeStruct((NDEV,) + s.shape, s.dtype) for s in shards], 5 * len(shards), make)

    @staticmethod
    def _pass_on_side(bufs):
        def make(ins, outs, send_sems, recv_sems):
            x, y, c = _axes()
            slots = [4 * px + 2 * py + c for px, py in [(1 - x, y), (x, 1 - y), (1 - x, 1 - y)]]
            return [_remote(src.at[s], dst.at[s], send_sems, recv_sems, 3 * a + k, (x, y, 1 - c))
                    for a, (src, dst) in enumerate(zip(ins, outs)) for k, s in enumerate(slots)]

        return _Side(bufs, [jax.ShapeDtypeStruct(b.shape, b.dtype) for b in bufs], 3 * len(bufs), make,
                     aliases=[(a, a) for a in range(len(bufs))])

    def early_start(self):
        return self._gather_side(self.early_shards)

    def early_weights(self, w, outs):
        bufs = _run_side(self._pass_on_side(outs), name="w_in_pass_on")
        w = dict(w)
        w["w_in_t"] = _join_windows(bufs[0])
        for n, a in zip(VECTORS, bufs[1:]):
            r, c = VEC_SHAPE[n]
            w[n] = a[:, 0].transpose(1, 0, 2).reshape(r, NDEV * c)
        return w

    def gather_start(self):
        return self._gather_side([self.late_shard])

    def gather_pass_on(self, outs):
        return self._pass_on_side(outs)

    def late_weights(self, w, outs):
        (buf,) = outs
        w = dict(w)
        for n, r, t in BIG[:-1]:
            w[n + "_t" if t else n] = buf[:, BIG_OFF[n]:BIG_OFF[n] + r].reshape(NDEV * r, D)
        return w

    @staticmethod
    def _plan_side(src, plan):
        n = len(plan(0, 0, 0))

        def make(ins, outs, send_sems, recv_sems):
            (s,), (dst,) = ins, outs
            return [_remote(s.at[slab], dst.at[k], send_sems, recv_sems, k, peer)
                    for k, (slab, peer) in enumerate(plan(*_axes()))]

        return _Side([src], [jax.ShapeDtypeStruct((n,) + src.shape[1:], src.dtype)], n, make)

    def _to_chips(self, send, sib, tag):
        sums = _pair_add(send.reshape((4, 2) + send.shape[1:]), sib, self.c_idx, name=tag + "_grad_pair_add")
        return sums, self._plan_side(sums, _to_chips_plan)

    def late_to_sibling(self, grads):
        send = jnp.concatenate([grads[n + "_t" if t else n].reshape(NDEV, r, D) for n, r, t in BIG[:-1]], axis=1)
        self.late_send = send.astype(BF16)
        return self._plan_side(self.late_send, _to_sibling_plan)

    def reduce_late(self, outs):
        self.late_sum, side = self._to_chips(self.late_send, outs[0], "late")
        return side

    def w_in_to_sibling(self, grad_main, grad_dt):
        self.w_in_send = _split_windows(grad_main.astype(BF16), grad_dt.astype(BF16))
        return self._plan_side(self.w_in_send, _to_sibling_plan)

    def reduce_w_in(self, outs):
        self.w_in_sum, side = self._to_chips(self.w_in_send, outs[0], "w_in")
        return side

    def reduced(self, late_outs, w_in_outs):
        (self.late_from_chips,), (self.w_in_from_chips,) = late_outs, w_in_outs


def kernel(x, mix_norm_w, w_in, gate_bias, gmlp_ln_w, gmlp_ln_b, gmlp_ws, gmlp_bs, ssm_conv_w, ssm_conv_b, ssm_dt_bias, ssm_a_log, ssm_d, ssm_norm_w, w_proj_a, w_proj_b, w_out, ffn_norm_w, ffn_w_up, ffn_conv_w, ffn_conv_b, ffn_w_down, final_norm_w, loss_target, m_mix_norm_w, m_w_in, m_gate_bias, m_gmlp_ln_w, m_gmlp_ln_b, m_gmlp_ws, m_gmlp_bs, m_ssm_conv_w, m_ssm_conv_b, m_ssm_dt_bias, m_ssm_a_log, m_ssm_d, m_ssm_norm_w, m_w_proj_a, m_w_proj_b, m_w_out, m_ffn_norm_w, m_ffn_w_up, m_ffn_conv_w, m_ffn_conv_b, m_ffn_w_down, m_final_norm_w, v_mix_norm_w, v_w_in, v_gate_bias, v_gmlp_ln_w, v_gmlp_ln_b, v_gmlp_ws, v_gmlp_bs, v_ssm_conv_w, v_ssm_conv_b, v_ssm_dt_bias, v_ssm_a_log, v_ssm_d, v_ssm_norm_w, v_w_proj_a, v_w_proj_b, v_w_out, v_ffn_norm_w, v_ffn_w_up, v_ffn_conv_w, v_ffn_conv_b, v_ffn_w_down, v_final_norm_w):
    given = dict(locals())
    wts = {n: given[n] for n in WEIGHTS}
    mom = {n: given["m_" + n] for n in WEIGHTS}
    var = {n: given["v_" + n] for n in WEIGHTS}
    xi, yi, ci = _axes()
    c_idx = jnp.reshape(ci, (1,)).astype(jnp.int32)
    q_idx = jnp.reshape(2 * xi + yi, (1,)).astype(jnp.int32)
    big_names = [n for n, _, _ in BIG]
    drop = lambda d, names: {n: d[n][0] for n in names}

    dev = 4 * xi + 2 * yi + ci
    packed = _pack_big(drop(wts, big_names), BF16, dev)
    full = {n: wts[n].reshape(REPLICATED[n]) for n in REPL_ORDER}

    hooks = _Exchange([packed[LATE_ROWS:]] + [wts[n] for n in VECTORS], packed[:LATE_ROWS], c_idx)
    loss_local, grad_x, grads = _local_step(x[0], loss_target[0], full, hooks)
    g_late = _grad_sum(hooks.late_sum, hooks.late_from_chips, q_idx, name="late_grad_sum")
    g_win = _grad_sum(hooks.w_in_sum, hooks.w_in_from_chips, q_idx, name="w_in_grad_sum")

    small = VECTORS + REPL_ORDER
    as_2d = lambda a: a if a.ndim >= 2 else a[None]
    part = [grads[n].reshape((1,) + SHARDED[n][0] if n in VECTORS else as_2d(wts[n]).shape) for n in small]
    part.append(loss_local)
    from_sibling = _exchange([p[None] for p in part], lambda x, y, c: [(0, (x, y, 1 - c))],
                             name="small_grads_to_sibling")
    chip_sums = _pair_sum_many(part, from_sibling, name="small_grad_pair_sum")
    chip_sums = [s.astype(BF16) if s.size >= SMALL_BF16_FROM else s for s in chip_sums]
    from_chips = _exchange([s[None] for s in chip_sums],
                           lambda x, y, c: [(0, (1 - x, y, c)), (0, (x, 1 - y, c)), (0, (1 - x, 1 - y, c))],
                           name="small_grads_to_chips")
    totals = _chip_sum_many(chip_sums, from_chips, q_idx, name="small_grad_chip_sum")
    g_small, loss = dict(zip(small, totals)), totals[-1][0, 0]
    for n in VECTORS:
        c = VEC_SHAPE[n][1]
        g_small[n] = lax.dynamic_slice_in_dim(g_small[n], dev * c, c, axis=2)

    outs = {}
    small_g = [g_small[n] for n in small]
    small_out = _adam_many(small_g, *[[as_2d(d[n]) for n in small] for d in (wts, mom, var)], name="adam_small")
    for i, n in enumerate(small):
        outs[n] = tuple(a[i].reshape(wts[n].shape) for a in (small_g,) + tuple(small_out))
    for n, r, t in BIG:
        if n == "w_in":
            g = lax.dynamic_slice(g_win, (_win_offset(dev), 0), (WIN_R, D))
        else:
            g = g_late[BIG_OFF[n]:BIG_OFF[n] + r]
        flip = (lambda a: a.transpose(0, 2, 1)) if t else (lambda a: a)
        g = g[None]
        new = _adam(g, flip(wts[n]), flip(mom[n]), flip(var[n]), name="adam_" + n)
        outs[n] = tuple(flip(a) for a in (g,) + tuple(new))
    return (loss, grad_x[None]) + tuple(outs[n][k] for k in range(4) for n in WEIGHTS)
```

```python
import functools

import jax
import jax.numpy as jnp
from jax import lax
from jax.experimental import pallas as pl
from jax.experimental.pallas import tpu as pltpu

F32 = jnp.float32
BF16 = jnp.bfloat16

D = 1024
EPS = 1e-5
GW = 1024
GB = 128
GG = 8
GD = 128
GCH = 64
SI = 2048
SH = 32
SP = 64
SG = 4
SN = 128
SGW = SI // SG
SK = 4
SXBC = SI + 2 * SG * SN
DFF = 2816
FK = 3
PMAIN = 2 * D + 2 * GW + SI + SXBC
IN_COLS = PMAIN + SH
DP_SSM = SI + SXBC
DP_GAP = (DP_SSM - (2 * D + 2 * GW) % DP_SSM) % DP_SSM
DP_COLS = 2 * D + 2 * GW + DP_GAP + DP_SSM
assert DP_GAP % D == 0 and (2 * D + 2 * GW) % D == 0
NDEV = 8
ADAM_LR, ADAM_B1, ADAM_B2, ADAM_EPS, ADAM_WD, ADAM_STEP = 0.001, 0.9, 0.999, 1e-08, 0.01, 10

LANE = 128
SUBLANE = 8
VMEM_MB_V7X = 64
VMEM_CAP_MB = VMEM_MB_V7X - 8

LS = 128
FT = DFF // 2

NN = (((1,), (0,)), ((), ()))
NT = (((1,), (1,)), ((), ()))
TN = (((0,), (0,)), ((), ()))


def _params(sem, vmem_mb):
    return pltpu.CompilerParams(dimension_semantics=sem,
                                vmem_limit_bytes=min(int(vmem_mb), VMEM_CAP_MB) * 1024 * 1024)


def _dot(a, b, dims=NN):
    return lax.dot_general(a, b, dims, preferred_element_type=F32)


def _sigmoid(x):
    return 1.0 / (1.0 + jnp.exp(-x))


def _split3(v):
    hi = v.astype(BF16)
    r = v - hi.astype(F32)
    mid = r.astype(BF16)
    lo = (r - mid.astype(F32)).astype(BF16)
    return hi, mid, lo


def _dot3(a_f32, b_bf16, dims):
    hi, mid, lo = _split3(a_f32)
    return _dot(hi, b_bf16, dims) + _dot(mid, b_bf16, dims) + _dot(lo, b_bf16, dims)


def _dot2(a_f32, b_bf16, dims):
    hi, mid, _ = _split3(a_f32)
    return _dot(hi, b_bf16, dims) + _dot(mid, b_bf16, dims)


def _dot3_rhs(a_bf16, b_f32, dims):
    hi, mid, lo = _split3(b_f32)
    return _dot(a_bf16, hi, dims) + _dot(a_bf16, mid, dims) + _dot(a_bf16, lo, dims)


def _matmul(a, b, *, name, out_dtype, ta=False, tb=False, tm=1024, tn=1024, tk=1024, add=None,
            j_outer=False, b_rows=None, a_gap=None, o_row=None, side=None):
    assert not (o_row and add is not None)
    gap0, gapw = a_gap or (0, 0)
    if ta:
        K, M = a.shape
        M -= gapw
    else:
        M, K = a.shape
        K -= gapw
    if tb:
        N, K2 = b.shape
        N = b_rows or N
    else:
        K2, N = b.shape
        K2 = b_rows or K2
    assert K == K2, (a.shape, b.shape, ta, tb)
    tm, tn, tk = min(tm, M), min(tn, N), min(tk, K)
    assert M % tm == 0 and N % tn == 0 and K % tk == 0, (M, N, K, tm, tn, tk)
    nk = K // tk
    dims = (((0 if ta else 1,), (1 if tb else 0,)), ((), ()))
    has_add = add is not None
    n_in = 3 if has_add else 2
    s_in = len(side.inputs) if side else 0
    s_out = len(side.out_shapes) if side else 0
    grid = (N // tn, M // tm, nk) if j_outer else (M // tm, N // tn, nk)

    def body(*refs):
        a_ref, b_ref = refs[:2]
        add_ref = refs[2] if has_add else None
        o_ref = refs[n_in + s_in]
        if side:
            side_refs = (refs[n_in:n_in + s_in], refs[n_in + s_in + 1:n_in + s_in + 1 + s_out]) + tuple(refs[-2:])
            ids = [pl.program_id(d) for d in range(3)]
            first = functools.reduce(jnp.logical_and, [i == 0 for i in ids])
            last = functools.reduce(jnp.logical_and, [i == g - 1 for i, g in zip(ids, grid)])

            @pl.when(first)
            def _():
                for cp in side.make(*side_refs):
                    cp.start()

            @pl.when(last)
            def _():
                for cp in side.make(*side_refs):
                    cp.wait()

        p = lax.dot_general(a_ref[...].astype(BF16), b_ref[...].astype(BF16), dims,
                            preferred_element_type=F32)

        def finish(acc):
            if has_add:
                acc = acc + add_ref[...].astype(F32)
            o_ref[...] = acc.astype(o_ref.dtype)

        if nk == 1:
            finish(p)
        else:
            acc_ref = refs[n_in + s_in + 1 + s_out]
            k = pl.program_id(2)

            @pl.when(k == 0)
            def _():
                acc_ref[...] = p

            @pl.when(jnp.logical_and(k > 0, k < nk - 1))
            def _():
                acc_ref[...] += p

            @pl.when(k == nk - 1)
            def _():
                finish(acc_ref[...] + p)

    if j_outer:
        ij = lambda g0, g1: (g1, g0)
    else:
        ij = lambda g0, g1: (g0, g1)

    ta_col = tm if ta else tk
    assert gap0 % ta_col == 0 and gapw % ta_col == 0, (a_gap, ta_col)

    def a_map(g0, g1, k):
        i, _ = ij(g0, g1)
        col = i if ta else k
        col = col + jnp.where(col >= gap0 // ta_col, gapw // ta_col, 0) if gapw else col
        return (k, col) if ta else (i, col)

    def b_map(g0, g1, k):
        _, j = ij(g0, g1)
        return (j, k) if tb else (k, j)

    def o_map(g0, g1, k):
        i, j = ij(g0, g1)
        return (o_row(i) if o_row else i, j)

    in_specs = [pl.BlockSpec((tk, tm) if ta else (tm, tk), a_map),
                pl.BlockSpec((tn, tk) if tb else (tk, tn), b_map)]
    args = [a, b]
    if has_add:
        in_specs.append(pl.BlockSpec((tm, tn), o_map))
        args.append(add)
    scratch = [pltpu.VMEM((tm, tn), F32)] if nk > 1 else []
    osz = jnp.dtype(out_dtype).itemsize
    est = (2 * (tm * tk * a.dtype.itemsize + tk * tn * b.dtype.itemsize) + 2 * tm * tn * osz
           + (2 * tm * tn * add.dtype.itemsize if has_add else 0)
           + 3 * tm * tn * 4 + 2 * (tm * tk + tk * tn)) / 2 ** 20 + 4
    out_specs = [pl.BlockSpec((tm, tn), o_map)]
    out_shape = [jax.ShapeDtypeStruct((M, N), out_dtype)]
    aliases = {}
    if side:
        hbm = pl.BlockSpec(memory_space=pltpu.HBM)
        in_specs += [hbm] * s_in
        args += list(side.inputs)
        out_specs += [hbm] * s_out
        out_shape += list(side.out_shapes)
        scratch += [pltpu.SemaphoreType.DMA((side.nsem,)), pltpu.SemaphoreType.DMA((side.nsem,))]
        aliases = {n_in + i: 1 + j for i, j in side.aliases}
    outs = pl.pallas_call(
        body, grid=grid, in_specs=in_specs, out_specs=out_specs, out_shape=out_shape, scratch_shapes=scratch,
        input_output_aliases=aliases, name=name,
        compiler_params=_params(("arbitrary", "arbitrary", "arbitrary"), est))(*args)
    return (outs[0], list(outs[1:])) if side else outs[0]


class _Side:
    def __init__(self, inputs, out_shapes, nsem, make, aliases=()):
        self.inputs, self.out_shapes, self.nsem, self.make, self.aliases = inputs, out_shapes, nsem, make, aliases


def _rms_fwd(x, w, *, name, side=None):
    T = x.shape[0]
    tm = min(512, T)

    def body(x_ref, w_ref, o_ref):
        xv = x_ref[...]
        r = lax.rsqrt(jnp.mean(xv * xv, axis=-1, keepdims=True) + EPS)
        o_ref[...] = (xv * r * w_ref[...]).astype(BF16)

    x_in, x_args, x_out, x_shapes, x_scratch = _ride_args(side)
    outs = pl.pallas_call(
        _ride(body, 2, 1, side, T // tm), grid=(T // tm,),
        in_specs=[pl.BlockSpec((tm, D), lambda i: (i, 0)), pl.BlockSpec((1, D), lambda i: (0, 0))] + x_in,
        out_specs=[pl.BlockSpec((tm, D), lambda i: (i, 0))] + x_out,
        out_shape=[jax.ShapeDtypeStruct((T, D), BF16)] + x_shapes, scratch_shapes=x_scratch, name=name,
        compiler_params=_params(("arbitrary",), 24))(x, w, *x_args)
    return (outs[0], list(outs[1:])) if side else outs[0]


def _rms_bwd(x, w, dy, dres, *, name):
    T = x.shape[0]
    tm = min(512, T)

    def body(x_ref, w_ref, dy_ref, dres_ref, dx_ref, dw_ref):
        xv = x_ref[...]
        r = lax.rsqrt(jnp.mean(xv * xv, axis=-1, keepdims=True) + EPS)
        xhat = xv * r
        dyv = dy_ref[...].astype(F32)
        g = dyv * w_ref[...]
        dx_ref[...] = dres_ref[...] + r * (g - xhat * jnp.mean(g * xhat, axis=-1, keepdims=True))
        part = jnp.sum(dyv * xhat, axis=0, keepdims=True)

        @pl.when(pl.program_id(0) == 0)
        def _():
            dw_ref[...] = part

        @pl.when(pl.program_id(0) > 0)
        def _():
            dw_ref[...] += part

    row = pl.BlockSpec((tm, D), lambda i: (i, 0))
    vec = pl.BlockSpec((1, D), lambda i: (0, 0))
    return pl.pallas_call(
        body, grid=(T // tm,), in_specs=[row, vec, row, row], out_specs=[row, vec],
        out_shape=[jax.ShapeDtypeStruct((T, D), F32), jax.ShapeDtypeStruct((1, D), F32)], name=name,
        compiler_params=_params(("arbitrary",), 32))(x, w, dy, dres)


def _loss_head(h, tgt, w):
    T = h.shape[0]
    tm = min(512, T)

    def body(h_ref, t_ref, w_ref, loss_ref, dh_ref, dw_ref):
        hv = h_ref[...]
        r = lax.rsqrt(jnp.mean(hv * hv, axis=-1, keepdims=True) + EPS)
        xhat = hv * r
        wv = w_ref[...]
        err = xhat * wv - t_ref[...]
        lpart = 0.5 * jnp.sum(jnp.mean(err * err, axis=-1, keepdims=True), axis=0, keepdims=True)
        dy = err * (1.0 / D)
        g = dy * wv
        dh_ref[...] = r * (g - xhat * jnp.mean(g * xhat, axis=-1, keepdims=True))
        wpart = jnp.sum(dy * xhat, axis=0, keepdims=True)
        lrow = jnp.broadcast_to(lpart, (1, LANE))

        @pl.when(pl.program_id(0) == 0)
        def _():
            dw_ref[...] = wpart
            loss_ref[...] = lrow

        @pl.when(pl.program_id(0) > 0)
        def _():
            dw_ref[...] += wpart
            loss_ref[...] += lrow

    row = pl.BlockSpec((tm, D), lambda i: (i, 0))
    vec = pl.BlockSpec((1, D), lambda i: (0, 0))
    return pl.pallas_call(
        body, grid=(T // tm,), in_specs=[row, row, vec],
        out_specs=[pl.BlockSpec((1, LANE), lambda i: (0, 0)), row, vec],
        out_shape=[jax.ShapeDtypeStruct((1, LANE), F32), jax.ShapeDtypeStruct((T, D), F32),
                   jax.ShapeDtypeStruct((1, D), F32)], name="loss_head",
        compiler_params=_params(("arbitrary",), 32))(h, tgt, w)


_GELU_C = 0.7978845608028654
_GELU_A = 0.044715


def _gelu(x, with_grad=False):
    x2 = x * x
    cx = _GELU_C * x
    t = jnp.tanh(cx * (1.0 + _GELU_A * x2))
    h = 0.5 * (1.0 + t)
    if not with_grad:
        return x * h
    return x * h, h + 0.5 * cx * (1.0 - t * t) * (1.0 + 3.0 * _GELU_A * x2)


def _gmlp_mask():
    r = lax.broadcasted_iota(jnp.int32, (GB, GB), 0) // GCH
    c = lax.broadcasted_iota(jnp.int32, (GB, GB), 1) // GCH
    return c <= r


def _ride(body, n_in, n_out, side, nsteps):
    if not side:
        return body
    s_in, s_out = len(side.inputs), len(side.out_shapes)

    def outer(*refs):
        ins, side_ins = refs[:n_in], refs[n_in:n_in + s_in]
        outs = refs[n_in + s_in:n_in + s_in + n_out]
        side_refs = (side_ins, refs[n_in + s_in + n_out:n_in + s_in + n_out + s_out]) + tuple(refs[-2:])

        @pl.when(pl.program_id(0) == 0)
        def _():
            for cp in side.make(*side_refs):
                cp.start()

        @pl.when(pl.program_id(0) == nsteps - 1)
        def _():
            for cp in side.make(*side_refs):
                cp.wait()

        body(*ins, *outs)

    return outer


def _ride_args(side):
    if not side:
        return [], [], [], [], []
    hbm = pl.BlockSpec(memory_space=pltpu.HBM)
    sems = [pltpu.SemaphoreType.DMA((side.nsem,)), pltpu.SemaphoreType.DMA((side.nsem,))]
    return [hbm] * len(side.inputs), list(side.inputs), [hbm] * len(side.out_shapes), list(side.out_shapes), sems


def _gmlp_fwd(proj, lnw, lnb, ws, bst, side=None):
    T = proj.shape[0]
    tm = min(512, T)
    nblk = tm // GB

    def body(u_ref, v_ref, lnw_ref, lnb_ref, ws_ref, bst_ref, o_ref):
        mask = _gmlp_mask()
        u = _gelu(u_ref[...].astype(F32))
        v = _gelu(v_ref[...].astype(F32))
        for g in range(GG):
            cs = slice(g * GD, (g + 1) * GD)
            vg = v[:, cs]
            mu = jnp.mean(vg, axis=-1, keepdims=True)
            vc = vg - mu
            var = jnp.mean(vc * vc, axis=-1, keepdims=True)
            vn = (vc * lax.rsqrt(var + EPS) * lnw_ref[g:g + 1, :] + lnb_ref[g:g + 1, :]).astype(BF16)
            wsg = jnp.where(mask, ws_ref[g], 0.0).astype(BF16)
            bcol = bst_ref[:, g:g + 1]
            for blk in range(nblk):
                rs = slice(blk * GB, (blk + 1) * GB)
                sv = _dot(wsg, vn[rs, :]) + bcol
                o_ref[rs, cs] = (u[rs, cs] * sv).astype(BF16)

    full = lambda shape: pl.BlockSpec(shape, lambda i: tuple(0 for _ in shape))
    x_in, x_args, x_out, x_shapes, x_scratch = _ride_args(side)
    outs = pl.pallas_call(
        _ride(body, 6, 1, side, T // tm), grid=(T // tm,),
        in_specs=[pl.BlockSpec((tm, GW), lambda i: (i, 2)), pl.BlockSpec((tm, GW), lambda i: (i, 3)),
                  full((GG, GD)), full((GG, GD)), full((GG, GB, GB)), full((GB, GG))] + x_in,
        out_specs=[pl.BlockSpec((tm, GW), lambda i: (i, 0))] + x_out,
        out_shape=[jax.ShapeDtypeStruct((T, GW), BF16)] + x_shapes, scratch_shapes=x_scratch,
        input_output_aliases={6 + i: 1 + j for i, j in (side.aliases if side else ())}, name="gmlp_fwd",
        compiler_params=_params(("arbitrary",), 40))(proj, proj, lnw, lnb, ws, bst, *x_args)
    return outs[0], list(outs[1:])


def _gmlp_bwd(proj, dya, dproj, lnw, lnb, ws, bst, side=None):
    T = proj.shape[0]
    tm = min(512, T)
    nblk = tm // GB
    n_in, n_out = 8, 5

    def body(u_ref, v_ref, dya_ref, dproj_in, lnw_ref, lnb_ref, ws_ref, bst_ref,
             dz_ref, dlnw_ref, dlnb_ref, dws_ref, dbst_ref):
        del dproj_in
        first = pl.program_id(0) == 0

        @pl.when(first)
        def _():
            dlnw_ref[...] = jnp.zeros_like(dlnw_ref)
            dlnb_ref[...] = jnp.zeros_like(dlnb_ref)
            dws_ref[...] = jnp.zeros_like(dws_ref)
            dbst_ref[...] = jnp.zeros_like(dbst_ref)

        mask = _gmlp_mask()
        lane = lax.broadcasted_iota(jnp.int32, (GB, LANE), 1)
        ur = u_ref[...].astype(F32)
        vr = v_ref[...].astype(F32)
        u, gu = _gelu(ur, with_grad=True)
        v, gv = _gelu(vr, with_grad=True)
        dy = dya_ref[...].astype(F32)
        dbst = jnp.zeros((GB, LANE), F32)
        dlnw_rows, dlnb_rows = [], []
        for g in range(GG):
            cs = slice(g * GD, (g + 1) * GD)
            vg = v[:, cs]
            mu = jnp.mean(vg, axis=-1, keepdims=True)
            vc = vg - mu
            var = jnp.mean(vc * vc, axis=-1, keepdims=True)
            rstd = lax.rsqrt(var + EPS)
            xhat = vc * rstd
            lw = lnw_ref[g:g + 1, :]
            vn = (xhat * lw + lnb_ref[g:g + 1, :]).astype(BF16)
            wsg = jnp.where(mask, ws_ref[g], 0.0).astype(BF16)
            bcol = bst_ref[:, g:g + 1]
            dyg = dy[:, cs]
            ug = u[:, cs]
            dsv = dyg * ug
            dsv_b = dsv.astype(BF16)
            dws_g = jnp.zeros((GB, GB), F32)
            bsum = jnp.zeros((GB, 1), F32)
            dvn_parts = []
            for blk in range(nblk):
                rs = slice(blk * GB, (blk + 1) * GB)
                sv = _dot(wsg, vn[rs, :]) + bcol
                dz_ref[rs, cs] = (dyg[rs, :] * sv * gu[rs, cs]).astype(BF16)
                dws_g = dws_g + _dot(dsv_b[rs, :], vn[rs, :], NT)
                bsum = bsum + jnp.sum(dsv[rs, :], axis=-1, keepdims=True)
                dvn_parts.append(_dot(wsg, dsv_b[rs, :], TN))
            dvn = jnp.concatenate(dvn_parts, axis=0)
            dws_ref[g] += jnp.where(mask, dws_g, 0.0)
            dbst = dbst + jnp.where(lane == g, bsum, 0.0)
            dlnw_rows.append(jnp.sum(dvn * xhat, axis=0, keepdims=True))
            dlnb_rows.append(jnp.sum(dvn, axis=0, keepdims=True))
            dxh = dvn * lw
            dvg = rstd * (dxh - jnp.mean(dxh, axis=-1, keepdims=True)
                          - xhat * jnp.mean(dxh * xhat, axis=-1, keepdims=True))
            dz_ref[:, GW + g * GD:GW + (g + 1) * GD] = (dvg * gv[:, cs]).astype(BF16)
        dlnw_ref[...] += jnp.concatenate(dlnw_rows, axis=0)
        dlnb_ref[...] += jnp.concatenate(dlnb_rows, axis=0)
        dbst_ref[...] += dbst

    full = lambda shape: pl.BlockSpec(shape, lambda i: tuple(0 for _ in shape))
    x_in, x_args, x_out, x_shapes, x_scratch = _ride_args(side)
    outs = pl.pallas_call(
        _ride(body, n_in, n_out, side, T // tm), grid=(T // tm,),
        in_specs=[pl.BlockSpec((tm, GW), lambda i: (i, 2)), pl.BlockSpec((tm, GW), lambda i: (i, 3)),
                  pl.BlockSpec((tm, GW), lambda i: (i, 0)), pl.BlockSpec(memory_space=pl.ANY),
                  full((GG, GD)), full((GG, GD)), full((GG, GB, GB)), full((GB, GG))] + x_in,
        out_specs=[pl.BlockSpec((tm, 2 * GW), lambda i: (i, 1)), full((GG, GD)), full((GG, GD)),
                   full((GG, GB, GB)), full((GB, LANE))] + x_out,
        out_shape=[jax.ShapeDtypeStruct(dproj.shape, dproj.dtype), jax.ShapeDtypeStruct((GG, GD), F32),
                   jax.ShapeDtypeStruct((GG, GD), F32), jax.ShapeDtypeStruct((GG, GB, GB), F32),
                   jax.ShapeDtypeStruct((GB, LANE), F32)] + x_shapes,
        scratch_shapes=x_scratch, input_output_aliases={3: 0}, name="gmlp_bwd",
        compiler_params=_params(("arbitrary",), 48))(proj, proj, dya, dproj, lnw, lnb, ws, bst, *x_args)
    return tuple(outs[:n_out]), list(outs[n_out:])


def _merge_fwd(ya_pre, yb_pre, proj, bias, wpa, wpb):
    T = proj.shape[0]
    tm = min(512, T)

    def body(ya_ref, yb_ref, g_ref, b_ref, wpa_ref, wpb_ref, m_ref, oa_ref, ob_ref):
        ya = _dot(ya_ref[...], wpa_ref[...])
        yb = _dot(yb_ref[...], wpb_ref[...])
        g = g_ref[...].astype(F32)
        sa = _sigmoid(g[:, :D] + b_ref[0:1, :])
        sb = _sigmoid(g[:, D:] + b_ref[1:2, :])
        m_ref[...] = (sa * ya + sb * yb).astype(BF16)
        oa_ref[...] = ya.astype(BF16)
        ob_ref[...] = yb.astype(BF16)

    row = lambda w: pl.BlockSpec((tm, w), lambda i: (i, 0))
    full = lambda shape: pl.BlockSpec(shape, lambda i: tuple(0 for _ in shape))
    o = jax.ShapeDtypeStruct((T, D), BF16)
    return pl.pallas_call(
        body, grid=(T // tm,),
        in_specs=[row(GW), row(SI), row(2 * D), full((2, D)), full((GW, D)), full((SI, D))],
        out_specs=[row(D), row(D), row(D)], out_shape=[o, o, o], name="merge_fwd",
        compiler_params=_params(("arbitrary",), 40))(ya_pre, yb_pre, proj, bias, wpa, wpb)


def _merge_bwd(dm, proj, bias, ya, yb, wpa, wpb):
    T = proj.shape[0]
    tm = min(512, T)

    def body(dm_ref, g_ref, b_ref, ya_ref, yb_ref, wpa_ref, wpb_ref,
             dg_ref, dya_ref, dyb_ref, dpa_ref, dpb_ref, db_ref):
        dmv = dm_ref[...].astype(F32)
        g = g_ref[...].astype(F32)
        sa = _sigmoid(g[:, :D] + b_ref[0:1, :])
        sb = _sigmoid(g[:, D:] + b_ref[1:2, :])
        dya = (dmv * sa).astype(BF16)
        dyb = (dmv * sb).astype(BF16)
        dga = dmv * ya_ref[...].astype(F32) * sa * (1.0 - sa)
        dgb = dmv * yb_ref[...].astype(F32) * sb * (1.0 - sb)
        dg_ref[:, :D] = dga.astype(BF16)
        dg_ref[:, D:] = dgb.astype(BF16)
        dya_ref[...] = dya
        dyb_ref[...] = dyb
        dpa_ref[...] = _dot(dya, wpa_ref[...], NT).astype(BF16)
        dpb_ref[...] = _dot(dyb, wpb_ref[...], NT).astype(BF16)
        part = jnp.concatenate([jnp.sum(dga, axis=0, keepdims=True), jnp.sum(dgb, axis=0, keepdims=True)], axis=0)

        @pl.when(pl.program_id(0) == 0)
        def _():
            db_ref[...] = part

        @pl.when(pl.program_id(0) > 0)
        def _():
            db_ref[...] += part

    row = lambda w: pl.BlockSpec((tm, w), lambda i: (i, 0))
    full = lambda shape: pl.BlockSpec(shape, lambda i: tuple(0 for _ in shape))
    o = lambda w: jax.ShapeDtypeStruct((T, w), BF16)
    return pl.pallas_call(
        body, grid=(T // tm,),
        in_specs=[row(D), row(2 * D), full((2, D)), row(D), row(D), full((GW, D)), full((SI, D))],
        out_specs=[row(2 * D), row(D), row(D), row(GW), row(SI), full((2, D))],
        out_shape=[o(DP_COLS), o(D), o(D), o(GW), o(SI), jax.ShapeDtypeStruct((2, D), F32)], name="merge_bwd",
        compiler_params=_params(("arbitrary",), 48))(dm, proj, bias, ya, yb, wpa, wpb)


RB = 128


def _shift_matrix(j):
    r = lax.broadcasted_iota(jnp.int32, (RB, RB), 0)
    c = lax.broadcasted_iota(jnp.int32, (RB, RB), 1)
    return jnp.where(c == r - j, 1.0, 0.0).astype(BF16)


def _rows_down(xb, before, shifts):
    H = SUBLANE
    mats = [_shift_matrix(j) for j in shifts]
    outs = [[] for _ in shifts]
    for b in range(xb.shape[0] // RB):
        blk = xb[b * RB:(b + 1) * RB]
        edge = jnp.concatenate([before, blk[:2 * H].astype(F32)[:H]], axis=0)
        for i, j in enumerate(shifts):
            outs[i] += [edge[H - j:2 * H - j], _dot(mats[i], blk)[H:]]
        before = blk[RB - 2 * H:].astype(F32)[H:]
    return [jnp.concatenate(o, axis=0) for o in outs]


def _rows_up(xb, after, shifts):
    H = SUBLANE
    nb = xb.shape[0] // RB
    mats = [_shift_matrix(-j) for j in shifts]
    outs = [[] for _ in shifts]
    for b in range(nb):
        blk = xb[b * RB:(b + 1) * RB]
        nxt = xb[(b + 1) * RB:(b + 1) * RB + 2 * H].astype(F32)[:H] if b + 1 < nb else after
        edge = jnp.concatenate([blk[RB - 2 * H:].astype(F32)[H:], nxt], axis=0)
        for i, j in enumerate(shifts):
            outs[i] += [_dot(mats[i], blk)[:RB - H], edge[j:H + j]]
    return [jnp.concatenate(o, axis=0) for o in outs]


def _ffn_act_fwd(up, cw, cb):
    T = up.shape[0]
    tm = min(512, T)
    H = SUBLANE

    def body(up_ref, cw_ref, cb_ref, o_ref, xc_ref, halo):
        @pl.when(pl.program_id(1) == 0)
        def _():
            halo[...] = jnp.zeros_like(halo)

        xb = up_ref[...]
        x2, x1 = _rows_down(xb, halo[...], (2, 1))
        xc = cb_ref[...] + cw_ref[0:1, :] * x2 + cw_ref[1:2, :] * x1 + cw_ref[2:3, :] * xb.astype(F32)
        xc_ref[...] = xc.astype(BF16)
        gate = xc[:, :FT]
        o_ref[...] = (gate * _sigmoid(gate) * xc[:, FT:]).astype(BF16)
        halo[...] = xb[tm - 2 * H:].astype(F32)[H:]

    tile = pl.BlockSpec((tm, 2 * FT), lambda j, i: (i, j))
    return pl.pallas_call(
        body, grid=(2, T // tm),
        in_specs=[tile, pl.BlockSpec((FK, 2 * FT), lambda j, i: (0, j)), pl.BlockSpec((1, 2 * FT), lambda j, i: (0, j))],
        out_specs=[pl.BlockSpec((tm, FT), lambda j, i: (i, j)), tile],
        out_shape=[jax.ShapeDtypeStruct((T, DFF), BF16), jax.ShapeDtypeStruct((T, 2 * DFF), BF16)],
        scratch_shapes=[pltpu.VMEM((H, 2 * FT), F32)], name="ffn_act_fwd",
        compiler_params=_params(("arbitrary", "arbitrary"), 48))(up, cw, cb)


def _ffn_act_bwd(up, xc, dact, cw):
    T = up.shape[0]
    tm = min(512, T)
    nt = T // tm
    H = SUBLANE

    def body(up_ref, xc_ref, da_ref, cw_ref, dup_ref, dcw_ref, dcb_ref, ahead):
        @pl.when(pl.program_id(1) == 0)
        def _():
            ahead[...] = jnp.zeros_like(ahead)
            dcw_ref[...] = jnp.zeros_like(dcw_ref)
            dcb_ref[...] = jnp.zeros_like(dcb_ref)

        xcv = xc_ref[...].astype(F32)
        gate, val = xcv[:, :FT], xcv[:, FT:]
        sg = _sigmoid(gate)
        dav = da_ref[...].astype(F32)
        dgate = dav * val * sg * (1.0 + gate * (1.0 - sg))
        dval = dav * gate * sg
        dxc = jnp.concatenate([dgate, dval], axis=1)
        d1, d2 = _rows_up(dxc.astype(BF16), ahead[...], (1, 2))
        x = up_ref[...].astype(F32)
        dcb_ref[...] += jnp.sum(dxc, axis=0, keepdims=True)
        dcw_ref[...] += jnp.concatenate([jnp.sum(d * x, axis=0, keepdims=True) for d in (d2, d1, dxc)], axis=0)
        dup_ref[...] = (cw_ref[2:3, :] * dxc + cw_ref[1:2, :] * d1 + cw_ref[0:1, :] * d2).astype(BF16)
        ahead[...] = dxc[0:H, :]

    tile = pl.BlockSpec((tm, 2 * FT), lambda j, i: (nt - 1 - i, j))
    return pl.pallas_call(
        body, grid=(2, nt),
        in_specs=[tile, tile, pl.BlockSpec((tm, FT), lambda j, i: (nt - 1 - i, j)),
                  pl.BlockSpec((FK, 2 * FT), lambda j, i: (0, j))],
        out_specs=[tile, pl.BlockSpec((FK, 2 * FT), lambda j, i: (0, j)), pl.BlockSpec((1, 2 * FT), lambda j, i: (0, j))],
        out_shape=[jax.ShapeDtypeStruct((T, 2 * DFF), BF16), jax.ShapeDtypeStruct((FK, 2 * DFF), F32),
                   jax.ShapeDtypeStruct((1, 2 * DFF), F32)],
        scratch_shapes=[pltpu.VMEM((H, 2 * FT), F32)], name="ffn_act_bwd",
        compiler_params=_params(("arbitrary", "arbitrary"), 56))(up, xc, dact, cw)


def _softplus(x):
    e = jnp.exp(-jnp.abs(x))
    return jnp.maximum(x, 0.0) + jnp.where(e < 1e-4, e * (1.0 - 0.5 * e), jnp.log(1.0 + e))


def _ssd_tril():
    li = lax.broadcasted_iota(jnp.int32, (LS, LS), 0)
    si = lax.broadcasted_iota(jnp.int32, (LS, LS), 1)
    return si <= li


def _head_expansion():
    hh = lax.broadcasted_iota(jnp.int32, (LANE, SI), 0)
    cc = lax.broadcasted_iota(jnp.int32, (LANE, SI), 1) // SP
    return jnp.where(hh == cc, 1.0, 0.0).astype(BF16)


def _ssd_pre(xc, dt_ref, dtb_ref, alog_ref, tril, expand):
    sx = _sigmoid(xc)
    xbc = xc * sx
    xs, bm, cm = xbc[:, :SI], xbc[:, SI:SI + SG * SN], xbc[:, SI + SG * SN:]
    dtin = dt_ref[...] + dtb_ref[...]
    dt = _softplus(dtin)
    a_neg = -jnp.exp(alog_ref[...])
    dta = dt * a_neg
    trilb = jnp.where(tril, 1.0, 0.0).astype(BF16)
    a = _dot3_rhs(trilb, dta, NN)
    a_exp = _dot3(a, expand, NN)
    dt_exp = _dot2(dt, expand, NN)
    xdt = xs * dt_exp
    a_last = a_exp[LS - 1:LS, :]
    return dict(xc=xc, sx=sx, xs=xs, bm=bm, cm=cm, dtin=dt_ref[...] + dtb_ref[...], dt=dt, a_neg=a_neg,
                a=a, a_t=a.T, a_exp=a_exp, dt_exp=dt_exp, xdt=xdt, ea=jnp.exp(a_exp),
                w=jnp.exp(a_last - a_exp), eal=jnp.exp(a_last))


def _head_decay(pre, tril, h):
    seg = pre["a"][:, h:h + 1] - pre["a_t"][h:h + 1, :]
    return jnp.exp(jnp.where(tril, seg, -1e30))


def _ssd_fwd(proj, dtraw, cw, cb, dtb, alog, dexp, nw):
    T = proj.shape[0]
    nc = T // LS
    H = SUBLANE

    def body(z_ref, x_ref, dt_ref, cw_ref, cb_ref, dtb_ref, alog_ref, dexp_ref, nw_ref, ex_ref,
             yb_ref, y_ref, sp_ref, xc_ref, halo, st):
        @pl.when(pl.program_id(0) == 0)
        def _():
            halo[...] = jnp.zeros_like(halo)
            st[...] = jnp.zeros_like(st)

        xb = x_ref[...]
        taps = _rows_down(xb, halo[...], (3, 2, 1)) + [xb.astype(F32)]
        xc = cb_ref[...]
        for k in range(SK):
            xc = xc + cw_ref[k:k + 1, :] * taps[k]
        xc_ref[...] = xc.astype(BF16)
        tril, expand = _ssd_tril(), ex_ref[...]
        pre = _ssd_pre(xc, dt_ref, dtb_ref, alog_ref, tril, expand)
        lane = lax.broadcasted_iota(jnp.int32, (LS, LANE), 1)
        lo = lane < SP
        zf = z_ref[...].astype(F32)
        siluz = zf * _sigmoid(zf)
        for g in range(SG):
            gs = slice(g * SGW, (g + 1) * SGW)
            bg = pre["bm"][:, g * SN:(g + 1) * SN].astype(BF16)
            cg = pre["cm"][:, g * SN:(g + 1) * SN].astype(BF16)
            gmat = _dot(cg, bg, NT)
            sg = st[g]
            sp_ref[0, g] = sg
            yoff = _dot(cg, sg.astype(BF16))
            parts = []
            for j in range(SGW // LANE):
                h0 = g * (SGW // SP) + 2 * j
                m0 = gmat * _head_decay(pre, tril, h0)
                m1 = gmat * _head_decay(pre, tril, h0 + 1)
                xp = pre["xdt"][:, g * SGW + j * LANE:g * SGW + (j + 1) * LANE]
                rhs = jnp.concatenate([jnp.where(lo, xp, 0.0), jnp.where(lo, 0.0, xp)], axis=0).astype(BF16)
                parts.append(_dot(jnp.concatenate([m0, m1], axis=1).astype(BF16), rhs))
            y = (jnp.concatenate(parts, axis=1) + pre["ea"][:, gs] * yoff + dexp_ref[:, gs] * pre["xs"][:, gs])
            st[g] = pre["eal"][:, gs] * sg + _dot(bg, (pre["w"][:, gs] * pre["xdt"][:, gs]).astype(BF16), TN)
            y_ref[:, gs] = y
            yg = y * siluz[:, gs]
            r = lax.rsqrt(jnp.mean(yg * yg, axis=-1, keepdims=True) + EPS)
            yb_ref[:, gs] = (yg * r * nw_ref[:, gs]).astype(BF16)
        halo[...] = xb[LS - 2 * H:].astype(F32)[H:]

    vec = lambda w: pl.BlockSpec((1, w), lambda c: (0, 0))
    return pl.pallas_call(
        body, grid=(nc,),
        in_specs=[pl.BlockSpec((LS, SI), lambda c: (c, 2)), pl.BlockSpec((LS, SXBC), lambda c: (c, 2)),
                  pl.BlockSpec((LS, LANE), lambda c: (c, 0)),
                  pl.BlockSpec((SK, SXBC), lambda c: (0, 0)), vec(SXBC), vec(LANE), vec(LANE), vec(SI), vec(SI),
                  pl.BlockSpec((LANE, SI), lambda c: (0, 0))],
        out_specs=[pl.BlockSpec((LS, SI), lambda c: (c, 0)), pl.BlockSpec((LS, SI), lambda c: (c, 0)),
                   pl.BlockSpec((1, SG, SN, SGW), lambda c: (c, 0, 0, 0)), pl.BlockSpec((LS, SXBC), lambda c: (c, 0))],
        out_shape=[jax.ShapeDtypeStruct((T, SI), BF16), jax.ShapeDtypeStruct((T, SI), F32),
                   jax.ShapeDtypeStruct((nc, SG, SN, SGW), F32), jax.ShapeDtypeStruct((T, SXBC), BF16)],
        scratch_shapes=[pltpu.VMEM((H, SXBC), F32), pltpu.VMEM((SG, SN, SGW), F32)], name="ssd_fwd",
        compiler_params=_params(("arbitrary",), VMEM_CAP_MB))(
            proj, proj, dtraw, cw, cb, dtb, alog, dexp, nw, _head_expansion())


def _ssd_bwd(proj, xcs, dtraw, y, sprev, dyb, dproj, cw, dtb, alog, dexp, nw):
    T = proj.shape[0]
    nc = T // LS
    H = SUBLANE
    NJ = 1

    def body(z_ref, x_ref, xc_ref, dt_ref, y_ref, sp_ref, dyb_ref, dproj_in,
             cw_ref, dtb_ref, alog_ref, dexp_ref, nw_ref, ex_ref,
             dp_ref, ddt_ref, dcw_ref, dcb_ref, ddtb_ref, da_ref, dd_ref, dnw_ref,
             ahead, ds, stage):
        del dproj_in
        i = pl.program_id(0)
        j = pl.program_id(1)

        @pl.when(jnp.logical_and(i == 0, j == 0))
        def _():
            ahead[...] = jnp.zeros_like(ahead)
            ds[...] = jnp.zeros_like(ds)
            for r in (dcw_ref, dcb_ref, ddtb_ref, da_ref, dd_ref, dnw_ref):
                r[...] = jnp.zeros_like(r)

        @pl.when(j == 0)
        def _():
            tril, expand = _ssd_tril(), ex_ref[...]
            pre = _ssd_pre(xc_ref[...].astype(F32), dt_ref, dtb_ref, alog_ref, tril, expand)
            lane = lax.broadcasted_iota(jnp.int32, (LS, LANE), 1)
            sub = lax.broadcasted_iota(jnp.int32, (LANE, LS), 0)
            rowi = lax.broadcasted_iota(jnp.int32, (LS, 1), 0)
            lo = lane < SP
            xs, xdt, ea, w, eal = pre["xs"], pre["xdt"], pre["ea"], pre["w"], pre["eal"]

            zf = z_ref[...].astype(F32)
            sz = _sigmoid(zf)
            siluz = zf * sz
            yv = y_ref[...]
            yg = yv * siluz
            dout = dyb_ref[...].astype(F32)
            dyg_parts, dnw_parts = [], []
            for g in range(SG):
                gs = slice(g * SGW, (g + 1) * SGW)
                ygg = yg[:, gs]
                r = lax.rsqrt(jnp.mean(ygg * ygg, axis=-1, keepdims=True) + EPS)
                yhat = ygg * r
                dn = dout[:, gs] * nw_ref[:, gs]
                dnw_parts.append(jnp.sum(dout[:, gs] * yhat, axis=0, keepdims=True))
                dyg_parts.append(r * (dn - yhat * jnp.mean(dn * yhat, axis=-1, keepdims=True)))
            dyg = jnp.concatenate(dyg_parts, axis=1)
            dnw_ref[...] += jnp.concatenate(dnw_parts, axis=1)
            dy = dyg * siluz
            stage[:, 0:SI] = (dyg * yv * sz * (1.0 + zf * (1.0 - sz))).astype(BF16)
            dd_ref[...] += jnp.sum(dy * xs, axis=0, keepdims=True)
            tt = ea * dy

            da_rows = jnp.zeros((LS, LANE), F32)
            da_cols = jnp.zeros((LANE, LS), F32)
            dxdt_parts, db_parts, dc_parts, daexp_parts = [], [], [], []
            for g in range(SG):
                gs = slice(g * SGW, (g + 1) * SGW)
                bg = pre["bm"][:, g * SN:(g + 1) * SN].astype(BF16)
                cg = pre["cm"][:, g * SN:(g + 1) * SN].astype(BF16)
                sg = sp_ref[0, g]
                sgb = sg.astype(BF16)
                dsg = ds[g]
                dsgb = dsg.astype(BF16)
                ttg = tt[:, gs].astype(BF16)
                yoff = _dot(cg, sgb)
                dc = _dot(ttg, sgb, NT)
                gmat = _dot(cg, bg, NT)
                dgm = jnp.zeros((LS, LS), F32)
                dxdt_pairs = []
                for jj in range(SGW // LANE):
                    h0 = g * (SGW // SP) + 2 * jj
                    ps = slice(g * SGW + jj * LANE, g * SGW + (jj + 1) * LANE)
                    l0 = _head_decay(pre, tril, h0)
                    l1 = _head_decay(pre, tril, h0 + 1)
                    m0 = gmat * l0
                    m1 = gmat * l1
                    dyp = dy[:, ps]
                    dy_lo = jnp.where(lo, dyp, 0.0).astype(BF16)
                    dy_hi = jnp.where(lo, 0.0, dyp).astype(BF16)
                    xpb = xdt[:, ps].astype(BF16)
                    dm0 = _dot(dy_lo, xpb, NT)
                    dm1 = _dot(dy_hi, xpb, NT)
                    q0 = dm0 * m0
                    q1 = dm1 * m1
                    da_rows = da_rows + jnp.where(lane == h0, jnp.sum(q0, axis=1, keepdims=True), 0.0)
                    da_rows = da_rows + jnp.where(lane == h0 + 1, jnp.sum(q1, axis=1, keepdims=True), 0.0)
                    da_cols = da_cols + jnp.where(sub == h0, jnp.sum(q0, axis=0, keepdims=True), 0.0)
                    da_cols = da_cols + jnp.where(sub == h0 + 1, jnp.sum(q1, axis=0, keepdims=True), 0.0)
                    dgm = dgm + dm0 * l0 + dm1 * l1
                    mcat = jnp.concatenate([m0, m1], axis=0).astype(BF16)
                    dycat = jnp.concatenate([dy_lo, dy_hi], axis=0)
                    dxdt_pairs.append(_dot(mcat, dycat, TN))
                dgb = dgm.astype(BF16)
                dc = dc + _dot(dgb, bg)
                db = _dot(dgb, cg, TN)
                zg = _dot(bg, dsgb)
                wg, xdtg = w[:, gs], xdt[:, gs]
                dxdt_g = jnp.concatenate(dxdt_pairs, axis=1) + wg * zg
                qg = zg * xdtg * wg
                last = (jnp.sum(qg, axis=0, keepdims=True)
                        + jnp.sum(dsg * sg, axis=0, keepdims=True) * eal[:, gs])
                daexp_parts.append(dy[:, gs] * ea[:, gs] * yoff - qg + jnp.where(rowi == LS - 1, last, 0.0))
                db = db + _dot((wg * xdtg).astype(BF16), dsgb, NT)
                ds[g] = eal[:, gs] * dsg + _dot(cg, ttg, TN)
                dxdt_parts.append(dxdt_g)
                db_parts.append(db)
                dc_parts.append(dc)
            dxdt = jnp.concatenate(dxdt_parts, axis=1)
            da_exp = jnp.concatenate(daexp_parts, axis=1)
            da = _dot2(da_exp, expand, NT) + da_rows - da_cols.T
            triub = jnp.where(tril, 1.0, 0.0).astype(BF16)
            ddta = _dot3_rhs(triub, da, TN)
            ddt = ddta * pre["a_neg"] + _dot2(dxdt * xs, expand, NT)
            da_ref[...] += jnp.sum(ddta * pre["dt"], axis=0, keepdims=True)
            ddt_raw = ddt * _sigmoid(pre["dtin"])
            ddt_ref[...] = ddt_raw
            ddtb_ref[...] += jnp.sum(ddt_raw, axis=0, keepdims=True)
            dxs = dexp_ref[...] * dy + dxdt * pre["dt_exp"]
            dxbc = jnp.concatenate([dxs] + db_parts + dc_parts, axis=1)
            sx, xc = pre["sx"], pre["xc"]
            dxc = dxbc * sx * (1.0 + xc * (1.0 - sx))
            taps = _rows_up(dxc.astype(BF16), ahead[...], (3, 2, 1)) + [dxc]
            xr = x_ref[...].astype(F32)
            dcb_ref[...] += jnp.sum(dxc, axis=0, keepdims=True)
            dcw_ref[...] += jnp.concatenate([jnp.sum(t * xr, axis=0, keepdims=True) for t in taps], axis=0)
            dxr = cw_ref[0:1, :] * taps[0]
            for k in range(1, SK):
                dxr = dxr + cw_ref[k:k + 1, :] * taps[k]
            stage[:, SI:] = dxr.astype(BF16)
            ahead[...] = dxc[0:H, :]

        dp_ref[...] = stage[...]

    vec = lambda w: pl.BlockSpec((1, w), lambda i, j: (0, 0))
    rev = lambda w, cb_: pl.BlockSpec((LS, w), lambda i, j: (nc - 1 - i, cb_))
    outs = pl.pallas_call(
        body, grid=(nc, NJ),
        in_specs=[rev(SI, 2), rev(SXBC, 2), rev(SXBC, 0),
                  rev(LANE, 0), rev(SI, 0),
                  pl.BlockSpec((1, SG, SN, SGW), lambda i, j: (nc - 1 - i, 0, 0, 0)),
                  rev(SI, 0), pl.BlockSpec(memory_space=pl.ANY),
                  pl.BlockSpec((SK, SXBC), lambda i, j: (0, 0)), vec(LANE), vec(LANE), vec(SI), vec(SI),
                  pl.BlockSpec((LANE, SI), lambda i, j: (0, 0))],
        out_specs=[rev(DP_SSM, 1), rev(LANE, 0),
                   pl.BlockSpec((SK, SXBC), lambda i, j: (0, 0)), vec(SXBC), vec(LANE), vec(LANE), vec(SI), vec(SI)],
        out_shape=[jax.ShapeDtypeStruct(dproj.shape, dproj.dtype), jax.ShapeDtypeStruct((T, LANE), F32),
                   jax.ShapeDtypeStruct((SK, SXBC), F32), jax.ShapeDtypeStruct((1, SXBC), F32),
                   jax.ShapeDtypeStruct((1, LANE), F32), jax.ShapeDtypeStruct((1, LANE), F32),
                   jax.ShapeDtypeStruct((1, SI), F32), jax.ShapeDtypeStruct((1, SI), F32)],
        scratch_shapes=[pltpu.VMEM((H, SXBC), F32),
                        pltpu.VMEM((SG, SN, SGW), F32), pltpu.VMEM((LS, SI + SXBC), BF16)],
        input_output_aliases={7: 0}, name="ssd_bwd",
        compiler_params=_params(("arbitrary", "arbitrary"), VMEM_CAP_MB))(
            proj, proj, xcs, dtraw, y, sprev, dyb, dproj, cw, dtb, alog, dexp, nw, _head_expansion())
    return outs


def _perm_ffn_cols(a):
    lead = a.shape[:-1]
    return a.reshape(lead + (2, 2, FT)).swapaxes(-3, -2).reshape(lead + (2 * DFF,))


def _perm_ffn_rows(a):
    return a.reshape((2, 2, FT) + a.shape[1:]).swapaxes(0, 1).reshape(a.shape)


def _pad_lanes(v, n=LANE):
    return jnp.pad(v, ((0, 0), (0, n - v.shape[-1])))


LATE = ["w_proj_a", "w_proj_b", "w_out", "ffn_w_up_t", "ffn_w_down"]
WGRAD = BF16
SMALL_BF16_FROM = 2 ** 16


class _NoExchange:
    def early_start(self):
        return None

    def early_weights(self, w, outs):
        return w

    def gather_start(self):
        return None

    def gather_pass_on(self, outs):
        return None

    def late_weights(self, w, outs):
        return w

    def late_to_sibling(self, grads):
        return None

    def reduce_late(self, outs):
        return None

    def w_in_to_sibling(self, grad_main, grad_dt):
        return None

    def reduce_w_in(self, outs):
        return None

    def reduced(self, late_outs, w_in_outs):
        pass


def _local_step(x, tgt, w, hooks=None):
    hooks = hooks or _NoExchange()

    def mm(*args, side=None, **kw):
        out = _matmul(*args, side=side, **kw)
        return out if side is not None else (out, [])

    mixw = w["mix_norm_w"][None, :]
    side = hooks.early_start()
    xn, got = _rms_fwd(x, mixw, name="mix_norm", side=side) if side else (_rms_fwd(x, mixw, name="mix_norm"), [])
    w = hooks.early_weights(w, got)
    win_t = w["w_in_t"]
    win_dt = jnp.pad(w["w_in_t"][PMAIN:], ((0, LANE - SH), (0, 0)))
    fcw = _perm_ffn_cols(w["ffn_conv_w"])
    fcb = _perm_ffn_cols(w["ffn_conv_b"][None, :])
    ffnw = w["ffn_norm_w"][None, :]
    finw = w["final_norm_w"][None, :]
    bst = w["gmlp_bs"].T
    scb = w["ssm_conv_b"][None, :]
    dtb = _pad_lanes(w["ssm_dt_bias"][None, :])
    alog = _pad_lanes(w["ssm_a_log"][None, :])
    dexp = jnp.repeat(w["ssm_d"], SP)[None, :]
    snw = w["ssm_norm_w"][None, :]

    proj, got = mm(xn, win_t, name="in_proj", out_dtype=BF16, tb=True, tn=3072, j_outer=True, b_rows=PMAIN,
                   side=hooks.gather_start())
    dtraw = _matmul(xn, win_dt, name="in_proj_dt", out_dtype=F32, tb=True)
    ya_pre, got = _gmlp_fwd(proj, w["gmlp_ln_w"], w["gmlp_ln_b"], w["gmlp_ws"], bst, side=hooks.gather_pass_on(got))
    w = hooks.late_weights(w, got)
    wup = _perm_ffn_rows(w["ffn_w_up_t"])
    yb_pre, y_ssd, sprev, ssm_xc = _ssd_fwd(proj, dtraw, w["ssm_conv_w"], scb, dtb, alog, dexp, snw)
    merged, ya, yb = _merge_fwd(ya_pre, yb_pre, proj, w["gate_bias"], w["w_proj_a"], w["w_proj_b"])
    h1 = _matmul(merged, w["w_out"], name="out_proj", out_dtype=F32, add=x)
    hn = _rms_fwd(h1, ffnw, name="ffn_norm")
    up = _matmul(hn, wup, name="ffn_up", out_dtype=BF16, tb=True, tn=2 * FT, j_outer=True)
    act, ffn_xc = _ffn_act_fwd(up, fcw, fcb)
    h2 = _matmul(act, w["ffn_w_down"], name="ffn_down", out_dtype=F32, tk=DFF, add=h1)

    loss_row, dh2, d_finw = _loss_head(h2, tgt, finw)
    dact = _matmul(dh2, w["ffn_w_down"], name="ffn_down_dx", out_dtype=BF16, tb=True, tn=DFF)
    d_wdown = _matmul(act, dh2, name="ffn_down_dw", out_dtype=WGRAD, ta=True, tm=FT, tk=2048)
    dup, d_fcw, d_fcb = _ffn_act_bwd(up, ffn_xc, dact, fcw)
    dhn = _matmul(dup, wup, name="ffn_up_dx", out_dtype=F32, tk=2 * FT)
    d_wup = _matmul(dup, hn, name="ffn_up_dw", out_dtype=WGRAD, ta=True, tm=FT, tk=2048,
                    o_row=lambda i: (i % 2) * 2 + i // 2)
    dh1, d_ffnw = _rms_bwd(h1, ffnw, dhn, dh2, name="ffn_norm_bwd")
    dmerged = _matmul(dh1, w["w_out"], name="out_proj_dx", out_dtype=BF16, tb=True)
    d_wout = _matmul(merged, dh1, name="out_proj_dw", out_dtype=WGRAD, ta=True, tk=2048)
    dproj, dya, dyb, dya_pre, dyb_pre, d_gbias = _merge_bwd(dmerged, proj, w["gate_bias"], ya, yb,
                                                           w["w_proj_a"], w["w_proj_b"])
    d_wpa = _matmul(ya_pre, dya, name="proj_a_dw", out_dtype=WGRAD, ta=True, tk=4096)
    d_wpb = _matmul(yb_pre, dyb, name="proj_b_dw", out_dtype=WGRAD, ta=True, tk=4096)
    late = {"w_proj_a": d_wpa, "w_proj_b": d_wpb, "w_out": d_wout, "ffn_w_up_t": d_wup,
            "ffn_w_down": d_wdown}
    (dproj, d_lnw, d_lnb, d_ws, d_bst), got = _gmlp_bwd(proj, dya_pre, dproj, w["gmlp_ln_w"], w["gmlp_ln_b"],
                                                        w["gmlp_ws"], bst, side=hooks.late_to_sibling(late))
    dproj, ddt, d_scw, d_scb, d_dtb, d_a, d_dch, d_snw = _ssd_bwd(
        proj, ssm_xc, dtraw, y_ssd, sprev, dyb_pre, dproj, w["ssm_conv_w"], dtb, alog, dexp, snw)
    gap = (2 * D + 2 * GW, DP_GAP)
    d_win_main, late_outs = mm(dproj, xn, name="in_proj_dw", out_dtype=WGRAD, ta=True, a_gap=gap, tk=4096,
                               side=hooks.reduce_late(got))
    d_win_dt = _matmul(ddt, xn, name="in_proj_dt_dw", out_dtype=F32, ta=True)
    d_win_t = jnp.concatenate([d_win_main, d_win_dt[:SH]], axis=0)
    dxn, got = mm(ddt, win_dt, name="in_proj_dt_dx", out_dtype=F32,
                  side=hooks.w_in_to_sibling(d_win_main, d_win_dt[:SH]))
    dxn, w_in_outs = mm(dproj, win_t, name="in_proj_dx", out_dtype=F32, add=dxn, b_rows=PMAIN, a_gap=gap,
                        side=hooks.reduce_w_in(got))
    hooks.reduced(late_outs, w_in_outs)
    grad_x, d_mixw = _rms_bwd(x, mixw, dxn, dh1, name="mix_norm_bwd")

    a_neg = -jnp.exp(w["ssm_a_log"])
    grads = {
        "mix_norm_w": d_mixw[0],
        "w_in_t": d_win_t,
        "gate_bias": d_gbias,
        "gmlp_ln_w": d_lnw, "gmlp_ln_b": d_lnb, "gmlp_ws": d_ws, "gmlp_bs": d_bst[:, :GG].T,
        "ssm_conv_w": d_scw, "ssm_conv_b": d_scb[0],
        "ssm_dt_bias": d_dtb[0, :SH], "ssm_a_log": d_a[0, :SH] * a_neg,
        "ssm_d": d_dch.reshape(SH, SP).sum(axis=-1), "ssm_norm_w": d_snw[0],
        **late,
        "ffn_norm_w": d_ffnw[0],
        "ffn_conv_w": _perm_ffn_cols(d_fcw), "ffn_conv_b": _perm_ffn_cols(d_fcb)[0],
        "ffn_w_down": d_wdown, "final_norm_w": d_finw[0],
    }
    return loss_row, grad_x, grads


MESH = pl.DeviceIdType.MESH
HBM_SPEC = pl.BlockSpec(memory_space=pltpu.HBM)


def _axes():
    return lax.axis_index("x"), lax.axis_index("y"), lax.axis_index("c")


def _run_side(side, *, name):
    n_in, n_out = len(side.inputs), len(side.out_shapes)

    def body(*refs):
        copies = side.make(refs[:n_in], refs[n_in:n_in + n_out], *refs[-2:])
        for cp in copies:
            cp.start()
        for cp in copies:
            cp.wait()

    return pl.pallas_call(
        body, out_shape=list(side.out_shapes), in_specs=[HBM_SPEC] * n_in, out_specs=[HBM_SPEC] * n_out,
        scratch_shapes=[pltpu.SemaphoreType.DMA((side.nsem,)), pltpu.SemaphoreType.DMA((side.nsem,))],
        input_output_aliases={i: j for i, j in side.aliases}, name=name)(*side.inputs)


def _exchange(srcs, plan, *, name):
    na = len(srcs)
    n = len(plan(0, 0, 0))

    def body(*refs):
        src_refs, out_refs = refs[:na], refs[na:2 * na]
        send_sems, recv_sems = refs[2 * na:]
        x, y, c = _axes()
        copies = []
        for k, (slab, peer) in enumerate(plan(x, y, c)):
            for a in range(na):
                cp = pltpu.make_async_remote_copy(
                    src_ref=src_refs[a].at[slab], dst_ref=out_refs[a].at[k], send_sem=send_sems.at[n * a + k],
                    recv_sem=recv_sems.at[n * a + k], device_id=peer, device_id_type=MESH)
                cp.start()
                copies.append(cp)
        for cp in copies:
            cp.wait()

    return pl.pallas_call(
        body, out_shape=[jax.ShapeDtypeStruct((n,) + s.shape[1:], s.dtype) for s in srcs],
        in_specs=[HBM_SPEC] * na, out_specs=[HBM_SPEC] * na,
        scratch_shapes=[pltpu.SemaphoreType.DMA((n * na,)), pltpu.SemaphoreType.DMA((n * na,))], name=name)(*srcs)


def _to_sibling_plan(x, y, c):
    return [(2 * q + (1 - c), (x, y, 1 - c)) for q in range(4)]


def _to_chips_plan(x, y, c):
    q = 2 * x + y
    return [(q ^ 2, (1 - x, y, c)), (q ^ 1, (x, 1 - y, c)), (q ^ 3, (1 - x, 1 - y, c))]


def _row_tile(rows, row_bytes, budget=2 * 2 ** 20, align=2 * SUBLANE):
    if rows * row_bytes <= 2 * budget:
        return rows
    best = None
    for d in range(align, rows + 1, align):
        if rows % d == 0 and d * row_bytes <= budget:
            best = d
    return best or rows


def _pair_add(g, ra, c_idx, *, name):
    _, _, R, C = g.shape
    tr = _row_tile(R, C * 4, budget=3 * 2 ** 20)

    def body(c_ref, g_ref, ra_ref, o_ref):
        del c_ref
        o_ref[...] = (g_ref[0].astype(F32) + ra_ref[...].astype(F32)).astype(o_ref.dtype)

    return pl.pallas_call(
        body,
        grid_spec=pltpu.PrefetchScalarGridSpec(
            num_scalar_prefetch=1, grid=(4, R // tr),
            in_specs=[pl.BlockSpec((1, 1, tr, C), lambda q, r, cr: (q, cr[0], r, 0)),
                      pl.BlockSpec((1, tr, C), lambda q, r, cr: (q, r, 0))],
            out_specs=pl.BlockSpec((1, tr, C), lambda q, r, cr: (q, r, 0))),
        out_shape=jax.ShapeDtypeStruct((4, R, C), g.dtype), name=name,
        compiler_params=_params(("arbitrary", "arbitrary"), 24))(c_idx, g, ra)


def _grad_sum(p, rb, q_idx, *, name):
    _, R, C = p.shape
    tr = _row_tile(R, C * 4, budget=3 * 2 ** 20)

    def body(q_ref, p_ref, rb_ref, o_ref):
        del q_ref
        g = p_ref[0].astype(F32)
        for k in range(3):
            g = g + rb_ref[k].astype(F32)
        o_ref[...] = g

    return pl.pallas_call(
        body,
        grid_spec=pltpu.PrefetchScalarGridSpec(
            num_scalar_prefetch=1, grid=(R // tr,),
            in_specs=[pl.BlockSpec((1, tr, C), lambda r, qr: (qr[0], r, 0)),
                      pl.BlockSpec((3, tr, C), lambda r, qr: (0, r, 0))],
            out_specs=pl.BlockSpec((tr, C), lambda r, qr: (r, 0))),
        out_shape=jax.ShapeDtypeStruct((R, C), F32), name=name,
        compiler_params=_params(("arbitrary",), 40))(q_idx, p, rb)


def _adamw(g, w, m, v):
    m = ADAM_B1 * m + (1.0 - ADAM_B1) * g
    v = ADAM_B2 * v + (1.0 - ADAM_B2) * (g * g)
    m_hat = m / (1.0 - ADAM_B1 ** ADAM_STEP)
    v_hat = v / (1.0 - ADAM_B2 ** ADAM_STEP)
    delta = -ADAM_LR * (m_hat / (jnp.sqrt(v_hat) + ADAM_EPS) + ADAM_WD * w)
    return delta, m, v


def _adam(g, w, m, v, *, name):
    _, R, C = w.shape
    tr = _row_tile(R, C * 4, budget=2 ** 20, align=SUBLANE)

    def body(g_ref, w_ref, m_ref, v_ref, d_out, m_out, v_out):
        delta, mn, vn = _adamw(g_ref[...], w_ref[...], m_ref[...], v_ref[...])
        d_out[...] = delta
        m_out[...] = mn
        v_out[...] = vn

    row = pl.BlockSpec((1, tr, C), lambda r: (0, r, 0))
    o = jax.ShapeDtypeStruct((1, R, C), F32)
    return pl.pallas_call(
        body, grid=(R // tr,), in_specs=[row, row, row, row], out_specs=[row, row, row], out_shape=[o, o, o],
        name=name, compiler_params=_params(("arbitrary",), 32))(g, w, m, v)


def _vmem_specs(n):
    return [pl.BlockSpec(memory_space=pltpu.VMEM)] * n


def _pair_sum_many(mine, theirs, *, name):
    n = len(mine)

    def body(*refs):
        for a in range(n):
            refs[2 * n + a][...] = refs[a][...] + refs[n + a][0]

    return pl.pallas_call(
        body, out_shape=[jax.ShapeDtypeStruct(m.shape, m.dtype) for m in mine], in_specs=_vmem_specs(2 * n),
        out_specs=_vmem_specs(n), name=name)(*mine, *theirs)


def _chip_sum_many(own, recv, q_idx, *, name):
    n = len(own)

    def body(q_ref, *refs):
        q = q_ref[0]
        for a in range(n):
            mine, r = refs[a][...], refs[n + a]
            total = None
            for chip in range(4):
                e = q ^ chip
                term = jnp.where(e == 0, mine, jnp.where(e == 2, r[0], jnp.where(e == 1, r[1], r[2]))).astype(F32)
                total = term if total is None else total + term
            refs[2 * n + a][...] = total

    return pl.pallas_call(
        body, out_shape=[jax.ShapeDtypeStruct(m.shape, F32) for m in own],
        in_specs=[pl.BlockSpec(memory_space=pltpu.SMEM)] + _vmem_specs(2 * n), out_specs=_vmem_specs(n),
        name=name)(q_idx, *own, *recv)


def _adam_many(gs, ws, ms, vs, *, name):
    n = len(gs)

    def body(*refs):
        for a in range(n):
            delta, mn, vn = _adamw(*(refs[k * n + a][...] for k in range(4)))
            refs[4 * n + a][...] = delta
            refs[5 * n + a][...] = mn
            refs[6 * n + a][...] = vn

    shapes = [jax.ShapeDtypeStruct(w.shape, w.dtype) for w in ws]
    out = pl.pallas_call(body, out_shape=shapes * 3, in_specs=_vmem_specs(4 * n), out_specs=_vmem_specs(3 * n),
                         name=name)(*gs, *ws, *ms, *vs)
    return out[:n], out[n:2 * n], out[2 * n:]


WEIGHTS = ["mix_norm_w", "w_in", "gate_bias", "gmlp_ln_w", "gmlp_ln_b", "gmlp_ws", "gmlp_bs", "ssm_conv_w",
           "ssm_conv_b", "ssm_dt_bias", "ssm_a_log", "ssm_d", "ssm_norm_w", "w_proj_a", "w_proj_b", "w_out",
           "ffn_norm_w", "ffn_w_up", "ffn_conv_w", "ffn_conv_b", "ffn_w_down", "final_norm_w"]
SHARDED = {"w_in": ((D, IN_COLS), 1), "gate_bias": ((2, D), 1), "ssm_conv_w": ((SK, SXBC), 1),
           "w_proj_a": ((GW, D), 0), "w_proj_b": ((SI, D), 0), "w_out": ((D, D), 0),
           "ffn_w_up": ((D, 2 * DFF), 1), "ffn_conv_w": ((FK, 2 * DFF), 1), "ffn_w_down": ((DFF, D), 0)}
REPLICATED = {"mix_norm_w": (D,), "gmlp_ln_w": (GG, GD), "gmlp_ln_b": (GG, GD), "gmlp_ws": (GG, GB, GB),
              "gmlp_bs": (GG, GB), "ssm_conv_b": (SXBC,), "ssm_dt_bias": (SH,), "ssm_a_log": (SH,), "ssm_d": (SH,),
              "ssm_norm_w": (SI,), "ffn_norm_w": (D,), "ffn_conv_b": (2 * DFF,), "final_norm_w": (D,)}
REPL_ORDER = [n for n in WEIGHTS if n in REPLICATED]
BTILE = 2 * SUBLANE
WIN_R = IN_COLS // NDEV
WIN_P = WIN_R + BTILE - WIN_R % BTILE
WIN_A = [WIN_R * d // BTILE * BTILE for d in range(NDEV)]
assert all(WIN_A[d] + WIN_P >= WIN_R * (d + 1) for d in range(NDEV)) and WIN_A[-1] + WIN_P == IN_COLS
BIG = [("w_proj_a", GW // NDEV, False), ("w_proj_b", SI // NDEV, False), ("w_out", D // NDEV, False),
       ("ffn_w_up", 2 * DFF // NDEV, True), ("ffn_w_down", DFF // NDEV, False), ("w_in", WIN_P, True)]
VECTORS = ["gate_bias", "ssm_conv_w", "ffn_conv_w"]


def _round_up(n, k):
    return (n + k - 1) // k * k


BIG_OFF = {}
_off = 0
for _n, _r, _t in BIG:
    BIG_OFF[_n] = _off
    _off += _r
BIG_USED = _off
BIG_ROWS = _round_up(BIG_USED, 2 * SUBLANE)
assert all(BIG_OFF[n] % (2 * SUBLANE) == 0 for n, _, _ in BIG)
VEC_SHAPE = {n: (SHARDED[n][0][0], SHARDED[n][0][1] // NDEV) for n in VECTORS}


def _win_offset(dev):
    return WIN_R * dev - WIN_R * dev // BTILE * BTILE


def _pack_big(arrs, dtype, dev):
    parts = []
    for n, r, t in BIG:
        a = (arrs[n].T if t else arrs[n]).astype(dtype)
        if n == "w_in":
            a = lax.dynamic_update_slice(jnp.zeros((WIN_P, D), dtype), a, (_win_offset(dev), 0))
        parts.append(a)
    parts.append(jnp.zeros((BIG_ROWS - BIG_USED, D), dtype))
    return jnp.concatenate(parts, axis=0)


def _join_windows(win):
    parts = []
    for d in range(NDEV):
        lo = BTILE if WIN_A[d] % WIN_R else 0
        if lo:
            parts.append(win[d - 1, WIN_P - BTILE:] + win[d, :BTILE])
        hi = WIN_P - BTILE if d + 1 < NDEV and WIN_A[d + 1] < WIN_A[d] + WIN_P else WIN_P
        parts.append(win[d, lo:hi])
    return jnp.concatenate(parts, axis=0)


def _split_windows(main, last):
    assert WIN_A[-2] + WIN_P <= PMAIN
    wins = [main[a:a + WIN_P] for a in WIN_A[:-1]]
    return jnp.stack(wins + [jnp.concatenate([main[WIN_A[-1]:], last], axis=0)])


LATE_ROWS = BIG_OFF["w_in"]
assert LATE_ROWS + WIN_P == BIG_ROWS and BIG[-1][0] == "w_in"


def _remote(src, dst, send_sems, recv_sems, k, to):
    return pltpu.make_async_remote_copy(src_ref=src, dst_ref=dst, send_sem=send_sems.at[k], recv_sem=recv_sems.at[k],
                                        device_id=to, device_id_type=MESH)


class _Exchange:
    def __init__(self, early_shards, late_shard, c_idx):
        self.early_shards, self.late_shard, self.c_idx = early_shards, late_shard, c_idx

    @staticmethod
    def _gather_side(shards):
        def make(ins, outs, send_sems, recv_sems):
            x, y, c = _axes()
            peers = [(x, y, 1 - c), (1 - x, y, c), (x, 1 - y, c), (1 - x, 1 - y, c)]
            copies = []
            for a, (x_ref, out) in enumerate(zip(ins, outs)):
                mine = out.at[4 * x + 2 * y + c]
                copies += [_remote(x_ref, mine, send_sems, recv_sems, 5 * a + k, p) for k, p in enumerate(peers)]
                copies.append(pltpu.make_async_copy(x_ref, mine, send_sems.at[5 * a + 4]))
            return copies

        return _Side(shards, [jax.ShapeDtypeStruct((NDEV,) + s.shape, s.dtype) for s in shards], 5 * len(shards), make)

    @staticmethod
    def _pass_on_side(bufs):
        def make(ins, outs, send_sems, recv_sems):
            x, y, c = _axes()
            slots = [4 * px + 2 * py + c for px, py in [(1 - x, y), (x, 1 - y), (1 - x, 1 - y)]]
            return [_remote(src.at[s], dst.at[s], send_sems, recv_sems, 3 * a + k, (x, y, 1 - c))
                    for a, (src, dst) in enumerate(zip(ins, outs)) for k, s in enumerate(slots)]

        return _Side(bufs, [jax.ShapeDtypeStruct(b.shape, b.dtype) for b in bufs], 3 * len(bufs), make,
                     aliases=[(a, a) for a in range(len(bufs))])

    def early_start(self):
        return self._gather_side(self.early_shards)

    def early_weights(self, w, outs):
        bufs = _run_side(self._pass_on_side(outs), name="w_in_pass_on")
        w = dict(w)
        w["w_in_t"] = _join_windows(bufs[0])
        for n, a in zip(VECTORS, bufs[1:]):
            r, c = VEC_SHAPE[n]
            w[n] = a[:, 0].transpose(1, 0, 2).reshape(r, NDEV * c)
        return w

    def gather_start(self):
        return self._gather_side([self.late_shard])

    def gather_pass_on(self, outs):
        return self._pass_on_side(outs)

    def late_weights(self, w, outs):
        (buf,) = outs
        w = dict(w)
        for n, r, t in BIG[:-1]:
            w[n + "_t" if t else n] = buf[:, BIG_OFF[n]:BIG_OFF[n] + r].reshape(NDEV * r, D)
        return w

    @staticmethod
    def _plan_side(src, plan):
        n = len(plan(0, 0, 0))

        def make(ins, outs, send_sems, recv_sems):
            (s,), (dst,) = ins, outs
            return [_remote(s.at[slab], dst.at[k], send_sems, recv_sems, k, peer)
                    for k, (slab, peer) in enumerate(plan(*_axes()))]

        return _Side([src], [jax.ShapeDtypeStruct((n,) + src.shape[1:], src.dtype)], n, make)

    def _to_chips(self, send, sib, tag):
        sums = _pair_add(send.reshape((4, 2) + send.shape[1:]), sib, self.c_idx, name=tag + "_grad_pair_add")
        return sums, self._plan_side(sums, _to_chips_plan)

    def late_to_sibling(self, grads):
        send = jnp.concatenate([grads[n + "_t" if t else n].reshape(NDEV, r, D) for n, r, t in BIG[:-1]], axis=1)
        self.late_send = send.astype(BF16)
        return self._plan_side(self.late_send, _to_sibling_plan)

    def reduce_late(self, outs):
        self.late_sum, side = self._to_chips(self.late_send, outs[0], "late")
        return side

    def w_in_to_sibling(self, grad_main, grad_dt):
        self.w_in_send = _split_windows(grad_main.astype(BF16), grad_dt.astype(BF16))
        return self._plan_side(self.w_in_send, _to_sibling_plan)

    def reduce_w_in(self, outs):
        self.w_in_sum, side = self._to_chips(self.w_in_send, outs[0], "w_in")
        return side

    def reduced(self, late_outs, w_in_outs):
        (self.late_from_chips,), (self.w_in_from_chips,) = late_outs, w_in_outs


def kernel(x, mix_norm_w, w_in, gate_bias, gmlp_ln_w, gmlp_ln_b, gmlp_ws, gmlp_bs, ssm_conv_w, ssm_conv_b, ssm_dt_bias, ssm_a_log, ssm_d, ssm_norm_w, w_proj_a, w_proj_b, w_out, ffn_norm_w, ffn_w_up, ffn_conv_w, ffn_conv_b, ffn_w_down, final_norm_w, loss_target, m_mix_norm_w, m_w_in, m_gate_bias, m_gmlp_ln_w, m_gmlp_ln_b, m_gmlp_ws, m_gmlp_bs, m_ssm_conv_w, m_ssm_conv_b, m_ssm_dt_bias, m_ssm_a_log, m_ssm_d, m_ssm_norm_w, m_w_proj_a, m_w_proj_b, m_w_out, m_ffn_norm_w, m_ffn_w_up, m_ffn_conv_w, m_ffn_conv_b, m_ffn_w_down, m_final_norm_w, v_mix_norm_w, v_w_in, v_gate_bias, v_gmlp_ln_w, v_gmlp_ln_b, v_gmlp_ws, v_gmlp_bs, v_ssm_conv_w, v_ssm_conv_b, v_ssm_dt_bias, v_ssm_a_log, v_ssm_d, v_ssm_norm_w, v_w_proj_a, v_w_proj_b, v_w_out, v_ffn_norm_w, v_ffn_w_up, v_ffn_conv_w, v_ffn_conv_b, v_ffn_w_down, v_final_norm_w):
    given = dict(locals())
    wts = {n: given[n] for n in WEIGHTS}
    mom = {n: given["m_" + n] for n in WEIGHTS}
    var = {n: given["v_" + n] for n in WEIGHTS}
    xi, yi, ci = _axes()
    c_idx = jnp.reshape(ci, (1,)).astype(jnp.int32)
    q_idx = jnp.reshape(2 * xi + yi, (1,)).astype(jnp.int32)
    big_names = [n for n, _, _ in BIG]
    drop = lambda d, names: {n: d[n][0] for n in names}

    dev = 4 * xi + 2 * yi + ci
    packed = _pack_big(drop(wts, big_names), BF16, dev)
    full = {n: wts[n].reshape(REPLICATED[n]) for n in REPL_ORDER}

    hooks = _Exchange([packed[LATE_ROWS:]] + [wts[n] for n in VECTORS], packed[:LATE_ROWS], c_idx)
    loss_local, grad_x, grads = _local_step(x[0], loss_target[0], full, hooks)
    g_late = _grad_sum(hooks.late_sum, hooks.late_from_chips, q_idx, name="late_grad_sum")
    g_win = _grad_sum(hooks.w_in_sum, hooks.w_in_from_chips, q_idx, name="w_in_grad_sum")

    small = VECTORS + REPL_ORDER
    as_2d = lambda a: a if a.ndim >= 2 else a[None]
    part = [grads[n].reshape((1,) + SHARDED[n][0] if n in VECTORS else as_2d(wts[n]).shape) for n in small]
    part.append(loss_local)
    from_sibling = _exchange([p[None] for p in part], lambda x, y, c: [(0, (x, y, 1 - c))],
                             name="small_grads_to_sibling")
    chip_sums = _pair_sum_many(part, from_sibling, name="small_grad_pair_sum")
    chip_sums = [s.astype(BF16) if s.size >= SMALL_BF16_FROM else s for s in chip_sums]
    from_chips = _exchange([s[None] for s in chip_sums],
                           lambda x, y, c: [(0, (1 - x, y, c)), (0, (x, 1 - y, c)), (0, (1 - x, 1 - y, c))],
                           name="small_grads_to_chips")
    totals = _chip_sum_many(chip_sums, from_chips, q_idx, name="small_grad_chip_sum")
    g_small, loss = dict(zip(small, totals)), totals[-1][0, 0]
    for n in VECTORS:
        c = VEC_SHAPE[n][1]
        g_small[n] = lax.dynamic_slice_in_dim(g_small[n], dev * c, c, axis=2)

    outs = {}
    small_g = [g_small[n] for n in small]
    small_out = _adam_many(small_g, *[[as_2d(d[n]) for n in small] for d in (wts, mom, var)], name="adam_small")
    for i, n in enumerate(small):
        outs[n] = tuple(a[i].reshape(wts[n].shape) for a in (small_g,) + tuple(small_out))
    for n, r, t in BIG:
        if n == "w_in":
            g = lax.dynamic_slice(g_win, (_win_offset(dev), 0), (WIN_R, D))
        else:
            g = g_late[BIG_OFF[n]:BIG_OFF[n] + r]
        flip = (lambda a: a.transpose(0, 2, 1)) if t else (lambda a: a)
        g = g[None]
        new = _adam(g, flip(wts[n]), flip(mom[n]), flip(var[n]), name="adam_" + n)
        outs[n] = tuple(flip(a) for a in (g,) + tuple(new))
    return (loss, grad_x[None]) + tuple(outs[n][k] for k in range(4) for n in WEIGHTS)
```

```python
import functools

import jax
import jax.numpy as jnp
from jax import lax
from jax.experimental import pallas as pl
from jax.experimental.pallas import tpu as pltpu

F32 = jnp.float32
BF16 = jnp.bfloat16

D = 1024
EPS = 1e-5
GW = 1024
GB = 128
GG = 8
GD = 128
GCH = 64
SI = 2048
SH = 32
SP = 64
SG = 4
SN = 128
SGW = SI // SG
SK = 4
SXBC = SI + 2 * SG * SN
DFF = 2816
FK = 3
PMAIN = 2 * D + 2 * GW + SI + SXBC
IN_COLS = PMAIN + SH
DP_SSM = SI + SXBC
DP_GAP = (DP_SSM - (2 * D + 2 * GW) % DP_SSM) % DP_SSM
DP_COLS = 2 * D + 2 * GW + DP_GAP + DP_SSM
assert DP_GAP % D == 0 and (2 * D + 2 * GW) % D == 0
NDEV = 8
ADAM_LR, ADAM_B1, ADAM_B2, ADAM_EPS, ADAM_WD, ADAM_STEP = 0.001, 0.9, 0.999, 1e-08, 0.01, 10

LANE = 128
SUBLANE = 8
VMEM_MB_V7X = 64
VMEM_CAP_MB = VMEM_MB_V7X - 8

LS = 128
FT = DFF // 2

NN = (((1,), (0,)), ((), ()))
NT = (((1,), (1,)), ((), ()))
TN = (((0,), (0,)), ((), ()))


def _params(sem, vmem_mb):
    return pltpu.CompilerParams(dimension_semantics=sem,
                                vmem_limit_bytes=min(int(vmem_mb), VMEM_CAP_MB) * 1024 * 1024)


def _dot(a, b, dims=NN):
    return lax.dot_general(a, b, dims, preferred_element_type=F32)


def _sigmoid(x):
    return 1.0 / (1.0 + jnp.exp(-x))


def _split3(v):
    hi = v.astype(BF16)
    r = v - hi.astype(F32)
    mid = r.astype(BF16)
    lo = (r - mid.astype(F32)).astype(BF16)
    return hi, mid, lo


def _dot3(a_f32, b_bf16, dims):
    hi, mid, lo = _split3(a_f32)
    return _dot(hi, b_bf16, dims) + _dot(mid, b_bf16, dims) + _dot(lo, b_bf16, dims)


def _dot2(a_f32, b_bf16, dims):
    hi, mid, _ = _split3(a_f32)
    return _dot(hi, b_bf16, dims) + _dot(mid, b_bf16, dims)


def _dot3_rhs(a_bf16, b_f32, dims):
    hi, mid, lo = _split3(b_f32)
    return _dot(a_bf16, hi, dims) + _dot(a_bf16, mid, dims) + _dot(a_bf16, lo, dims)


def _matmul(a, b, *, name, out_dtype, ta=False, tb=False, tm=1024, tn=1024, tk=1024, add=None,
            j_outer=False, b_rows=None, a_gap=None, o_row=None, side=None):
    assert not (o_row and add is not None)
    gap0, gapw = a_gap or (0, 0)
    if ta:
        K, M = a.shape
        M -= gapw
    else:
        M, K = a.shape
        K -= gapw
    if tb:
        N, K2 = b.shape
        N = b_rows or N
    else:
        K2, N = b.shape
        K2 = b_rows or K2
    assert K == K2, (a.shape, b.shape, ta, tb)
    tm, tn, tk = min(tm, M), min(tn, N), min(tk, K)
    assert M % tm == 0 and N % tn == 0 and K % tk == 0, (M, N, K, tm, tn, tk)
    nk = K // tk
    dims = (((0 if ta else 1,), (1 if tb else 0,)), ((), ()))
    has_add = add is not None
    n_in = 3 if has_add else 2
    s_in = len(side.inputs) if side else 0
    s_out = len(side.out_shapes) if side else 0
    grid = (N // tn, M // tm, nk) if j_outer else (M // tm, N // tn, nk)

    def body(*refs):
        a_ref, b_ref = refs[:2]
        add_ref = refs[2] if has_add else None
        o_ref = refs[n_in + s_in]
        if side:
            side_refs = (refs[n_in:n_in + s_in], refs[n_in + s_in + 1:n_in + s_in + 1 + s_out]) + tuple(refs[-2:])
            ids = [pl.program_id(d) for d in range(3)]
            first = functools.reduce(jnp.logical_and, [i == 0 for i in ids])
            last = functools.reduce(jnp.logical_and, [i == g - 1 for i, g in zip(ids, grid)])

            @pl.when(first)
            def _():
                for cp in side.make(*side_refs):
                    cp.start()

            @pl.when(last)
            def _():
                for cp in side.make(*side_refs):
                    cp.wait()

        p = lax.dot_general(a_ref[...].astype(BF16), b_ref[...].astype(BF16), dims,
                            preferred_element_type=F32)

        def finish(acc):
            if has_add:
                acc = acc + add_ref[...].astype(F32)
            o_ref[...] = acc.astype(o_ref.dtype)

        if nk == 1:
            finish(p)
        else:
            acc_ref = refs[n_in + s_in + 1 + s_out]
            k = pl.program_id(2)

            @pl.when(k == 0)
            def _():
                acc_ref[...] = p

            @pl.when(jnp.logical_and(k > 0, k < nk - 1))
            def _():
                acc_ref[...] += p

            @pl.when(k == nk - 1)
            def _():
                finish(acc_ref[...] + p)

    if j_outer:
        ij = lambda g0, g1: (g1, g0)
    else:
        ij = lambda g0, g1: (g0, g1)

    ta_col = tm if ta else tk
    assert gap0 % ta_col == 0 and gapw % ta_col == 0, (a_gap, ta_col)

    def a_map(g0, g1, k):
        i, _ = ij(g0, g1)
        col = i if ta else k
        col = col + jnp.where(col >= gap0 // ta_col, gapw // ta_col, 0) if gapw else col
        return (k, col) if ta else (i, col)

    def b_map(g0, g1, k):
        _, j = ij(g0, g1)
        return (j, k) if tb else (k, j)

    def o_map(g0, g1, k):
        i, j = ij(g0, g1)
        return (o_row(i) if o_row else i, j)

    in_specs = [pl.BlockSpec((tk, tm) if ta else (tm, tk), a_map),
                pl.BlockSpec((tn, tk) if tb else (tk, tn), b_map)]
    args = [a, b]
    if has_add:
        in_specs.append(pl.BlockSpec((tm, tn), o_map))
        args.append(add)
    scratch = [pltpu.VMEM((tm, tn), F32)] if nk > 1 else []
    osz = jnp.dtype(out_dtype).itemsize
    est = (2 * (tm * tk * a.dtype.itemsize + tk * tn * b.dtype.itemsize) + 2 * tm * tn * osz
           + (2 * tm * tn * add.dtype.itemsize if has_add else 0)
           + 3 * tm * tn * 4 + 2 * (tm * tk + tk * tn)) / 2 ** 20 + 4
    out_specs = [pl.BlockSpec((tm, tn), o_map)]
    out_shape = [jax.ShapeDtypeStruct((M, N), out_dtype)]
    aliases = {}
    if side:
        hbm = pl.BlockSpec(memory_space=pltpu.HBM)
        in_specs += [hbm] * s_in
        args += list(side.inputs)
        out_specs += [hbm] * s_out
        out_shape += list(side.out_shapes)
        scratch += [pltpu.SemaphoreType.DMA((side.nsem,)), pltpu.SemaphoreType.DMA((side.nsem,))]
        aliases = {n_in + i: 1 + j for i, j in side.aliases}
    outs = pl.pallas_call(
        body, grid=grid, in_specs=in_specs, out_specs=out_specs, out_shape=out_shape, scratch_shapes=scratch,
        input_output_aliases=aliases, name=name,
        compiler_params=_params(("arbitrary", "arbitrary", "arbitrary"), est))(*args)
    return (outs[0], list(outs[1:])) if side else outs[0]


class _Side:
    def __init__(self, inputs, out_shapes, nsem, make, aliases=()):
        self.inputs, self.out_shapes, self.nsem, self.make, self.aliases = inputs, out_shapes, nsem, make, aliases


def _rms_fwd(x, w, *, name, side=None):
    T = x.shape[0]
    tm = min(512, T)

    def body(x_ref, w_ref, o_ref):
        xv = x_ref[...]
        r = lax.rsqrt(jnp.mean(xv * xv, axis=-1, keepdims=True) + EPS)
        o_ref[...] = (xv * r * w_ref[...]).astype(BF16)

    x_in, x_args, x_out, x_shapes, x_scratch = _ride_args(side)
    outs = pl.pallas_call(
        _ride(body, 2, 1, side, T // tm), grid=(T // tm,),
        in_specs=[pl.BlockSpec((tm, D), lambda i: (i, 0)), pl.BlockSpec((1, D), lambda i: (0, 0))] + x_in,
        out_specs=[pl.BlockSpec((tm, D), lambda i: (i, 0))] + x_out,
        out_shape=[jax.ShapeDtypeStruct((T, D), BF16)] + x_shapes, scratch_shapes=x_scratch, name=name,
        compiler_params=_params(("arbitrary",), 24))(x, w, *x_args)
    return (outs[0], list(outs[1:])) if side else outs[0]


def _rms_bwd(x, w, dy, dres, *, name, matmul_copy=False):
    T = x.shape[0]
    tm = min(512, T)

    def body(x_ref, w_ref, dy_ref, dres_ref, dx_ref, dw_ref, *copy_ref):
        xv = x_ref[...]
        r = lax.rsqrt(jnp.mean(xv * xv, axis=-1, keepdims=True) + EPS)
        xhat = xv * r
        dyv = dy_ref[...].astype(F32)
        g = dyv * w_ref[...]
        dx = dres_ref[...] + r * (g - xhat * jnp.mean(g * xhat, axis=-1, keepdims=True))
        dx_ref[...] = dx
        if matmul_copy:
            copy_ref[0][...] = dx.astype(BF16)
        part = jnp.sum(dyv * xhat, axis=0, keepdims=True)

        @pl.when(pl.program_id(0) == 0)
        def _():
            dw_ref[...] = part

        @pl.when(pl.program_id(0) > 0)
        def _():
            dw_ref[...] += part

    row = pl.BlockSpec((tm, D), lambda i: (i, 0))
    vec = pl.BlockSpec((1, D), lambda i: (0, 0))
    extra = [jax.ShapeDtypeStruct((T, D), BF16)] if matmul_copy else []
    return pl.pallas_call(
        body, grid=(T // tm,), in_specs=[row, vec, row, row], out_specs=[row, vec] + [row] * len(extra),
        out_shape=[jax.ShapeDtypeStruct((T, D), F32), jax.ShapeDtypeStruct((1, D), F32)] + extra, name=name,
        compiler_params=_params(("arbitrary",), 32))(x, w, dy, dres)


def _loss_head(h, tgt, w):
    T = h.shape[0]
    tm = min(512, T)

    def body(h_ref, t_ref, w_ref, loss_ref, dh_ref, dw_ref, dhb_ref):
        hv = h_ref[...]
        r = lax.rsqrt(jnp.mean(hv * hv, axis=-1, keepdims=True) + EPS)
        xhat = hv * r
        wv = w_ref[...]
        err = xhat * wv - t_ref[...]
        lpart = 0.5 * jnp.sum(jnp.mean(err * err, axis=-1, keepdims=True), axis=0, keepdims=True)
        dy = err * (1.0 / D)
        g = dy * wv
        dh = r * (g - xhat * jnp.mean(g * xhat, axis=-1, keepdims=True))
        dh_ref[...] = dh
        dhb_ref[...] = dh.astype(BF16)
        wpart = jnp.sum(dy * xhat, axis=0, keepdims=True)
        lrow = jnp.broadcast_to(lpart, (1, LANE))

        @pl.when(pl.program_id(0) == 0)
        def _():
            dw_ref[...] = wpart
            loss_ref[...] = lrow

        @pl.when(pl.program_id(0) > 0)
        def _():
            dw_ref[...] += wpart
            loss_ref[...] += lrow

    row = pl.BlockSpec((tm, D), lambda i: (i, 0))
    vec = pl.BlockSpec((1, D), lambda i: (0, 0))
    return pl.pallas_call(
        body, grid=(T // tm,), in_specs=[row, row, vec],
        out_specs=[pl.BlockSpec((1, LANE), lambda i: (0, 0)), row, vec, row],
        out_shape=[jax.ShapeDtypeStruct((1, LANE), F32), jax.ShapeDtypeStruct((T, D), F32),
                   jax.ShapeDtypeStruct((1, D), F32), jax.ShapeDtypeStruct((T, D), BF16)], name="loss_head",
        compiler_params=_params(("arbitrary",), 32))(h, tgt, w)


_GELU_C = 0.7978845608028654
_GELU_A = 0.044715


def _gelu(x, with_grad=False):
    x2 = x * x
    cx = _GELU_C * x
    t = jnp.tanh(cx * (1.0 + _GELU_A * x2))
    h = 0.5 * (1.0 + t)
    if not with_grad:
        return x * h
    return x * h, h + 0.5 * cx * (1.0 - t * t) * (1.0 + 3.0 * _GELU_A * x2)


def _gmlp_mask():
    r = lax.broadcasted_iota(jnp.int32, (GB, GB), 0) // GCH
    c = lax.broadcasted_iota(jnp.int32, (GB, GB), 1) // GCH
    return c <= r


def _ride(body, n_in, n_out, side, nsteps):
    if not side:
        return body
    s_in, s_out = len(side.inputs), len(side.out_shapes)

    def outer(*refs):
        ins, side_ins = refs[:n_in], refs[n_in:n_in + s_in]
        outs = refs[n_in + s_in:n_in + s_in + n_out]
        side_refs = (side_ins, refs[n_in + s_in + n_out:n_in + s_in + n_out + s_out]) + tuple(refs[-2:])

        @pl.when(pl.program_id(0) == 0)
        def _():
            for cp in side.make(*side_refs):
                cp.start()

        @pl.when(pl.program_id(0) == nsteps - 1)
        def _():
            for cp in side.make(*side_refs):
                cp.wait()

        body(*ins, *outs)

    return outer


def _ride_args(side):
    if not side:
        return [], [], [], [], []
    hbm = pl.BlockSpec(memory_space=pltpu.HBM)
    sems = [pltpu.SemaphoreType.DMA((side.nsem,)), pltpu.SemaphoreType.DMA((side.nsem,))]
    return [hbm] * len(side.inputs), list(side.inputs), [hbm] * len(side.out_shapes), list(side.out_shapes), sems


def _gmlp_fwd(proj, lnw, lnb, ws, bst, side=None):
    T = proj.shape[0]
    tm = min(512, T)
    nblk = tm // GB

    def body(u_ref, v_ref, lnw_ref, lnb_ref, ws_ref, bst_ref, o_ref):
        mask = _gmlp_mask()
        u = _gelu(u_ref[...].astype(F32))
        v = _gelu(v_ref[...].astype(F32))
        for g in range(GG):
            cs = slice(g * GD, (g + 1) * GD)
            vg = v[:, cs]
            mu = jnp.mean(vg, axis=-1, keepdims=True)
            vc = vg - mu
            var = jnp.mean(vc * vc, axis=-1, keepdims=True)
            vn = (vc * lax.rsqrt(var + EPS) * lnw_ref[g:g + 1, :] + lnb_ref[g:g + 1, :]).astype(BF16)
            wsg = jnp.where(mask, ws_ref[g], 0.0).astype(BF16)
            bcol = bst_ref[:, g:g + 1]
            for blk in range(nblk):
                rs = slice(blk * GB, (blk + 1) * GB)
                sv = _dot(wsg, vn[rs, :]) + bcol
                o_ref[rs, cs] = (u[rs, cs] * sv).astype(BF16)

    full = lambda shape: pl.BlockSpec(shape, lambda i: tuple(0 for _ in shape))
    x_in, x_args, x_out, x_shapes, x_scratch = _ride_args(side)
    outs = pl.pallas_call(
        _ride(body, 6, 1, side, T // tm), grid=(T // tm,),
        in_specs=[pl.BlockSpec((tm, GW), lambda i: (i, 2)), pl.BlockSpec((tm, GW), lambda i: (i, 3)),
                  full((GG, GD)), full((GG, GD)), full((GG, GB, GB)), full((GB, GG))] + x_in,
        out_specs=[pl.BlockSpec((tm, GW), lambda i: (i, 0))] + x_out,
        out_shape=[jax.ShapeDtypeStruct((T, GW), BF16)] + x_shapes, scratch_shapes=x_scratch,
        input_output_aliases={6 + i: 1 + j for i, j in (side.aliases if side else ())}, name="gmlp_fwd",
        compiler_params=_params(("arbitrary",), 40))(proj, proj, lnw, lnb, ws, bst, *x_args)
    return outs[0], list(outs[1:])


def _gmlp_bwd(proj, dya, dproj, lnw, lnb, ws, bst, side=None):
    T = proj.shape[0]
    tm = min(512, T)
    nblk = tm // GB
    n_in, n_out = 8, 5

    def body(u_ref, v_ref, dya_ref, dproj_in, lnw_ref, lnb_ref, ws_ref, bst_ref,
             dz_ref, dlnw_ref, dlnb_ref, dws_ref, dbst_ref):
        del dproj_in
        first = pl.program_id(0) == 0

        @pl.when(first)
        def _():
            dlnw_ref[...] = jnp.zeros_like(dlnw_ref)
            dlnb_ref[...] = jnp.zeros_like(dlnb_ref)
            dws_ref[...] = jnp.zeros_like(dws_ref)
            dbst_ref[...] = jnp.zeros_like(dbst_ref)

        mask = _gmlp_mask()
        lane = lax.broadcasted_iota(jnp.int32, (GB, LANE), 1)
        ur = u_ref[...].astype(F32)
        vr = v_ref[...].astype(F32)
        u, gu = _gelu(ur, with_grad=True)
        v, gv = _gelu(vr, with_grad=True)
        dy = dya_ref[...].astype(F32)
        dbst = jnp.zeros((GB, LANE), F32)
        dlnw_rows, dlnb_rows = [], []
        for g in range(GG):
            cs = slice(g * GD, (g + 1) * GD)
            vg = v[:, cs]
            mu = jnp.mean(vg, axis=-1, keepdims=True)
            vc = vg - mu
            var = jnp.mean(vc * vc, axis=-1, keepdims=True)
            rstd = lax.rsqrt(var + EPS)
            xhat = vc * rstd
            lw = lnw_ref[g:g + 1, :]
            vn = (xhat * lw + lnb_ref[g:g + 1, :]).astype(BF16)
            wsg = jnp.where(mask, ws_ref[g], 0.0).astype(BF16)
            bcol = bst_ref[:, g:g + 1]
            dyg = dy[:, cs]
            ug = u[:, cs]
            dsv = dyg * ug
            dsv_b = dsv.astype(BF16)
            dws_g = jnp.zeros((GB, GB), F32)
            bsum = jnp.zeros((GB, 1), F32)
            dvn_parts = []
            for blk in range(nblk):
                rs = slice(blk * GB, (blk + 1) * GB)
                sv = _dot(wsg, vn[rs, :]) + bcol
                dz_ref[rs, cs] = (dyg[rs, :] * sv * gu[rs, cs]).astype(BF16)
                dws_g = dws_g + _dot(dsv_b[rs, :], vn[rs, :], NT)
                bsum = bsum + jnp.sum(dsv[rs, :], axis=-1, keepdims=True)
                dvn_parts.append(_dot(wsg, dsv_b[rs, :], TN))
            dvn = jnp.concatenate(dvn_parts, axis=0)
            dws_ref[g] += jnp.where(mask, dws_g, 0.0)
            dbst = dbst + jnp.where(lane == g, bsum, 0.0)
            dlnw_rows.append(jnp.sum(dvn * xhat, axis=0, keepdims=True))
            dlnb_rows.append(jnp.sum(dvn, axis=0, keepdims=True))
            dxh = dvn * lw
            dvg = rstd * (dxh - jnp.mean(dxh, axis=-1, keepdims=True)
                          - xhat * jnp.mean(dxh * xhat, axis=-1, keepdims=True))
            dz_ref[:, GW + g * GD:GW + (g + 1) * GD] = (dvg * gv[:, cs]).astype(BF16)
        dlnw_ref[...] += jnp.concatenate(dlnw_rows, axis=0)
        dlnb_ref[...] += jnp.concatenate(dlnb_rows, axis=0)
        dbst_ref[...] += dbst

    full = lambda shape: pl.BlockSpec(shape, lambda i: tuple(0 for _ in shape))
    x_in, x_args, x_out, x_shapes, x_scratch = _ride_args(side)
    outs = pl.pallas_call(
        _ride(body, n_in, n_out, side, T // tm), grid=(T // tm,),
        in_specs=[pl.BlockSpec((tm, GW), lambda i: (i, 2)), pl.BlockSpec((tm, GW), lambda i: (i, 3)),
                  pl.BlockSpec((tm, GW), lambda i: (i, 0)), pl.BlockSpec(memory_space=pl.ANY),
                  full((GG, GD)), full((GG, GD)), full((GG, GB, GB)), full((GB, GG))] + x_in,
        out_specs=[pl.BlockSpec((tm, 2 * GW), lambda i: (i, 1)), full((GG, GD)), full((GG, GD)),
                   full((GG, GB, GB)), full((GB, LANE))] + x_out,
        out_shape=[jax.ShapeDtypeStruct(dproj.shape, dproj.dtype), jax.ShapeDtypeStruct((GG, GD), F32),
                   jax.ShapeDtypeStruct((GG, GD), F32), jax.ShapeDtypeStruct((GG, GB, GB), F32),
                   jax.ShapeDtypeStruct((GB, LANE), F32)] + x_shapes,
        scratch_shapes=x_scratch, input_output_aliases={3: 0}, name="gmlp_bwd",
        compiler_params=_params(("arbitrary",), 48))(proj, proj, dya, dproj, lnw, lnb, ws, bst, *x_args)
    return tuple(outs[:n_out]), list(outs[n_out:])


def _merge_fwd(ya_pre, yb_pre, proj, bias, wpa, wpb):
    T = proj.shape[0]
    tm = min(512, T)

    def body(ya_ref, yb_ref, g_ref, b_ref, wpa_ref, wpb_ref, m_ref, oa_ref, ob_ref):
        ya = _dot(ya_ref[...], wpa_ref[...])
        yb = _dot(yb_ref[...], wpb_ref[...])
        g = g_ref[...].astype(F32)
        sa = _sigmoid(g[:, :D] + b_ref[0:1, :])
        sb = _sigmoid(g[:, D:] + b_ref[1:2, :])
        m_ref[...] = (sa * ya + sb * yb).astype(BF16)
        oa_ref[...] = ya.astype(BF16)
        ob_ref[...] = yb.astype(BF16)

    row = lambda w: pl.BlockSpec((tm, w), lambda i: (i, 0))
    full = lambda shape: pl.BlockSpec(shape, lambda i: tuple(0 for _ in shape))
    o = jax.ShapeDtypeStruct((T, D), BF16)
    return pl.pallas_call(
        body, grid=(T // tm,),
        in_specs=[row(GW), row(SI), row(2 * D), full((2, D)), full((GW, D)), full((SI, D))],
        out_specs=[row(D), row(D), row(D)], out_shape=[o, o, o], name="merge_fwd",
        compiler_params=_params(("arbitrary",), 40))(ya_pre, yb_pre, proj, bias, wpa, wpb)


def _merge_bwd(dm, proj, bias, ya, yb, wpa, wpb):
    T = proj.shape[0]
    tm = min(512, T)

    def body(dm_ref, g_ref, b_ref, ya_ref, yb_ref, wpa_ref, wpb_ref,
             dg_ref, dya_ref, dyb_ref, dpa_ref, dpb_ref, db_ref):
        dmv = dm_ref[...].astype(F32)
        g = g_ref[...].astype(F32)
        sa = _sigmoid(g[:, :D] + b_ref[0:1, :])
        sb = _sigmoid(g[:, D:] + b_ref[1:2, :])
        dya = (dmv * sa).astype(BF16)
        dyb = (dmv * sb).astype(BF16)
        dga = dmv * ya_ref[...].astype(F32) * sa * (1.0 - sa)
        dgb = dmv * yb_ref[...].astype(F32) * sb * (1.0 - sb)
        dg_ref[:, :D] = dga.astype(BF16)
        dg_ref[:, D:] = dgb.astype(BF16)
        dya_ref[...] = dya
        dyb_ref[...] = dyb
        dpa_ref[...] = _dot(dya, wpa_ref[...], NT).astype(BF16)
        dpb_ref[...] = _dot(dyb, wpb_ref[...], NT).astype(BF16)
        part = jnp.concatenate([jnp.sum(dga, axis=0, keepdims=True), jnp.sum(dgb, axis=0, keepdims=True)], axis=0)

        @pl.when(pl.program_id(0) == 0)
        def _():
            db_ref[...] = part

        @pl.when(pl.program_id(0) > 0)
        def _():
            db_ref[...] += part

    row = lambda w: pl.BlockSpec((tm, w), lambda i: (i, 0))
    full = lambda shape: pl.BlockSpec(shape, lambda i: tuple(0 for _ in shape))
    o = lambda w: jax.ShapeDtypeStruct((T, w), BF16)
    return pl.pallas_call(
        body, grid=(T // tm,),
        in_specs=[row(D), row(2 * D), full((2, D)), row(D), row(D), full((GW, D)), full((SI, D))],
        out_specs=[row(2 * D), row(D), row(D), row(GW), row(SI), full((2, D))],
        out_shape=[o(DP_COLS), o(D), o(D), o(GW), o(SI), jax.ShapeDtypeStruct((2, D), F32)], name="merge_bwd",
        compiler_params=_params(("arbitrary",), 48))(dm, proj, bias, ya, yb, wpa, wpb)


RB = 128


def _shift_matrix(j):
    r = lax.broadcasted_iota(jnp.int32, (RB, RB), 0)
    c = lax.broadcasted_iota(jnp.int32, (RB, RB), 1)
    return jnp.where(c == r - j, 1.0, 0.0).astype(BF16)


def _rows_down(xb, before, shifts):
    H = SUBLANE
    mats = [_shift_matrix(j) for j in shifts]
    outs = [[] for _ in shifts]
    for b in range(xb.shape[0] // RB):
        blk = xb[b * RB:(b + 1) * RB]
        edge = jnp.concatenate([before, blk[:2 * H].astype(F32)[:H]], axis=0)
        for i, j in enumerate(shifts):
            outs[i] += [edge[H - j:2 * H - j], _dot(mats[i], blk)[H:]]
        before = blk[RB - 2 * H:].astype(F32)[H:]
    return [jnp.concatenate(o, axis=0) for o in outs]


def _rows_up(xb, after, shifts):
    H = SUBLANE
    nb = xb.shape[0] // RB
    mats = [_shift_matrix(-j) for j in shifts]
    outs = [[] for _ in shifts]
    for b in range(nb):
        blk = xb[b * RB:(b + 1) * RB]
        nxt = xb[(b + 1) * RB:(b + 1) * RB + 2 * H].astype(F32)[:H] if b + 1 < nb else after
        edge = jnp.concatenate([blk[RB - 2 * H:].astype(F32)[H:], nxt], axis=0)
        for i, j in enumerate(shifts):
            outs[i] += [_dot(mats[i], blk)[:RB - H], edge[j:H + j]]
    return [jnp.concatenate(o, axis=0) for o in outs]


def _ffn_act_fwd(up, cw, cb):
    T = up.shape[0]
    tm = min(512, T)
    H = SUBLANE

    def body(up_ref, cw_ref, cb_ref, o_ref, xc_ref, halo):
        @pl.when(pl.program_id(1) == 0)
        def _():
            halo[...] = jnp.zeros_like(halo)

        xb = up_ref[...]
        x2, x1 = _rows_down(xb, halo[...], (2, 1))
        xc = cb_ref[...] + cw_ref[0:1, :] * x2 + cw_ref[1:2, :] * x1 + cw_ref[2:3, :] * xb.astype(F32)
        xc_ref[...] = xc.astype(BF16)
        gate = xc[:, :FT]
        o_ref[...] = (gate * _sigmoid(gate) * xc[:, FT:]).astype(BF16)
        halo[...] = xb[tm - 2 * H:].astype(F32)[H:]

    tile = pl.BlockSpec((tm, 2 * FT), lambda j, i: (i, j))
    return pl.pallas_call(
        body, grid=(2, T // tm),
        in_specs=[tile, pl.BlockSpec((FK, 2 * FT), lambda j, i: (0, j)), pl.BlockSpec((1, 2 * FT), lambda j, i: (0, j))],
        out_specs=[pl.BlockSpec((tm, FT), lambda j, i: (i, j)), tile],
        out_shape=[jax.ShapeDtypeStruct((T, DFF), BF16), jax.ShapeDtypeStruct((T, 2 * DFF), BF16)],
        scratch_shapes=[pltpu.VMEM((H, 2 * FT), F32)], name="ffn_act_fwd",
        compiler_params=_params(("arbitrary", "arbitrary"), 48))(up, cw, cb)


def _ffn_act_bwd(up, xc, dact, cw):
    T = up.shape[0]
    tm = min(512, T)
    nt = T // tm
    H = SUBLANE

    def body(up_ref, xc_ref, da_ref, cw_ref, dup_ref, dcw_ref, dcb_ref, ahead):
        @pl.when(pl.program_id(1) == 0)
        def _():
            ahead[...] = jnp.zeros_like(ahead)
            dcw_ref[...] = jnp.zeros_like(dcw_ref)
            dcb_ref[...] = jnp.zeros_like(dcb_ref)

        xcv = xc_ref[...].astype(F32)
        gate, val = xcv[:, :FT], xcv[:, FT:]
        sg = _sigmoid(gate)
        dav = da_ref[...].astype(F32)
        dgate = dav * val * sg * (1.0 + gate * (1.0 - sg))
        dval = dav * gate * sg
        dxc = jnp.concatenate([dgate, dval], axis=1)
        d1, d2 = _rows_up(dxc.astype(BF16), ahead[...], (1, 2))
        x = up_ref[...].astype(F32)
        dcb_ref[...] += jnp.sum(dxc, axis=0, keepdims=True)
        dcw_ref[...] += jnp.concatenate([jnp.sum(d * x, axis=0, keepdims=True) for d in (d2, d1, dxc)], axis=0)
        dup_ref[...] = (cw_ref[2:3, :] * dxc + cw_ref[1:2, :] * d1 + cw_ref[0:1, :] * d2).astype(BF16)
        ahead[...] = dxc[0:H, :]

    tile = pl.BlockSpec((tm, 2 * FT), lambda j, i: (nt - 1 - i, j))
    return pl.pallas_call(
        body, grid=(2, nt),
        in_specs=[tile, tile, pl.BlockSpec((tm, FT), lambda j, i: (nt - 1 - i, j)),
                  pl.BlockSpec((FK, 2 * FT), lambda j, i: (0, j))],
        out_specs=[tile, pl.BlockSpec((FK, 2 * FT), lambda j, i: (0, j)), pl.BlockSpec((1, 2 * FT), lambda j, i: (0, j))],
        out_shape=[jax.ShapeDtypeStruct((T, 2 * DFF), BF16), jax.ShapeDtypeStruct((FK, 2 * DFF), F32),
                   jax.ShapeDtypeStruct((1, 2 * DFF), F32)],
        scratch_shapes=[pltpu.VMEM((H, 2 * FT), F32)], name="ffn_act_bwd",
        compiler_params=_params(("arbitrary", "arbitrary"), 56))(up, xc, dact, cw)


def _softplus(x):
    e = jnp.exp(-jnp.abs(x))
    return jnp.maximum(x, 0.0) + jnp.where(e < 1e-4, e * (1.0 - 0.5 * e), jnp.log(1.0 + e))


def _ssd_tril():
    li = lax.broadcasted_iota(jnp.int32, (LS, LS), 0)
    si = lax.broadcasted_iota(jnp.int32, (LS, LS), 1)
    return si <= li


def _head_expansion():
    hh = lax.broadcasted_iota(jnp.int32, (LANE, SI), 0)
    cc = lax.broadcasted_iota(jnp.int32, (LANE, SI), 1) // SP
    return jnp.where(hh == cc, 1.0, 0.0).astype(BF16)


def _ssd_pre(xc, dt_ref, dtb_ref, alog_ref, tril, expand):
    sx = _sigmoid(xc)
    xbc = xc * sx
    xs, bm, cm = xbc[:, :SI], xbc[:, SI:SI + SG * SN], xbc[:, SI + SG * SN:]
    dtin = dt_ref[...] + dtb_ref[...]
    dt = _softplus(dtin)
    a_neg = -jnp.exp(alog_ref[...])
    dta = dt * a_neg
    trilb = jnp.where(tril, 1.0, 0.0).astype(BF16)
    a = _dot3_rhs(trilb, dta, NN)
    a_exp = _dot3(a, expand, NN)
    dt_exp = _dot2(dt, expand, NN)
    xdt = xs * dt_exp
    a_last = a_exp[LS - 1:LS, :]
    return dict(xc=xc, sx=sx, xs=xs, bm=bm, cm=cm, dtin=dt_ref[...] + dtb_ref[...], dt=dt, a_neg=a_neg,
                a=a, a_t=a.T, a_exp=a_exp, dt_exp=dt_exp, xdt=xdt, ea=jnp.exp(a_exp),
                w=jnp.exp(a_last - a_exp), eal=jnp.exp(a_last))


def _head_decay(pre, tril, h):
    seg = pre["a"][:, h:h + 1] - pre["a_t"][h:h + 1, :]
    return jnp.exp(jnp.where(tril, seg, -1e30))


def _ssd_fwd(proj, dtraw, cw, cb, dtb, alog, dexp, nw):
    T = proj.shape[0]
    nc = T // LS
    H = SUBLANE

    def body(z_ref, x_ref, dt_ref, cw_ref, cb_ref, dtb_ref, alog_ref, dexp_ref, nw_ref, ex_ref,
             yb_ref, y_ref, sp_ref, xc_ref, halo, st):
        @pl.when(pl.program_id(0) == 0)
        def _():
            halo[...] = jnp.zeros_like(halo)
            st[...] = jnp.zeros_like(st)

        xb = x_ref[...]
        taps = _rows_down(xb, halo[...], (3, 2, 1)) + [xb.astype(F32)]
        xc = cb_ref[...]
        for k in range(SK):
            xc = xc + cw_ref[k:k + 1, :] * taps[k]
        xc_ref[...] = xc.astype(BF16)
        tril, expand = _ssd_tril(), ex_ref[...]
        pre = _ssd_pre(xc, dt_ref, dtb_ref, alog_ref, tril, expand)
        lane = lax.broadcasted_iota(jnp.int32, (LS, LANE), 1)
        lo = lane < SP
        zf = z_ref[...].astype(F32)
        siluz = zf * _sigmoid(zf)
        for g in range(SG):
            gs = slice(g * SGW, (g + 1) * SGW)
            bg = pre["bm"][:, g * SN:(g + 1) * SN].astype(BF16)
            cg = pre["cm"][:, g * SN:(g + 1) * SN].astype(BF16)
            gmat = _dot(cg, bg, NT)
            sg = st[g]
            sp_ref[0, g] = sg
            yoff = _dot(cg, sg.astype(BF16))
            parts = []
            for j in range(SGW // LANE):
                h0 = g * (SGW // SP) + 2 * j
                m0 = gmat * _head_decay(pre, tril, h0)
                m1 = gmat * _head_decay(pre, tril, h0 + 1)
                xp = pre["xdt"][:, g * SGW + j * LANE:g * SGW + (j + 1) * LANE]
                rhs = jnp.concatenate([jnp.where(lo, xp, 0.0), jnp.where(lo, 0.0, xp)], axis=0).astype(BF16)
                parts.append(_dot(jnp.concatenate([m0, m1], axis=1).astype(BF16), rhs))
            y = (jnp.concatenate(parts, axis=1) + pre["ea"][:, gs] * yoff + dexp_ref[:, gs] * pre["xs"][:, gs])
            st[g] = pre["eal"][:, gs] * sg + _dot(bg, (pre["w"][:, gs] * pre["xdt"][:, gs]).astype(BF16), TN)
            y_ref[:, gs] = y
            yg = y * siluz[:, gs]
            r = lax.rsqrt(jnp.mean(yg * yg, axis=-1, keepdims=True) + EPS)
            yb_ref[:, gs] = (yg * r * nw_ref[:, gs]).astype(BF16)
        halo[...] = xb[LS - 2 * H:].astype(F32)[H:]

    vec = lambda w: pl.BlockSpec((1, w), lambda c: (0, 0))
    return pl.pallas_call(
        body, grid=(nc,),
        in_specs=[pl.BlockSpec((LS, SI), lambda c: (c, 2)), pl.BlockSpec((LS, SXBC), lambda c: (c, 2)),
                  pl.BlockSpec((LS, LANE), lambda c: (c, 0)),
                  pl.BlockSpec((SK, SXBC), lambda c: (0, 0)), vec(SXBC), vec(LANE), vec(LANE), vec(SI), vec(SI),
                  pl.BlockSpec((LANE, SI), lambda c: (0, 0))],
        out_specs=[pl.BlockSpec((LS, SI), lambda c: (c, 0)), pl.BlockSpec((LS, SI), lambda c: (c, 0)),
                   pl.BlockSpec((1, SG, SN, SGW), lambda c: (c, 0, 0, 0)), pl.BlockSpec((LS, SXBC), lambda c: (c, 0))],
        out_shape=[jax.ShapeDtypeStruct((T, SI), BF16), jax.ShapeDtypeStruct((T, SI), F32),
                   jax.ShapeDtypeStruct((nc, SG, SN, SGW), F32), jax.ShapeDtypeStruct((T, SXBC), BF16)],
        scratch_shapes=[pltpu.VMEM((H, SXBC), F32), pltpu.VMEM((SG, SN, SGW), F32)], name="ssd_fwd",
        compiler_params=_params(("arbitrary",), VMEM_CAP_MB))(
            proj, proj, dtraw, cw, cb, dtb, alog, dexp, nw, _head_expansion())


def _ssd_bwd(proj, xcs, dtraw, y, sprev, dyb, dproj, cw, dtb, alog, dexp, nw):
    T = proj.shape[0]
    nc = T // LS
    H = SUBLANE
    NJ = 1

    def body(z_ref, x_ref, xc_ref, dt_ref, y_ref, sp_ref, dyb_ref, dproj_in,
             cw_ref, dtb_ref, alog_ref, dexp_ref, nw_ref, ex_ref,
             dp_ref, ddt_ref, dcw_ref, dcb_ref, ddtb_ref, da_ref, dd_ref, dnw_ref,
             ahead, ds, stage):
        del dproj_in
        i = pl.program_id(0)
        j = pl.program_id(1)

        @pl.when(jnp.logical_and(i == 0, j == 0))
        def _():
            ahead[...] = jnp.zeros_like(ahead)
            ds[...] = jnp.zeros_like(ds)
            for r in (dcw_ref, dcb_ref, ddtb_ref, da_ref, dd_ref, dnw_ref):
                r[...] = jnp.zeros_like(r)

        @pl.when(j == 0)
        def _():
            tril, expand = _ssd_tril(), ex_ref[...]
            pre = _ssd_pre(xc_ref[...].astype(F32), dt_ref, dtb_ref, alog_ref, tril, expand)
            lane = lax.broadcasted_iota(jnp.int32, (LS, LANE), 1)
            sub = lax.broadcasted_iota(jnp.int32, (LANE, LS), 0)
            rowi = lax.broadcasted_iota(jnp.int32, (LS, 1), 0)
            lo = lane < SP
            xs, xdt, ea, w, eal = pre["xs"], pre["xdt"], pre["ea"], pre["w"], pre["eal"]

            zf = z_ref[...].astype(F32)
            sz = _sigmoid(zf)
            siluz = zf * sz
            yv = y_ref[...]
            yg = yv * siluz
            dout = dyb_ref[...].astype(F32)
            dyg_parts, dnw_parts = [], []
            for g in range(SG):
                gs = slice(g * SGW, (g + 1) * SGW)
                ygg = yg[:, gs]
                r = lax.rsqrt(jnp.mean(ygg * ygg, axis=-1, keepdims=True) + EPS)
                yhat = ygg * r
                dn = dout[:, gs] * nw_ref[:, gs]
                dnw_parts.append(jnp.sum(dout[:, gs] * yhat, axis=0, keepdims=True))
                dyg_parts.append(r * (dn - yhat * jnp.mean(dn * yhat, axis=-1, keepdims=True)))
            dyg = jnp.concatenate(dyg_parts, axis=1)
            dnw_ref[...] += jnp.concatenate(dnw_parts, axis=1)
            dy = dyg * siluz
            stage[:, 0:SI] = (dyg * yv * sz * (1.0 + zf * (1.0 - sz))).astype(BF16)
            dd_ref[...] += jnp.sum(dy * xs, axis=0, keepdims=True)
            tt = ea * dy

            da_rows = jnp.zeros((LS, LANE), F32)
            da_cols = jnp.zeros((LANE, LS), F32)
            dxdt_parts, db_parts, dc_parts, daexp_parts = [], [], [], []
            for g in range(SG):
                gs = slice(g * SGW, (g + 1) * SGW)
                bg = pre["bm"][:, g * SN:(g + 1) * SN].astype(BF16)
                cg = pre["cm"][:, g * SN:(g + 1) * SN].astype(BF16)
                sg = sp_ref[0, g]
                sgb = sg.astype(BF16)
                dsg = ds[g]
                dsgb = dsg.astype(BF16)
                ttg = tt[:, gs].astype(BF16)
                yoff = _dot(cg, sgb)
                dc = _dot(ttg, sgb, NT)
                gmat = _dot(cg, bg, NT)
                dgm = jnp.zeros((LS, LS), F32)
                dxdt_pairs = []
                for jj in range(SGW // LANE):
                    h0 = g * (SGW // SP) + 2 * jj
                    ps = slice(g * SGW + jj * LANE, g * SGW + (jj + 1) * LANE)
                    l0 = _head_decay(pre, tril, h0)
                    l1 = _head_decay(pre, tril, h0 + 1)
                    m0 = gmat * l0
                    m1 = gmat * l1
                    dyp = dy[:, ps]
                    dy_lo = jnp.where(lo, dyp, 0.0).astype(BF16)
                    dy_hi = jnp.where(lo, 0.0, dyp).astype(BF16)
                    xpb = xdt[:, ps].astype(BF16)
                    dm0 = _dot(dy_lo, xpb, NT)
                    dm1 = _dot(dy_hi, xpb, NT)
                    q0 = dm0 * m0
                    q1 = dm1 * m1
                    da_rows = da_rows + jnp.where(lane == h0, jnp.sum(q0, axis=1, keepdims=True), 0.0)
                    da_rows = da_rows + jnp.where(lane == h0 + 1, jnp.sum(q1, axis=1, keepdims=True), 0.0)
                    da_cols = da_cols + jnp.where(sub == h0, jnp.sum(q0, axis=0, keepdims=True), 0.0)
                    da_cols = da_cols + jnp.where(sub == h0 + 1, jnp.sum(q1, axis=0, keepdims=True), 0.0)
                    dgm = dgm + dm0 * l0 + dm1 * l1
                    mcat = jnp.concatenate([m0, m1], axis=0).astype(BF16)
                    dycat = jnp.concatenate([dy_lo, dy_hi], axis=0)
                    dxdt_pairs.append(_dot(mcat, dycat, TN))
                dgb = dgm.astype(BF16)
                dc = dc + _dot(dgb, bg)
                db = _dot(dgb, cg, TN)
                zg = _dot(bg, dsgb)
                wg, xdtg = w[:, gs], xdt[:, gs]
                dxdt_g = jnp.concatenate(dxdt_pairs, axis=1) + wg * zg
                qg = zg * xdtg * wg
                last = (jnp.sum(qg, axis=0, keepdims=True)
                        + jnp.sum(dsg * sg, axis=0, keepdims=True) * eal[:, gs])
                daexp_parts.append(dy[:, gs] * ea[:, gs] * yoff - qg + jnp.where(rowi == LS - 1, last, 0.0))
                db = db + _dot((wg * xdtg).astype(BF16), dsgb, NT)
                ds[g] = eal[:, gs] * dsg + _dot(cg, ttg, TN)
                dxdt_parts.append(dxdt_g)
                db_parts.append(db)
                dc_parts.append(dc)
            dxdt = jnp.concatenate(dxdt_parts, axis=1)
            da_exp = jnp.concatenate(daexp_parts, axis=1)
            da = _dot2(da_exp, expand, NT) + da_rows - da_cols.T
            triub = jnp.where(tril, 1.0, 0.0).astype(BF16)
            ddta = _dot3_rhs(triub, da, TN)
            ddt = ddta * pre["a_neg"] + _dot2(dxdt * xs, expand, NT)
            da_ref[...] += jnp.sum(ddta * pre["dt"], axis=0, keepdims=True)
            ddt_raw = ddt * _sigmoid(pre["dtin"])
            ddt_ref[...] = ddt_raw
            ddtb_ref[...] += jnp.sum(ddt_raw, axis=0, keepdims=True)
            dxs = dexp_ref[...] * dy + dxdt * pre["dt_exp"]
            dxbc = jnp.concatenate([dxs] + db_parts + dc_parts, axis=1)
            sx, xc = pre["sx"], pre["xc"]
            dxc = dxbc * sx * (1.0 + xc * (1.0 - sx))
            taps = _rows_up(dxc.astype(BF16), ahead[...], (3, 2, 1)) + [dxc]
            xr = x_ref[...].astype(F32)
            dcb_ref[...] += jnp.sum(dxc, axis=0, keepdims=True)
            dcw_ref[...] += jnp.concatenate([jnp.sum(t * xr, axis=0, keepdims=True) for t in taps], axis=0)
            dxr = cw_ref[0:1, :] * taps[0]
            for k in range(1, SK):
                dxr = dxr + cw_ref[k:k + 1, :] * taps[k]
            stage[:, SI:] = dxr.astype(BF16)
            ahead[...] = dxc[0:H, :]

        dp_ref[...] = stage[...]

    vec = lambda w: pl.BlockSpec((1, w), lambda i, j: (0, 0))
    rev = lambda w, cb_: pl.BlockSpec((LS, w), lambda i, j: (nc - 1 - i, cb_))
    outs = pl.pallas_call(
        body, grid=(nc, NJ),
        in_specs=[rev(SI, 2), rev(SXBC, 2), rev(SXBC, 0),
                  rev(LANE, 0), rev(SI, 0),
                  pl.BlockSpec((1, SG, SN, SGW), lambda i, j: (nc - 1 - i, 0, 0, 0)),
                  rev(SI, 0), pl.BlockSpec(memory_space=pl.ANY),
                  pl.BlockSpec((SK, SXBC), lambda i, j: (0, 0)), vec(LANE), vec(LANE), vec(SI), vec(SI),
                  pl.BlockSpec((LANE, SI), lambda i, j: (0, 0))],
        out_specs=[rev(DP_SSM, 1), rev(LANE, 0),
                   pl.BlockSpec((SK, SXBC), lambda i, j: (0, 0)), vec(SXBC), vec(LANE), vec(LANE), vec(SI), vec(SI)],
        out_shape=[jax.ShapeDtypeStruct(dproj.shape, dproj.dtype), jax.ShapeDtypeStruct((T, LANE), F32),
                   jax.ShapeDtypeStruct((SK, SXBC), F32), jax.ShapeDtypeStruct((1, SXBC), F32),
                   jax.ShapeDtypeStruct((1, LANE), F32), jax.ShapeDtypeStruct((1, LANE), F32),
                   jax.ShapeDtypeStruct((1, SI), F32), jax.ShapeDtypeStruct((1, SI), F32)],
        scratch_shapes=[pltpu.VMEM((H, SXBC), F32),
                        pltpu.VMEM((SG, SN, SGW), F32), pltpu.VMEM((LS, SI + SXBC), BF16)],
        input_output_aliases={7: 0}, name="ssd_bwd",
        compiler_params=_params(("arbitrary", "arbitrary"), VMEM_CAP_MB))(
            proj, proj, xcs, dtraw, y, sprev, dyb, dproj, cw, dtb, alog, dexp, nw, _head_expansion())
    return outs


def _perm_ffn_cols(a):
    lead = a.shape[:-1]
    return a.reshape(lead + (2, 2, FT)).swapaxes(-3, -2).reshape(lead + (2 * DFF,))


def _perm_ffn_rows(a):
    return a.reshape((2, 2, FT) + a.shape[1:]).swapaxes(0, 1).reshape(a.shape)


def _pad_lanes(v, n=LANE):
    return jnp.pad(v, ((0, 0), (0, n - v.shape[-1])))


LATE = ["w_proj_a", "w_proj_b", "w_out", "ffn_w_up_t", "ffn_w_down"]
WGRAD = BF16
SMALL_BF16_FROM = 2 ** 16


class _NoExchange:
    def early_start(self):
        return None

    def early_weights(self, w, outs):
        return w

    def gather_start(self):
        return None

    def gather_pass_on(self, outs):
        return None

    def late_weights(self, w, outs):
        return w

    def late_to_sibling(self, grads):
        return None

    def reduce_late(self, outs):
        return None

    def w_in_to_sibling(self, grad_main, grad_dt):
        return None

    def reduce_w_in(self, outs):
        return None

    def reduced(self, late_outs, w_in_outs):
        pass


def _local_step(x, tgt, w, hooks=None):
    hooks = hooks or _NoExchange()

    def mm(*args, side=None, **kw):
        out = _matmul(*args, side=side, **kw)
        return out if side is not None else (out, [])

    mixw = w["mix_norm_w"][None, :]
    side = hooks.early_start()
    xn, got = _rms_fwd(x, mixw, name="mix_norm", side=side) if side else (_rms_fwd(x, mixw, name="mix_norm"), [])
    w = hooks.early_weights(w, got)
    win_t = w["w_in_t"]
    win_dt = jnp.pad(w["w_in_t"][PMAIN:], ((0, LANE - SH), (0, 0)))
    fcw = _perm_ffn_cols(w["ffn_conv_w"])
    fcb = _perm_ffn_cols(w["ffn_conv_b"][None, :])
    ffnw = w["ffn_norm_w"][None, :]
    finw = w["final_norm_w"][None, :]
    bst = w["gmlp_bs"].T
    scb = w["ssm_conv_b"][None, :]
    dtb = _pad_lanes(w["ssm_dt_bias"][None, :])
    alog = _pad_lanes(w["ssm_a_log"][None, :])
    dexp = jnp.repeat(w["ssm_d"], SP)[None, :]
    snw = w["ssm_norm_w"][None, :]

    proj, got = mm(xn, win_t, name="in_proj", out_dtype=BF16, tb=True, tn=3072, j_outer=True, b_rows=PMAIN,
                   side=hooks.gather_start())
    dtraw = _matmul(xn, win_dt, name="in_proj_dt", out_dtype=F32, tb=True)
    ya_pre, got = _gmlp_fwd(proj, w["gmlp_ln_w"], w["gmlp_ln_b"], w["gmlp_ws"], bst, side=hooks.gather_pass_on(got))
    w = hooks.late_weights(w, got)
    wup = _perm_ffn_rows(w["ffn_w_up_t"])
    yb_pre, y_ssd, sprev, ssm_xc = _ssd_fwd(proj, dtraw, w["ssm_conv_w"], scb, dtb, alog, dexp, snw)
    merged, ya, yb = _merge_fwd(ya_pre, yb_pre, proj, w["gate_bias"], w["w_proj_a"], w["w_proj_b"])
    h1 = _matmul(merged, w["w_out"], name="out_proj", out_dtype=F32, add=x)
    hn = _rms_fwd(h1, ffnw, name="ffn_norm")
    up = _matmul(hn, wup, name="ffn_up", out_dtype=BF16, tb=True, tn=2 * FT, j_outer=True)
    act, ffn_xc = _ffn_act_fwd(up, fcw, fcb)
    h2 = _matmul(act, w["ffn_w_down"], name="ffn_down", out_dtype=F32, tk=DFF, add=h1)

    loss_row, dh2, d_finw, dh2_b = _loss_head(h2, tgt, finw)
    dact = _matmul(dh2_b, w["ffn_w_down"], name="ffn_down_dx", out_dtype=BF16, tb=True, tn=DFF)
    d_wdown = _matmul(act, dh2_b, name="ffn_down_dw", out_dtype=WGRAD, ta=True, tm=FT, tk=2048)
    dup, d_fcw, d_fcb = _ffn_act_bwd(up, ffn_xc, dact, fcw)
    dhn = _matmul(dup, wup, name="ffn_up_dx", out_dtype=F32, tk=2 * FT)
    d_wup = _matmul(dup, hn, name="ffn_up_dw", out_dtype=WGRAD, ta=True, tm=FT, tk=2048,
                    o_row=lambda i: (i % 2) * 2 + i // 2)
    dh1, d_ffnw, dh1_b = _rms_bwd(h1, ffnw, dhn, dh2, name="ffn_norm_bwd", matmul_copy=True)
    dmerged = _matmul(dh1_b, w["w_out"], name="out_proj_dx", out_dtype=BF16, tb=True)
    d_wout = _matmul(merged, dh1_b, name="out_proj_dw", out_dtype=WGRAD, ta=True, tk=2048)
    dproj, dya, dyb, dya_pre, dyb_pre, d_gbias = _merge_bwd(dmerged, proj, w["gate_bias"], ya, yb,
                                                           w["w_proj_a"], w["w_proj_b"])
    d_wpa = _matmul(ya_pre, dya, name="proj_a_dw", out_dtype=WGRAD, ta=True, tk=2048)
    d_wpb = _matmul(yb_pre, dyb, name="proj_b_dw", out_dtype=WGRAD, ta=True, tk=2048)
    late = {"w_proj_a": d_wpa, "w_proj_b": d_wpb, "w_out": d_wout, "ffn_w_up_t": d_wup,
            "ffn_w_down": d_wdown}
    (dproj, d_lnw, d_lnb, d_ws, d_bst), got = _gmlp_bwd(proj, dya_pre, dproj, w["gmlp_ln_w"], w["gmlp_ln_b"],
                                                        w["gmlp_ws"], bst, side=hooks.late_to_sibling(late))
    dproj, ddt, d_scw, d_scb, d_dtb, d_a, d_dch, d_snw = _ssd_bwd(
        proj, ssm_xc, dtraw, y_ssd, sprev, dyb_pre, dproj, w["ssm_conv_w"], dtb, alog, dexp, snw)
    gap = (2 * D + 2 * GW, DP_GAP)
    d_win_main, late_outs = mm(dproj, xn, name="in_proj_dw", out_dtype=WGRAD, ta=True, a_gap=gap, tk=4096,
                               side=hooks.reduce_late(got))
    d_win_dt = _matmul(ddt, xn, name="in_proj_dt_dw", out_dtype=F32, ta=True)
    d_win_t = jnp.concatenate([d_win_main, d_win_dt[:SH]], axis=0)
    dxn, got = mm(ddt, win_dt, name="in_proj_dt_dx", out_dtype=F32,
                  side=hooks.w_in_to_sibling(d_win_main, d_win_dt[:SH]))
    dxn, w_in_outs = mm(dproj, win_t, name="in_proj_dx", out_dtype=F32, add=dxn, b_rows=PMAIN, a_gap=gap,
                        side=hooks.reduce_w_in(got))
    hooks.reduced(late_outs, w_in_outs)
    grad_x, d_mixw = _rms_bwd(x, mixw, dxn, dh1, name="mix_norm_bwd")

    a_neg = -jnp.exp(w["ssm_a_log"])
    grads = {
        "mix_norm_w": d_mixw[0],
        "w_in_t": d_win_t,
        "gate_bias": d_gbias,
        "gmlp_ln_w": d_lnw, "gmlp_ln_b": d_lnb, "gmlp_ws": d_ws, "gmlp_bs": d_bst[:, :GG].T,
        "ssm_conv_w": d_scw, "ssm_conv_b": d_scb[0],
        "ssm_dt_bias": d_dtb[0, :SH], "ssm_a_log": d_a[0, :SH] * a_neg,
        "ssm_d": d_dch.reshape(SH, SP).sum(axis=-1), "ssm_norm_w": d_snw[0],
        **late,
        "ffn_norm_w": d_ffnw[0],
        "ffn_conv_w": _perm_ffn_cols(d_fcw), "ffn_conv_b": _perm_ffn_cols(d_fcb)[0],
        "ffn_w_down": d_wdown, "final_norm_w": d_finw[0],
    }
    return loss_row, grad_x, grads


MESH = pl.DeviceIdType.MESH
HBM_SPEC = pl.BlockSpec(memory_space=pltpu.HBM)


def _axes():
    return lax.axis_index("x"), lax.axis_index("y"), lax.axis_index("c")


def _run_side(side, *, name):
    n_in, n_out = len(side.inputs), len(side.out_shapes)

    def body(*refs):
        copies = side.make(refs[:n_in], refs[n_in:n_in + n_out], *refs[-2:])
        for cp in copies:
            cp.start()
        for cp in copies:
            cp.wait()

    return pl.pallas_call(
        body, out_shape=list(side.out_shapes), in_specs=[HBM_SPEC] * n_in, out_specs=[HBM_SPEC] * n_out,
        scratch_shapes=[pltpu.SemaphoreType.DMA((side.nsem,)), pltpu.SemaphoreType.DMA((side.nsem,))],
        input_output_aliases={i: j for i, j in side.aliases}, name=name)(*side.inputs)


def _exchange(srcs, plan, *, name):
    na = len(srcs)
    n = len(plan(0, 0, 0))

    def body(*refs):
        src_refs, out_refs = refs[:na], refs[na:2 * na]
        send_sems, recv_sems = refs[2 * na:]
        x, y, c = _axes()
        copies = []
        for k, (slab, peer) in enumerate(plan(x, y, c)):
            for a in range(na):
                cp = pltpu.make_async_remote_copy(
                    src_ref=src_refs[a].at[slab], dst_ref=out_refs[a].at[k], send_sem=send_sems.at[n * a + k],
                    recv_sem=recv_sems.at[n * a + k], device_id=peer, device_id_type=MESH)
                cp.start()
                copies.append(cp)
        for cp in copies:
            cp.wait()

    return pl.pallas_call(
        body, out_shape=[jax.ShapeDtypeStruct((n,) + s.shape[1:], s.dtype) for s in srcs],
        in_specs=[HBM_SPEC] * na, out_specs=[HBM_SPEC] * na,
        scratch_shapes=[pltpu.SemaphoreType.DMA((n * na,)), pltpu.SemaphoreType.DMA((n * na,))], name=name)(*srcs)


def _to_sibling_plan(x, y, c):
    return [(2 * q + (1 - c), (x, y, 1 - c)) for q in range(4)]


def _to_chips_plan(x, y, c):
    q = 2 * x + y
    return [(q ^ 2, (1 - x, y, c)), (q ^ 1, (x, 1 - y, c)), (q ^ 3, (1 - x, 1 - y, c))]


def _row_tile(rows, row_bytes, budget=2 * 2 ** 20, align=2 * SUBLANE):
    if rows * row_bytes <= 2 * budget:
        return rows
    best = None
    for d in range(align, rows + 1, align):
        if rows % d == 0 and d * row_bytes <= budget:
            best = d
    return best or rows


def _pair_add(g, ra, c_idx, *, name):
    _, _, R, C = g.shape
    tr = _row_tile(R, C * 4, budget=3 * 2 ** 20)

    def body(c_ref, g_ref, ra_ref, o_ref):
        del c_ref
        o_ref[...] = (g_ref[0].astype(F32) + ra_ref[...].astype(F32)).astype(o_ref.dtype)

    return pl.pallas_call(
        body,
        grid_spec=pltpu.PrefetchScalarGridSpec(
            num_scalar_prefetch=1, grid=(4, R // tr),
            in_specs=[pl.BlockSpec((1, 1, tr, C), lambda q, r, cr: (q, cr[0], r, 0)),
                      pl.BlockSpec((1, tr, C), lambda q, r, cr: (q, r, 0))],
            out_specs=pl.BlockSpec((1, tr, C), lambda q, r, cr: (q, r, 0))),
        out_shape=jax.ShapeDtypeStruct((4, R, C), g.dtype), name=name,
        compiler_params=_params(("arbitrary", "arbitrary"), 24))(c_idx, g, ra)


def _grad_sum(p, rb, q_idx, *, name):
    _, R, C = p.shape
    tr = _row_tile(R, C * 4, budget=3 * 2 ** 20)

    def body(q_ref, p_ref, rb_ref, o_ref):
        del q_ref
        g = p_ref[0].astype(F32)
        for k in range(3):
            g = g + rb_ref[k].astype(F32)
        o_ref[...] = g

    return pl.pallas_call(
        body,
        grid_spec=pltpu.PrefetchScalarGridSpec(
            num_scalar_prefetch=1, grid=(R // tr,),
            in_specs=[pl.BlockSpec((1, tr, C), lambda r, qr: (qr[0], r, 0)),
                      pl.BlockSpec((3, tr, C), lambda r, qr: (0, r, 0))],
            out_specs=pl.BlockSpec((tr, C), lambda r, qr: (r, 0))),
        out_shape=jax.ShapeDtypeStruct((R, C), F32), name=name,
        compiler_params=_params(("arbitrary",), 40))(q_idx, p, rb)


def _adamw(g, w, m, v):
    m = ADAM_B1 * m + (1.0 - ADAM_B1) * g
    v = ADAM_B2 * v + (1.0 - ADAM_B2) * (g * g)
    m_hat = m / (1.0 - ADAM_B1 ** ADAM_STEP)
    v_hat = v / (1.0 - ADAM_B2 ** ADAM_STEP)
    delta = -ADAM_LR * (m_hat / (jnp.sqrt(v_hat) + ADAM_EPS) + ADAM_WD * w)
    return delta, m, v


def _adam(g, w, m, v, *, name):
    _, R, C = w.shape
    tr = _row_tile(R, C * 4, budget=2 ** 20, align=SUBLANE)

    def body(g_ref, w_ref, m_ref, v_ref, d_out, m_out, v_out):
        delta, mn, vn = _adamw(g_ref[...], w_ref[...], m_ref[...], v_ref[...])
        d_out[...] = delta
        m_out[...] = mn
        v_out[...] = vn

    row = pl.BlockSpec((1, tr, C), lambda r: (0, r, 0))
    o = jax.ShapeDtypeStruct((1, R, C), F32)
    return pl.pallas_call(
        body, grid=(R // tr,), in_specs=[row, row, row, row], out_specs=[row, row, row], out_shape=[o, o, o],
        name=name, compiler_params=_params(("arbitrary",), 32))(g, w, m, v)


def _vmem_specs(n):
    return [pl.BlockSpec(memory_space=pltpu.VMEM)] * n


def _pair_sum_many(mine, theirs, *, name):
    n = len(mine)

    def body(*refs):
        for a in range(n):
            refs[2 * n + a][...] = refs[a][...] + refs[n + a][0]

    return pl.pallas_call(
        body, out_shape=[jax.ShapeDtypeStruct(m.shape, m.dtype) for m in mine], in_specs=_vmem_specs(2 * n),
        out_specs=_vmem_specs(n), name=name)(*mine, *theirs)


def _chip_sum_many(own, recv, q_idx, *, name):
    n = len(own)

    def body(q_ref, *refs):
        q = q_ref[0]
        for a in range(n):
            mine, r = refs[a][...], refs[n + a]
            total = None
            for chip in range(4):
                e = q ^ chip
                term = jnp.where(e == 0, mine, jnp.where(e == 2, r[0], jnp.where(e == 1, r[1], r[2]))).astype(F32)
                total = term if total is None else total + term
            refs[2 * n + a][...] = total

    return pl.pallas_call(
        body, out_shape=[jax.ShapeDtypeStruct(m.shape, F32) for m in own],
        in_specs=[pl.BlockSpec(memory_space=pltpu.SMEM)] + _vmem_specs(2 * n), out_specs=_vmem_specs(n),
        name=name)(q_idx, *own, *recv)


def _adam_many(gs, ws, ms, vs, *, name):
    n = len(gs)

    def body(*refs):
        for a in range(n):
            delta, mn, vn = _adamw(*(refs[k * n + a][...] for k in range(4)))
            refs[4 * n + a][...] = delta
            refs[5 * n + a][...] = mn
            refs[6 * n + a][...] = vn

    shapes = [jax.ShapeDtypeStruct(w.shape, w.dtype) for w in ws]
    out = pl.pallas_call(body, out_shape=shapes * 3, in_specs=_vmem_specs(4 * n), out_specs=_vmem_specs(3 * n),
                         name=name)(*gs, *ws, *ms, *vs)
    return out[:n], out[n:2 * n], out[2 * n:]


WEIGHTS = ["mix_norm_w", "w_in", "gate_bias", "gmlp_ln_w", "gmlp_ln_b", "gmlp_ws", "gmlp_bs", "ssm_conv_w",
           "ssm_conv_b", "ssm_dt_bias", "ssm_a_log", "ssm_d", "ssm_norm_w", "w_proj_a", "w_proj_b", "w_out",
           "ffn_norm_w", "ffn_w_up", "ffn_conv_w", "ffn_conv_b", "ffn_w_down", "final_norm_w"]
SHARDED = {"w_in": ((D, IN_COLS), 1), "gate_bias": ((2, D), 1), "ssm_conv_w": ((SK, SXBC), 1),
           "w_proj_a": ((GW, D), 0), "w_proj_b": ((SI, D), 0), "w_out": ((D, D), 0),
           "ffn_w_up": ((D, 2 * DFF), 1), "ffn_conv_w": ((FK, 2 * DFF), 1), "ffn_w_down": ((DFF, D), 0)}
REPLICATED = {"mix_norm_w": (D,), "gmlp_ln_w": (GG, GD), "gmlp_ln_b": (GG, GD), "gmlp_ws": (GG, GB, GB),
              "gmlp_bs": (GG, GB), "ssm_conv_b": (SXBC,), "ssm_dt_bias": (SH,), "ssm_a_log": (SH,), "ssm_d": (SH,),
              "ssm_norm_w": (SI,), "ffn_norm_w": (D,), "ffn_conv_b": (2 * DFF,), "final_norm_w": (D,)}
REPL_ORDER = [n for n in WEIGHTS if n in REPLICATED]
BTILE = 2 * SUBLANE
WIN_R = IN_COLS // NDEV
WIN_P = WIN_R + BTILE - WIN_R % BTILE
WIN_A = [WIN_R * d // BTILE * BTILE for d in range(NDEV)]
assert all(WIN_A[d] + WIN_P >= WIN_R * (d + 1) for d in range(NDEV)) and WIN_A[-1] + WIN_P == IN_COLS
BIG = [("w_proj_a", GW // NDEV, False), ("w_proj_b", SI // NDEV, False), ("w_out", D // NDEV, False),
       ("ffn_w_up", 2 * DFF // NDEV, True), ("ffn_w_down", DFF // NDEV, False), ("w_in", WIN_P, True)]
VECTORS = ["gate_bias", "ssm_conv_w", "ffn_conv_w"]


def _round_up(n, k):
    return (n + k - 1) // k * k


BIG_OFF = {}
_off = 0
for _n, _r, _t in BIG:
    BIG_OFF[_n] = _off
    _off += _r
BIG_USED = _off
BIG_ROWS = _round_up(BIG_USED, 2 * SUBLANE)
assert all(BIG_OFF[n] % (2 * SUBLANE) == 0 for n, _, _ in BIG)
VEC_SHAPE = {n: (SHARDED[n][0][0], SHARDED[n][0][1] // NDEV) for n in VECTORS}


def _win_offset(dev):
    return WIN_R * dev - WIN_R * dev // BTILE * BTILE


def _pack_big(arrs, dtype, dev):
    parts = []
    for n, r, t in BIG:
        a = (arrs[n].T if t else arrs[n]).astype(dtype)
        if n == "w_in":
            a = lax.dynamic_update_slice(jnp.zeros((WIN_P, D), dtype), a, (_win_offset(dev), 0))
        parts.append(a)
    parts.append(jnp.zeros((BIG_ROWS - BIG_USED, D), dtype))
    return jnp.concatenate(parts, axis=0)


def _join_windows(win):
    parts = []
    for d in range(NDEV):
        lo = BTILE if WIN_A[d] % WIN_R else 0
        if lo:
            parts.append(win[d - 1, WIN_P - BTILE:] + win[d, :BTILE])
        hi = WIN_P - BTILE if d + 1 < NDEV and WIN_A[d + 1] < WIN_A[d] + WIN_P else WIN_P
        parts.append(win[d, lo:hi])
    return jnp.concatenate(parts, axis=0)


def _split_windows(main, last):
    assert WIN_A[-2] + WIN_P <= PMAIN
    wins = [main[a:a + WIN_P] for a in WIN_A[:-1]]
    return jnp.stack(wins + [jnp.concatenate([main[WIN_A[-1]:], last], axis=0)])


LATE_ROWS = BIG_OFF["w_in"]
assert LATE_ROWS + WIN_P == BIG_ROWS and BIG[-1][0] == "w_in"


def _remote(src, dst, send_sems, recv_sems, k, to):
    return pltpu.make_async_remote_copy(src_ref=src, dst_ref=dst, send_sem=send_sems.at[k], recv_sem=recv_sems.at[k],
                                        device_id=to, device_id_type=MESH)


class _Exchange:
    def __init__(self, early_shards, late_shard, c_idx):
        self.early_shards, self.late_shard, self.c_idx = early_shards, late_shard, c_idx

    @staticmethod
    def _gather_side(shards):
        def make(ins, outs, send_sems, recv_sems):
            x, y, c = _axes()
            peers = [(x, y, 1 - c), (1 - x, y, c), (x, 1 - y, c), (1 - x, 1 - y, c)]
            copies = []
            for a, (x_ref, out) in enumerate(zip(ins, outs)):
                mine = out.at[4 * x + 2 * y + c]
                copies += [_remote(x_ref, mine, send_sems, recv_sems, 5 * a + k, p) for k, p in enumerate(peers)]
                copies.append(pltpu.make_async_copy(x_ref, mine, send_sems.at[5 * a + 4]))
            return copies

        return _Side(shards, [jax.ShapeDtypeStruct((NDEV,) + s.shape, s.dtype) for s in shards], 5 * len(shards), make)

    @staticmethod
    def _pass_on_side(bufs):
        def make(ins, outs, send_sems, recv_sems):
            x, y, c = _axes()
            slots = [4 * px + 2 * py + c for px, py in [(1 - x, y), (x, 1 - y), (1 - x, 1 - y)]]
            return [_remote(src.at[s], dst.at[s], send_sems, recv_sems, 3 * a + k, (x, y, 1 - c))
                    for a, (src, dst) in enumerate(zip(ins, outs)) for k, s in enumerate(slots)]

        return _Side(bufs, [jax.ShapeDtypeStruct(b.shape, b.dtype) for b in bufs], 3 * len(bufs), make,
                     aliases=[(a, a) for a in range(len(bufs))])

    def early_start(self):
        return self._gather_side(self.early_shards)

    def early_weights(self, w, outs):
        bufs = _run_side(self._pass_on_side(outs), name="w_in_pass_on")
        w = dict(w)
        w["w_in_t"] = _join_windows(bufs[0])
        for n, a in zip(VECTORS, bufs[1:]):
            r, c = VEC_SHAPE[n]
            w[n] = a[:, 0].transpose(1, 0, 2).reshape(r, NDEV * c)
        return w

    def gather_start(self):
        return self._gather_side([self.late_shard])

    def gather_pass_on(self, outs):
        return self._pass_on_side(outs)

    def late_weights(self, w, outs):
        (buf,) = outs
        w = dict(w)
        for n, r, t in BIG[:-1]:
            w[n + "_t" if t else n] = buf[:, BIG_OFF[n]:BIG_OFF[n] + r].reshape(NDEV * r, D)
        return w

    @staticmethod
    def _plan_side(src, plan):
        n = len(plan(0, 0, 0))

        def make(ins, outs, send_sems, recv_sems):
            (s,), (dst,) = ins, outs
            return [_remote(s.at[slab], dst.at[k], send_sems, recv_sems, k, peer)
                    for k, (slab, peer) in enumerate(plan(*_axes()))]

        return _Side([src], [jax.ShapeDtypeStruct((n,) + src.shape[1:], src.dtype)], n, make)

    def _to_chips(self, send, sib, tag):
        sums = _pair_add(send.reshape((4, 2) + send.shape[1:]), sib, self.c_idx, name=tag + "_grad_pair_add")
        return sums, self._plan_side(sums, _to_chips_plan)

    def late_to_sibling(self, grads):
        send = jnp.concatenate([grads[n + "_t" if t else n].reshape(NDEV, r, D) for n, r, t in BIG[:-1]], axis=1)
        self.late_send = send.astype(BF16)
        return self._plan_side(self.late_send, _to_sibling_plan)

    def reduce_late(self, outs):
        self.late_sum, side = self._to_chips(self.late_send, outs[0], "late")
        return side

    def w_in_to_sibling(self, grad_main, grad_dt):
        self.w_in_send = _split_windows(grad_main.astype(BF16), grad_dt.astype(BF16))
        return self._plan_side(self.w_in_send, _to_sibling_plan)

    def reduce_w_in(self, outs):
        self.w_in_sum, side = self._to_chips(self.w_in_send, outs[0], "w_in")
        return side

    def reduced(self, late_outs, w_in_outs):
        (self.late_from_chips,), (self.w_in_from_chips,) = late_outs, w_in_outs


def kernel(x, mix_norm_w, w_in, gate_bias, gmlp_ln_w, gmlp_ln_b, gmlp_ws, gmlp_bs, ssm_conv_w, ssm_conv_b, ssm_dt_bias, ssm_a_log, ssm_d, ssm_norm_w, w_proj_a, w_proj_b, w_out, ffn_norm_w, ffn_w_up, ffn_conv_w, ffn_conv_b, ffn_w_down, final_norm_w, loss_target, m_mix_norm_w, m_w_in, m_gate_bias, m_gmlp_ln_w, m_gmlp_ln_b, m_gmlp_ws, m_gmlp_bs, m_ssm_conv_w, m_ssm_conv_b, m_ssm_dt_bias, m_ssm_a_log, m_ssm_d, m_ssm_norm_w, m_w_proj_a, m_w_proj_b, m_w_out, m_ffn_norm_w, m_ffn_w_up, m_ffn_conv_w, m_ffn_conv_b, m_ffn_w_down, m_final_norm_w, v_mix_norm_w, v_w_in, v_gate_bias, v_gmlp_ln_w, v_gmlp_ln_b, v_gmlp_ws, v_gmlp_bs, v_ssm_conv_w, v_ssm_conv_b, v_ssm_dt_bias, v_ssm_a_log, v_ssm_d, v_ssm_norm_w, v_w_proj_a, v_w_proj_b, v_w_out, v_ffn_norm_w, v_ffn_w_up, v_ffn_conv_w, v_ffn_conv_b, v_ffn_w_down, v_final_norm_w):
    given = dict(locals())
    wts = {n: given[n] for n in WEIGHTS}
    mom = {n: given["m_" + n] for n in WEIGHTS}
    var = {n: given["v_" + n] for n in WEIGHTS}
    xi, yi, ci = _axes()
    c_idx = jnp.reshape(ci, (1,)).astype(jnp.int32)
    q_idx = jnp.reshape(2 * xi + yi, (1,)).astype(jnp.int32)
    big_names = [n for n, _, _ in BIG]
    drop = lambda d, names: {n: d[n][0] for n in names}

    dev = 4 * xi + 2 * yi + ci
    packed = _pack_big(drop(wts, big_names), BF16, dev)
    full = {n: wts[n].reshape(REPLICATED[n]) for n in REPL_ORDER}

    hooks = _Exchange([packed[LATE_ROWS:]] + [wts[n] for n in VECTORS], packed[:LATE_ROWS], c_idx)
    loss_local, grad_x, grads = _local_step(x[0], loss_target[0], full, hooks)
    g_late = _grad_sum(hooks.late_sum, hooks.late_from_chips, q_idx, name="late_grad_sum")
    g_win = _grad_sum(hooks.w_in_sum, hooks.w_in_from_chips, q_idx, name="w_in_grad_sum")

    small = VECTORS + REPL_ORDER
    as_2d = lambda a: a if a.ndim >= 2 else a[None]
    part = [grads[n].reshape((1,) + SHARDED[n][0] if n in VECTORS else as_2d(wts[n]).shape) for n in small]
    part.append(loss_local)
    from_sibling = _exchange([p[None] for p in part], lambda x, y, c: [(0, (x, y, 1 - c))],
                             name="small_grads_to_sibling")
    chip_sums = _pair_sum_many(part, from_sibling, name="small_grad_pair_sum")
    chip_sums = [s.astype(BF16) if s.size >= SMALL_BF16_FROM else s for s in chip_sums]
    from_chips = _exchange([s[None] for s in chip_sums],
                           lambda x, y, c: [(0, (1 - x, y, c)), (0, (x, 1 - y, c)), (0, (1 - x, 1 - y, c))],
                           name="small_grads_to_chips")
    totals = _chip_sum_many(chip_sums, from_chips, q_idx, name="small_grad_chip_sum")
    g_small, loss = dict(zip(small, totals)), totals[-1][0, 0]
    for n in VECTORS:
        c = VEC_SHAPE[n][1]
        g_small[n] = lax.dynamic_slice_in_dim(g_small[n], dev * c, c, axis=2)

    outs = {}
    small_g = [g_small[n] for n in small]
    small_out = _adam_many(small_g, *[[as_2d(d[n]) for n in small] for d in (wts, mom, var)], name="adam_small")
    for i, n in enumerate(small):
        outs[n] = tuple(a[i].reshape(wts[n].shape) for a in (small_g,) + tuple(small_out))
    for n, r, t in BIG:
        if n == "w_in":
            g = lax.dynamic_slice(g_win, (_win_offset(dev), 0), (WIN_R, D))
        else:
            g = g_late[BIG_OFF[n]:BIG_OFF[n] + r]
        flip = (lambda a: a.transpose(0, 2, 1)) if t else (lambda a: a)
        g = g[None]
        new = _adam(g, flip(wts[n]), flip(mom[n]), flip(var[n]), name="adam_" + n)
        outs[n] = tuple(flip(a) for a in (g,) + tuple(new))
    return (loss, grad_x[None]) + tuple(outs[n][k] for k in range(4) for n in WEIGHTS)
```

```python
import functools

import jax
import jax.numpy as jnp
from jax import lax
from jax.experimental import pallas as pl
from jax.experimental.pallas import tpu as pltpu

F32 = jnp.float32
BF16 = jnp.bfloat16

D = 1024
EPS = 1e-5
GW = 1024
GB = 128
GG = 8
GD = 128
GCH = 64
SI = 2048
SH = 32
SP = 64
SG = 4
SN = 128
SGW = SI // SG
SK = 4
SXBC = SI + 2 * SG * SN
DFF = 2816
FK = 3
PMAIN = 2 * D + 2 * GW + SI + SXBC
IN_COLS = PMAIN + SH
DP_SSM = SI + SXBC
DP_GAP = (DP_SSM - (2 * D + 2 * GW) % DP_SSM) % DP_SSM
DP_COLS = 2 * D + 2 * GW + DP_GAP + DP_SSM
assert DP_GAP % D == 0 and (2 * D + 2 * GW) % D == 0
NDEV = 8
ADAM_LR, ADAM_B1, ADAM_B2, ADAM_EPS, ADAM_WD, ADAM_STEP = 0.001, 0.9, 0.999, 1e-08, 0.01, 10

LANE = 128
SUBLANE = 8
VMEM_MB_V7X = 64
VMEM_CAP_MB = VMEM_MB_V7X - 8

LS = 128
FT = DFF // 2

NN = (((1,), (0,)), ((), ()))
NT = (((1,), (1,)), ((), ()))
TN = (((0,), (0,)), ((), ()))


def _params(sem, vmem_mb):
    return pltpu.CompilerParams(dimension_semantics=sem,
                                vmem_limit_bytes=min(int(vmem_mb), VMEM_CAP_MB) * 1024 * 1024)


def _dot(a, b, dims=NN):
    return lax.dot_general(a, b, dims, preferred_element_type=F32)


def _sigmoid(x):
    return 1.0 / (1.0 + jnp.exp(-x))


def _split3(v):
    hi = v.astype(BF16)
    r = v - hi.astype(F32)
    mid = r.astype(BF16)
    lo = (r - mid.astype(F32)).astype(BF16)
    return hi, mid, lo


def _dot3(a_f32, b_bf16, dims):
    hi, mid, lo = _split3(a_f32)
    return _dot(hi, b_bf16, dims) + _dot(mid, b_bf16, dims) + _dot(lo, b_bf16, dims)


def _dot2(a_f32, b_bf16, dims):
    hi, mid, _ = _split3(a_f32)
    return _dot(hi, b_bf16, dims) + _dot(mid, b_bf16, dims)


def _dot3_rhs(a_bf16, b_f32, dims):
    hi, mid, lo = _split3(b_f32)
    return _dot(a_bf16, hi, dims) + _dot(a_bf16, mid, dims) + _dot(a_bf16, lo, dims)


def _matmul(a, b, *, name, out_dtype, ta=False, tb=False, tm=1024, tn=1024, tk=1024, add=None,
            j_outer=False, b_rows=None, a_gap=None, o_row=None, side=None):
    assert not (o_row and add is not None)
    gap0, gapw = a_gap or (0, 0)
    if ta:
        K, M = a.shape
        M -= gapw
    else:
        M, K = a.shape
        K -= gapw
    if tb:
        N, K2 = b.shape
        N = b_rows or N
    else:
        K2, N = b.shape
        K2 = b_rows or K2
    assert K == K2, (a.shape, b.shape, ta, tb)
    tm, tn, tk = min(tm, M), min(tn, N), min(tk, K)
    assert M % tm == 0 and N % tn == 0 and K % tk == 0, (M, N, K, tm, tn, tk)
    nk = K // tk
    dims = (((0 if ta else 1,), (1 if tb else 0,)), ((), ()))
    has_add = add is not None
    n_in = 3 if has_add else 2
    s_in = len(side.inputs) if side else 0
    s_out = len(side.out_shapes) if side else 0
    grid = (N // tn, M // tm, nk) if j_outer else (M // tm, N // tn, nk)

    def body(*refs):
        a_ref, b_ref = refs[:2]
        add_ref = refs[2] if has_add else None
        o_ref = refs[n_in + s_in]
        if side:
            side_refs = (refs[n_in:n_in + s_in], refs[n_in + s_in + 1:n_in + s_in + 1 + s_out]) + tuple(refs[-2:])
            ids = [pl.program_id(d) for d in range(3)]
            first = functools.reduce(jnp.logical_and, [i == 0 for i in ids])
            last = functools.reduce(jnp.logical_and, [i == g - 1 for i, g in zip(ids, grid)])

            @pl.when(first)
            def _():
                for cp in side.make(*side_refs):
                    cp.start()

            @pl.when(last)
            def _():
                for cp in side.make(*side_refs):
                    cp.wait()

        p = lax.dot_general(a_ref[...].astype(BF16), b_ref[...].astype(BF16), dims,
                            preferred_element_type=F32)

        def finish(acc):
            if has_add:
                acc = acc + add_ref[...].astype(F32)
            o_ref[...] = acc.astype(o_ref.dtype)

        if nk == 1:
            finish(p)
        else:
            acc_ref = refs[n_in + s_in + 1 + s_out]
            k = pl.program_id(2)

            @pl.when(k == 0)
            def _():
                acc_ref[...] = p

            @pl.when(jnp.logical_and(k > 0, k < nk - 1))
            def _():
                acc_ref[...] += p

            @pl.when(k == nk - 1)
            def _():
                finish(acc_ref[...] + p)

    if j_outer:
        ij = lambda g0, g1: (g1, g0)
    else:
        ij = lambda g0, g1: (g0, g1)

    ta_col = tm if ta else tk
    assert gap0 % ta_col == 0 and gapw % ta_col == 0, (a_gap, ta_col)

    def a_map(g0, g1, k):
        i, _ = ij(g0, g1)
        col = i if ta else k
        col = col + jnp.where(col >= gap0 // ta_col, gapw // ta_col, 0) if gapw else col
        return (k, col) if ta else (i, col)

    def b_map(g0, g1, k):
        _, j = ij(g0, g1)
        return (j, k) if tb else (k, j)

    def o_map(g0, g1, k):
        i, j = ij(g0, g1)
        return (o_row(i) if o_row else i, j)

    in_specs = [pl.BlockSpec((tk, tm) if ta else (tm, tk), a_map),
                pl.BlockSpec((tn, tk) if tb else (tk, tn), b_map)]
    args = [a, b]
    if has_add:
        in_specs.append(pl.BlockSpec((tm, tn), o_map))
        args.append(add)
    scratch = [pltpu.VMEM((tm, tn), F32)] if nk > 1 else []
    osz = jnp.dtype(out_dtype).itemsize
    est = (2 * (tm * tk * a.dtype.itemsize + tk * tn * b.dtype.itemsize) + 2 * tm * tn * osz
           + (2 * tm * tn * add.dtype.itemsize if has_add else 0)
           + 3 * tm * tn * 4 + 2 * (tm * tk + tk * tn)) / 2 ** 20 + 4
    out_specs = [pl.BlockSpec((tm, tn), o_map)]
    out_shape = [jax.ShapeDtypeStruct((M, N), out_dtype)]
    aliases = {}
    if side:
        hbm = pl.BlockSpec(memory_space=pltpu.HBM)
        in_specs += [hbm] * s_in
        args += list(side.inputs)
        out_specs += [hbm] * s_out
        out_shape += list(side.out_shapes)
        scratch += [pltpu.SemaphoreType.DMA((side.nsem,)), pltpu.SemaphoreType.DMA((side.nsem,))]
        aliases = {n_in + i: 1 + j for i, j in side.aliases}
    outs = pl.pallas_call(
        body, grid=grid, in_specs=in_specs, out_specs=out_specs, out_shape=out_shape, scratch_shapes=scratch,
        input_output_aliases=aliases, name=name,
        compiler_params=_params(("arbitrary", "arbitrary", "arbitrary"), est))(*args)
    return (outs[0], list(outs[1:])) if side else outs[0]


class _Side:
    def __init__(self, inputs, out_shapes, nsem, make, aliases=()):
        self.inputs, self.out_shapes, self.nsem, self.make, self.aliases = inputs, out_shapes, nsem, make, aliases


def _rms_fwd(x, w, *, name, side=None):
    T = x.shape[0]
    tm = min(512, T)

    def body(x_ref, w_ref, o_ref):
        xv = x_ref[...]
        r = lax.rsqrt(jnp.mean(xv * xv, axis=-1, keepdims=True) + EPS)
        o_ref[...] = (xv * r * w_ref[...]).astype(BF16)

    x_in, x_args, x_out, x_shapes, x_scratch = _ride_args(side)
    outs = pl.pallas_call(
        _ride(body, 2, 1, side, T // tm), grid=(T // tm,),
        in_specs=[pl.BlockSpec((tm, D), lambda i: (i, 0)), pl.BlockSpec((1, D), lambda i: (0, 0))] + x_in,
        out_specs=[pl.BlockSpec((tm, D), lambda i: (i, 0))] + x_out,
        out_shape=[jax.ShapeDtypeStruct((T, D), BF16)] + x_shapes, scratch_shapes=x_scratch, name=name,
        compiler_params=_params(("arbitrary",), 24))(x, w, *x_args)
    return (outs[0], list(outs[1:])) if side else outs[0]


def _rms_bwd(x, w, dy, dres, *, name):
    T = x.shape[0]
    tm = min(512, T)

    def body(x_ref, w_ref, dy_ref, dres_ref, dx_ref, dw_ref):
        xv = x_ref[...]
        r = lax.rsqrt(jnp.mean(xv * xv, axis=-1, keepdims=True) + EPS)
        xhat = xv * r
        dyv = dy_ref[...].astype(F32)
        g = dyv * w_ref[...]
        dx_ref[...] = dres_ref[...] + r * (g - xhat * jnp.mean(g * xhat, axis=-1, keepdims=True))
        part = jnp.sum(dyv * xhat, axis=0, keepdims=True)

        @pl.when(pl.program_id(0) == 0)
        def _():
            dw_ref[...] = part

        @pl.when(pl.program_id(0) > 0)
        def _():
            dw_ref[...] += part

    row = pl.BlockSpec((tm, D), lambda i: (i, 0))
    vec = pl.BlockSpec((1, D), lambda i: (0, 0))
    return pl.pallas_call(
        body, grid=(T // tm,), in_specs=[row, vec, row, row], out_specs=[row, vec],
        out_shape=[jax.ShapeDtypeStruct((T, D), F32), jax.ShapeDtypeStruct((1, D), F32)], name=name,
        compiler_params=_params(("arbitrary",), 32))(x, w, dy, dres)


def _loss_head(h, tgt, w):
    T = h.shape[0]
    tm = min(512, T)

    def body(h_ref, t_ref, w_ref, loss_ref, dh_ref, dw_ref):
        hv = h_ref[...]
        r = lax.rsqrt(jnp.mean(hv * hv, axis=-1, keepdims=True) + EPS)
        xhat = hv * r
        wv = w_ref[...]
        err = xhat * wv - t_ref[...]
        lpart = 0.5 * jnp.sum(jnp.mean(err * err, axis=-1, keepdims=True), axis=0, keepdims=True)
        dy = err * (1.0 / D)
        g = dy * wv
        dh_ref[...] = r * (g - xhat * jnp.mean(g * xhat, axis=-1, keepdims=True))
        wpart = jnp.sum(dy * xhat, axis=0, keepdims=True)
        lrow = jnp.broadcast_to(lpart, (1, LANE))

        @pl.when(pl.program_id(0) == 0)
        def _():
            dw_ref[...] = wpart
            loss_ref[...] = lrow

        @pl.when(pl.program_id(0) > 0)
        def _():
            dw_ref[...] += wpart
            loss_ref[...] += lrow

    row = pl.BlockSpec((tm, D), lambda i: (i, 0))
    vec = pl.BlockSpec((1, D), lambda i: (0, 0))
    return pl.pallas_call(
        body, grid=(T // tm,), in_specs=[row, row, vec],
        out_specs=[pl.BlockSpec((1, LANE), lambda i: (0, 0)), row, vec],
        out_shape=[jax.ShapeDtypeStruct((1, LANE), F32), jax.ShapeDtypeStruct((T, D), F32),
                   jax.ShapeDtypeStruct((1, D), F32)], name="loss_head",
        compiler_params=_params(("arbitrary",), 32))(h, tgt, w)


_GELU_C = 0.7978845608028654
_GELU_A = 0.044715


def _gelu(x, with_grad=False):
    x2 = x * x
    cx = _GELU_C * x
    t = jnp.tanh(cx * (1.0 + _GELU_A * x2))
    h = 0.5 * (1.0 + t)
    if not with_grad:
        return x * h
    return x * h, h + 0.5 * cx * (1.0 - t * t) * (1.0 + 3.0 * _GELU_A * x2)


def _gmlp_mask():
    r = lax.broadcasted_iota(jnp.int32, (GB, GB), 0) // GCH
    c = lax.broadcasted_iota(jnp.int32, (GB, GB), 1) // GCH
    return c <= r


def _ride(body, n_in, n_out, side, nsteps):
    if not side:
        return body
    s_in, s_out = len(side.inputs), len(side.out_shapes)

    def outer(*refs):
        ins, side_ins = refs[:n_in], refs[n_in:n_in + s_in]
        outs = refs[n_in + s_in:n_in + s_in + n_out]
        side_refs = (side_ins, refs[n_in + s_in + n_out:n_in + s_in + n_out + s_out]) + tuple(refs[-2:])

        @pl.when(pl.program_id(0) == 0)
        def _():
            for cp in side.make(*side_refs):
                cp.start()

        @pl.when(pl.program_id(0) == nsteps - 1)
        def _():
            for cp in side.make(*side_refs):
                cp.wait()

        body(*ins, *outs)

    return outer


def _ride_args(side):
    if not side:
        return [], [], [], [], []
    hbm = pl.BlockSpec(memory_space=pltpu.HBM)
    sems = [pltpu.SemaphoreType.DMA((side.nsem,)), pltpu.SemaphoreType.DMA((side.nsem,))]
    return [hbm] * len(side.inputs), list(side.inputs), [hbm] * len(side.out_shapes), list(side.out_shapes), sems


def _gmlp_fwd(proj, lnw, lnb, ws, bst, side=None):
    T = proj.shape[0]
    tm = min(512, T)
    nblk = tm // GB

    def body(u_ref, v_ref, lnw_ref, lnb_ref, ws_ref, bst_ref, o_ref):
        mask = _gmlp_mask()
        u = _gelu(u_ref[...].astype(F32))
        v = _gelu(v_ref[...].astype(F32))
        for g in range(GG):
            cs = slice(g * GD, (g + 1) * GD)
            vg = v[:, cs]
            mu = jnp.mean(vg, axis=-1, keepdims=True)
            vc = vg - mu
            var = jnp.mean(vc * vc, axis=-1, keepdims=True)
            vn = (vc * lax.rsqrt(var + EPS) * lnw_ref[g:g + 1, :] + lnb_ref[g:g + 1, :]).astype(BF16)
            wsg = jnp.where(mask, ws_ref[g], 0.0).astype(BF16)
            bcol = bst_ref[:, g:g + 1]
            for blk in range(nblk):
                rs = slice(blk * GB, (blk + 1) * GB)
                sv = _dot(wsg, vn[rs, :]) + bcol
                o_ref[rs, cs] = (u[rs, cs] * sv).astype(BF16)

    full = lambda shape: pl.BlockSpec(shape, lambda i: tuple(0 for _ in shape))
    x_in, x_args, x_out, x_shapes, x_scratch = _ride_args(side)
    outs = pl.pallas_call(
        _ride(body, 6, 1, side, T // tm), grid=(T // tm,),
        in_specs=[pl.BlockSpec((tm, GW), lambda i: (i, 2)), pl.BlockSpec((tm, GW), lambda i: (i, 3)),
                  full((GG, GD)), full((GG, GD)), full((GG, GB, GB)), full((GB, GG))] + x_in,
        out_specs=[pl.BlockSpec((tm, GW), lambda i: (i, 0))] + x_out,
        out_shape=[jax.ShapeDtypeStruct((T, GW), BF16)] + x_shapes, scratch_shapes=x_scratch,
        input_output_aliases={6 + i: 1 + j for i, j in (side.aliases if side else ())}, name="gmlp_fwd",
        compiler_params=_params(("arbitrary",), 40))(proj, proj, lnw, lnb, ws, bst, *x_args)
    return outs[0], list(outs[1:])


def _gmlp_bwd(proj, dya, dproj, lnw, lnb, ws, bst, side=None):
    T = proj.shape[0]
    tm = min(512, T)
    nblk = tm // GB
    n_in, n_out = 8, 5

    def body(u_ref, v_ref, dya_ref, dproj_in, lnw_ref, lnb_ref, ws_ref, bst_ref,
             dz_ref, dlnw_ref, dlnb_ref, dws_ref, dbst_ref):
        del dproj_in
        first = pl.program_id(0) == 0

        @pl.when(first)
        def _():
            dlnw_ref[...] = jnp.zeros_like(dlnw_ref)
            dlnb_ref[...] = jnp.zeros_like(dlnb_ref)
            dws_ref[...] = jnp.zeros_like(dws_ref)
            dbst_ref[...] = jnp.zeros_like(dbst_ref)

        mask = _gmlp_mask()
        lane = lax.broadcasted_iota(jnp.int32, (GB, LANE), 1)
        ur = u_ref[...].astype(F32)
        vr = v_ref[...].astype(F32)
        u, gu = _gelu(ur, with_grad=True)
        v, gv = _gelu(vr, with_grad=True)
        dy = dya_ref[...].astype(F32)
        dbst = jnp.zeros((GB, LANE), F32)
        dlnw_rows, dlnb_rows = [], []
        for g in range(GG):
            cs = slice(g * GD, (g + 1) * GD)
            vg = v[:, cs]
            mu = jnp.mean(vg, axis=-1, keepdims=True)
            vc = vg - mu
            var = jnp.mean(vc * vc, axis=-1, keepdims=True)
            rstd = lax.rsqrt(var + EPS)
            xhat = vc * rstd
            lw = lnw_ref[g:g + 1, :]
            vn = (xhat * lw + lnb_ref[g:g + 1, :]).astype(BF16)
            wsg = jnp.where(mask, ws_ref[g], 0.0).astype(BF16)
            bcol = bst_ref[:, g:g + 1]
            dyg = dy[:, cs]
            ug = u[:, cs]
            dsv = dyg * ug
            dsv_b = dsv.astype(BF16)
            dws_g = jnp.zeros((GB, GB), F32)
            bsum = jnp.zeros((GB, 1), F32)
            dvn_parts = []
            for blk in range(nblk):
                rs = slice(blk * GB, (blk + 1) * GB)
                sv = _dot(wsg, vn[rs, :]) + bcol
                dz_ref[rs, cs] = (dyg[rs, :] * sv * gu[rs, cs]).astype(BF16)
                dws_g = dws_g + _dot(dsv_b[rs, :], vn[rs, :], NT)
                bsum = bsum + jnp.sum(dsv[rs, :], axis=-1, keepdims=True)
                dvn_parts.append(_dot(wsg, dsv_b[rs, :], TN))
            dvn = jnp.concatenate(dvn_parts, axis=0)
            dws_ref[g] += jnp.where(mask, dws_g, 0.0)
            dbst = dbst + jnp.where(lane == g, bsum, 0.0)
            dlnw_rows.append(jnp.sum(dvn * xhat, axis=0, keepdims=True))
            dlnb_rows.append(jnp.sum(dvn, axis=0, keepdims=True))
            dxh = dvn * lw
            dvg = rstd * (dxh - jnp.mean(dxh, axis=-1, keepdims=True)
                          - xhat * jnp.mean(dxh * xhat, axis=-1, keepdims=True))
            dz_ref[:, GW + g * GD:GW + (g + 1) * GD] = (dvg * gv[:, cs]).astype(BF16)
        dlnw_ref[...] += jnp.concatenate(dlnw_rows, axis=0)
        dlnb_ref[...] += jnp.concatenate(dlnb_rows, axis=0)
        dbst_ref[...] += dbst

    full = lambda shape: pl.BlockSpec(shape, lambda i: tuple(0 for _ in shape))
    x_in, x_args, x_out, x_shapes, x_scratch = _ride_args(side)
    outs = pl.pallas_call(
        _ride(body, n_in, n_out, side, T // tm), grid=(T // tm,),
        in_specs=[pl.BlockSpec((tm, GW), lambda i: (i, 2)), pl.BlockSpec((tm, GW), lambda i: (i, 3)),
                  pl.BlockSpec((tm, GW), lambda i: (i, 0)), pl.BlockSpec(memory_space=pl.ANY),
                  full((GG, GD)), full((GG, GD)), full((GG, GB, GB)), full((GB, GG))] + x_in,
        out_specs=[pl.BlockSpec((tm, 2 * GW), lambda i: (i, 1)), full((GG, GD)), full((GG, GD)),
                   full((GG, GB, GB)), full((GB, LANE))] + x_out,
        out_shape=[jax.ShapeDtypeStruct(dproj.shape, dproj.dtype), jax.ShapeDtypeStruct((GG, GD), F32),
                   jax.ShapeDtypeStruct((GG, GD), F32), jax.ShapeDtypeStruct((GG, GB, GB), F32),
                   jax.ShapeDtypeStruct((GB, LANE), F32)] + x_shapes,
        scratch_shapes=x_scratch, input_output_aliases={3: 0}, name="gmlp_bwd",
        compiler_params=_params(("arbitrary",), 48))(proj, proj, dya, dproj, lnw, lnb, ws, bst, *x_args)
    return tuple(outs[:n_out]), list(outs[n_out:])


def _merge_fwd(ya_pre, yb_pre, proj, bias, wpa, wpb):
    T = proj.shape[0]
    tm = min(512, T)

    def body(ya_ref, yb_ref, g_ref, b_ref, wpa_ref, wpb_ref, m_ref, oa_ref, ob_ref):
        ya = _dot(ya_ref[...], wpa_ref[...])
        yb = _dot(yb_ref[...], wpb_ref[...])
        g = g_ref[...].astype(F32)
        sa = _sigmoid(g[:, :D] + b_ref[0:1, :])
        sb = _sigmoid(g[:, D:] + b_ref[1:2, :])
        m_ref[...] = (sa * ya + sb * yb).astype(BF16)
        oa_ref[...] = ya.astype(BF16)
        ob_ref[...] = yb.astype(BF16)

    row = lambda w: pl.BlockSpec((tm, w), lambda i: (i, 0))
    full = lambda shape: pl.BlockSpec(shape, lambda i: tuple(0 for _ in shape))
    o = jax.ShapeDtypeStruct((T, D), BF16)
    return pl.pallas_call(
        body, grid=(T // tm,),
        in_specs=[row(GW), row(SI), row(2 * D), full((2, D)), full((GW, D)), full((SI, D))],
        out_specs=[row(D), row(D), row(D)], out_shape=[o, o, o], name="merge_fwd",
        compiler_params=_params(("arbitrary",), 40))(ya_pre, yb_pre, proj, bias, wpa, wpb)


def _merge_bwd(dm, proj, bias, ya, yb, wpa, wpb):
    T = proj.shape[0]
    tm = min(512, T)

    def body(dm_ref, g_ref, b_ref, ya_ref, yb_ref, wpa_ref, wpb_ref,
             dg_ref, dya_ref, dyb_ref, dpa_ref, dpb_ref, db_ref):
        dmv = dm_ref[...].astype(F32)
        g = g_ref[...].astype(F32)
        sa = _sigmoid(g[:, :D] + b_ref[0:1, :])
        sb = _sigmoid(g[:, D:] + b_ref[1:2, :])
        dya = (dmv * sa).astype(BF16)
        dyb = (dmv * sb).astype(BF16)
        dga = dmv * ya_ref[...].astype(F32) * sa * (1.0 - sa)
        dgb = dmv * yb_ref[...].astype(F32) * sb * (1.0 - sb)
        dg_ref[:, :D] = dga.astype(BF16)
        dg_ref[:, D:] = dgb.astype(BF16)
        dya_ref[...] = dya
        dyb_ref[...] = dyb
        dpa_ref[...] = _dot(dya, wpa_ref[...], NT).astype(BF16)
        dpb_ref[...] = _dot(dyb, wpb_ref[...], NT).astype(BF16)
        part = jnp.concatenate([jnp.sum(dga, axis=0, keepdims=True), jnp.sum(dgb, axis=0, keepdims=True)], axis=0)

        @pl.when(pl.program_id(0) == 0)
        def _():
            db_ref[...] = part

        @pl.when(pl.program_id(0) > 0)
        def _():
            db_ref[...] += part

    row = lambda w: pl.BlockSpec((tm, w), lambda i: (i, 0))
    full = lambda shape: pl.BlockSpec(shape, lambda i: tuple(0 for _ in shape))
    o = lambda w: jax.ShapeDtypeStruct((T, w), BF16)
    return pl.pallas_call(
        body, grid=(T // tm,),
        in_specs=[row(D), row(2 * D), full((2, D)), row(D), row(D), full((GW, D)), full((SI, D))],
        out_specs=[row(2 * D), row(D), row(D), row(GW), row(SI), full((2, D))],
        out_shape=[o(DP_COLS), o(D), o(D), o(GW), o(SI), jax.ShapeDtypeStruct((2, D), F32)], name="merge_bwd",
        compiler_params=_params(("arbitrary",), 48))(dm, proj, bias, ya, yb, wpa, wpb)


RB = 128


def _shift_matrix(j):
    r = lax.broadcasted_iota(jnp.int32, (RB, RB), 0)
    c = lax.broadcasted_iota(jnp.int32, (RB, RB), 1)
    return jnp.where(c == r - j, 1.0, 0.0).astype(BF16)


def _rows_down(xb, before, shifts):
    H = SUBLANE
    mats = [_shift_matrix(j) for j in shifts]
    outs = [[] for _ in shifts]
    for b in range(xb.shape[0] // RB):
        blk = xb[b * RB:(b + 1) * RB]
        edge = jnp.concatenate([before, blk[:2 * H].astype(F32)[:H]], axis=0)
        for i, j in enumerate(shifts):
            outs[i] += [edge[H - j:2 * H - j], _dot(mats[i], blk)[H:]]
        before = blk[RB - 2 * H:].astype(F32)[H:]
    return [jnp.concatenate(o, axis=0) for o in outs]


def _rows_up(xb, after, shifts):
    H = SUBLANE
    nb = xb.shape[0] // RB
    mats = [_shift_matrix(-j) for j in shifts]
    outs = [[] for _ in shifts]
    for b in range(nb):
        blk = xb[b * RB:(b + 1) * RB]
        nxt = xb[(b + 1) * RB:(b + 1) * RB + 2 * H].astype(F32)[:H] if b + 1 < nb else after
        edge = jnp.concatenate([blk[RB - 2 * H:].astype(F32)[H:], nxt], axis=0)
        for i, j in enumerate(shifts):
            outs[i] += [_dot(mats[i], blk)[:RB - H], edge[j:H + j]]
    return [jnp.concatenate(o, axis=0) for o in outs]


def _ffn_act_fwd(up, cw, cb):
    T = up.shape[0]
    tm = min(256, T)
    H = SUBLANE

    def body(up_ref, cw_ref, cb_ref, o_ref, xc_ref, halo):
        @pl.when(pl.program_id(1) == 0)
        def _():
            halo[...] = jnp.zeros_like(halo)

        xb = up_ref[...]
        x2, x1 = _rows_down(xb, halo[...], (2, 1))
        xc = cb_ref[...] + cw_ref[0:1, :] * x2 + cw_ref[1:2, :] * x1 + cw_ref[2:3, :] * xb.astype(F32)
        xc_ref[...] = xc.astype(BF16)
        gate = xc[:, :FT]
        o_ref[...] = (gate * _sigmoid(gate) * xc[:, FT:]).astype(BF16)
        halo[...] = xb[tm - 2 * H:].astype(F32)[H:]

    tile = pl.BlockSpec((tm, 2 * FT), lambda j, i: (i, j))
    return pl.pallas_call(
        body, grid=(2, T // tm),
        in_specs=[tile, pl.BlockSpec((FK, 2 * FT), lambda j, i: (0, j)), pl.BlockSpec((1, 2 * FT), lambda j, i: (0, j))],
        out_specs=[pl.BlockSpec((tm, FT), lambda j, i: (i, j)), tile],
        out_shape=[jax.ShapeDtypeStruct((T, DFF), BF16), jax.ShapeDtypeStruct((T, 2 * DFF), BF16)],
        scratch_shapes=[pltpu.VMEM((H, 2 * FT), F32)], name="ffn_act_fwd",
        compiler_params=_params(("arbitrary", "arbitrary"), 48))(up, cw, cb)


def _ffn_act_bwd(up, xc, dact, cw):
    T = up.shape[0]
    tm = min(256, T)
    nt = T // tm
    H = SUBLANE

    def body(up_ref, xc_ref, da_ref, cw_ref, dup_ref, dcw_ref, dcb_ref, ahead):
        @pl.when(pl.program_id(1) == 0)
        def _():
            ahead[...] = jnp.zeros_like(ahead)
            dcw_ref[...] = jnp.zeros_like(dcw_ref)
            dcb_ref[...] = jnp.zeros_like(dcb_ref)

        xcv = xc_ref[...].astype(F32)
        gate, val = xcv[:, :FT], xcv[:, FT:]
        sg = _sigmoid(gate)
        dav = da_ref[...].astype(F32)
        dgate = dav * val * sg * (1.0 + gate * (1.0 - sg))
        dval = dav * gate * sg
        dxc = jnp.concatenate([dgate, dval], axis=1)
        d1, d2 = _rows_up(dxc.astype(BF16), ahead[...], (1, 2))
        x = up_ref[...].astype(F32)
        dcb_ref[...] += jnp.sum(dxc, axis=0, keepdims=True)
        dcw_ref[...] += jnp.concatenate([jnp.sum(d * x, axis=0, keepdims=True) for d in (d2, d1, dxc)], axis=0)
        dup_ref[...] = (cw_ref[2:3, :] * dxc + cw_ref[1:2, :] * d1 + cw_ref[0:1, :] * d2).astype(BF16)
        ahead[...] = dxc[0:H, :]

    tile = pl.BlockSpec((tm, 2 * FT), lambda j, i: (nt - 1 - i, j))
    return pl.pallas_call(
        body, grid=(2, nt),
        in_specs=[tile, tile, pl.BlockSpec((tm, FT), lambda j, i: (nt - 1 - i, j)),
                  pl.BlockSpec((FK, 2 * FT), lambda j, i: (0, j))],
        out_specs=[tile, pl.BlockSpec((FK, 2 * FT), lambda j, i: (0, j)), pl.BlockSpec((1, 2 * FT), lambda j, i: (0, j))],
        out_shape=[jax.ShapeDtypeStruct((T, 2 * DFF), BF16), jax.ShapeDtypeStruct((FK, 2 * DFF), F32),
                   jax.ShapeDtypeStruct((1, 2 * DFF), F32)],
        scratch_shapes=[pltpu.VMEM((H, 2 * FT), F32)], name="ffn_act_bwd",
        compiler_params=_params(("arbitrary", "arbitrary"), 56))(up, xc, dact, cw)


def _softplus(x):
    e = jnp.exp(-jnp.abs(x))
    return jnp.maximum(x, 0.0) + jnp.where(e < 1e-4, e * (1.0 - 0.5 * e), jnp.log(1.0 + e))


def _ssd_tril():
    li = lax.broadcasted_iota(jnp.int32, (LS, LS), 0)
    si = lax.broadcasted_iota(jnp.int32, (LS, LS), 1)
    return si <= li


def _head_expansion():
    hh = lax.broadcasted_iota(jnp.int32, (LANE, SI), 0)
    cc = lax.broadcasted_iota(jnp.int32, (LANE, SI), 1) // SP
    return jnp.where(hh == cc, 1.0, 0.0).astype(BF16)


def _ssd_pre(xc, dt_ref, dtb_ref, alog_ref, tril, expand):
    sx = _sigmoid(xc)
    xbc = xc * sx
    xs, bm, cm = xbc[:, :SI], xbc[:, SI:SI + SG * SN], xbc[:, SI + SG * SN:]
    dtin = dt_ref[...] + dtb_ref[...]
    dt = _softplus(dtin)
    a_neg = -jnp.exp(alog_ref[...])
    dta = dt * a_neg
    trilb = jnp.where(tril, 1.0, 0.0).astype(BF16)
    a = _dot3_rhs(trilb, dta, NN)
    a_exp = _dot3(a, expand, NN)
    dt_exp = _dot2(dt, expand, NN)
    xdt = xs * dt_exp
    a_last = a_exp[LS - 1:LS, :]
    return dict(xc=xc, sx=sx, xs=xs, bm=bm, cm=cm, dtin=dt_ref[...] + dtb_ref[...], dt=dt, a_neg=a_neg,
                a=a, a_t=a.T, a_exp=a_exp, dt_exp=dt_exp, xdt=xdt, ea=jnp.exp(a_exp),
                w=jnp.exp(a_last - a_exp), eal=jnp.exp(a_last))


def _head_decay(pre, tril, h):
    seg = pre["a"][:, h:h + 1] - pre["a_t"][h:h + 1, :]
    return jnp.exp(jnp.where(tril, seg, -1e30))


def _ssd_fwd(proj, dtraw, cw, cb, dtb, alog, dexp, nw):
    T = proj.shape[0]
    nc = T // LS
    H = SUBLANE

    def body(z_ref, x_ref, dt_ref, cw_ref, cb_ref, dtb_ref, alog_ref, dexp_ref, nw_ref, ex_ref,
             yb_ref, y_ref, sp_ref, xc_ref, halo, st):
        @pl.when(pl.program_id(0) == 0)
        def _():
            halo[...] = jnp.zeros_like(halo)
            st[...] = jnp.zeros_like(st)

        xb = x_ref[...]
        taps = _rows_down(xb, halo[...], (3, 2, 1)) + [xb.astype(F32)]
        xc = cb_ref[...]
        for k in range(SK):
            xc = xc + cw_ref[k:k + 1, :] * taps[k]
        xc_ref[...] = xc.astype(BF16)
        tril, expand = _ssd_tril(), ex_ref[...]
        pre = _ssd_pre(xc, dt_ref, dtb_ref, alog_ref, tril, expand)
        lane = lax.broadcasted_iota(jnp.int32, (LS, LANE), 1)
        lo = lane < SP
        zf = z_ref[...].astype(F32)
        siluz = zf * _sigmoid(zf)
        for g in range(SG):
            gs = slice(g * SGW, (g + 1) * SGW)
            bg = pre["bm"][:, g * SN:(g + 1) * SN].astype(BF16)
            cg = pre["cm"][:, g * SN:(g + 1) * SN].astype(BF16)
            gmat = _dot(cg, bg, NT)
            sg = st[g]
            sp_ref[0, g] = sg
            yoff = _dot(cg, sg.astype(BF16))
            parts = []
            for j in range(SGW // LANE):
                h0 = g * (SGW // SP) + 2 * j
                m0 = gmat * _head_decay(pre, tril, h0)
                m1 = gmat * _head_decay(pre, tril, h0 + 1)
                xp = pre["xdt"][:, g * SGW + j * LANE:g * SGW + (j + 1) * LANE]
                rhs = jnp.concatenate([jnp.where(lo, xp, 0.0), jnp.where(lo, 0.0, xp)], axis=0).astype(BF16)
                parts.append(_dot(jnp.concatenate([m0, m1], axis=1).astype(BF16), rhs))
            y = (jnp.concatenate(parts, axis=1) + pre["ea"][:, gs] * yoff + dexp_ref[:, gs] * pre["xs"][:, gs])
            st[g] = pre["eal"][:, gs] * sg + _dot(bg, (pre["w"][:, gs] * pre["xdt"][:, gs]).astype(BF16), TN)
            y_ref[:, gs] = y
            yg = y * siluz[:, gs]
            r = lax.rsqrt(jnp.mean(yg * yg, axis=-1, keepdims=True) + EPS)
            yb_ref[:, gs] = (yg * r * nw_ref[:, gs]).astype(BF16)
        halo[...] = xb[LS - 2 * H:].astype(F32)[H:]

    vec = lambda w: pl.BlockSpec((1, w), lambda c: (0, 0))
    return pl.pallas_call(
        body, grid=(nc,),
        in_specs=[pl.BlockSpec((LS, SI), lambda c: (c, 2)), pl.BlockSpec((LS, SXBC), lambda c: (c, 2)),
                  pl.BlockSpec((LS, LANE), lambda c: (c, 0)),
                  pl.BlockSpec((SK, SXBC), lambda c: (0, 0)), vec(SXBC), vec(LANE), vec(LANE), vec(SI), vec(SI),
                  pl.BlockSpec((LANE, SI), lambda c: (0, 0))],
        out_specs=[pl.BlockSpec((LS, SI), lambda c: (c, 0)), pl.BlockSpec((LS, SI), lambda c: (c, 0)),
                   pl.BlockSpec((1, SG, SN, SGW), lambda c: (c, 0, 0, 0)), pl.BlockSpec((LS, SXBC), lambda c: (c, 0))],
        out_shape=[jax.ShapeDtypeStruct((T, SI), BF16), jax.ShapeDtypeStruct((T, SI), F32),
                   jax.ShapeDtypeStruct((nc, SG, SN, SGW), F32), jax.ShapeDtypeStruct((T, SXBC), BF16)],
        scratch_shapes=[pltpu.VMEM((H, SXBC), F32), pltpu.VMEM((SG, SN, SGW), F32)], name="ssd_fwd",
        compiler_params=_params(("arbitrary",), VMEM_CAP_MB))(
            proj, proj, dtraw, cw, cb, dtb, alog, dexp, nw, _head_expansion())


def _ssd_bwd(proj, xcs, dtraw, y, sprev, dyb, dproj, cw, dtb, alog, dexp, nw):
    T = proj.shape[0]
    nc = T // LS
    H = SUBLANE
    NJ = 1

    def body(z_ref, x_ref, xc_ref, dt_ref, y_ref, sp_ref, dyb_ref, dproj_in,
             cw_ref, dtb_ref, alog_ref, dexp_ref, nw_ref, ex_ref,
             dp_ref, ddt_ref, dcw_ref, dcb_ref, ddtb_ref, da_ref, dd_ref, dnw_ref,
             ahead, ds, stage):
        del dproj_in
        i = pl.program_id(0)
        j = pl.program_id(1)

        @pl.when(jnp.logical_and(i == 0, j == 0))
        def _():
            ahead[...] = jnp.zeros_like(ahead)
            ds[...] = jnp.zeros_like(ds)
            for r in (dcw_ref, dcb_ref, ddtb_ref, da_ref, dd_ref, dnw_ref):
                r[...] = jnp.zeros_like(r)

        @pl.when(j == 0)
        def _():
            tril, expand = _ssd_tril(), ex_ref[...]
            pre = _ssd_pre(xc_ref[...].astype(F32), dt_ref, dtb_ref, alog_ref, tril, expand)
            lane = lax.broadcasted_iota(jnp.int32, (LS, LANE), 1)
            sub = lax.broadcasted_iota(jnp.int32, (LANE, LS), 0)
            rowi = lax.broadcasted_iota(jnp.int32, (LS, 1), 0)
            lo = lane < SP
            xs, xdt, ea, w, eal = pre["xs"], pre["xdt"], pre["ea"], pre["w"], pre["eal"]

            zf = z_ref[...].astype(F32)
            sz = _sigmoid(zf)
            siluz = zf * sz
            yv = y_ref[...]
            yg = yv * siluz
            dout = dyb_ref[...].astype(F32)
            dyg_parts, dnw_parts = [], []
            for g in range(SG):
                gs = slice(g * SGW, (g + 1) * SGW)
                ygg = yg[:, gs]
                r = lax.rsqrt(jnp.mean(ygg * ygg, axis=-1, keepdims=True) + EPS)
                yhat = ygg * r
                dn = dout[:, gs] * nw_ref[:, gs]
                dnw_parts.append(jnp.sum(dout[:, gs] * yhat, axis=0, keepdims=True))
                dyg_parts.append(r * (dn - yhat * jnp.mean(dn * yhat, axis=-1, keepdims=True)))
            dyg = jnp.concatenate(dyg_parts, axis=1)
            dnw_ref[...] += jnp.concatenate(dnw_parts, axis=1)
            dy = dyg * siluz
            stage[:, 0:SI] = (dyg * yv * sz * (1.0 + zf * (1.0 - sz))).astype(BF16)
            dd_ref[...] += jnp.sum(dy * xs, axis=0, keepdims=True)
            tt = ea * dy

            da_rows = jnp.zeros((LS, LANE), F32)
            da_cols = jnp.zeros((LANE, LS), F32)
            dxdt_parts, db_parts, dc_parts, daexp_parts = [], [], [], []
            for g in range(SG):
                gs = slice(g * SGW, (g + 1) * SGW)
                bg = pre["bm"][:, g * SN:(g + 1) * SN].astype(BF16)
                cg = pre["cm"][:, g * SN:(g + 1) * SN].astype(BF16)
                sg = sp_ref[0, g]
                sgb = sg.astype(BF16)
                dsg = ds[g]
                dsgb = dsg.astype(BF16)
                ttg = tt[:, gs].astype(BF16)
                yoff = _dot(cg, sgb)
                dc = _dot(ttg, sgb, NT)
                gmat = _dot(cg, bg, NT)
                dgm = jnp.zeros((LS, LS), F32)
                dxdt_pairs = []
                for jj in range(SGW // LANE):
                    h0 = g * (SGW // SP) + 2 * jj
                    ps = slice(g * SGW + jj * LANE, g * SGW + (jj + 1) * LANE)
                    l0 = _head_decay(pre, tril, h0)
                    l1 = _head_decay(pre, tril, h0 + 1)
                    m0 = gmat * l0
                    m1 = gmat * l1
                    dyp = dy[:, ps]
                    dy_lo = jnp.where(lo, dyp, 0.0).astype(BF16)
                    dy_hi = jnp.where(lo, 0.0, dyp).astype(BF16)
                    xpb = xdt[:, ps].astype(BF16)
                    dm0 = _dot(dy_lo, xpb, NT)
                    dm1 = _dot(dy_hi, xpb, NT)
                    q0 = dm0 * m0
                    q1 = dm1 * m1
                    da_rows = da_rows + jnp.where(lane == h0, jnp.sum(q0, axis=1, keepdims=True), 0.0)
                    da_rows = da_rows + jnp.where(lane == h0 + 1, jnp.sum(q1, axis=1, keepdims=True), 0.0)
                    da_cols = da_cols + jnp.where(sub == h0, jnp.sum(q0, axis=0, keepdims=True), 0.0)
                    da_cols = da_cols + jnp.where(sub == h0 + 1, jnp.sum(q1, axis=0, keepdims=True), 0.0)
                    dgm = dgm + dm0 * l0 + dm1 * l1
                    mcat = jnp.concatenate([m0, m1], axis=0).astype(BF16)
                    dycat = jnp.concatenate([dy_lo, dy_hi], axis=0)
                    dxdt_pairs.append(_dot(mcat, dycat, TN))
                dgb = dgm.astype(BF16)
                dc = dc + _dot(dgb, bg)
                db = _dot(dgb, cg, TN)
                zg = _dot(bg, dsgb)
                wg, xdtg = w[:, gs], xdt[:, gs]
                dxdt_g = jnp.concatenate(dxdt_pairs, axis=1) + wg * zg
                qg = zg * xdtg * wg
                last = (jnp.sum(qg, axis=0, keepdims=True)
                        + jnp.sum(dsg * sg, axis=0, keepdims=True) * eal[:, gs])
                daexp_parts.append(dy[:, gs] * ea[:, gs] * yoff - qg + jnp.where(rowi == LS - 1, last, 0.0))
                db = db + _dot((wg * xdtg).astype(BF16), dsgb, NT)
                ds[g] = eal[:, gs] * dsg + _dot(cg, ttg, TN)
                dxdt_parts.append(dxdt_g)
                db_parts.append(db)
                dc_parts.append(dc)
            dxdt = jnp.concatenate(dxdt_parts, axis=1)
            da_exp = jnp.concatenate(daexp_parts, axis=1)
            da = _dot2(da_exp, expand, NT) + da_rows - da_cols.T
            triub = jnp.where(tril, 1.0, 0.0).astype(BF16)
            ddta = _dot3_rhs(triub, da, TN)
            ddt = ddta * pre["a_neg"] + _dot2(dxdt * xs, expand, NT)
            da_ref[...] += jnp.sum(ddta * pre["dt"], axis=0, keepdims=True)
            ddt_raw = ddt * _sigmoid(pre["dtin"])
            ddt_ref[...] = ddt_raw
            ddtb_ref[...] += jnp.sum(ddt_raw, axis=0, keepdims=True)
            dxs = dexp_ref[...] * dy + dxdt * pre["dt_exp"]
            dxbc = jnp.concatenate([dxs] + db_parts + dc_parts, axis=1)
            sx, xc = pre["sx"], pre["xc"]
            dxc = dxbc * sx * (1.0 + xc * (1.0 - sx))
            taps = _rows_up(dxc.astype(BF16), ahead[...], (3, 2, 1)) + [dxc]
            xr = x_ref[...].astype(F32)
            dcb_ref[...] += jnp.sum(dxc, axis=0, keepdims=True)
            dcw_ref[...] += jnp.concatenate([jnp.sum(t * xr, axis=0, keepdims=True) for t in taps], axis=0)
            dxr = cw_ref[0:1, :] * taps[0]
            for k in range(1, SK):
                dxr = dxr + cw_ref[k:k + 1, :] * taps[k]
            stage[:, SI:] = dxr.astype(BF16)
            ahead[...] = dxc[0:H, :]

        dp_ref[...] = stage[...]

    vec = lambda w: pl.BlockSpec((1, w), lambda i, j: (0, 0))
    rev = lambda w, cb_: pl.BlockSpec((LS, w), lambda i, j: (nc - 1 - i, cb_))
    outs = pl.pallas_call(
        body, grid=(nc, NJ),
        in_specs=[rev(SI, 2), rev(SXBC, 2), rev(SXBC, 0),
                  rev(LANE, 0), rev(SI, 0),
                  pl.BlockSpec((1, SG, SN, SGW), lambda i, j: (nc - 1 - i, 0, 0, 0)),
                  rev(SI, 0), pl.BlockSpec(memory_space=pl.ANY),
                  pl.BlockSpec((SK, SXBC), lambda i, j: (0, 0)), vec(LANE), vec(LANE), vec(SI), vec(SI),
                  pl.BlockSpec((LANE, SI), lambda i, j: (0, 0))],
        out_specs=[rev(DP_SSM, 1), rev(LANE, 0),
                   pl.BlockSpec((SK, SXBC), lambda i, j: (0, 0)), vec(SXBC), vec(LANE), vec(LANE), vec(SI), vec(SI)],
        out_shape=[jax.ShapeDtypeStruct(dproj.shape, dproj.dtype), jax.ShapeDtypeStruct((T, LANE), F32),
                   jax.ShapeDtypeStruct((SK, SXBC), F32), jax.ShapeDtypeStruct((1, SXBC), F32),
                   jax.ShapeDtypeStruct((1, LANE), F32), jax.ShapeDtypeStruct((1, LANE), F32),
                   jax.ShapeDtypeStruct((1, SI), F32), jax.ShapeDtypeStruct((1, SI), F32)],
        scratch_shapes=[pltpu.VMEM((H, SXBC), F32),
                        pltpu.VMEM((SG, SN, SGW), F32), pltpu.VMEM((LS, SI + SXBC), BF16)],
        input_output_aliases={7: 0}, name="ssd_bwd",
        compiler_params=_params(("arbitrary", "arbitrary"), VMEM_CAP_MB))(
            proj, proj, xcs, dtraw, y, sprev, dyb, dproj, cw, dtb, alog, dexp, nw, _head_expansion())
    return outs


def _perm_ffn_cols(a):
    lead = a.shape[:-1]
    return a.reshape(lead + (2, 2, FT)).swapaxes(-3, -2).reshape(lead + (2 * DFF,))


def _perm_ffn_rows(a):
    return a.reshape((2, 2, FT) + a.shape[1:]).swapaxes(0, 1).reshape(a.shape)


def _pad_lanes(v, n=LANE):
    return jnp.pad(v, ((0, 0), (0, n - v.shape[-1])))


LATE = ["w_proj_a", "w_proj_b", "w_out", "ffn_w_up_t", "ffn_w_down"]
WGRAD = BF16
SMALL_BF16_FROM = 2 ** 16


class _NoExchange:
    def early_start(self):
        return None

    def early_weights(self, w, outs):
        return w

    def gather_start(self):
        return None

    def gather_pass_on(self, outs):
        return None

    def late_weights(self, w, outs):
        return w

    def late_to_sibling(self, grads):
        return None

    def reduce_late(self, outs):
        return None

    def w_in_to_sibling(self, grad_main, grad_dt):
        return None

    def reduce_w_in(self, outs):
        return None

    def reduced(self, late_outs, w_in_outs):
        pass


def _local_step(x, tgt, w, hooks=None):
    hooks = hooks or _NoExchange()

    def mm(*args, side=None, **kw):
        out = _matmul(*args, side=side, **kw)
        return out if side is not None else (out, [])

    mixw = w["mix_norm_w"][None, :]
    side = hooks.early_start()
    xn, got = _rms_fwd(x, mixw, name="mix_norm", side=side) if side else (_rms_fwd(x, mixw, name="mix_norm"), [])
    w = hooks.early_weights(w, got)
    win_t = w["w_in_t"]
    win_dt = jnp.pad(w["w_in_t"][PMAIN:], ((0, LANE - SH), (0, 0)))
    fcw = _perm_ffn_cols(w["ffn_conv_w"])
    fcb = _perm_ffn_cols(w["ffn_conv_b"][None, :])
    ffnw = w["ffn_norm_w"][None, :]
    finw = w["final_norm_w"][None, :]
    bst = w["gmlp_bs"].T
    scb = w["ssm_conv_b"][None, :]
    dtb = _pad_lanes(w["ssm_dt_bias"][None, :])
    alog = _pad_lanes(w["ssm_a_log"][None, :])
    dexp = jnp.repeat(w["ssm_d"], SP)[None, :]
    snw = w["ssm_norm_w"][None, :]

    proj, got = mm(xn, win_t, name="in_proj", out_dtype=BF16, tb=True, tn=3072, j_outer=True, b_rows=PMAIN,
                   side=hooks.gather_start())
    dtraw = _matmul(xn, win_dt, name="in_proj_dt", out_dtype=F32, tb=True)
    ya_pre, got = _gmlp_fwd(proj, w["gmlp_ln_w"], w["gmlp_ln_b"], w["gmlp_ws"], bst, side=hooks.gather_pass_on(got))
    w = hooks.late_weights(w, got)
    wup = _perm_ffn_rows(w["ffn_w_up_t"])
    yb_pre, y_ssd, sprev, ssm_xc = _ssd_fwd(proj, dtraw, w["ssm_conv_w"], scb, dtb, alog, dexp, snw)
    merged, ya, yb = _merge_fwd(ya_pre, yb_pre, proj, w["gate_bias"], w["w_proj_a"], w["w_proj_b"])
    h1 = _matmul(merged, w["w_out"], name="out_proj", out_dtype=F32, add=x)
    hn = _rms_fwd(h1, ffnw, name="ffn_norm")
    up = _matmul(hn, wup, name="ffn_up", out_dtype=BF16, tb=True, tn=2 * FT, j_outer=True)
    act, ffn_xc = _ffn_act_fwd(up, fcw, fcb)
    h2 = _matmul(act, w["ffn_w_down"], name="ffn_down", out_dtype=F32, tk=DFF, add=h1)

    loss_row, dh2, d_finw = _loss_head(h2, tgt, finw)
    dact = _matmul(dh2, w["ffn_w_down"], name="ffn_down_dx", out_dtype=BF16, tb=True, tn=DFF)
    d_wdown = _matmul(act, dh2, name="ffn_down_dw", out_dtype=WGRAD, ta=True, tm=FT, tk=2048)
    dup, d_fcw, d_fcb = _ffn_act_bwd(up, ffn_xc, dact, fcw)
    dhn = _matmul(dup, wup, name="ffn_up_dx", out_dtype=F32, tk=2 * FT)
    d_wup = _matmul(dup, hn, name="ffn_up_dw", out_dtype=WGRAD, ta=True, tm=FT, tk=2048,
                    o_row=lambda i: (i % 2) * 2 + i // 2)
    dh1, d_ffnw = _rms_bwd(h1, ffnw, dhn, dh2, name="ffn_norm_bwd")
    dmerged = _matmul(dh1, w["w_out"], name="out_proj_dx", out_dtype=BF16, tb=True)
    d_wout = _matmul(merged, dh1, name="out_proj_dw", out_dtype=WGRAD, ta=True, tk=2048)
    dproj, dya, dyb, dya_pre, dyb_pre, d_gbias = _merge_bwd(dmerged, proj, w["gate_bias"], ya, yb,
                                                           w["w_proj_a"], w["w_proj_b"])
    d_wpa = _matmul(ya_pre, dya, name="proj_a_dw", out_dtype=WGRAD, ta=True, tk=2048)
    d_wpb = _matmul(yb_pre, dyb, name="proj_b_dw", out_dtype=WGRAD, ta=True, tk=2048)
    late = {"w_proj_a": d_wpa, "w_proj_b": d_wpb, "w_out": d_wout, "ffn_w_up_t": d_wup,
            "ffn_w_down": d_wdown}
    (dproj, d_lnw, d_lnb, d_ws, d_bst), got = _gmlp_bwd(proj, dya_pre, dproj, w["gmlp_ln_w"], w["gmlp_ln_b"],
                                                        w["gmlp_ws"], bst, side=hooks.late_to_sibling(late))
    dproj, ddt, d_scw, d_scb, d_dtb, d_a, d_dch, d_snw = _ssd_bwd(
        proj, ssm_xc, dtraw, y_ssd, sprev, dyb_pre, dproj, w["ssm_conv_w"], dtb, alog, dexp, snw)
    gap = (2 * D + 2 * GW, DP_GAP)
    d_win_main, late_outs = mm(dproj, xn, name="in_proj_dw", out_dtype=WGRAD, ta=True, a_gap=gap, tk=4096,
                               side=hooks.reduce_late(got))
    d_win_dt = _matmul(ddt, xn, name="in_proj_dt_dw", out_dtype=F32, ta=True)
    d_win_t = jnp.concatenate([d_win_main, d_win_dt[:SH]], axis=0)
    dxn, got = mm(ddt, win_dt, name="in_proj_dt_dx", out_dtype=F32,
                  side=hooks.w_in_to_sibling(d_win_main, d_win_dt[:SH]))
    dxn, w_in_outs = mm(dproj, win_t, name="in_proj_dx", out_dtype=F32, add=dxn, b_rows=PMAIN, a_gap=gap,
                        side=hooks.reduce_w_in(got))
    hooks.reduced(late_outs, w_in_outs)
    grad_x, d_mixw = _rms_bwd(x, mixw, dxn, dh1, name="mix_norm_bwd")

    a_neg = -jnp.exp(w["ssm_a_log"])
    grads = {
        "mix_norm_w": d_mixw[0],
        "w_in_t": d_win_t,
        "gate_bias": d_gbias,
        "gmlp_ln_w": d_lnw, "gmlp_ln_b": d_lnb, "gmlp_ws": d_ws, "gmlp_bs": d_bst[:, :GG].T,
        "ssm_conv_w": d_scw, "ssm_conv_b": d_scb[0],
        "ssm_dt_bias": d_dtb[0, :SH], "ssm_a_log": d_a[0, :SH] * a_neg,
        "ssm_d": d_dch.reshape(SH, SP).sum(axis=-1), "ssm_norm_w": d_snw[0],
        **late,
        "ffn_norm_w": d_ffnw[0],
        "ffn_conv_w": _perm_ffn_cols(d_fcw), "ffn_conv_b": _perm_ffn_cols(d_fcb)[0],
        "ffn_w_down": d_wdown, "final_norm_w": d_finw[0],
    }
    return loss_row, grad_x, grads


MESH = pl.DeviceIdType.MESH
HBM_SPEC = pl.BlockSpec(memory_space=pltpu.HBM)


def _axes():
    return lax.axis_index("x"), lax.axis_index("y"), lax.axis_index("c")


def _run_side(side, *, name):
    n_in, n_out = len(side.inputs), len(side.out_shapes)

    def body(*refs):
        copies = side.make(refs[:n_in], refs[n_in:n_in + n_out], *refs[-2:])
        for cp in copies:
            cp.start()
        for cp in copies:
            cp.wait()

    return pl.pallas_call(
        body, out_shape=list(side.out_shapes), in_specs=[HBM_SPEC] * n_in, out_specs=[HBM_SPEC] * n_out,
        scratch_shapes=[pltpu.SemaphoreType.DMA((side.nsem,)), pltpu.SemaphoreType.DMA((side.nsem,))],
        input_output_aliases={i: j for i, j in side.aliases}, name=name)(*side.inputs)


def _exchange(srcs, plan, *, name):
    na = len(srcs)
    n = len(plan(0, 0, 0))

    def body(*refs):
        src_refs, out_refs = refs[:na], refs[na:2 * na]
        send_sems, recv_sems = refs[2 * na:]
        x, y, c = _axes()
        copies = []
        for k, (slab, peer) in enumerate(plan(x, y, c)):
            for a in range(na):
                cp = pltpu.make_async_remote_copy(
                    src_ref=src_refs[a].at[slab], dst_ref=out_refs[a].at[k], send_sem=send_sems.at[n * a + k],
                    recv_sem=recv_sems.at[n * a + k], device_id=peer, device_id_type=MESH)
                cp.start()
                copies.append(cp)
        for cp in copies:
            cp.wait()

    return pl.pallas_call(
        body, out_shape=[jax.ShapeDtypeStruct((n,) + s.shape[1:], s.dtype) for s in srcs],
        in_specs=[HBM_SPEC] * na, out_specs=[HBM_SPEC] * na,
        scratch_shapes=[pltpu.SemaphoreType.DMA((n * na,)), pltpu.SemaphoreType.DMA((n * na,))], name=name)(*srcs)


def _to_sibling_plan(x, y, c):
    return [(2 * q + (1 - c), (x, y, 1 - c)) for q in range(4)]


def _to_chips_plan(x, y, c):
    q = 2 * x + y
    return [(q ^ 2, (1 - x, y, c)), (q ^ 1, (x, 1 - y, c)), (q ^ 3, (1 - x, 1 - y, c))]


def _row_tile(rows, row_bytes, budget=2 * 2 ** 20, align=2 * SUBLANE):
    if rows * row_bytes <= 2 * budget:
        return rows
    best = None
    for d in range(align, rows + 1, align):
        if rows % d == 0 and d * row_bytes <= budget:
            best = d
    return best or rows


def _pair_add(g, ra, c_idx, *, name):
    _, _, R, C = g.shape
    tr = _row_tile(R, C * 4, budget=3 * 2 ** 20)

    def body(c_ref, g_ref, ra_ref, o_ref):
        del c_ref
        o_ref[...] = (g_ref[0].astype(F32) + ra_ref[...].astype(F32)).astype(o_ref.dtype)

    return pl.pallas_call(
        body,
        grid_spec=pltpu.PrefetchScalarGridSpec(
            num_scalar_prefetch=1, grid=(4, R // tr),
            in_specs=[pl.BlockSpec((1, 1, tr, C), lambda q, r, cr: (q, cr[0], r, 0)),
                      pl.BlockSpec((1, tr, C), lambda q, r, cr: (q, r, 0))],
            out_specs=pl.BlockSpec((1, tr, C), lambda q, r, cr: (q, r, 0))),
        out_shape=jax.ShapeDtypeStruct((4, R, C), g.dtype), name=name,
        compiler_params=_params(("arbitrary", "arbitrary"), 24))(c_idx, g, ra)


def _grad_sum(p, rb, q_idx, *, name):
    _, R, C = p.shape
    tr = _row_tile(R, C * 4, budget=3 * 2 ** 20)

    def body(q_ref, p_ref, rb_ref, o_ref):
        del q_ref
        g = p_ref[0].astype(F32)
        for k in range(3):
            g = g + rb_ref[k].astype(F32)
        o_ref[...] = g

    return pl.pallas_call(
        body,
        grid_spec=pltpu.PrefetchScalarGridSpec(
            num_scalar_prefetch=1, grid=(R // tr,),
            in_specs=[pl.BlockSpec((1, tr, C), lambda r, qr: (qr[0], r, 0)),
                      pl.BlockSpec((3, tr, C), lambda r, qr: (0, r, 0))],
            out_specs=pl.BlockSpec((tr, C), lambda r, qr: (r, 0))),
        out_shape=jax.ShapeDtypeStruct((R, C), F32), name=name,
        compiler_params=_params(("arbitrary",), 40))(q_idx, p, rb)


def _adamw(g, w, m, v):
    m = ADAM_B1 * m + (1.0 - ADAM_B1) * g
    v = ADAM_B2 * v + (1.0 - ADAM_B2) * (g * g)
    m_hat = m / (1.0 - ADAM_B1 ** ADAM_STEP)
    v_hat = v / (1.0 - ADAM_B2 ** ADAM_STEP)
    delta = -ADAM_LR * (m_hat / (jnp.sqrt(v_hat) + ADAM_EPS) + ADAM_WD * w)
    return delta, m, v


def _adam(g, w, m, v, *, name):
    _, R, C = w.shape
    tr = _row_tile(R, C * 4, budget=2 ** 20, align=SUBLANE)

    def body(g_ref, w_ref, m_ref, v_ref, d_out, m_out, v_out):
        delta, mn, vn = _adamw(g_ref[...], w_ref[...], m_ref[...], v_ref[...])
        d_out[...] = delta
        m_out[...] = mn
        v_out[...] = vn

    row = pl.BlockSpec((1, tr, C), lambda r: (0, r, 0))
    o = jax.ShapeDtypeStruct((1, R, C), F32)
    return pl.pallas_call(
        body, grid=(R // tr,), in_specs=[row, row, row, row], out_specs=[row, row, row], out_shape=[o, o, o],
        name=name, compiler_params=_params(("arbitrary",), 32))(g, w, m, v)


def _vmem_specs(n):
    return [pl.BlockSpec(memory_space=pltpu.VMEM)] * n


def _pair_sum_many(mine, theirs, *, name):
    n = len(mine)

    def body(*refs):
        for a in range(n):
            refs[2 * n + a][...] = refs[a][...] + refs[n + a][0]

    return pl.pallas_call(
        body, out_shape=[jax.ShapeDtypeStruct(m.shape, m.dtype) for m in mine], in_specs=_vmem_specs(2 * n),
        out_specs=_vmem_specs(n), name=name)(*mine, *theirs)


def _chip_sum_many(own, recv, q_idx, *, name):
    n = len(own)

    def body(q_ref, *refs):
        q = q_ref[0]
        for a in range(n):
            mine, r = refs[a][...], refs[n + a]
            total = None
            for chip in range(4):
                e = q ^ chip
                term = jnp.where(e == 0, mine, jnp.where(e == 2, r[0], jnp.where(e == 1, r[1], r[2]))).astype(F32)
                total = term if total is None else total + term
            refs[2 * n + a][...] = total

    return pl.pallas_call(
        body, out_shape=[jax.ShapeDtypeStruct(m.shape, F32) for m in own],
        in_specs=[pl.BlockSpec(memory_space=pltpu.SMEM)] + _vmem_specs(2 * n), out_specs=_vmem_specs(n),
        name=name)(q_idx, *own, *recv)


def _adam_many(gs, ws, ms, vs, *, name):
    n = len(gs)

    def body(*refs):
        for a in range(n):
            delta, mn, vn = _adamw(*(refs[k * n + a][...] for k in range(4)))
            refs[4 * n + a][...] = delta
            refs[5 * n + a][...] = mn
            refs[6 * n + a][...] = vn

    shapes = [jax.ShapeDtypeStruct(w.shape, w.dtype) for w in ws]
    out = pl.pallas_call(body, out_shape=shapes * 3, in_specs=_vmem_specs(4 * n), out_specs=_vmem_specs(3 * n),
                         name=name)(*gs, *ws, *ms, *vs)
    return out[:n], out[n:2 * n], out[2 * n:]


WEIGHTS = ["mix_norm_w", "w_in", "gate_bias", "gmlp_ln_w", "gmlp_ln_b", "gmlp_ws", "gmlp_bs", "ssm_conv_w",
           "ssm_conv_b", "ssm_dt_bias", "ssm_a_log", "ssm_d", "ssm_norm_w", "w_proj_a", "w_proj_b", "w_out",
           "ffn_norm_w", "ffn_w_up", "ffn_conv_w", "ffn_conv_b", "ffn_w_down", "final_norm_w"]
SHARDED = {"w_in": ((D, IN_COLS), 1), "gate_bias": ((2, D), 1), "ssm_conv_w": ((SK, SXBC), 1),
           "w_proj_a": ((GW, D), 0), "w_proj_b": ((SI, D), 0), "w_out": ((D, D), 0),
           "ffn_w_up": ((D, 2 * DFF), 1), "ffn_conv_w": ((FK, 2 * DFF), 1), "ffn_w_down": ((DFF, D), 0)}
REPLICATED = {"mix_norm_w": (D,), "gmlp_ln_w": (GG, GD), "gmlp_ln_b": (GG, GD), "gmlp_ws": (GG, GB, GB),
              "gmlp_bs": (GG, GB), "ssm_conv_b": (SXBC,), "ssm_dt_bias": (SH,), "ssm_a_log": (SH,), "ssm_d": (SH,),
              "ssm_norm_w": (SI,), "ffn_norm_w": (D,), "ffn_conv_b": (2 * DFF,), "final_norm_w": (D,)}
REPL_ORDER = [n for n in WEIGHTS if n in REPLICATED]
BTILE = 2 * SUBLANE
WIN_R = IN_COLS // NDEV
WIN_P = WIN_R + BTILE - WIN_R % BTILE
WIN_A = [WIN_R * d // BTILE * BTILE for d in range(NDEV)]
assert all(WIN_A[d] + WIN_P >= WIN_R * (d + 1) for d in range(NDEV)) and WIN_A[-1] + WIN_P == IN_COLS
BIG = [("w_proj_a", GW // NDEV, False), ("w_proj_b", SI // NDEV, False), ("w_out", D // NDEV, False),
       ("ffn_w_up", 2 * DFF // NDEV, True), ("ffn_w_down", DFF // NDEV, False), ("w_in", WIN_P, True)]
VECTORS = ["gate_bias", "ssm_conv_w", "ffn_conv_w"]


def _round_up(n, k):
    return (n + k - 1) // k * k


BIG_OFF = {}
_off = 0
for _n, _r, _t in BIG:
    BIG_OFF[_n] = _off
    _off += _r
BIG_USED = _off
BIG_ROWS = _round_up(BIG_USED, 2 * SUBLANE)
assert all(BIG_OFF[n] % (2 * SUBLANE) == 0 for n, _, _ in BIG)
VEC_SHAPE = {n: (SHARDED[n][0][0], SHARDED[n][0][1] // NDEV) for n in VECTORS}


def _win_offset(dev):
    return WIN_R * dev - WIN_R * dev // BTILE * BTILE


def _pack_big(arrs, dtype, dev):
    parts = []
    for n, r, t in BIG:
        a = (arrs[n].T if t else arrs[n]).astype(dtype)
        if n == "w_in":
            a = lax.dynamic_update_slice(jnp.zeros((WIN_P, D), dtype), a, (_win_offset(dev), 0))
        parts.append(a)
    parts.append(jnp.zeros((BIG_ROWS - BIG_USED, D), dtype))
    return jnp.concatenate(parts, axis=0)


def _join_windows(win):
    parts = []
    for d in range(NDEV):
        lo = BTILE if WIN_A[d] % WIN_R else 0
        if lo:
            parts.append(win[d - 1, WIN_P - BTILE:] + win[d, :BTILE])
        hi = WIN_P - BTILE if d + 1 < NDEV and WIN_A[d + 1] < WIN_A[d] + WIN_P else WIN_P
        parts.append(win[d, lo:hi])
    return jnp.concatenate(parts, axis=0)


def _split_windows(main, last):
    assert WIN_A[-2] + WIN_P <= PMAIN
    wins = [main[a:a + WIN_P] for a in WIN_A[:-1]]
    return jnp.stack(wins + [jnp.concatenate([main[WIN_A[-1]:], last], axis=0)])


LATE_ROWS = BIG_OFF["w_in"]
assert LATE_ROWS + WIN_P == BIG_ROWS and BIG[-1][0] == "w_in"


def _remote(src, dst, send_sems, recv_sems, k, to):
    return pltpu.make_async_remote_copy(src_ref=src, dst_ref=dst, send_sem=send_sems.at[k], recv_sem=recv_sems.at[k],
                                        device_id=to, device_id_type=MESH)


class _Exchange:
    def __init__(self, early_shards, late_shard, c_idx):
        self.early_shards, self.late_shard, self.c_idx = early_shards, late_shard, c_idx

    @staticmethod
    def _gather_side(shards):
        def make(ins, outs, send_sems, recv_sems):
            x, y, c = _axes()
            peers = [(x, y, 1 - c), (1 - x, y, c), (x, 1 - y, c), (1 - x, 1 - y, c)]
            copies = []
            for a, (x_ref, out) in enumerate(zip(ins, outs)):
                mine = out.at[4 * x + 2 * y + c]
                copies += [_remote(x_ref, mine, send_sems, recv_sems, 5 * a + k, p) for k, p in enumerate(peers)]
                copies.append(pltpu.make_async_copy(x_ref, mine, send_sems.at[5 * a + 4]))
            return copies

        return _Side(shards, [jax.ShapeDtypeStruct((NDEV,) + s.shape, s.dtype) for s in shards], 5 * len(shards), make)

    @staticmethod
    def _pass_on_side(bufs):
        def make(ins, outs, send_sems, recv_sems):
            x, y, c = _axes()
            slots = [4 * px + 2 * py + c for px, py in [(1 - x, y), (x, 1 - y), (1 - x, 1 - y)]]
            return [_remote(src.at[s], dst.at[s], send_sems, recv_sems, 3 * a + k, (x, y, 1 - c))
                    for a, (src, dst) in enumerate(zip(ins, outs)) for k, s in enumerate(slots)]

        return _Side(bufs, [jax.ShapeDtypeStruct(b.shape, b.dtype) for b in bufs], 3 * len(bufs), make,
                     aliases=[(a, a) for a in range(len(bufs))])

    def early_start(self):
        return self._gather_side(self.early_shards)

    def early_weights(self, w, outs):
        bufs = _run_side(self._pass_on_side(outs), name="w_in_pass_on")
        w = dict(w)
        w["w_in_t"] = _join_windows(bufs[0])
        for n, a in zip(VECTORS, bufs[1:]):
            r, c = VEC_SHAPE[n]
            w[n] = a[:, 0].transpose(1, 0, 2).reshape(r, NDEV * c)
        return w

    def gather_start(self):
        return self._gather_side([self.late_shard])

    def gather_pass_on(self, outs):
        return self._pass_on_side(outs)

    def late_weights(self, w, outs):
        (buf,) = outs
        w = dict(w)
        for n, r, t in BIG[:-1]:
            w[n + "_t" if t else n] = buf[:, BIG_OFF[n]:BIG_OFF[n] + r].reshape(NDEV * r, D)
        return w

    @staticmethod
    def _plan_side(src, plan):
        n = len(plan(0, 0, 0))

        def make(ins, outs, send_sems, recv_sems):
            (s,), (dst,) = ins, outs
            return [_remote(s.at[slab], dst.at[k], send_sems, recv_sems, k, peer)
                    for k, (slab, peer) in enumerate(plan(*_axes()))]

        return _Side([src], [jax.ShapeDtypeStruct((n,) + src.shape[1:], src.dtype)], n, make)

    def _to_chips(self, send, sib, tag):
        sums = _pair_add(send.reshape((4, 2) + send.shape[1:]), sib, self.c_idx, name=tag + "_grad_pair_add")
        return sums, self._plan_side(sums, _to_chips_plan)

    def late_to_sibling(self, grads):
        send = jnp.concatenate([grads[n + "_t" if t else n].reshape(NDEV, r, D) for n, r, t in BIG[:-1]], axis=1)
        self.late_send = send.astype(BF16)
        return self._plan_side(self.late_send, _to_sibling_plan)

    def reduce_late(self, outs):
        self.late_sum, side = self._to_chips(self.late_send, outs[0], "late")
        return side

    def w_in_to_sibling(self, grad_main, grad_dt):
        self.w_in_send = _split_windows(grad_main.astype(BF16), grad_dt.astype(BF16))
        return self._plan_side(self.w_in_send, _to_sibling_plan)

    def reduce_w_in(self, outs):
        self.w_in_sum, side = self._to_chips(self.w_in_send, outs[0], "w_in")
        return side

    def reduced(self, late_outs, w_in_outs):
        (self.late_from_chips,), (self.w_in_from_chips,) = late_outs, w_in_outs


def kernel(x, mix_norm_w, w_in, gate_bias, gmlp_ln_w, gmlp_ln_b, gmlp_ws, gmlp_bs, ssm_conv_w, ssm_conv_b, ssm_dt_bias, ssm_a_log, ssm_d, ssm_norm_w, w_proj_a, w_proj_b, w_out, ffn_norm_w, ffn_w_up, ffn_conv_w, ffn_conv_b, ffn_w_down, final_norm_w, loss_target, m_mix_norm_w, m_w_in, m_gate_bias, m_gmlp_ln_w, m_gmlp_ln_b, m_gmlp_ws, m_gmlp_bs, m_ssm_conv_w, m_ssm_conv_b, m_ssm_dt_bias, m_ssm_a_log, m_ssm_d, m_ssm_norm_w, m_w_proj_a, m_w_proj_b, m_w_out, m_ffn_norm_w, m_ffn_w_up, m_ffn_conv_w, m_ffn_conv_b, m_ffn_w_down, m_final_norm_w, v_mix_norm_w, v_w_in, v_gate_bias, v_gmlp_ln_w, v_gmlp_ln_b, v_gmlp_ws, v_gmlp_bs, v_ssm_conv_w, v_ssm_conv_b, v_ssm_dt_bias, v_ssm_a_log, v_ssm_d, v_ssm_norm_w, v_w_proj_a, v_w_proj_b, v_w_out, v_ffn_norm_w, v_ffn_w_up, v_ffn_conv_w, v_ffn_conv_b, v_ffn_w_down, v_final_norm_w):
    given = dict(locals())
    wts = {n: given[n] for n in WEIGHTS}
    mom = {n: given["m_" + n] for n in WEIGHTS}
    var = {n: given["v_" + n] for n in WEIGHTS}
    xi, yi, ci = _axes()
    c_idx = jnp.reshape(ci, (1,)).astype(jnp.int32)
    q_idx = jnp.reshape(2 * xi + yi, (1,)).astype(jnp.int32)
    big_names = [n for n, _, _ in BIG]
    drop = lambda d, names: {n: d[n][0] for n in names}

    dev = 4 * xi + 2 * yi + ci
    packed = _pack_big(drop(wts, big_names), BF16, dev)
    full = {n: wts[n].reshape(REPLICATED[n]) for n in REPL_ORDER}

    hooks = _Exchange([packed[LATE_ROWS:]] + [wts[n] for n in VECTORS], packed[:LATE_ROWS], c_idx)
    loss_local, grad_x, grads = _local_step(x[0], loss_target[0], full, hooks)
    g_late = _grad_sum(hooks.late_sum, hooks.late_from_chips, q_idx, name="late_grad_sum")
    g_win = _grad_sum(hooks.w_in_sum, hooks.w_in_from_chips, q_idx, name="w_in_grad_sum")

    small = VECTORS + REPL_ORDER
    as_2d = lambda a: a if a.ndim >= 2 else a[None]
    part = [grads[n].reshape((1,) + SHARDED[n][0] if n in VECTORS else as_2d(wts[n]).shape) for n in small]
    part.append(loss_local)
    from_sibling = _exchange([p[None] for p in part], lambda x, y, c: [(0, (x, y, 1 - c))],
                             name="small_grads_to_sibling")
    chip_sums = _pair_sum_many(part, from_sibling, name="small_grad_pair_sum")
    chip_sums = [s.astype(BF16) if s.size >= SMALL_BF16_FROM else s for s in chip_sums]
    from_chips = _exchange([s[None] for s in chip_sums],
                           lambda x, y, c: [(0, (1 - x, y, c)), (0, (x, 1 - y, c)), (0, (1 - x, 1 - y, c))],
                           name="small_grads_to_chips")
    totals = _chip_sum_many(chip_sums, from_chips, q_idx, name="small_grad_chip_sum")
    g_small, loss = dict(zip(small, totals)), totals[-1][0, 0]
    for n in VECTORS:
        c = VEC_SHAPE[n][1]
        g_small[n] = lax.dynamic_slice_in_dim(g_small[n], dev * c, c, axis=2)

    outs = {}
    small_g = [g_small[n] for n in small]
    small_out = _adam_many(small_g, *[[as_2d(d[n]) for n in small] for d in (wts, mom, var)], name="adam_small")
    for i, n in enumerate(small):
        outs[n] = tuple(a[i].reshape(wts[n].shape) for a in (small_g,) + tuple(small_out))
    for n, r, t in BIG:
        if n == "w_in":
            g = lax.dynamic_slice(g_win, (_win_offset(dev), 0), (WIN_R, D))
        else:
            g = g_late[BIG_OFF[n]:BIG_OFF[n] + r]
        flip = (lambda a: a.transpose(0, 2, 1)) if t else (lambda a: a)
        g = g[None]
        new = _adam(g, flip(wts[n]), flip(mom[n]), flip(var[n]), name="adam_" + n)
        outs[n] = tuple(flip(a) for a in (g,) + tuple(new))
    return (loss, grad_x[None]) + tuple(outs[n][k] for k in range(4) for n in WEIGHTS)
```
